```python
import math
import jax, jax.numpy as jnp
from jax import lax
import numpy as np

D_MODEL = 1024
BATCH = 32
SEQ = 2048
DEPTH = 1

MEM_LEN = 256
MEM_HEADS = 4
MEM_HEAD_DIM = 128
GM_WIDTH = D_MODEL // 2
GM_CHUNK = 128
GM_GROUPS = 4
GM_GROUP_W = GM_WIDTH // GM_GROUPS
MLA_HEADS = 8
MLA_NOPE = 128
MLA_ROPE = 64
MLA_V = 128
Q_LORA = 384
KV_LORA = 256
ROPE_BASE = 10000.0
Q_BLOCK = 128
D_FF = 4 * D_MODEL
N_BRANCH = 3
EPS = 1e-6
W_GM = 2 * GM_WIDTH
W_MLA = Q_LORA + KV_LORA + MLA_ROPE
W_MEMQ = MEM_HEADS * MEM_HEAD_DIM
W_GATE = N_BRANCH * D_MODEL
W_IN_COLS = W_GM + W_MLA + W_MEMQ + W_GATE

kernel_name = "hybrid_gmlp_mla_memory_gated_block"


def rmsnorm(x, g):
    xf = x.astype(jnp.float32)
    y = xf * lax.rsqrt(jnp.mean(xf * xf, axis=-1, keepdims=True) + EPS)
    return (y * g.astype(jnp.float32)).astype(x.dtype)


def layernorm(x, g, b):
    xf = x.astype(jnp.float32)
    mu = jnp.mean(xf, axis=-1, keepdims=True)
    xc = xf - mu
    y = xc * lax.rsqrt(jnp.mean(xc * xc, axis=-1, keepdims=True) + EPS)
    return (y * g.astype(jnp.float32) + b.astype(jnp.float32)).astype(x.dtype)


def rope_tables(positions):
    inv_freq = ROPE_BASE ** (-jnp.arange(0, MLA_ROPE, 2, dtype=jnp.float32) / MLA_ROPE)
    ang = positions.astype(jnp.float32)[..., None] * inv_freq
    return jnp.cos(ang), jnp.sin(ang)


def apply_rope(x, cos, sin):
    x1, x2 = jnp.split(x.astype(jnp.float32), 2, axis=-1)
    return jnp.concatenate([x1 * cos - x2 * sin, x2 * cos + x1 * sin], axis=-1).astype(x.dtype)


def gmlp_branch(z_u, z_v, g_ln, b_ln, w_s, b_s):
    B, S, _ = z_u.shape
    u = jax.nn.gelu(z_u)
    v = layernorm(jax.nn.gelu(z_v), g_ln, b_ln)
    v5 = v.reshape(B, S // GM_CHUNK, GM_CHUNK, GM_GROUPS, GM_GROUP_W)
    w_causal = jnp.tril(w_s).astype(v.dtype)
    mixed = jnp.einsum('gts,bnsgw->bntgw', w_causal, v5) + b_s.T[:, :, None].astype(v.dtype)
    return u * mixed.reshape(B, S, GM_WIDTH)


def mla_branch(c_q, c_kv, k_pe, cos, sin, g_cq, w_uq, g_ckv, w_ukv,
               g_q_nope, g_q_pe, g_k_nope, g_k_pe):
    B, S, _ = c_q.shape
    q = (rmsnorm(c_q, g_cq) @ w_uq).reshape(B, S, MLA_HEADS, MLA_NOPE + MLA_ROPE)
    q_nope, q_pe = q[..., :MLA_NOPE], q[..., MLA_NOPE:]
    kv = (rmsnorm(c_kv, g_ckv) @ w_ukv).reshape(B, S, MLA_HEADS, MLA_NOPE + MLA_V)
    k_nope, v = kv[..., :MLA_NOPE], kv[..., MLA_NOPE:]
    q_nope = rmsnorm(q_nope, g_q_nope)
    k_nope = rmsnorm(k_nope, g_k_nope)
    q_pe = apply_rope(rmsnorm(q_pe, g_q_pe), cos[:, :, None, :], sin[:, :, None, :])
    k_pe = apply_rope(rmsnorm(k_pe, g_k_pe), cos, sin)
    scale = 1.0 / math.sqrt(MLA_NOPE + MLA_ROPE)
    nb = S // Q_BLOCK
    qn_b = q_nope.reshape(B, nb, Q_BLOCK, MLA_HEADS, MLA_NOPE).transpose(1, 0, 2, 3, 4)
    qp_b = q_pe.reshape(B, nb, Q_BLOCK, MLA_HEADS, MLA_ROPE).transpose(1, 0, 2, 3, 4)
    key_pos = jnp.arange(S)

    def block(args):
        qn, qp, i = args
        s = (jnp.einsum('bqhd,bkhd->bhqk', qn, k_nope)
             + jnp.einsum('bqhd,bkd->bhqk', qp, k_pe)).astype(jnp.float32) * scale
        q_pos = i * Q_BLOCK + jnp.arange(Q_BLOCK)
        s = jnp.where((q_pos[:, None] >= key_pos[None, :])[None, None], s, -jnp.inf)
        p = jax.nn.softmax(s, axis=-1).astype(v.dtype)
        return jnp.einsum('bhqk,bkhd->bqhd', p, v)

    out = lax.map(block, (qn_b, qp_b, jnp.arange(nb)))
    return out.transpose(1, 0, 2, 3, 4).reshape(B, S, MLA_HEADS * MLA_V)


def memory_branch(q_m, mem, g_mem, w_mem_kv, g_mq, g_mk):
    B, S, _ = q_m.shape
    M = mem.shape[1]
    q = rmsnorm(q_m.reshape(B, S, MEM_HEADS, MEM_HEAD_DIM), g_mq)
    kv = (rmsnorm(mem, g_mem) @ w_mem_kv).reshape(B, M, 2, MEM_HEADS, MEM_HEAD_DIM)
    k = rmsnorm(kv[:, :, 0], g_mk)
    v = kv[:, :, 1]
    s = jnp.einsum('bshd,bmhd->bhsm', q, k).astype(jnp.float32) / math.sqrt(MEM_HEAD_DIM)
    p = jax.nn.softmax(s, axis=-1).astype(v.dtype)
    return jnp.einsum('bhsm,bmhd->bshd', p, v).reshape(B, S, MEM_HEADS * MEM_HEAD_DIM)


def _fwd_setup_inputs(seed: int = 0) -> dict:
    key = jax.random.key(seed)
    ks = iter(jax.random.split(key, 40))

    def w(shape, fan_in):
        return jax.random.normal(next(ks), (DEPTH,) + shape, jnp.float32) * (fan_in ** -0.5)

    def gain(n):
        return 1.0 + 0.02 * jax.random.normal(next(ks), (DEPTH, n), jnp.float32)

    x = jax.random.normal(next(ks), (BATCH, SEQ, D_MODEL), jnp.float32)
    mem = jax.random.normal(next(ks), (BATCH, MEM_LEN, D_MODEL), jnp.float32)
    offset = jax.random.randint(next(ks), (BATCH, 1), 0, 4096, dtype=jnp.int32)
    positions = (offset + jnp.arange(SEQ, dtype=jnp.int32)[None, :]).astype(jnp.int32)
    return {
        "x": x,
        "mem": mem,
        "positions": positions,
        "g_mix": gain(D_MODEL),
        "w_in": w((D_MODEL, W_IN_COLS), D_MODEL),
        "g_cq": gain(Q_LORA),
        "w_uq": w((Q_LORA, MLA_HEADS * (MLA_NOPE + MLA_ROPE)), Q_LORA),
        "g_ckv": gain(KV_LORA),
        "w_ukv": w((KV_LORA, MLA_HEADS * (MLA_NOPE + MLA_V)), KV_LORA),
        "g_q_nope": gain(MLA_NOPE),
        "g_q_pe": gain(MLA_ROPE),
        "g_k_nope": gain(MLA_NOPE),
        "g_k_pe": gain(MLA_ROPE),
        "g_gm_ln": gain(GM_WIDTH),
        "b_gm_ln": 0.02 * jax.random.normal(next(ks), (DEPTH, GM_WIDTH), jnp.float32),
        "w_spatial": w((GM_GROUPS, GM_CHUNK, GM_CHUNK), GM_CHUNK),
        "b_spatial": 1.0 + 0.02 * jax.random.normal(next(ks), (DEPTH, GM_GROUPS, GM_CHUNK), jnp.float32),
        "g_mem": gain(D_MODEL),
        "w_mem_kv": w((D_MODEL, 2 * MEM_HEADS * MEM_HEAD_DIM), D_MODEL),
        "g_mq": gain(MEM_HEAD_DIM),
        "g_mk": gain(MEM_HEAD_DIM),
        "w_o_gm": w((GM_WIDTH, D_MODEL), GM_WIDTH),
        "w_o_mla": w((MLA_HEADS * MLA_V, D_MODEL), MLA_HEADS * MLA_V),
        "w_o_mem": w((MEM_HEADS * MEM_HEAD_DIM, D_MODEL), MEM_HEADS * MEM_HEAD_DIM),
        "w_out": w((D_MODEL, D_MODEL), D_MODEL),
        "g_ffn": gain(D_MODEL),
        "w_ff1": w((D_MODEL, D_FF), D_MODEL),
        "w_ff2": w((D_FF, D_MODEL), D_FF),
    }


def _fwd_reference(x, mem, positions, g_mix, w_in, g_cq, w_uq, g_ckv, w_ukv,
              g_q_nope, g_q_pe, g_k_nope, g_k_pe, g_gm_ln, b_gm_ln, w_spatial, b_spatial,
              g_mem, w_mem_kv, g_mq, g_mk, w_o_gm, w_o_mla, w_o_mem, w_out,
              g_ffn, w_ff1, w_ff2):
    cos, sin = rope_tables(positions)
    split_at = [GM_WIDTH, W_GM, W_GM + Q_LORA, W_GM + Q_LORA + KV_LORA,
                W_GM + W_MLA, W_GM + W_MLA + W_MEMQ]
    for l in range(DEPTH):
        h = rmsnorm(x, g_mix[l])
        z = h @ w_in[l]
        z_u, z_v, c_q, c_kv, k_pe, q_m, z_g = jnp.split(z, split_at, axis=-1)
        y_gm = gmlp_branch(z_u, z_v, g_gm_ln[l], b_gm_ln[l], w_spatial[l], b_spatial[l]) @ w_o_gm[l]
        y_mla = mla_branch(c_q, c_kv, k_pe, cos, sin, g_cq[l], w_uq[l], g_ckv[l], w_ukv[l],
                           g_q_nope[l], g_q_pe[l], g_k_nope[l], g_k_pe[l]) @ w_o_mla[l]
        y_mem = memory_branch(q_m, mem, g_mem[l], w_mem_kv[l], g_mq[l], g_mk[l]) @ w_o_mem[l]
        gates = jax.nn.sigmoid(z_g).reshape(z_g.shape[:-1] + (N_BRANCH, D_MODEL))
        merged = gates[..., 0, :] * y_gm + gates[..., 1, :] * y_mla + gates[..., 2, :] * y_mem
        x = x + merged @ w_out[l]
        h2 = rmsnorm(x, g_ffn[l])
        x = x + jnp.square(jax.nn.relu(h2 @ w_ff1[l])) @ w_ff2[l]
    return x


import jax as _jax
import jax.numpy as _jnp

TWIN_FORMAT = 'train_step'
FWD_PARAMS = ['x', 'mem', 'positions', 'g_mix', 'w_in', 'g_cq', 'w_uq', 'g_ckv', 'w_ukv', 'g_q_nope', 'g_q_pe', 'g_k_nope', 'g_k_pe', 'g_gm_ln', 'b_gm_ln', 'w_spatial', 'b_spatial', 'g_mem', 'w_mem_kv', 'g_mq', 'g_mk', 'w_o_gm', 'w_o_mla', 'w_o_mem', 'w_out', 'g_ffn', 'w_ff1', 'w_ff2']
TWIN_WEIGHTS = ['g_mix', 'w_in', 'g_cq', 'w_uq', 'g_ckv', 'w_ukv', 'g_q_nope', 'g_q_pe', 'g_k_nope', 'g_k_pe', 'g_gm_ln', 'b_gm_ln', 'w_spatial', 'b_spatial', 'g_mem', 'w_mem_kv', 'g_mq', 'g_mk', 'w_o_gm', 'w_o_mla', 'w_o_mem', 'w_out', 'g_ffn', 'w_ff1', 'w_ff2']
TWIN_DIFF_INPUT = 'x'
TWIN_INPUTS = ['x', 'mem', 'positions', 'g_mix', 'w_in', 'g_cq', 'w_uq', 'g_ckv', 'w_ukv', 'g_q_nope', 'g_q_pe', 'g_k_nope', 'g_k_pe', 'g_gm_ln', 'b_gm_ln', 'w_spatial', 'b_spatial', 'g_mem', 'w_mem_kv', 'g_mq', 'g_mk', 'w_o_gm', 'w_o_mla', 'w_o_mem', 'w_out', 'g_ffn', 'w_ff1', 'w_ff2', 'loss_target', 'm_g_mix', 'm_w_in', 'm_g_cq', 'm_w_uq', 'm_g_ckv', 'm_w_ukv', 'm_g_q_nope', 'm_g_q_pe', 'm_g_k_nope', 'm_g_k_pe', 'm_g_gm_ln', 'm_b_gm_ln', 'm_w_spatial', 'm_b_spatial', 'm_g_mem', 'm_w_mem_kv', 'm_g_mq', 'm_g_mk', 'm_w_o_gm', 'm_w_o_mla', 'm_w_o_mem', 'm_w_out', 'm_g_ffn', 'm_w_ff1', 'm_w_ff2', 'v_g_mix', 'v_w_in', 'v_g_cq', 'v_w_uq', 'v_g_ckv', 'v_w_ukv', 'v_g_q_nope', 'v_g_q_pe', 'v_g_k_nope', 'v_g_k_pe', 'v_g_gm_ln', 'v_b_gm_ln', 'v_w_spatial', 'v_b_spatial', 'v_g_mem', 'v_w_mem_kv', 'v_g_mq', 'v_g_mk', 'v_w_o_gm', 'v_w_o_mla', 'v_w_o_mem', 'v_w_out', 'v_g_ffn', 'v_w_ff1', 'v_w_ff2']
TWIN_OUTPUTS = ['loss', 'grad_x', 'grad_g_mix', 'grad_w_in', 'grad_g_cq', 'grad_w_uq', 'grad_g_ckv', 'grad_w_ukv', 'grad_g_q_nope', 'grad_g_q_pe', 'grad_g_k_nope', 'grad_g_k_pe', 'grad_g_gm_ln', 'grad_b_gm_ln', 'grad_w_spatial', 'grad_b_spatial', 'grad_g_mem', 'grad_w_mem_kv', 'grad_g_mq', 'grad_g_mk', 'grad_w_o_gm', 'grad_w_o_mla', 'grad_w_o_mem', 'grad_w_out', 'grad_g_ffn', 'grad_w_ff1', 'grad_w_ff2', 'delta_g_mix', 'delta_w_in', 'delta_g_cq', 'delta_w_uq', 'delta_g_ckv', 'delta_w_ukv', 'delta_g_q_nope', 'delta_g_q_pe', 'delta_g_k_nope', 'delta_g_k_pe', 'delta_g_gm_ln', 'delta_b_gm_ln', 'delta_w_spatial', 'delta_b_spatial', 'delta_g_mem', 'delta_w_mem_kv', 'delta_g_mq', 'delta_g_mk', 'delta_w_o_gm', 'delta_w_o_mla', 'delta_w_o_mem', 'delta_w_out', 'delta_g_ffn', 'delta_w_ff1', 'delta_w_ff2', 'new_m_g_mix', 'new_m_w_in', 'new_m_g_cq', 'new_m_w_uq', 'new_m_g_ckv', 'new_m_w_ukv', 'new_m_g_q_nope', 'new_m_g_q_pe', 'new_m_g_k_nope', 'new_m_g_k_pe', 'new_m_g_gm_ln', 'new_m_b_gm_ln', 'new_m_w_spatial', 'new_m_b_spatial', 'new_m_g_mem', 'new_m_w_mem_kv', 'new_m_g_mq', 'new_m_g_mk', 'new_m_w_o_gm', 'new_m_w_o_mla', 'new_m_w_o_mem', 'new_m_w_out', 'new_m_g_ffn', 'new_m_w_ff1', 'new_m_w_ff2', 'new_v_g_mix', 'new_v_w_in', 'new_v_g_cq', 'new_v_w_uq', 'new_v_g_ckv', 'new_v_w_ukv', 'new_v_g_q_nope', 'new_v_g_q_pe', 'new_v_g_k_nope', 'new_v_g_k_pe', 'new_v_g_gm_ln', 'new_v_b_gm_ln', 'new_v_w_spatial', 'new_v_b_spatial', 'new_v_g_mem', 'new_v_w_mem_kv', 'new_v_g_mq', 'new_v_g_mk', 'new_v_w_o_gm', 'new_v_w_o_mla', 'new_v_w_o_mem', 'new_v_w_out', 'new_v_g_ffn', 'new_v_w_ff1', 'new_v_w_ff2']
TWIN_LEAF_KINDS = {'loss': 'loss', 'grad_x': 'grad_x', 'grad_g_mix': 'grad_w', 'grad_w_in': 'grad_w', 'grad_g_cq': 'grad_w', 'grad_w_uq': 'grad_w', 'grad_g_ckv': 'grad_w', 'grad_w_ukv': 'grad_w', 'grad_g_q_nope': 'grad_w', 'grad_g_q_pe': 'grad_w', 'grad_g_k_nope': 'grad_w', 'grad_g_k_pe': 'grad_w', 'grad_g_gm_ln': 'grad_w', 'grad_b_gm_ln': 'grad_w', 'grad_w_spatial': 'grad_w', 'grad_b_spatial': 'grad_w', 'grad_g_mem': 'grad_w', 'grad_w_mem_kv': 'grad_w', 'grad_g_mq': 'grad_w', 'grad_g_mk': 'grad_w', 'grad_w_o_gm': 'grad_w', 'grad_w_o_mla': 'grad_w', 'grad_w_o_mem': 'grad_w', 'grad_w_out': 'grad_w', 'grad_g_ffn': 'grad_w', 'grad_w_ff1': 'grad_w', 'grad_w_ff2': 'grad_w', 'delta_g_mix': 'delta_w', 'delta_w_in': 'delta_w', 'delta_g_cq': 'delta_w', 'delta_w_uq': 'delta_w', 'delta_g_ckv': 'delta_w', 'delta_w_ukv': 'delta_w', 'delta_g_q_nope': 'delta_w', 'delta_g_q_pe': 'delta_w', 'delta_g_k_nope': 'delta_w', 'delta_g_k_pe': 'delta_w', 'delta_g_gm_ln': 'delta_w', 'delta_b_gm_ln': 'delta_w', 'delta_w_spatial': 'delta_w', 'delta_b_spatial': 'delta_w', 'delta_g_mem': 'delta_w', 'delta_w_mem_kv': 'delta_w', 'delta_g_mq': 'delta_w', 'delta_g_mk': 'delta_w', 'delta_w_o_gm': 'delta_w', 'delta_w_o_mla': 'delta_w', 'delta_w_o_mem': 'delta_w', 'delta_w_out': 'delta_w', 'delta_g_ffn': 'delta_w', 'delta_w_ff1': 'delta_w', 'delta_w_ff2': 'delta_w', 'new_m_g_mix': 'new_m', 'new_m_w_in': 'new_m', 'new_m_g_cq': 'new_m', 'new_m_w_uq': 'new_m', 'new_m_g_ckv': 'new_m', 'new_m_w_ukv': 'new_m', 'new_m_g_q_nope': 'new_m', 'new_m_g_q_pe': 'new_m', 'new_m_g_k_nope': 'new_m', 'new_m_g_k_pe': 'new_m', 'new_m_g_gm_ln': 'new_m', 'new_m_b_gm_ln': 'new_m', 'new_m_w_spatial': 'new_m', 'new_m_b_spatial': 'new_m', 'new_m_g_mem': 'new_m', 'new_m_w_mem_kv': 'new_m', 'new_m_g_mq': 'new_m', 'new_m_g_mk': 'new_m', 'new_m_w_o_gm': 'new_m', 'new_m_w_o_mla': 'new_m', 'new_m_w_o_mem': 'new_m', 'new_m_w_out': 'new_m', 'new_m_g_ffn': 'new_m', 'new_m_w_ff1': 'new_m', 'new_m_w_ff2': 'new_m', 'new_v_g_mix': 'new_v', 'new_v_w_in': 'new_v', 'new_v_g_cq': 'new_v', 'new_v_w_uq': 'new_v', 'new_v_g_ckv': 'new_v', 'new_v_w_ukv': 'new_v', 'new_v_g_q_nope': 'new_v', 'new_v_g_q_pe': 'new_v', 'new_v_g_k_nope': 'new_v', 'new_v_g_k_pe': 'new_v', 'new_v_g_gm_ln': 'new_v', 'new_v_b_gm_ln': 'new_v', 'new_v_w_spatial': 'new_v', 'new_v_b_spatial': 'new_v', 'new_v_g_mem': 'new_v', 'new_v_w_mem_kv': 'new_v', 'new_v_g_mq': 'new_v', 'new_v_g_mk': 'new_v', 'new_v_w_o_gm': 'new_v', 'new_v_w_o_mla': 'new_v', 'new_v_w_o_mem': 'new_v', 'new_v_w_out': 'new_v', 'new_v_g_ffn': 'new_v', 'new_v_w_ff1': 'new_v', 'new_v_w_ff2': 'new_v'}


def _forward(args):
    return _fwd_reference(*[args[k] for k in FWD_PARAMS])


def _output_shape():
    out = _jax.eval_shape(lambda: _forward(_fwd_setup_inputs(0)))
    return out.shape, out.dtype

N_MICROBATCH = 1
ADAM_LR = 0.001
ADAM_B1 = 0.9
ADAM_B2 = 0.999
ADAM_EPS = 1e-08
ADAM_WD = 0.01
ADAM_STEP = 10
PER_EXAMPLE_BATCH_AXIS = {'x': 0, 'mem': 0, 'positions': 0, 'loss_target': 0}
SHARED_INPUTS = []
_WEIGHT_DTYPES = {'g_mix': _jnp.float32, 'w_in': _jnp.float32, 'g_cq': _jnp.float32, 'w_uq': _jnp.float32, 'g_ckv': _jnp.float32, 'w_ukv': _jnp.float32, 'g_q_nope': _jnp.float32, 'g_q_pe': _jnp.float32, 'g_k_nope': _jnp.float32, 'g_k_pe': _jnp.float32, 'g_gm_ln': _jnp.float32, 'b_gm_ln': _jnp.float32, 'w_spatial': _jnp.float32, 'b_spatial': _jnp.float32, 'g_mem': _jnp.float32, 'w_mem_kv': _jnp.float32, 'g_mq': _jnp.float32, 'g_mk': _jnp.float32, 'w_o_gm': _jnp.float32, 'w_o_mla': _jnp.float32, 'w_o_mem': _jnp.float32, 'w_out': _jnp.float32, 'g_ffn': _jnp.float32, 'w_ff1': _jnp.float32, 'w_ff2': _jnp.float32}
MOMENT_SCALE = {'g_mix': 1.413491e+01, 'w_in': 3.652366e-01, 'g_cq': 1.638864e-01, 'w_uq': 8.395524e-02, 'g_ckv': 6.755940e-01, 'w_ukv': 1.755876e-01, 'g_q_nope': 4.486701e-01, 'g_q_pe': 4.051457e-01, 'g_k_nope': 4.489645e-01, 'g_k_pe': 3.967340e-01, 'g_gm_ln': 7.216708e+00, 'b_gm_ln': 4.789098e-01, 'w_spatial': 5.922280e-01, 'b_spatial': 1.572128e+01, 'g_mem': 7.344053e-01, 'w_mem_kv': 7.598950e-01, 'g_mq': 9.464474e-01, 'g_mk': 9.406329e-01, 'w_o_gm': 8.901972e+00, 'w_o_mla': 2.379014e-01, 'w_o_mem': 8.476147e-01, 'w_out': 8.522028e+00, 'g_ffn': 1.919605e+02, 'w_ff1': 3.258280e+00, 'w_ff2': 1.692794e+01}


def _to_microbatches(a, axis):
    t = _jnp.moveaxis(a, axis, 0)
    t = t.reshape((N_MICROBATCH, t.shape[0] // N_MICROBATCH) + t.shape[1:])
    return _jnp.moveaxis(t, 1, axis + 1)


def setup_inputs(seed: int = 0) -> dict:
    inp = _fwd_setup_inputs(seed)
    key = _jax.random.fold_in(_jax.random.key(seed), 7919)
    shape, _ = _output_shape()
    out = dict(inp)
    out["loss_target"] = _jax.random.normal(_jax.random.fold_in(key, 0), shape, _jnp.float32)
    for i, name in enumerate(TWIN_WEIGHTS):
        w = inp[name].astype(_jnp.float32)
        if MOMENT_SCALE is None:
            s = _jnp.sqrt(_jnp.mean(_jnp.square(w)) + 1e-30)
        else:
            s = MOMENT_SCALE[name]
        km, kv = _jax.random.split(_jax.random.fold_in(key, i + 1))
        out[name] = w
        out["m_" + name] = s * _jax.random.normal(km, w.shape, _jnp.float32)
        out["v_" + name] = (s * s) * _jax.random.uniform(kv, w.shape, _jnp.float32, 0.5, 1.5)
    if N_MICROBATCH > 1:
        for name, axis in PER_EXAMPLE_BATCH_AXIS.items():
            out[name] = _to_microbatches(out[name], axis)
    return {'x': out['x'], 'mem': out['mem'], 'positions': out['positions'], 'g_mix': out['g_mix'], 'w_in': out['w_in'], 'g_cq': out['g_cq'], 'w_uq': out['w_uq'], 'g_ckv': out['g_ckv'], 'w_ukv': out['w_ukv'], 'g_q_nope': out['g_q_nope'], 'g_q_pe': out['g_q_pe'], 'g_k_nope': out['g_k_nope'], 'g_k_pe': out['g_k_pe'], 'g_gm_ln': out['g_gm_ln'], 'b_gm_ln': out['b_gm_ln'], 'w_spatial': out['w_spatial'], 'b_spatial': out['b_spatial'], 'g_mem': out['g_mem'], 'w_mem_kv': out['w_mem_kv'], 'g_mq': out['g_mq'], 'g_mk': out['g_mk'], 'w_o_gm': out['w_o_gm'], 'w_o_mla': out['w_o_mla'], 'w_o_mem': out['w_o_mem'], 'w_out': out['w_out'], 'g_ffn': out['g_ffn'], 'w_ff1': out['w_ff1'], 'w_ff2': out['w_ff2'], 'loss_target': out['loss_target'], 'm_g_mix': out['m_g_mix'], 'm_w_in': out['m_w_in'], 'm_g_cq': out['m_g_cq'], 'm_w_uq': out['m_w_uq'], 'm_g_ckv': out['m_g_ckv'], 'm_w_ukv': out['m_w_ukv'], 'm_g_q_nope': out['m_g_q_nope'], 'm_g_q_pe': out['m_g_q_pe'], 'm_g_k_nope': out['m_g_k_nope'], 'm_g_k_pe': out['m_g_k_pe'], 'm_g_gm_ln': out['m_g_gm_ln'], 'm_b_gm_ln': out['m_b_gm_ln'], 'm_w_spatial': out['m_w_spatial'], 'm_b_spatial': out['m_b_spatial'], 'm_g_mem': out['m_g_mem'], 'm_w_mem_kv': out['m_w_mem_kv'], 'm_g_mq': out['m_g_mq'], 'm_g_mk': out['m_g_mk'], 'm_w_o_gm': out['m_w_o_gm'], 'm_w_o_mla': out['m_w_o_mla'], 'm_w_o_mem': out['m_w_o_mem'], 'm_w_out': out['m_w_out'], 'm_g_ffn': out['m_g_ffn'], 'm_w_ff1': out['m_w_ff1'], 'm_w_ff2': out['m_w_ff2'], 'v_g_mix': out['v_g_mix'], 'v_w_in': out['v_w_in'], 'v_g_cq': out['v_g_cq'], 'v_w_uq': out['v_w_uq'], 'v_g_ckv': out['v_g_ckv'], 'v_w_ukv': out['v_w_ukv'], 'v_g_q_nope': out['v_g_q_nope'], 'v_g_q_pe': out['v_g_q_pe'], 'v_g_k_nope': out['v_g_k_nope'], 'v_g_k_pe': out['v_g_k_pe'], 'v_g_gm_ln': out['v_g_gm_ln'], 'v_b_gm_ln': out['v_b_gm_ln'], 'v_w_spatial': out['v_w_spatial'], 'v_b_spatial': out['v_b_spatial'], 'v_g_mem': out['v_g_mem'], 'v_w_mem_kv': out['v_w_mem_kv'], 'v_g_mq': out['v_g_mq'], 'v_g_mk': out['v_g_mk'], 'v_w_o_gm': out['v_w_o_gm'], 'v_w_o_mla': out['v_w_o_mla'], 'v_w_o_mem': out['v_w_o_mem'], 'v_w_out': out['v_w_out'], 'v_g_ffn': out['v_g_ffn'], 'v_w_ff1': out['v_w_ff1'], 'v_w_ff2': out['v_w_ff2']}


def _loss(weights, diff, rest, loss_target):
    with _jax.named_scope("forward"):
        args = {**rest, TWIN_DIFF_INPUT: diff, **{k: w.astype(_WEIGHT_DTYPES[k]) for k, w in weights.items()}}
        y = _forward(args)
    with _jax.named_scope("loss_head"):
        err = _jnp.square(y.astype(_jnp.float32) - loss_target)
        return 0.5 * _jnp.sum(_jnp.mean(err, axis=-1)) if err.ndim else 0.5 * err


def _adamw(w, g, m, v):
    m = ADAM_B1 * m + (1.0 - ADAM_B1) * g
    v = ADAM_B2 * v + (1.0 - ADAM_B2) * _jnp.square(g)
    m_hat = m / (1.0 - ADAM_B1 ** ADAM_STEP)
    v_hat = v / (1.0 - ADAM_B2 ** ADAM_STEP)
    delta = -ADAM_LR * (m_hat / (_jnp.sqrt(v_hat) + ADAM_EPS) + ADAM_WD * w)
    return delta, m, v


def reference(x, mem, positions, g_mix, w_in, g_cq, w_uq, g_ckv, w_ukv, g_q_nope, g_q_pe, g_k_nope, g_k_pe, g_gm_ln, b_gm_ln, w_spatial, b_spatial, g_mem, w_mem_kv, g_mq, g_mk, w_o_gm, w_o_mla, w_o_mem, w_out, g_ffn, w_ff1, w_ff2, loss_target, m_g_mix, m_w_in, m_g_cq, m_w_uq, m_g_ckv, m_w_ukv, m_g_q_nope, m_g_q_pe, m_g_k_nope, m_g_k_pe, m_g_gm_ln, m_b_gm_ln, m_w_spatial, m_b_spatial, m_g_mem, m_w_mem_kv, m_g_mq, m_g_mk, m_w_o_gm, m_w_o_mla, m_w_o_mem, m_w_out, m_g_ffn, m_w_ff1, m_w_ff2, v_g_mix, v_w_in, v_g_cq, v_w_uq, v_g_ckv, v_w_ukv, v_g_q_nope, v_g_q_pe, v_g_k_nope, v_g_k_pe, v_g_gm_ln, v_b_gm_ln, v_w_spatial, v_b_spatial, v_g_mem, v_w_mem_kv, v_g_mq, v_g_mk, v_w_o_gm, v_w_o_mla, v_w_o_mem, v_w_out, v_g_ffn, v_w_ff1, v_w_ff2):
    given = dict(x=x, mem=mem, positions=positions, g_mix=g_mix, w_in=w_in, g_cq=g_cq, w_uq=w_uq, g_ckv=g_ckv, w_ukv=w_ukv, g_q_nope=g_q_nope, g_q_pe=g_q_pe, g_k_nope=g_k_nope, g_k_pe=g_k_pe, g_gm_ln=g_gm_ln, b_gm_ln=b_gm_ln, w_spatial=w_spatial, b_spatial=b_spatial, g_mem=g_mem, w_mem_kv=w_mem_kv, g_mq=g_mq, g_mk=g_mk, w_o_gm=w_o_gm, w_o_mla=w_o_mla, w_o_mem=w_o_mem, w_out=w_out, g_ffn=g_ffn, w_ff1=w_ff1, w_ff2=w_ff2, loss_target=loss_target, m_g_mix=m_g_mix, m_w_in=m_w_in, m_g_cq=m_g_cq, m_w_uq=m_w_uq, m_g_ckv=m_g_ckv, m_w_ukv=m_w_ukv, m_g_q_nope=m_g_q_nope, m_g_q_pe=m_g_q_pe, m_g_k_nope=m_g_k_nope, m_g_k_pe=m_g_k_pe, m_g_gm_ln=m_g_gm_ln, m_b_gm_ln=m_b_gm_ln, m_w_spatial=m_w_spatial, m_b_spatial=m_b_spatial, m_g_mem=m_g_mem, m_w_mem_kv=m_w_mem_kv, m_g_mq=m_g_mq, m_g_mk=m_g_mk, m_w_o_gm=m_w_o_gm, m_w_o_mla=m_w_o_mla, m_w_o_mem=m_w_o_mem, m_w_out=m_w_out, m_g_ffn=m_g_ffn, m_w_ff1=m_w_ff1, m_w_ff2=m_w_ff2, v_g_mix=v_g_mix, v_w_in=v_w_in, v_g_cq=v_g_cq, v_w_uq=v_w_uq, v_g_ckv=v_g_ckv, v_w_ukv=v_w_ukv, v_g_q_nope=v_g_q_nope, v_g_q_pe=v_g_q_pe, v_g_k_nope=v_g_k_nope, v_g_k_pe=v_g_k_pe, v_g_gm_ln=v_g_gm_ln, v_b_gm_ln=v_b_gm_ln, v_w_spatial=v_w_spatial, v_b_spatial=v_b_spatial, v_g_mem=v_g_mem, v_w_mem_kv=v_w_mem_kv, v_g_mq=v_g_mq, v_g_mk=v_g_mk, v_w_o_gm=v_w_o_gm, v_w_o_mla=v_w_o_mla, v_w_o_mem=v_w_o_mem, v_w_out=v_w_out, v_g_ffn=v_g_ffn, v_w_ff1=v_w_ff1, v_w_ff2=v_w_ff2)
    weights = {n: given[n] for n in TWIN_WEIGHTS}
    shared = {n: given[n] for n in SHARED_INPUTS}
    per_example = {n: given[n] for n in ['x', 'mem', 'positions']}
    grad_fn = _jax.value_and_grad(_loss, argnums=(0, 1))

    def one_microbatch(ex, loss_target):
        ex = dict(ex)
        diff = ex.pop(TWIN_DIFF_INPUT)
        return grad_fn(weights, diff, {**shared, **ex}, loss_target)

    if N_MICROBATCH == 1:
        loss, (grad_w, grad_x) = one_microbatch(per_example, given["loss_target"])
    else:
        def body(carry, xs):
            loss_sum, grad_sum = carry
            l_k, (gw_k, gx_k) = one_microbatch(xs[0], xs[1])
            with _jax.named_scope("update"):
                return (loss_sum + l_k, _jax.tree.map(_jnp.add, grad_sum, gw_k)), gx_k

        init = (_jnp.zeros((), _jnp.float32), _jax.tree.map(_jnp.zeros_like, weights))
        (loss, grad_w), grad_x = _jax.lax.scan(body, init, (per_example, given["loss_target"]))
    with _jax.named_scope("update"):
        delta_w, new_m, new_v = {}, {}, {}
        for n in TWIN_WEIGHTS:
            delta_w[n], new_m[n], new_v[n] = _adamw(weights[n], grad_w[n], given["m_" + n], given["v_" + n])
    return (loss, grad_x, *[grad_w[n] for n in TWIN_WEIGHTS], *[delta_w[n] for n in TWIN_WEIGHTS],
            *[new_m[n] for n in TWIN_WEIGHTS], *[new_v[n] for n in TWIN_WEIGHTS])
```

```python
import functools
import math

import numpy as np
import jax
import jax.numpy as jnp
from jax import lax
from jax.experimental import pallas as pl
from jax.experimental.pallas import tpu as pltpu

F32 = jnp.float32
BF = jnp.bfloat16
SDS = jax.ShapeDtypeStruct
MESH = pl.DeviceIdType.MESH

D_MODEL = 1024
MEM_LEN = 256
MEM_HEADS = 4
HEAD_DIM = 128
GM_WIDTH = 512
GM_CHUNK = 128
GM_GROUPS = 4
MLA_HEADS = 8
MLA_NOPE = 128
MLA_ROPE = 64
MLA_V = 128
Q_LORA = 384
KV_LORA = 256
ROPE_BASE = 10000.0
D_FF = 4096
EPS = 1e-6
W_IN_COLS = 5312
ADAM_LR, ADAM_B1, ADAM_B2, ADAM_EPS, ADAM_WD, ADAM_STEP = 0.001, 0.9, 0.999, 1e-08, 0.01, 10

ZG, ZU, ZV, QM, CQ, KPE, CKV = 0, 3072, 3584, 4096, 4608, 4992, 5120
Z_COLS = 5376
LANES = 128
ROW_TILE = 256
ATT_TILE = 512
VMEM_LIMIT = 56 * 1024 * 1024

N_CHIPS = 4
SHARD_ROWS = 4672
HALF_ROWS = SHARD_ROWS // 2
FLAT_TILE = 584
SMALL_ROWS = 560

BIG = ["w_in", "w_uq", "w_ukv", "w_mem_kv", "w_o_gm", "w_o_mla", "w_o_mem", "w_out", "w_ff1", "w_ff2"]
BIG_SHAPE = {"w_in": (1024, 5312), "w_uq": (384, 1536), "w_ukv": (256, 2048), "w_mem_kv": (1024, 1024),
             "w_o_gm": (512, 1024), "w_o_mla": (1024, 1024), "w_o_mem": (512, 1024), "w_out": (1024, 1024),
             "w_ff1": (1024, 4096), "w_ff2": (4096, 1024)}
COL_SHARDED = {"w_in", "w_uq", "w_ukv", "w_o_gm", "w_o_mem", "w_ff1"}
SMALL = ["g_mix", "g_cq", "g_ckv", "g_q_nope", "g_q_pe", "g_k_nope", "g_k_pe", "g_gm_ln", "b_gm_ln",
         "w_spatial", "b_spatial", "g_mem", "g_mq", "g_mk", "g_ffn"]
WEIGHTS = ['g_mix', 'w_in', 'g_cq', 'w_uq', 'g_ckv', 'w_ukv', 'g_q_nope', 'g_q_pe', 'g_k_nope', 'g_k_pe',
           'g_gm_ln', 'b_gm_ln', 'w_spatial', 'b_spatial', 'g_mem', 'w_mem_kv', 'g_mq', 'g_mk', 'w_o_gm',
           'w_o_mla', 'w_o_mem', 'w_out', 'g_ffn', 'w_ff1', 'w_ff2']


def _params(sem=None):
    return pltpu.CompilerParams(vmem_limit_bytes=VMEM_LIMIT, dimension_semantics=sem)


def _pick(n, prefs):
    for p in prefs:
        if n % p == 0:
            return p
    return n


def _full(shape):
    nd = len(shape)
    return pl.BlockSpec(shape, lambda *_: (0,) * nd)


def _rows(t, w, blk=0):
    return pl.BlockSpec((t, w), lambda i: (i, blk))


def _acc(ref, val, first):
    @pl.when(first)
    def _():
        ref[...] = val

    @pl.when(jnp.logical_not(first))
    def _():
        ref[...] += val


def _dn(a, b, ca, cb):
    return lax.dot_general(a.astype(BF), b.astype(BF), (((ca,), (cb,)), ((), ())), preferred_element_type=F32)


@jax.custom_vjp
def _mm_nn(a, b):
    return _dn(a, b, 1, 0)


def _mm_nn_fwd(a, b):
    return _dn(a, b, 1, 0), (a.astype(BF), b.astype(BF))


def _mm_nn_bwd(res, ct):
    a, b = res
    return _dn(ct, b, 1, 1), _dn(a, ct, 0, 0)


_mm_nn.defvjp(_mm_nn_fwd, _mm_nn_bwd)


@jax.custom_vjp
def _mm_nt(a, b):
    return _dn(a, b, 1, 1)


def _mm_nt_fwd(a, b):
    return _dn(a, b, 1, 1), (a.astype(BF), b.astype(BF))


def _mm_nt_bwd(res, ct):
    a, b = res
    return _dn(ct, b, 1, 0), _dn(ct, a, 0, 0)


_mm_nt.defvjp(_mm_nt_fwd, _mm_nt_bwd)


def _rmsn(x, g, n):
    ms = jnp.sum(x * x, axis=-1, keepdims=True) * (1.0 / n)
    return x * lax.rsqrt(ms + EPS) * g


def _layernorm(x, g, b):
    mu = jnp.mean(x, axis=-1, keepdims=True)
    xc = x - mu
    y = xc * lax.rsqrt(jnp.mean(xc * xc, axis=-1, keepdims=True) + EPS)
    return y * g + b


def _rope(x, cos_f, sin_s, swap):
    xs = lax.dot_general(x, swap, (((1,), (0,)), ((), ())), precision=lax.Precision.HIGHEST,
                         preferred_element_type=F32)
    return x * cos_f + xs * sin_s


def _softmax(s):
    m = lax.stop_gradient(jnp.max(s, axis=-1, keepdims=True))
    p = jnp.exp(s - m)
    return p / jnp.sum(p, axis=-1, keepdims=True)


def _mm(a, b, *, ta=False, tb=False, add=None, out_dtype=F32, name):
    if ta:
        k_dim, m = a.shape
    else:
        m, k_dim = a.shape
    if tb:
        n, kb = b.shape
    else:
        kb, n = b.shape
    assert k_dim == kb, (a.shape, b.shape, ta, tb)
    tm = _pick(m, (512, 256, 128))
    tn = _pick(n, (1024, 768, 512, 384, 256, 128))
    tk = _pick(k_dim, (1024, 768, 512, 256, 128))
    nk = k_dim // tk
    ca = 0 if ta else 1
    cb = 1 if tb else 0
    has_add = add is not None

    def body(*refs):
        if has_add:
            a_ref, b_ref, add_ref, o_ref = refs[:4]
        else:
            a_ref, b_ref, o_ref = refs[:3]
            add_ref = None
        part = _dn(a_ref[...], b_ref[...], ca, cb)
        if nk == 1:
            if has_add:
                part = part + add_ref[...]
            o_ref[...] = part.astype(out_dtype)
            return
        acc = refs[-1]
        k = pl.program_id(2)
        _acc(acc, part, k == 0)

        @pl.when(k == nk - 1)
        def _():
            r = acc[...]
            if has_add:
                r = r + add_ref[...]
            o_ref[...] = r.astype(out_dtype)

    a_spec = pl.BlockSpec((tk, tm), lambda i, j, k: (k, i)) if ta else pl.BlockSpec((tm, tk), lambda i, j, k: (i, k))
    b_spec = pl.BlockSpec((tn, tk), lambda i, j, k: (j, k)) if tb else pl.BlockSpec((tk, tn), lambda i, j, k: (k, j))
    o_spec = pl.BlockSpec((tm, tn), lambda i, j, k: (i, j))
    in_specs = [a_spec, b_spec] + ([o_spec] if has_add else [])
    args = [a, b] + ([add] if has_add else [])
    return pl.pallas_call(
        body, grid=(m // tm, n // tn, nk), in_specs=in_specs, out_specs=o_spec,
        out_shape=SDS((m, n), out_dtype),
        scratch_shapes=[pltpu.VMEM((tm, tn), F32)] if nk > 1 else [],
        compiler_params=_params(("parallel", "parallel", "arbitrary")), name=name)(*args)


def _rms_fwd(x, g, name):
    n, w = x.shape
    t = min(ROW_TILE, n)

    def body(x_ref, g_ref, o_ref):
        o_ref[...] = _rmsn(x_ref[...], g_ref[...], w).astype(BF)

    return pl.pallas_call(body, grid=(n // t,), in_specs=[_rows(t, w), _full((1, w))], out_specs=_rows(t, w),
                          out_shape=SDS((n, w), BF), compiler_params=_params(("arbitrary",)), name=name)(x, g)


def _rms_bwd(x, g, dh, res, name):
    n, w = x.shape
    t = min(ROW_TILE, n)
    has_res = res is not None

    def body(*refs):
        if has_res:
            x_ref, g_ref, dh_ref, res_ref, dx_ref, dxb_ref, dg_ref = refs
        else:
            x_ref, g_ref, dh_ref, dx_ref, dxb_ref, dg_ref = refs
        _, vjp = jax.vjp(lambda xx, gg: _rmsn(xx, gg, w), x_ref[...], g_ref[...])
        dx, dg = vjp(dh_ref[...])
        if has_res:
            dx = dx + res_ref[...]
        dx_ref[...] = dx
        dxb_ref[...] = dx.astype(BF)
        _acc(dg_ref, dg, pl.program_id(0) == 0)

    in_specs = [_rows(t, w), _full((1, w)), _rows(t, w)] + ([_rows(t, w)] if has_res else [])
    args = [x, g, dh] + ([res] if has_res else [])
    return pl.pallas_call(body, grid=(n // t,), in_specs=in_specs,
                          out_specs=[_rows(t, w), _rows(t, w), _full((1, w))],
                          out_shape=[SDS((n, w), F32), SDS((n, w), BF), SDS((1, w), F32)],
                          compiler_params=_params(("arbitrary",)), name=name)(*args)


def _act_fwd(a, name):
    n, w = a.shape
    t = min(ROW_TILE, n)

    def body(a_ref, o_ref):
        r = jnp.maximum(a_ref[...], 0.0)
        o_ref[...] = (r * r).astype(BF)

    return pl.pallas_call(body, grid=(n // t,), in_specs=[_rows(t, w)], out_specs=_rows(t, w),
                          out_shape=SDS((n, w), BF), compiler_params=_params(("parallel",)), name=name)(a)


def _act_bwd(a, dr, name):
    n, w = a.shape
    t = min(ROW_TILE, n)

    def body(a_ref, dr_ref, o_ref):
        o_ref[...] = (dr_ref[...] * (2.0 * jnp.maximum(a_ref[...], 0.0))).astype(BF)

    return pl.pallas_call(body, grid=(n // t,), in_specs=[_rows(t, w), _rows(t, w)], out_specs=_rows(t, w),
                          out_shape=SDS((n, w), BF), compiler_params=_params(("parallel",)), name=name)(a, dr)


def _loss_call(y, tgt, name):
    n, w = y.shape
    t = min(ROW_TILE, n)

    def body(y_ref, t_ref, dy_ref, dyb_ref, l_ref):
        e = y_ref[...] - t_ref[...]
        dy = e * (1.0 / w)
        dy_ref[...] = dy
        dyb_ref[...] = dy.astype(BF)
        part = jnp.sum(jnp.sum(e * e, axis=-1, keepdims=True), axis=0, keepdims=True) * (0.5 / w)
        _acc(l_ref, jnp.broadcast_to(part, (8, LANES)), pl.program_id(0) == 0)

    return pl.pallas_call(body, grid=(n // t,), in_specs=[_rows(t, w), _rows(t, w)],
                          out_specs=[_rows(t, w), _rows(t, w), _full((8, LANES))],
                          out_shape=[SDS((n, w), F32), SDS((n, w), BF), SDS((8, LANES), F32)],
                          compiler_params=_params(("arbitrary",)), name=name)(y, tgt)


def _merge_core(zg0, zg1, zg2, y0, y1, y2):
    return jax.nn.sigmoid(zg0) * y0 + jax.nn.sigmoid(zg1) * y1 + jax.nn.sigmoid(zg2) * y2


def _merge_fwd(z, y_gm, y_mla, y_mem, name):
    n = z.shape[0]
    t = min(ROW_TILE, n)
    w = D_MODEL

    def body(g0, g1, g2, y0, y1, y2, o_ref):
        o_ref[...] = _merge_core(g0[...], g1[...], g2[...], y0[...], y1[...], y2[...]).astype(BF)

    return pl.pallas_call(body, grid=(n // t,),
                          in_specs=[_rows(t, w, 0), _rows(t, w, 1), _rows(t, w, 2)] + [_rows(t, w)] * 3,
                          out_specs=_rows(t, w), out_shape=SDS((n, w), BF),
                          compiler_params=_params(("parallel",)), name=name)(z, z, z, y_gm, y_mla, y_mem)


def _merge_bwd(z, y_gm, y_mla, y_mem, dmerged, name):
    n = z.shape[0]
    t = min(ROW_TILE, n)
    w = D_MODEL

    def body(g0, g1, g2, y0, y1, y2, dm, dzg_ref, d0_ref, d1_ref, d2_ref):
        _, vjp = jax.vjp(_merge_core, g0[...], g1[...], g2[...], y0[...], y1[...], y2[...])
        dg0, dg1, dg2, dy0, dy1, dy2 = vjp(dm[...])
        dzg_ref[:, 0:w] = dg0.astype(BF)
        dzg_ref[:, w:2 * w] = dg1.astype(BF)
        dzg_ref[:, 2 * w:3 * w] = dg2.astype(BF)
        d0_ref[...] = dy0.astype(BF)
        d1_ref[...] = dy1.astype(BF)
        d2_ref[...] = dy2.astype(BF)

    return pl.pallas_call(body, grid=(n // t,),
                          in_specs=[_rows(t, w, 0), _rows(t, w, 1), _rows(t, w, 2)] + [_rows(t, w)] * 4,
                          out_specs=[_rows(t, 3 * w)] + [_rows(t, w)] * 3,
                          out_shape=[SDS((n, 3 * w), BF)] + [SDS((n, w), BF)] * 3,
                          compiler_params=_params(("parallel",)), name=name)(z, z, z, y_gm, y_mla, y_mem, dmerged)


def _gm_core(zu, zv, g_ln, b_ln, ws, bcols):
    t = zu.shape[0]
    u = jax.nn.gelu(zu)
    v = _layernorm(jax.nn.gelu(zv), g_ln, b_ln)
    row = lax.broadcasted_iota(jnp.int32, (GM_CHUNK, GM_CHUNK), 0)
    col = lax.broadcasted_iota(jnp.int32, (GM_CHUNK, GM_CHUNK), 1)
    wc = [jnp.where(row >= col, ws[g], 0.0) for g in range(GM_GROUPS)]
    chunks = []
    for c in range(t // GM_CHUNK):
        cols = []
        for g in range(GM_GROUPS):
            vc = v[c * GM_CHUNK:(c + 1) * GM_CHUNK, g * LANES:(g + 1) * LANES]
            cols.append(_mm_nn(wc[g], vc) + bcols[g])
        chunks.append(jnp.concatenate(cols, axis=1))
    mixed = chunks[0] if len(chunks) == 1 else jnp.concatenate(chunks, axis=0)
    return u * mixed


def _gm_specs(t):
    return [_rows(t, GM_WIDTH, ZU // GM_WIDTH), _rows(t, GM_WIDTH, ZV // GM_WIDTH), _full((1, GM_WIDTH)),
            _full((1, GM_WIDTH)), _full((GM_GROUPS, GM_CHUNK, GM_CHUNK))] + [_full((GM_CHUNK, 1))] * GM_GROUPS


def _gm_fwd(z, g_ln, b_ln, ws, bcols, name):
    n = z.shape[0]
    t = min(ROW_TILE, n)

    def body(zu, zv, g_ref, b_ref, ws_ref, c0, c1, c2, c3, o_ref):
        out = _gm_core(zu[...], zv[...], g_ref[...], b_ref[...], [ws_ref[g] for g in range(GM_GROUPS)],
                       [c0[...], c1[...], c2[...], c3[...]])
        o_ref[...] = out.astype(BF)

    return pl.pallas_call(body, grid=(n // t,), in_specs=_gm_specs(t), out_specs=_rows(t, GM_WIDTH),
                          out_shape=SDS((n, GM_WIDTH), BF), compiler_params=_params(("parallel",)),
                          name=name)(z, z, g_ln, b_ln, ws, *bcols)


def _gm_bwd(z, g_ln, b_ln, ws, bcols, dgm, name):
    n = z.shape[0]
    t = min(ROW_TILE, n)

    def body(zu, zv, g_ref, b_ref, ws_ref, c0, c1, c2, c3, dgm_ref, dz_ref, dg_ref, db_ref, dws_ref, e0, e1, e2, e3):
        first = pl.program_id(0) == 0
        _, vjp = jax.vjp(_gm_core, zu[...], zv[...], g_ref[...], b_ref[...],
                         [ws_ref[g] for g in range(GM_GROUPS)], [c0[...], c1[...], c2[...], c3[...]])
        dzu, dzv, dg, db, dws, dcols = vjp(dgm_ref[...])
        dz_ref[:, 0:GM_WIDTH] = dzu.astype(BF)
        dz_ref[:, GM_WIDTH:2 * GM_WIDTH] = dzv.astype(BF)
        _acc(dg_ref, dg, first)
        _acc(db_ref, db, first)
        _acc(dws_ref, jnp.stack(dws, axis=0), first)
        for ref, val in zip((e0, e1, e2, e3), dcols):
            _acc(ref, val, first)

    return pl.pallas_call(
        body, grid=(n // t,), in_specs=_gm_specs(t) + [_rows(t, GM_WIDTH)],
        out_specs=[_rows(t, 2 * GM_WIDTH), _full((1, GM_WIDTH)), _full((1, GM_WIDTH)),
                   _full((GM_GROUPS, GM_CHUNK, GM_CHUNK))] + [_full((GM_CHUNK, 1))] * GM_GROUPS,
        out_shape=[SDS((n, 2 * GM_WIDTH), BF), SDS((1, GM_WIDTH), F32), SDS((1, GM_WIDTH), F32),
                   SDS((GM_GROUPS, GM_CHUNK, GM_CHUNK), F32)] + [SDS((GM_CHUNK, 1), F32)] * GM_GROUPS,
        compiler_params=_params(("arbitrary",)), name=name)(z, z, g_ln, b_ln, ws, *bcols, dgm)


def _rope_tables(pos_f, inv_full, cmask, smask, name):
    n = pos_f.shape[0]
    t = min(ROW_TILE, n)

    def body(p_ref, inv_ref, cm_ref, sm_ref, cos_ref, sin_ref):
        ang = p_ref[...] * inv_ref[...]
        cos_ref[...] = jnp.cos(ang) * cm_ref[...]
        sin_ref[...] = jnp.sin(ang) * sm_ref[...]

    return pl.pallas_call(body, grid=(n // t,), in_specs=[_rows(t, 1)] + [_full((1, LANES))] * 3,
                          out_specs=[_rows(t, LANES)] * 2, out_shape=[SDS((n, LANES), F32)] * 2,
                          compiler_params=_params(("parallel",)), name=name)(pos_f, inv_full, cmask, smask)


def _prep_core(cq, kpe, ckv, gains, wqn, wqp, wkn, wv, cos_f, sin_s, swap):
    g_cq, g_ckv, g_qn, g_qp, g_kn, g_kp = gains
    cqn = _rmsn(cq, g_cq, Q_LORA)
    ckvn = _rmsn(ckv, g_ckv, KV_LORA)
    kp = _rope(_rmsn(kpe, g_kp, MLA_ROPE), cos_f, sin_s, swap)
    qs, ks, vs = [], [], []
    for h in range(MLA_HEADS):
        qs.append(_rmsn(_mm_nn(cqn, wqn[h]), g_qn, MLA_NOPE))
        qs.append(_rope(_rmsn(_mm_nn(cqn, wqp[h]), g_qp, MLA_ROPE), cos_f, sin_s, swap))
        ks.append(_rmsn(_mm_nn(ckvn, wkn[h]), g_kn, MLA_NOPE))
        ks.append(kp)
        vs.append(_mm_nn(ckvn, wv[h]))
    return jnp.concatenate(qs, axis=1), jnp.concatenate(ks, axis=1), jnp.concatenate(vs, axis=1)


def _prep_in_specs(t):
    return ([_rows(t, Q_LORA, CQ // Q_LORA), _rows(t, LANES, KPE // LANES), _rows(t, KV_LORA, CKV // KV_LORA),
             _rows(t, LANES), _rows(t, LANES), _full((1, Q_LORA)), _full((1, KV_LORA))] + [_full((1, LANES))] * 4
            + [_full((Q_LORA, 2048)), _full((KV_LORA, 2048)), _full((LANES, LANES))])


def _prep_load(refs):
    cq, kpe, ckv, cos_f, sin_s, g_cq, g_ckv, g_qn, g_qp, g_kn, g_kp, wq, wkv, swap = refs
    hs = range(MLA_HEADS)
    wqn = [wq[:, h * LANES:(h + 1) * LANES].astype(F32) for h in hs]
    wqp = [wq[:, 1024 + h * LANES:1024 + (h + 1) * LANES].astype(F32) for h in hs]
    wkn = [wkv[:, h * LANES:(h + 1) * LANES].astype(F32) for h in hs]
    wv = [wkv[:, 1024 + h * LANES:1024 + (h + 1) * LANES].astype(F32) for h in hs]
    gains = [g_cq[...], g_ckv[...], g_qn[...], g_qp[...], g_kn[...], g_kp[...]]
    return (cq[...], kpe[...], ckv[...], gains, wqn, wqp, wkn, wv), (cos_f[...], sin_s[...], swap[...])


def _prep_fwd(z, cos_f, sin_s, gains, wq, wkv, swap, name):
    n = z.shape[0]
    t = min(ROW_TILE, n)

    def body(*refs):
        diff, const = _prep_load(refs[:14])
        q, k, v = _prep_core(*diff, *const)
        q_ref, k_ref, v_ref = refs[14:]
        q_ref[...] = q.astype(BF)
        k_ref[...] = k.astype(BF)
        v_ref[...] = v.astype(BF)

    return pl.pallas_call(body, grid=(n // t,), in_specs=_prep_in_specs(t),
                          out_specs=[_rows(t, 2048), _rows(t, 2048), _rows(t, 1024)],
                          out_shape=[SDS((n, 2048), BF), SDS((n, 2048), BF), SDS((n, 1024), BF)],
                          compiler_params=_params(("parallel",)),
                          name=name)(z, z, z, cos_f, sin_s, *gains, wq, wkv, swap)


def _prep_bwd(z, cos_f, sin_s, gains, wq, wkv, swap, dq, dk, dv, name):
    n = z.shape[0]
    t = min(ROW_TILE, n)
    wz = Q_LORA + LANES + KV_LORA

    def body(*refs):
        diff, const = _prep_load(refs[:14])
        dq_ref, dk_ref, dv_ref = refs[14:17]
        dz_ref, o_cq, o_ckv, o_qn, o_qp, o_kn, o_kp, dwq_ref, dwkv_ref = refs[17:]
        first = pl.program_id(0) == 0
        _, vjp = jax.vjp(lambda *d: _prep_core(*d, *const), *diff)
        dcq, dkpe, dckv, dgains, dwqn, dwqp, dwkn, dwv = vjp((dq_ref[...], dk_ref[...], dv_ref[...]))
        dz_ref[:, 0:Q_LORA] = dcq.astype(BF)
        dz_ref[:, Q_LORA:Q_LORA + LANES] = dkpe.astype(BF)
        dz_ref[:, Q_LORA + LANES:wz] = dckv.astype(BF)
        for ref, val in zip((o_cq, o_ckv, o_qn, o_qp, o_kn, o_kp), dgains):
            _acc(ref, val, first)
        _acc(dwq_ref, jnp.concatenate(dwqn + dwqp, axis=1), first)
        _acc(dwkv_ref, jnp.concatenate(dwkn + dwv, axis=1), first)

    gain_specs = [_full((1, Q_LORA)), _full((1, KV_LORA))] + [_full((1, LANES))] * 4
    gain_shapes = [SDS((1, Q_LORA), F32), SDS((1, KV_LORA), F32)] + [SDS((1, LANES), F32)] * 4
    return pl.pallas_call(
        body, grid=(n // t,), in_specs=_prep_in_specs(t) + [_rows(t, 2048), _rows(t, 2048), _rows(t, 1024)],
        out_specs=[_rows(t, wz)] + gain_specs + [_full((Q_LORA, 2048)), _full((KV_LORA, 2048))],
        out_shape=[SDS((n, wz), BF)] + gain_shapes + [SDS((Q_LORA, 2048), F32), SDS((KV_LORA, 2048), F32)],
        compiler_params=_params(("arbitrary",)),
        name=name)(z, z, z, cos_f, sin_s, *gains, wq, wkv, swap, dq, dk, dv)


MLA_QK = 256
MLA_SCALE = 1.0 / math.sqrt(MLA_NOPE + MLA_ROPE)


def _causal_mask(s, q0, k0):
    tq, tk = s.shape
    row = q0 + lax.broadcasted_iota(jnp.int32, (tq, tk), 0)
    col = k0 + lax.broadcasted_iota(jnp.int32, (tq, tk), 1)
    return jnp.where(row >= col, s, -jnp.inf)


def _mla_fwd(q, k, v, batch, seq, name):
    n = q.shape[0]
    tq = min(ATT_TILE, seq)
    nq = seq // tq

    def body(q_ref, k_ref, v_ref, o_ref, lse_ref):
        i = pl.program_id(2)
        qb = q_ref[...]

        def step(j, carry):
            m, l, acc = carry
            k0 = pl.multiple_of(j * tq, tq)
            kb = k_ref[pl.ds(k0, tq), :]
            vb = v_ref[pl.ds(k0, tq), :]
            s = _causal_mask(_dn(qb, kb, 1, 1) * MLA_SCALE, i * tq, k0)
            m_new = jnp.maximum(m, jnp.max(s, axis=-1, keepdims=True))
            p = jnp.exp(s - m_new)
            alpha = jnp.exp(m - m_new)
            l = alpha * l + jnp.sum(p, axis=-1, keepdims=True)
            acc = alpha * acc + _dn(p, vb, 1, 0)
            return m_new, l, acc

        init = (jnp.full((tq, 1), -jnp.inf, F32), jnp.zeros((tq, 1), F32), jnp.zeros((tq, MLA_V), F32))
        m, l, acc = lax.fori_loop(0, i + 1, step, init)
        o_ref[...] = acc / l
        lse_ref[...] = jnp.broadcast_to(m + jnp.log(l), (tq, LANES))

    return pl.pallas_call(
        body, grid=(batch, MLA_HEADS, nq),
        in_specs=[pl.BlockSpec((tq, MLA_QK), lambda b, h, i: (b * nq + i, h)),
                  pl.BlockSpec((seq, MLA_QK), lambda b, h, i: (b, h)),
                  pl.BlockSpec((seq, MLA_V), lambda b, h, i: (b, h))],
        out_specs=[pl.BlockSpec((tq, MLA_V), lambda b, h, i: (b * nq + i, h)),
                   pl.BlockSpec((tq, LANES), lambda b, h, i: (b * nq + i, h))],
        out_shape=[SDS((n, MLA_HEADS * MLA_V), F32), SDS((n, MLA_HEADS * LANES), F32)],
        compiler_params=_params(("parallel", "parallel", "arbitrary")), name=name)(q, k, v)


def _mla_bwd(q, k, v, o, lse, do, batch, seq, name):
    n = q.shape[0]
    tk = min(ATT_TILE, seq)
    nk = seq // tk

    def body(q_ref, k_ref, v_ref, o_ref, lse_ref, do_ref, dq_ref, dk_ref, dv_ref):
        jk = pl.program_id(2)
        kb = k_ref[...]
        vb = v_ref[...]

        @pl.when(jk == 0)
        def _():
            dq_ref[...] = jnp.zeros_like(dq_ref)

        def step(i, carry):
            dk_acc, dv_acc = carry
            q0 = pl.multiple_of(i * tk, tk)
            rows = pl.ds(q0, tk)
            qb = q_ref[rows, :]
            dob = do_ref[rows, :]
            delta = jnp.sum(dob * o_ref[rows, :], axis=-1, keepdims=True)
            s = _causal_mask(_dn(qb, kb, 1, 1) * MLA_SCALE, q0, jk * tk)
            p = jnp.exp(s - lse_ref[rows, :][:, 0:1])
            dv_acc = dv_acc + _dn(p, dob, 0, 0)
            dp = _dn(dob, vb, 1, 1)
            ds = p * (dp - delta) * MLA_SCALE
            dk_acc = dk_acc + _dn(ds, qb, 0, 0)
            dq_ref[rows, :] += _dn(ds, kb, 1, 0)
            return dk_acc, dv_acc

        dk_acc, dv_acc = lax.fori_loop(jk, nk, step, (jnp.zeros((tk, MLA_QK), F32), jnp.zeros((tk, MLA_V), F32)))
        dk_ref[...] = dk_acc
        dv_ref[...] = dv_acc

    full_qk = pl.BlockSpec((seq, MLA_QK), lambda b, h, j: (b, h))
    full_v = pl.BlockSpec((seq, MLA_V), lambda b, h, j: (b, h))
    blk_qk = pl.BlockSpec((tk, MLA_QK), lambda b, h, j: (b * nk + j, h))
    blk_v = pl.BlockSpec((tk, MLA_V), lambda b, h, j: (b * nk + j, h))
    return pl.pallas_call(
        body, grid=(batch, MLA_HEADS, nk),
        in_specs=[full_qk, blk_qk, blk_v, full_v, full_v, full_v],
        out_specs=[full_qk, blk_qk, blk_v],
        out_shape=[SDS((n, MLA_HEADS * MLA_QK), F32), SDS((n, MLA_HEADS * MLA_QK), F32),
                   SDS((n, MLA_HEADS * MLA_V), F32)],
        compiler_params=_params(("parallel", "parallel", "arbitrary")), name=name)(q, k, v, o, lse, do)


MEM_SCALE = 1.0 / math.sqrt(HEAD_DIM)
MEM_W = MEM_HEADS * HEAD_DIM


def _mem_core(qs, ks, vs, g_mq, g_mk):
    outs = []
    for h in range(MEM_HEADS):
        qh = _rmsn(qs[h], g_mq, HEAD_DIM)
        kh = _rmsn(ks[h], g_mk, HEAD_DIM)
        p = _softmax(_mm_nt(qh, kh) * MEM_SCALE)
        outs.append(_mm_nn(p, vs[h]))
    return jnp.concatenate(outs, axis=1)


def _mem_load(qm, kvm, g_mq, g_mk):
    hs = range(MEM_HEADS)
    qs = [qm[:, h * LANES:(h + 1) * LANES] for h in hs]
    ks = [kvm[:, h * LANES:(h + 1) * LANES] for h in hs]
    vs = [kvm[:, MEM_W + h * LANES:MEM_W + (h + 1) * LANES] for h in hs]
    return qs, ks, vs, g_mq[...], g_mk[...]


def _mem_fwd(z, kvm, g_mq, g_mk, batch, seq, name):
    n = z.shape[0]
    t = min(ROW_TILE, seq)
    per = seq // t

    def body(qm, kvm_ref, gq, gk, o_ref):
        o_ref[...] = _mem_core(*_mem_load(qm, kvm_ref, gq, gk)).astype(BF)

    return pl.pallas_call(
        body, grid=(n // t,),
        in_specs=[_rows(t, MEM_W, QM // MEM_W), pl.BlockSpec((MEM_LEN, 2 * MEM_W), lambda i: (i // per, 0)),
                  _full((1, LANES)), _full((1, LANES))],
        out_specs=_rows(t, MEM_W), out_shape=SDS((n, MEM_W), BF),
        compiler_params=_params(("parallel",)), name=name)(z, kvm, g_mq, g_mk)


def _mem_bwd(z, kvm, g_mq, g_mk, dom, batch, seq, name):
    n = z.shape[0]
    t = min(ROW_TILE, seq)
    per = seq // t

    def body(qm, kvm_ref, gq, gk, dom_ref, dz_ref, dkvm_ref, dgq_ref, dgk_ref):
        i = pl.program_id(0)
        _, vjp = jax.vjp(_mem_core, *_mem_load(qm, kvm_ref, gq, gk))
        dqs, dks, dvs, dgq, dgk = vjp(dom_ref[...])
        dz_ref[...] = jnp.concatenate(dqs, axis=1).astype(BF)
        _acc(dkvm_ref, jnp.concatenate(dks + dvs, axis=1), i % per == 0)
        _acc(dgq_ref, dgq, i == 0)
        _acc(dgk_ref, dgk, i == 0)

    kv_spec = pl.BlockSpec((MEM_LEN, 2 * MEM_W), lambda i: (i // per, 0))
    return pl.pallas_call(
        body, grid=(n // t,),
        in_specs=[_rows(t, MEM_W, QM // MEM_W), kv_spec, _full((1, LANES)), _full((1, LANES)), _rows(t, MEM_W)],
        out_specs=[_rows(t, MEM_W), kv_spec, _full((1, LANES)), _full((1, LANES))],
        out_shape=[SDS((n, MEM_W), BF), SDS((batch * MEM_LEN, 2 * MEM_W), F32), SDS((1, LANES), F32),
                   SDS((1, LANES), F32)],
        compiler_params=_params(("arbitrary",)), name=name)(z, kvm, g_mq, g_mk, dom)


ANY = pl.BlockSpec(memory_space=pl.ANY)


def _me():
    return lax.axis_index("x"), lax.axis_index("y"), lax.axis_index("c")


def _other_chips(x, y):
    return [(1 - x, y), (x, 1 - y), (1 - x, 1 - y)]


def _gather_shards(shard, name):
    def body(s_ref, o_ref, send, recv, local):
        x, y, c = _me()
        k = 2 * x + y
        sib = (x, y, 1 - c)
        chips = _other_chips(x, y)

        def half(chip, core):
            return o_ref.at[2 * chip[0] + chip[1], pl.ds(core * HALF_ROWS, HALF_ROWS)]

        def copy(sem, src, dst, to):
            return pltpu.make_async_remote_copy(src_ref=src, dst_ref=dst, send_sem=send.at[sem],
                                                recv_sem=recv.at[sem], device_id=to, device_id_type=MESH)

        mine = pltpu.make_async_copy(s_ref, o_ref.at[k], local)
        mine.start()
        my_half = s_ref.at[pl.ds(c * HALF_ROWS, HALF_ROWS)]
        first = [copy(j, my_half, half((x, y), c), (*chip, c)) for j, chip in enumerate(chips)]
        for cp in first:
            cp.start()
        passed = [copy(3 + j, half(chip, c), half(chip, c), sib) for j, chip in enumerate(chips)]
        for j, chip in enumerate(chips):
            copy(j, my_half, half(chip, c), (*chip, c)).wait_recv()
            passed[j].start()
        for j, chip in enumerate(chips):
            copy(3 + j, half(chip, 1 - c), half(chip, 1 - c), sib).wait_recv()
        for cp in first + passed:
            cp.wait_send()
        mine.wait()

    return pl.pallas_call(
        body, in_specs=[ANY], out_specs=ANY, out_shape=SDS((N_CHIPS, SHARD_ROWS, D_MODEL), BF),
        scratch_shapes=[pltpu.SemaphoreType.DMA((6,)), pltpu.SemaphoreType.DMA((6,)), pltpu.SemaphoreType.DMA],
        name=name)(shard)


def _pair_split(g, name):
    def body(g_ref, mine_ref, theirs_ref, send, recv, local):
        x, y, c = _me()
        keep = pltpu.make_async_copy(g_ref.at[:, pl.ds(c * HALF_ROWS, HALF_ROWS)], mine_ref, local)
        keep.start()
        give = pltpu.make_async_remote_copy(
            src_ref=g_ref.at[:, pl.ds((1 - c) * HALF_ROWS, HALF_ROWS)], dst_ref=theirs_ref, send_sem=send,
            recv_sem=recv, device_id=(x, y, 1 - c), device_id_type=MESH)
        give.start()
        give.wait()
        keep.wait()

    shape = SDS((N_CHIPS, HALF_ROWS, D_MODEL), F32)
    return pl.pallas_call(body, in_specs=[ANY], out_specs=[ANY, ANY], out_shape=[shape, shape],
                          scratch_shapes=[pltpu.SemaphoreType.DMA] * 3, name=name)(g)


def _scatter_partials(p, name):
    def body(p_ref, o_ref, send, recv, local):
        x, y, c = _me()
        chips = _other_chips(x, y)
        keep = pltpu.make_async_copy(p_ref.at[2 * x + y], o_ref.at[0], local)
        keep.start()
        sends = [pltpu.make_async_remote_copy(
            src_ref=p_ref.at[2 * chip[0] + chip[1]], dst_ref=o_ref.at[1 + j], send_sem=send.at[j],
            recv_sem=recv.at[j], device_id=(*chip, c), device_id_type=MESH) for j, chip in enumerate(chips)]
        for cp in sends:
            cp.start()
        for cp in sends:
            cp.wait()
        keep.wait()

    return pl.pallas_call(
        body, in_specs=[ANY], out_specs=ANY, out_shape=SDS((N_CHIPS, HALF_ROWS, D_MODEL), F32),
        scratch_shapes=[pltpu.SemaphoreType.DMA((3,)), pltpu.SemaphoreType.DMA((3,)), pltpu.SemaphoreType.DMA],
        name=name)(p)


def _join_halves(r, name):
    def body(r_ref, o_ref, send, recv, local):
        x, y, c = _me()
        rows = pl.ds(c * HALF_ROWS, HALF_ROWS)
        keep = pltpu.make_async_copy(r_ref, o_ref.at[rows], local)
        keep.start()
        give = pltpu.make_async_remote_copy(src_ref=r_ref, dst_ref=o_ref.at[rows], send_sem=send, recv_sem=recv,
                                            device_id=(x, y, 1 - c), device_id_type=MESH)
        give.start()
        give.wait()
        keep.wait()

    return pl.pallas_call(body, in_specs=[ANY], out_specs=ANY, out_shape=SDS((SHARD_ROWS, D_MODEL), F32),
                          scratch_shapes=[pltpu.SemaphoreType.DMA] * 3, name=name)(r)


def _gather_small(s, name):
    def body(s_ref, o_ref, send, recv, local):
        x, y, c = _me()
        me = 4 * x + 2 * y + c
        keep = pltpu.make_async_copy(s_ref, o_ref.at[me], local)
        keep.start()
        sends = []
        for r in range(1, 8):
            fx, fy, fc = (r >> 2) & 1, (r >> 1) & 1, r & 1
            to = (x ^ fx, y ^ fy, c ^ fc)
            sends.append(pltpu.make_async_remote_copy(
                src_ref=s_ref, dst_ref=o_ref.at[me], send_sem=send.at[r - 1], recv_sem=recv.at[r - 1],
                device_id=to, device_id_type=MESH))
        for cp in sends:
            cp.start()
        for r in range(1, 8):
            fx, fy, fc = (r >> 2) & 1, (r >> 1) & 1, r & 1
            src = 4 * (x ^ fx) + 2 * (y ^ fy) + (c ^ fc)
            pltpu.make_async_remote_copy(
                src_ref=s_ref, dst_ref=o_ref.at[src], send_sem=send.at[r - 1], recv_sem=recv.at[r - 1],
                device_id=(x ^ fx, y ^ fy, c ^ fc), device_id_type=MESH).wait_recv()
        for cp in sends:
            cp.wait_send()
        keep.wait()

    return pl.pallas_call(
        body, in_specs=[ANY], out_specs=ANY, out_shape=SDS((8, SMALL_ROWS, LANES), F32),
        scratch_shapes=[pltpu.SemaphoreType.DMA((7,)), pltpu.SemaphoreType.DMA((7,)), pltpu.SemaphoreType.DMA],
        name=name)(s)


def _add2(a, b, name):
    rows, w = a.shape
    t = _pick(rows, (FLAT_TILE, 8))

    def body(a_ref, b_ref, o_ref):
        o_ref[...] = a_ref[...] + b_ref[...]

    return pl.pallas_call(body, grid=(rows // t,), in_specs=[_rows(t, w)] * 2, out_specs=_rows(t, w),
                          out_shape=SDS((rows, w), F32), compiler_params=_params(("parallel",)), name=name)(a, b)


def _sum_slots(s, name):
    k, rows, w = s.shape
    t = _pick(rows, (FLAT_TILE, 80, 8))

    def body(s_ref, o_ref):
        acc = s_ref[0]
        for j in range(1, k):
            acc = acc + s_ref[j]
        o_ref[...] = acc

    return pl.pallas_call(body, grid=(rows // t,), in_specs=[pl.BlockSpec((k, t, w), lambda i: (0, i, 0))],
                          out_specs=_rows(t, w), out_shape=SDS((rows, w), F32),
                          compiler_params=_params(("parallel",)), name=name)(s)


def _adamw(w, g, m, v, name):
    rows, width = w.shape
    t = _pick(rows, (FLAT_TILE, 80, 8))

    def body(w_ref, g_ref, m_ref, v_ref, d_ref, nm_ref, nv_ref):
        g_ = g_ref[...]
        nm = ADAM_B1 * m_ref[...] + (1.0 - ADAM_B1) * g_
        nv = ADAM_B2 * v_ref[...] + (1.0 - ADAM_B2) * (g_ * g_)
        m_hat = nm / (1.0 - ADAM_B1 ** ADAM_STEP)
        v_hat = nv / (1.0 - ADAM_B2 ** ADAM_STEP)
        d_ref[...] = -ADAM_LR * (m_hat / (jnp.sqrt(v_hat) + ADAM_EPS) + ADAM_WD * w_ref[...])
        nm_ref[...] = nm
        nv_ref[...] = nv

    return pl.pallas_call(body, grid=(rows // t,), in_specs=[_rows(t, width)] * 4, out_specs=[_rows(t, width)] * 3,
                          out_shape=[SDS((rows, width), F32)] * 3, compiler_params=_params(("parallel",)),
                          name=name)(w, g, m, v)


def _pack_shard(ws):
    return jnp.concatenate([ws[n].reshape(-1, D_MODEL) for n in BIG], axis=0)


def _shard_shape(name):
    r, c = BIG_SHAPE[name]
    return (r, c // N_CHIPS) if name in COL_SHARDED else (r // N_CHIPS, c)


def _unpack_shard(flat):
    out, r0 = {}, 0
    for n in BIG:
        shp = _shard_shape(n)
        rows = shp[0] * shp[1] // D_MODEL
        out[n] = flat[r0:r0 + rows].reshape(shp)
        r0 += rows
    return out


def _unpack_gathered(allw):
    per_chip = [_unpack_shard(allw[k]) for k in range(N_CHIPS)]
    return {n: jnp.concatenate([per_chip[k][n] for k in range(N_CHIPS)], axis=1 if n in COL_SHARDED else 0)
            for n in BIG}


def _pack_full_grads(gs):
    chips = []
    for k in range(N_CHIPS):
        parts = {}
        for n in BIG:
            r, c = _shard_shape(n)
            parts[n] = gs[n][:, k * c:(k + 1) * c] if n in COL_SHARDED else gs[n][k * r:(k + 1) * r]
        chips.append(_pack_shard(parts))
    return jnp.stack(chips, axis=0)


def _win_layout(w_in):
    pad = jnp.zeros((w_in.shape[0], LANES - MLA_ROPE), w_in.dtype)
    u, v, cq = w_in[:, 0:512], w_in[:, 512:1024], w_in[:, 1024:1408]
    ckv, kpe, qm, zg = w_in[:, 1408:1664], w_in[:, 1664:1728], w_in[:, 1728:2240], w_in[:, 2240:5312]
    return jnp.concatenate([zg, u, v, qm, cq, kpe, pad, ckv], axis=1)


def _win_unlayout(g):
    zg, u, v, qm = g[:, ZG:ZG + 3072], g[:, ZU:ZU + 512], g[:, ZV:ZV + 512], g[:, QM:QM + 512]
    cq, kpe, ckv = g[:, CQ:CQ + 384], g[:, KPE:KPE + MLA_ROPE], g[:, CKV:CKV + 256]
    return jnp.concatenate([u, v, cq, ckv, kpe, qm, zg], axis=1)


def _wq_layout(w_uq):
    w = w_uq.reshape(Q_LORA, MLA_HEADS, MLA_NOPE + MLA_ROPE)
    nope = w[:, :, :MLA_NOPE].reshape(Q_LORA, MLA_HEADS * MLA_NOPE)
    pe = jnp.pad(w[:, :, MLA_NOPE:], ((0, 0), (0, 0), (0, LANES - MLA_ROPE))).reshape(Q_LORA, MLA_HEADS * LANES)
    return jnp.concatenate([nope, pe], axis=1)


def _wq_unlayout(g):
    nope = g[:, :1024].reshape(Q_LORA, MLA_HEADS, MLA_NOPE)
    pe = g[:, 1024:].reshape(Q_LORA, MLA_HEADS, LANES)[:, :, :MLA_ROPE]
    return jnp.concatenate([nope, pe], axis=2).reshape(Q_LORA, MLA_HEADS * (MLA_NOPE + MLA_ROPE))


def _wkv_layout(w_ukv):
    w = w_ukv.reshape(KV_LORA, MLA_HEADS, MLA_NOPE + MLA_V)
    return jnp.concatenate([w[:, :, :MLA_NOPE].reshape(KV_LORA, 1024), w[:, :, MLA_NOPE:].reshape(KV_LORA, 1024)],
                           axis=1)


def _wkv_unlayout(g):
    kn = g[:, :1024].reshape(KV_LORA, MLA_HEADS, MLA_NOPE)
    v = g[:, 1024:].reshape(KV_LORA, MLA_HEADS, MLA_V)
    return jnp.concatenate([kn, v], axis=2).reshape(KV_LORA, MLA_HEADS * (MLA_NOPE + MLA_V))


def _small_rows(name, a):
    a = a.reshape(-1)
    if a.shape[0] % LANES:
        a = jnp.pad(a, (0, LANES - a.shape[0] % LANES))
    return a.reshape(-1, LANES)


def _pack_small(d):
    rows = jnp.concatenate([_small_rows(n, d[n]) for n in SMALL], axis=0)
    return jnp.pad(rows, ((0, SMALL_ROWS - rows.shape[0]), (0, 0)))


SMALL_SHAPE = {"g_mix": (1, 1024), "g_cq": (1, 384), "g_ckv": (1, 256), "g_q_nope": (1, 128), "g_q_pe": (1, 64),
               "g_k_nope": (1, 128), "g_k_pe": (1, 64), "g_gm_ln": (1, 512), "b_gm_ln": (1, 512),
               "w_spatial": (1, 4, 128, 128), "b_spatial": (1, 4, 128), "g_mem": (1, 1024), "g_mq": (1, 128),
               "g_mk": (1, 128), "g_ffn": (1, 1024)}


def _unpack_small(rows):
    out, r0 = {}, 0
    for n in SMALL:
        size = int(np.prod(SMALL_SHAPE[n]))
        nr = -(-size // LANES)
        out[n] = rows[r0:r0 + nr].reshape(-1)[:size].reshape(SMALL_SHAPE[n])
        r0 += nr
    return out


def _pad_lanes(g):
    return jnp.pad(g, ((0, 0), (0, LANES - g.shape[1])))


def kernel(x, mem, positions, g_mix, w_in, g_cq, w_uq, g_ckv, w_ukv, g_q_nope, g_q_pe, g_k_nope, g_k_pe, g_gm_ln, b_gm_ln, w_spatial, b_spatial, g_mem, w_mem_kv, g_mq, g_mk, w_o_gm, w_o_mla, w_o_mem, w_out, g_ffn, w_ff1, w_ff2, loss_target, m_g_mix, m_w_in, m_g_cq, m_w_uq, m_g_ckv, m_w_ukv, m_g_q_nope, m_g_q_pe, m_g_k_nope, m_g_k_pe, m_g_gm_ln, m_b_gm_ln, m_w_spatial, m_b_spatial, m_g_mem, m_w_mem_kv, m_g_mq, m_g_mk, m_w_o_gm, m_w_o_mla, m_w_o_mem, m_w_out, m_g_ffn, m_w_ff1, m_w_ff2, v_g_mix, v_w_in, v_g_cq, v_w_uq, v_g_ckv, v_w_ukv, v_g_q_nope, v_g_q_pe, v_g_k_nope, v_g_k_pe, v_g_gm_ln, v_b_gm_ln, v_w_spatial, v_b_spatial, v_g_mem, v_w_mem_kv, v_g_mq, v_g_mk, v_w_o_gm, v_w_o_mla, v_w_o_mem, v_w_out, v_g_ffn, v_w_ff1, v_w_ff2):
    given = dict(locals())
    wts = {n: given[n] for n in WEIGHTS}
    mom = {n: given["m_" + n] for n in WEIGHTS}
    var = {n: given["v_" + n] for n in WEIGHTS}
    batch, seq, _ = x.shape
    n_tok = batch * seq

    shard_bf = _pack_shard({n: wts[n][0].astype(BF) for n in BIG})
    full = _unpack_gathered(_gather_shards(shard_bf, "gather_weights"))
    win = _win_layout(full["w_in"])
    wq = _wq_layout(full["w_uq"])
    wkv = _wkv_layout(full["w_ukv"])

    x2 = x.reshape(n_tok, D_MODEL)
    tgt2 = loss_target.reshape(n_tok, D_MODEL)
    mem2 = mem.reshape(batch * MEM_LEN, D_MODEL)
    pos_f = positions.reshape(n_tok, 1).astype(F32)

    inv = ROPE_BASE ** (-jnp.arange(0, MLA_ROPE, 2, dtype=F32) / MLA_ROPE)
    zeros64 = jnp.zeros((LANES - MLA_ROPE,), F32)
    inv_full = jnp.concatenate([inv, inv, zeros64]).reshape(1, LANES)
    half = MLA_ROPE // 2
    cmask = jnp.concatenate([jnp.ones((MLA_ROPE,), F32), zeros64]).reshape(1, LANES)
    smask = jnp.concatenate([-jnp.ones((half,), F32), jnp.ones((half,), F32), zeros64]).reshape(1, LANES)
    swap_np = np.zeros((LANES, LANES), np.float32)
    for j in range(half):
        swap_np[j + half, j] = 1.0
        swap_np[j, j + half] = 1.0
    swap = jnp.asarray(swap_np)

    prep_gains = [g_cq, g_ckv, g_q_nope, _pad_lanes(g_q_pe), g_k_nope, _pad_lanes(g_k_pe)]
    ws = w_spatial[0]
    bcols = [b_spatial[0, g].reshape(GM_CHUNK, 1) for g in range(GM_GROUPS)]

    h1 = _rms_fwd(x2, g_mix, "rms_mix")
    z = _mm(h1, win, name="mm_in")
    gm = _gm_fwd(z, g_gm_ln, b_gm_ln, ws, bcols, "gm_fwd")
    cos_f, sin_s = _rope_tables(pos_f, inv_full, cmask, smask, "rope_tables")
    qc, kc, vc = _prep_fwd(z, cos_f, sin_s, prep_gains, wq, wkv, swap, "prep_fwd")
    o_mla, lse = _mla_fwd(qc, kc, vc, batch, seq, "mla_fwd")
    memn = _rms_fwd(mem2, g_mem, "rms_mem")
    kvm = _mm(memn, full["w_mem_kv"], name="mm_memkv")
    o_mem = _mem_fwd(z, kvm, g_mq, g_mk, batch, seq, "mem_fwd")
    y_gm = _mm(gm, full["w_o_gm"], name="mm_o_gm")
    y_mla = _mm(o_mla, full["w_o_mla"], name="mm_o_mla")
    y_mem = _mm(o_mem, full["w_o_mem"], name="mm_o_mem")
    merged = _merge_fwd(z, y_gm, y_mla, y_mem, "merge_fwd")
    x1 = _mm(merged, full["w_out"], add=x2, name="mm_out")
    h2 = _rms_fwd(x1, g_ffn, "rms_ffn")
    a_ff = _mm(h2, full["w_ff1"], name="mm_ff1")
    r_ff = _act_fwd(a_ff, "act_fwd")
    y = _mm(r_ff, full["w_ff2"], add=x1, name="mm_ff2")
    dy, dyb, loss_tile = _loss_call(y, tgt2, "loss")

    gw = {}
    dr = _mm(dyb, full["w_ff2"], tb=True, name="mm_d_r")
    gw["w_ff2"] = _mm(r_ff, dyb, ta=True, name="mm_dw_ff2")
    da = _act_bwd(a_ff, dr, "act_bwd")
    gw["w_ff1"] = _mm(h2, da, ta=True, name="mm_dw_ff1")
    dh2 = _mm(da, full["w_ff1"], tb=True, name="mm_d_h2")
    dx1, dx1b, dg_ffn = _rms_bwd(x1, g_ffn, dh2, dy, "rms_ffn_bwd")
    dmerged = _mm(dx1b, full["w_out"], tb=True, name="mm_d_merged")
    gw["w_out"] = _mm(merged, dx1b, ta=True, name="mm_dw_out")
    dzg, dy_gm, dy_mla, dy_mem = _merge_bwd(z, y_gm, y_mla, y_mem, dmerged, "merge_bwd")
    dgm = _mm(dy_gm, full["w_o_gm"], tb=True, name="mm_d_gm")
    gw["w_o_gm"] = _mm(gm, dy_gm, ta=True, name="mm_dw_o_gm")
    do_mla = _mm(dy_mla, full["w_o_mla"], tb=True, name="mm_d_omla")
    gw["w_o_mla"] = _mm(o_mla, dy_mla, ta=True, name="mm_dw_o_mla")
    do_mem = _mm(dy_mem, full["w_o_mem"], tb=True, name="mm_d_omem")
    gw["w_o_mem"] = _mm(o_mem, dy_mem, ta=True, name="mm_dw_o_mem")
    dz_uv, dg_ln, db_ln, dws, *dbcols = _gm_bwd(z, g_gm_ln, b_gm_ln, ws, bcols, dgm, "gm_bwd")
    dq, dk, dv = _mla_bwd(qc, kc, vc, o_mla, lse, do_mla, batch, seq, "mla_bwd")
    dz_mla, dg_cq, dg_ckv, dg_qn, dg_qp, dg_kn, dg_kp, dwq, dwkv = _prep_bwd(
        z, cos_f, sin_s, prep_gains, wq, wkv, swap, dq, dk, dv, "prep_bwd")
    dz_qm, dkvm, dg_mq, dg_mk = _mem_bwd(z, kvm, g_mq, g_mk, do_mem, batch, seq, "mem_bwd")
    dmemn = _mm(dkvm, full["w_mem_kv"], tb=True, name="mm_d_memn")
    gw["w_mem_kv"] = _mm(memn, dkvm, ta=True, name="mm_dw_memkv")
    _, _, dg_mem = _rms_bwd(mem2, g_mem, dmemn, None, "rms_mem_bwd")
    dz = jnp.concatenate([dzg, dz_uv, dz_qm, dz_mla], axis=1)
    dh1 = _mm(dz, win, tb=True, name="mm_d_h1")
    gw["w_in"] = _win_unlayout(_mm(h1, dz, ta=True, name="mm_dw_in"))
    gw["w_uq"] = _wq_unlayout(dwq)
    gw["w_ukv"] = _wkv_unlayout(dwkv)
    grad_x, _, dg_mix = _rms_bwd(x2, g_mix, dh1, dx1, "rms_mix_bwd")

    mine, theirs = _pair_split(_pack_full_grads(gw), "pair_split")
    pair = _add2(mine.reshape(-1, D_MODEL), theirs.reshape(-1, D_MODEL), "pair_add")
    slots = _scatter_partials(pair.reshape(N_CHIPS, HALF_ROWS, D_MODEL), "scatter_partials")
    g_flat = _join_halves(_sum_slots(slots, "sum_chips"), "join_halves")

    small_g = {"g_mix": dg_mix, "g_cq": dg_cq, "g_ckv": dg_ckv, "g_q_nope": dg_qn, "g_q_pe": dg_qp[:, :MLA_ROPE],
               "g_k_nope": dg_kn, "g_k_pe": dg_kp[:, :MLA_ROPE], "g_gm_ln": dg_ln, "b_gm_ln": db_ln,
               "w_spatial": dws, "b_spatial": jnp.concatenate(dbcols, axis=1).T, "g_mem": dg_mem, "g_mq": dg_mq,
               "g_mk": dg_mk, "g_ffn": dg_ffn}
    g_small = _sum_slots(_gather_small(_pack_small(small_g), "gather_small"), "sum_small")

    w_flat = _pack_shard({n: wts[n][0] for n in BIG})
    m_flat = _pack_shard({n: mom[n][0] for n in BIG})
    v_flat = _pack_shard({n: var[n][0] for n in BIG})
    d_flat, nm_flat, nv_flat = _adamw(w_flat, g_flat, m_flat, v_flat, "adamw_shards")
    d_small, nm_small, nv_small = _adamw(_pack_small(wts), g_small, _pack_small(mom), _pack_small(var), "adamw_small")

    def unpack(flat, small):
        big = {n: a[None] for n, a in _unpack_shard(flat).items()}
        big.update(_unpack_small(small))
        return [big[n] for n in WEIGHTS]

    loss = lax.psum(loss_tile[0, 0], ("x", "y", "c"))
    grad_x = grad_x.reshape(batch, seq, D_MODEL)
    return (loss, grad_x, *unpack(g_flat, g_small), *unpack(d_flat, d_small), *unpack(nm_flat, nm_small),
            *unpack(nv_flat, nv_small))
```

```python
import functools
import math

import numpy as np
import jax
import jax.numpy as jnp
from jax import lax
from jax.experimental import pallas as pl
from jax.experimental.pallas import tpu as pltpu

F32 = jnp.float32
BF = jnp.bfloat16
SDS = jax.ShapeDtypeStruct
MESH = pl.DeviceIdType.MESH

D_MODEL = 1024
MEM_LEN = 256
MEM_HEADS = 4
HEAD_DIM = 128
GM_WIDTH = 512
GM_CHUNK = 128
GM_GROUPS = 4
MLA_HEADS = 8
MLA_NOPE = 128
MLA_ROPE = 64
MLA_V = 128
Q_LORA = 384
KV_LORA = 256
ROPE_BASE = 10000.0
D_FF = 4096
EPS = 1e-6
W_IN_COLS = 5312
ADAM_LR, ADAM_B1, ADAM_B2, ADAM_EPS, ADAM_WD, ADAM_STEP = 0.001, 0.9, 0.999, 1e-08, 0.01, 10

ZG, ZU, ZV, QM, CQ, KPE, CKV = 0, 3072, 3584, 4096, 4608, 4992, 5120
Z_COLS = 5376
LANES = 128
ROW_TILE = 256
ATT_TILE = 512
VMEM_LIMIT = 56 * 1024 * 1024

N_CHIPS = 4
SHARD_ROWS = 4672
HALF_ROWS = SHARD_ROWS // 2
FLAT_TILE = 584
PIECES = 4
PIECE_ROWS = HALF_ROWS // PIECES
GATHER_PIECES = 2
GATHER_ROWS = HALF_ROWS // GATHER_PIECES
SMALL_ROWS = 560

BIG = ["w_in", "w_uq", "w_ukv", "w_mem_kv", "w_o_gm", "w_o_mla", "w_o_mem", "w_out", "w_ff1", "w_ff2"]
BIG_SHAPE = {"w_in": (1024, 5312), "w_uq": (384, 1536), "w_ukv": (256, 2048), "w_mem_kv": (1024, 1024),
             "w_o_gm": (512, 1024), "w_o_mla": (1024, 1024), "w_o_mem": (512, 1024), "w_out": (1024, 1024),
             "w_ff1": (1024, 4096), "w_ff2": (4096, 1024)}
COL_SHARDED = {"w_in", "w_uq", "w_ukv", "w_o_gm", "w_o_mem", "w_ff1"}
SMALL = ["g_mix", "g_cq", "g_ckv", "g_q_nope", "g_q_pe", "g_k_nope", "g_k_pe", "g_gm_ln", "b_gm_ln",
         "w_spatial", "b_spatial", "g_mem", "g_mq", "g_mk", "g_ffn"]
WEIGHTS = ['g_mix', 'w_in', 'g_cq', 'w_uq', 'g_ckv', 'w_ukv', 'g_q_nope', 'g_q_pe', 'g_k_nope', 'g_k_pe',
           'g_gm_ln', 'b_gm_ln', 'w_spatial', 'b_spatial', 'g_mem', 'w_mem_kv', 'g_mq', 'g_mk', 'w_o_gm',
           'w_o_mla', 'w_o_mem', 'w_out', 'g_ffn', 'w_ff1', 'w_ff2']


def _params(sem=None):
    return pltpu.CompilerParams(vmem_limit_bytes=VMEM_LIMIT, dimension_semantics=sem)


def _pick(n, prefs):
    for p in prefs:
        if n % p == 0:
            return p
    return n


def _full(shape):
    nd = len(shape)
    return pl.BlockSpec(shape, lambda *_: (0,) * nd)


def _rows(t, w, blk=0):
    return pl.BlockSpec((t, w), lambda i: (i, blk))


def _acc(ref, val, first):
    @pl.when(first)
    def _():
        ref[...] = val

    @pl.when(jnp.logical_not(first))
    def _():
        ref[...] += val


def _dn(a, b, ca, cb):
    return lax.dot_general(a.astype(BF), b.astype(BF), (((ca,), (cb,)), ((), ())), preferred_element_type=F32)


@jax.custom_vjp
def _mm_nn(a, b):
    return _dn(a, b, 1, 0)


def _mm_nn_fwd(a, b):
    return _dn(a, b, 1, 0), (a.astype(BF), b.astype(BF))


def _mm_nn_bwd(res, ct):
    a, b = res
    return _dn(ct, b, 1, 1), _dn(a, ct, 0, 0)


_mm_nn.defvjp(_mm_nn_fwd, _mm_nn_bwd)


@jax.custom_vjp
def _mm_nt(a, b):
    return _dn(a, b, 1, 1)


def _mm_nt_fwd(a, b):
    return _dn(a, b, 1, 1), (a.astype(BF), b.astype(BF))


def _mm_nt_bwd(res, ct):
    a, b = res
    return _dn(ct, b, 1, 0), _dn(ct, a, 0, 0)


_mm_nt.defvjp(_mm_nt_fwd, _mm_nt_bwd)


def _rmsn(x, g, n):
    ms = jnp.sum(x * x, axis=-1, keepdims=True) * (1.0 / n)
    return x * lax.rsqrt(ms + EPS) * g


def _layernorm(x, g, b):
    mu = jnp.mean(x, axis=-1, keepdims=True)
    xc = x - mu
    y = xc * lax.rsqrt(jnp.mean(xc * xc, axis=-1, keepdims=True) + EPS)
    return y * g + b


def _rope(x, cos_f, sin_s, swap):
    xs = lax.dot_general(x, swap, (((1,), (0,)), ((), ())), precision=lax.Precision.HIGHEST,
                         preferred_element_type=F32)
    return x * cos_f + xs * sin_s


def _softmax(s):
    m = lax.stop_gradient(jnp.max(s, axis=-1, keepdims=True))
    p = jnp.exp(s - m)
    return p / jnp.sum(p, axis=-1, keepdims=True)


def _mm(a, b, *, ta=False, tb=False, add=None, out_dtype=F32, name):
    if ta:
        k_dim, m = a.shape
    else:
        m, k_dim = a.shape
    if tb:
        n, kb = b.shape
    else:
        kb, n = b.shape
    assert k_dim == kb, (a.shape, b.shape, ta, tb)
    tm = _pick(m, (1024, 512, 256, 128))
    tn = _pick(n, (1024, 768, 512, 384, 256, 128))
    tk = _pick(k_dim, (2048, 1024, 768, 512, 256, 128))
    nk = k_dim // tk
    ca = 0 if ta else 1
    cb = 1 if tb else 0
    has_add = add is not None

    def body(*refs):
        if has_add:
            a_ref, b_ref, add_ref, o_ref = refs[:4]
        else:
            a_ref, b_ref, o_ref = refs[:3]
            add_ref = None
        part = _dn(a_ref[...], b_ref[...], ca, cb)
        if nk == 1:
            if has_add:
                part = part + add_ref[...]
            o_ref[...] = part.astype(out_dtype)
            return
        acc = refs[-1]
        k = pl.program_id(2)
        _acc(acc, part, k == 0)

        @pl.when(k == nk - 1)
        def _():
            r = acc[...]
            if has_add:
                r = r + add_ref[...]
            o_ref[...] = r.astype(out_dtype)

    a_spec = pl.BlockSpec((tk, tm), lambda i, j, k: (k, i)) if ta else pl.BlockSpec((tm, tk), lambda i, j, k: (i, k))
    b_spec = pl.BlockSpec((tn, tk), lambda i, j, k: (j, k)) if tb else pl.BlockSpec((tk, tn), lambda i, j, k: (k, j))
    o_spec = pl.BlockSpec((tm, tn), lambda i, j, k: (i, j))
    in_specs = [a_spec, b_spec] + ([o_spec] if has_add else [])
    args = [a, b] + ([add] if has_add else [])
    return pl.pallas_call(
        body, grid=(m // tm, n // tn, nk), in_specs=in_specs, out_specs=o_spec,
        out_shape=SDS((m, n), out_dtype),
        scratch_shapes=[pltpu.VMEM((tm, tn), F32)] if nk > 1 else [],
        compiler_params=_params(("parallel", "parallel", "arbitrary")), name=name)(*args)


def _rms_fwd(x, g, name):
    n, w = x.shape
    t = min(ROW_TILE, n)

    def body(x_ref, g_ref, o_ref):
        o_ref[...] = _rmsn(x_ref[...], g_ref[...], w).astype(BF)

    return pl.pallas_call(body, grid=(n // t,), in_specs=[_rows(t, w), _full((1, w))], out_specs=_rows(t, w),
                          out_shape=SDS((n, w), BF), compiler_params=_params(("arbitrary",)), name=name)(x, g)


def _rms_bwd(x, g, dh, res, name):
    n, w = x.shape
    t = min(ROW_TILE, n)
    has_res = res is not None

    def body(*refs):
        if has_res:
            x_ref, g_ref, dh_ref, res_ref, dx_ref, dxb_ref, dg_ref = refs
        else:
            x_ref, g_ref, dh_ref, dx_ref, dxb_ref, dg_ref = refs
        _, vjp = jax.vjp(lambda xx, gg: _rmsn(xx, gg, w), x_ref[...], g_ref[...])
        dx, dg = vjp(dh_ref[...])
        if has_res:
            dx = dx + res_ref[...]
        dx_ref[...] = dx
        dxb_ref[...] = dx.astype(BF)
        _acc(dg_ref, dg, pl.program_id(0) == 0)

    in_specs = [_rows(t, w), _full((1, w)), _rows(t, w)] + ([_rows(t, w)] if has_res else [])
    args = [x, g, dh] + ([res] if has_res else [])
    return pl.pallas_call(body, grid=(n // t,), in_specs=in_specs,
                          out_specs=[_rows(t, w), _rows(t, w), _full((1, w))],
                          out_shape=[SDS((n, w), F32), SDS((n, w), BF), SDS((1, w), F32)],
                          compiler_params=_params(("arbitrary",)), name=name)(*args)


def _act_fwd(a, name):
    n, w = a.shape
    t = min(ROW_TILE, n)

    def body(a_ref, o_ref):
        r = jnp.maximum(a_ref[...], 0.0)
        o_ref[...] = (r * r).astype(BF)

    return pl.pallas_call(body, grid=(n // t,), in_specs=[_rows(t, w)], out_specs=_rows(t, w),
                          out_shape=SDS((n, w), BF), compiler_params=_params(("parallel",)), name=name)(a)


def _act_bwd(a, dr, name):
    n, w = a.shape
    t = min(ROW_TILE, n)

    def body(a_ref, dr_ref, o_ref):
        o_ref[...] = (dr_ref[...] * (2.0 * jnp.maximum(a_ref[...], 0.0))).astype(BF)

    return pl.pallas_call(body, grid=(n // t,), in_specs=[_rows(t, w), _rows(t, w)], out_specs=_rows(t, w),
                          out_shape=SDS((n, w), BF), compiler_params=_params(("parallel",)), name=name)(a, dr)


def _loss_call(y, tgt, name):
    n, w = y.shape
    t = min(ROW_TILE, n)

    def body(y_ref, t_ref, dy_ref, dyb_ref, l_ref):
        e = y_ref[...] - t_ref[...]
        dy = e * (1.0 / w)
        dy_ref[...] = dy
        dyb_ref[...] = dy.astype(BF)
        part = jnp.sum(jnp.sum(e * e, axis=-1, keepdims=True), axis=0, keepdims=True) * (0.5 / w)
        _acc(l_ref, jnp.broadcast_to(part, (8, LANES)), pl.program_id(0) == 0)

    return pl.pallas_call(body, grid=(n // t,), in_specs=[_rows(t, w), _rows(t, w)],
                          out_specs=[_rows(t, w), _rows(t, w), _full((8, LANES))],
                          out_shape=[SDS((n, w), F32), SDS((n, w), BF), SDS((8, LANES), F32)],
                          compiler_params=_params(("arbitrary",)), name=name)(y, tgt)


def _merge_core(zg0, zg1, zg2, y0, y1, y2):
    return jax.nn.sigmoid(zg0) * y0 + jax.nn.sigmoid(zg1) * y1 + jax.nn.sigmoid(zg2) * y2


def _merge_fwd(z, y_gm, y_mla, y_mem, name):
    n = z.shape[0]
    t = min(ROW_TILE, n)
    w = D_MODEL

    def body(g0, g1, g2, y0, y1, y2, o_ref):
        o_ref[...] = _merge_core(g0[...], g1[...], g2[...], y0[...], y1[...], y2[...]).astype(BF)

    return pl.pallas_call(body, grid=(n // t,),
                          in_specs=[_rows(t, w, 0), _rows(t, w, 1), _rows(t, w, 2)] + [_rows(t, w)] * 3,
                          out_specs=_rows(t, w), out_shape=SDS((n, w), BF),
                          compiler_params=_params(("parallel",)), name=name)(z, z, z, y_gm, y_mla, y_mem)


def _merge_bwd(z, y_gm, y_mla, y_mem, dmerged, name):
    n = z.shape[0]
    t = min(ROW_TILE, n)
    w = D_MODEL

    def body(g0, g1, g2, y0, y1, y2, dm, dzg_ref, d0_ref, d1_ref, d2_ref):
        _, vjp = jax.vjp(_merge_core, g0[...], g1[...], g2[...], y0[...], y1[...], y2[...])
        dg0, dg1, dg2, dy0, dy1, dy2 = vjp(dm[...])
        dzg_ref[:, 0:w] = dg0.astype(BF)
        dzg_ref[:, w:2 * w] = dg1.astype(BF)
        dzg_ref[:, 2 * w:3 * w] = dg2.astype(BF)
        d0_ref[...] = dy0.astype(BF)
        d1_ref[...] = dy1.astype(BF)
        d2_ref[...] = dy2.astype(BF)

    return pl.pallas_call(body, grid=(n // t,),
                          in_specs=[_rows(t, w, 0), _rows(t, w, 1), _rows(t, w, 2)] + [_rows(t, w)] * 4,
                          out_specs=[_rows(t, 3 * w)] + [_rows(t, w)] * 3,
                          out_shape=[SDS((n, 3 * w), BF)] + [SDS((n, w), BF)] * 3,
                          compiler_params=_params(("parallel",)), name=name)(z, z, z, y_gm, y_mla, y_mem, dmerged)


def _gm_core(zu, zv, g_ln, b_ln, ws, bcols):
    t = zu.shape[0]
    u = jax.nn.gelu(zu)
    v = _layernorm(jax.nn.gelu(zv), g_ln, b_ln)
    row = lax.broadcasted_iota(jnp.int32, (GM_CHUNK, GM_CHUNK), 0)
    col = lax.broadcasted_iota(jnp.int32, (GM_CHUNK, GM_CHUNK), 1)
    wc = [jnp.where(row >= col, ws[g], 0.0) for g in range(GM_GROUPS)]
    chunks = []
    for c in range(t // GM_CHUNK):
        cols = []
        for g in range(GM_GROUPS):
            vc = v[c * GM_CHUNK:(c + 1) * GM_CHUNK, g * LANES:(g + 1) * LANES]
            cols.append(_mm_nn(wc[g], vc) + bcols[g])
        chunks.append(jnp.concatenate(cols, axis=1))
    mixed = chunks[0] if len(chunks) == 1 else jnp.concatenate(chunks, axis=0)
    return u * mixed


def _gm_specs(t):
    return [_rows(t, GM_WIDTH, ZU // GM_WIDTH), _rows(t, GM_WIDTH, ZV // GM_WIDTH), _full((1, GM_WIDTH)),
            _full((1, GM_WIDTH)), _full((GM_GROUPS, GM_CHUNK, GM_CHUNK))] + [_full((GM_CHUNK, 1))] * GM_GROUPS


def _gm_fwd(z, g_ln, b_ln, ws, bcols, name):
    n = z.shape[0]
    t = min(ROW_TILE, n)

    def body(zu, zv, g_ref, b_ref, ws_ref, c0, c1, c2, c3, o_ref):
        out = _gm_core(zu[...], zv[...], g_ref[...], b_ref[...], [ws_ref[g] for g in range(GM_GROUPS)],
                       [c0[...], c1[...], c2[...], c3[...]])
        o_ref[...] = out.astype(BF)

    return pl.pallas_call(body, grid=(n // t,), in_specs=_gm_specs(t), out_specs=_rows(t, GM_WIDTH),
                          out_shape=SDS((n, GM_WIDTH), BF), compiler_params=_params(("parallel",)),
                          name=name)(z, z, g_ln, b_ln, ws, *bcols)


def _gm_bwd(z, g_ln, b_ln, ws, bcols, dgm, name):
    n = z.shape[0]
    t = min(ROW_TILE, n)

    def body(zu, zv, g_ref, b_ref, ws_ref, c0, c1, c2, c3, dgm_ref, dz_ref, dg_ref, db_ref, dws_ref, e0, e1, e2, e3):
        first = pl.program_id(0) == 0
        _, vjp = jax.vjp(_gm_core, zu[...], zv[...], g_ref[...], b_ref[...],
                         [ws_ref[g] for g in range(GM_GROUPS)], [c0[...], c1[...], c2[...], c3[...]])
        dzu, dzv, dg, db, dws, dcols = vjp(dgm_ref[...])
        dz_ref[:, 0:GM_WIDTH] = dzu.astype(BF)
        dz_ref[:, GM_WIDTH:2 * GM_WIDTH] = dzv.astype(BF)
        _acc(dg_ref, dg, first)
        _acc(db_ref, db, first)
        _acc(dws_ref, jnp.stack(dws, axis=0), first)
        for ref, val in zip((e0, e1, e2, e3), dcols):
            _acc(ref, val, first)

    return pl.pallas_call(
        body, grid=(n // t,), in_specs=_gm_specs(t) + [_rows(t, GM_WIDTH)],
        out_specs=[_rows(t, 2 * GM_WIDTH), _full((1, GM_WIDTH)), _full((1, GM_WIDTH)),
                   _full((GM_GROUPS, GM_CHUNK, GM_CHUNK))] + [_full((GM_CHUNK, 1))] * GM_GROUPS,
        out_shape=[SDS((n, 2 * GM_WIDTH), BF), SDS((1, GM_WIDTH), F32), SDS((1, GM_WIDTH), F32),
                   SDS((GM_GROUPS, GM_CHUNK, GM_CHUNK), F32)] + [SDS((GM_CHUNK, 1), F32)] * GM_GROUPS,
        compiler_params=_params(("arbitrary",)), name=name)(z, z, g_ln, b_ln, ws, *bcols, dgm)


def _rope_tables(pos_f, inv_full, cmask, smask, name):
    n = pos_f.shape[0]
    t = min(ROW_TILE, n)

    def body(p_ref, inv_ref, cm_ref, sm_ref, cos_ref, sin_ref):
        ang = p_ref[...] * inv_ref[...]
        cos_ref[...] = jnp.cos(ang) * cm_ref[...]
        sin_ref[...] = jnp.sin(ang) * sm_ref[...]

    return pl.pallas_call(body, grid=(n // t,), in_specs=[_rows(t, 1)] + [_full((1, LANES))] * 3,
                          out_specs=[_rows(t, LANES)] * 2, out_shape=[SDS((n, LANES), F32)] * 2,
                          compiler_params=_params(("parallel",)), name=name)(pos_f, inv_full, cmask, smask)


def _prep_core(cq, kpe, ckv, gains, wqn, wqp, wkn, wv, cos_f, sin_s, swap):
    g_cq, g_ckv, g_qn, g_qp, g_kn, g_kp = gains
    cqn = _rmsn(cq, g_cq, Q_LORA)
    ckvn = _rmsn(ckv, g_ckv, KV_LORA)
    kp = _rope(_rmsn(kpe, g_kp, MLA_ROPE), cos_f, sin_s, swap)
    qs, ks, vs = [], [], []
    for h in range(MLA_HEADS):
        qs.append(_rmsn(_mm_nn(cqn, wqn[h]), g_qn, MLA_NOPE))
        qs.append(_rope(_rmsn(_mm_nn(cqn, wqp[h]), g_qp, MLA_ROPE), cos_f, sin_s, swap))
        ks.append(_rmsn(_mm_nn(ckvn, wkn[h]), g_kn, MLA_NOPE))
        ks.append(kp)
        vs.append(_mm_nn(ckvn, wv[h]))
    return jnp.concatenate(qs, axis=1), jnp.concatenate(ks, axis=1), jnp.concatenate(vs, axis=1)


def _prep_in_specs(t):
    return ([_rows(t, Q_LORA, CQ // Q_LORA), _rows(t, LANES, KPE // LANES), _rows(t, KV_LORA, CKV // KV_LORA),
             _rows(t, LANES), _rows(t, LANES), _full((1, Q_LORA)), _full((1, KV_LORA))] + [_full((1, LANES))] * 4
            + [_full((Q_LORA, 2048)), _full((KV_LORA, 2048)), _full((LANES, LANES))])


def _prep_load(refs):
    cq, kpe, ckv, cos_f, sin_s, g_cq, g_ckv, g_qn, g_qp, g_kn, g_kp, wq, wkv, swap = refs
    hs = range(MLA_HEADS)
    wqn = [wq[:, h * LANES:(h + 1) * LANES].astype(F32) for h in hs]
    wqp = [wq[:, 1024 + h * LANES:1024 + (h + 1) * LANES].astype(F32) for h in hs]
    wkn = [wkv[:, h * LANES:(h + 1) * LANES].astype(F32) for h in hs]
    wv = [wkv[:, 1024 + h * LANES:1024 + (h + 1) * LANES].astype(F32) for h in hs]
    gains = [g_cq[...], g_ckv[...], g_qn[...], g_qp[...], g_kn[...], g_kp[...]]
    return (cq[...], kpe[...], ckv[...], gains, wqn, wqp, wkn, wv), (cos_f[...], sin_s[...], swap[...])


def _prep_fwd(z, cos_f, sin_s, gains, wq, wkv, swap, name):
    n = z.shape[0]
    t = min(ROW_TILE, n)

    def body(*refs):
        diff, const = _prep_load(refs[:14])
        q, k, v = _prep_core(*diff, *const)
        q_ref, k_ref, v_ref = refs[14:]
        q_ref[...] = q.astype(BF)
        k_ref[...] = k.astype(BF)
        v_ref[...] = v.astype(BF)

    return pl.pallas_call(body, grid=(n // t,), in_specs=_prep_in_specs(t),
                          out_specs=[_rows(t, 2048), _rows(t, 2048), _rows(t, 1024)],
                          out_shape=[SDS((n, 2048), BF), SDS((n, 2048), BF), SDS((n, 1024), BF)],
                          compiler_params=_params(("parallel",)),
                          name=name)(z, z, z, cos_f, sin_s, *gains, wq, wkv, swap)


def _prep_bwd(z, cos_f, sin_s, gains, wq, wkv, swap, dq, dk, dv, name):
    n = z.shape[0]
    t = min(ROW_TILE, n)
    wz = Q_LORA + LANES + KV_LORA

    def body(*refs):
        diff, const = _prep_load(refs[:14])
        dq_ref, dk_ref, dv_ref = refs[14:17]
        dz_ref, o_cq, o_ckv, o_qn, o_qp, o_kn, o_kp, dwq_ref, dwkv_ref = refs[17:]
        first = pl.program_id(0) == 0
        _, vjp = jax.vjp(lambda *d: _prep_core(*d, *const), *diff)
        dcq, dkpe, dckv, dgains, dwqn, dwqp, dwkn, dwv = vjp((dq_ref[...], dk_ref[...], dv_ref[...]))
        dz_ref[:, 0:Q_LORA] = dcq.astype(BF)
        dz_ref[:, Q_LORA:Q_LORA + LANES] = dkpe.astype(BF)
        dz_ref[:, Q_LORA + LANES:wz] = dckv.astype(BF)
        for ref, val in zip((o_cq, o_ckv, o_qn, o_qp, o_kn, o_kp), dgains):
            _acc(ref, val, first)
        _acc(dwq_ref, jnp.concatenate(dwqn + dwqp, axis=1), first)
        _acc(dwkv_ref, jnp.concatenate(dwkn + dwv, axis=1), first)

    gain_specs = [_full((1, Q_LORA)), _full((1, KV_LORA))] + [_full((1, LANES))] * 4
    gain_shapes = [SDS((1, Q_LORA), F32), SDS((1, KV_LORA), F32)] + [SDS((1, LANES), F32)] * 4
    return pl.pallas_call(
        body, grid=(n // t,), in_specs=_prep_in_specs(t) + [_rows(t, 2048), _rows(t, 2048), _rows(t, 1024)],
        out_specs=[_rows(t, wz)] + gain_specs + [_full((Q_LORA, 2048)), _full((KV_LORA, 2048))],
        out_shape=[SDS((n, wz), BF)] + gain_shapes + [SDS((Q_LORA, 2048), F32), SDS((KV_LORA, 2048), F32)],
        compiler_params=_params(("arbitrary",)),
        name=name)(z, z, z, cos_f, sin_s, *gains, wq, wkv, swap, dq, dk, dv)


MLA_QK = 256
MLA_SCALE = 1.0 / math.sqrt(MLA_NOPE + MLA_ROPE)


def _causal_mask(s, q0, k0):
    tq, tk = s.shape
    row = q0 + lax.broadcasted_iota(jnp.int32, (tq, tk), 0)
    col = k0 + lax.broadcasted_iota(jnp.int32, (tq, tk), 1)
    return jnp.where(row >= col, s, -jnp.inf)


def _mla_fwd(q, k, v, batch, seq, name):
    n = q.shape[0]
    tq = min(ATT_TILE, seq)
    nq = seq // tq

    def body(q_ref, k_ref, v_ref, o_ref, lse_ref):
        i = pl.program_id(2)
        qb = q_ref[...]

        def step(j, carry):
            m, l, acc = carry
            k0 = pl.multiple_of(j * tq, tq)
            kb = k_ref[pl.ds(k0, tq), :]
            vb = v_ref[pl.ds(k0, tq), :]
            s = _causal_mask(_dn(qb, kb, 1, 1) * MLA_SCALE, i * tq, k0)
            m_new = jnp.maximum(m, jnp.max(s, axis=-1, keepdims=True))
            p = jnp.exp(s - m_new)
            alpha = jnp.exp(m - m_new)
            l = alpha * l + jnp.sum(p, axis=-1, keepdims=True)
            acc = alpha * acc + _dn(p, vb, 1, 0)
            return m_new, l, acc

        init = (jnp.full((tq, 1), -jnp.inf, F32), jnp.zeros((tq, 1), F32), jnp.zeros((tq, MLA_V), F32))
        m, l, acc = lax.fori_loop(0, i + 1, step, init)
        o_ref[...] = acc / l
        lse_ref[...] = jnp.broadcast_to(m + jnp.log(l), (tq, LANES))

    return pl.pallas_call(
        body, grid=(batch, MLA_HEADS, nq),
        in_specs=[pl.BlockSpec((tq, MLA_QK), lambda b, h, i: (b * nq + i, h)),
                  pl.BlockSpec((seq, MLA_QK), lambda b, h, i: (b, h)),
                  pl.BlockSpec((seq, MLA_V), lambda b, h, i: (b, h))],
        out_specs=[pl.BlockSpec((tq, MLA_V), lambda b, h, i: (b * nq + i, h)),
                   pl.BlockSpec((tq, LANES), lambda b, h, i: (b * nq + i, h))],
        out_shape=[SDS((n, MLA_HEADS * MLA_V), F32), SDS((n, MLA_HEADS * LANES), F32)],
        compiler_params=_params(("parallel", "parallel", "arbitrary")), name=name)(q, k, v)


def _mla_bwd(q, k, v, o, lse, do, batch, seq, name):
    n = q.shape[0]
    tk = min(ATT_TILE, seq)
    nk = seq // tk

    def body(q_ref, k_ref, v_ref, o_ref, lse_ref, do_ref, dq_ref, dk_ref, dv_ref):
        jk = pl.program_id(2)
        kb = k_ref[...]
        vb = v_ref[...]

        @pl.when(jk == 0)
        def _():
            dq_ref[...] = jnp.zeros_like(dq_ref)

        def step(i, carry):
            dk_acc, dv_acc = carry
            q0 = pl.multiple_of(i * tk, tk)
            rows = pl.ds(q0, tk)
            qb = q_ref[rows, :]
            dob = do_ref[rows, :]
            delta = jnp.sum(dob * o_ref[rows, :], axis=-1, keepdims=True)
            s = _causal_mask(_dn(qb, kb, 1, 1) * MLA_SCALE, q0, jk * tk)
            p = jnp.exp(s - lse_ref[rows, :][:, 0:1])
            dv_acc = dv_acc + _dn(p, dob, 0, 0)
            dp = _dn(dob, vb, 1, 1)
            ds = p * (dp - delta) * MLA_SCALE
            dk_acc = dk_acc + _dn(ds, qb, 0, 0)
            dq_ref[rows, :] += _dn(ds, kb, 1, 0)
            return dk_acc, dv_acc

        dk_acc, dv_acc = lax.fori_loop(jk, nk, step, (jnp.zeros((tk, MLA_QK), F32), jnp.zeros((tk, MLA_V), F32)))
        dk_ref[...] = dk_acc
        dv_ref[...] = dv_acc

    full_qk = pl.BlockSpec((seq, MLA_QK), lambda b, h, j: (b, h))
    full_v = pl.BlockSpec((seq, MLA_V), lambda b, h, j: (b, h))
    blk_qk = pl.BlockSpec((tk, MLA_QK), lambda b, h, j: (b * nk + j, h))
    blk_v = pl.BlockSpec((tk, MLA_V), lambda b, h, j: (b * nk + j, h))
    return pl.pallas_call(
        body, grid=(batch, MLA_HEADS, nk),
        in_specs=[full_qk, blk_qk, blk_v, full_v, full_v, full_v],
        out_specs=[full_qk, blk_qk, blk_v],
        out_shape=[SDS((n, MLA_HEADS * MLA_QK), F32), SDS((n, MLA_HEADS * MLA_QK), F32),
                   SDS((n, MLA_HEADS * MLA_V), F32)],
        compiler_params=_params(("parallel", "parallel", "arbitrary")), name=name)(q, k, v, o, lse, do)


MEM_SCALE = 1.0 / math.sqrt(HEAD_DIM)
MEM_W = MEM_HEADS * HEAD_DIM


def _mem_core(qs, ks, vs, g_mq, g_mk):
    outs = []
    for h in range(MEM_HEADS):
        qh = _rmsn(qs[h], g_mq, HEAD_DIM)
        kh = _rmsn(ks[h], g_mk, HEAD_DIM)
        p = _softmax(_mm_nt(qh, kh) * MEM_SCALE)
        outs.append(_mm_nn(p, vs[h]))
    return jnp.concatenate(outs, axis=1)


def _mem_load(qm, kvm, g_mq, g_mk):
    hs = range(MEM_HEADS)
    qs = [qm[:, h * LANES:(h + 1) * LANES] for h in hs]
    ks = [kvm[:, h * LANES:(h + 1) * LANES] for h in hs]
    vs = [kvm[:, MEM_W + h * LANES:MEM_W + (h + 1) * LANES] for h in hs]
    return qs, ks, vs, g_mq[...], g_mk[...]


def _mem_fwd(z, kvm, g_mq, g_mk, batch, seq, name):
    n = z.shape[0]
    t = min(ROW_TILE, seq)
    per = seq // t

    def body(qm, kvm_ref, gq, gk, o_ref):
        o_ref[...] = _mem_core(*_mem_load(qm, kvm_ref, gq, gk)).astype(BF)

    return pl.pallas_call(
        body, grid=(n // t,),
        in_specs=[_rows(t, MEM_W, QM // MEM_W), pl.BlockSpec((MEM_LEN, 2 * MEM_W), lambda i: (i // per, 0)),
                  _full((1, LANES)), _full((1, LANES))],
        out_specs=_rows(t, MEM_W), out_shape=SDS((n, MEM_W), BF),
        compiler_params=_params(("parallel",)), name=name)(z, kvm, g_mq, g_mk)


def _mem_bwd(z, kvm, g_mq, g_mk, dom, batch, seq, name):
    n = z.shape[0]
    t = min(ROW_TILE, seq)
    per = seq // t

    def body(qm, kvm_ref, gq, gk, dom_ref, dz_ref, dkvm_ref, dgq_ref, dgk_ref):
        i = pl.program_id(0)
        _, vjp = jax.vjp(_mem_core, *_mem_load(qm, kvm_ref, gq, gk))
        dqs, dks, dvs, dgq, dgk = vjp(dom_ref[...])
        dz_ref[...] = jnp.concatenate(dqs, axis=1).astype(BF)
        _acc(dkvm_ref, jnp.concatenate(dks + dvs, axis=1), i % per == 0)
        _acc(dgq_ref, dgq, i == 0)
        _acc(dgk_ref, dgk, i == 0)

    kv_spec = pl.BlockSpec((MEM_LEN, 2 * MEM_W), lambda i: (i // per, 0))
    return pl.pallas_call(
        body, grid=(n // t,),
        in_specs=[_rows(t, MEM_W, QM // MEM_W), kv_spec, _full((1, LANES)), _full((1, LANES)), _rows(t, MEM_W)],
        out_specs=[_rows(t, MEM_W), kv_spec, _full((1, LANES)), _full((1, LANES))],
        out_shape=[SDS((n, MEM_W), BF), SDS((batch * MEM_LEN, 2 * MEM_W), F32), SDS((1, LANES), F32),
                   SDS((1, LANES), F32)],
        compiler_params=_params(("arbitrary",)), name=name)(z, kvm, g_mq, g_mk, dom)


ANY = pl.BlockSpec(memory_space=pl.ANY)


def _me():
    return lax.axis_index("x"), lax.axis_index("y"), lax.axis_index("c")


def _other_chips(x, y):
    return [(1 - x, y), (x, 1 - y), (1 - x, 1 - y)]


def _gather_shards(shard, name):
    n_far = 3 * GATHER_PIECES

    def body(s_ref, o_ref, send, recv, local):
        x, y, c = _me()
        k = 2 * x + y
        sib = (x, y, 1 - c)
        chips = _other_chips(x, y)

        def piece(ref, core, p):
            return ref.at[pl.ds(core * HALF_ROWS + p * GATHER_ROWS, GATHER_ROWS)]

        def landed(chip, core, p):
            return piece(o_ref.at[2 * chip[0] + chip[1]], core, p)

        def copy(sem, src, dst, to):
            return pltpu.make_async_remote_copy(src_ref=src, dst_ref=dst, send_sem=send.at[sem],
                                                recv_sem=recv.at[sem], device_id=to, device_id_type=MESH)

        mine = [pltpu.make_async_copy(piece(s_ref, q // GATHER_PIECES, q % GATHER_PIECES),
                                      piece(o_ref.at[k], q // GATHER_PIECES, q % GATHER_PIECES), local.at[q])
                for q in range(2 * GATHER_PIECES)]
        for cp in mine:
            cp.start()
        first, passed = [], []
        for j, chip in enumerate(chips):
            for p in range(GATHER_PIECES):
                s = j * GATHER_PIECES + p
                first.append(copy(s, piece(s_ref, c, p), landed((x, y), c, p), (*chip, c)))
                passed.append(copy(n_far + s, landed(chip, c, p), landed(chip, c, p), sib))
        for cp in first:
            cp.start()
        for j, chip in enumerate(chips):
            for p in range(GATHER_PIECES):
                s = j * GATHER_PIECES + p
                copy(s, piece(s_ref, c, p), landed(chip, c, p), (*chip, c)).wait_recv()
                passed[s].start()
        for j, chip in enumerate(chips):
            for p in range(GATHER_PIECES):
                s = j * GATHER_PIECES + p
                copy(n_far + s, landed(chip, 1 - c, p), landed(chip, 1 - c, p), sib).wait_recv()
        for cp in first + passed:
            cp.wait_send()
        for cp in mine:
            cp.wait()

    return pl.pallas_call(
        body, in_specs=[ANY], out_specs=ANY, out_shape=SDS((N_CHIPS, SHARD_ROWS, D_MODEL), BF),
        scratch_shapes=[pltpu.SemaphoreType.DMA((2 * n_far,)), pltpu.SemaphoreType.DMA((2 * n_far,)),
                        pltpu.SemaphoreType.DMA((2 * GATHER_PIECES,))],
        name=name)(shard)


def _pair_exchange(g, name):
    n = N_CHIPS * PIECES

    def body(g_ref, o_ref, send, recv):
        x, y, c = _me()
        copies = []
        for k in range(N_CHIPS):
            for p in range(PIECES):
                s = k * PIECES + p
                copies.append(pltpu.make_async_remote_copy(
                    src_ref=g_ref.at[k, pl.ds((1 - c) * HALF_ROWS + p * PIECE_ROWS, PIECE_ROWS)],
                    dst_ref=o_ref.at[k, pl.ds(p * PIECE_ROWS, PIECE_ROWS)], send_sem=send.at[s], recv_sem=recv.at[s],
                    device_id=(x, y, 1 - c), device_id_type=MESH))
        for cp in copies:
            cp.start()
        for cp in copies:
            cp.wait()

    return pl.pallas_call(body, in_specs=[ANY], out_specs=ANY, out_shape=SDS((N_CHIPS, HALF_ROWS, D_MODEL), F32),
                          scratch_shapes=[pltpu.SemaphoreType.DMA((n,)), pltpu.SemaphoreType.DMA((n,))],
                          name=name)(g)


def _pair_add(ck, g, theirs, name):
    def body(ck_ref, g_ref, t_ref, p32_ref, pbf_ref):
        s = g_ref[...] + t_ref[...]
        p32_ref[...] = s
        pbf_ref[...] = s.astype(BF)

    spec = pltpu.PrefetchScalarGridSpec(
        num_scalar_prefetch=1, grid=(N_CHIPS, PIECES),
        in_specs=[pl.BlockSpec((None, PIECE_ROWS, D_MODEL), lambda k, p, ck: (k, ck[0] * PIECES + p, 0)),
                  pl.BlockSpec((None, PIECE_ROWS, D_MODEL), lambda k, p, ck: (k, p, 0))],
        out_specs=[pl.BlockSpec((None, PIECE_ROWS, D_MODEL), lambda k, p, ck: (k, p, 0)),
                   pl.BlockSpec((None, None, PIECE_ROWS, D_MODEL), lambda k, p, ck: (k, p, 0, 0))])
    return pl.pallas_call(
        body, grid_spec=spec,
        out_shape=[SDS((N_CHIPS, HALF_ROWS, D_MODEL), F32), SDS((N_CHIPS, PIECES, PIECE_ROWS, D_MODEL), BF)],
        compiler_params=_params(("arbitrary", "arbitrary")), name=name)(ck, g, theirs)


def _scatter_partials(pbf, name):
    n = 3 * PIECES

    def body(p_ref, o_ref, send, recv):
        x, y, c = _me()
        copies = []
        for j, chip in enumerate(_other_chips(x, y)):
            for p in range(PIECES):
                s = j * PIECES + p
                copies.append(pltpu.make_async_remote_copy(
                    src_ref=p_ref.at[2 * chip[0] + chip[1], p], dst_ref=o_ref.at[j, p], send_sem=send.at[s],
                    recv_sem=recv.at[s], device_id=(*chip, c), device_id_type=MESH))
        for cp in copies:
            cp.start()
        for cp in copies:
            cp.wait()

    return pl.pallas_call(body, in_specs=[ANY], out_specs=ANY, out_shape=SDS((3, PIECES, PIECE_ROWS, D_MODEL), BF),
                          scratch_shapes=[pltpu.SemaphoreType.DMA((n,)), pltpu.SemaphoreType.DMA((n,))],
                          name=name)(pbf)


def _sum_chips(ck, p32, slots, name):
    def body(ck_ref, p_ref, s_ref, o_ref):
        o_ref[...] = ((p_ref[...] + s_ref[0].astype(F32)) + s_ref[1].astype(F32)) + s_ref[2].astype(F32)

    spec = pltpu.PrefetchScalarGridSpec(
        num_scalar_prefetch=1, grid=(PIECES,),
        in_specs=[pl.BlockSpec((None, PIECE_ROWS, D_MODEL), lambda p, ck: (ck[1], p, 0)),
                  pl.BlockSpec((3, None, PIECE_ROWS, D_MODEL), lambda p, ck: (0, p, 0, 0))],
        out_specs=pl.BlockSpec((PIECE_ROWS, D_MODEL), lambda p, ck: (ck[0] * PIECES + p, 0)))
    return pl.pallas_call(body, grid_spec=spec, out_shape=SDS((SHARD_ROWS, D_MODEL), F32),
                          compiler_params=_params(("arbitrary",)), name=name)(ck, p32, slots)


def _join_halves(r, name):
    def body(r_ref, o_ref, send, recv):
        x, y, c = _me()
        copies = []
        for p in range(PIECES):
            rows = pl.ds(c * HALF_ROWS + p * PIECE_ROWS, PIECE_ROWS)
            copies.append(pltpu.make_async_remote_copy(
                src_ref=r_ref.at[rows], dst_ref=o_ref.at[rows], send_sem=send.at[p], recv_sem=recv.at[p],
                device_id=(x, y, 1 - c), device_id_type=MESH))
        for cp in copies:
            cp.start()
        for cp in copies:
            cp.wait()

    return pl.pallas_call(body, in_specs=[ANY], out_specs=ANY, out_shape=SDS((SHARD_ROWS, D_MODEL), F32),
                          input_output_aliases={0: 0},
                          scratch_shapes=[pltpu.SemaphoreType.DMA((PIECES,)), pltpu.SemaphoreType.DMA((PIECES,))],
                          name=name)(r)


def _gather_small(s, name):
    def body(s_ref, o_ref, send, recv, local):
        x, y, c = _me()
        me = 4 * x + 2 * y + c
        keep = pltpu.make_async_copy(s_ref, o_ref.at[me], local)
        keep.start()
        sends = []
        for r in range(1, 8):
            fx, fy, fc = (r >> 2) & 1, (r >> 1) & 1, r & 1
            to = (x ^ fx, y ^ fy, c ^ fc)
            sends.append(pltpu.make_async_remote_copy(
                src_ref=s_ref, dst_ref=o_ref.at[me], send_sem=send.at[r - 1], recv_sem=recv.at[r - 1],
                device_id=to, device_id_type=MESH))
        for cp in sends:
            cp.start()
        for r in range(1, 8):
            fx, fy, fc = (r >> 2) & 1, (r >> 1) & 1, r & 1
            src = 4 * (x ^ fx) + 2 * (y ^ fy) + (c ^ fc)
            pltpu.make_async_remote_copy(
                src_ref=s_ref, dst_ref=o_ref.at[src], send_sem=send.at[r - 1], recv_sem=recv.at[r - 1],
                device_id=(x ^ fx, y ^ fy, c ^ fc), device_id_type=MESH).wait_recv()
        for cp in sends:
            cp.wait_send()
        keep.wait()

    return pl.pallas_call(
        body, in_specs=[ANY], out_specs=ANY, out_shape=SDS((8, SMALL_ROWS, LANES), F32),
        scratch_shapes=[pltpu.SemaphoreType.DMA((7,)), pltpu.SemaphoreType.DMA((7,)), pltpu.SemaphoreType.DMA],
        name=name)(s)


def _sum_slots(s, name):
    k, rows, w = s.shape
    t = _pick(rows, (FLAT_TILE, 80, 8))

    def body(s_ref, o_ref):
        acc = s_ref[0]
        for j in range(1, k):
            acc = acc + s_ref[j]
        o_ref[...] = acc

    return pl.pallas_call(body, grid=(rows // t,), in_specs=[pl.BlockSpec((k, t, w), lambda i: (0, i, 0))],
                          out_specs=_rows(t, w), out_shape=SDS((rows, w), F32),
                          compiler_params=_params(("parallel",)), name=name)(s)


def _adamw(w, g, m, v, name):
    rows, width = w.shape
    t = _pick(rows, (FLAT_TILE, 80, 8))

    def body(w_ref, g_ref, m_ref, v_ref, d_ref, nm_ref, nv_ref):
        g_ = g_ref[...]
        nm = ADAM_B1 * m_ref[...] + (1.0 - ADAM_B1) * g_
        nv = ADAM_B2 * v_ref[...] + (1.0 - ADAM_B2) * (g_ * g_)
        m_hat = nm / (1.0 - ADAM_B1 ** ADAM_STEP)
        v_hat = nv / (1.0 - ADAM_B2 ** ADAM_STEP)
        d_ref[...] = -ADAM_LR * (m_hat / (jnp.sqrt(v_hat) + ADAM_EPS) + ADAM_WD * w_ref[...])
        nm_ref[...] = nm
        nv_ref[...] = nv

    return pl.pallas_call(body, grid=(rows // t,), in_specs=[_rows(t, width)] * 4, out_specs=[_rows(t, width)] * 3,
                          out_shape=[SDS((rows, width), F32)] * 3, compiler_params=_params(("parallel",)),
                          name=name)(w, g, m, v)


def _pack_shard(ws):
    return jnp.concatenate([ws[n].reshape(-1, D_MODEL) for n in BIG], axis=0)


def _shard_shape(name):
    r, c = BIG_SHAPE[name]
    return (r, c // N_CHIPS) if name in COL_SHARDED else (r // N_CHIPS, c)


def _unpack_shard(flat):
    out, r0 = {}, 0
    for n in BIG:
        shp = _shard_shape(n)
        rows = shp[0] * shp[1] // D_MODEL
        out[n] = flat[r0:r0 + rows].reshape(shp)
        r0 += rows
    return out


def _unpack_gathered(allw):
    per_chip = [_unpack_shard(allw[k]) for k in range(N_CHIPS)]
    return {n: jnp.concatenate([per_chip[k][n] for k in range(N_CHIPS)], axis=1 if n in COL_SHARDED else 0)
            for n in BIG}


def _pack_full_grads(gs):
    chips = []
    for k in range(N_CHIPS):
        parts = {}
        for n in BIG:
            r, c = _shard_shape(n)
            parts[n] = gs[n][:, k * c:(k + 1) * c] if n in COL_SHARDED else gs[n][k * r:(k + 1) * r]
        chips.append(_pack_shard(parts))
    return jnp.stack(chips, axis=0)


def _win_layout(w_in):
    pad = jnp.zeros((w_in.shape[0], LANES - MLA_ROPE), w_in.dtype)
    u, v, cq = w_in[:, 0:512], w_in[:, 512:1024], w_in[:, 1024:1408]
    ckv, kpe, qm, zg = w_in[:, 1408:1664], w_in[:, 1664:1728], w_in[:, 1728:2240], w_in[:, 2240:5312]
    return jnp.concatenate([zg, u, v, qm, cq, kpe, pad, ckv], axis=1)


def _win_unlayout(g):
    zg, u, v, qm = g[:, ZG:ZG + 3072], g[:, ZU:ZU + 512], g[:, ZV:ZV + 512], g[:, QM:QM + 512]
    cq, kpe, ckv = g[:, CQ:CQ + 384], g[:, KPE:KPE + MLA_ROPE], g[:, CKV:CKV + 256]
    return jnp.concatenate([u, v, cq, ckv, kpe, qm, zg], axis=1)


def _wq_layout(w_uq):
    w = w_uq.reshape(Q_LORA, MLA_HEADS, MLA_NOPE + MLA_ROPE)
    nope = w[:, :, :MLA_NOPE].reshape(Q_LORA, MLA_HEADS * MLA_NOPE)
    pe = jnp.pad(w[:, :, MLA_NOPE:], ((0, 0), (0, 0), (0, LANES - MLA_ROPE))).reshape(Q_LORA, MLA_HEADS * LANES)
    return jnp.concatenate([nope, pe], axis=1)


def _wq_unlayout(g):
    nope = g[:, :1024].reshape(Q_LORA, MLA_HEADS, MLA_NOPE)
    pe = g[:, 1024:].reshape(Q_LORA, MLA_HEADS, LANES)[:, :, :MLA_ROPE]
    return jnp.concatenate([nope, pe], axis=2).reshape(Q_LORA, MLA_HEADS * (MLA_NOPE + MLA_ROPE))


def _wkv_layout(w_ukv):
    w = w_ukv.reshape(KV_LORA, MLA_HEADS, MLA_NOPE + MLA_V)
    return jnp.concatenate([w[:, :, :MLA_NOPE].reshape(KV_LORA, 1024), w[:, :, MLA_NOPE:].reshape(KV_LORA, 1024)],
                           axis=1)


def _wkv_unlayout(g):
    kn = g[:, :1024].reshape(KV_LORA, MLA_HEADS, MLA_NOPE)
    v = g[:, 1024:].reshape(KV_LORA, MLA_HEADS, MLA_V)
    return jnp.concatenate([kn, v], axis=2).reshape(KV_LORA, MLA_HEADS * (MLA_NOPE + MLA_V))


def _small_rows(name, a):
    a = a.reshape(-1)
    if a.shape[0] % LANES:
        a = jnp.pad(a, (0, LANES - a.shape[0] % LANES))
    return a.reshape(-1, LANES)


def _pack_small(d):
    rows = jnp.concatenate([_small_rows(n, d[n]) for n in SMALL], axis=0)
    return jnp.pad(rows, ((0, SMALL_ROWS - rows.shape[0]), (0, 0)))


SMALL_SHAPE = {"g_mix": (1, 1024), "g_cq": (1, 384), "g_ckv": (1, 256), "g_q_nope": (1, 128), "g_q_pe": (1, 64),
               "g_k_nope": (1, 128), "g_k_pe": (1, 64), "g_gm_ln": (1, 512), "b_gm_ln": (1, 512),
               "w_spatial": (1, 4, 128, 128), "b_spatial": (1, 4, 128), "g_mem": (1, 1024), "g_mq": (1, 128),
               "g_mk": (1, 128), "g_ffn": (1, 1024)}


def _unpack_small(rows):
    out, r0 = {}, 0
    for n in SMALL:
        size = int(np.prod(SMALL_SHAPE[n]))
        nr = -(-size // LANES)
        out[n] = rows[r0:r0 + nr].reshape(-1)[:size].reshape(SMALL_SHAPE[n])
        r0 += nr
    return out


def _pad_lanes(g):
    return jnp.pad(g, ((0, 0), (0, LANES - g.shape[1])))


def kernel(x, mem, positions, g_mix, w_in, g_cq, w_uq, g_ckv, w_ukv, g_q_nope, g_q_pe, g_k_nope, g_k_pe, g_gm_ln, b_gm_ln, w_spatial, b_spatial, g_mem, w_mem_kv, g_mq, g_mk, w_o_gm, w_o_mla, w_o_mem, w_out, g_ffn, w_ff1, w_ff2, loss_target, m_g_mix, m_w_in, m_g_cq, m_w_uq, m_g_ckv, m_w_ukv, m_g_q_nope, m_g_q_pe, m_g_k_nope, m_g_k_pe, m_g_gm_ln, m_b_gm_ln, m_w_spatial, m_b_spatial, m_g_mem, m_w_mem_kv, m_g_mq, m_g_mk, m_w_o_gm, m_w_o_mla, m_w_o_mem, m_w_out, m_g_ffn, m_w_ff1, m_w_ff2, v_g_mix, v_w_in, v_g_cq, v_w_uq, v_g_ckv, v_w_ukv, v_g_q_nope, v_g_q_pe, v_g_k_nope, v_g_k_pe, v_g_gm_ln, v_b_gm_ln, v_w_spatial, v_b_spatial, v_g_mem, v_w_mem_kv, v_g_mq, v_g_mk, v_w_o_gm, v_w_o_mla, v_w_o_mem, v_w_out, v_g_ffn, v_w_ff1, v_w_ff2):
    given = dict(locals())
    wts = {n: given[n] for n in WEIGHTS}
    mom = {n: given["m_" + n] for n in WEIGHTS}
    var = {n: given["v_" + n] for n in WEIGHTS}
    batch, seq, _ = x.shape
    n_tok = batch * seq

    shard_bf = _pack_shard({n: wts[n][0].astype(BF) for n in BIG})
    full = _unpack_gathered(_gather_shards(shard_bf, "gather_weights"))
    win = _win_layout(full["w_in"])
    wq = _wq_layout(full["w_uq"])
    wkv = _wkv_layout(full["w_ukv"])

    x2 = x.reshape(n_tok, D_MODEL)
    tgt2 = loss_target.reshape(n_tok, D_MODEL)
    mem2 = mem.reshape(batch * MEM_LEN, D_MODEL)
    pos_f = positions.reshape(n_tok, 1).astype(F32)

    inv = ROPE_BASE ** (-jnp.arange(0, MLA_ROPE, 2, dtype=F32) / MLA_ROPE)
    zeros64 = jnp.zeros((LANES - MLA_ROPE,), F32)
    inv_full = jnp.concatenate([inv, inv, zeros64]).reshape(1, LANES)
    half = MLA_ROPE // 2
    cmask = jnp.concatenate([jnp.ones((MLA_ROPE,), F32), zeros64]).reshape(1, LANES)
    smask = jnp.concatenate([-jnp.ones((half,), F32), jnp.ones((half,), F32), zeros64]).reshape(1, LANES)
    swap_np = np.zeros((LANES, LANES), np.float32)
    for j in range(half):
        swap_np[j + half, j] = 1.0
        swap_np[j, j + half] = 1.0
    swap = jnp.asarray(swap_np)

    prep_gains = [g_cq, g_ckv, g_q_nope, _pad_lanes(g_q_pe), g_k_nope, _pad_lanes(g_k_pe)]
    ws = w_spatial[0]
    bcols = [b_spatial[0, g].reshape(GM_CHUNK, 1) for g in range(GM_GROUPS)]

    h1 = _rms_fwd(x2, g_mix, "rms_mix")
    z = _mm(h1, win, name="mm_in")
    gm = _gm_fwd(z, g_gm_ln, b_gm_ln, ws, bcols, "gm_fwd")
    cos_f, sin_s = _rope_tables(pos_f, inv_full, cmask, smask, "rope_tables")
    qc, kc, vc = _prep_fwd(z, cos_f, sin_s, prep_gains, wq, wkv, swap, "prep_fwd")
    o_mla, lse = _mla_fwd(qc, kc, vc, batch, seq, "mla_fwd")
    memn = _rms_fwd(mem2, g_mem, "rms_mem")
    kvm = _mm(memn, full["w_mem_kv"], name="mm_memkv")
    o_mem = _mem_fwd(z, kvm, g_mq, g_mk, batch, seq, "mem_fwd")
    y_gm = _mm(gm, full["w_o_gm"], name="mm_o_gm")
    y_mla = _mm(o_mla, full["w_o_mla"], name="mm_o_mla")
    y_mem = _mm(o_mem, full["w_o_mem"], name="mm_o_mem")
    merged = _merge_fwd(z, y_gm, y_mla, y_mem, "merge_fwd")
    x1 = _mm(merged, full["w_out"], add=x2, name="mm_out")
    h2 = _rms_fwd(x1, g_ffn, "rms_ffn")
    a_ff = _mm(h2, full["w_ff1"], name="mm_ff1")
    r_ff = _act_fwd(a_ff, "act_fwd")
    y = _mm(r_ff, full["w_ff2"], add=x1, name="mm_ff2")
    dy, dyb, loss_tile = _loss_call(y, tgt2, "loss")

    gw = {}
    dr = _mm(dyb, full["w_ff2"], tb=True, name="mm_d_r")
    gw["w_ff2"] = _mm(r_ff, dyb, ta=True, name="mm_dw_ff2")
    da = _act_bwd(a_ff, dr, "act_bwd")
    gw["w_ff1"] = _mm(h2, da, ta=True, name="mm_dw_ff1")
    dh2 = _mm(da, full["w_ff1"], tb=True, name="mm_d_h2")
    dx1, dx1b, dg_ffn = _rms_bwd(x1, g_ffn, dh2, dy, "rms_ffn_bwd")
    dmerged = _mm(dx1b, full["w_out"], tb=True, name="mm_d_merged")
    gw["w_out"] = _mm(merged, dx1b, ta=True, name="mm_dw_out")
    dzg, dy_gm, dy_mla, dy_mem = _merge_bwd(z, y_gm, y_mla, y_mem, dmerged, "merge_bwd")
    dgm = _mm(dy_gm, full["w_o_gm"], tb=True, name="mm_d_gm")
    gw["w_o_gm"] = _mm(gm, dy_gm, ta=True, name="mm_dw_o_gm")
    do_mla = _mm(dy_mla, full["w_o_mla"], tb=True, name="mm_d_omla")
    gw["w_o_mla"] = _mm(o_mla, dy_mla, ta=True, name="mm_dw_o_mla")
    do_mem = _mm(dy_mem, full["w_o_mem"], tb=True, name="mm_d_omem")
    gw["w_o_mem"] = _mm(o_mem, dy_mem, ta=True, name="mm_dw_o_mem")
    dz_uv, dg_ln, db_ln, dws, *dbcols = _gm_bwd(z, g_gm_ln, b_gm_ln, ws, bcols, dgm, "gm_bwd")
    dq, dk, dv = _mla_bwd(qc, kc, vc, o_mla, lse, do_mla, batch, seq, "mla_bwd")
    dz_mla, dg_cq, dg_ckv, dg_qn, dg_qp, dg_kn, dg_kp, dwq, dwkv = _prep_bwd(
        z, cos_f, sin_s, prep_gains, wq, wkv, swap, dq, dk, dv, "prep_bwd")
    dz_qm, dkvm, dg_mq, dg_mk = _mem_bwd(z, kvm, g_mq, g_mk, do_mem, batch, seq, "mem_bwd")
    dmemn = _mm(dkvm, full["w_mem_kv"], tb=True, name="mm_d_memn")
    gw["w_mem_kv"] = _mm(memn, dkvm, ta=True, name="mm_dw_memkv")
    _, _, dg_mem = _rms_bwd(mem2, g_mem, dmemn, None, "rms_mem_bwd")
    dz = jnp.concatenate([dzg, dz_uv, dz_qm, dz_mla], axis=1)
    dh1 = _mm(dz, win, tb=True, name="mm_d_h1")
    gw["w_in"] = _win_unlayout(_mm(h1, dz, ta=True, name="mm_dw_in"))
    gw["w_uq"] = _wq_unlayout(dwq)
    gw["w_ukv"] = _wkv_unlayout(dwkv)
    grad_x, _, dg_mix = _rms_bwd(x2, g_mix, dh1, dx1, "rms_mix_bwd")

    ck = jnp.stack([lax.axis_index("c"), 2 * lax.axis_index("x") + lax.axis_index("y")]).astype(jnp.int32)
    g_all = _pack_full_grads(gw)
    p32, pbf = _pair_add(ck, g_all, _pair_exchange(g_all, "pair_exchange"), "pair_add")
    g_flat = _join_halves(_sum_chips(ck, p32, _scatter_partials(pbf, "scatter_partials"), "sum_chips"), "join_halves")

    small_g = {"g_mix": dg_mix, "g_cq": dg_cq, "g_ckv": dg_ckv, "g_q_nope": dg_qn, "g_q_pe": dg_qp[:, :MLA_ROPE],
               "g_k_nope": dg_kn, "g_k_pe": dg_kp[:, :MLA_ROPE], "g_gm_ln": dg_ln, "b_gm_ln": db_ln,
               "w_spatial": dws, "b_spatial": jnp.concatenate(dbcols, axis=1).T, "g_mem": dg_mem, "g_mq": dg_mq,
               "g_mk": dg_mk, "g_ffn": dg_ffn}
    g_small = _sum_slots(_gather_small(_pack_small(small_g), "gather_small"), "sum_small")

    w_flat = _pack_shard({n: wts[n][0] for n in BIG})
    m_flat = _pack_shard({n: mom[n][0] for n in BIG})
    v_flat = _pack_shard({n: var[n][0] for n in BIG})
    d_flat, nm_flat, nv_flat = _adamw(w_flat, g_flat, m_flat, v_flat, "adamw_shards")
    d_small, nm_small, nv_small = _adamw(_pack_small(wts), g_small, _pack_small(mom), _pack_small(var), "adamw_small")

    def unpack(flat, small):
        big = {n: a[None] for n, a in _unpack_shard(flat).items()}
        big.update(_unpack_small(small))
        return [big[n] for n in WEIGHTS]

    loss = lax.psum(loss_tile[0, 0], ("x", "y", "c"))
    grad_x = grad_x.reshape(batch, seq, D_MODEL)
    return (loss, grad_x, *unpack(g_flat, g_small), *unpack(d_flat, d_small), *unpack(nm_flat, nm_small),
            *unpack(nv_flat, nv_small))
```

```python
import math

import numpy as np
import jax
import jax.numpy as jnp
from jax import lax
from jax.experimental import pallas as pl
from jax.experimental.pallas import tpu as pltpu

F32 = jnp.float32
BF = jnp.bfloat16
SDS = jax.ShapeDtypeStruct
MESH = pl.DeviceIdType.MESH

D_MODEL = 1024
MEM_LEN = 256
MEM_HEADS = 4
HEAD_DIM = 128
GM_WIDTH = 512
GM_CHUNK = 128
GM_GROUPS = 4
MLA_HEADS = 8
MLA_NOPE = 128
MLA_ROPE = 64
MLA_V = 128
Q_LORA = 384
KV_LORA = 256
ROPE_BASE = 10000.0
D_FF = 4096
EPS = 1e-6
W_IN_COLS = 5312
ADAM_LR, ADAM_B1, ADAM_B2, ADAM_EPS, ADAM_WD, ADAM_STEP = 0.001, 0.9, 0.999, 1e-08, 0.01, 10

ZG, ZU, ZV, QM, CQ, KPE, CKV = 0, 3072, 3584, 4096, 4608, 4992, 5120
Z_COLS = 5376
LANES = 128
ROW_TILE = 256
ATT_TILE = 512
VMEM_LIMIT = 56 * 1024 * 1024

N_CHIPS = 4
PIECE_ROWS = 256
SMALL_ROWS = 560

BIG = ["w_in", "w_uq", "w_ukv", "w_mem_kv", "w_o_gm", "w_o_mla", "w_o_mem", "w_out", "w_ff1", "w_ff2"]
BIG_SHAPE = {"w_in": (1024, 5312), "w_uq": (384, 1536), "w_ukv": (256, 2048), "w_mem_kv": (1024, 1024),
             "w_o_gm": (512, 1024), "w_o_mla": (1024, 1024), "w_o_mem": (512, 1024), "w_out": (1024, 1024),
             "w_ff1": (1024, 4096), "w_ff2": (4096, 1024)}
COL_SHARDED = {"w_in", "w_uq", "w_ukv", "w_o_gm", "w_o_mem", "w_ff1"}
SMALL = ["w_spatial", "b_spatial", "g_mix", "g_cq", "g_ckv", "g_q_nope", "g_q_pe", "g_k_nope", "g_k_pe", "g_gm_ln",
         "b_gm_ln", "g_mem", "g_mq", "g_mk", "g_ffn"]
SMALL_SHAPE = {"g_mix": (1, 1024), "g_cq": (1, 384), "g_ckv": (1, 256), "g_q_nope": (1, 128), "g_q_pe": (1, 64),
               "g_k_nope": (1, 128), "g_k_pe": (1, 64), "g_gm_ln": (1, 512), "b_gm_ln": (1, 512),
               "w_spatial": (1, 4, 128, 128), "b_spatial": (1, 4, 128), "g_mem": (1, 1024), "g_mq": (1, 128),
               "g_mk": (1, 128), "g_ffn": (1, 1024)}
WEIGHTS = ['g_mix', 'w_in', 'g_cq', 'w_uq', 'g_ckv', 'w_ukv', 'g_q_nope', 'g_q_pe', 'g_k_nope', 'g_k_pe',
           'g_gm_ln', 'b_gm_ln', 'w_spatial', 'b_spatial', 'g_mem', 'w_mem_kv', 'g_mq', 'g_mk', 'w_o_gm',
           'w_o_mla', 'w_o_mem', 'w_out', 'g_ffn', 'w_ff1', 'w_ff2']


def _params(sem=None):
    return pltpu.CompilerParams(vmem_limit_bytes=VMEM_LIMIT, dimension_semantics=sem)


def _pick(n, prefs):
    for p in prefs:
        if n % p == 0:
            return p
    return n


def _full(shape):
    nd = len(shape)
    return pl.BlockSpec(shape, lambda *_: (0,) * nd)


def _rows(t, w, blk=0):
    return pl.BlockSpec((t, w), lambda i: (i, blk))


def _acc(ref, val, first):
    @pl.when(first)
    def _():
        ref[...] = val

    @pl.when(jnp.logical_not(first))
    def _():
        ref[...] += val


def _dn(a, b, ca, cb):
    return lax.dot_general(a.astype(BF), b.astype(BF), (((ca,), (cb,)), ((), ())), preferred_element_type=F32)


@jax.custom_vjp
def _mm_nn(a, b):
    return _dn(a, b, 1, 0)


def _mm_nn_fwd(a, b):
    return _dn(a, b, 1, 0), (a.astype(BF), b.astype(BF))


def _mm_nn_bwd(res, ct):
    a, b = res
    return _dn(ct, b, 1, 1), _dn(a, ct, 0, 0)


_mm_nn.defvjp(_mm_nn_fwd, _mm_nn_bwd)


@jax.custom_vjp
def _mm_nt(a, b):
    return _dn(a, b, 1, 1)


def _mm_nt_fwd(a, b):
    return _dn(a, b, 1, 1), (a.astype(BF), b.astype(BF))


def _mm_nt_bwd(res, ct):
    a, b = res
    return _dn(ct, b, 1, 0), _dn(ct, a, 0, 0)


_mm_nt.defvjp(_mm_nt_fwd, _mm_nt_bwd)


def _rmsn(x, g, n):
    ms = jnp.sum(x * x, axis=-1, keepdims=True) * (1.0 / n)
    return x * lax.rsqrt(ms + EPS) * g


def _layernorm(x, g, b):
    mu = jnp.mean(x, axis=-1, keepdims=True)
    xc = x - mu
    y = xc * lax.rsqrt(jnp.mean(xc * xc, axis=-1, keepdims=True) + EPS)
    return y * g + b


def _rope(x, cos_f, sin_s, swap):
    xs = lax.dot_general(x, swap, (((1,), (0,)), ((), ())), precision=lax.Precision.HIGHEST,
                         preferred_element_type=F32)
    return x * cos_f + xs * sin_s


def _softmax(s):
    m = lax.stop_gradient(jnp.max(s, axis=-1, keepdims=True))
    p = jnp.exp(s - m)
    return p / jnp.sum(p, axis=-1, keepdims=True)


def _mm(a, b, *, ta=False, tb=False, ins=(), epilogue=None, out_dtypes=(F32,), owner_cols=None, name):
    if ta:
        k_dim, m = a.shape
    else:
        m, k_dim = a.shape
    if tb:
        n, kb = b.shape
    else:
        kb, n = b.shape
    assert k_dim == kb, (a.shape, b.shape, ta, tb)
    tm = _pick(m, (1024, 512, 256, 128))
    tn = _pick(n if owner_cols is None else owner_cols, (1024, 768, 512, 384, 256, 128))
    tk = _pick(k_dim, (2048, 1024, 768, 512, 256, 128))
    nk = k_dim // tk
    ca = 0 if ta else 1
    cb = 1 if tb else 0
    n_in = len(ins)
    n_out = len(out_dtypes)

    def finish(r, in_refs, out_refs):
        vals = epilogue(r, *[ref[...] for ref in in_refs]) if epilogue is not None else (r,)
        for ref, val, dt in zip(out_refs, vals, out_dtypes):
            ref[...] = val.astype(dt)

    def body(*refs):
        a_ref, b_ref = refs[:2]
        in_refs = refs[2:2 + n_in]
        out_refs = refs[2 + n_in:2 + n_in + n_out]
        part = _dn(a_ref[...], b_ref[...], ca, cb)
        if nk == 1:
            finish(part, in_refs, out_refs)
            return
        acc = refs[-1]
        k = pl.program_id(2)
        _acc(acc, part, k == 0)

        @pl.when(k == nk - 1)
        def _():
            finish(acc[...], in_refs, out_refs)

    a_spec = pl.BlockSpec((tk, tm), lambda i, j, k: (k, i)) if ta else pl.BlockSpec((tm, tk), lambda i, j, k: (i, k))
    b_spec = pl.BlockSpec((tn, tk), lambda i, j, k: (j, k)) if tb else pl.BlockSpec((tk, tn), lambda i, j, k: (k, j))
    t_spec = pl.BlockSpec((tm, tn), lambda i, j, k: (i, j))
    if owner_cols is None:
        o_spec, o_shape = t_spec, (m, n)
    else:
        per = owner_cols // tn
        o_spec = pl.BlockSpec((None, tm, tn), lambda i, j, k: (j // per, i, j % per))
        o_shape = (n // owner_cols, m, owner_cols)
    outs = pl.pallas_call(
        body, grid=(m // tm, n // tn, nk), in_specs=[a_spec, b_spec] + [t_spec] * n_in,
        out_specs=[o_spec] * n_out, out_shape=[SDS(o_shape, dt) for dt in out_dtypes],
        scratch_shapes=[pltpu.VMEM((tm, tn), F32)] if nk > 1 else [],
        compiler_params=_params(("parallel", "parallel", "arbitrary")), name=name)(a, b, *ins)
    return outs[0] if n_out == 1 else outs


def _add_to(r, x):
    return (r + x,)


def _relu2(r):
    p = jnp.maximum(r, 0.0)
    return r, p * p


def _relu2_bwd(dr, a):
    return (dr * (2.0 * jnp.maximum(a, 0.0)),)


def _rms_fwd(x, g, name):
    n, w = x.shape
    t = min(ROW_TILE, n)

    def body(x_ref, g_ref, o_ref):
        o_ref[...] = _rmsn(x_ref[...], g_ref[...], w).astype(BF)

    return pl.pallas_call(body, grid=(n // t,), in_specs=[_rows(t, w), _full((1, w))], out_specs=_rows(t, w),
                          out_shape=SDS((n, w), BF), compiler_params=_params(("arbitrary",)), name=name)(x, g)


def _rms_bwd(x, g, dh, res, name):
    n, w = x.shape
    t = min(ROW_TILE, n)
    has_res = res is not None

    def body(*refs):
        if has_res:
            x_ref, g_ref, dh_ref, res_ref, dx_ref, dxb_ref, dg_ref = refs
        else:
            x_ref, g_ref, dh_ref, dx_ref, dxb_ref, dg_ref = refs
        _, vjp = jax.vjp(lambda xx, gg: _rmsn(xx, gg, w), x_ref[...], g_ref[...])
        dx, dg = vjp(dh_ref[...])
        if has_res:
            dx = dx + res_ref[...]
        dx_ref[...] = dx
        dxb_ref[...] = dx.astype(BF)
        _acc(dg_ref, dg, pl.program_id(0) == 0)

    in_specs = [_rows(t, w), _full((1, w)), _rows(t, w)] + ([_rows(t, w)] if has_res else [])
    args = [x, g, dh] + ([res] if has_res else [])
    return pl.pallas_call(body, grid=(n // t,), in_specs=in_specs,
                          out_specs=[_rows(t, w), _rows(t, w), _full((1, w))],
                          out_shape=[SDS((n, w), F32), SDS((n, w), BF), SDS((1, w), F32)],
                          compiler_params=_params(("arbitrary",)), name=name)(*args)


def _loss_call(y, tgt, name):
    n, w = y.shape
    t = min(ROW_TILE, n)

    def body(y_ref, t_ref, dy_ref, dyb_ref, l_ref):
        e = y_ref[...] - t_ref[...]
        dy = e * (1.0 / w)
        dy_ref[...] = dy
        dyb_ref[...] = dy.astype(BF)
        part = jnp.sum(jnp.sum(e * e, axis=-1, keepdims=True), axis=0, keepdims=True) * (0.5 / w)
        _acc(l_ref, jnp.broadcast_to(part, (8, LANES)), pl.program_id(0) == 0)

    return pl.pallas_call(body, grid=(n // t,), in_specs=[_rows(t, w), _rows(t, w)],
                          out_specs=[_rows(t, w), _rows(t, w), _full((8, LANES))],
                          out_shape=[SDS((n, w), F32), SDS((n, w), BF), SDS((8, LANES), F32)],
                          compiler_params=_params(("arbitrary",)), name=name)(y, tgt)


def _merge_core(zg0, zg1, zg2, y0, y1, y2):
    return jax.nn.sigmoid(zg0) * y0 + jax.nn.sigmoid(zg1) * y1 + jax.nn.sigmoid(zg2) * y2


def _merge_fwd(z, y_gm, y_mla, y_mem, name):
    n = z.shape[0]
    t = min(ROW_TILE, n)
    w = D_MODEL

    def body(g0, g1, g2, y0, y1, y2, o_ref):
        o_ref[...] = _merge_core(g0[...], g1[...], g2[...], y0[...], y1[...], y2[...]).astype(BF)

    return pl.pallas_call(body, grid=(n // t,),
                          in_specs=[_rows(t, w, 0), _rows(t, w, 1), _rows(t, w, 2)] + [_rows(t, w)] * 3,
                          out_specs=_rows(t, w), out_shape=SDS((n, w), BF),
                          compiler_params=_params(("parallel",)), name=name)(z, z, z, y_gm, y_mla, y_mem)


def _merge_bwd(z, y_gm, y_mla, y_mem, dmerged, name):
    n = z.shape[0]
    t = min(ROW_TILE, n)
    w = D_MODEL

    def body(g0, g1, g2, y0, y1, y2, dm, dzg_ref, d0_ref, d1_ref, d2_ref):
        _, vjp = jax.vjp(_merge_core, g0[...], g1[...], g2[...], y0[...], y1[...], y2[...])
        dg0, dg1, dg2, dy0, dy1, dy2 = vjp(dm[...])
        dzg_ref[:, 0:w] = dg0.astype(BF)
        dzg_ref[:, w:2 * w] = dg1.astype(BF)
        dzg_ref[:, 2 * w:3 * w] = dg2.astype(BF)
        d0_ref[...] = dy0.astype(BF)
        d1_ref[...] = dy1.astype(BF)
        d2_ref[...] = dy2.astype(BF)

    return pl.pallas_call(body, grid=(n // t,),
                          in_specs=[_rows(t, w, 0), _rows(t, w, 1), _rows(t, w, 2)] + [_rows(t, w)] * 4,
                          out_specs=[_rows(t, 3 * w)] + [_rows(t, w)] * 3,
                          out_shape=[SDS((n, 3 * w), BF)] + [SDS((n, w), BF)] * 3,
                          compiler_params=_params(("parallel",)), name=name)(z, z, z, y_gm, y_mla, y_mem, dmerged)


def _gm_core(zu, zv, g_ln, b_ln, ws, bcols):
    t = zu.shape[0]
    u = jax.nn.gelu(zu)
    v = _layernorm(jax.nn.gelu(zv), g_ln, b_ln)
    row = lax.broadcasted_iota(jnp.int32, (GM_CHUNK, GM_CHUNK), 0)
    col = lax.broadcasted_iota(jnp.int32, (GM_CHUNK, GM_CHUNK), 1)
    wc = [jnp.where(row >= col, ws[g], 0.0) for g in range(GM_GROUPS)]
    chunks = []
    for c in range(t // GM_CHUNK):
        cols = []
        for g in range(GM_GROUPS):
            vc = v[c * GM_CHUNK:(c + 1) * GM_CHUNK, g * LANES:(g + 1) * LANES]
            cols.append(_mm_nn(wc[g], vc) + bcols[g])
        chunks.append(jnp.concatenate(cols, axis=1))
    mixed = chunks[0] if len(chunks) == 1 else jnp.concatenate(chunks, axis=0)
    return u * mixed


def _gm_specs(t):
    return [_rows(t, GM_WIDTH, ZU // GM_WIDTH), _rows(t, GM_WIDTH, ZV // GM_WIDTH), _full((1, GM_WIDTH)),
            _full((1, GM_WIDTH)), _full((GM_GROUPS, GM_CHUNK, GM_CHUNK))] + [_full((GM_CHUNK, 1))] * GM_GROUPS


def _gm_fwd(z, g_ln, b_ln, ws, bcols, name):
    n = z.shape[0]
    t = min(ROW_TILE, n)

    def body(zu, zv, g_ref, b_ref, ws_ref, c0, c1, c2, c3, o_ref):
        out = _gm_core(zu[...], zv[...], g_ref[...], b_ref[...], [ws_ref[g] for g in range(GM_GROUPS)],
                       [c0[...], c1[...], c2[...], c3[...]])
        o_ref[...] = out.astype(BF)

    return pl.pallas_call(body, grid=(n // t,), in_specs=_gm_specs(t), out_specs=_rows(t, GM_WIDTH),
                          out_shape=SDS((n, GM_WIDTH), BF), compiler_params=_params(("parallel",)),
                          name=name)(z, z, g_ln, b_ln, ws, *bcols)


def _gm_bwd(z, g_ln, b_ln, ws, bcols, dgm, name):
    n = z.shape[0]
    t = min(ROW_TILE, n)

    def body(zu, zv, g_ref, b_ref, ws_ref, c0, c1, c2, c3, dgm_ref, dz_ref, dg_ref, db_ref, dws_ref, e0, e1, e2, e3):
        first = pl.program_id(0) == 0
        _, vjp = jax.vjp(_gm_core, zu[...], zv[...], g_ref[...], b_ref[...],
                         [ws_ref[g] for g in range(GM_GROUPS)], [c0[...], c1[...], c2[...], c3[...]])
        dzu, dzv, dg, db, dws, dcols = vjp(dgm_ref[...])
        dz_ref[:, 0:GM_WIDTH] = dzu.astype(BF)
        dz_ref[:, GM_WIDTH:2 * GM_WIDTH] = dzv.astype(BF)
        _acc(dg_ref, dg, first)
        _acc(db_ref, db, first)
        _acc(dws_ref, jnp.stack(dws, axis=0), first)
        for ref, val in zip((e0, e1, e2, e3), dcols):
            _acc(ref, val, first)

    return pl.pallas_call(
        body, grid=(n // t,), in_specs=_gm_specs(t) + [_rows(t, GM_WIDTH)],
        out_specs=[_rows(t, 2 * GM_WIDTH), _full((1, GM_WIDTH)), _full((1, GM_WIDTH)),
                   _full((GM_GROUPS, GM_CHUNK, GM_CHUNK))] + [_full((GM_CHUNK, 1))] * GM_GROUPS,
        out_shape=[SDS((n, 2 * GM_WIDTH), BF), SDS((1, GM_WIDTH), F32), SDS((1, GM_WIDTH), F32),
                   SDS((GM_GROUPS, GM_CHUNK, GM_CHUNK), F32)] + [SDS((GM_CHUNK, 1), F32)] * GM_GROUPS,
        compiler_params=_params(("arbitrary",)), name=name)(z, z, g_ln, b_ln, ws, *bcols, dgm)


def _rope_tables(pos_f, inv_full, cmask, smask, name):
    n = pos_f.shape[0]
    t = min(ROW_TILE, n)

    def body(p_ref, inv_ref, cm_ref, sm_ref, cos_ref, sin_ref):
        ang = p_ref[...] * inv_ref[...]
        cos_ref[...] = jnp.cos(ang) * cm_ref[...]
        sin_ref[...] = jnp.sin(ang) * sm_ref[...]

    return pl.pallas_call(body, grid=(n // t,), in_specs=[_rows(t, 1)] + [_full((1, LANES))] * 3,
                          out_specs=[_rows(t, LANES)] * 2, out_shape=[SDS((n, LANES), F32)] * 2,
                          compiler_params=_params(("parallel",)), name=name)(pos_f, inv_full, cmask, smask)


def _prep_core(cq, kpe, ckv, gains, wqn, wqp, wkn, wv, cos_f, sin_s, swap):
    g_cq, g_ckv, g_qn, g_qp, g_kn, g_kp = gains
    cqn = _rmsn(cq, g_cq, Q_LORA)
    ckvn = _rmsn(ckv, g_ckv, KV_LORA)
    kp = _rope(_rmsn(kpe, g_kp, MLA_ROPE), cos_f, sin_s, swap)
    qs, ks, vs = [], [], []
    for h in range(MLA_HEADS):
        qs.append(_rmsn(_mm_nn(cqn, wqn[h]), g_qn, MLA_NOPE))
        qs.append(_rope(_rmsn(_mm_nn(cqn, wqp[h]), g_qp, MLA_ROPE), cos_f, sin_s, swap))
        ks.append(_rmsn(_mm_nn(ckvn, wkn[h]), g_kn, MLA_NOPE))
        ks.append(kp)
        vs.append(_mm_nn(ckvn, wv[h]))
    return jnp.concatenate(qs, axis=1), jnp.concatenate(ks, axis=1), jnp.concatenate(vs, axis=1)


def _prep_in_specs(t):
    return ([_rows(t, Q_LORA, CQ // Q_LORA), _rows(t, LANES, KPE // LANES), _rows(t, KV_LORA, CKV // KV_LORA),
             _rows(t, LANES), _rows(t, LANES), _full((1, Q_LORA)), _full((1, KV_LORA))] + [_full((1, LANES))] * 4
            + [_full((Q_LORA, 2048)), _full((KV_LORA, 2048)), _full((LANES, LANES))])


def _prep_load(refs):
    cq, kpe, ckv, cos_f, sin_s, g_cq, g_ckv, g_qn, g_qp, g_kn, g_kp, wq, wkv, swap = refs
    hs = range(MLA_HEADS)
    wqn = [wq[:, h * LANES:(h + 1) * LANES].astype(F32) for h in hs]
    wqp = [wq[:, 1024 + h * LANES:1024 + (h + 1) * LANES].astype(F32) for h in hs]
    wkn = [wkv[:, h * LANES:(h + 1) * LANES].astype(F32) for h in hs]
    wv = [wkv[:, 1024 + h * LANES:1024 + (h + 1) * LANES].astype(F32) for h in hs]
    gains = [g_cq[...], g_ckv[...], g_qn[...], g_qp[...], g_kn[...], g_kp[...]]
    return (cq[...], kpe[...], ckv[...], gains, wqn, wqp, wkn, wv), (cos_f[...], sin_s[...], swap[...])


def _prep_fwd(z, cos_f, sin_s, gains, wq, wkv, swap, name):
    n = z.shape[0]
    t = min(ROW_TILE, n)

    def body(*refs):
        diff, const = _prep_load(refs[:14])
        q, k, v = _prep_core(*diff, *const)
        q_ref, k_ref, v_ref = refs[14:]
        q_ref[...] = q.astype(BF)
        k_ref[...] = k.astype(BF)
        v_ref[...] = v.astype(BF)

    return pl.pallas_call(body, grid=(n // t,), in_specs=_prep_in_specs(t),
                          out_specs=[_rows(t, 2048), _rows(t, 2048), _rows(t, 1024)],
                          out_shape=[SDS((n, 2048), BF), SDS((n, 2048), BF), SDS((n, 1024), BF)],
                          compiler_params=_params(("parallel",)),
                          name=name)(z, z, z, cos_f, sin_s, *gains, wq, wkv, swap)


def _prep_bwd(z, cos_f, sin_s, gains, wq, wkv, swap, dq, dk, dv, name):
    n = z.shape[0]
    t = min(ROW_TILE, n)
    wz = Q_LORA + LANES + KV_LORA

    def body(*refs):
        diff, const = _prep_load(refs[:14])
        dq_ref, dk_ref, dv_ref = refs[14:17]
        dz_ref, o_cq, o_ckv, o_qn, o_qp, o_kn, o_kp, dwq_ref, dwkv_ref = refs[17:]
        first = pl.program_id(0) == 0
        _, vjp = jax.vjp(lambda *d: _prep_core(*d, *const), *diff)
        dcq, dkpe, dckv, dgains, dwqn, dwqp, dwkn, dwv = vjp((dq_ref[...], dk_ref[...], dv_ref[...]))
        dz_ref[:, 0:Q_LORA] = dcq.astype(BF)
        dz_ref[:, Q_LORA:Q_LORA + LANES] = dkpe.astype(BF)
        dz_ref[:, Q_LORA + LANES:wz] = dckv.astype(BF)
        for ref, val in zip((o_cq, o_ckv, o_qn, o_qp, o_kn, o_kp), dgains):
            _acc(ref, val, first)
        _acc(dwq_ref, jnp.concatenate(dwqn + dwqp, axis=1), first)
        _acc(dwkv_ref, jnp.concatenate(dwkn + dwv, axis=1), first)

    gain_specs = [_full((1, Q_LORA)), _full((1, KV_LORA))] + [_full((1, LANES))] * 4
    gain_shapes = [SDS((1, Q_LORA), F32), SDS((1, KV_LORA), F32)] + [SDS((1, LANES), F32)] * 4
    return pl.pallas_call(
        body, grid=(n // t,), in_specs=_prep_in_specs(t) + [_rows(t, 2048), _rows(t, 2048), _rows(t, 1024)],
        out_specs=[_rows(t, wz)] + gain_specs + [_full((Q_LORA, 2048)), _full((KV_LORA, 2048))],
        out_shape=[SDS((n, wz), BF)] + gain_shapes + [SDS((Q_LORA, 2048), F32), SDS((KV_LORA, 2048), F32)],
        compiler_params=_params(("arbitrary",)),
        name=name)(z, z, z, cos_f, sin_s, *gains, wq, wkv, swap, dq, dk, dv)


MLA_QK = 256
MLA_SCALE = 1.0 / math.sqrt(MLA_NOPE + MLA_ROPE)


def _causal_mask(s, q0, k0):
    tq, tk = s.shape
    row = q0 + lax.broadcasted_iota(jnp.int32, (tq, tk), 0)
    col = k0 + lax.broadcasted_iota(jnp.int32, (tq, tk), 1)
    return jnp.where(row >= col, s, -jnp.inf)


def _mla_fwd(q, k, v, batch, seq, name):
    n = q.shape[0]
    tq = min(ATT_TILE, seq)
    nq = seq // tq

    def body(q_ref, k_ref, v_ref, o_ref, lse_ref):
        i = pl.program_id(2)
        qb = q_ref[...]

        def step(j, carry):
            m, l, acc = carry
            k0 = pl.multiple_of(j * tq, tq)
            kb = k_ref[pl.ds(k0, tq), :]
            vb = v_ref[pl.ds(k0, tq), :]
            s = _causal_mask(_dn(qb, kb, 1, 1) * MLA_SCALE, i * tq, k0)
            m_new = jnp.maximum(m, jnp.max(s, axis=-1, keepdims=True))
            p = jnp.exp(s - m_new)
            alpha = jnp.exp(m - m_new)
            l = alpha * l + jnp.sum(p, axis=-1, keepdims=True)
            acc = alpha * acc + _dn(p, vb, 1, 0)
            return m_new, l, acc

        init = (jnp.full((tq, 1), -jnp.inf, F32), jnp.zeros((tq, 1), F32), jnp.zeros((tq, MLA_V), F32))
        m, l, acc = lax.fori_loop(0, i + 1, step, init)
        o_ref[...] = acc / l
        lse_ref[...] = jnp.broadcast_to(m + jnp.log(l), (tq, LANES))

    return pl.pallas_call(
        body, grid=(batch, MLA_HEADS, nq),
        in_specs=[pl.BlockSpec((tq, MLA_QK), lambda b, h, i: (b * nq + i, h)),
                  pl.BlockSpec((seq, MLA_QK), lambda b, h, i: (b, h)),
                  pl.BlockSpec((seq, MLA_V), lambda b, h, i: (b, h))],
        out_specs=[pl.BlockSpec((tq, MLA_V), lambda b, h, i: (b * nq + i, h)),
                   pl.BlockSpec((tq, LANES), lambda b, h, i: (b * nq + i, h))],
        out_shape=[SDS((n, MLA_HEADS * MLA_V), F32), SDS((n, MLA_HEADS * LANES), F32)],
        compiler_params=_params(("parallel", "parallel", "arbitrary")), name=name)(q, k, v)


def _mla_bwd(q, k, v, o, lse, do, batch, seq, name):
    n = q.shape[0]
    tk = min(ATT_TILE, seq)
    nk = seq // tk

    def body(q_ref, k_ref, v_ref, o_ref, lse_ref, do_ref, dq_ref, dk_ref, dv_ref):
        jk = pl.program_id(2)
        kb = k_ref[...]
        vb = v_ref[...]

        @pl.when(jk == 0)
        def _():
            dq_ref[...] = jnp.zeros_like(dq_ref)

        def step(i, carry):
            dk_acc, dv_acc = carry
            q0 = pl.multiple_of(i * tk, tk)
            rows = pl.ds(q0, tk)
            qb = q_ref[rows, :]
            dob = do_ref[rows, :]
            delta = jnp.sum(dob * o_ref[rows, :], axis=-1, keepdims=True)
            s = _causal_mask(_dn(qb, kb, 1, 1) * MLA_SCALE, q0, jk * tk)
            p = jnp.exp(s - lse_ref[rows, :][:, 0:1])
            dv_acc = dv_acc + _dn(p, dob, 0, 0)
            dp = _dn(dob, vb, 1, 1)
            ds = p * (dp - delta) * MLA_SCALE
            dk_acc = dk_acc + _dn(ds, qb, 0, 0)
            dq_ref[rows, :] += _dn(ds, kb, 1, 0)
            return dk_acc, dv_acc

        dk_acc, dv_acc = lax.fori_loop(jk, nk, step, (jnp.zeros((tk, MLA_QK), F32), jnp.zeros((tk, MLA_V), F32)))
        dk_ref[...] = dk_acc
        dv_ref[...] = dv_acc

    full_qk = pl.BlockSpec((seq, MLA_QK), lambda b, h, j: (b, h))
    full_v = pl.BlockSpec((seq, MLA_V), lambda b, h, j: (b, h))
    blk_qk = pl.BlockSpec((tk, MLA_QK), lambda b, h, j: (b * nk + j, h))
    blk_v = pl.BlockSpec((tk, MLA_V), lambda b, h, j: (b * nk + j, h))
    return pl.pallas_call(
        body, grid=(batch, MLA_HEADS, nk),
        in_specs=[full_qk, blk_qk, blk_v, full_v, full_v, full_v],
        out_specs=[full_qk, blk_qk, blk_v],
        out_shape=[SDS((n, MLA_HEADS * MLA_QK), F32), SDS((n, MLA_HEADS * MLA_QK), F32),
                   SDS((n, MLA_HEADS * MLA_V), F32)],
        compiler_params=_params(("parallel", "parallel", "arbitrary")), name=name)(q, k, v, o, lse, do)


MEM_SCALE = 1.0 / math.sqrt(HEAD_DIM)
MEM_W = MEM_HEADS * HEAD_DIM


def _mem_core(qs, ks, vs, g_mq, g_mk):
    outs = []
    for h in range(MEM_HEADS):
        qh = _rmsn(qs[h], g_mq, HEAD_DIM)
        kh = _rmsn(ks[h], g_mk, HEAD_DIM)
        p = _softmax(_mm_nt(qh, kh) * MEM_SCALE)
        outs.append(_mm_nn(p, vs[h]))
    return jnp.concatenate(outs, axis=1)


def _mem_load(qm, kvm, g_mq, g_mk):
    hs = range(MEM_HEADS)
    qs = [qm[:, h * LANES:(h + 1) * LANES] for h in hs]
    ks = [kvm[:, h * LANES:(h + 1) * LANES] for h in hs]
    vs = [kvm[:, MEM_W + h * LANES:MEM_W + (h + 1) * LANES] for h in hs]
    return qs, ks, vs, g_mq[...], g_mk[...]


def _mem_fwd(z, kvm, g_mq, g_mk, batch, seq, name):
    n = z.shape[0]
    t = min(ROW_TILE, seq)
    per = seq // t

    def body(qm, kvm_ref, gq, gk, o_ref):
        o_ref[...] = _mem_core(*_mem_load(qm, kvm_ref, gq, gk)).astype(BF)

    return pl.pallas_call(
        body, grid=(n // t,),
        in_specs=[_rows(t, MEM_W, QM // MEM_W), pl.BlockSpec((MEM_LEN, 2 * MEM_W), lambda i: (i // per, 0)),
                  _full((1, LANES)), _full((1, LANES))],
        out_specs=_rows(t, MEM_W), out_shape=SDS((n, MEM_W), BF),
        compiler_params=_params(("parallel",)), name=name)(z, kvm, g_mq, g_mk)


def _mem_bwd(z, kvm, g_mq, g_mk, dom, batch, seq, name):
    n = z.shape[0]
    t = min(ROW_TILE, seq)
    per = seq // t

    def body(qm, kvm_ref, gq, gk, dom_ref, dz_ref, dkvm_ref, dgq_ref, dgk_ref):
        i = pl.program_id(0)
        _, vjp = jax.vjp(_mem_core, *_mem_load(qm, kvm_ref, gq, gk))
        dqs, dks, dvs, dgq, dgk = vjp(dom_ref[...])
        dz_ref[...] = jnp.concatenate(dqs, axis=1).astype(BF)
        _acc(dkvm_ref, jnp.concatenate(dks + dvs, axis=1), i % per == 0)
        _acc(dgq_ref, dgq, i == 0)
        _acc(dgk_ref, dgk, i == 0)

    kv_spec = pl.BlockSpec((MEM_LEN, 2 * MEM_W), lambda i: (i // per, 0))
    return pl.pallas_call(
        body, grid=(n // t,),
        in_specs=[_rows(t, MEM_W, QM // MEM_W), kv_spec, _full((1, LANES)), _full((1, LANES)), _rows(t, MEM_W)],
        out_specs=[_rows(t, MEM_W), kv_spec, _full((1, LANES)), _full((1, LANES))],
        out_shape=[SDS((n, MEM_W), BF), SDS((batch * MEM_LEN, 2 * MEM_W), F32), SDS((1, LANES), F32),
                   SDS((1, LANES), F32)],
        compiler_params=_params(("arbitrary",)), name=name)(z, kvm, g_mq, g_mk, dom)


ANY = pl.BlockSpec(memory_space=pl.ANY)


def _me():
    return lax.axis_index("x"), lax.axis_index("y"), lax.axis_index("c")


def _other_chips(x, y):
    return [(1 - x, y), (x, 1 - y), (1 - x, 1 - y)]


def _shard_shape(name):
    r, c = BIG_SHAPE[name]
    return (r, c // N_CHIPS) if name in COL_SHARDED else (r // N_CHIPS, c)


def _n_pieces(half_rows):
    return max(1, half_rows // PIECE_ROWS)


def _piece_plan(shapes):
    plan = []
    for r, _ in shapes:
        h = r // 2
        n = _n_pieces(h)
        plan.append((h, n, h // n))
    return plan


def _gather_weights(shards, name):
    n_t = len(shards)
    plan = _piece_plan([s.shape for s in shards])
    n_far = 3 * sum(n for _, n, _ in plan)
    n_loc = 2 * sum(n for _, n, _ in plan)

    def body(*refs):
        s_refs, o_refs = refs[:n_t], refs[n_t:2 * n_t]
        send, recv, local = refs[2 * n_t:]
        x, y, c = _me()
        k = 2 * x + y
        sib = (x, y, 1 - c)
        chips = _other_chips(x, y)

        def copy(sem, src, dst, to):
            return pltpu.make_async_remote_copy(src_ref=src, dst_ref=dst, send_sem=send.at[sem],
                                                recv_sem=recv.at[sem], device_id=to, device_id_type=MESH)

        mine, first, passed, from_far, from_sib = [], [], [], [], []
        s = q = 0
        for t in range(n_t):
            h, n, pr = plan[t]
            s_ref, o_ref = s_refs[t], o_refs[t]
            for core in range(2):
                for p in range(n):
                    rows = pl.ds(core * h + p * pr, pr)
                    mine.append(pltpu.make_async_copy(s_ref.at[rows], o_ref.at[k, rows], local.at[q]))
                    q += 1
            for chip in chips:
                ci = 2 * chip[0] + chip[1]
                for p in range(n):
                    rows = pl.ds(c * h + p * pr, pr)
                    rows_sib = pl.ds((1 - c) * h + p * pr, pr)
                    first.append(copy(s, s_ref.at[rows], o_ref.at[k, rows], (*chip, c)))
                    from_far.append(copy(s, s_ref.at[rows], o_ref.at[ci, rows], (*chip, c)))
                    passed.append(copy(n_far + s, o_ref.at[ci, rows], o_ref.at[ci, rows], sib))
                    from_sib.append(copy(n_far + s, o_ref.at[ci, rows_sib], o_ref.at[ci, rows_sib], sib))
                    s += 1
        for cp in mine + first:
            cp.start()
        for arrived, onward in zip(from_far, passed):
            arrived.wait_recv()
            onward.start()
        for cp in from_sib:
            cp.wait_recv()
        for cp in first + passed:
            cp.wait_send()
        for cp in mine:
            cp.wait()

    return pl.pallas_call(
        body, in_specs=[ANY] * n_t, out_specs=[ANY] * n_t,
        out_shape=[SDS((N_CHIPS,) + s.shape, s.dtype) for s in shards],
        scratch_shapes=[pltpu.SemaphoreType.DMA((2 * n_far,)), pltpu.SemaphoreType.DMA((2 * n_far,)),
                        pltpu.SemaphoreType.DMA((n_loc,))],
        name=name)(*shards)


def _pair_exchange(grads, name):
    n_t = len(grads)
    plan = _piece_plan([g.shape[1:] for g in grads])
    n_sem = sum(n for _, n, _ in plan)

    def body(*refs):
        g_refs, o_refs = refs[:n_t], refs[n_t:2 * n_t]
        send, recv = refs[2 * n_t:]
        x, y, c = _me()
        copies = []
        for t in range(n_t):
            h, n, pr = plan[t]
            for p in range(n):
                s = len(copies)
                copies.append(pltpu.make_async_remote_copy(
                    src_ref=g_refs[t].at[:, pl.ds((1 - c) * h + p * pr, pr)], dst_ref=o_refs[t].at[:, pl.ds(p * pr, pr)],
                    send_sem=send.at[s], recv_sem=recv.at[s], device_id=(x, y, 1 - c), device_id_type=MESH))
        for cp in copies:
            cp.start()
        for cp in copies:
            cp.wait()

    return pl.pallas_call(
        body, in_specs=[ANY] * n_t, out_specs=[ANY] * n_t,
        out_shape=[SDS((N_CHIPS, g.shape[1] // 2, g.shape[2]), F32) for g in grads],
        scratch_shapes=[pltpu.SemaphoreType.DMA((n_sem,)), pltpu.SemaphoreType.DMA((n_sem,))], name=name)(*grads)


def _pair_add(ck, g, theirs, name):
    _, r, c = g.shape
    (h, n, pr), = _piece_plan([(r, c)])

    def body(ck_ref, g_ref, t_ref, p32_ref, pbf_ref):
        s = g_ref[...] + t_ref[...]
        p32_ref[...] = s
        pbf_ref[...] = s.astype(BF)

    half = pl.BlockSpec((None, pr, c), lambda k, p, ck: (k, p, 0))
    spec = pltpu.PrefetchScalarGridSpec(
        num_scalar_prefetch=1, grid=(N_CHIPS, n),
        in_specs=[pl.BlockSpec((None, pr, c), lambda k, p, ck: (k, ck[0] * n + p, 0)), half], out_specs=[half, half])
    return pl.pallas_call(body, grid_spec=spec, out_shape=[SDS((N_CHIPS, h, c), F32), SDS((N_CHIPS, h, c), BF)],
                          compiler_params=_params(("arbitrary", "arbitrary")), name=name)(ck, g, theirs)


def _scatter_partials(pbfs, name):
    n_t = len(pbfs)
    plan = [(h, _n_pieces(h), h // _n_pieces(h)) for h in [p.shape[1] for p in pbfs]]
    n_sem = 3 * sum(n for _, n, _ in plan)

    def body(*refs):
        p_refs, o_refs = refs[:n_t], refs[n_t:2 * n_t]
        send, recv = refs[2 * n_t:]
        x, y, c = _me()
        copies = []
        for t in range(n_t):
            h, n, pr = plan[t]
            for j, chip in enumerate(_other_chips(x, y)):
                for p in range(n):
                    s = len(copies)
                    rows = pl.ds(p * pr, pr)
                    copies.append(pltpu.make_async_remote_copy(
                        src_ref=p_refs[t].at[2 * chip[0] + chip[1], rows], dst_ref=o_refs[t].at[j, rows],
                        send_sem=send.at[s], recv_sem=recv.at[s], device_id=(*chip, c), device_id_type=MESH))
        for cp in copies:
            cp.start()
        for cp in copies:
            cp.wait()

    return pl.pallas_call(
        body, in_specs=[ANY] * n_t, out_specs=[ANY] * n_t,
        out_shape=[SDS((3,) + p.shape[1:], BF) for p in pbfs],
        scratch_shapes=[pltpu.SemaphoreType.DMA((n_sem,)), pltpu.SemaphoreType.DMA((n_sem,))], name=name)(*pbfs)


def _sum_chips(ck, p32, slots, name):
    _, h, c = p32.shape
    n = _n_pieces(h)
    pr = h // n

    def body(ck_ref, p_ref, s_ref, o_ref):
        o_ref[...] = ((p_ref[...] + s_ref[0].astype(F32)) + s_ref[1].astype(F32)) + s_ref[2].astype(F32)

    spec = pltpu.PrefetchScalarGridSpec(
        num_scalar_prefetch=1, grid=(n,),
        in_specs=[pl.BlockSpec((None, pr, c), lambda p, ck: (ck[1], p, 0)),
                  pl.BlockSpec((3, pr, c), lambda p, ck: (0, p, 0))],
        out_specs=pl.BlockSpec((pr, c), lambda p, ck: (ck[0] * n + p, 0)))
    return pl.pallas_call(body, grid_spec=spec, out_shape=SDS((2 * h, c), F32),
                          compiler_params=_params(("arbitrary",)), name=name)(ck, p32, slots)


def _join_halves(sums, name):
    n_t = len(sums)
    plan = _piece_plan([s.shape for s in sums])
    n_sem = sum(n for _, n, _ in plan)

    def body(*refs):
        r_refs, o_refs = refs[:n_t], refs[n_t:2 * n_t]
        send, recv = refs[2 * n_t:]
        x, y, c = _me()
        copies = []
        for t in range(n_t):
            h, n, pr = plan[t]
            for p in range(n):
                s = len(copies)
                rows = pl.ds(c * h + p * pr, pr)
                copies.append(pltpu.make_async_remote_copy(
                    src_ref=r_refs[t].at[rows], dst_ref=o_refs[t].at[rows], send_sem=send.at[s], recv_sem=recv.at[s],
                    device_id=(x, y, 1 - c), device_id_type=MESH))
        for cp in copies:
            cp.start()
        for cp in copies:
            cp.wait()

    return pl.pallas_call(
        body, in_specs=[ANY] * n_t, out_specs=[ANY] * n_t, out_shape=[SDS(s.shape, F32) for s in sums],
        input_output_aliases={t: t for t in range(n_t)},
        scratch_shapes=[pltpu.SemaphoreType.DMA((n_sem,)), pltpu.SemaphoreType.DMA((n_sem,))], name=name)(*sums)


def _gather_small(s, name):
    def body(s_ref, o_ref, send, recv, local):
        x, y, c = _me()
        me = 4 * x + 2 * y + c
        keep = pltpu.make_async_copy(s_ref, o_ref.at[me], local)
        keep.start()
        sends = []
        for r in range(1, 8):
            fx, fy, fc = (r >> 2) & 1, (r >> 1) & 1, r & 1
            to = (x ^ fx, y ^ fy, c ^ fc)
            sends.append(pltpu.make_async_remote_copy(
                src_ref=s_ref, dst_ref=o_ref.at[me], send_sem=send.at[r - 1], recv_sem=recv.at[r - 1],
                device_id=to, device_id_type=MESH))
        for cp in sends:
            cp.start()
        for r in range(1, 8):
            fx, fy, fc = (r >> 2) & 1, (r >> 1) & 1, r & 1
            src = 4 * (x ^ fx) + 2 * (y ^ fy) + (c ^ fc)
            pltpu.make_async_remote_copy(
                src_ref=s_ref, dst_ref=o_ref.at[src], send_sem=send.at[r - 1], recv_sem=recv.at[r - 1],
                device_id=(x ^ fx, y ^ fy, c ^ fc), device_id_type=MESH).wait_recv()
        for cp in sends:
            cp.wait_send()
        keep.wait()

    return pl.pallas_call(
        body, in_specs=[ANY], out_specs=ANY, out_shape=SDS((8, SMALL_ROWS, LANES), F32),
        scratch_shapes=[pltpu.SemaphoreType.DMA((7,)), pltpu.SemaphoreType.DMA((7,)), pltpu.SemaphoreType.DMA],
        name=name)(s)


def _adam_math(w, g, m, v):
    nm = ADAM_B1 * m + (1.0 - ADAM_B1) * g
    nv = ADAM_B2 * v + (1.0 - ADAM_B2) * (g * g)
    m_hat = nm / (1.0 - ADAM_B1 ** ADAM_STEP)
    v_hat = nv / (1.0 - ADAM_B2 ** ADAM_STEP)
    return -ADAM_LR * (m_hat / (jnp.sqrt(v_hat) + ADAM_EPS) + ADAM_WD * w), nm, nv


def _adamw(w, g, m, v, name):
    _, r, c = w.shape
    t = _pick(r, (256, 128, 64))

    def body(w_ref, g_ref, m_ref, v_ref, go_ref, d_ref, nm_ref, nv_ref):
        g_ = g_ref[...]
        d, nm, nv = _adam_math(w_ref[...], g_, m_ref[...], v_ref[...])
        go_ref[...] = g_
        d_ref[...] = d
        nm_ref[...] = nm
        nv_ref[...] = nv

    lead = pl.BlockSpec((None, t, c), lambda i: (0, i, 0))
    return pl.pallas_call(body, grid=(r // t,), in_specs=[lead, _rows(t, c), lead, lead], out_specs=[lead] * 4,
                          out_shape=[SDS((1, r, c), F32)] * 4, compiler_params=_params(("parallel",)),
                          name=name)(w, g, m, v)


def _small_layout():
    out, r0 = {}, 0
    for n in SMALL:
        size = int(np.prod(SMALL_SHAPE[n]))
        nr = -(-size // LANES)
        out[n] = (r0, nr)
        r0 += nr
    assert r0 <= SMALL_ROWS
    return out, r0


def _pack_small(grads, name):
    layout, used = _small_layout()

    def body(*refs):
        o_ref = refs[-1]
        for n, ref in zip(SMALL, refs[:-1]):
            r0, nr = layout[n]
            if n == "w_spatial":
                for g in range(GM_GROUPS):
                    o_ref[r0 + g * GM_CHUNK:r0 + (g + 1) * GM_CHUNK, :] = ref[g]
            elif n == "b_spatial":
                o_ref[r0:r0 + nr, :] = ref[...]
            else:
                for i in range(nr):
                    o_ref[r0 + i:r0 + i + 1, :] = ref[:, i * LANES:(i + 1) * LANES]
        o_ref[used:SMALL_ROWS, :] = jnp.zeros((SMALL_ROWS - used, LANES), F32)

    return pl.pallas_call(body, out_shape=SDS((SMALL_ROWS, LANES), F32), name=name)(*grads)


def _adamw_small(gathered, ws, ms, vs, name):
    layout, _ = _small_layout()
    n_t = len(SMALL)

    def body(*refs):
        g_ref = refs[0]
        w_refs, m_refs, v_refs = refs[1:1 + n_t], refs[1 + n_t:1 + 2 * n_t], refs[1 + 2 * n_t:1 + 3 * n_t]
        outs = refs[1 + 3 * n_t:1 + 7 * n_t]
        acc = refs[-1]
        total = g_ref[0]
        for j in range(1, 8):
            total = total + g_ref[j]
        acc[...] = total
        for t, n in enumerate(SMALL):
            r0, nr = layout[n]
            o_refs = [outs[t], outs[n_t + t], outs[2 * n_t + t], outs[3 * n_t + t]]
            if n == "w_spatial":
                views = [((0, g), slice(r0 + g * GM_CHUNK, r0 + (g + 1) * GM_CHUNK), slice(None))
                         for g in range(GM_GROUPS)]
            elif n == "b_spatial":
                views = [((0,), slice(r0, r0 + nr), slice(None))]
            else:
                width = SMALL_SHAPE[n][1]
                views = [((slice(None), slice(i * LANES, min((i + 1) * LANES, width))), slice(r0 + i, r0 + i + 1),
                          slice(0, min(LANES, width - i * LANES))) for i in range(nr)]
            for idx, rows, lanes in views:
                g = acc[rows, lanes]
                d, nm, nv = _adam_math(w_refs[t][idx], g, m_refs[t][idx], v_refs[t][idx])
                for ref, val in zip(o_refs, (g, d, nm, nv)):
                    ref[idx] = val

    shapes = [SDS(SMALL_SHAPE[n], F32) for n in SMALL]
    return pl.pallas_call(body, out_shape=shapes * 4, scratch_shapes=[pltpu.VMEM((SMALL_ROWS, LANES), F32)],
                          name=name)(gathered, *ws, *ms, *vs)


def _win_layout(w_in):
    pad = jnp.zeros((w_in.shape[0], LANES - MLA_ROPE), w_in.dtype)
    u, v, cq = w_in[:, 0:512], w_in[:, 512:1024], w_in[:, 1024:1408]
    ckv, kpe, qm, zg = w_in[:, 1408:1664], w_in[:, 1664:1728], w_in[:, 1728:2240], w_in[:, 2240:5312]
    return jnp.concatenate([zg, u, v, qm, cq, kpe, pad, ckv], axis=1)


def _win_unlayout(g):
    zg, u, v, qm = g[:, ZG:ZG + 3072], g[:, ZU:ZU + 512], g[:, ZV:ZV + 512], g[:, QM:QM + 512]
    cq, kpe, ckv = g[:, CQ:CQ + 384], g[:, KPE:KPE + MLA_ROPE], g[:, CKV:CKV + 256]
    return jnp.concatenate([u, v, cq, ckv, kpe, qm, zg], axis=1)


def _wq_layout(w_uq):
    w = w_uq.reshape(Q_LORA, MLA_HEADS, MLA_NOPE + MLA_ROPE)
    nope = w[:, :, :MLA_NOPE].reshape(Q_LORA, MLA_HEADS * MLA_NOPE)
    pe = jnp.pad(w[:, :, MLA_NOPE:], ((0, 0), (0, 0), (0, LANES - MLA_ROPE))).reshape(Q_LORA, MLA_HEADS * LANES)
    return jnp.concatenate([nope, pe], axis=1)


def _wq_unlayout(g):
    nope = g[:, :1024].reshape(Q_LORA, MLA_HEADS, MLA_NOPE)
    pe = g[:, 1024:].reshape(Q_LORA, MLA_HEADS, LANES)[:, :, :MLA_ROPE]
    return jnp.concatenate([nope, pe], axis=2).reshape(Q_LORA, MLA_HEADS * (MLA_NOPE + MLA_ROPE))


def _wkv_layout(w_ukv):
    w = w_ukv.reshape(KV_LORA, MLA_HEADS, MLA_NOPE + MLA_V)
    return jnp.concatenate([w[:, :, :MLA_NOPE].reshape(KV_LORA, 1024), w[:, :, MLA_NOPE:].reshape(KV_LORA, 1024)],
                           axis=1)


def _wkv_unlayout(g):
    kn = g[:, :1024].reshape(KV_LORA, MLA_HEADS, MLA_NOPE)
    v = g[:, 1024:].reshape(KV_LORA, MLA_HEADS, MLA_V)
    return jnp.concatenate([kn, v], axis=2).reshape(KV_LORA, MLA_HEADS * (MLA_NOPE + MLA_V))


def _owner_major(g, name):
    r, c = _shard_shape(name)
    return g.reshape(r, N_CHIPS, c).transpose(1, 0, 2) if name in COL_SHARDED else g.reshape(N_CHIPS, r, c)


def _pad_lanes(g):
    return jnp.pad(g, ((0, 0), (0, LANES - g.shape[1])))


def kernel(x, mem, positions, g_mix, w_in, g_cq, w_uq, g_ckv, w_ukv, g_q_nope, g_q_pe, g_k_nope, g_k_pe, g_gm_ln, b_gm_ln, w_spatial, b_spatial, g_mem, w_mem_kv, g_mq, g_mk, w_o_gm, w_o_mla, w_o_mem, w_out, g_ffn, w_ff1, w_ff2, loss_target, m_g_mix, m_w_in, m_g_cq, m_w_uq, m_g_ckv, m_w_ukv, m_g_q_nope, m_g_q_pe, m_g_k_nope, m_g_k_pe, m_g_gm_ln, m_b_gm_ln, m_w_spatial, m_b_spatial, m_g_mem, m_w_mem_kv, m_g_mq, m_g_mk, m_w_o_gm, m_w_o_mla, m_w_o_mem, m_w_out, m_g_ffn, m_w_ff1, m_w_ff2, v_g_mix, v_w_in, v_g_cq, v_w_uq, v_g_ckv, v_w_ukv, v_g_q_nope, v_g_q_pe, v_g_k_nope, v_g_k_pe, v_g_gm_ln, v_b_gm_ln, v_w_spatial, v_b_spatial, v_g_mem, v_w_mem_kv, v_g_mq, v_g_mk, v_w_o_gm, v_w_o_mla, v_w_o_mem, v_w_out, v_g_ffn, v_w_ff1, v_w_ff2):
    given = dict(locals())
    wts = {n: given[n] for n in WEIGHTS}
    mom = {n: given["m_" + n] for n in WEIGHTS}
    var = {n: given["v_" + n] for n in WEIGHTS}
    batch, seq, _ = x.shape
    n_tok = batch * seq

    gathered = _gather_weights([wts[n][0].astype(BF) for n in BIG], "gather_weights")
    full = {}
    for n, g in zip(BIG, gathered):
        r, c = _shard_shape(n)
        full[n] = g.transpose(1, 0, 2).reshape(r, N_CHIPS * c) if n in COL_SHARDED else g.reshape(N_CHIPS * r, c)
    win = _win_layout(full["w_in"])
    wq = _wq_layout(full["w_uq"])
    wkv = _wkv_layout(full["w_ukv"])

    x2 = x.reshape(n_tok, D_MODEL)
    tgt2 = loss_target.reshape(n_tok, D_MODEL)
    mem2 = mem.reshape(batch * MEM_LEN, D_MODEL)
    pos_f = positions.reshape(n_tok, 1).astype(F32)

    inv = ROPE_BASE ** (-jnp.arange(0, MLA_ROPE, 2, dtype=F32) / MLA_ROPE)
    zeros64 = jnp.zeros((LANES - MLA_ROPE,), F32)
    inv_full = jnp.concatenate([inv, inv, zeros64]).reshape(1, LANES)
    half = MLA_ROPE // 2
    cmask = jnp.concatenate([jnp.ones((MLA_ROPE,), F32), zeros64]).reshape(1, LANES)
    smask = jnp.concatenate([-jnp.ones((half,), F32), jnp.ones((half,), F32), zeros64]).reshape(1, LANES)
    swap_np = np.zeros((LANES, LANES), np.float32)
    for j in range(half):
        swap_np[j + half, j] = 1.0
        swap_np[j, j + half] = 1.0
    swap = jnp.asarray(swap_np)

    prep_gains = [g_cq, g_ckv, g_q_nope, _pad_lanes(g_q_pe), g_k_nope, _pad_lanes(g_k_pe)]
    ws = w_spatial[0]
    bcols = [b_spatial[0, g].reshape(GM_CHUNK, 1) for g in range(GM_GROUPS)]

    h1 = _rms_fwd(x2, g_mix, "rms_mix")
    z = _mm(h1, win, name="mm_in")
    gm = _gm_fwd(z, g_gm_ln, b_gm_ln, ws, bcols, "gm_fwd")
    cos_f, sin_s = _rope_tables(pos_f, inv_full, cmask, smask, "rope_tables")
    qc, kc, vc = _prep_fwd(z, cos_f, sin_s, prep_gains, wq, wkv, swap, "prep_fwd")
    o_mla, lse = _mla_fwd(qc, kc, vc, batch, seq, "mla_fwd")
    memn = _rms_fwd(mem2, g_mem, "rms_mem")
    kvm = _mm(memn, full["w_mem_kv"], name="mm_memkv")
    o_mem = _mem_fwd(z, kvm, g_mq, g_mk, batch, seq, "mem_fwd")
    y_gm = _mm(gm, full["w_o_gm"], name="mm_o_gm")
    y_mla = _mm(o_mla, full["w_o_mla"], name="mm_o_mla")
    y_mem = _mm(o_mem, full["w_o_mem"], name="mm_o_mem")
    merged = _merge_fwd(z, y_gm, y_mla, y_mem, "merge_fwd")
    x1 = _mm(merged, full["w_out"], ins=(x2,), epilogue=_add_to, name="mm_out")
    h2 = _rms_fwd(x1, g_ffn, "rms_ffn")
    a_ff, r_ff = _mm(h2, full["w_ff1"], epilogue=_relu2, out_dtypes=(F32, BF), name="mm_ff1")
    y = _mm(r_ff, full["w_ff2"], ins=(x1,), epilogue=_add_to, name="mm_ff2")
    dy, dyb, loss_tile = _loss_call(y, tgt2, "loss")

    gw = {}
    da = _mm(dyb, full["w_ff2"], tb=True, ins=(a_ff,), epilogue=_relu2_bwd, out_dtypes=(BF,), name="mm_d_a")
    gw["w_ff2"] = _owner_major(_mm(r_ff, dyb, ta=True, name="mm_dw_ff2"), "w_ff2")
    gw["w_ff1"] = _mm(h2, da, ta=True, owner_cols=D_FF // N_CHIPS, name="mm_dw_ff1")
    dh2 = _mm(da, full["w_ff1"], tb=True, name="mm_d_h2")
    dx1, dx1b, dg_ffn = _rms_bwd(x1, g_ffn, dh2, dy, "rms_ffn_bwd")
    dmerged = _mm(dx1b, full["w_out"], tb=True, name="mm_d_merged")
    gw["w_out"] = _owner_major(_mm(merged, dx1b, ta=True, name="mm_dw_out"), "w_out")
    dzg, dy_gm, dy_mla, dy_mem = _merge_bwd(z, y_gm, y_mla, y_mem, dmerged, "merge_bwd")
    dgm = _mm(dy_gm, full["w_o_gm"], tb=True, name="mm_d_gm")
    gw["w_o_gm"] = _mm(gm, dy_gm, ta=True, owner_cols=D_MODEL // N_CHIPS, name="mm_dw_o_gm")
    do_mla = _mm(dy_mla, full["w_o_mla"], tb=True, name="mm_d_omla")
    gw["w_o_mla"] = _owner_major(_mm(o_mla, dy_mla, ta=True, name="mm_dw_o_mla"), "w_o_mla")
    do_mem = _mm(dy_mem, full["w_o_mem"], tb=True, name="mm_d_omem")
    gw["w_o_mem"] = _mm(o_mem, dy_mem, ta=True, owner_cols=D_MODEL // N_CHIPS, name="mm_dw_o_mem")
    dz_uv, dg_ln, db_ln, dws, *dbcols = _gm_bwd(z, g_gm_ln, b_gm_ln, ws, bcols, dgm, "gm_bwd")
    dq, dk, dv = _mla_bwd(qc, kc, vc, o_mla, lse, do_mla, batch, seq, "mla_bwd")
    dz_mla, dg_cq, dg_ckv, dg_qn, dg_qp, dg_kn, dg_kp, dwq, dwkv = _prep_bwd(
        z, cos_f, sin_s, prep_gains, wq, wkv, swap, dq, dk, dv, "prep_bwd")
    dz_qm, dkvm, dg_mq, dg_mk = _mem_bwd(z, kvm, g_mq, g_mk, do_mem, batch, seq, "mem_bwd")
    dmemn = _mm(dkvm, full["w_mem_kv"], tb=True, name="mm_d_memn")
    gw["w_mem_kv"] = _owner_major(_mm(memn, dkvm, ta=True, name="mm_dw_memkv"), "w_mem_kv")
    _, _, dg_mem = _rms_bwd(mem2, g_mem, dmemn, None, "rms_mem_bwd")
    dz = jnp.concatenate([dzg, dz_uv, dz_qm, dz_mla], axis=1)
    dh1 = _mm(dz, win, tb=True, name="mm_d_h1")
    gw["w_in"] = _owner_major(_win_unlayout(_mm(h1, dz, ta=True, name="mm_dw_in")), "w_in")
    gw["w_uq"] = _owner_major(_wq_unlayout(dwq), "w_uq")
    gw["w_ukv"] = _owner_major(_wkv_unlayout(dwkv), "w_ukv")
    grad_x, _, dg_mix = _rms_bwd(x2, g_mix, dh1, dx1, "rms_mix_bwd")

    ck = jnp.stack([lax.axis_index("c"), 2 * lax.axis_index("x") + lax.axis_index("y")]).astype(jnp.int32)
    grads = [gw[n] for n in BIG]
    theirs = _pair_exchange(grads, "pair_exchange")
    pairs = [_pair_add(ck, g, t, "pair_add_" + n) for n, g, t in zip(BIG, grads, theirs)]
    slots = _scatter_partials([p[1] for p in pairs], "scatter_partials")
    sums = [_sum_chips(ck, p[0], s, "sum_chips_" + n) for n, p, s in zip(BIG, pairs, slots)]
    reduced = _join_halves(sums, "join_halves")
    results = {n: _adamw(wts[n], g, mom[n], var[n], "adamw_" + n) for n, g in zip(BIG, reduced)}

    small_g = {"g_mix": dg_mix, "g_cq": dg_cq, "g_ckv": dg_ckv, "g_q_nope": dg_qn, "g_q_pe": dg_qp,
               "g_k_nope": dg_kn, "g_k_pe": dg_kp, "g_gm_ln": dg_ln, "b_gm_ln": db_ln, "w_spatial": dws,
               "b_spatial": jnp.concatenate(dbcols, axis=1).T, "g_mem": dg_mem, "g_mq": dg_mq, "g_mk": dg_mk,
               "g_ffn": dg_ffn}
    packed = _pack_small([small_g[n] for n in SMALL], "pack_small")
    small_out = _adamw_small(_gather_small(packed, "gather_small"), [wts[n] for n in SMALL],
                             [mom[n] for n in SMALL], [var[n] for n in SMALL], "adamw_small")
    for t, n in enumerate(SMALL):
        results[n] = [small_out[j * len(SMALL) + t] for j in range(4)]

    loss = lax.psum(loss_tile[0, 0], ("x", "y", "c"))
    grad_x = grad_x.reshape(batch, seq, D_MODEL)
    return (loss, grad_x, *[results[n][0] for n in WEIGHTS], *[results[n][1] for n in WEIGHTS],
            *[results[n][2] for n in WEIGHTS], *[results[n][3] for n in WEIGHTS])
```

```python
import functools
import math

import numpy as np
import jax
import jax.numpy as jnp
from jax import lax
from jax.experimental import pallas as pl
from jax.experimental.pallas import tpu as pltpu

F32 = jnp.float32
BF = jnp.bfloat16
SDS = jax.ShapeDtypeStruct
MESH = pl.DeviceIdType.MESH

D_MODEL = 1024
MEM_LEN = 256
MEM_HEADS = 4
HEAD_DIM = 128
GM_WIDTH = 512
GM_CHUNK = 128
GM_GROUPS = 4
MLA_HEADS = 8
MLA_NOPE = 128
MLA_ROPE = 64
MLA_V = 128
Q_LORA = 384
KV_LORA = 256
ROPE_BASE = 10000.0
D_FF = 4096
EPS = 1e-6
W_IN_COLS = 5312
ADAM_LR, ADAM_B1, ADAM_B2, ADAM_EPS, ADAM_WD, ADAM_STEP = 0.001, 0.9, 0.999, 1e-08, 0.01, 10

ZG, ZU, ZV, QM, CQ, KPE, CKV = 0, 3072, 3584, 4096, 4608, 4992, 5120
Z_COLS = 5376
LANES = 128
ROW_TILE = 256
ATT_TILE = 512
VMEM_LIMIT = 56 * 1024 * 1024

N_CHIPS = 4
PIECE_ROWS = 256
SMALL_ROWS = 560

BIG = ["w_in", "w_uq", "w_ukv", "w_mem_kv", "w_o_gm", "w_o_mla", "w_o_mem", "w_out", "w_ff1", "w_ff2"]
BIG_SHAPE = {"w_in": (1024, 5312), "w_uq": (384, 1536), "w_ukv": (256, 2048), "w_mem_kv": (1024, 1024),
             "w_o_gm": (512, 1024), "w_o_mla": (1024, 1024), "w_o_mem": (512, 1024), "w_out": (1024, 1024),
             "w_ff1": (1024, 4096), "w_ff2": (4096, 1024)}
COL_SHARDED = {"w_in", "w_uq", "w_ukv", "w_o_gm", "w_o_mem", "w_ff1"}
EARLY = ["w_in", "w_uq", "w_ukv", "w_mem_kv"]
LATE = ["w_o_gm", "w_o_mla", "w_o_mem", "w_out", "w_ff1", "w_ff2"]
SMALL = ["w_spatial", "b_spatial", "g_mix", "g_cq", "g_ckv", "g_q_nope", "g_q_pe", "g_k_nope", "g_k_pe", "g_gm_ln",
         "b_gm_ln", "g_mem", "g_mq", "g_mk", "g_ffn"]
SMALL_SHAPE = {"g_mix": (1, 1024), "g_cq": (1, 384), "g_ckv": (1, 256), "g_q_nope": (1, 128), "g_q_pe": (1, 64),
               "g_k_nope": (1, 128), "g_k_pe": (1, 64), "g_gm_ln": (1, 512), "b_gm_ln": (1, 512),
               "w_spatial": (1, 4, 128, 128), "b_spatial": (1, 4, 128), "g_mem": (1, 1024), "g_mq": (1, 128),
               "g_mk": (1, 128), "g_ffn": (1, 1024)}
WEIGHTS = ['g_mix', 'w_in', 'g_cq', 'w_uq', 'g_ckv', 'w_ukv', 'g_q_nope', 'g_q_pe', 'g_k_nope', 'g_k_pe',
           'g_gm_ln', 'b_gm_ln', 'w_spatial', 'b_spatial', 'g_mem', 'w_mem_kv', 'g_mq', 'g_mk', 'w_o_gm',
           'w_o_mla', 'w_o_mem', 'w_out', 'g_ffn', 'w_ff1', 'w_ff2']


def _params(sem=None):
    return pltpu.CompilerParams(vmem_limit_bytes=VMEM_LIMIT, dimension_semantics=sem)


def _pick(n, prefs):
    for p in prefs:
        if n % p == 0:
            return p
    return n


def _full(shape):
    nd = len(shape)
    return pl.BlockSpec(shape, lambda *_: (0,) * nd)


def _rows(t, w, blk=0):
    return pl.BlockSpec((t, w), lambda i: (i, blk))


def _acc(ref, val, first):
    @pl.when(first)
    def _():
        ref[...] = val

    @pl.when(jnp.logical_not(first))
    def _():
        ref[...] += val


ANY = pl.BlockSpec(memory_space=pl.ANY)


class _Phase:
    def __init__(self, operands, out_shapes, n_sem, n_local, copies, aliases=None):
        self.operands, self.out_shapes, self.aliases = list(operands), list(out_shapes), dict(aliases or {})
        self.n_sem, self.n_local, self.copies = n_sem, max(n_local, 1), copies

    def sem_shapes(self):
        return [pltpu.SemaphoreType.DMA((self.n_sem,)), pltpu.SemaphoreType.DMA((self.n_sem,)),
                pltpu.SemaphoreType.DMA((self.n_local,))]

    def start(self, ins, outs, send, recv, local):
        sends, _, locals_ = self.copies(ins, outs, send, recv, local)
        for cp in locals_ + sends:
            cp.start()

    def finish(self, ins, outs, send, recv, local):
        sends, arrivals, locals_ = self.copies(ins, outs, send, recv, local)
        for cp in arrivals:
            cp.wait_recv()
        for cp in sends:
            cp.wait_send()
        for cp in locals_:
            cp.wait()


def _run_phase(phase, name):
    n_in = len(phase.operands)

    def body(*refs):
        ins, outs, sems = refs[:n_in], refs[n_in:n_in + len(phase.out_shapes)], refs[n_in + len(phase.out_shapes):]
        phase.start(ins, outs, *sems)
        phase.finish(ins, outs, *sems)

    return pl.pallas_call(body, in_specs=[ANY] * n_in, out_specs=[ANY] * len(phase.out_shapes),
                          out_shape=phase.out_shapes, scratch_shapes=phase.sem_shapes(),
                          input_output_aliases=phase.aliases, name=name)(*phase.operands)


def _pcall(body, *, grid, in_specs, out_specs, out_shape, scratch_shapes=(), sem=None, name, comm=None):
    single = not isinstance(out_shape, (list, tuple))
    o_specs = [out_specs] if single else list(out_specs)
    o_shape = [out_shape] if single else list(out_shape)
    if comm is None:
        call = pl.pallas_call(body, grid=grid, in_specs=list(in_specs), out_specs=o_specs, out_shape=o_shape,
                              scratch_shapes=list(scratch_shapes), compiler_params=_params(sem), name=name)

        def run_plain(*args):
            res = call(*args)
            return res[0] if single else res

        return run_plain

    n_in, n_out, n_scr = len(in_specs), len(o_specs), len(scratch_shapes)
    nc_in, nc_out = len(comm.operands), len(comm.out_shapes)

    def wrapped(*refs):
        ins, cins = refs[:n_in], refs[n_in:n_in + nc_in]
        o0 = n_in + nc_in
        outs, couts = refs[o0:o0 + n_out], refs[o0 + n_out:o0 + n_out + nc_out]
        s0 = o0 + n_out + nc_out
        scr, csem = refs[s0:s0 + n_scr], refs[s0 + n_scr:]
        ids = [pl.program_id(d) for d in range(len(grid))]
        first = functools.reduce(jnp.logical_and, [i == 0 for i in ids])
        last = functools.reduce(jnp.logical_and, [i == g - 1 for i, g in zip(ids, grid)])

        @pl.when(first)
        def _():
            comm.start(cins, couts, *csem)

        body(*ins, *outs, *scr)

        @pl.when(last)
        def _():
            comm.finish(cins, couts, *csem)

    call = pl.pallas_call(
        wrapped, grid=grid, in_specs=list(in_specs) + [ANY] * nc_in, out_specs=o_specs + [ANY] * nc_out,
        out_shape=o_shape + comm.out_shapes, scratch_shapes=list(scratch_shapes) + comm.sem_shapes(),
        input_output_aliases={n_in + i: n_out + j for i, j in comm.aliases.items()},
        compiler_params=_params(("arbitrary",) * len(grid)), name=name)

    def run_carrying(*args):
        res = call(*args, *comm.operands)
        return (res[0] if single else res[:n_out]), res[n_out:]

    return run_carrying


def _dn(a, b, ca, cb):
    return lax.dot_general(a.astype(BF), b.astype(BF), (((ca,), (cb,)), ((), ())), preferred_element_type=F32)


@jax.custom_vjp
def _mm_nn(a, b):
    return _dn(a, b, 1, 0)


def _mm_nn_fwd(a, b):
    return _dn(a, b, 1, 0), (a.astype(BF), b.astype(BF))


def _mm_nn_bwd(res, ct):
    a, b = res
    return _dn(ct, b, 1, 1), _dn(a, ct, 0, 0)


_mm_nn.defvjp(_mm_nn_fwd, _mm_nn_bwd)


@jax.custom_vjp
def _mm_nt(a, b):
    return _dn(a, b, 1, 1)


def _mm_nt_fwd(a, b):
    return _dn(a, b, 1, 1), (a.astype(BF), b.astype(BF))


def _mm_nt_bwd(res, ct):
    a, b = res
    return _dn(ct, b, 1, 0), _dn(ct, a, 0, 0)


_mm_nt.defvjp(_mm_nt_fwd, _mm_nt_bwd)


def _rmsn(x, g, n):
    ms = jnp.sum(x * x, axis=-1, keepdims=True) * (1.0 / n)
    return x * lax.rsqrt(ms + EPS) * g


def _layernorm(x, g, b):
    mu = jnp.mean(x, axis=-1, keepdims=True)
    xc = x - mu
    y = xc * lax.rsqrt(jnp.mean(xc * xc, axis=-1, keepdims=True) + EPS)
    return y * g + b


def _rope(x, cos_f, sin_s, swap):
    xs = lax.dot_general(x, swap, (((1,), (0,)), ((), ())), precision=lax.Precision.HIGHEST,
                         preferred_element_type=F32)
    return x * cos_f + xs * sin_s


def _softmax(s):
    m = lax.stop_gradient(jnp.max(s, axis=-1, keepdims=True))
    p = jnp.exp(s - m)
    return p / jnp.sum(p, axis=-1, keepdims=True)


def _mm(a, b, *, ta=False, tb=False, ins=(), epilogue=None, out_dtypes=(F32,), owner_cols=None, name, comm=None):
    if ta:
        k_dim, m = a.shape
    else:
        m, k_dim = a.shape
    if tb:
        n, kb = b.shape
    else:
        kb, n = b.shape
    assert k_dim == kb, (a.shape, b.shape, ta, tb)
    tm = _pick(m, (1024, 512, 256, 128))
    tn = _pick(n if owner_cols is None else owner_cols, (1024, 768, 512, 384, 256, 128))
    tk = _pick(k_dim, (2048, 1024, 768, 512, 256, 128))
    nk = k_dim // tk
    ca = 0 if ta else 1
    cb = 1 if tb else 0
    n_in = len(ins)
    n_out = len(out_dtypes)

    def finish(r, in_refs, out_refs):
        vals = epilogue(r, *[ref[...] for ref in in_refs]) if epilogue is not None else (r,)
        for ref, val, dt in zip(out_refs, vals, out_dtypes):
            ref[...] = val.astype(dt)

    def body(*refs):
        a_ref, b_ref = refs[:2]
        in_refs = refs[2:2 + n_in]
        out_refs = refs[2 + n_in:2 + n_in + n_out]
        part = _dn(a_ref[...], b_ref[...], ca, cb)
        if nk == 1:
            finish(part, in_refs, out_refs)
            return
        acc = refs[-1]
        k = pl.program_id(2)
        _acc(acc, part, k == 0)

        @pl.when(k == nk - 1)
        def _():
            finish(acc[...], in_refs, out_refs)

    a_spec = pl.BlockSpec((tk, tm), lambda i, j, k: (k, i)) if ta else pl.BlockSpec((tm, tk), lambda i, j, k: (i, k))
    b_spec = pl.BlockSpec((tn, tk), lambda i, j, k: (j, k)) if tb else pl.BlockSpec((tk, tn), lambda i, j, k: (k, j))
    t_spec = pl.BlockSpec((tm, tn), lambda i, j, k: (i, j))
    if owner_cols is None:
        o_spec, o_shape = t_spec, (m, n)
    else:
        per = owner_cols // tn
        o_spec = pl.BlockSpec((None, tm, tn), lambda i, j, k: (j // per, i, j % per))
        o_shape = (n // owner_cols, m, owner_cols)
    run = _pcall(body, grid=(m // tm, n // tn, nk), in_specs=[a_spec, b_spec] + [t_spec] * n_in,
                 out_specs=[o_spec] * n_out, out_shape=[SDS(o_shape, dt) for dt in out_dtypes],
                 scratch_shapes=[pltpu.VMEM((tm, tn), F32)] if nk > 1 else [],
                 sem=("parallel", "parallel", "arbitrary"), name=name, comm=comm)
    if comm is None:
        outs = run(a, b, *ins)
        return outs[0] if n_out == 1 else outs
    outs, exchanged = run(a, b, *ins)
    return (outs[0] if n_out == 1 else outs), exchanged


def _add_to(r, x):
    return (r + x,)


def _relu2(r):
    p = jnp.maximum(r, 0.0)
    return r, p * p


def _relu2_bwd(dr, a):
    return (dr * (2.0 * jnp.maximum(a, 0.0)),)


def _rms_fwd(x, g, name):
    n, w = x.shape
    t = min(ROW_TILE, n)

    def body(x_ref, g_ref, o_ref):
        o_ref[...] = _rmsn(x_ref[...], g_ref[...], w).astype(BF)

    return pl.pallas_call(body, grid=(n // t,), in_specs=[_rows(t, w), _full((1, w))], out_specs=_rows(t, w),
                          out_shape=SDS((n, w), BF), compiler_params=_params(("arbitrary",)), name=name)(x, g)


def _rms_bwd(x, g, dh, res, name, comm=None):
    n, w = x.shape
    t = min(ROW_TILE, n)
    has_res = res is not None

    def body(*refs):
        if has_res:
            x_ref, g_ref, dh_ref, res_ref, dx_ref, dxb_ref, dg_ref = refs
        else:
            x_ref, g_ref, dh_ref, dx_ref, dxb_ref, dg_ref = refs
        _, vjp = jax.vjp(lambda xx, gg: _rmsn(xx, gg, w), x_ref[...], g_ref[...])
        dx, dg = vjp(dh_ref[...])
        if has_res:
            dx = dx + res_ref[...]
        dx_ref[...] = dx
        dxb_ref[...] = dx.astype(BF)
        _acc(dg_ref, dg, pl.program_id(0) == 0)

    in_specs = [_rows(t, w), _full((1, w)), _rows(t, w)] + ([_rows(t, w)] if has_res else [])
    args = [x, g, dh] + ([res] if has_res else [])
    return _pcall(body, grid=(n // t,), in_specs=in_specs, out_specs=[_rows(t, w), _rows(t, w), _full((1, w))],
                  out_shape=[SDS((n, w), F32), SDS((n, w), BF), SDS((1, w), F32)], sem=("arbitrary",), name=name,
                  comm=comm)(*args)


def _loss_call(y, tgt, name):
    n, w = y.shape
    t = min(ROW_TILE, n)

    def body(y_ref, t_ref, dy_ref, dyb_ref, l_ref):
        e = y_ref[...] - t_ref[...]
        dy = e * (1.0 / w)
        dy_ref[...] = dy
        dyb_ref[...] = dy.astype(BF)
        part = jnp.sum(jnp.sum(e * e, axis=-1, keepdims=True), axis=0, keepdims=True) * (0.5 / w)
        _acc(l_ref, jnp.broadcast_to(part, (8, LANES)), pl.program_id(0) == 0)

    return pl.pallas_call(body, grid=(n // t,), in_specs=[_rows(t, w), _rows(t, w)],
                          out_specs=[_rows(t, w), _rows(t, w), _full((8, LANES))],
                          out_shape=[SDS((n, w), F32), SDS((n, w), BF), SDS((8, LANES), F32)],
                          compiler_params=_params(("arbitrary",)), name=name)(y, tgt)


def _merge_core(zg0, zg1, zg2, y0, y1, y2):
    return jax.nn.sigmoid(zg0) * y0 + jax.nn.sigmoid(zg1) * y1 + jax.nn.sigmoid(zg2) * y2


def _merge_fwd(z, y_gm, y_mla, y_mem, name):
    n = z.shape[0]
    t = min(ROW_TILE, n)
    w = D_MODEL

    def body(g0, g1, g2, y0, y1, y2, o_ref):
        o_ref[...] = _merge_core(g0[...], g1[...], g2[...], y0[...], y1[...], y2[...]).astype(BF)

    return pl.pallas_call(body, grid=(n // t,),
                          in_specs=[_rows(t, w, 0), _rows(t, w, 1), _rows(t, w, 2)] + [_rows(t, w)] * 3,
                          out_specs=_rows(t, w), out_shape=SDS((n, w), BF),
                          compiler_params=_params(("parallel",)), name=name)(z, z, z, y_gm, y_mla, y_mem)


def _merge_bwd(z, y_gm, y_mla, y_mem, dmerged, name):
    n = z.shape[0]
    t = min(ROW_TILE, n)
    w = D_MODEL

    def body(g0, g1, g2, y0, y1, y2, dm, dzg_ref, d0_ref, d1_ref, d2_ref):
        _, vjp = jax.vjp(_merge_core, g0[...], g1[...], g2[...], y0[...], y1[...], y2[...])
        dg0, dg1, dg2, dy0, dy1, dy2 = vjp(dm[...])
        dzg_ref[:, 0:w] = dg0.astype(BF)
        dzg_ref[:, w:2 * w] = dg1.astype(BF)
        dzg_ref[:, 2 * w:3 * w] = dg2.astype(BF)
        d0_ref[...] = dy0.astype(BF)
        d1_ref[...] = dy1.astype(BF)
        d2_ref[...] = dy2.astype(BF)

    return pl.pallas_call(body, grid=(n // t,),
                          in_specs=[_rows(t, w, 0), _rows(t, w, 1), _rows(t, w, 2)] + [_rows(t, w)] * 4,
                          out_specs=[_rows(t, 3 * w)] + [_rows(t, w)] * 3,
                          out_shape=[SDS((n, 3 * w), BF)] + [SDS((n, w), BF)] * 3,
                          compiler_params=_params(("parallel",)), name=name)(z, z, z, y_gm, y_mla, y_mem, dmerged)


def _gm_core(zu, zv, g_ln, b_ln, ws, bcols):
    t = zu.shape[0]
    u = jax.nn.gelu(zu)
    v = _layernorm(jax.nn.gelu(zv), g_ln, b_ln)
    row = lax.broadcasted_iota(jnp.int32, (GM_CHUNK, GM_CHUNK), 0)
    col = lax.broadcasted_iota(jnp.int32, (GM_CHUNK, GM_CHUNK), 1)
    wc = [jnp.where(row >= col, ws[g], 0.0) for g in range(GM_GROUPS)]
    chunks = []
    for c in range(t // GM_CHUNK):
        cols = []
        for g in range(GM_GROUPS):
            vc = v[c * GM_CHUNK:(c + 1) * GM_CHUNK, g * LANES:(g + 1) * LANES]
            cols.append(_mm_nn(wc[g], vc) + bcols[g])
        chunks.append(jnp.concatenate(cols, axis=1))
    mixed = chunks[0] if len(chunks) == 1 else jnp.concatenate(chunks, axis=0)
    return u * mixed


def _gm_specs(t):
    return [_rows(t, GM_WIDTH, ZU // GM_WIDTH), _rows(t, GM_WIDTH, ZV // GM_WIDTH), _full((1, GM_WIDTH)),
            _full((1, GM_WIDTH)), _full((GM_GROUPS, GM_CHUNK, GM_CHUNK))] + [_full((GM_CHUNK, 1))] * GM_GROUPS


def _gm_fwd(z, g_ln, b_ln, ws, bcols, name):
    n = z.shape[0]
    t = min(ROW_TILE, n)

    def body(zu, zv, g_ref, b_ref, ws_ref, c0, c1, c2, c3, o_ref):
        out = _gm_core(zu[...], zv[...], g_ref[...], b_ref[...], [ws_ref[g] for g in range(GM_GROUPS)],
                       [c0[...], c1[...], c2[...], c3[...]])
        o_ref[...] = out.astype(BF)

    return pl.pallas_call(body, grid=(n // t,), in_specs=_gm_specs(t), out_specs=_rows(t, GM_WIDTH),
                          out_shape=SDS((n, GM_WIDTH), BF), compiler_params=_params(("parallel",)),
                          name=name)(z, z, g_ln, b_ln, ws, *bcols)


def _gm_bwd(z, g_ln, b_ln, ws, bcols, dgm, name, comm=None):
    n = z.shape[0]
    t = min(ROW_TILE, n)

    def body(zu, zv, g_ref, b_ref, ws_ref, c0, c1, c2, c3, dgm_ref, dz_ref, dg_ref, db_ref, dws_ref, e0, e1, e2, e3):
        first = pl.program_id(0) == 0
        _, vjp = jax.vjp(_gm_core, zu[...], zv[...], g_ref[...], b_ref[...],
                         [ws_ref[g] for g in range(GM_GROUPS)], [c0[...], c1[...], c2[...], c3[...]])
        dzu, dzv, dg, db, dws, dcols = vjp(dgm_ref[...])
        dz_ref[:, 0:GM_WIDTH] = dzu.astype(BF)
        dz_ref[:, GM_WIDTH:2 * GM_WIDTH] = dzv.astype(BF)
        _acc(dg_ref, dg, first)
        _acc(db_ref, db, first)
        _acc(dws_ref, jnp.stack(dws, axis=0), first)
        for ref, val in zip((e0, e1, e2, e3), dcols):
            _acc(ref, val, first)

    return _pcall(
        body, grid=(n // t,), in_specs=_gm_specs(t) + [_rows(t, GM_WIDTH)],
        out_specs=[_rows(t, 2 * GM_WIDTH), _full((1, GM_WIDTH)), _full((1, GM_WIDTH)),
                   _full((GM_GROUPS, GM_CHUNK, GM_CHUNK))] + [_full((GM_CHUNK, 1))] * GM_GROUPS,
        out_shape=[SDS((n, 2 * GM_WIDTH), BF), SDS((1, GM_WIDTH), F32), SDS((1, GM_WIDTH), F32),
                   SDS((GM_GROUPS, GM_CHUNK, GM_CHUNK), F32)] + [SDS((GM_CHUNK, 1), F32)] * GM_GROUPS,
        sem=("arbitrary",), name=name, comm=comm)(z, z, g_ln, b_ln, ws, *bcols, dgm)


def _rope_tables(pos_f, inv_full, cmask, smask, name):
    n = pos_f.shape[0]
    t = min(ROW_TILE, n)

    def body(p_ref, inv_ref, cm_ref, sm_ref, cos_ref, sin_ref):
        ang = p_ref[...] * inv_ref[...]
        cos_ref[...] = jnp.cos(ang) * cm_ref[...]
        sin_ref[...] = jnp.sin(ang) * sm_ref[...]

    return pl.pallas_call(body, grid=(n // t,), in_specs=[_rows(t, 1)] + [_full((1, LANES))] * 3,
                          out_specs=[_rows(t, LANES)] * 2, out_shape=[SDS((n, LANES), F32)] * 2,
                          compiler_params=_params(("parallel",)), name=name)(pos_f, inv_full, cmask, smask)


def _prep_core(cq, kpe, ckv, gains, wqn, wqp, wkn, wv, cos_f, sin_s, swap):
    g_cq, g_ckv, g_qn, g_qp, g_kn, g_kp = gains
    cqn = _rmsn(cq, g_cq, Q_LORA)
    ckvn = _rmsn(ckv, g_ckv, KV_LORA)
    kp = _rope(_rmsn(kpe, g_kp, MLA_ROPE), cos_f, sin_s, swap)
    qs, ks, vs = [], [], []
    for h in range(MLA_HEADS):
        qs.append(_rmsn(_mm_nn(cqn, wqn[h]), g_qn, MLA_NOPE))
        qs.append(_rope(_rmsn(_mm_nn(cqn, wqp[h]), g_qp, MLA_ROPE), cos_f, sin_s, swap))
        ks.append(_rmsn(_mm_nn(ckvn, wkn[h]), g_kn, MLA_NOPE))
        ks.append(kp)
        vs.append(_mm_nn(ckvn, wv[h]))
    return jnp.concatenate(qs, axis=1), jnp.concatenate(ks, axis=1), jnp.concatenate(vs, axis=1)


def _prep_in_specs(t):
    return ([_rows(t, Q_LORA, CQ // Q_LORA), _rows(t, LANES, KPE // LANES), _rows(t, KV_LORA, CKV // KV_LORA),
             _rows(t, LANES), _rows(t, LANES), _full((1, Q_LORA)), _full((1, KV_LORA))] + [_full((1, LANES))] * 4
            + [_full((Q_LORA, 2048)), _full((KV_LORA, 2048)), _full((LANES, LANES))])


def _prep_load(refs):
    cq, kpe, ckv, cos_f, sin_s, g_cq, g_ckv, g_qn, g_qp, g_kn, g_kp, wq, wkv, swap = refs
    hs = range(MLA_HEADS)
    wqn = [wq[:, h * LANES:(h + 1) * LANES].astype(F32) for h in hs]
    wqp = [wq[:, 1024 + h * LANES:1024 + (h + 1) * LANES].astype(F32) for h in hs]
    wkn = [wkv[:, h * LANES:(h + 1) * LANES].astype(F32) for h in hs]
    wv = [wkv[:, 1024 + h * LANES:1024 + (h + 1) * LANES].astype(F32) for h in hs]
    gains = [g_cq[...], g_ckv[...], g_qn[...], g_qp[...], g_kn[...], g_kp[...]]
    return (cq[...], kpe[...], ckv[...], gains, wqn, wqp, wkn, wv), (cos_f[...], sin_s[...], swap[...])


def _prep_fwd(z, cos_f, sin_s, gains, wq, wkv, swap, name):
    n = z.shape[0]
    t = min(ROW_TILE, n)

    def body(*refs):
        diff, const = _prep_load(refs[:14])
        q, k, v = _prep_core(*diff, *const)
        q_ref, k_ref, v_ref = refs[14:]
        q_ref[...] = q.astype(BF)
        k_ref[...] = k.astype(BF)
        v_ref[...] = v.astype(BF)

    return pl.pallas_call(body, grid=(n // t,), in_specs=_prep_in_specs(t),
                          out_specs=[_rows(t, 2048), _rows(t, 2048), _rows(t, 1024)],
                          out_shape=[SDS((n, 2048), BF), SDS((n, 2048), BF), SDS((n, 1024), BF)],
                          compiler_params=_params(("parallel",)),
                          name=name)(z, z, z, cos_f, sin_s, *gains, wq, wkv, swap)


def _prep_bwd(z, cos_f, sin_s, gains, wq, wkv, swap, dq, dk, dv, name, comm=None):
    n = z.shape[0]
    t = min(ROW_TILE, n)
    wz = Q_LORA + LANES + KV_LORA

    def body(*refs):
        diff, const = _prep_load(refs[:14])
        dq_ref, dk_ref, dv_ref = refs[14:17]
        dz_ref, o_cq, o_ckv, o_qn, o_qp, o_kn, o_kp, dwq_ref, dwkv_ref = refs[17:]
        first = pl.program_id(0) == 0
        _, vjp = jax.vjp(lambda *d: _prep_core(*d, *const), *diff)
        dcq, dkpe, dckv, dgains, dwqn, dwqp, dwkn, dwv = vjp((dq_ref[...], dk_ref[...], dv_ref[...]))
        dz_ref[:, 0:Q_LORA] = dcq.astype(BF)
        dz_ref[:, Q_LORA:Q_LORA + LANES] = dkpe.astype(BF)
        dz_ref[:, Q_LORA + LANES:wz] = dckv.astype(BF)
        for ref, val in zip((o_cq, o_ckv, o_qn, o_qp, o_kn, o_kp), dgains):
            _acc(ref, val, first)
        _acc(dwq_ref, jnp.concatenate(dwqn + dwqp, axis=1), first)
        _acc(dwkv_ref, jnp.concatenate(dwkn + dwv, axis=1), first)

    gain_specs = [_full((1, Q_LORA)), _full((1, KV_LORA))] + [_full((1, LANES))] * 4
    gain_shapes = [SDS((1, Q_LORA), F32), SDS((1, KV_LORA), F32)] + [SDS((1, LANES), F32)] * 4
    return _pcall(
        body, grid=(n // t,), in_specs=_prep_in_specs(t) + [_rows(t, 2048), _rows(t, 2048), _rows(t, 1024)],
        out_specs=[_rows(t, wz)] + gain_specs + [_full((Q_LORA, 2048)), _full((KV_LORA, 2048))],
        out_shape=[SDS((n, wz), BF)] + gain_shapes + [SDS((Q_LORA, 2048), F32), SDS((KV_LORA, 2048), F32)],
        sem=("arbitrary",), name=name, comm=comm)(z, z, z, cos_f, sin_s, *gains, wq, wkv, swap, dq, dk, dv)


MLA_QK = 256
MLA_SCALE = 1.0 / math.sqrt(MLA_NOPE + MLA_ROPE)


def _causal_mask(s, q0, k0):
    tq, tk = s.shape
    row = q0 + lax.broadcasted_iota(jnp.int32, (tq, tk), 0)
    col = k0 + lax.broadcasted_iota(jnp.int32, (tq, tk), 1)
    return jnp.where(row >= col, s, -jnp.inf)


def _mla_fwd(q, k, v, batch, seq, name):
    n = q.shape[0]
    tq = min(ATT_TILE, seq)
    nq = seq // tq

    def body(q_ref, k_ref, v_ref, o_ref, lse_ref):
        i = pl.program_id(2)
        qb = q_ref[...]

        def step(j, carry):
            m, l, acc = carry
            k0 = pl.multiple_of(j * tq, tq)
            kb = k_ref[pl.ds(k0, tq), :]
            vb = v_ref[pl.ds(k0, tq), :]
            s = _causal_mask(_dn(qb, kb, 1, 1) * MLA_SCALE, i * tq, k0)
            m_new = jnp.maximum(m, jnp.max(s, axis=-1, keepdims=True))
            p = jnp.exp(s - m_new)
            alpha = jnp.exp(m - m_new)
            l = alpha * l + jnp.sum(p, axis=-1, keepdims=True)
            acc = alpha * acc + _dn(p, vb, 1, 0)
            return m_new, l, acc

        init = (jnp.full((tq, 1), -jnp.inf, F32), jnp.zeros((tq, 1), F32), jnp.zeros((tq, MLA_V), F32))
        m, l, acc = lax.fori_loop(0, i + 1, step, init)
        o_ref[...] = acc / l
        lse_ref[...] = jnp.broadcast_to(m + jnp.log(l), (tq, LANES))

    return pl.pallas_call(
        body, grid=(batch, MLA_HEADS, nq),
        in_specs=[pl.BlockSpec((tq, MLA_QK), lambda b, h, i: (b * nq + i, h)),
                  pl.BlockSpec((seq, MLA_QK), lambda b, h, i: (b, h)),
                  pl.BlockSpec((seq, MLA_V), lambda b, h, i: (b, h))],
        out_specs=[pl.BlockSpec((tq, MLA_V), lambda b, h, i: (b * nq + i, h)),
                   pl.BlockSpec((tq, LANES), lambda b, h, i: (b * nq + i, h))],
        out_shape=[SDS((n, MLA_HEADS * MLA_V), F32), SDS((n, MLA_HEADS * LANES), F32)],
        compiler_params=_params(("parallel", "parallel", "arbitrary")), name=name)(q, k, v)


def _mla_bwd(q, k, v, o, lse, do, batch, seq, name, comm=None):
    n = q.shape[0]
    tk = min(ATT_TILE, seq)
    nk = seq // tk

    def body(q_ref, k_ref, v_ref, o_ref, lse_ref, do_ref, dq_ref, dk_ref, dv_ref):
        jk = pl.program_id(2)
        kb = k_ref[...]
        vb = v_ref[...]

        @pl.when(jk == 0)
        def _():
            dq_ref[...] = jnp.zeros_like(dq_ref)

        def step(i, carry):
            dk_acc, dv_acc = carry
            q0 = pl.multiple_of(i * tk, tk)
            rows = pl.ds(q0, tk)
            qb = q_ref[rows, :]
            dob = do_ref[rows, :]
            delta = jnp.sum(dob * o_ref[rows, :], axis=-1, keepdims=True)
            s = _causal_mask(_dn(qb, kb, 1, 1) * MLA_SCALE, q0, jk * tk)
            p = jnp.exp(s - lse_ref[rows, :][:, 0:1])
            dv_acc = dv_acc + _dn(p, dob, 0, 0)
            dp = _dn(dob, vb, 1, 1)
            ds = p * (dp - delta) * MLA_SCALE
            dk_acc = dk_acc + _dn(ds, qb, 0, 0)
            dq_ref[rows, :] += _dn(ds, kb, 1, 0)
            return dk_acc, dv_acc

        dk_acc, dv_acc = lax.fori_loop(jk, nk, step, (jnp.zeros((tk, MLA_QK), F32), jnp.zeros((tk, MLA_V), F32)))
        dk_ref[...] = dk_acc
        dv_ref[...] = dv_acc

    full_qk = pl.BlockSpec((seq, MLA_QK), lambda b, h, j: (b, h))
    full_v = pl.BlockSpec((seq, MLA_V), lambda b, h, j: (b, h))
    blk_qk = pl.BlockSpec((tk, MLA_QK), lambda b, h, j: (b * nk + j, h))
    blk_v = pl.BlockSpec((tk, MLA_V), lambda b, h, j: (b * nk + j, h))
    return _pcall(
        body, grid=(batch, MLA_HEADS, nk),
        in_specs=[full_qk, blk_qk, blk_v, full_v, full_v, full_v],
        out_specs=[full_qk, blk_qk, blk_v],
        out_shape=[SDS((n, MLA_HEADS * MLA_QK), F32), SDS((n, MLA_HEADS * MLA_QK), F32),
                   SDS((n, MLA_HEADS * MLA_V), F32)],
        sem=("parallel", "parallel", "arbitrary"), name=name, comm=comm)(q, k, v, o, lse, do)


MEM_SCALE = 1.0 / math.sqrt(HEAD_DIM)
MEM_W = MEM_HEADS * HEAD_DIM


def _mem_core(qs, ks, vs, g_mq, g_mk):
    outs = []
    for h in range(MEM_HEADS):
        qh = _rmsn(qs[h], g_mq, HEAD_DIM)
        kh = _rmsn(ks[h], g_mk, HEAD_DIM)
        p = _softmax(_mm_nt(qh, kh) * MEM_SCALE)
        outs.append(_mm_nn(p, vs[h]))
    return jnp.concatenate(outs, axis=1)


def _mem_load(qm, kvm, g_mq, g_mk):
    hs = range(MEM_HEADS)
    qs = [qm[:, h * LANES:(h + 1) * LANES] for h in hs]
    ks = [kvm[:, h * LANES:(h + 1) * LANES] for h in hs]
    vs = [kvm[:, MEM_W + h * LANES:MEM_W + (h + 1) * LANES] for h in hs]
    return qs, ks, vs, g_mq[...], g_mk[...]


def _mem_fwd(z, kvm, g_mq, g_mk, batch, seq, name):
    n = z.shape[0]
    t = min(ROW_TILE, seq)
    per = seq // t

    def body(qm, kvm_ref, gq, gk, o_ref):
        o_ref[...] = _mem_core(*_mem_load(qm, kvm_ref, gq, gk)).astype(BF)

    return pl.pallas_call(
        body, grid=(n // t,),
        in_specs=[_rows(t, MEM_W, QM // MEM_W), pl.BlockSpec((MEM_LEN, 2 * MEM_W), lambda i: (i // per, 0)),
                  _full((1, LANES)), _full((1, LANES))],
        out_specs=_rows(t, MEM_W), out_shape=SDS((n, MEM_W), BF),
        compiler_params=_params(("parallel",)), name=name)(z, kvm, g_mq, g_mk)


def _mem_bwd(z, kvm, g_mq, g_mk, dom, batch, seq, name):
    n = z.shape[0]
    t = min(ROW_TILE, seq)
    per = seq // t

    def body(qm, kvm_ref, gq, gk, dom_ref, dz_ref, dkvm_ref, dgq_ref, dgk_ref):
        i = pl.program_id(0)
        _, vjp = jax.vjp(_mem_core, *_mem_load(qm, kvm_ref, gq, gk))
        dqs, dks, dvs, dgq, dgk = vjp(dom_ref[...])
        dz_ref[...] = jnp.concatenate(dqs, axis=1).astype(BF)
        _acc(dkvm_ref, jnp.concatenate(dks + dvs, axis=1), i % per == 0)
        _acc(dgq_ref, dgq, i == 0)
        _acc(dgk_ref, dgk, i == 0)

    kv_spec = pl.BlockSpec((MEM_LEN, 2 * MEM_W), lambda i: (i // per, 0))
    return pl.pallas_call(
        body, grid=(n // t,),
        in_specs=[_rows(t, MEM_W, QM // MEM_W), kv_spec, _full((1, LANES)), _full((1, LANES)), _rows(t, MEM_W)],
        out_specs=[_rows(t, MEM_W), kv_spec, _full((1, LANES)), _full((1, LANES))],
        out_shape=[SDS((n, MEM_W), BF), SDS((batch * MEM_LEN, 2 * MEM_W), F32), SDS((1, LANES), F32),
                   SDS((1, LANES), F32)],
        compiler_params=_params(("arbitrary",)), name=name)(z, kvm, g_mq, g_mk, dom)


def _me():
    return lax.axis_index("x"), lax.axis_index("y"), lax.axis_index("c")


def _other_chips(x, y):
    return [(1 - x, y), (x, 1 - y), (1 - x, 1 - y)]


def _shard_shape(name):
    r, c = BIG_SHAPE[name]
    return (r, c // N_CHIPS) if name in COL_SHARDED else (r // N_CHIPS, c)


def _n_pieces(half_rows):
    return max(1, half_rows // PIECE_ROWS)


def _piece_plan(shapes):
    plan = []
    for r, _ in shapes:
        h = r // 2
        n = _n_pieces(h)
        plan.append((h, n, h // n))
    return plan


def _remote(send, recv, sem, src, dst, to):
    return pltpu.make_async_remote_copy(src_ref=src, dst_ref=dst, send_sem=send.at[sem], recv_sem=recv.at[sem],
                                        device_id=to, device_id_type=MESH)


def _gather_far(shards):
    plan = _piece_plan([s.shape for s in shards])
    n_far = 3 * sum(n for _, n, _ in plan)
    n_loc = 2 * sum(n for _, n, _ in plan)

    def copies(s_refs, o_refs, send, recv, local):
        x, y, c = _me()
        k = 2 * x + y
        mine, sends, arrivals = [], [], []
        for t, (h, n, pr) in enumerate(plan):
            s_ref, o_ref = s_refs[t], o_refs[t]
            for core in range(2):
                for p in range(n):
                    rows = pl.ds(core * h + p * pr, pr)
                    mine.append(pltpu.make_async_copy(s_ref.at[rows], o_ref.at[k, rows], local.at[len(mine)]))
            for chip in _other_chips(x, y):
                for p in range(n):
                    rows = pl.ds(c * h + p * pr, pr)
                    s = len(sends)
                    sends.append(_remote(send, recv, s, s_ref.at[rows], o_ref.at[k, rows], (*chip, c)))
                    arrivals.append(_remote(send, recv, s, s_ref.at[rows], o_ref.at[2 * chip[0] + chip[1], rows],
                                            (*chip, c)))
        return sends, arrivals, mine

    return _Phase(shards, [SDS((N_CHIPS,) + s.shape, s.dtype) for s in shards], n_far, n_loc, copies)


def _gather_near(bufs):
    plan = _piece_plan([b.shape[1:] for b in bufs])
    n_sem = 3 * sum(n for _, n, _ in plan)

    def copies(i_refs, o_refs, send, recv, local):
        x, y, c = _me()
        sib = (x, y, 1 - c)
        sends, arrivals = [], []
        for t, (h, n, pr) in enumerate(plan):
            for chip in _other_chips(x, y):
                ci = 2 * chip[0] + chip[1]
                for p in range(n):
                    rows = pl.ds(c * h + p * pr, pr)
                    rows_sib = pl.ds((1 - c) * h + p * pr, pr)
                    s = len(sends)
                    sends.append(_remote(send, recv, s, i_refs[t].at[ci, rows], o_refs[t].at[ci, rows], sib))
                    arrivals.append(_remote(send, recv, s, i_refs[t].at[ci, rows_sib], o_refs[t].at[ci, rows_sib], sib))
        return sends, arrivals, []

    return _Phase(bufs, [SDS(b.shape, b.dtype) for b in bufs], n_sem, 0, copies, {t: t for t in range(len(bufs))})


def _pair_exchange(grads):
    plan = _piece_plan([g.shape[1:] for g in grads])
    n_sem = sum(n for _, n, _ in plan)

    def copies(g_refs, o_refs, send, recv, local):
        x, y, c = _me()
        sends = []
        for t, (h, n, pr) in enumerate(plan):
            for p in range(n):
                sends.append(_remote(send, recv, len(sends), g_refs[t].at[:, pl.ds((1 - c) * h + p * pr, pr)],
                                     o_refs[t].at[:, pl.ds(p * pr, pr)], (x, y, 1 - c)))
        return sends, sends, []

    return _Phase(grads, [SDS((N_CHIPS, g.shape[1] // 2, g.shape[2]), F32) for g in grads], n_sem, 0, copies)


def _pair_add(ck, g, theirs, name):
    _, r, c = g.shape
    (h, n, pr), = _piece_plan([(r, c)])

    def body(ck_ref, g_ref, t_ref, p32_ref, pbf_ref):
        s = g_ref[...] + t_ref[...]
        p32_ref[...] = s
        pbf_ref[...] = s.astype(BF)

    half = pl.BlockSpec((None, pr, c), lambda k, p, ck: (k, p, 0))
    spec = pltpu.PrefetchScalarGridSpec(
        num_scalar_prefetch=1, grid=(N_CHIPS, n),
        in_specs=[pl.BlockSpec((None, pr, c), lambda k, p, ck: (k, ck[0] * n + p, 0)), half], out_specs=[half, half])
    return pl.pallas_call(body, grid_spec=spec, out_shape=[SDS((N_CHIPS, h, c), F32), SDS((N_CHIPS, h, c), BF)],
                          compiler_params=_params(("arbitrary", "arbitrary")), name=name)(ck, g, theirs)


def _scatter_partials(pbfs):
    plan = [(h, _n_pieces(h), h // _n_pieces(h)) for h in [p.shape[1] for p in pbfs]]
    n_sem = 3 * sum(n for _, n, _ in plan)

    def copies(p_refs, o_refs, send, recv, local):
        x, y, c = _me()
        sends = []
        for t, (h, n, pr) in enumerate(plan):
            for j, chip in enumerate(_other_chips(x, y)):
                for p in range(n):
                    rows = pl.ds(p * pr, pr)
                    sends.append(_remote(send, recv, len(sends), p_refs[t].at[2 * chip[0] + chip[1], rows],
                                         o_refs[t].at[j, rows], (*chip, c)))
        return sends, sends, []

    return _Phase(pbfs, [SDS((3,) + p.shape[1:], BF) for p in pbfs], n_sem, 0, copies)


def _sum_chips(ck, p32, slots, name):
    _, h, c = p32.shape
    n = _n_pieces(h)
    pr = h // n

    def body(ck_ref, p_ref, s_ref, o_ref):
        o_ref[...] = ((p_ref[...] + s_ref[0].astype(F32)) + s_ref[1].astype(F32)) + s_ref[2].astype(F32)

    spec = pltpu.PrefetchScalarGridSpec(
        num_scalar_prefetch=1, grid=(n,),
        in_specs=[pl.BlockSpec((None, pr, c), lambda p, ck: (ck[1], p, 0)),
                  pl.BlockSpec((3, pr, c), lambda p, ck: (0, p, 0))],
        out_specs=pl.BlockSpec((pr, c), lambda p, ck: (ck[0] * n + p, 0)))
    return pl.pallas_call(body, grid_spec=spec, out_shape=SDS((2 * h, c), F32),
                          compiler_params=_params(("arbitrary",)), name=name)(ck, p32, slots)


def _join_halves(sums):
    plan = _piece_plan([s.shape for s in sums])
    n_sem = sum(n for _, n, _ in plan)

    def copies(r_refs, o_refs, send, recv, local):
        x, y, c = _me()
        sends, arrivals = [], []
        for t, (h, n, pr) in enumerate(plan):
            for p in range(n):
                rows = pl.ds(c * h + p * pr, pr)
                rows_sib = pl.ds((1 - c) * h + p * pr, pr)
                s = len(sends)
                sends.append(_remote(send, recv, s, r_refs[t].at[rows], o_refs[t].at[rows], (x, y, 1 - c)))
                arrivals.append(_remote(send, recv, s, r_refs[t].at[rows_sib], o_refs[t].at[rows_sib], (x, y, 1 - c)))
        return sends, arrivals, []

    return _Phase(sums, [SDS(s.shape, F32) for s in sums], n_sem, 0, copies, {t: t for t in range(len(sums))})


def _gather_small(s, name):
    def body(s_ref, o_ref, send, recv, local):
        x, y, c = _me()
        me = 4 * x + 2 * y + c
        keep = pltpu.make_async_copy(s_ref, o_ref.at[me], local)
        keep.start()
        sends = []
        for r in range(1, 8):
            fx, fy, fc = (r >> 2) & 1, (r >> 1) & 1, r & 1
            to = (x ^ fx, y ^ fy, c ^ fc)
            sends.append(pltpu.make_async_remote_copy(
                src_ref=s_ref, dst_ref=o_ref.at[me], send_sem=send.at[r - 1], recv_sem=recv.at[r - 1],
                device_id=to, device_id_type=MESH))
        for cp in sends:
            cp.start()
        for r in range(1, 8):
            fx, fy, fc = (r >> 2) & 1, (r >> 1) & 1, r & 1
            src = 4 * (x ^ fx) + 2 * (y ^ fy) + (c ^ fc)
            pltpu.make_async_remote_copy(
                src_ref=s_ref, dst_ref=o_ref.at[src], send_sem=send.at[r - 1], recv_sem=recv.at[r - 1],
                device_id=(x ^ fx, y ^ fy, c ^ fc), device_id_type=MESH).wait_recv()
        for cp in sends:
            cp.wait_send()
        keep.wait()

    return pl.pallas_call(
        body, in_specs=[ANY], out_specs=ANY, out_shape=SDS((8, SMALL_ROWS, LANES), F32),
        scratch_shapes=[pltpu.SemaphoreType.DMA((7,)), pltpu.SemaphoreType.DMA((7,)), pltpu.SemaphoreType.DMA],
        name=name)(s)


def _adam_math(w, g, m, v):
    nm = ADAM_B1 * m + (1.0 - ADAM_B1) * g
    nv = ADAM_B2 * v + (1.0 - ADAM_B2) * (g * g)
    m_hat = nm / (1.0 - ADAM_B1 ** ADAM_STEP)
    v_hat = nv / (1.0 - ADAM_B2 ** ADAM_STEP)
    return -ADAM_LR * (m_hat / (jnp.sqrt(v_hat) + ADAM_EPS) + ADAM_WD * w), nm, nv


def _adamw(w, g, m, v, name):
    _, r, c = w.shape
    t = _pick(r, (256, 128, 64))

    def body(w_ref, g_ref, m_ref, v_ref, go_ref, d_ref, nm_ref, nv_ref):
        g_ = g_ref[...]
        d, nm, nv = _adam_math(w_ref[...], g_, m_ref[...], v_ref[...])
        go_ref[...] = g_
        d_ref[...] = d
        nm_ref[...] = nm
        nv_ref[...] = nv

    lead = pl.BlockSpec((None, t, c), lambda i: (0, i, 0))
    return pl.pallas_call(body, grid=(r // t,), in_specs=[lead, _rows(t, c), lead, lead], out_specs=[lead] * 4,
                          out_shape=[SDS((1, r, c), F32)] * 4, compiler_params=_params(("parallel",)),
                          name=name)(w, g, m, v)


def _small_layout():
    out, r0 = {}, 0
    for n in SMALL:
        size = int(np.prod(SMALL_SHAPE[n]))
        nr = -(-size // LANES)
        out[n] = (r0, nr)
        r0 += nr
    assert r0 <= SMALL_ROWS
    return out, r0


def _pack_small(grads, name):
    layout, used = _small_layout()

    def body(*refs):
        o_ref = refs[-1]
        for n, ref in zip(SMALL, refs[:-1]):
            r0, nr = layout[n]
            if n == "w_spatial":
                for g in range(GM_GROUPS):
                    o_ref[r0 + g * GM_CHUNK:r0 + (g + 1) * GM_CHUNK, :] = ref[g]
            elif n == "b_spatial":
                o_ref[r0:r0 + nr, :] = ref[...]
            else:
                for i in range(nr):
                    o_ref[r0 + i:r0 + i + 1, :] = ref[:, i * LANES:(i + 1) * LANES]
        o_ref[used:SMALL_ROWS, :] = jnp.zeros((SMALL_ROWS - used, LANES), F32)

    return pl.pallas_call(body, out_shape=SDS((SMALL_ROWS, LANES), F32), name=name)(*grads)


def _adamw_small(gathered, ws, ms, vs, name):
    layout, _ = _small_layout()
    n_t = len(SMALL)

    def body(*refs):
        g_ref = refs[0]
        w_refs, m_refs, v_refs = refs[1:1 + n_t], refs[1 + n_t:1 + 2 * n_t], refs[1 + 2 * n_t:1 + 3 * n_t]
        outs = refs[1 + 3 * n_t:1 + 7 * n_t]
        acc = refs[-1]
        total = g_ref[0]
        for j in range(1, 8):
            total = total + g_ref[j]
        acc[...] = total
        for t, n in enumerate(SMALL):
            r0, nr = layout[n]
            o_refs = [outs[t], outs[n_t + t], outs[2 * n_t + t], outs[3 * n_t + t]]
            if n == "w_spatial":
                views = [((0, g), slice(r0 + g * GM_CHUNK, r0 + (g + 1) * GM_CHUNK), slice(None))
                         for g in range(GM_GROUPS)]
            elif n == "b_spatial":
                views = [((0,), slice(r0, r0 + nr), slice(None))]
            else:
                width = SMALL_SHAPE[n][1]
                views = [((slice(None), slice(i * LANES, min((i + 1) * LANES, width))), slice(r0 + i, r0 + i + 1),
                          slice(0, min(LANES, width - i * LANES))) for i in range(nr)]
            for idx, rows, lanes in views:
                g = acc[rows, lanes]
                d, nm, nv = _adam_math(w_refs[t][idx], g, m_refs[t][idx], v_refs[t][idx])
                for ref, val in zip(o_refs, (g, d, nm, nv)):
                    ref[idx] = val

    shapes = [SDS(SMALL_SHAPE[n], F32) for n in SMALL]
    return pl.pallas_call(body, out_shape=shapes * 4, scratch_shapes=[pltpu.VMEM((SMALL_ROWS, LANES), F32)],
                          name=name)(gathered, *ws, *ms, *vs)


def _win_layout(w_in):
    pad = jnp.zeros((w_in.shape[0], LANES - MLA_ROPE), w_in.dtype)
    u, v, cq = w_in[:, 0:512], w_in[:, 512:1024], w_in[:, 1024:1408]
    ckv, kpe, qm, zg = w_in[:, 1408:1664], w_in[:, 1664:1728], w_in[:, 1728:2240], w_in[:, 2240:5312]
    return jnp.concatenate([zg, u, v, qm, cq, kpe, pad, ckv], axis=1)


def _win_unlayout(g):
    zg, u, v, qm = g[:, ZG:ZG + 3072], g[:, ZU:ZU + 512], g[:, ZV:ZV + 512], g[:, QM:QM + 512]
    cq, kpe, ckv = g[:, CQ:CQ + 384], g[:, KPE:KPE + MLA_ROPE], g[:, CKV:CKV + 256]
    return jnp.concatenate([u, v, cq, ckv, kpe, qm, zg], axis=1)


def _wq_layout(w_uq):
    w = w_uq.reshape(Q_LORA, MLA_HEADS, MLA_NOPE + MLA_ROPE)
    nope = w[:, :, :MLA_NOPE].reshape(Q_LORA, MLA_HEADS * MLA_NOPE)
    pe = jnp.pad(w[:, :, MLA_NOPE:], ((0, 0), (0, 0), (0, LANES - MLA_ROPE))).reshape(Q_LORA, MLA_HEADS * LANES)
    return jnp.concatenate([nope, pe], axis=1)


def _wq_unlayout(g):
    nope = g[:, :1024].reshape(Q_LORA, MLA_HEADS, MLA_NOPE)
    pe = g[:, 1024:].reshape(Q_LORA, MLA_HEADS, LANES)[:, :, :MLA_ROPE]
    return jnp.concatenate([nope, pe], axis=2).reshape(Q_LORA, MLA_HEADS * (MLA_NOPE + MLA_ROPE))


def _wkv_layout(w_ukv):
    w = w_ukv.reshape(KV_LORA, MLA_HEADS, MLA_NOPE + MLA_V)
    return jnp.concatenate([w[:, :, :MLA_NOPE].reshape(KV_LORA, 1024), w[:, :, MLA_NOPE:].reshape(KV_LORA, 1024)],
                           axis=1)


def _wkv_unlayout(g):
    kn = g[:, :1024].reshape(KV_LORA, MLA_HEADS, MLA_NOPE)
    v = g[:, 1024:].reshape(KV_LORA, MLA_HEADS, MLA_V)
    return jnp.concatenate([kn, v], axis=2).reshape(KV_LORA, MLA_HEADS * (MLA_NOPE + MLA_V))


def _owner_major(g, name):
    r, c = _shard_shape(name)
    return g.reshape(r, N_CHIPS, c).transpose(1, 0, 2) if name in COL_SHARDED else g.reshape(N_CHIPS, r, c)


def _pad_lanes(g):
    return jnp.pad(g, ((0, 0), (0, LANES - g.shape[1])))


def kernel(x, mem, positions, g_mix, w_in, g_cq, w_uq, g_ckv, w_ukv, g_q_nope, g_q_pe, g_k_nope, g_k_pe, g_gm_ln, b_gm_ln, w_spatial, b_spatial, g_mem, w_mem_kv, g_mq, g_mk, w_o_gm, w_o_mla, w_o_mem, w_out, g_ffn, w_ff1, w_ff2, loss_target, m_g_mix, m_w_in, m_g_cq, m_w_uq, m_g_ckv, m_w_ukv, m_g_q_nope, m_g_q_pe, m_g_k_nope, m_g_k_pe, m_g_gm_ln, m_b_gm_ln, m_w_spatial, m_b_spatial, m_g_mem, m_w_mem_kv, m_g_mq, m_g_mk, m_w_o_gm, m_w_o_mla, m_w_o_mem, m_w_out, m_g_ffn, m_w_ff1, m_w_ff2, v_g_mix, v_w_in, v_g_cq, v_w_uq, v_g_ckv, v_w_ukv, v_g_q_nope, v_g_q_pe, v_g_k_nope, v_g_k_pe, v_g_gm_ln, v_b_gm_ln, v_w_spatial, v_b_spatial, v_g_mem, v_w_mem_kv, v_g_mq, v_g_mk, v_w_o_gm, v_w_o_mla, v_w_o_mem, v_w_out, v_g_ffn, v_w_ff1, v_w_ff2):
    given = dict(locals())
    wts = {n: given[n] for n in WEIGHTS}
    mom = {n: given["m_" + n] for n in WEIGHTS}
    var = {n: given["v_" + n] for n in WEIGHTS}
    batch, seq, _ = x.shape
    n_tok = batch * seq

    def natural(n, g):
        r, c = _shard_shape(n)
        return g.transpose(1, 0, 2).reshape(r, N_CHIPS * c) if n in COL_SHARDED else g.reshape(N_CHIPS * r, c)

    early = _run_phase(_gather_near(_run_phase(_gather_far([wts[n][0].astype(BF) for n in EARLY]), "gather_early")),
                       "gather_early_pass")
    full = {n: natural(n, g) for n, g in zip(EARLY, early)}
    win = _win_layout(full["w_in"])
    wq = _wq_layout(full["w_uq"])
    wkv = _wkv_layout(full["w_ukv"])

    x2 = x.reshape(n_tok, D_MODEL)
    tgt2 = loss_target.reshape(n_tok, D_MODEL)
    mem2 = mem.reshape(batch * MEM_LEN, D_MODEL)
    pos_f = positions.reshape(n_tok, 1).astype(F32)

    inv = ROPE_BASE ** (-jnp.arange(0, MLA_ROPE, 2, dtype=F32) / MLA_ROPE)
    zeros64 = jnp.zeros((LANES - MLA_ROPE,), F32)
    inv_full = jnp.concatenate([inv, inv, zeros64]).reshape(1, LANES)
    half = MLA_ROPE // 2
    cmask = jnp.concatenate([jnp.ones((MLA_ROPE,), F32), zeros64]).reshape(1, LANES)
    smask = jnp.concatenate([-jnp.ones((half,), F32), jnp.ones((half,), F32), zeros64]).reshape(1, LANES)
    swap_np = np.zeros((LANES, LANES), np.float32)
    for j in range(half):
        swap_np[j + half, j] = 1.0
        swap_np[j, j + half] = 1.0
    swap = jnp.asarray(swap_np)

    prep_gains = [g_cq, g_ckv, g_q_nope, _pad_lanes(g_q_pe), g_k_nope, _pad_lanes(g_k_pe)]
    ws = w_spatial[0]
    bcols = [b_spatial[0, g].reshape(GM_CHUNK, 1) for g in range(GM_GROUPS)]

    h1 = _rms_fwd(x2, g_mix, "rms_mix")
    z, late_far = _mm(h1, win, name="mm_in", comm=_gather_far([wts[n][0].astype(BF) for n in LATE]))
    gm = _gm_fwd(z, g_gm_ln, b_gm_ln, ws, bcols, "gm_fwd")
    cos_f, sin_s = _rope_tables(pos_f, inv_full, cmask, smask, "rope_tables")
    qc, kc, vc = _prep_fwd(z, cos_f, sin_s, prep_gains, wq, wkv, swap, "prep_fwd")
    o_mla, lse = _mla_fwd(qc, kc, vc, batch, seq, "mla_fwd")
    memn = _rms_fwd(mem2, g_mem, "rms_mem")
    kvm, late = _mm(memn, full["w_mem_kv"], name="mm_memkv", comm=_gather_near(late_far))
    full.update({n: natural(n, g) for n, g in zip(LATE, late)})
    o_mem = _mem_fwd(z, kvm, g_mq, g_mk, batch, seq, "mem_fwd")
    y_gm = _mm(gm, full["w_o_gm"], name="mm_o_gm")
    y_mla = _mm(o_mla, full["w_o_mla"], name="mm_o_mla")
    y_mem = _mm(o_mem, full["w_o_mem"], name="mm_o_mem")
    merged = _merge_fwd(z, y_gm, y_mla, y_mem, "merge_fwd")
    x1 = _mm(merged, full["w_out"], ins=(x2,), epilogue=_add_to, name="mm_out")
    h2 = _rms_fwd(x1, g_ffn, "rms_ffn")
    a_ff, r_ff = _mm(h2, full["w_ff1"], epilogue=_relu2, out_dtypes=(F32, BF), name="mm_ff1")
    y = _mm(r_ff, full["w_ff2"], ins=(x1,), epilogue=_add_to, name="mm_ff2")
    dy, dyb, loss_tile = _loss_call(y, tgt2, "loss")

    gw = {}
    da = _mm(dyb, full["w_ff2"], tb=True, ins=(a_ff,), epilogue=_relu2_bwd, out_dtypes=(BF,), name="mm_d_a")
    gw["w_ff2"] = _owner_major(_mm(r_ff, dyb, ta=True, name="mm_dw_ff2"), "w_ff2")
    gw["w_ff1"] = _mm(h2, da, ta=True, owner_cols=D_FF // N_CHIPS, name="mm_dw_ff1")
    dh2 = _mm(da, full["w_ff1"], tb=True, name="mm_d_h2")
    dx1, dx1b, dg_ffn = _rms_bwd(x1, g_ffn, dh2, dy, "rms_ffn_bwd")
    dmerged = _mm(dx1b, full["w_out"], tb=True, name="mm_d_merged")
    gw["w_out"] = _owner_major(_mm(merged, dx1b, ta=True, name="mm_dw_out"), "w_out")
    dzg, dy_gm, dy_mla, dy_mem = _merge_bwd(z, y_gm, y_mla, y_mem, dmerged, "merge_bwd")
    dgm = _mm(dy_gm, full["w_o_gm"], tb=True, name="mm_d_gm")
    gw["w_o_gm"] = _mm(gm, dy_gm, ta=True, owner_cols=D_MODEL // N_CHIPS, name="mm_dw_o_gm")
    do_mla = _mm(dy_mla, full["w_o_mla"], tb=True, name="mm_d_omla")
    gw["w_o_mla"] = _owner_major(_mm(o_mla, dy_mla, ta=True, name="mm_dw_o_mla"), "w_o_mla")
    do_mem = _mm(dy_mem, full["w_o_mem"], tb=True, name="mm_d_omem")
    gw["w_o_mem"] = _mm(o_mem, dy_mem, ta=True, owner_cols=D_MODEL // N_CHIPS, name="mm_dw_o_mem")
    ck = jnp.stack([lax.axis_index("c"), 2 * lax.axis_index("x") + lax.axis_index("y")]).astype(jnp.int32)

    def pair_sums(names, theirs):
        return [_pair_add(ck, gw[n], t, "pair_add_" + n) for n, t in zip(names, theirs)]

    def chip_sums(names, pairs, slots):
        return [_sum_chips(ck, p[0], s, "sum_chips_" + n) for n, p, s in zip(names, pairs, slots)]

    (dz_uv, dg_ln, db_ln, dws, *dbcols), theirs = _gm_bwd(z, g_gm_ln, b_gm_ln, ws, bcols, dgm, "gm_bwd",
                                                         comm=_pair_exchange([gw[n] for n in LATE]))
    pairs = pair_sums(LATE, theirs)
    (dq, dk, dv), slots = _mla_bwd(qc, kc, vc, o_mla, lse, do_mla, batch, seq, "mla_bwd",
                                   comm=_scatter_partials([p[1] for p in pairs]))
    sums = chip_sums(LATE, pairs, slots)
    (dz_mla, dg_cq, dg_ckv, dg_qn, dg_qp, dg_kn, dg_kp, dwq, dwkv), reduced_late = _prep_bwd(
        z, cos_f, sin_s, prep_gains, wq, wkv, swap, dq, dk, dv, "prep_bwd", comm=_join_halves(sums))
    dz_qm, dkvm, dg_mq, dg_mk = _mem_bwd(z, kvm, g_mq, g_mk, do_mem, batch, seq, "mem_bwd")
    dmemn = _mm(dkvm, full["w_mem_kv"], tb=True, name="mm_d_memn")
    gw["w_mem_kv"] = _owner_major(_mm(memn, dkvm, ta=True, name="mm_dw_memkv"), "w_mem_kv")
    _, _, dg_mem = _rms_bwd(mem2, g_mem, dmemn, None, "rms_mem_bwd")
    dz = jnp.concatenate([dzg, dz_uv, dz_qm, dz_mla], axis=1)
    gw["w_in"] = _owner_major(_win_unlayout(_mm(h1, dz, ta=True, name="mm_dw_in")), "w_in")
    gw["w_uq"] = _owner_major(_wq_unlayout(dwq), "w_uq")
    gw["w_ukv"] = _owner_major(_wkv_unlayout(dwkv), "w_ukv")
    dh1, theirs = _mm(dz, win, tb=True, name="mm_d_h1", comm=_pair_exchange([gw[n] for n in EARLY]))
    pairs = pair_sums(EARLY, theirs)
    (grad_x, _, dg_mix), slots = _rms_bwd(x2, g_mix, dh1, dx1, "rms_mix_bwd",
                                          comm=_scatter_partials([p[1] for p in pairs]))
    reduced_early = _run_phase(_join_halves(chip_sums(EARLY, pairs, slots)), "join_early")
    results = {n: _adamw(wts[n], g, mom[n], var[n], "adamw_" + n)
               for n, g in zip(LATE + EARLY, list(reduced_late) + list(reduced_early))}

    small_g = {"g_mix": dg_mix, "g_cq": dg_cq, "g_ckv": dg_ckv, "g_q_nope": dg_qn, "g_q_pe": dg_qp,
               "g_k_nope": dg_kn, "g_k_pe": dg_kp, "g_gm_ln": dg_ln, "b_gm_ln": db_ln, "w_spatial": dws,
               "b_spatial": jnp.concatenate(dbcols, axis=1).T, "g_mem": dg_mem, "g_mq": dg_mq, "g_mk": dg_mk,
               "g_ffn": dg_ffn}
    packed = _pack_small([small_g[n] for n in SMALL], "pack_small")
    small_out = _adamw_small(_gather_small(packed, "gather_small"), [wts[n] for n in SMALL],
                             [mom[n] for n in SMALL], [var[n] for n in SMALL], "adamw_small")
    for t, n in enumerate(SMALL):
        results[n] = [small_out[j * len(SMALL) + t] for j in range(4)]

    loss = lax.psum(loss_tile[0, 0], ("x", "y", "c"))
    grad_x = grad_x.reshape(batch, seq, D_MODEL)
    return (loss, grad_x, *[results[n][0] for n in WEIGHTS], *[results[n][1] for n in WEIGHTS],
            *[results[n][2] for n in WEIGHTS], *[results[n][3] for n in WEIGHTS])
```

```python
import functools
import math

import numpy as np
import jax
import jax.numpy as jnp
from jax import lax
from jax.experimental import pallas as pl
from jax.experimental.pallas import tpu as pltpu

F32 = jnp.float32
BF = jnp.bfloat16
SDS = jax.ShapeDtypeStruct
MESH = pl.DeviceIdType.MESH

D_MODEL = 1024
MEM_LEN = 256
MEM_HEADS = 4
HEAD_DIM = 128
GM_WIDTH = 512
GM_CHUNK = 128
GM_GROUPS = 4
MLA_HEADS = 8
MLA_NOPE = 128
MLA_ROPE = 64
MLA_V = 128
Q_LORA = 384
KV_LORA = 256
ROPE_BASE = 10000.0
D_FF = 4096
EPS = 1e-6
W_IN_COLS = 5312
ADAM_LR, ADAM_B1, ADAM_B2, ADAM_EPS, ADAM_WD, ADAM_STEP = 0.001, 0.9, 0.999, 1e-08, 0.01, 10

ZG, ZU, ZV, QM, CQ, KPE, CKV = 0, 3072, 3584, 4096, 4608, 4992, 5120
Z_COLS = 5376
LANES = 128
ROW_TILE = 256
ATT_TILE = 512
ATT_HEADS = 2
VMEM_LIMIT = 56 * 1024 * 1024

N_CHIPS = 4
PIECE_ROWS = 256
SMALL_ROWS = 560

BIG = ["w_in", "w_uq", "w_ukv", "w_mem_kv", "w_o_gm", "w_o_mla", "w_o_mem", "w_out", "w_ff1", "w_ff2"]
BIG_SHAPE = {"w_in": (1024, 5312), "w_uq": (384, 1536), "w_ukv": (256, 2048), "w_mem_kv": (1024, 1024),
             "w_o_gm": (512, 1024), "w_o_mla": (1024, 1024), "w_o_mem": (512, 1024), "w_out": (1024, 1024),
             "w_ff1": (1024, 4096), "w_ff2": (4096, 1024)}
COL_SHARDED = {"w_in", "w_uq", "w_ukv", "w_o_gm", "w_o_mem", "w_ff1"}
EARLY = ["w_in", "w_uq", "w_ukv", "w_mem_kv"]
LATE = ["w_o_gm", "w_o_mla", "w_o_mem", "w_out", "w_ff1", "w_ff2"]
SMALL = ["w_spatial", "b_spatial", "g_mix", "g_cq", "g_ckv", "g_q_nope", "g_q_pe", "g_k_nope", "g_k_pe", "g_gm_ln",
         "b_gm_ln", "g_mem", "g_mq", "g_mk", "g_ffn"]
SMALL_SHAPE = {"g_mix": (1, 1024), "g_cq": (1, 384), "g_ckv": (1, 256), "g_q_nope": (1, 128), "g_q_pe": (1, 64),
               "g_k_nope": (1, 128), "g_k_pe": (1, 64), "g_gm_ln": (1, 512), "b_gm_ln": (1, 512),
               "w_spatial": (1, 4, 128, 128), "b_spatial": (1, 4, 128), "g_mem": (1, 1024), "g_mq": (1, 128),
               "g_mk": (1, 128), "g_ffn": (1, 1024)}
WEIGHTS = ['g_mix', 'w_in', 'g_cq', 'w_uq', 'g_ckv', 'w_ukv', 'g_q_nope', 'g_q_pe', 'g_k_nope', 'g_k_pe',
           'g_gm_ln', 'b_gm_ln', 'w_spatial', 'b_spatial', 'g_mem', 'w_mem_kv', 'g_mq', 'g_mk', 'w_o_gm',
           'w_o_mla', 'w_o_mem', 'w_out', 'g_ffn', 'w_ff1', 'w_ff2']


def _params(sem=None):
    return pltpu.CompilerParams(vmem_limit_bytes=VMEM_LIMIT, dimension_semantics=sem)


def _pick(n, prefs):
    for p in prefs:
        if n % p == 0:
            return p
    return n


def _full(shape):
    nd = len(shape)
    return pl.BlockSpec(shape, lambda *_: (0,) * nd)


def _rows(t, w, blk=0):
    return pl.BlockSpec((t, w), lambda i: (i, blk))


def _acc(ref, val, first):
    @pl.when(first)
    def _():
        ref[...] = val

    @pl.when(jnp.logical_not(first))
    def _():
        ref[...] += val


ANY = pl.BlockSpec(memory_space=pl.ANY)


class _Phase:
    def __init__(self, operands, out_shapes, n_sem, n_local, copies, aliases=None):
        self.operands, self.out_shapes, self.aliases = list(operands), list(out_shapes), dict(aliases or {})
        self.n_sem, self.n_local, self.copies = n_sem, max(n_local, 1), copies

    def sem_shapes(self):
        return [pltpu.SemaphoreType.DMA((self.n_sem,)), pltpu.SemaphoreType.DMA((self.n_sem,)),
                pltpu.SemaphoreType.DMA((self.n_local,))]

    def start(self, ins, outs, send, recv, local):
        sends, _, locals_ = self.copies(ins, outs, send, recv, local)
        for cp in locals_ + sends:
            cp.start()

    def finish(self, ins, outs, send, recv, local):
        sends, arrivals, locals_ = self.copies(ins, outs, send, recv, local)
        for cp in arrivals:
            cp.wait_recv()
        for cp in sends:
            cp.wait_send()
        for cp in locals_:
            cp.wait()


def _run_phase(phase, name):
    n_in = len(phase.operands)

    def body(*refs):
        ins, outs, sems = refs[:n_in], refs[n_in:n_in + len(phase.out_shapes)], refs[n_in + len(phase.out_shapes):]
        phase.start(ins, outs, *sems)
        phase.finish(ins, outs, *sems)

    return pl.pallas_call(body, in_specs=[ANY] * n_in, out_specs=[ANY] * len(phase.out_shapes),
                          out_shape=phase.out_shapes, scratch_shapes=phase.sem_shapes(),
                          input_output_aliases=phase.aliases, name=name)(*phase.operands)


def _pcall(body, *, grid, in_specs, out_specs, out_shape, scratch_shapes=(), sem=None, name, comm=None):
    single = not isinstance(out_shape, (list, tuple))
    o_specs = [out_specs] if single else list(out_specs)
    o_shape = [out_shape] if single else list(out_shape)
    if comm is None:
        call = pl.pallas_call(body, grid=grid, in_specs=list(in_specs), out_specs=o_specs, out_shape=o_shape,
                              scratch_shapes=list(scratch_shapes), compiler_params=_params(sem), name=name)

        def run_plain(*args):
            res = call(*args)
            return res[0] if single else res

        return run_plain

    n_in, n_out, n_scr = len(in_specs), len(o_specs), len(scratch_shapes)
    nc_in, nc_out = len(comm.operands), len(comm.out_shapes)

    def wrapped(*refs):
        ins, cins = refs[:n_in], refs[n_in:n_in + nc_in]
        o0 = n_in + nc_in
        outs, couts = refs[o0:o0 + n_out], refs[o0 + n_out:o0 + n_out + nc_out]
        s0 = o0 + n_out + nc_out
        scr, csem = refs[s0:s0 + n_scr], refs[s0 + n_scr:]
        ids = [pl.program_id(d) for d in range(len(grid))]
        first = functools.reduce(jnp.logical_and, [i == 0 for i in ids])
        last = functools.reduce(jnp.logical_and, [i == g - 1 for i, g in zip(ids, grid)])

        @pl.when(first)
        def _():
            comm.start(cins, couts, *csem)

        body(*ins, *outs, *scr)

        @pl.when(last)
        def _():
            comm.finish(cins, couts, *csem)

    call = pl.pallas_call(
        wrapped, grid=grid, in_specs=list(in_specs) + [ANY] * nc_in, out_specs=o_specs + [ANY] * nc_out,
        out_shape=o_shape + comm.out_shapes, scratch_shapes=list(scratch_shapes) + comm.sem_shapes(),
        input_output_aliases={n_in + i: n_out + j for i, j in comm.aliases.items()},
        compiler_params=_params(("arbitrary",) * len(grid)), name=name)

    def run_carrying(*args):
        res = call(*args, *comm.operands)
        return (res[0] if single else res[:n_out]), res[n_out:]

    return run_carrying


def _dn(a, b, ca, cb):
    return lax.dot_general(a.astype(BF), b.astype(BF), (((ca,), (cb,)), ((), ())), preferred_element_type=F32)


@jax.custom_vjp
def _mm_nn(a, b):
    return _dn(a, b, 1, 0)


def _mm_nn_fwd(a, b):
    return _dn(a, b, 1, 0), (a.astype(BF), b.astype(BF))


def _mm_nn_bwd(res, ct):
    a, b = res
    return _dn(ct, b, 1, 1), _dn(a, ct, 0, 0)


_mm_nn.defvjp(_mm_nn_fwd, _mm_nn_bwd)


@jax.custom_vjp
def _mm_nt(a, b):
    return _dn(a, b, 1, 1)


def _mm_nt_fwd(a, b):
    return _dn(a, b, 1, 1), (a.astype(BF), b.astype(BF))


def _mm_nt_bwd(res, ct):
    a, b = res
    return _dn(ct, b, 1, 0), _dn(ct, a, 0, 0)


_mm_nt.defvjp(_mm_nt_fwd, _mm_nt_bwd)


def _rmsn(x, g, n):
    ms = jnp.sum(x * x, axis=-1, keepdims=True) * (1.0 / n)
    return x * lax.rsqrt(ms + EPS) * g


def _layernorm(x, g, b):
    mu = jnp.mean(x, axis=-1, keepdims=True)
    xc = x - mu
    y = xc * lax.rsqrt(jnp.mean(xc * xc, axis=-1, keepdims=True) + EPS)
    return y * g + b


def _swap_lanes(x):
    half = MLA_ROPE // 2
    lane = lax.broadcasted_iota(jnp.int32, x.shape, 1)
    return jnp.where(lane < half, pltpu.roll(x, LANES - half, axis=1),
                     jnp.where(lane < MLA_ROPE, pltpu.roll(x, half, axis=1), 0.0))


@jax.custom_vjp
def _swap_halves(x):
    return _swap_lanes(x)


_swap_halves.defvjp(lambda x: (_swap_lanes(x), None), lambda _, ct: (_swap_lanes(ct),))


def _rope(x, cos_f, sin_s):
    return x * cos_f + _swap_halves(x) * sin_s


def _lane_blocks(x):
    return tuple(x[:, i * LANES:(i + 1) * LANES] for i in range(x.shape[1] // LANES))


@jax.custom_vjp
def _split_lanes(x):
    return _lane_blocks(x)


_split_lanes.defvjp(lambda x: (_lane_blocks(x), None), lambda _, cts: (jnp.concatenate(cts, axis=1),))


def _softmax(s):
    m = lax.stop_gradient(jnp.max(s, axis=-1, keepdims=True))
    p = jnp.exp(s - m)
    return p / jnp.sum(p, axis=-1, keepdims=True)


def _mm(a, b, *, ta=False, tb=False, ins=(), epilogue=None, out_dtypes=(F32,), owner_cols=None, name, comm=None):
    if ta:
        k_dim, m = a.shape
    else:
        m, k_dim = a.shape
    if tb:
        n, kb = b.shape
    else:
        kb, n = b.shape
    assert k_dim == kb, (a.shape, b.shape, ta, tb)
    tm = _pick(m, (1024, 512, 256, 128))
    tn = _pick(n if owner_cols is None else owner_cols, (1024, 768, 512, 384, 256, 128))
    tk = _pick(k_dim, (2048, 1024, 768, 512, 256, 128))
    nk = k_dim // tk
    ca = 0 if ta else 1
    cb = 1 if tb else 0
    n_in = len(ins)
    n_out = len(out_dtypes)

    def finish(r, in_refs, out_refs):
        vals = epilogue(r, *[ref[...] for ref in in_refs]) if epilogue is not None else (r,)
        for ref, val, dt in zip(out_refs, vals, out_dtypes):
            ref[...] = val.astype(dt)

    def body(*refs):
        a_ref, b_ref = refs[:2]
        in_refs = refs[2:2 + n_in]
        out_refs = refs[2 + n_in:2 + n_in + n_out]
        part = _dn(a_ref[...], b_ref[...], ca, cb)
        if nk == 1:
            finish(part, in_refs, out_refs)
            return
        acc = refs[-1]
        k = pl.program_id(2)
        _acc(acc, part, k == 0)

        @pl.when(k == nk - 1)
        def _():
            finish(acc[...], in_refs, out_refs)

    a_spec = pl.BlockSpec((tk, tm), lambda i, j, k: (k, i)) if ta else pl.BlockSpec((tm, tk), lambda i, j, k: (i, k))
    b_spec = pl.BlockSpec((tn, tk), lambda i, j, k: (j, k)) if tb else pl.BlockSpec((tk, tn), lambda i, j, k: (k, j))
    t_spec = pl.BlockSpec((tm, tn), lambda i, j, k: (i, j))
    if owner_cols is None:
        o_spec, o_shape = t_spec, (m, n)
    else:
        per = owner_cols // tn
        o_spec = pl.BlockSpec((None, tm, tn), lambda i, j, k: (j // per, i, j % per))
        o_shape = (n // owner_cols, m, owner_cols)
    run = _pcall(body, grid=(m // tm, n // tn, nk), in_specs=[a_spec, b_spec] + [t_spec] * n_in,
                 out_specs=[o_spec] * n_out, out_shape=[SDS(o_shape, dt) for dt in out_dtypes],
                 scratch_shapes=[pltpu.VMEM((tm, tn), F32)] if nk > 1 else [],
                 sem=("parallel", "parallel", "arbitrary"), name=name, comm=comm)
    if comm is None:
        outs = run(a, b, *ins)
        return outs[0] if n_out == 1 else outs
    outs, exchanged = run(a, b, *ins)
    return (outs[0] if n_out == 1 else outs), exchanged


def _add_to(r, x):
    return (r + x,)


def _relu2(r):
    p = jnp.maximum(r, 0.0)
    return r, p * p


def _relu2_bwd(dr, a):
    return (dr * (2.0 * jnp.maximum(a, 0.0)),)


def _rms_fwd(x, g, name):
    n, w = x.shape
    t = min(ROW_TILE, n)

    def body(x_ref, g_ref, o_ref):
        o_ref[...] = _rmsn(x_ref[...], g_ref[...], w).astype(BF)

    return pl.pallas_call(body, grid=(n // t,), in_specs=[_rows(t, w), _full((1, w))], out_specs=_rows(t, w),
                          out_shape=SDS((n, w), BF), compiler_params=_params(("arbitrary",)), name=name)(x, g)


def _rms_bwd(x, g, dh, res, name, comm=None):
    n, w = x.shape
    t = min(ROW_TILE, n)
    has_res = res is not None

    def body(*refs):
        if has_res:
            x_ref, g_ref, dh_ref, res_ref, dx_ref, dxb_ref, dg_ref = refs
        else:
            x_ref, g_ref, dh_ref, dx_ref, dxb_ref, dg_ref = refs
        _, vjp = jax.vjp(lambda xx, gg: _rmsn(xx, gg, w), x_ref[...], g_ref[...])
        dx, dg = vjp(dh_ref[...])
        if has_res:
            dx = dx + res_ref[...]
        dx_ref[...] = dx
        dxb_ref[...] = dx.astype(BF)
        _acc(dg_ref, dg, pl.program_id(0) == 0)

    in_specs = [_rows(t, w), _full((1, w)), _rows(t, w)] + ([_rows(t, w)] if has_res else [])
    args = [x, g, dh] + ([res] if has_res else [])
    return _pcall(body, grid=(n // t,), in_specs=in_specs, out_specs=[_rows(t, w), _rows(t, w), _full((1, w))],
                  out_shape=[SDS((n, w), F32), SDS((n, w), BF), SDS((1, w), F32)], sem=("arbitrary",), name=name,
                  comm=comm)(*args)


def _loss_call(y, tgt, name):
    n, w = y.shape
    t = min(ROW_TILE, n)

    def body(y_ref, t_ref, dy_ref, dyb_ref, l_ref):
        e = y_ref[...] - t_ref[...]
        dy = e * (1.0 / w)
        dy_ref[...] = dy
        dyb_ref[...] = dy.astype(BF)
        part = jnp.sum(jnp.sum(e * e, axis=-1, keepdims=True), axis=0, keepdims=True) * (0.5 / w)
        _acc(l_ref, jnp.broadcast_to(part, (8, LANES)), pl.program_id(0) == 0)

    return pl.pallas_call(body, grid=(n // t,), in_specs=[_rows(t, w), _rows(t, w)],
                          out_specs=[_rows(t, w), _rows(t, w), _full((8, LANES))],
                          out_shape=[SDS((n, w), F32), SDS((n, w), BF), SDS((8, LANES), F32)],
                          compiler_params=_params(("arbitrary",)), name=name)(y, tgt)


def _merge_core(zg0, zg1, zg2, y0, y1, y2):
    return jax.nn.sigmoid(zg0) * y0 + jax.nn.sigmoid(zg1) * y1 + jax.nn.sigmoid(zg2) * y2


def _merge_fwd(z, y_gm, y_mla, y_mem, name):
    n = z.shape[0]
    t = min(ROW_TILE, n)
    w = D_MODEL

    def body(g0, g1, g2, y0, y1, y2, o_ref):
        o_ref[...] = _merge_core(g0[...], g1[...], g2[...], y0[...], y1[...], y2[...]).astype(BF)

    return pl.pallas_call(body, grid=(n // t,),
                          in_specs=[_rows(t, w, 0), _rows(t, w, 1), _rows(t, w, 2)] + [_rows(t, w)] * 3,
                          out_specs=_rows(t, w), out_shape=SDS((n, w), BF),
                          compiler_params=_params(("parallel",)), name=name)(z, z, z, y_gm, y_mla, y_mem)


def _merge_bwd(z, y_gm, y_mla, y_mem, dmerged, name):
    n = z.shape[0]
    t = min(ROW_TILE, n)
    w = D_MODEL

    def body(g0, g1, g2, y0, y1, y2, dm, dzg_ref, d0_ref, d1_ref, d2_ref):
        _, vjp = jax.vjp(_merge_core, g0[...], g1[...], g2[...], y0[...], y1[...], y2[...])
        dg0, dg1, dg2, dy0, dy1, dy2 = vjp(dm[...])
        dzg_ref[:, 0:w] = dg0.astype(BF)
        dzg_ref[:, w:2 * w] = dg1.astype(BF)
        dzg_ref[:, 2 * w:3 * w] = dg2.astype(BF)
        d0_ref[...] = dy0.astype(BF)
        d1_ref[...] = dy1.astype(BF)
        d2_ref[...] = dy2.astype(BF)

    return pl.pallas_call(body, grid=(n // t,),
                          in_specs=[_rows(t, w, 0), _rows(t, w, 1), _rows(t, w, 2)] + [_rows(t, w)] * 4,
                          out_specs=[_rows(t, 3 * w)] + [_rows(t, w)] * 3,
                          out_shape=[SDS((n, 3 * w), BF)] + [SDS((n, w), BF)] * 3,
                          compiler_params=_params(("parallel",)), name=name)(z, z, z, y_gm, y_mla, y_mem, dmerged)


def _gm_core(zu, zv, g_ln, b_ln, ws, bcols):
    t = zu.shape[0]
    u = jax.nn.gelu(zu)
    v = _layernorm(jax.nn.gelu(zv), g_ln, b_ln)
    row = lax.broadcasted_iota(jnp.int32, (GM_CHUNK, GM_CHUNK), 0)
    col = lax.broadcasted_iota(jnp.int32, (GM_CHUNK, GM_CHUNK), 1)
    wc = [jnp.where(row >= col, ws[g], 0.0) for g in range(GM_GROUPS)]
    chunks = []
    for c in range(t // GM_CHUNK):
        cols = []
        for g in range(GM_GROUPS):
            vc = v[c * GM_CHUNK:(c + 1) * GM_CHUNK, g * LANES:(g + 1) * LANES]
            cols.append(_mm_nn(wc[g], vc) + bcols[g])
        chunks.append(jnp.concatenate(cols, axis=1))
    mixed = chunks[0] if len(chunks) == 1 else jnp.concatenate(chunks, axis=0)
    return u * mixed


def _gm_specs(t):
    return [_rows(t, GM_WIDTH, ZU // GM_WIDTH), _rows(t, GM_WIDTH, ZV // GM_WIDTH), _full((1, GM_WIDTH)),
            _full((1, GM_WIDTH)), _full((GM_GROUPS, GM_CHUNK, GM_CHUNK))] + [_full((GM_CHUNK, 1))] * GM_GROUPS


def _gm_fwd(z, g_ln, b_ln, ws, bcols, name):
    n = z.shape[0]
    t = min(ROW_TILE, n)

    def body(zu, zv, g_ref, b_ref, ws_ref, c0, c1, c2, c3, o_ref):
        out = _gm_core(zu[...], zv[...], g_ref[...], b_ref[...], [ws_ref[g] for g in range(GM_GROUPS)],
                       [c0[...], c1[...], c2[...], c3[...]])
        o_ref[...] = out.astype(BF)

    return pl.pallas_call(body, grid=(n // t,), in_specs=_gm_specs(t), out_specs=_rows(t, GM_WIDTH),
                          out_shape=SDS((n, GM_WIDTH), BF), compiler_params=_params(("parallel",)),
                          name=name)(z, z, g_ln, b_ln, ws, *bcols)


def _gm_bwd(z, g_ln, b_ln, ws, bcols, dgm, name, comm=None):
    n = z.shape[0]
    t = min(ROW_TILE, n)

    def body(zu, zv, g_ref, b_ref, ws_ref, c0, c1, c2, c3, dgm_ref, dz_ref, dg_ref, db_ref, dws_ref, e0, e1, e2, e3):
        first = pl.program_id(0) == 0
        _, vjp = jax.vjp(_gm_core, zu[...], zv[...], g_ref[...], b_ref[...],
                         [ws_ref[g] for g in range(GM_GROUPS)], [c0[...], c1[...], c2[...], c3[...]])
        dzu, dzv, dg, db, dws, dcols = vjp(dgm_ref[...])
        dz_ref[:, 0:GM_WIDTH] = dzu.astype(BF)
        dz_ref[:, GM_WIDTH:2 * GM_WIDTH] = dzv.astype(BF)
        _acc(dg_ref, dg, first)
        _acc(db_ref, db, first)
        _acc(dws_ref, jnp.stack(dws, axis=0), first)
        for ref, val in zip((e0, e1, e2, e3), dcols):
            _acc(ref, val, first)

    return _pcall(
        body, grid=(n // t,), in_specs=_gm_specs(t) + [_rows(t, GM_WIDTH)],
        out_specs=[_rows(t, 2 * GM_WIDTH), _full((1, GM_WIDTH)), _full((1, GM_WIDTH)),
                   _full((GM_GROUPS, GM_CHUNK, GM_CHUNK))] + [_full((GM_CHUNK, 1))] * GM_GROUPS,
        out_shape=[SDS((n, 2 * GM_WIDTH), BF), SDS((1, GM_WIDTH), F32), SDS((1, GM_WIDTH), F32),
                   SDS((GM_GROUPS, GM_CHUNK, GM_CHUNK), F32)] + [SDS((GM_CHUNK, 1), F32)] * GM_GROUPS,
        sem=("arbitrary",), name=name, comm=comm)(z, z, g_ln, b_ln, ws, *bcols, dgm)


def _rope_tables(pos_f, inv_full, cmask, smask, name):
    n = pos_f.shape[0]
    t = min(ROW_TILE, n)

    def body(p_ref, inv_ref, cm_ref, sm_ref, cos_ref, sin_ref):
        ang = p_ref[...] * inv_ref[...]
        cos_ref[...] = jnp.cos(ang) * cm_ref[...]
        sin_ref[...] = jnp.sin(ang) * sm_ref[...]

    return pl.pallas_call(body, grid=(n // t,), in_specs=[_rows(t, 1)] + [_full((1, LANES))] * 3,
                          out_specs=[_rows(t, LANES)] * 2, out_shape=[SDS((n, LANES), F32)] * 2,
                          compiler_params=_params(("parallel",)), name=name)(pos_f, inv_full, cmask, smask)


def _prep_norms(cq, ckv, g_cq, g_ckv):
    return _rmsn(cq, g_cq, Q_LORA), _rmsn(ckv, g_ckv, KV_LORA)


def _prep_heads(qa, kva, kpe, head_gains, cos_f, sin_s):
    g_qn, g_qp, g_kn, g_kp = head_gains
    qs = _split_lanes(qa)
    kvs = _split_lanes(kva)
    kp = _rope(_rmsn(kpe, g_kp, MLA_ROPE), cos_f, sin_s)
    q_out, k_out = [], []
    for h in range(MLA_HEADS):
        q_out.append(_rmsn(qs[h], g_qn, MLA_NOPE))
        q_out.append(_rope(_rmsn(qs[MLA_HEADS + h], g_qp, MLA_ROPE), cos_f, sin_s))
        k_out.append(_rmsn(kvs[h], g_kn, MLA_NOPE))
        k_out.append(kp)
    return (jnp.concatenate(q_out, axis=1), jnp.concatenate(k_out, axis=1),
            jnp.concatenate(kvs[MLA_HEADS:], axis=1))


def _prep_in_specs(t):
    return ([_rows(t, Q_LORA, CQ // Q_LORA), _rows(t, LANES, KPE // LANES), _rows(t, KV_LORA, CKV // KV_LORA),
             _rows(t, LANES), _rows(t, LANES), _full((1, Q_LORA)), _full((1, KV_LORA))] + [_full((1, LANES))] * 4
            + [_full((Q_LORA, 2048)), _full((KV_LORA, 2048))])


def _prep_fwd(z, cos_f, sin_s, gains, wq, wkv, name):
    n = z.shape[0]
    t = min(ROW_TILE, n)

    def body(cq, kpe, ckv, cos_ref, sin_ref, g_cq, g_ckv, g_qn, g_qp, g_kn, g_kp, wq_ref, wkv_ref, q_ref, k_ref, v_ref):
        cqn, ckvn = _prep_norms(cq[...], ckv[...], g_cq[...], g_ckv[...])
        qa = _dn(cqn, wq_ref[...], 1, 0)
        kva = _dn(ckvn, wkv_ref[...], 1, 0)
        q, k, v = _prep_heads(qa, kva, kpe[...], (g_qn[...], g_qp[...], g_kn[...], g_kp[...]), cos_ref[...],
                              sin_ref[...])
        q_ref[...] = q.astype(BF)
        k_ref[...] = k.astype(BF)
        v_ref[...] = v.astype(BF)

    return pl.pallas_call(body, grid=(n // t,), in_specs=_prep_in_specs(t),
                          out_specs=[_rows(t, 2048), _rows(t, 2048), _rows(t, 1024)],
                          out_shape=[SDS((n, 2048), BF), SDS((n, 2048), BF), SDS((n, 1024), BF)],
                          compiler_params=_params(("parallel",)),
                          name=name)(z, z, z, cos_f, sin_s, *gains, wq, wkv)


def _prep_bwd(z, cos_f, sin_s, gains, wq, wkv, dq, dk, dv, name, comm=None):
    n = z.shape[0]
    t = min(ROW_TILE, n)
    wz = Q_LORA + LANES + KV_LORA

    def body(cq, kpe, ckv, cos_ref, sin_ref, g_cq, g_ckv, g_qn, g_qp, g_kn, g_kp, wq_ref, wkv_ref, dq_ref, dk_ref,
             dv_ref, dz_ref, o_cq, o_ckv, o_qn, o_qp, o_kn, o_kp, dwq_ref, dwkv_ref):
        first = pl.program_id(0) == 0
        cos_t, sin_t = cos_ref[...], sin_ref[...]
        (cqn, ckvn), vjp_norms = jax.vjp(_prep_norms, cq[...], ckv[...], g_cq[...], g_ckv[...])
        wq_t, wkv_t = wq_ref[...], wkv_ref[...]
        qa = _dn(cqn, wq_t, 1, 0)
        kva = _dn(ckvn, wkv_t, 1, 0)
        _, vjp_heads = jax.vjp(lambda a, b, c, g: _prep_heads(a, b, c, g, cos_t, sin_t), qa, kva, kpe[...],
                               (g_qn[...], g_qp[...], g_kn[...], g_kp[...]))
        dqa, dkva, dkpe, dhead = vjp_heads((dq_ref[...], dk_ref[...], dv_ref[...]))
        _acc(dwq_ref, _dn(cqn, dqa, 0, 0), first)
        _acc(dwkv_ref, _dn(ckvn, dkva, 0, 0), first)
        dcq, dckv, dg_cq, dg_ckv = vjp_norms((_dn(dqa, wq_t, 1, 1), _dn(dkva, wkv_t, 1, 1)))
        dz_ref[:, 0:Q_LORA] = dcq.astype(BF)
        dz_ref[:, Q_LORA:Q_LORA + LANES] = dkpe.astype(BF)
        dz_ref[:, Q_LORA + LANES:wz] = dckv.astype(BF)
        for ref, val in zip((o_cq, o_ckv, o_qn, o_qp, o_kn, o_kp), (dg_cq, dg_ckv) + tuple(dhead)):
            _acc(ref, val, first)

    gain_specs = [_full((1, Q_LORA)), _full((1, KV_LORA))] + [_full((1, LANES))] * 4
    gain_shapes = [SDS((1, Q_LORA), F32), SDS((1, KV_LORA), F32)] + [SDS((1, LANES), F32)] * 4
    return _pcall(
        body, grid=(n // t,), in_specs=_prep_in_specs(t) + [_rows(t, 2048), _rows(t, 2048), _rows(t, 1024)],
        out_specs=[_rows(t, wz)] + gain_specs + [_full((Q_LORA, 2048)), _full((KV_LORA, 2048))],
        out_shape=[SDS((n, wz), BF)] + gain_shapes + [SDS((Q_LORA, 2048), F32), SDS((KV_LORA, 2048), F32)],
        sem=("arbitrary",), name=name, comm=comm)(z, z, z, cos_f, sin_s, *gains, wq, wkv, dq, dk, dv)


MLA_QK = 256
MLA_SCALE = 1.0 / math.sqrt(MLA_NOPE + MLA_ROPE)


def _causal_mask(s, q0, k0):
    tq, tk = s.shape
    row = q0 + lax.broadcasted_iota(jnp.int32, (tq, tk), 0)
    col = k0 + lax.broadcasted_iota(jnp.int32, (tq, tk), 1)
    return jnp.where(row >= col, s, -jnp.inf)


def _mla_fwd(q, k, v, batch, seq, name):
    n = q.shape[0]
    tq = min(ATT_TILE, seq)
    nq = seq // tq

    nh = ATT_HEADS

    def body(q_ref, k_ref, v_ref, o_ref, lse_ref):
        i = pl.program_id(2)

        def step(j, carry, diagonal=False):
            k0 = pl.multiple_of(j * tq, tq)
            out = []
            for hh in range(nh):
                m, l, acc = carry[hh]
                qb = q_ref[:, hh * MLA_QK:(hh + 1) * MLA_QK]
                kb = k_ref[pl.ds(k0, tq), hh * MLA_QK:(hh + 1) * MLA_QK]
                vb = v_ref[pl.ds(k0, tq), hh * MLA_V:(hh + 1) * MLA_V]
                s = _dn(qb, kb, 1, 1) * MLA_SCALE
                if diagonal:
                    s = _causal_mask(s, i * tq, k0)
                m_new = jnp.maximum(m, jnp.max(s, axis=-1, keepdims=True))
                p = jnp.exp(s - m_new)
                alpha = jnp.exp(m - m_new)
                l = alpha * l + jnp.sum(p, axis=-1, keepdims=True)
                acc = alpha * acc + _dn(p, vb, 1, 0)
                out.append((m_new, l, acc))
            return tuple(out)

        init = tuple((jnp.full((tq, 1), -jnp.inf, F32), jnp.zeros((tq, 1), F32), jnp.zeros((tq, MLA_V), F32))
                     for _ in range(nh))
        final = step(i, lax.fori_loop(0, i, step, init), diagonal=True)
        for hh, (m, l, acc) in enumerate(final):
            o_ref[:, hh * MLA_V:(hh + 1) * MLA_V] = acc / l
            lse_ref[:, hh * LANES:(hh + 1) * LANES] = jnp.broadcast_to(m + jnp.log(l), (tq, LANES))

    return pl.pallas_call(
        body, grid=(batch, MLA_HEADS // nh, nq),
        in_specs=[pl.BlockSpec((tq, nh * MLA_QK), lambda b, h, i: (b * nq + i, h)),
                  pl.BlockSpec((seq, nh * MLA_QK), lambda b, h, i: (b, h)),
                  pl.BlockSpec((seq, nh * MLA_V), lambda b, h, i: (b, h))],
        out_specs=[pl.BlockSpec((tq, nh * MLA_V), lambda b, h, i: (b * nq + i, h)),
                   pl.BlockSpec((tq, nh * LANES), lambda b, h, i: (b * nq + i, h))],
        out_shape=[SDS((n, MLA_HEADS * MLA_V), F32), SDS((n, MLA_HEADS * LANES), F32)],
        compiler_params=_params(("parallel", "parallel", "arbitrary")), name=name)(q, k, v)


def _mla_bwd(q, k, v, o, lse, do, batch, seq, name, comm=None):
    n = q.shape[0]
    tk = min(ATT_TILE, seq)
    nk = seq // tk

    nh = ATT_HEADS

    def body(q_ref, k_ref, v_ref, o_ref, lse_ref, do_ref, dq_ref, dk_ref, dv_ref):
        jk = pl.program_id(2)

        @pl.when(jk == 0)
        def _():
            dq_ref[...] = jnp.zeros_like(dq_ref)

        def step(i, carry, diagonal=False):
            q0 = pl.multiple_of(i * tk, tk)
            rows = pl.ds(q0, tk)
            out = []
            for hh in range(nh):
                dk_acc, dv_acc = carry[hh]
                qk_cols = slice(hh * MLA_QK, (hh + 1) * MLA_QK)
                v_cols = slice(hh * MLA_V, (hh + 1) * MLA_V)
                kb = k_ref[:, qk_cols]
                vb = v_ref[:, v_cols]
                qb = q_ref[rows, qk_cols]
                dob = do_ref[rows, v_cols]
                delta = jnp.sum(dob * o_ref[rows, v_cols], axis=-1, keepdims=True)
                s = _dn(qb, kb, 1, 1) * MLA_SCALE
                if diagonal:
                    s = _causal_mask(s, q0, jk * tk)
                p = jnp.exp(s - lse_ref[rows, hh * LANES:hh * LANES + 1])
                dv_acc = dv_acc + _dn(p, dob, 0, 0)
                dp = _dn(dob, vb, 1, 1)
                ds = p * (dp - delta) * MLA_SCALE
                dk_acc = dk_acc + _dn(ds, qb, 0, 0)
                dq_ref[rows, qk_cols] += _dn(ds, kb, 1, 0)
                out.append((dk_acc, dv_acc))
            return tuple(out)

        init = tuple((jnp.zeros((tk, MLA_QK), F32), jnp.zeros((tk, MLA_V), F32)) for _ in range(nh))
        final = lax.fori_loop(jk + 1, nk, step, step(jk, init, diagonal=True))
        for hh, (dk_acc, dv_acc) in enumerate(final):
            dk_ref[:, hh * MLA_QK:(hh + 1) * MLA_QK] = dk_acc
            dv_ref[:, hh * MLA_V:(hh + 1) * MLA_V] = dv_acc

    full_qk = pl.BlockSpec((seq, nh * MLA_QK), lambda b, h, j: (b, h))
    full_v = pl.BlockSpec((seq, nh * MLA_V), lambda b, h, j: (b, h))
    blk_qk = pl.BlockSpec((tk, nh * MLA_QK), lambda b, h, j: (b * nk + j, h))
    blk_v = pl.BlockSpec((tk, nh * MLA_V), lambda b, h, j: (b * nk + j, h))
    return _pcall(
        body, grid=(batch, MLA_HEADS // nh, nk),
        in_specs=[full_qk, blk_qk, blk_v, full_v, full_v, full_v],
        out_specs=[full_qk, blk_qk, blk_v],
        out_shape=[SDS((n, MLA_HEADS * MLA_QK), F32), SDS((n, MLA_HEADS * MLA_QK), F32),
                   SDS((n, MLA_HEADS * MLA_V), F32)],
        sem=("parallel", "parallel", "arbitrary"), name=name, comm=comm)(q, k, v, o, lse, do)


MEM_SCALE = 1.0 / math.sqrt(HEAD_DIM)
MEM_W = MEM_HEADS * HEAD_DIM


def _mem_core(qs, ks, vs, g_mq, g_mk):
    outs = []
    for h in range(MEM_HEADS):
        qh = _rmsn(qs[h], g_mq, HEAD_DIM)
        kh = _rmsn(ks[h], g_mk, HEAD_DIM)
        p = _softmax(_mm_nt(qh, kh) * MEM_SCALE)
        outs.append(_mm_nn(p, vs[h]))
    return jnp.concatenate(outs, axis=1)


def _mem_load(qm, kvm, g_mq, g_mk):
    hs = range(MEM_HEADS)
    qs = [qm[:, h * LANES:(h + 1) * LANES] for h in hs]
    ks = [kvm[:, h * LANES:(h + 1) * LANES] for h in hs]
    vs = [kvm[:, MEM_W + h * LANES:MEM_W + (h + 1) * LANES] for h in hs]
    return qs, ks, vs, g_mq[...], g_mk[...]


def _mem_fwd(z, kvm, g_mq, g_mk, batch, seq, name):
    n = z.shape[0]
    t = min(ROW_TILE, seq)
    per = seq // t

    def body(qm, kvm_ref, gq, gk, o_ref):
        o_ref[...] = _mem_core(*_mem_load(qm, kvm_ref, gq, gk)).astype(BF)

    return pl.pallas_call(
        body, grid=(n // t,),
        in_specs=[_rows(t, MEM_W, QM // MEM_W), pl.BlockSpec((MEM_LEN, 2 * MEM_W), lambda i: (i // per, 0)),
                  _full((1, LANES)), _full((1, LANES))],
        out_specs=_rows(t, MEM_W), out_shape=SDS((n, MEM_W), BF),
        compiler_params=_params(("parallel",)), name=name)(z, kvm, g_mq, g_mk)


def _mem_bwd(z, kvm, g_mq, g_mk, dom, batch, seq, name):
    n = z.shape[0]
    t = min(ROW_TILE, seq)
    per = seq // t

    def body(qm, kvm_ref, gq, gk, dom_ref, dz_ref, dkvm_ref, dgq_ref, dgk_ref):
        i = pl.program_id(0)
        _, vjp = jax.vjp(_mem_core, *_mem_load(qm, kvm_ref, gq, gk))
        dqs, dks, dvs, dgq, dgk = vjp(dom_ref[...])
        dz_ref[...] = jnp.concatenate(dqs, axis=1).astype(BF)
        _acc(dkvm_ref, jnp.concatenate(dks + dvs, axis=1), i % per == 0)
        _acc(dgq_ref, dgq, i == 0)
        _acc(dgk_ref, dgk, i == 0)

    kv_spec = pl.BlockSpec((MEM_LEN, 2 * MEM_W), lambda i: (i // per, 0))
    return pl.pallas_call(
        body, grid=(n // t,),
        in_specs=[_rows(t, MEM_W, QM // MEM_W), kv_spec, _full((1, LANES)), _full((1, LANES)), _rows(t, MEM_W)],
        out_specs=[_rows(t, MEM_W), kv_spec, _full((1, LANES)), _full((1, LANES))],
        out_shape=[SDS((n, MEM_W), BF), SDS((batch * MEM_LEN, 2 * MEM_W), F32), SDS((1, LANES), F32),
                   SDS((1, LANES), F32)],
        compiler_params=_params(("arbitrary",)), name=name)(z, kvm, g_mq, g_mk, dom)


def _me():
    return lax.axis_index("x"), lax.axis_index("y"), lax.axis_index("c")


def _other_chips(x, y):
    return [(1 - x, y), (x, 1 - y), (1 - x, 1 - y)]


def _shard_shape(name):
    r, c = BIG_SHAPE[name]
    return (r, c // N_CHIPS) if name in COL_SHARDED else (r // N_CHIPS, c)


def _n_pieces(half_rows):
    return max(1, half_rows // PIECE_ROWS)


def _piece_plan(shapes):
    plan = []
    for r, _ in shapes:
        h = r // 2
        n = _n_pieces(h)
        plan.append((h, n, h // n))
    return plan


def _remote(send, recv, sem, src, dst, to):
    return pltpu.make_async_remote_copy(src_ref=src, dst_ref=dst, send_sem=send.at[sem], recv_sem=recv.at[sem],
                                        device_id=to, device_id_type=MESH)


def _gather_far(shards):
    plan = _piece_plan([s.shape for s in shards])
    n_far = 3 * sum(n for _, n, _ in plan)
    n_loc = 2 * sum(n for _, n, _ in plan)

    def copies(s_refs, o_refs, send, recv, local):
        x, y, c = _me()
        k = 2 * x + y
        mine, sends, arrivals = [], [], []
        for t, (h, n, pr) in enumerate(plan):
            s_ref, o_ref = s_refs[t], o_refs[t]
            for core in range(2):
                for p in range(n):
                    rows = pl.ds(core * h + p * pr, pr)
                    mine.append(pltpu.make_async_copy(s_ref.at[rows], o_ref.at[k, rows], local.at[len(mine)]))
            for chip in _other_chips(x, y):
                for p in range(n):
                    rows = pl.ds(c * h + p * pr, pr)
                    s = len(sends)
                    sends.append(_remote(send, recv, s, s_ref.at[rows], o_ref.at[k, rows], (*chip, c)))
                    arrivals.append(_remote(send, recv, s, s_ref.at[rows], o_ref.at[2 * chip[0] + chip[1], rows],
                                            (*chip, c)))
        return sends, arrivals, mine

    return _Phase(shards, [SDS((N_CHIPS,) + s.shape, s.dtype) for s in shards], n_far, n_loc, copies)


def _gather_near(bufs):
    plan = _piece_plan([b.shape[1:] for b in bufs])
    n_sem = 3 * sum(n for _, n, _ in plan)

    def copies(i_refs, o_refs, send, recv, local):
        x, y, c = _me()
        sib = (x, y, 1 - c)
        sends, arrivals = [], []
        for t, (h, n, pr) in enumerate(plan):
            for chip in _other_chips(x, y):
                ci = 2 * chip[0] + chip[1]
                for p in range(n):
                    rows = pl.ds(c * h + p * pr, pr)
                    rows_sib = pl.ds((1 - c) * h + p * pr, pr)
                    s = len(sends)
                    sends.append(_remote(send, recv, s, i_refs[t].at[ci, rows], o_refs[t].at[ci, rows], sib))
                    arrivals.append(_remote(send, recv, s, i_refs[t].at[ci, rows_sib], o_refs[t].at[ci, rows_sib], sib))
        return sends, arrivals, []

    return _Phase(bufs, [SDS(b.shape, b.dtype) for b in bufs], n_sem, 0, copies, {t: t for t in range(len(bufs))})


def _pair_exchange(grads):
    plan = _piece_plan([g.shape[1:] for g in grads])
    n_sem = sum(n for _, n, _ in plan)

    def copies(g_refs, o_refs, send, recv, local):
        x, y, c = _me()
        sends = []
        for t, (h, n, pr) in enumerate(plan):
            for p in range(n):
                sends.append(_remote(send, recv, len(sends), g_refs[t].at[:, pl.ds((1 - c) * h + p * pr, pr)],
                                     o_refs[t].at[:, pl.ds(p * pr, pr)], (x, y, 1 - c)))
        return sends, sends, []

    return _Phase(grads, [SDS((N_CHIPS, g.shape[1] // 2, g.shape[2]), F32) for g in grads], n_sem, 0, copies)


def _pair_add(ck, g, theirs, name):
    _, r, c = g.shape
    (h, n, pr), = _piece_plan([(r, c)])

    def body(ck_ref, g_ref, t_ref, p32_ref, pbf_ref):
        s = g_ref[...] + t_ref[...]
        p32_ref[...] = s
        pbf_ref[...] = s.astype(BF)

    half = pl.BlockSpec((None, pr, c), lambda k, p, ck: (k, p, 0))
    spec = pltpu.PrefetchScalarGridSpec(
        num_scalar_prefetch=1, grid=(N_CHIPS, n),
        in_specs=[pl.BlockSpec((None, pr, c), lambda k, p, ck: (k, ck[0] * n + p, 0)), half], out_specs=[half, half])
    return pl.pallas_call(body, grid_spec=spec, out_shape=[SDS((N_CHIPS, h, c), F32), SDS((N_CHIPS, h, c), BF)],
                          compiler_params=_params(("arbitrary", "arbitrary")), name=name)(ck, g, theirs)


def _scatter_partials(pbfs):
    plan = [(h, _n_pieces(h), h // _n_pieces(h)) for h in [p.shape[1] for p in pbfs]]
    n_sem = 3 * sum(n for _, n, _ in plan)

    def copies(p_refs, o_refs, send, recv, local):
        x, y, c = _me()
        sends = []
        for t, (h, n, pr) in enumerate(plan):
            for j, chip in enumerate(_other_chips(x, y)):
                for p in range(n):
                    rows = pl.ds(p * pr, pr)
                    sends.append(_remote(send, recv, len(sends), p_refs[t].at[2 * chip[0] + chip[1], rows],
                                         o_refs[t].at[j, rows], (*chip, c)))
        return sends, sends, []

    return _Phase(pbfs, [SDS((3,) + p.shape[1:], BF) for p in pbfs], n_sem, 0, copies)


def _sum_chips(ck, p32, slots, name):
    _, h, c = p32.shape
    n = _n_pieces(h)
    pr = h // n

    def body(ck_ref, p_ref, s_ref, o_ref):
        o_ref[...] = ((p_ref[...] + s_ref[0].astype(F32)) + s_ref[1].astype(F32)) + s_ref[2].astype(F32)

    spec = pltpu.PrefetchScalarGridSpec(
        num_scalar_prefetch=1, grid=(n,),
        in_specs=[pl.BlockSpec((None, pr, c), lambda p, ck: (ck[1], p, 0)),
                  pl.BlockSpec((3, pr, c), lambda p, ck: (0, p, 0))],
        out_specs=pl.BlockSpec((pr, c), lambda p, ck: (ck[0] * n + p, 0)))
    return pl.pallas_call(body, grid_spec=spec, out_shape=SDS((2 * h, c), F32),
                          compiler_params=_params(("arbitrary",)), name=name)(ck, p32, slots)


def _join_halves(sums):
    plan = _piece_plan([s.shape for s in sums])
    n_sem = sum(n for _, n, _ in plan)

    def copies(r_refs, o_refs, send, recv, local):
        x, y, c = _me()
        sends, arrivals = [], []
        for t, (h, n, pr) in enumerate(plan):
            for p in range(n):
                rows = pl.ds(c * h + p * pr, pr)
                rows_sib = pl.ds((1 - c) * h + p * pr, pr)
                s = len(sends)
                sends.append(_remote(send, recv, s, r_refs[t].at[rows], o_refs[t].at[rows], (x, y, 1 - c)))
                arrivals.append(_remote(send, recv, s, r_refs[t].at[rows_sib], o_refs[t].at[rows_sib], (x, y, 1 - c)))
        return sends, arrivals, []

    return _Phase(sums, [SDS(s.shape, F32) for s in sums], n_sem, 0, copies, {t: t for t in range(len(sums))})


def _gather_small(s, name):
    def body(s_ref, o_ref, send, recv, local):
        x, y, c = _me()
        me = 4 * x + 2 * y + c
        keep = pltpu.make_async_copy(s_ref, o_ref.at[me], local)
        keep.start()
        sends = []
        for r in range(1, 8):
            fx, fy, fc = (r >> 2) & 1, (r >> 1) & 1, r & 1
            to = (x ^ fx, y ^ fy, c ^ fc)
            sends.append(pltpu.make_async_remote_copy(
                src_ref=s_ref, dst_ref=o_ref.at[me], send_sem=send.at[r - 1], recv_sem=recv.at[r - 1],
                device_id=to, device_id_type=MESH))
        for cp in sends:
            cp.start()
        for r in range(1, 8):
            fx, fy, fc = (r >> 2) & 1, (r >> 1) & 1, r & 1
            src = 4 * (x ^ fx) + 2 * (y ^ fy) + (c ^ fc)
            pltpu.make_async_remote_copy(
                src_ref=s_ref, dst_ref=o_ref.at[src], send_sem=send.at[r - 1], recv_sem=recv.at[r - 1],
                device_id=(x ^ fx, y ^ fy, c ^ fc), device_id_type=MESH).wait_recv()
        for cp in sends:
            cp.wait_send()
        keep.wait()

    return pl.pallas_call(
        body, in_specs=[ANY], out_specs=ANY, out_shape=SDS((8, SMALL_ROWS, LANES), F32),
        scratch_shapes=[pltpu.SemaphoreType.DMA((7,)), pltpu.SemaphoreType.DMA((7,)), pltpu.SemaphoreType.DMA],
        name=name)(s)


def _adam_math(w, g, m, v):
    nm = ADAM_B1 * m + (1.0 - ADAM_B1) * g
    nv = ADAM_B2 * v + (1.0 - ADAM_B2) * (g * g)
    m_hat = nm / (1.0 - ADAM_B1 ** ADAM_STEP)
    v_hat = nv / (1.0 - ADAM_B2 ** ADAM_STEP)
    return -ADAM_LR * (m_hat / (jnp.sqrt(v_hat) + ADAM_EPS) + ADAM_WD * w), nm, nv


def _adamw(w, g, m, v, name):
    _, r, c = w.shape
    t = _pick(r, (256, 128, 64))

    def body(w_ref, g_ref, m_ref, v_ref, go_ref, d_ref, nm_ref, nv_ref):
        g_ = g_ref[...]
        d, nm, nv = _adam_math(w_ref[...], g_, m_ref[...], v_ref[...])
        go_ref[...] = g_
        d_ref[...] = d
        nm_ref[...] = nm
        nv_ref[...] = nv

    lead = pl.BlockSpec((None, t, c), lambda i: (0, i, 0))
    return pl.pallas_call(body, grid=(r // t,), in_specs=[lead, _rows(t, c), lead, lead], out_specs=[lead] * 4,
                          out_shape=[SDS((1, r, c), F32)] * 4, compiler_params=_params(("parallel",)),
                          name=name)(w, g, m, v)


def _small_layout():
    out, r0 = {}, 0
    for n in SMALL:
        size = int(np.prod(SMALL_SHAPE[n]))
        nr = -(-size // LANES)
        out[n] = (r0, nr)
        r0 += nr
    assert r0 <= SMALL_ROWS
    return out, r0


def _pack_small(grads, name):
    layout, used = _small_layout()

    def body(*refs):
        o_ref = refs[-1]
        for n, ref in zip(SMALL, refs[:-1]):
            r0, nr = layout[n]
            if n == "w_spatial":
                for g in range(GM_GROUPS):
                    o_ref[r0 + g * GM_CHUNK:r0 + (g + 1) * GM_CHUNK, :] = ref[g]
            elif n == "b_spatial":
                o_ref[r0:r0 + nr, :] = ref[...]
            else:
                for i in range(nr):
                    o_ref[r0 + i:r0 + i + 1, :] = ref[:, i * LANES:(i + 1) * LANES]
        o_ref[used:SMALL_ROWS, :] = jnp.zeros((SMALL_ROWS - used, LANES), F32)

    return pl.pallas_call(body, out_shape=SDS((SMALL_ROWS, LANES), F32), name=name)(*grads)


def _adamw_small(gathered, ws, ms, vs, name):
    layout, _ = _small_layout()
    n_t = len(SMALL)

    def body(*refs):
        g_ref = refs[0]
        w_refs, m_refs, v_refs = refs[1:1 + n_t], refs[1 + n_t:1 + 2 * n_t], refs[1 + 2 * n_t:1 + 3 * n_t]
        outs = refs[1 + 3 * n_t:1 + 7 * n_t]
        acc = refs[-1]
        total = g_ref[0]
        for j in range(1, 8):
            total = total + g_ref[j]
        acc[...] = total
        for t, n in enumerate(SMALL):
            r0, nr = layout[n]
            o_refs = [outs[t], outs[n_t + t], outs[2 * n_t + t], outs[3 * n_t + t]]
            if n == "w_spatial":
                views = [((0, g), slice(r0 + g * GM_CHUNK, r0 + (g + 1) * GM_CHUNK), slice(None))
                         for g in range(GM_GROUPS)]
            elif n == "b_spatial":
                views = [((0,), slice(r0, r0 + nr), slice(None))]
            else:
                width = SMALL_SHAPE[n][1]
                views = [((slice(None), slice(i * LANES, min((i + 1) * LANES, width))), slice(r0 + i, r0 + i + 1),
                          slice(0, min(LANES, width - i * LANES))) for i in range(nr)]
            for idx, rows, lanes in views:
                g = acc[rows, lanes]
                d, nm, nv = _adam_math(w_refs[t][idx], g, m_refs[t][idx], v_refs[t][idx])
                for ref, val in zip(o_refs, (g, d, nm, nv)):
                    ref[idx] = val

    shapes = [SDS(SMALL_SHAPE[n], F32) for n in SMALL]
    return pl.pallas_call(body, out_shape=shapes * 4, scratch_shapes=[pltpu.VMEM((SMALL_ROWS, LANES), F32)],
                          name=name)(gathered, *ws, *ms, *vs)


def _win_layout(w_in):
    pad = jnp.zeros((w_in.shape[0], LANES - MLA_ROPE), w_in.dtype)
    u, v, cq = w_in[:, 0:512], w_in[:, 512:1024], w_in[:, 1024:1408]
    ckv, kpe, qm, zg = w_in[:, 1408:1664], w_in[:, 1664:1728], w_in[:, 1728:2240], w_in[:, 2240:5312]
    return jnp.concatenate([zg, u, v, qm, cq, kpe, pad, ckv], axis=1)


def _win_unlayout(g):
    zg, u, v, qm = g[:, ZG:ZG + 3072], g[:, ZU:ZU + 512], g[:, ZV:ZV + 512], g[:, QM:QM + 512]
    cq, kpe, ckv = g[:, CQ:CQ + 384], g[:, KPE:KPE + MLA_ROPE], g[:, CKV:CKV + 256]
    return jnp.concatenate([u, v, cq, ckv, kpe, qm, zg], axis=1)


def _wq_layout(w_uq):
    w = w_uq.reshape(Q_LORA, MLA_HEADS, MLA_NOPE + MLA_ROPE)
    nope = w[:, :, :MLA_NOPE].reshape(Q_LORA, MLA_HEADS * MLA_NOPE)
    pe = jnp.pad(w[:, :, MLA_NOPE:], ((0, 0), (0, 0), (0, LANES - MLA_ROPE))).reshape(Q_LORA, MLA_HEADS * LANES)
    return jnp.concatenate([nope, pe], axis=1)


def _wq_unlayout(g):
    nope = g[:, :1024].reshape(Q_LORA, MLA_HEADS, MLA_NOPE)
    pe = g[:, 1024:].reshape(Q_LORA, MLA_HEADS, LANES)[:, :, :MLA_ROPE]
    return jnp.concatenate([nope, pe], axis=2).reshape(Q_LORA, MLA_HEADS * (MLA_NOPE + MLA_ROPE))


def _wkv_layout(w_ukv):
    w = w_ukv.reshape(KV_LORA, MLA_HEADS, MLA_NOPE + MLA_V)
    return jnp.concatenate([w[:, :, :MLA_NOPE].reshape(KV_LORA, 1024), w[:, :, MLA_NOPE:].reshape(KV_LORA, 1024)],
                           axis=1)


def _wkv_unlayout(g):
    kn = g[:, :1024].reshape(KV_LORA, MLA_HEADS, MLA_NOPE)
    v = g[:, 1024:].reshape(KV_LORA, MLA_HEADS, MLA_V)
    return jnp.concatenate([kn, v], axis=2).reshape(KV_LORA, MLA_HEADS * (MLA_NOPE + MLA_V))


def _owner_major(g, name):
    r, c = _shard_shape(name)
    return g.reshape(r, N_CHIPS, c).transpose(1, 0, 2) if name in COL_SHARDED else g.reshape(N_CHIPS, r, c)


def _pad_lanes(g):
    return jnp.pad(g, ((0, 0), (0, LANES - g.shape[1])))


def kernel(x, mem, positions, g_mix, w_in, g_cq, w_uq, g_ckv, w_ukv, g_q_nope, g_q_pe, g_k_nope, g_k_pe, g_gm_ln, b_gm_ln, w_spatial, b_spatial, g_mem, w_mem_kv, g_mq, g_mk, w_o_gm, w_o_mla, w_o_mem, w_out, g_ffn, w_ff1, w_ff2, loss_target, m_g_mix, m_w_in, m_g_cq, m_w_uq, m_g_ckv, m_w_ukv, m_g_q_nope, m_g_q_pe, m_g_k_nope, m_g_k_pe, m_g_gm_ln, m_b_gm_ln, m_w_spatial, m_b_spatial, m_g_mem, m_w_mem_kv, m_g_mq, m_g_mk, m_w_o_gm, m_w_o_mla, m_w_o_mem, m_w_out, m_g_ffn, m_w_ff1, m_w_ff2, v_g_mix, v_w_in, v_g_cq, v_w_uq, v_g_ckv, v_w_ukv, v_g_q_nope, v_g_q_pe, v_g_k_nope, v_g_k_pe, v_g_gm_ln, v_b_gm_ln, v_w_spatial, v_b_spatial, v_g_mem, v_w_mem_kv, v_g_mq, v_g_mk, v_w_o_gm, v_w_o_mla, v_w_o_mem, v_w_out, v_g_ffn, v_w_ff1, v_w_ff2):
    given = dict(locals())
    wts = {n: given[n] for n in WEIGHTS}
    mom = {n: given["m_" + n] for n in WEIGHTS}
    var = {n: given["v_" + n] for n in WEIGHTS}
    batch, seq, _ = x.shape
    n_tok = batch * seq

    def natural(n, g):
        r, c = _shard_shape(n)
        return g.transpose(1, 0, 2).reshape(r, N_CHIPS * c) if n in COL_SHARDED else g.reshape(N_CHIPS * r, c)

    early = _run_phase(_gather_near(_run_phase(_gather_far([wts[n][0].astype(BF) for n in EARLY]), "gather_early")),
                       "gather_early_pass")
    full = {n: natural(n, g) for n, g in zip(EARLY, early)}
    win = _win_layout(full["w_in"])
    wq = _wq_layout(full["w_uq"])
    wkv = _wkv_layout(full["w_ukv"])

    x2 = x.reshape(n_tok, D_MODEL)
    tgt2 = loss_target.reshape(n_tok, D_MODEL)
    mem2 = mem.reshape(batch * MEM_LEN, D_MODEL)
    pos_f = positions.reshape(n_tok, 1).astype(F32)

    inv = ROPE_BASE ** (-jnp.arange(0, MLA_ROPE, 2, dtype=F32) / MLA_ROPE)
    zeros64 = jnp.zeros((LANES - MLA_ROPE,), F32)
    inv_full = jnp.concatenate([inv, inv, zeros64]).reshape(1, LANES)
    half = MLA_ROPE // 2
    cmask = jnp.concatenate([jnp.ones((MLA_ROPE,), F32), zeros64]).reshape(1, LANES)
    smask = jnp.concatenate([-jnp.ones((half,), F32), jnp.ones((half,), F32), zeros64]).reshape(1, LANES)

    prep_gains = [g_cq, g_ckv, g_q_nope, _pad_lanes(g_q_pe), g_k_nope, _pad_lanes(g_k_pe)]
    ws = w_spatial[0]
    bcols = [b_spatial[0, g].reshape(GM_CHUNK, 1) for g in range(GM_GROUPS)]

    h1 = _rms_fwd(x2, g_mix, "rms_mix")
    z, late_far = _mm(h1, win, name="mm_in", comm=_gather_far([wts[n][0].astype(BF) for n in LATE]))
    gm = _gm_fwd(z, g_gm_ln, b_gm_ln, ws, bcols, "gm_fwd")
    cos_f, sin_s = _rope_tables(pos_f, inv_full, cmask, smask, "rope_tables")
    qc, kc, vc = _prep_fwd(z, cos_f, sin_s, prep_gains, wq, wkv, "prep_fwd")
    o_mla, lse = _mla_fwd(qc, kc, vc, batch, seq, "mla_fwd")
    memn = _rms_fwd(mem2, g_mem, "rms_mem")
    kvm, late = _mm(memn, full["w_mem_kv"], name="mm_memkv", comm=_gather_near(late_far))
    full.update({n: natural(n, g) for n, g in zip(LATE, late)})
    o_mem = _mem_fwd(z, kvm, g_mq, g_mk, batch, seq, "mem_fwd")
    y_gm = _mm(gm, full["w_o_gm"], name="mm_o_gm")
    y_mla = _mm(o_mla, full["w_o_mla"], name="mm_o_mla")
    y_mem = _mm(o_mem, full["w_o_mem"], name="mm_o_mem")
    merged = _merge_fwd(z, y_gm, y_mla, y_mem, "merge_fwd")
    x1 = _mm(merged, full["w_out"], ins=(x2,), epilogue=_add_to, name="mm_out")
    h2 = _rms_fwd(x1, g_ffn, "rms_ffn")
    a_ff, r_ff = _mm(h2, full["w_ff1"], epilogue=_relu2, out_dtypes=(F32, BF), name="mm_ff1")
    y = _mm(r_ff, full["w_ff2"], ins=(x1,), epilogue=_add_to, name="mm_ff2")
    dy, dyb, loss_tile = _loss_call(y, tgt2, "loss")

    gw = {}
    da = _mm(dyb, full["w_ff2"], tb=True, ins=(a_ff,), epilogue=_relu2_bwd, out_dtypes=(BF,), name="mm_d_a")
    gw["w_ff2"] = _owner_major(_mm(r_ff, dyb, ta=True, name="mm_dw_ff2"), "w_ff2")
    gw["w_ff1"] = _mm(h2, da, ta=True, owner_cols=D_FF // N_CHIPS, name="mm_dw_ff1")
    dh2 = _mm(da, full["w_ff1"], tb=True, name="mm_d_h2")
    dx1, dx1b, dg_ffn = _rms_bwd(x1, g_ffn, dh2, dy, "rms_ffn_bwd")
    dmerged = _mm(dx1b, full["w_out"], tb=True, name="mm_d_merged")
    gw["w_out"] = _owner_major(_mm(merged, dx1b, ta=True, name="mm_dw_out"), "w_out")
    dzg, dy_gm, dy_mla, dy_mem = _merge_bwd(z, y_gm, y_mla, y_mem, dmerged, "merge_bwd")
    dgm = _mm(dy_gm, full["w_o_gm"], tb=True, name="mm_d_gm")
    gw["w_o_gm"] = _mm(gm, dy_gm, ta=True, owner_cols=D_MODEL // N_CHIPS, name="mm_dw_o_gm")
    do_mla = _mm(dy_mla, full["w_o_mla"], tb=True, name="mm_d_omla")
    gw["w_o_mla"] = _owner_major(_mm(o_mla, dy_mla, ta=True, name="mm_dw_o_mla"), "w_o_mla")
    do_mem = _mm(dy_mem, full["w_o_mem"], tb=True, name="mm_d_omem")
    gw["w_o_mem"] = _mm(o_mem, dy_mem, ta=True, owner_cols=D_MODEL // N_CHIPS, name="mm_dw_o_mem")
    ck = jnp.stack([lax.axis_index("c"), 2 * lax.axis_index("x") + lax.axis_index("y")]).astype(jnp.int32)

    def pair_sums(names, theirs):
        return [_pair_add(ck, gw[n], t, "pair_add_" + n) for n, t in zip(names, theirs)]

    def chip_sums(names, pairs, slots):
        return [_sum_chips(ck, p[0], s, "sum_chips_" + n) for n, p, s in zip(names, pairs, slots)]

    (dz_uv, dg_ln, db_ln, dws, *dbcols), theirs = _gm_bwd(z, g_gm_ln, b_gm_ln, ws, bcols, dgm, "gm_bwd",
                                                         comm=_pair_exchange([gw[n] for n in LATE]))
    pairs = pair_sums(LATE, theirs)
    (dq, dk, dv), slots = _mla_bwd(qc, kc, vc, o_mla, lse, do_mla, batch, seq, "mla_bwd",
                                   comm=_scatter_partials([p[1] for p in pairs]))
    sums = chip_sums(LATE, pairs, slots)
    (dz_mla, dg_cq, dg_ckv, dg_qn, dg_qp, dg_kn, dg_kp, dwq, dwkv), reduced_late = _prep_bwd(
        z, cos_f, sin_s, prep_gains, wq, wkv, dq, dk, dv, "prep_bwd", comm=_join_halves(sums))
    dz_qm, dkvm, dg_mq, dg_mk = _mem_bwd(z, kvm, g_mq, g_mk, do_mem, batch, seq, "mem_bwd")
    dmemn = _mm(dkvm, full["w_mem_kv"], tb=True, name="mm_d_memn")
    gw["w_mem_kv"] = _owner_major(_mm(memn, dkvm, ta=True, name="mm_dw_memkv"), "w_mem_kv")
    _, _, dg_mem = _rms_bwd(mem2, g_mem, dmemn, None, "rms_mem_bwd")
    dz = jnp.concatenate([dzg, dz_uv, dz_qm, dz_mla], axis=1)
    gw["w_in"] = _owner_major(_win_unlayout(_mm(h1, dz, ta=True, name="mm_dw_in")), "w_in")
    gw["w_uq"] = _owner_major(_wq_unlayout(dwq), "w_uq")
    gw["w_ukv"] = _owner_major(_wkv_unlayout(dwkv), "w_ukv")
    dh1, theirs = _mm(dz, win, tb=True, name="mm_d_h1", comm=_pair_exchange([gw[n] for n in EARLY]))
    pairs = pair_sums(EARLY, theirs)
    (grad_x, _, dg_mix), slots = _rms_bwd(x2, g_mix, dh1, dx1, "rms_mix_bwd",
                                          comm=_scatter_partials([p[1] for p in pairs]))
    reduced_early = _run_phase(_join_halves(chip_sums(EARLY, pairs, slots)), "join_early")
    results = {n: _adamw(wts[n], g, mom[n], var[n], "adamw_" + n)
               for n, g in zip(LATE + EARLY, list(reduced_late) + list(reduced_early))}

    small_g = {"g_mix": dg_mix, "g_cq": dg_cq, "g_ckv": dg_ckv, "g_q_nope": dg_qn, "g_q_pe": dg_qp,
               "g_k_nope": dg_kn, "g_k_pe": dg_kp, "g_gm_ln": dg_ln, "b_gm_ln": db_ln, "w_spatial": dws,
               "b_spatial": jnp.concatenate(dbcols, axis=1).T, "g_mem": dg_mem, "g_mq": dg_mq, "g_mk": dg_mk,
               "g_ffn": dg_ffn}
    packed = _pack_small([small_g[n] for n in SMALL], "pack_small")
    small_out = _adamw_small(_gather_small(packed, "gather_small"), [wts[n] for n in SMALL],
                             [mom[n] for n in SMALL], [var[n] for n in SMALL], "adamw_small")
    for t, n in enumerate(SMALL):
        results[n] = [small_out[j * len(SMALL) + t] for j in range(4)]

    loss = lax.psum(loss_tile[0, 0], ("x", "y", "c"))
    grad_x = grad_x.reshape(batch, seq, D_MODEL)
    return (loss, grad_x, *[results[n][0] for n in WEIGHTS], *[results[n][1] for n in WEIGHTS],
            *[results[n][2] for n in WEIGHTS], *[results[n][3] for n in WEIGHTS])
```

```python
import functools
import math

import numpy as np
import jax
import jax.numpy as jnp
from jax import lax
from jax.experimental import pallas as pl
from jax.experimental.pallas import tpu as pltpu

F32 = jnp.float32
BF = jnp.bfloat16
SDS = jax.ShapeDtypeStruct
MESH = pl.DeviceIdType.MESH

D_MODEL = 1024
MEM_LEN = 256
MEM_HEADS = 4
HEAD_DIM = 128
GM_WIDTH = 512
GM_CHUNK = 128
GM_GROUPS = 4
MLA_HEADS = 8
MLA_NOPE = 128
MLA_ROPE = 64
MLA_V = 128
Q_LORA = 384
KV_LORA = 256
ROPE_BASE = 10000.0
D_FF = 4096
EPS = 1e-6
W_IN_COLS = 5312
ADAM_LR, ADAM_B1, ADAM_B2, ADAM_EPS, ADAM_WD, ADAM_STEP = 0.001, 0.9, 0.999, 1e-08, 0.01, 10

ZG, ZU, ZV, QM, CQ, KPE, CKV = 0, 3072, 3584, 4096, 4608, 4992, 5120
Z_COLS = 5376
LANES = 128
ROW_TILE = 256
ATT_TILE = 512
ATT_HEADS = 2
VMEM_LIMIT = 56 * 1024 * 1024

N_CHIPS = 4
PIECE_ROWS = 256
SMALL_ROWS = 560

BIG = ["w_in", "w_uq", "w_ukv", "w_mem_kv", "w_o_gm", "w_o_mla", "w_o_mem", "w_out", "w_ff1", "w_ff2"]
BIG_SHAPE = {"w_in": (1024, 5312), "w_uq": (384, 1536), "w_ukv": (256, 2048), "w_mem_kv": (1024, 1024),
             "w_o_gm": (512, 1024), "w_o_mla": (1024, 1024), "w_o_mem": (512, 1024), "w_out": (1024, 1024),
             "w_ff1": (1024, 4096), "w_ff2": (4096, 1024)}
COL_SHARDED = {"w_in", "w_uq", "w_ukv", "w_o_gm", "w_o_mem", "w_ff1"}
EARLY = ["w_in", "w_uq", "w_ukv", "w_mem_kv"]
LATE_PROJ = ["w_o_gm", "w_o_mla", "w_o_mem", "w_out"]
LATE_FF = ["w_ff1", "w_ff2"]
LATE = LATE_PROJ + LATE_FF
SMALL = ["w_spatial", "b_spatial", "g_mix", "g_cq", "g_ckv", "g_q_nope", "g_q_pe", "g_k_nope", "g_k_pe", "g_gm_ln",
         "b_gm_ln", "g_mem", "g_mq", "g_mk", "g_ffn"]
SMALL_SHAPE = {"g_mix": (1, 1024), "g_cq": (1, 384), "g_ckv": (1, 256), "g_q_nope": (1, 128), "g_q_pe": (1, 64),
               "g_k_nope": (1, 128), "g_k_pe": (1, 64), "g_gm_ln": (1, 512), "b_gm_ln": (1, 512),
               "w_spatial": (1, 4, 128, 128), "b_spatial": (1, 4, 128), "g_mem": (1, 1024), "g_mq": (1, 128),
               "g_mk": (1, 128), "g_ffn": (1, 1024)}
WEIGHTS = ['g_mix', 'w_in', 'g_cq', 'w_uq', 'g_ckv', 'w_ukv', 'g_q_nope', 'g_q_pe', 'g_k_nope', 'g_k_pe',
           'g_gm_ln', 'b_gm_ln', 'w_spatial', 'b_spatial', 'g_mem', 'w_mem_kv', 'g_mq', 'g_mk', 'w_o_gm',
           'w_o_mla', 'w_o_mem', 'w_out', 'g_ffn', 'w_ff1', 'w_ff2']


def _params(sem=None):
    return pltpu.CompilerParams(vmem_limit_bytes=VMEM_LIMIT, dimension_semantics=sem)


def _pick(n, prefs):
    for p in prefs:
        if n % p == 0:
            return p
    return n


def _full(shape):
    nd = len(shape)
    return pl.BlockSpec(shape, lambda *_: (0,) * nd)


def _rows(t, w, blk=0):
    return pl.BlockSpec((t, w), lambda i: (i, blk))


def _acc(ref, val, first):
    @pl.when(first)
    def _():
        ref[...] = val

    @pl.when(jnp.logical_not(first))
    def _():
        ref[...] += val


ANY = pl.BlockSpec(memory_space=pl.ANY)


class _Phase:
    def __init__(self, operands, out_shapes, n_sem, n_local, copies, aliases=None):
        self.operands, self.out_shapes, self.aliases = list(operands), list(out_shapes), dict(aliases or {})
        self.n_sem, self.n_local, self.copies = n_sem, max(n_local, 1), copies

    def sem_shapes(self):
        return [pltpu.SemaphoreType.DMA((self.n_sem,)), pltpu.SemaphoreType.DMA((self.n_sem,)),
                pltpu.SemaphoreType.DMA((self.n_local,))]

    def start(self, ins, outs, send, recv, local):
        sends, _, locals_ = self.copies(ins, outs, send, recv, local)
        for cp in locals_ + sends:
            cp.start()

    def finish(self, ins, outs, send, recv, local):
        sends, arrivals, locals_ = self.copies(ins, outs, send, recv, local)
        for cp in arrivals:
            cp.wait_recv()
        for cp in sends:
            cp.wait_send()
        for cp in locals_:
            cp.wait()


def _run_phase(phase, name):
    n_in = len(phase.operands)

    def body(*refs):
        ins, outs, sems = refs[:n_in], refs[n_in:n_in + len(phase.out_shapes)], refs[n_in + len(phase.out_shapes):]
        phase.start(ins, outs, *sems)
        phase.finish(ins, outs, *sems)

    return pl.pallas_call(body, in_specs=[ANY] * n_in, out_specs=[ANY] * len(phase.out_shapes),
                          out_shape=phase.out_shapes, scratch_shapes=phase.sem_shapes(),
                          input_output_aliases=phase.aliases, name=name)(*phase.operands)


def _pcall(body, *, grid, in_specs, out_specs, out_shape, scratch_shapes=(), sem=None, name, comm=None, aliases=None):
    single = not isinstance(out_shape, (list, tuple))
    o_specs = [out_specs] if single else list(out_specs)
    o_shape = [out_shape] if single else list(out_shape)
    aliases = dict(aliases or {})
    if comm is None:
        call = pl.pallas_call(body, grid=grid, in_specs=list(in_specs), out_specs=o_specs, out_shape=o_shape,
                              scratch_shapes=list(scratch_shapes), input_output_aliases=aliases,
                              compiler_params=_params(sem), name=name)

        def run_plain(*args):
            res = call(*args)
            return res[0] if single else res

        return run_plain

    n_in, n_out, n_scr = len(in_specs), len(o_specs), len(scratch_shapes)
    nc_in, nc_out = len(comm.operands), len(comm.out_shapes)

    def wrapped(*refs):
        ins, cins = refs[:n_in], refs[n_in:n_in + nc_in]
        o0 = n_in + nc_in
        outs, couts = refs[o0:o0 + n_out], refs[o0 + n_out:o0 + n_out + nc_out]
        s0 = o0 + n_out + nc_out
        scr, csem = refs[s0:s0 + n_scr], refs[s0 + n_scr:]
        ids = [pl.program_id(d) for d in range(len(grid))]
        first = functools.reduce(jnp.logical_and, [i == 0 for i in ids])
        last = functools.reduce(jnp.logical_and, [i == g - 1 for i, g in zip(ids, grid)])

        @pl.when(first)
        def _():
            comm.start(cins, couts, *csem)

        body(*ins, *outs, *scr)

        @pl.when(last)
        def _():
            comm.finish(cins, couts, *csem)

    call = pl.pallas_call(
        wrapped, grid=grid, in_specs=list(in_specs) + [ANY] * nc_in, out_specs=o_specs + [ANY] * nc_out,
        out_shape=o_shape + comm.out_shapes, scratch_shapes=list(scratch_shapes) + comm.sem_shapes(),
        input_output_aliases={**aliases, **{n_in + i: n_out + j for i, j in comm.aliases.items()}},
        compiler_params=_params(("arbitrary",) * len(grid)), name=name)

    def run_carrying(*args):
        res = call(*args, *comm.operands)
        return (res[0] if single else res[:n_out]), res[n_out:]

    return run_carrying


def _dn(a, b, ca, cb):
    return lax.dot_general(a.astype(BF), b.astype(BF), (((ca,), (cb,)), ((), ())), preferred_element_type=F32)


@jax.custom_vjp
def _mm_nn(a, b):
    return _dn(a, b, 1, 0)


def _mm_nn_fwd(a, b):
    return _dn(a, b, 1, 0), (a.astype(BF), b.astype(BF))


def _mm_nn_bwd(res, ct):
    a, b = res
    return _dn(ct, b, 1, 1), _dn(a, ct, 0, 0)


_mm_nn.defvjp(_mm_nn_fwd, _mm_nn_bwd)


@jax.custom_vjp
def _mm_nt(a, b):
    return _dn(a, b, 1, 1)


def _mm_nt_fwd(a, b):
    return _dn(a, b, 1, 1), (a.astype(BF), b.astype(BF))


def _mm_nt_bwd(res, ct):
    a, b = res
    return _dn(ct, b, 1, 0), _dn(ct, a, 0, 0)


_mm_nt.defvjp(_mm_nt_fwd, _mm_nt_bwd)


def _rmsn(x, g, n):
    ms = jnp.sum(x * x, axis=-1, keepdims=True) * (1.0 / n)
    return x * lax.rsqrt(ms + EPS) * g


def _layernorm(x, g, b):
    mu = jnp.mean(x, axis=-1, keepdims=True)
    xc = x - mu
    y = xc * lax.rsqrt(jnp.mean(xc * xc, axis=-1, keepdims=True) + EPS)
    return y * g + b


def _swap_lanes(x):
    half = MLA_ROPE // 2
    lane = lax.broadcasted_iota(jnp.int32, x.shape, 1)
    return jnp.where(lane < half, pltpu.roll(x, LANES - half, axis=1),
                     jnp.where(lane < MLA_ROPE, pltpu.roll(x, half, axis=1), 0.0))


@jax.custom_vjp
def _swap_halves(x):
    return _swap_lanes(x)


_swap_halves.defvjp(lambda x: (_swap_lanes(x), None), lambda _, ct: (_swap_lanes(ct),))


def _rope(x, cos_f, sin_s):
    return x * cos_f + _swap_halves(x) * sin_s


def _lane_blocks(x):
    return tuple(x[:, i * LANES:(i + 1) * LANES] for i in range(x.shape[1] // LANES))


@jax.custom_vjp
def _split_lanes(x):
    return _lane_blocks(x)


_split_lanes.defvjp(lambda x: (_lane_blocks(x), None), lambda _, cts: (jnp.concatenate(cts, axis=1),))


def _softmax(s):
    m = lax.stop_gradient(jnp.max(s, axis=-1, keepdims=True))
    p = jnp.exp(s - m)
    return p / jnp.sum(p, axis=-1, keepdims=True)


def _mm(a, b, *, ta=False, tb=False, ins=(), epilogue=None, out_dtypes=(F32,), owner_cols=None, total=False, name,
        comm=None):
    if ta:
        k_dim, m = a.shape
    else:
        m, k_dim = a.shape
    if tb:
        n, kb = b.shape
    else:
        kb, n = b.shape
    assert k_dim == kb, (a.shape, b.shape, ta, tb)
    tm = _pick(m, (1024, 512, 256, 128))
    tn = _pick(n if owner_cols is None else owner_cols, (1024, 768, 512, 384, 256, 128))
    tk = _pick(k_dim, (2048, 1024, 768, 512, 256, 128))
    nk = k_dim // tk
    ca = 0 if ta else 1
    cb = 1 if tb else 0
    n_in = len(ins)
    n_out = len(out_dtypes)

    def finish(r, in_refs, out_refs, first_tile):
        vals = epilogue(r, *[ref[...].astype(F32) for ref in in_refs]) if epilogue is not None else (r,)
        for ref, val, dt in zip(out_refs, vals, out_dtypes):
            ref[...] = val.astype(dt)
        if total:
            _acc(out_refs[n_out], vals[n_out], first_tile)

    def body(*refs):
        a_ref, b_ref = refs[:2]
        in_refs = refs[2:2 + n_in]
        out_refs = refs[2 + n_in:2 + n_in + n_out + int(total)]
        first_tile = jnp.logical_and(pl.program_id(0) == 0, pl.program_id(1) == 0)
        part = _dn(a_ref[...], b_ref[...], ca, cb)
        if nk == 1:
            finish(part, in_refs, out_refs, first_tile)
            return
        acc = refs[-1]
        k = pl.program_id(2)
        _acc(acc, part, k == 0)

        @pl.when(k == nk - 1)
        def _():
            finish(acc[...], in_refs, out_refs, first_tile)

    a_spec = pl.BlockSpec((tk, tm), lambda i, j, k: (k, i)) if ta else pl.BlockSpec((tm, tk), lambda i, j, k: (i, k))
    b_spec = pl.BlockSpec((tn, tk), lambda i, j, k: (j, k)) if tb else pl.BlockSpec((tk, tn), lambda i, j, k: (k, j))
    t_spec = pl.BlockSpec((tm, tn), lambda i, j, k: (i, j))
    if owner_cols is None:
        o_spec, o_shape = t_spec, (m, n)
    else:
        per = owner_cols // tn
        o_spec = pl.BlockSpec((None, tm, tn), lambda i, j, k: (j // per, i, j % per))
        o_shape = (n // owner_cols, m, owner_cols)
    o_specs = [o_spec] * n_out + ([pl.BlockSpec((8, LANES), lambda i, j, k: (0, 0))] if total else [])
    o_shapes = [SDS(o_shape, dt) for dt in out_dtypes] + ([SDS((8, LANES), F32)] if total else [])
    run = _pcall(body, grid=(m // tm, n // tn, nk), in_specs=[a_spec, b_spec] + [t_spec] * n_in,
                 out_specs=o_specs, out_shape=o_shapes,
                 scratch_shapes=[pltpu.VMEM((tm, tn), F32)] if nk > 1 else [],
                 sem=("arbitrary",) * 3 if total else ("parallel", "parallel", "arbitrary"), name=name, comm=comm)
    if comm is None:
        outs = run(a, b, *ins)
        return outs[0] if len(outs) == 1 else outs
    outs, exchanged = run(a, b, *ins)
    return (outs[0] if len(outs) == 1 else outs), exchanged


def _add_to(r, x):
    return (r + x,)


def _relu2(r):
    p = jnp.maximum(r, 0.0)
    return r, p * p


def _relu2_bwd(dr, a):
    return (dr * (2.0 * jnp.maximum(a, 0.0)),)


def _loss_tail(r, x1, tgt):
    e = (r + x1) - tgt
    dy = e * (1.0 / D_MODEL)
    part = jnp.sum(jnp.sum(e * e, axis=-1, keepdims=True), axis=0, keepdims=True) * (0.5 / D_MODEL)
    return dy, dy, jnp.broadcast_to(part, (8, LANES))


def _rms_fwd(x, g, name, comm=None):
    n, w = x.shape
    t = min(ROW_TILE, n)

    def body(x_ref, g_ref, o_ref):
        o_ref[...] = _rmsn(x_ref[...], g_ref[...], w).astype(BF)

    return _pcall(body, grid=(n // t,), in_specs=[_rows(t, w), _full((1, w))], out_specs=_rows(t, w),
                  out_shape=SDS((n, w), BF), sem=("arbitrary",), name=name, comm=comm)(x, g)


def _rms_bwd(x, g, dh, res, name, comm=None):
    n, w = x.shape
    t = min(ROW_TILE, n)
    has_res = res is not None

    def body(*refs):
        if has_res:
            x_ref, g_ref, dh_ref, res_ref, dx_ref, dxb_ref, dg_ref = refs
        else:
            x_ref, g_ref, dh_ref, dx_ref, dxb_ref, dg_ref = refs
        _, vjp = jax.vjp(lambda xx, gg: _rmsn(xx, gg, w), x_ref[...], g_ref[...])
        dx, dg = vjp(dh_ref[...])
        if has_res:
            dx = dx + res_ref[...]
        dx_ref[...] = dx
        dxb_ref[...] = dx.astype(BF)
        _acc(dg_ref, dg, pl.program_id(0) == 0)

    in_specs = [_rows(t, w), _full((1, w)), _rows(t, w)] + ([_rows(t, w)] if has_res else [])
    args = [x, g, dh] + ([res] if has_res else [])
    return _pcall(body, grid=(n // t,), in_specs=in_specs, out_specs=[_rows(t, w), _rows(t, w), _full((1, w))],
                  out_shape=[SDS((n, w), F32), SDS((n, w), BF), SDS((1, w), F32)], sem=("arbitrary",), name=name,
                  comm=comm)(*args)


def _loss_call(y, tgt, name):
    n, w = y.shape
    t = min(ROW_TILE, n)

    def body(y_ref, t_ref, dy_ref, dyb_ref, l_ref):
        e = y_ref[...] - t_ref[...]
        dy = e * (1.0 / w)
        dy_ref[...] = dy
        dyb_ref[...] = dy.astype(BF)
        part = jnp.sum(jnp.sum(e * e, axis=-1, keepdims=True), axis=0, keepdims=True) * (0.5 / w)
        _acc(l_ref, jnp.broadcast_to(part, (8, LANES)), pl.program_id(0) == 0)

    return pl.pallas_call(body, grid=(n // t,), in_specs=[_rows(t, w), _rows(t, w)],
                          out_specs=[_rows(t, w), _rows(t, w), _full((8, LANES))],
                          out_shape=[SDS((n, w), F32), SDS((n, w), BF), SDS((8, LANES), F32)],
                          compiler_params=_params(("arbitrary",)), name=name)(y, tgt)


def _merge_core(zg0, zg1, zg2, y0, y1, y2):
    return jax.nn.sigmoid(zg0) * y0 + jax.nn.sigmoid(zg1) * y1 + jax.nn.sigmoid(zg2) * y2


def _merge_fwd(z, y_gm, y_mla, y_mem, name):
    n = z.shape[0]
    t = min(ROW_TILE, n)
    w = D_MODEL

    def body(g0, g1, g2, y0, y1, y2, o_ref):
        o_ref[...] = _merge_core(g0[...], g1[...], g2[...], y0[...].astype(F32), y1[...].astype(F32),
                                 y2[...].astype(F32)).astype(BF)

    return pl.pallas_call(body, grid=(n // t,),
                          in_specs=[_rows(t, w, 0), _rows(t, w, 1), _rows(t, w, 2)] + [_rows(t, w)] * 3,
                          out_specs=_rows(t, w), out_shape=SDS((n, w), BF),
                          compiler_params=_params(("parallel",)), name=name)(z, z, z, y_gm, y_mla, y_mem)


def _merge_bwd(z, y_gm, y_mla, y_mem, dmerged, name):
    n = z.shape[0]
    t = min(ROW_TILE, n)
    w = D_MODEL

    def body(g0, g1, g2, y0, y1, y2, dm, dzg_ref, d0_ref, d1_ref, d2_ref):
        _, vjp = jax.vjp(_merge_core, g0[...], g1[...], g2[...], y0[...].astype(F32), y1[...].astype(F32),
                         y2[...].astype(F32))
        dg0, dg1, dg2, dy0, dy1, dy2 = vjp(dm[...])
        dzg_ref[:, 0:w] = dg0.astype(BF)
        dzg_ref[:, w:2 * w] = dg1.astype(BF)
        dzg_ref[:, 2 * w:3 * w] = dg2.astype(BF)
        d0_ref[...] = dy0.astype(BF)
        d1_ref[...] = dy1.astype(BF)
        d2_ref[...] = dy2.astype(BF)

    return pl.pallas_call(body, grid=(n // t,),
                          in_specs=[_rows(t, w, 0), _rows(t, w, 1), _rows(t, w, 2)] + [_rows(t, w)] * 4,
                          out_specs=[_rows(t, 3 * w, ZG // (3 * w))] + [_rows(t, w)] * 3,
                          out_shape=[SDS((n, Z_COLS), BF)] + [SDS((n, w), BF)] * 3,
                          compiler_params=_params(("parallel",)), name=name)(z, z, z, y_gm, y_mla, y_mem, dmerged)


def _gm_core(zu, zv, g_ln, b_ln, ws, bcols):
    t = zu.shape[0]
    u = jax.nn.gelu(zu)
    v = _layernorm(jax.nn.gelu(zv), g_ln, b_ln)
    row = lax.broadcasted_iota(jnp.int32, (GM_CHUNK, GM_CHUNK), 0)
    col = lax.broadcasted_iota(jnp.int32, (GM_CHUNK, GM_CHUNK), 1)
    wc = [jnp.where(row >= col, ws[g], 0.0) for g in range(GM_GROUPS)]
    chunks = []
    for c in range(t // GM_CHUNK):
        cols = []
        for g in range(GM_GROUPS):
            vc = v[c * GM_CHUNK:(c + 1) * GM_CHUNK, g * LANES:(g + 1) * LANES]
            cols.append(_mm_nn(wc[g], vc) + bcols[g])
        chunks.append(jnp.concatenate(cols, axis=1))
    mixed = chunks[0] if len(chunks) == 1 else jnp.concatenate(chunks, axis=0)
    return u * mixed


def _gm_specs(t):
    return [_rows(t, GM_WIDTH, ZU // GM_WIDTH), _rows(t, GM_WIDTH, ZV // GM_WIDTH), _full((1, GM_WIDTH)),
            _full((1, GM_WIDTH)), _full((GM_GROUPS, GM_CHUNK, GM_CHUNK))] + [_full((GM_CHUNK, 1))] * GM_GROUPS


def _gm_fwd(z, g_ln, b_ln, ws, bcols, name):
    n = z.shape[0]
    t = min(ROW_TILE, n)

    def body(zu, zv, g_ref, b_ref, ws_ref, c0, c1, c2, c3, o_ref):
        out = _gm_core(zu[...], zv[...], g_ref[...], b_ref[...], [ws_ref[g] for g in range(GM_GROUPS)],
                       [c0[...], c1[...], c2[...], c3[...]])
        o_ref[...] = out.astype(BF)

    return pl.pallas_call(body, grid=(n // t,), in_specs=_gm_specs(t), out_specs=_rows(t, GM_WIDTH),
                          out_shape=SDS((n, GM_WIDTH), BF), compiler_params=_params(("parallel",)),
                          name=name)(z, z, g_ln, b_ln, ws, *bcols)


def _gm_bwd(z, g_ln, b_ln, ws, bcols, dgm, dz, name, comm=None):
    n = z.shape[0]
    t = min(ROW_TILE, n)

    def body(zu, zv, g_ref, b_ref, ws_ref, c0, c1, c2, c3, dgm_ref, _, dz_ref, dg_ref, db_ref, dws_ref, e0, e1, e2,
             e3):
        first = pl.program_id(0) == 0
        _, vjp = jax.vjp(_gm_core, zu[...], zv[...], g_ref[...], b_ref[...],
                         [ws_ref[g] for g in range(GM_GROUPS)], [c0[...], c1[...], c2[...], c3[...]])
        dzu, dzv, dg, db, dws, dcols = vjp(dgm_ref[...])
        dz_ref[:, 0:GM_WIDTH] = dzu.astype(BF)
        dz_ref[:, GM_WIDTH:2 * GM_WIDTH] = dzv.astype(BF)
        _acc(dg_ref, dg, first)
        _acc(db_ref, db, first)
        _acc(dws_ref, jnp.stack(dws, axis=0), first)
        for ref, val in zip((e0, e1, e2, e3), dcols):
            _acc(ref, val, first)

    in_specs = _gm_specs(t) + [_rows(t, GM_WIDTH), ANY]
    return _pcall(
        body, grid=(n // t,), in_specs=in_specs,
        out_specs=[_rows(t, 2 * GM_WIDTH, ZU // (2 * GM_WIDTH)), _full((1, GM_WIDTH)), _full((1, GM_WIDTH)),
                   _full((GM_GROUPS, GM_CHUNK, GM_CHUNK))] + [_full((GM_CHUNK, 1))] * GM_GROUPS,
        out_shape=[SDS((n, Z_COLS), BF), SDS((1, GM_WIDTH), F32), SDS((1, GM_WIDTH), F32),
                   SDS((GM_GROUPS, GM_CHUNK, GM_CHUNK), F32)] + [SDS((GM_CHUNK, 1), F32)] * GM_GROUPS,
        sem=("arbitrary",), name=name, comm=comm, aliases={len(in_specs) - 1: 0})(z, z, g_ln, b_ln, ws, *bcols, dgm, dz)


def _rope_tables(pos_f, inv_full, cmask, smask, name, comm=None):
    n = pos_f.shape[0]
    t = min(ROW_TILE, n)

    def body(p_ref, inv_ref, cm_ref, sm_ref, cos_ref, sin_ref):
        ang = p_ref[...] * inv_ref[...]
        cos_ref[...] = jnp.cos(ang) * cm_ref[...]
        sin_ref[...] = jnp.sin(ang) * sm_ref[...]

    return _pcall(body, grid=(n // t,), in_specs=[_rows(t, 1)] + [_full((1, LANES))] * 3,
                  out_specs=[_rows(t, LANES)] * 2, out_shape=[SDS((n, LANES), F32)] * 2, sem=("parallel",),
                  name=name, comm=comm)(pos_f, inv_full, cmask, smask)


def _prep_norms(cq, ckv, g_cq, g_ckv):
    return _rmsn(cq, g_cq, Q_LORA), _rmsn(ckv, g_ckv, KV_LORA)


def _prep_heads(qa, kva, kpe, head_gains, cos_f, sin_s):
    g_qn, g_qp, g_kn, g_kp = head_gains
    qs = _split_lanes(qa)
    kvs = _split_lanes(kva)
    kp = _rope(_rmsn(kpe, g_kp, MLA_ROPE), cos_f, sin_s)
    q_out, k_out = [], []
    for h in range(MLA_HEADS):
        q_out.append(_rmsn(qs[h], g_qn, MLA_NOPE))
        q_out.append(_rope(_rmsn(qs[MLA_HEADS + h], g_qp, MLA_ROPE), cos_f, sin_s))
        k_out.append(_rmsn(kvs[h], g_kn, MLA_NOPE))
        k_out.append(kp)
    return (jnp.concatenate(q_out, axis=1), jnp.concatenate(k_out, axis=1),
            jnp.concatenate(kvs[MLA_HEADS:], axis=1))


def _prep_in_specs(t):
    return ([_rows(t, Q_LORA, CQ // Q_LORA), _rows(t, LANES, KPE // LANES), _rows(t, KV_LORA, CKV // KV_LORA),
             _rows(t, LANES), _rows(t, LANES), _full((1, Q_LORA)), _full((1, KV_LORA))] + [_full((1, LANES))] * 4
            + [_full((Q_LORA, 2048)), _full((KV_LORA, 2048))])


def _prep_fwd(z, cos_f, sin_s, gains, wq, wkv, name):
    n = z.shape[0]
    t = min(ROW_TILE, n)

    def body(cq, kpe, ckv, cos_ref, sin_ref, g_cq, g_ckv, g_qn, g_qp, g_kn, g_kp, wq_ref, wkv_ref, q_ref, k_ref, v_ref):
        cqn, ckvn = _prep_norms(cq[...], ckv[...], g_cq[...], g_ckv[...])
        qa = _dn(cqn, wq_ref[...], 1, 0)
        kva = _dn(ckvn, wkv_ref[...], 1, 0)
        q, k, v = _prep_heads(qa, kva, kpe[...], (g_qn[...], g_qp[...], g_kn[...], g_kp[...]), cos_ref[...],
                              sin_ref[...])
        q_ref[...] = q.astype(BF)
        k_ref[...] = k.astype(BF)
        v_ref[...] = v.astype(BF)

    return pl.pallas_call(body, grid=(n // t,), in_specs=_prep_in_specs(t),
                          out_specs=[_rows(t, 2048), _rows(t, 2048), _rows(t, 1024)],
                          out_shape=[SDS((n, 2048), BF), SDS((n, 2048), BF), SDS((n, 1024), BF)],
                          compiler_params=_params(("parallel",)),
                          name=name)(z, z, z, cos_f, sin_s, *gains, wq, wkv)


def _prep_bwd(z, cos_f, sin_s, gains, wq, wkv, dq, dk, dv, dz, name, comm=None):
    n = z.shape[0]
    t = min(ROW_TILE, n)
    wz = Q_LORA + LANES + KV_LORA

    def body(cq, kpe, ckv, cos_ref, sin_ref, g_cq, g_ckv, g_qn, g_qp, g_kn, g_kp, wq_ref, wkv_ref, dq_ref, dk_ref,
             dv_ref, _, dz_ref, o_cq, o_ckv, o_qn, o_qp, o_kn, o_kp, dwq_ref, dwkv_ref):
        first = pl.program_id(0) == 0
        cos_t, sin_t = cos_ref[...], sin_ref[...]
        (cqn, ckvn), vjp_norms = jax.vjp(_prep_norms, cq[...], ckv[...], g_cq[...], g_ckv[...])
        wq_t, wkv_t = wq_ref[...], wkv_ref[...]
        qa = _dn(cqn, wq_t, 1, 0)
        kva = _dn(ckvn, wkv_t, 1, 0)
        _, vjp_heads = jax.vjp(lambda a, b, c, g: _prep_heads(a, b, c, g, cos_t, sin_t), qa, kva, kpe[...],
                               (g_qn[...], g_qp[...], g_kn[...], g_kp[...]))
        dqa, dkva, dkpe, dhead = vjp_heads((dq_ref[...], dk_ref[...], dv_ref[...]))
        _acc(dwq_ref, _dn(cqn, dqa, 0, 0), first)
        _acc(dwkv_ref, _dn(ckvn, dkva, 0, 0), first)
        dcq, dckv, dg_cq, dg_ckv = vjp_norms((_dn(dqa, wq_t, 1, 1), _dn(dkva, wkv_t, 1, 1)))
        dz_ref[:, 0:Q_LORA] = dcq.astype(BF)
        dz_ref[:, Q_LORA:Q_LORA + LANES] = dkpe.astype(BF)
        dz_ref[:, Q_LORA + LANES:wz] = dckv.astype(BF)
        for ref, val in zip((o_cq, o_ckv, o_qn, o_qp, o_kn, o_kp), (dg_cq, dg_ckv) + tuple(dhead)):
            _acc(ref, val, first)

    gain_specs = [_full((1, Q_LORA)), _full((1, KV_LORA))] + [_full((1, LANES))] * 4
    gain_shapes = [SDS((1, Q_LORA), F32), SDS((1, KV_LORA), F32)] + [SDS((1, LANES), F32)] * 4
    in_specs = _prep_in_specs(t) + [_rows(t, 2048), _rows(t, 2048), _rows(t, 1024), ANY]
    return _pcall(
        body, grid=(n // t,), in_specs=in_specs,
        out_specs=[_rows(t, wz, CQ // wz)] + gain_specs + [_full((Q_LORA, 2048)), _full((KV_LORA, 2048))],
        out_shape=[SDS((n, Z_COLS), BF)] + gain_shapes + [SDS((Q_LORA, 2048), F32), SDS((KV_LORA, 2048), F32)],
        sem=("arbitrary",), name=name, comm=comm,
        aliases={len(in_specs) - 1: 0})(z, z, z, cos_f, sin_s, *gains, wq, wkv, dq, dk, dv, dz)


MLA_QK = 256
MLA_SCALE = 1.0 / math.sqrt(MLA_NOPE + MLA_ROPE)


def _causal_mask(s, q0, k0):
    tq, tk = s.shape
    row = q0 + lax.broadcasted_iota(jnp.int32, (tq, tk), 0)
    col = k0 + lax.broadcasted_iota(jnp.int32, (tq, tk), 1)
    return jnp.where(row >= col, s, -jnp.inf)


def _mla_fwd(q, k, v, batch, seq, name, comm=None):
    n = q.shape[0]
    tq = min(ATT_TILE, seq)
    nq = seq // tq

    nh = ATT_HEADS

    def body(q_ref, k_ref, v_ref, o_ref, lse_ref):
        i = pl.program_id(2)

        def step(j, carry, diagonal=False):
            k0 = pl.multiple_of(j * tq, tq)
            out = []
            for hh in range(nh):
                m, l, acc = carry[hh]
                qb = q_ref[:, hh * MLA_QK:(hh + 1) * MLA_QK]
                kb = k_ref[pl.ds(k0, tq), hh * MLA_QK:(hh + 1) * MLA_QK]
                vb = v_ref[pl.ds(k0, tq), hh * MLA_V:(hh + 1) * MLA_V]
                s = _dn(qb, kb, 1, 1) * MLA_SCALE
                if diagonal:
                    s = _causal_mask(s, i * tq, k0)
                m_new = jnp.maximum(m, jnp.max(s, axis=-1, keepdims=True))
                p = jnp.exp(s - m_new)
                alpha = jnp.exp(m - m_new)
                l = alpha * l + jnp.sum(p, axis=-1, keepdims=True)
                acc = alpha * acc + _dn(p, vb, 1, 0)
                out.append((m_new, l, acc))
            return tuple(out)

        init = tuple((jnp.full((tq, 1), -jnp.inf, F32), jnp.zeros((tq, 1), F32), jnp.zeros((tq, MLA_V), F32))
                     for _ in range(nh))
        final = step(i, lax.fori_loop(0, i, step, init), diagonal=True)
        for hh, (m, l, acc) in enumerate(final):
            o_ref[:, hh * MLA_V:(hh + 1) * MLA_V] = acc / l
            lse_ref[:, hh * LANES:(hh + 1) * LANES] = jnp.broadcast_to(m + jnp.log(l), (tq, LANES))

    return _pcall(
        body, grid=(batch, MLA_HEADS // nh, nq),
        in_specs=[pl.BlockSpec((tq, nh * MLA_QK), lambda b, h, i: (b * nq + i, h)),
                  pl.BlockSpec((seq, nh * MLA_QK), lambda b, h, i: (b, h)),
                  pl.BlockSpec((seq, nh * MLA_V), lambda b, h, i: (b, h))],
        out_specs=[pl.BlockSpec((tq, nh * MLA_V), lambda b, h, i: (b * nq + i, h)),
                   pl.BlockSpec((tq, nh * LANES), lambda b, h, i: (b * nq + i, h))],
        out_shape=[SDS((n, MLA_HEADS * MLA_V), F32), SDS((n, MLA_HEADS * LANES), F32)],
        sem=("parallel", "parallel", "arbitrary"), name=name, comm=comm)(q, k, v)


def _mla_bwd(q, k, v, o, lse, do, batch, seq, name, comm=None):
    n = q.shape[0]
    tk = min(ATT_TILE, seq)
    nk = seq // tk

    nh = ATT_HEADS

    def body(q_ref, k_ref, v_ref, o_ref, lse_ref, do_ref, dq_ref, dk_ref, dv_ref):
        jk = pl.program_id(2)

        @pl.when(jk == 0)
        def _():
            dq_ref[...] = jnp.zeros_like(dq_ref)

        def step(i, carry, diagonal=False):
            q0 = pl.multiple_of(i * tk, tk)
            rows = pl.ds(q0, tk)
            out = []
            for hh in range(nh):
                dk_acc, dv_acc = carry[hh]
                qk_cols = slice(hh * MLA_QK, (hh + 1) * MLA_QK)
                v_cols = slice(hh * MLA_V, (hh + 1) * MLA_V)
                kb = k_ref[:, qk_cols]
                vb = v_ref[:, v_cols]
                qb = q_ref[rows, qk_cols]
                dob = do_ref[rows, v_cols]
                delta = jnp.sum(dob * o_ref[rows, v_cols], axis=-1, keepdims=True)
                s = _dn(qb, kb, 1, 1) * MLA_SCALE
                if diagonal:
                    s = _causal_mask(s, q0, jk * tk)
                p = jnp.exp(s - lse_ref[rows, hh * LANES:hh * LANES + 1])
                dv_acc = dv_acc + _dn(p, dob, 0, 0)
                dp = _dn(dob, vb, 1, 1)
                ds = p * (dp - delta) * MLA_SCALE
                dk_acc = dk_acc + _dn(ds, qb, 0, 0)
                dq_ref[rows, qk_cols] += _dn(ds, kb, 1, 0)
                out.append((dk_acc, dv_acc))
            return tuple(out)

        init = tuple((jnp.zeros((tk, MLA_QK), F32), jnp.zeros((tk, MLA_V), F32)) for _ in range(nh))
        final = lax.fori_loop(jk + 1, nk, step, step(jk, init, diagonal=True))
        for hh, (dk_acc, dv_acc) in enumerate(final):
            dk_ref[:, hh * MLA_QK:(hh + 1) * MLA_QK] = dk_acc
            dv_ref[:, hh * MLA_V:(hh + 1) * MLA_V] = dv_acc

    full_qk = pl.BlockSpec((seq, nh * MLA_QK), lambda b, h, j: (b, h))
    full_v = pl.BlockSpec((seq, nh * MLA_V), lambda b, h, j: (b, h))
    blk_qk = pl.BlockSpec((tk, nh * MLA_QK), lambda b, h, j: (b * nk + j, h))
    blk_v = pl.BlockSpec((tk, nh * MLA_V), lambda b, h, j: (b * nk + j, h))
    return _pcall(
        body, grid=(batch, MLA_HEADS // nh, nk),
        in_specs=[full_qk, blk_qk, blk_v, full_v, full_v, full_v],
        out_specs=[full_qk, blk_qk, blk_v],
        out_shape=[SDS((n, MLA_HEADS * MLA_QK), F32), SDS((n, MLA_HEADS * MLA_QK), F32),
                   SDS((n, MLA_HEADS * MLA_V), F32)],
        sem=("parallel", "parallel", "arbitrary"), name=name, comm=comm)(q, k, v, o, lse, do)


MEM_SCALE = 1.0 / math.sqrt(HEAD_DIM)
MEM_W = MEM_HEADS * HEAD_DIM


def _mem_core(qs, ks, vs, g_mq, g_mk):
    outs = []
    for h in range(MEM_HEADS):
        qh = _rmsn(qs[h], g_mq, HEAD_DIM)
        kh = _rmsn(ks[h], g_mk, HEAD_DIM)
        p = _softmax(_mm_nt(qh, kh) * MEM_SCALE)
        outs.append(_mm_nn(p, vs[h]))
    return jnp.concatenate(outs, axis=1)


def _mem_load(qm, kvm, g_mq, g_mk):
    hs = range(MEM_HEADS)
    qs = [qm[:, h * LANES:(h + 1) * LANES] for h in hs]
    ks = [kvm[:, h * LANES:(h + 1) * LANES] for h in hs]
    vs = [kvm[:, MEM_W + h * LANES:MEM_W + (h + 1) * LANES] for h in hs]
    return qs, ks, vs, g_mq[...], g_mk[...]


def _mem_fwd(z, kvm, g_mq, g_mk, batch, seq, name, comm=None):
    n = z.shape[0]
    t = min(ROW_TILE, seq)
    per = seq // t

    def body(qm, kvm_ref, gq, gk, o_ref):
        o_ref[...] = _mem_core(*_mem_load(qm, kvm_ref, gq, gk)).astype(BF)

    return _pcall(
        body, grid=(n // t,),
        in_specs=[_rows(t, MEM_W, QM // MEM_W), pl.BlockSpec((MEM_LEN, 2 * MEM_W), lambda i: (i // per, 0)),
                  _full((1, LANES)), _full((1, LANES))],
        out_specs=_rows(t, MEM_W), out_shape=SDS((n, MEM_W), BF), sem=("parallel",), name=name,
        comm=comm)(z, kvm, g_mq, g_mk)


def _mem_bwd(z, kvm, g_mq, g_mk, dom, dz, batch, seq, name):
    n = z.shape[0]
    t = min(ROW_TILE, seq)
    per = seq // t

    def body(qm, kvm_ref, gq, gk, dom_ref, _, dz_ref, dkvm_ref, dgq_ref, dgk_ref):
        i = pl.program_id(0)
        _, vjp = jax.vjp(_mem_core, *_mem_load(qm, kvm_ref, gq, gk))
        dqs, dks, dvs, dgq, dgk = vjp(dom_ref[...])
        dz_ref[...] = jnp.concatenate(dqs, axis=1).astype(BF)
        _acc(dkvm_ref, jnp.concatenate(dks + dvs, axis=1), i % per == 0)
        _acc(dgq_ref, dgq, i == 0)
        _acc(dgk_ref, dgk, i == 0)

    kv_spec = pl.BlockSpec((MEM_LEN, 2 * MEM_W), lambda i: (i // per, 0))
    return pl.pallas_call(
        body, grid=(n // t,),
        in_specs=[_rows(t, MEM_W, QM // MEM_W), kv_spec, _full((1, LANES)), _full((1, LANES)), _rows(t, MEM_W), ANY],
        out_specs=[_rows(t, MEM_W, QM // MEM_W), kv_spec, _full((1, LANES)), _full((1, LANES))],
        out_shape=[SDS((n, Z_COLS), BF), SDS((batch * MEM_LEN, 2 * MEM_W), F32), SDS((1, LANES), F32),
                   SDS((1, LANES), F32)],
        input_output_aliases={5: 0},
        compiler_params=_params(("arbitrary",)), name=name)(z, kvm, g_mq, g_mk, dom, dz)


def _me():
    return lax.axis_index("x"), lax.axis_index("y"), lax.axis_index("c")


def _other_chips(x, y):
    return [(1 - x, y), (x, 1 - y), (1 - x, 1 - y)]


def _shard_shape(name):
    r, c = BIG_SHAPE[name]
    return (r, c // N_CHIPS) if name in COL_SHARDED else (r // N_CHIPS, c)


def _n_pieces(half_rows):
    return max(1, half_rows // PIECE_ROWS)


def _piece_plan(shapes):
    plan = []
    for r, _ in shapes:
        h = r // 2
        n = _n_pieces(h)
        plan.append((h, n, h // n))
    return plan


def _remote(send, recv, sem, src, dst, to):
    return pltpu.make_async_remote_copy(src_ref=src, dst_ref=dst, send_sem=send.at[sem], recv_sem=recv.at[sem],
                                        device_id=to, device_id_type=MESH)


def _gather_far(shards):
    plan = _piece_plan([s.shape for s in shards])
    n_far = 3 * sum(n for _, n, _ in plan)
    n_loc = 2 * sum(n for _, n, _ in plan)

    def copies(s_refs, o_refs, send, recv, local):
        x, y, c = _me()
        k = 2 * x + y
        mine, sends, arrivals = [], [], []
        for t, (h, n, pr) in enumerate(plan):
            s_ref, o_ref = s_refs[t], o_refs[t]
            for core in range(2):
                for p in range(n):
                    rows = pl.ds(core * h + p * pr, pr)
                    mine.append(pltpu.make_async_copy(s_ref.at[rows], o_ref.at[k, rows], local.at[len(mine)]))
            for chip in _other_chips(x, y):
                for p in range(n):
                    rows = pl.ds(c * h + p * pr, pr)
                    s = len(sends)
                    sends.append(_remote(send, recv, s, s_ref.at[rows], o_ref.at[k, rows], (*chip, c)))
                    arrivals.append(_remote(send, recv, s, s_ref.at[rows], o_ref.at[2 * chip[0] + chip[1], rows],
                                            (*chip, c)))
        return sends, arrivals, mine

    return _Phase(shards, [SDS((N_CHIPS,) + s.shape, s.dtype) for s in shards], n_far, n_loc, copies)


def _gather_near(bufs):
    plan = _piece_plan([b.shape[1:] for b in bufs])
    n_sem = 3 * sum(n for _, n, _ in plan)

    def copies(i_refs, o_refs, send, recv, local):
        x, y, c = _me()
        sib = (x, y, 1 - c)
        sends, arrivals = [], []
        for t, (h, n, pr) in enumerate(plan):
            for chip in _other_chips(x, y):
                ci = 2 * chip[0] + chip[1]
                for p in range(n):
                    rows = pl.ds(c * h + p * pr, pr)
                    rows_sib = pl.ds((1 - c) * h + p * pr, pr)
                    s = len(sends)
                    sends.append(_remote(send, recv, s, i_refs[t].at[ci, rows], o_refs[t].at[ci, rows], sib))
                    arrivals.append(_remote(send, recv, s, i_refs[t].at[ci, rows_sib], o_refs[t].at[ci, rows_sib], sib))
        return sends, arrivals, []

    return _Phase(bufs, [SDS(b.shape, b.dtype) for b in bufs], n_sem, 0, copies, {t: t for t in range(len(bufs))})


def _pair_exchange(grads):
    plan = _piece_plan([g.shape[1:] for g in grads])
    n_sem = sum(n for _, n, _ in plan)

    def copies(g_refs, o_refs, send, recv, local):
        x, y, c = _me()
        sends = []
        for t, (h, n, pr) in enumerate(plan):
            for p in range(n):
                sends.append(_remote(send, recv, len(sends), g_refs[t].at[:, pl.ds((1 - c) * h + p * pr, pr)],
                                     o_refs[t].at[:, pl.ds(p * pr, pr)], (x, y, 1 - c)))
        return sends, sends, []

    return _Phase(grads, [SDS((N_CHIPS, g.shape[1] // 2, g.shape[2]), F32) for g in grads], n_sem, 0, copies)


def _pair_add(ck, g, theirs, name):
    _, r, c = g.shape
    (h, n, pr), = _piece_plan([(r, c)])

    def body(ck_ref, g_ref, t_ref, p32_ref, pbf_ref):
        s = g_ref[...] + t_ref[...]
        p32_ref[...] = s
        pbf_ref[...] = s.astype(BF)

    half = pl.BlockSpec((None, pr, c), lambda k, p, ck: (k, p, 0))
    spec = pltpu.PrefetchScalarGridSpec(
        num_scalar_prefetch=1, grid=(N_CHIPS, n),
        in_specs=[pl.BlockSpec((None, pr, c), lambda k, p, ck: (k, ck[0] * n + p, 0)), half], out_specs=[half, half])
    return pl.pallas_call(body, grid_spec=spec, out_shape=[SDS((N_CHIPS, h, c), F32), SDS((N_CHIPS, h, c), BF)],
                          compiler_params=_params(("arbitrary", "arbitrary")), name=name)(ck, g, theirs)


def _scatter_partials(pbfs):
    plan = [(h, _n_pieces(h), h // _n_pieces(h)) for h in [p.shape[1] for p in pbfs]]
    n_sem = 3 * sum(n for _, n, _ in plan)

    def copies(p_refs, o_refs, send, recv, local):
        x, y, c = _me()
        sends = []
        for t, (h, n, pr) in enumerate(plan):
            for j, chip in enumerate(_other_chips(x, y)):
                for p in range(n):
                    rows = pl.ds(p * pr, pr)
                    sends.append(_remote(send, recv, len(sends), p_refs[t].at[2 * chip[0] + chip[1], rows],
                                         o_refs[t].at[j, rows], (*chip, c)))
        return sends, sends, []

    return _Phase(pbfs, [SDS((3,) + p.shape[1:], BF) for p in pbfs], n_sem, 0, copies)


def _sum_chips(ck, p32, slots, name):
    _, h, c = p32.shape
    n = _n_pieces(h)
    pr = h // n

    def body(ck_ref, p_ref, s_ref, o_ref):
        o_ref[...] = ((p_ref[...] + s_ref[0].astype(F32)) + s_ref[1].astype(F32)) + s_ref[2].astype(F32)

    spec = pltpu.PrefetchScalarGridSpec(
        num_scalar_prefetch=1, grid=(n,),
        in_specs=[pl.BlockSpec((None, pr, c), lambda p, ck: (ck[1], p, 0)),
                  pl.BlockSpec((3, pr, c), lambda p, ck: (0, p, 0))],
        out_specs=pl.BlockSpec((pr, c), lambda p, ck: (ck[0] * n + p, 0)))
    return pl.pallas_call(body, grid_spec=spec, out_shape=SDS((2 * h, c), F32),
                          compiler_params=_params(("arbitrary",)), name=name)(ck, p32, slots)


def _join_halves(sums):
    plan = _piece_plan([s.shape for s in sums])
    n_sem = sum(n for _, n, _ in plan)

    def copies(r_refs, o_refs, send, recv, local):
        x, y, c = _me()
        sends, arrivals = [], []
        for t, (h, n, pr) in enumerate(plan):
            for p in range(n):
                rows = pl.ds(c * h + p * pr, pr)
                rows_sib = pl.ds((1 - c) * h + p * pr, pr)
                s = len(sends)
                sends.append(_remote(send, recv, s, r_refs[t].at[rows], o_refs[t].at[rows], (x, y, 1 - c)))
                arrivals.append(_remote(send, recv, s, r_refs[t].at[rows_sib], o_refs[t].at[rows_sib], (x, y, 1 - c)))
        return sends, arrivals, []

    return _Phase(sums, [SDS(s.shape, F32) for s in sums], n_sem, 0, copies, {t: t for t in range(len(sums))})


def _gather_small(s, name):
    def body(s_ref, o_ref, send, recv, local):
        x, y, c = _me()
        me = 4 * x + 2 * y + c
        keep = pltpu.make_async_copy(s_ref, o_ref.at[me], local)
        keep.start()
        sends = []
        for r in range(1, 8):
            fx, fy, fc = (r >> 2) & 1, (r >> 1) & 1, r & 1
            to = (x ^ fx, y ^ fy, c ^ fc)
            sends.append(pltpu.make_async_remote_copy(
                src_ref=s_ref, dst_ref=o_ref.at[me], send_sem=send.at[r - 1], recv_sem=recv.at[r - 1],
                device_id=to, device_id_type=MESH))
        for cp in sends:
            cp.start()
        for r in range(1, 8):
            fx, fy, fc = (r >> 2) & 1, (r >> 1) & 1, r & 1
            src = 4 * (x ^ fx) + 2 * (y ^ fy) + (c ^ fc)
            pltpu.make_async_remote_copy(
                src_ref=s_ref, dst_ref=o_ref.at[src], send_sem=send.at[r - 1], recv_sem=recv.at[r - 1],
                device_id=(x ^ fx, y ^ fy, c ^ fc), device_id_type=MESH).wait_recv()
        for cp in sends:
            cp.wait_send()
        keep.wait()

    return pl.pallas_call(
        body, in_specs=[ANY], out_specs=ANY, out_shape=SDS((8, SMALL_ROWS, LANES), F32),
        scratch_shapes=[pltpu.SemaphoreType.DMA((7,)), pltpu.SemaphoreType.DMA((7,)), pltpu.SemaphoreType.DMA],
        name=name)(s)


def _adam_math(w, g, m, v):
    nm = ADAM_B1 * m + (1.0 - ADAM_B1) * g
    nv = ADAM_B2 * v + (1.0 - ADAM_B2) * (g * g)
    m_hat = nm / (1.0 - ADAM_B1 ** ADAM_STEP)
    v_hat = nv / (1.0 - ADAM_B2 ** ADAM_STEP)
    return -ADAM_LR * (m_hat / (jnp.sqrt(v_hat) + ADAM_EPS) + ADAM_WD * w), nm, nv


def _adamw(w, g, m, v, name):
    _, r, c = w.shape
    t = _pick(r, (256, 128, 64))

    def body(w_ref, g_ref, m_ref, v_ref, go_ref, d_ref, nm_ref, nv_ref):
        g_ = g_ref[...]
        d, nm, nv = _adam_math(w_ref[...], g_, m_ref[...], v_ref[...])
        go_ref[...] = g_
        d_ref[...] = d
        nm_ref[...] = nm
        nv_ref[...] = nv

    lead = pl.BlockSpec((None, t, c), lambda i: (0, i, 0))
    return pl.pallas_call(body, grid=(r // t,), in_specs=[lead, _rows(t, c), lead, lead], out_specs=[lead] * 4,
                          out_shape=[SDS((1, r, c), F32)] * 4, compiler_params=_params(("parallel",)),
                          name=name)(w, g, m, v)


def _small_layout():
    out, r0 = {}, 0
    for n in SMALL:
        size = int(np.prod(SMALL_SHAPE[n]))
        nr = -(-size // LANES)
        out[n] = (r0, nr)
        r0 += nr
    assert r0 <= SMALL_ROWS
    return out, r0


def _pack_small(grads, name):
    layout, used = _small_layout()

    def body(*refs):
        o_ref = refs[-1]
        for n, ref in zip(SMALL, refs[:-1]):
            r0, nr = layout[n]
            if n == "w_spatial":
                for g in range(GM_GROUPS):
                    o_ref[r0 + g * GM_CHUNK:r0 + (g + 1) * GM_CHUNK, :] = ref[g]
            elif n == "b_spatial":
                o_ref[r0:r0 + nr, :] = ref[...]
            else:
                for i in range(nr):
                    o_ref[r0 + i:r0 + i + 1, :] = ref[:, i * LANES:(i + 1) * LANES]
        o_ref[used:SMALL_ROWS, :] = jnp.zeros((SMALL_ROWS - used, LANES), F32)

    return pl.pallas_call(body, out_shape=SDS((SMALL_ROWS, LANES), F32), name=name)(*grads)


def _adamw_small(gathered, ws, ms, vs, name):
    layout, _ = _small_layout()
    n_t = len(SMALL)

    def body(*refs):
        g_ref = refs[0]
        w_refs, m_refs, v_refs = refs[1:1 + n_t], refs[1 + n_t:1 + 2 * n_t], refs[1 + 2 * n_t:1 + 3 * n_t]
        outs = refs[1 + 3 * n_t:1 + 7 * n_t]
        acc = refs[-1]
        total = g_ref[0]
        for j in range(1, 8):
            total = total + g_ref[j]
        acc[...] = total
        for t, n in enumerate(SMALL):
            r0, nr = layout[n]
            o_refs = [outs[t], outs[n_t + t], outs[2 * n_t + t], outs[3 * n_t + t]]
            if n == "w_spatial":
                views = [((0, g), slice(r0 + g * GM_CHUNK, r0 + (g + 1) * GM_CHUNK), slice(None))
                         for g in range(GM_GROUPS)]
            elif n == "b_spatial":
                views = [((0,), slice(r0, r0 + nr), slice(None))]
            else:
                width = SMALL_SHAPE[n][1]
                views = [((slice(None), slice(i * LANES, min((i + 1) * LANES, width))), slice(r0 + i, r0 + i + 1),
                          slice(0, min(LANES, width - i * LANES))) for i in range(nr)]
            for idx, rows, lanes in views:
                g = acc[rows, lanes]
                d, nm, nv = _adam_math(w_refs[t][idx], g, m_refs[t][idx], v_refs[t][idx])
                for ref, val in zip(o_refs, (g, d, nm, nv)):
                    ref[idx] = val

    shapes = [SDS(SMALL_SHAPE[n], F32) for n in SMALL]
    return pl.pallas_call(body, out_shape=shapes * 4, scratch_shapes=[pltpu.VMEM((SMALL_ROWS, LANES), F32)],
                          name=name)(gathered, *ws, *ms, *vs)


def _win_layout(w_in):
    pad = jnp.zeros((w_in.shape[0], LANES - MLA_ROPE), w_in.dtype)
    u, v, cq = w_in[:, 0:512], w_in[:, 512:1024], w_in[:, 1024:1408]
    ckv, kpe, qm, zg = w_in[:, 1408:1664], w_in[:, 1664:1728], w_in[:, 1728:2240], w_in[:, 2240:5312]
    return jnp.concatenate([zg, u, v, qm, cq, kpe, pad, ckv], axis=1)


def _win_unlayout(g):
    zg, u, v, qm = g[:, ZG:ZG + 3072], g[:, ZU:ZU + 512], g[:, ZV:ZV + 512], g[:, QM:QM + 512]
    cq, kpe, ckv = g[:, CQ:CQ + 384], g[:, KPE:KPE + MLA_ROPE], g[:, CKV:CKV + 256]
    return jnp.concatenate([u, v, cq, ckv, kpe, qm, zg], axis=1)


def _wq_layout(w_uq):
    w = w_uq.reshape(Q_LORA, MLA_HEADS, MLA_NOPE + MLA_ROPE)
    nope = w[:, :, :MLA_NOPE].reshape(Q_LORA, MLA_HEADS * MLA_NOPE)
    pe = jnp.pad(w[:, :, MLA_NOPE:], ((0, 0), (0, 0), (0, LANES - MLA_ROPE))).reshape(Q_LORA, MLA_HEADS * LANES)
    return jnp.concatenate([nope, pe], axis=1)


def _wq_unlayout(g):
    nope = g[:, :1024].reshape(Q_LORA, MLA_HEADS, MLA_NOPE)
    pe = g[:, 1024:].reshape(Q_LORA, MLA_HEADS, LANES)[:, :, :MLA_ROPE]
    return jnp.concatenate([nope, pe], axis=2).reshape(Q_LORA, MLA_HEADS * (MLA_NOPE + MLA_ROPE))


def _wkv_layout(w_ukv):
    w = w_ukv.reshape(KV_LORA, MLA_HEADS, MLA_NOPE + MLA_V)
    return jnp.concatenate([w[:, :, :MLA_NOPE].reshape(KV_LORA, 1024), w[:, :, MLA_NOPE:].reshape(KV_LORA, 1024)],
                           axis=1)


def _wkv_unlayout(g):
    kn = g[:, :1024].reshape(KV_LORA, MLA_HEADS, MLA_NOPE)
    v = g[:, 1024:].reshape(KV_LORA, MLA_HEADS, MLA_V)
    return jnp.concatenate([kn, v], axis=2).reshape(KV_LORA, MLA_HEADS * (MLA_NOPE + MLA_V))


def _owner_major(g, name):
    r, c = _shard_shape(name)
    return g.reshape(r, N_CHIPS, c).transpose(1, 0, 2) if name in COL_SHARDED else g.reshape(N_CHIPS, r, c)


def _pad_lanes(g):
    return jnp.pad(g, ((0, 0), (0, LANES - g.shape[1])))


def kernel(x, mem, positions, g_mix, w_in, g_cq, w_uq, g_ckv, w_ukv, g_q_nope, g_q_pe, g_k_nope, g_k_pe, g_gm_ln, b_gm_ln, w_spatial, b_spatial, g_mem, w_mem_kv, g_mq, g_mk, w_o_gm, w_o_mla, w_o_mem, w_out, g_ffn, w_ff1, w_ff2, loss_target, m_g_mix, m_w_in, m_g_cq, m_w_uq, m_g_ckv, m_w_ukv, m_g_q_nope, m_g_q_pe, m_g_k_nope, m_g_k_pe, m_g_gm_ln, m_b_gm_ln, m_w_spatial, m_b_spatial, m_g_mem, m_w_mem_kv, m_g_mq, m_g_mk, m_w_o_gm, m_w_o_mla, m_w_o_mem, m_w_out, m_g_ffn, m_w_ff1, m_w_ff2, v_g_mix, v_w_in, v_g_cq, v_w_uq, v_g_ckv, v_w_ukv, v_g_q_nope, v_g_q_pe, v_g_k_nope, v_g_k_pe, v_g_gm_ln, v_b_gm_ln, v_w_spatial, v_b_spatial, v_g_mem, v_w_mem_kv, v_g_mq, v_g_mk, v_w_o_gm, v_w_o_mla, v_w_o_mem, v_w_out, v_g_ffn, v_w_ff1, v_w_ff2):
    given = dict(locals())
    wts = {n: given[n] for n in WEIGHTS}
    mom = {n: given["m_" + n] for n in WEIGHTS}
    var = {n: given["v_" + n] for n in WEIGHTS}
    batch, seq, _ = x.shape
    n_tok = batch * seq

    def natural(n, g):
        r, c = _shard_shape(n)
        return g.transpose(1, 0, 2).reshape(r, N_CHIPS * c) if n in COL_SHARDED else g.reshape(N_CHIPS * r, c)

    def far(names):
        return _gather_far([wts[n][0].astype(BF) for n in names])

    x2 = x.reshape(n_tok, D_MODEL)
    tgt2 = loss_target.reshape(n_tok, D_MODEL)
    mem2 = mem.reshape(batch * MEM_LEN, D_MODEL)
    pos_f = positions.reshape(n_tok, 1).astype(F32)

    inv = ROPE_BASE ** (-jnp.arange(0, MLA_ROPE, 2, dtype=F32) / MLA_ROPE)
    zeros64 = jnp.zeros((LANES - MLA_ROPE,), F32)
    inv_full = jnp.concatenate([inv, inv, zeros64]).reshape(1, LANES)
    half = MLA_ROPE // 2
    cmask = jnp.concatenate([jnp.ones((MLA_ROPE,), F32), zeros64]).reshape(1, LANES)
    smask = jnp.concatenate([-jnp.ones((half,), F32), jnp.ones((half,), F32), zeros64]).reshape(1, LANES)

    prep_gains = [g_cq, g_ckv, g_q_nope, _pad_lanes(g_q_pe), g_k_nope, _pad_lanes(g_k_pe)]
    ws = w_spatial[0]
    bcols = [b_spatial[0, g].reshape(GM_CHUNK, 1) for g in range(GM_GROUPS)]

    h1, early_far = _rms_fwd(x2, g_mix, "rms_mix", comm=far(EARLY))
    (cos_f, sin_s), early = _rope_tables(pos_f, inv_full, cmask, smask, "rope_tables", comm=_gather_near(early_far))
    full = {n: natural(n, g) for n, g in zip(EARLY, early)}
    win = _win_layout(full["w_in"])
    wq = _wq_layout(full["w_uq"])
    wkv = _wkv_layout(full["w_ukv"])
    z, proj_far = _mm(h1, win, name="mm_in", comm=far(LATE_PROJ))
    gm = _gm_fwd(z, g_gm_ln, b_gm_ln, ws, bcols, "gm_fwd")
    qc, kc, vc = _prep_fwd(z, cos_f, sin_s, prep_gains, wq, wkv, "prep_fwd")
    (o_mla, lse), ff_far = _mla_fwd(qc, kc, vc, batch, seq, "mla_fwd", comm=far(LATE_FF))
    memn = _rms_fwd(mem2, g_mem, "rms_mem")
    kvm, proj = _mm(memn, full["w_mem_kv"], name="mm_memkv", comm=_gather_near(proj_far))
    o_mem, ff = _mem_fwd(z, kvm, g_mq, g_mk, batch, seq, "mem_fwd", comm=_gather_near(ff_far))
    full.update({n: natural(n, g) for n, g in zip(LATE_PROJ + LATE_FF, list(proj) + list(ff))})
    y_gm = _mm(gm, full["w_o_gm"], out_dtypes=(BF,), name="mm_o_gm")
    y_mla = _mm(o_mla, full["w_o_mla"], out_dtypes=(BF,), name="mm_o_mla")
    y_mem = _mm(o_mem, full["w_o_mem"], out_dtypes=(BF,), name="mm_o_mem")
    merged = _merge_fwd(z, y_gm, y_mla, y_mem, "merge_fwd")
    x1 = _mm(merged, full["w_out"], ins=(x2,), epilogue=_add_to, name="mm_out")
    h2 = _rms_fwd(x1, g_ffn, "rms_ffn")
    a_ff, r_ff = _mm(h2, full["w_ff1"], epilogue=_relu2, out_dtypes=(BF, BF), name="mm_ff1")
    dy, dyb, loss_tile = _mm(r_ff, full["w_ff2"], ins=(x1, tgt2), epilogue=_loss_tail, out_dtypes=(F32, BF),
                             total=True, name="mm_ff2")

    gw = {}
    da = _mm(dyb, full["w_ff2"], tb=True, ins=(a_ff,), epilogue=_relu2_bwd, out_dtypes=(BF,), name="mm_d_a")
    gw["w_ff2"] = _owner_major(_mm(r_ff, dyb, ta=True, name="mm_dw_ff2"), "w_ff2")
    gw["w_ff1"] = _mm(h2, da, ta=True, owner_cols=D_FF // N_CHIPS, name="mm_dw_ff1")
    dh2 = _mm(da, full["w_ff1"], tb=True, name="mm_d_h2")
    dx1, dx1b, dg_ffn = _rms_bwd(x1, g_ffn, dh2, dy, "rms_ffn_bwd")
    dmerged = _mm(dx1b, full["w_out"], tb=True, name="mm_d_merged")
    gw["w_out"] = _owner_major(_mm(merged, dx1b, ta=True, name="mm_dw_out"), "w_out")
    dz, dy_gm, dy_mla, dy_mem = _merge_bwd(z, y_gm, y_mla, y_mem, dmerged, "merge_bwd")
    dgm = _mm(dy_gm, full["w_o_gm"], tb=True, name="mm_d_gm")
    gw["w_o_gm"] = _mm(gm, dy_gm, ta=True, owner_cols=D_MODEL // N_CHIPS, name="mm_dw_o_gm")
    do_mla = _mm(dy_mla, full["w_o_mla"], tb=True, name="mm_d_omla")
    gw["w_o_mla"] = _owner_major(_mm(o_mla, dy_mla, ta=True, name="mm_dw_o_mla"), "w_o_mla")
    do_mem = _mm(dy_mem, full["w_o_mem"], tb=True, name="mm_d_omem")
    gw["w_o_mem"] = _mm(o_mem, dy_mem, ta=True, owner_cols=D_MODEL // N_CHIPS, name="mm_dw_o_mem")
    ck = jnp.stack([lax.axis_index("c"), 2 * lax.axis_index("x") + lax.axis_index("y")]).astype(jnp.int32)

    def pair_sums(names, theirs):
        return [_pair_add(ck, gw[n], t, "pair_add_" + n) for n, t in zip(names, theirs)]

    def chip_sums(names, pairs, slots):
        return [_sum_chips(ck, p[0], s, "sum_chips_" + n) for n, p, s in zip(names, pairs, slots)]

    (dz, dg_ln, db_ln, dws, *dbcols), theirs = _gm_bwd(z, g_gm_ln, b_gm_ln, ws, bcols, dgm, dz, "gm_bwd",
                                                      comm=_pair_exchange([gw[n] for n in LATE]))
    pairs = pair_sums(LATE, theirs)
    (dq, dk, dv), slots = _mla_bwd(qc, kc, vc, o_mla, lse, do_mla, batch, seq, "mla_bwd",
                                   comm=_scatter_partials([p[1] for p in pairs]))
    sums = chip_sums(LATE, pairs, slots)
    (dz, dg_cq, dg_ckv, dg_qn, dg_qp, dg_kn, dg_kp, dwq, dwkv), reduced_late = _prep_bwd(
        z, cos_f, sin_s, prep_gains, wq, wkv, dq, dk, dv, dz, "prep_bwd", comm=_join_halves(sums))
    dz, dkvm, dg_mq, dg_mk = _mem_bwd(z, kvm, g_mq, g_mk, do_mem, dz, batch, seq, "mem_bwd")
    dmemn = _mm(dkvm, full["w_mem_kv"], tb=True, name="mm_d_memn")
    gw["w_mem_kv"] = _owner_major(_mm(memn, dkvm, ta=True, name="mm_dw_memkv"), "w_mem_kv")
    _, _, dg_mem = _rms_bwd(mem2, g_mem, dmemn, None, "rms_mem_bwd")
    gw["w_in"] = _owner_major(_win_unlayout(_mm(h1, dz, ta=True, name="mm_dw_in")), "w_in")
    gw["w_uq"] = _owner_major(_wq_unlayout(dwq), "w_uq")
    gw["w_ukv"] = _owner_major(_wkv_unlayout(dwkv), "w_ukv")
    dh1, theirs = _mm(dz, win, tb=True, name="mm_d_h1", comm=_pair_exchange([gw[n] for n in EARLY]))
    pairs = pair_sums(EARLY, theirs)
    (grad_x, _, dg_mix), slots = _rms_bwd(x2, g_mix, dh1, dx1, "rms_mix_bwd",
                                          comm=_scatter_partials([p[1] for p in pairs]))
    reduced_early = _run_phase(_join_halves(chip_sums(EARLY, pairs, slots)), "join_early")
    results = {n: _adamw(wts[n], g, mom[n], var[n], "adamw_" + n)
               for n, g in zip(LATE + EARLY, list(reduced_late) + list(reduced_early))}

    small_g = {"g_mix": dg_mix, "g_cq": dg_cq, "g_ckv": dg_ckv, "g_q_nope": dg_qn, "g_q_pe": dg_qp,
               "g_k_nope": dg_kn, "g_k_pe": dg_kp, "g_gm_ln": dg_ln, "b_gm_ln": db_ln, "w_spatial": dws,
               "b_spatial": jnp.concatenate(dbcols, axis=1).T, "g_mem": dg_mem, "g_mq": dg_mq, "g_mk": dg_mk,
               "g_ffn": dg_ffn}
    packed = _pack_small([small_g[n] for n in SMALL], "pack_small")
    small_out = _adamw_small(_gather_small(packed, "gather_small"), [wts[n] for n in SMALL],
                             [mom[n] for n in SMALL], [var[n] for n in SMALL], "adamw_small")
    for t, n in enumerate(SMALL):
        results[n] = [small_out[j * len(SMALL) + t] for j in range(4)]

    loss = lax.psum(loss_tile[0, 0], ("x", "y", "c"))
    grad_x = grad_x.reshape(batch, seq, D_MODEL)
    return (loss, grad_x, *[results[n][0] for n in WEIGHTS], *[results[n][1] for n in WEIGHTS],
            *[results[n][2] for n in WEIGHTS], *[results[n][3] for n in WEIGHTS])
```

```python
import functools
import math

import numpy as np
import jax
import jax.numpy as jnp
from jax import lax
from jax.experimental import pallas as pl
from jax.experimental.pallas import tpu as pltpu

F32 = jnp.float32
BF = jnp.bfloat16
SDS = jax.ShapeDtypeStruct
MESH = pl.DeviceIdType.MESH

D_MODEL = 1024
MEM_LEN = 256
MEM_HEADS = 4
HEAD_DIM = 128
GM_WIDTH = 512
GM_CHUNK = 128
GM_GROUPS = 4
MLA_HEADS = 8
MLA_NOPE = 128
MLA_ROPE = 64
MLA_V = 128
Q_LORA = 384
KV_LORA = 256
ROPE_BASE = 10000.0
D_FF = 4096
EPS = 1e-6
W_IN_COLS = 5312
ADAM_LR, ADAM_B1, ADAM_B2, ADAM_EPS, ADAM_WD, ADAM_STEP = 0.001, 0.9, 0.999, 1e-08, 0.01, 10

ZG, ZU, ZV, QM, CQ, KPE, CKV = 0, 3072, 3584, 4096, 4608, 4992, 5120
Z_COLS = 5376
LANES = 128
ROW_TILE = 256
ATT_TILE = 512
ATT_HEADS = 2
VMEM_LIMIT = 56 * 1024 * 1024

N_CHIPS = 4
PIECE_ROWS = 256
SMALL_ROWS = 560

BIG = ["w_in", "w_uq", "w_ukv", "w_mem_kv", "w_o_gm", "w_o_mla", "w_o_mem", "w_out", "w_ff1", "w_ff2"]
BIG_SHAPE = {"w_in": (1024, 5312), "w_uq": (384, 1536), "w_ukv": (256, 2048), "w_mem_kv": (1024, 1024),
             "w_o_gm": (512, 1024), "w_o_mla": (1024, 1024), "w_o_mem": (512, 1024), "w_out": (1024, 1024),
             "w_ff1": (1024, 4096), "w_ff2": (4096, 1024)}
COL_SHARDED = {"w_in", "w_uq", "w_ukv", "w_o_gm", "w_o_mem", "w_ff1"}
EARLY = ["w_in", "w_uq", "w_ukv", "w_mem_kv"]
LATE_PROJ = ["w_o_gm", "w_o_mla", "w_o_mem", "w_out"]
LATE_FF = ["w_ff1", "w_ff2"]
LATE = LATE_PROJ + LATE_FF
SMALL = ["w_spatial", "b_spatial", "g_mix", "g_cq", "g_ckv", "g_q_nope", "g_q_pe", "g_k_nope", "g_k_pe", "g_gm_ln",
         "b_gm_ln", "g_mem", "g_mq", "g_mk", "g_ffn"]
SMALL_SHAPE = {"g_mix": (1, 1024), "g_cq": (1, 384), "g_ckv": (1, 256), "g_q_nope": (1, 128), "g_q_pe": (1, 64),
               "g_k_nope": (1, 128), "g_k_pe": (1, 64), "g_gm_ln": (1, 512), "b_gm_ln": (1, 512),
               "w_spatial": (1, 4, 128, 128), "b_spatial": (1, 4, 128), "g_mem": (1, 1024), "g_mq": (1, 128),
               "g_mk": (1, 128), "g_ffn": (1, 1024)}
WEIGHTS = ['g_mix', 'w_in', 'g_cq', 'w_uq', 'g_ckv', 'w_ukv', 'g_q_nope', 'g_q_pe', 'g_k_nope', 'g_k_pe',
           'g_gm_ln', 'b_gm_ln', 'w_spatial', 'b_spatial', 'g_mem', 'w_mem_kv', 'g_mq', 'g_mk', 'w_o_gm',
           'w_o_mla', 'w_o_mem', 'w_out', 'g_ffn', 'w_ff1', 'w_ff2']


def _params(sem=None):
    return pltpu.CompilerParams(vmem_limit_bytes=VMEM_LIMIT, dimension_semantics=sem)


def _pick(n, prefs):
    for p in prefs:
        if n % p == 0:
            return p
    return n


def _full(shape):
    nd = len(shape)
    return pl.BlockSpec(shape, lambda *_: (0,) * nd)


def _rows(t, w, blk=0):
    return pl.BlockSpec((t, w), lambda i: (i, blk))


def _acc(ref, val, first):
    @pl.when(first)
    def _():
        ref[...] = val

    @pl.when(jnp.logical_not(first))
    def _():
        ref[...] += val


ANY = pl.BlockSpec(memory_space=pl.ANY)


class _Phase:
    def __init__(self, operands, out_shapes, n_sem, n_local, copies, aliases=None):
        self.operands, self.out_shapes, self.aliases = list(operands), list(out_shapes), dict(aliases or {})
        self.n_sem, self.n_local, self.copies = n_sem, max(n_local, 1), copies

    def sem_shapes(self):
        return [pltpu.SemaphoreType.DMA((self.n_sem,)), pltpu.SemaphoreType.DMA((self.n_sem,)),
                pltpu.SemaphoreType.DMA((self.n_local,))]

    def start(self, ins, outs, send, recv, local):
        sends, _, locals_ = self.copies(ins, outs, send, recv, local)
        for cp in locals_ + sends:
            cp.start()

    def finish(self, ins, outs, send, recv, local):
        sends, arrivals, locals_ = self.copies(ins, outs, send, recv, local)
        for cp in arrivals:
            cp.wait_recv()
        for cp in sends:
            cp.wait_send()
        for cp in locals_:
            cp.wait()


def _run_phase(phase, name):
    n_in = len(phase.operands)

    def body(*refs):
        ins, outs, sems = refs[:n_in], refs[n_in:n_in + len(phase.out_shapes)], refs[n_in + len(phase.out_shapes):]
        phase.start(ins, outs, *sems)
        phase.finish(ins, outs, *sems)

    return pl.pallas_call(body, in_specs=[ANY] * n_in, out_specs=[ANY] * len(phase.out_shapes),
                          out_shape=phase.out_shapes, scratch_shapes=phase.sem_shapes(),
                          input_output_aliases=phase.aliases, name=name)(*phase.operands)


def _pcall(body, *, grid, in_specs, out_specs, out_shape, scratch_shapes=(), sem=None, name, comm=None, aliases=None):
    single = not isinstance(out_shape, (list, tuple))
    o_specs = [out_specs] if single else list(out_specs)
    o_shape = [out_shape] if single else list(out_shape)
    aliases = dict(aliases or {})
    if comm is None:
        call = pl.pallas_call(body, grid=grid, in_specs=list(in_specs), out_specs=o_specs, out_shape=o_shape,
                              scratch_shapes=list(scratch_shapes), input_output_aliases=aliases,
                              compiler_params=_params(sem), name=name)

        def run_plain(*args):
            res = call(*args)
            return res[0] if single else res

        return run_plain

    n_in, n_out, n_scr = len(in_specs), len(o_specs), len(scratch_shapes)
    nc_in, nc_out = len(comm.operands), len(comm.out_shapes)

    def wrapped(*refs):
        ins, cins = refs[:n_in], refs[n_in:n_in + nc_in]
        o0 = n_in + nc_in
        outs, couts = refs[o0:o0 + n_out], refs[o0 + n_out:o0 + n_out + nc_out]
        s0 = o0 + n_out + nc_out
        scr, csem = refs[s0:s0 + n_scr], refs[s0 + n_scr:]
        ids = [pl.program_id(d) for d in range(len(grid))]
        first = functools.reduce(jnp.logical_and, [i == 0 for i in ids])
        last = functools.reduce(jnp.logical_and, [i == g - 1 for i, g in zip(ids, grid)])

        @pl.when(first)
        def _():
            comm.start(cins, couts, *csem)

        body(*ins, *outs, *scr)

        @pl.when(last)
        def _():
            comm.finish(cins, couts, *csem)

    call = pl.pallas_call(
        wrapped, grid=grid, in_specs=list(in_specs) + [ANY] * nc_in, out_specs=o_specs + [ANY] * nc_out,
        out_shape=o_shape + comm.out_shapes, scratch_shapes=list(scratch_shapes) + comm.sem_shapes(),
        input_output_aliases={**aliases, **{n_in + i: n_out + j for i, j in comm.aliases.items()}},
        compiler_params=_params(("arbitrary",) * len(grid)), name=name)

    def run_carrying(*args):
        res = call(*args, *comm.operands)
        return (res[0] if single else res[:n_out]), res[n_out:]

    return run_carrying


def _dn(a, b, ca, cb):
    return lax.dot_general(a.astype(BF), b.astype(BF), (((ca,), (cb,)), ((), ())), preferred_element_type=F32)


@jax.custom_vjp
def _mm_nn(a, b):
    return _dn(a, b, 1, 0)


def _mm_nn_fwd(a, b):
    return _dn(a, b, 1, 0), (a.astype(BF), b.astype(BF))


def _mm_nn_bwd(res, ct):
    a, b = res
    return _dn(ct, b, 1, 1), _dn(a, ct, 0, 0)


_mm_nn.defvjp(_mm_nn_fwd, _mm_nn_bwd)


@jax.custom_vjp
def _mm_nt(a, b):
    return _dn(a, b, 1, 1)


def _mm_nt_fwd(a, b):
    return _dn(a, b, 1, 1), (a.astype(BF), b.astype(BF))


def _mm_nt_bwd(res, ct):
    a, b = res
    return _dn(ct, b, 1, 0), _dn(ct, a, 0, 0)


_mm_nt.defvjp(_mm_nt_fwd, _mm_nt_bwd)


def _rmsn(x, g, n):
    ms = jnp.sum(x * x, axis=-1, keepdims=True) * (1.0 / n)
    return x * lax.rsqrt(ms + EPS) * g


def _layernorm(x, g, b):
    mu = jnp.mean(x, axis=-1, keepdims=True)
    xc = x - mu
    y = xc * lax.rsqrt(jnp.mean(xc * xc, axis=-1, keepdims=True) + EPS)
    return y * g + b


def _swap_lanes(x):
    half = MLA_ROPE // 2
    lane = lax.broadcasted_iota(jnp.int32, x.shape, 1)
    return jnp.where(lane < half, pltpu.roll(x, LANES - half, axis=1),
                     jnp.where(lane < MLA_ROPE, pltpu.roll(x, half, axis=1), 0.0))


@jax.custom_vjp
def _swap_halves(x):
    return _swap_lanes(x)


_swap_halves.defvjp(lambda x: (_swap_lanes(x), None), lambda _, ct: (_swap_lanes(ct),))


def _rope(x, cos_f, sin_s):
    return x * cos_f + _swap_halves(x) * sin_s


def _lane_blocks(x):
    return tuple(x[:, i * LANES:(i + 1) * LANES] for i in range(x.shape[1] // LANES))


@jax.custom_vjp
def _split_lanes(x):
    return _lane_blocks(x)


_split_lanes.defvjp(lambda x: (_lane_blocks(x), None), lambda _, cts: (jnp.concatenate(cts, axis=1),))


def _softmax(s):
    m = lax.stop_gradient(jnp.max(s, axis=-1, keepdims=True))
    p = jnp.exp(s - m)
    return p / jnp.sum(p, axis=-1, keepdims=True)


def _mm(a, b, *, ta=False, tb=False, ins=(), epilogue=None, out_dtypes=(F32,), owner_cols=None, total=False, name,
        comm=None, rows=None, into=None):
    if ta:
        k_dim, m = a.shape
    else:
        m, k_dim = a.shape
    if tb:
        n, kb = b.shape
    else:
        kb, n = b.shape
    assert k_dim == kb, (a.shape, b.shape, ta, tb)
    part, n_parts = rows if rows is not None else (0, 1)
    tm = _pick(m // n_parts, (1024, 512, 256, 128))
    tn = _pick(n if owner_cols is None else owner_cols, (1024, 768, 512, 384, 256, 128))
    tk = _pick(k_dim, (2048, 1024, 768, 512, 256, 128))
    nk = k_dim // tk
    m_steps = m // tm // n_parts
    off = part * m_steps
    ca = 0 if ta else 1
    cb = 1 if tb else 0
    n_in = len(ins)
    n_out = len(out_dtypes)
    n_pass = 0 if into is None else 1

    def finish(r, in_refs, out_refs, first_tile):
        vals = epilogue(r, *[ref[...].astype(F32) for ref in in_refs]) if epilogue is not None else (r,)
        for ref, val, dt in zip(out_refs, vals, out_dtypes):
            ref[...] = val.astype(dt)
        if total:
            _acc(out_refs[n_out], vals[n_out], first_tile)

    def body(*refs):
        a_ref, b_ref = refs[:2]
        in_refs = refs[2:2 + n_in]
        o0 = 2 + n_in + n_pass
        out_refs = refs[o0:o0 + n_out + int(total)]
        first_tile = jnp.logical_and(pl.program_id(0) == 0, pl.program_id(1) == 0)
        part = _dn(a_ref[...], b_ref[...], ca, cb)
        if nk == 1:
            finish(part, in_refs, out_refs, first_tile)
            return
        acc = refs[-1]
        k = pl.program_id(2)
        _acc(acc, part, k == 0)

        @pl.when(k == nk - 1)
        def _():
            finish(acc[...], in_refs, out_refs, first_tile)

    a_spec = (pl.BlockSpec((tk, tm), lambda i, j, k: (k, i + off)) if ta
              else pl.BlockSpec((tm, tk), lambda i, j, k: (i + off, k)))
    b_spec = pl.BlockSpec((tn, tk), lambda i, j, k: (j, k)) if tb else pl.BlockSpec((tk, tn), lambda i, j, k: (k, j))
    t_spec = pl.BlockSpec((tm, tn), lambda i, j, k: (i + off, j))
    if owner_cols is None:
        o_spec, o_shape = t_spec, (m, n)
    else:
        per = owner_cols // tn
        o_spec = pl.BlockSpec((None, tm, tn), lambda i, j, k: (j // per, i + off, j % per))
        o_shape = (n // owner_cols, m, owner_cols)
    o_specs = [o_spec] * n_out + ([pl.BlockSpec((8, LANES), lambda i, j, k: (0, 0))] if total else [])
    o_shapes = [SDS(o_shape, dt) for dt in out_dtypes] + ([SDS((8, LANES), F32)] if total else [])
    in_specs = [a_spec, b_spec] + [t_spec] * n_in + [ANY] * n_pass
    args = [a, b, *ins] + ([into] if n_pass else [])
    run = _pcall(body, grid=(m_steps, n // tn, nk), in_specs=in_specs, out_specs=o_specs, out_shape=o_shapes,
                 scratch_shapes=[pltpu.VMEM((tm, tn), F32)] if nk > 1 else [],
                 sem=("arbitrary",) * 3 if total else ("parallel", "parallel", "arbitrary"), name=name, comm=comm,
                 aliases={len(in_specs) - 1: 0} if n_pass else None)
    if comm is None:
        outs = run(*args)
        return outs[0] if len(outs) == 1 else outs
    outs, exchanged = run(*args)
    return (outs[0] if len(outs) == 1 else outs), exchanged


def _add_to(r, x):
    return (r + x,)


def _relu2(r):
    p = jnp.maximum(r, 0.0)
    return r, p * p


def _relu2_bwd(dr, a):
    return (dr * (2.0 * jnp.maximum(a, 0.0)),)


def _loss_tail(r, x1, tgt):
    e = (r + x1) - tgt
    dy = e * (1.0 / D_MODEL)
    part = jnp.sum(jnp.sum(e * e, axis=-1, keepdims=True), axis=0, keepdims=True) * (0.5 / D_MODEL)
    return dy, dy, jnp.broadcast_to(part, (8, LANES))


def _rms_fwd(x, g, name, comm=None):
    n, w = x.shape
    t = min(ROW_TILE, n)

    def body(x_ref, g_ref, o_ref):
        o_ref[...] = _rmsn(x_ref[...], g_ref[...], w).astype(BF)

    return _pcall(body, grid=(n // t,), in_specs=[_rows(t, w), _full((1, w))], out_specs=_rows(t, w),
                  out_shape=SDS((n, w), BF), sem=("arbitrary",), name=name, comm=comm)(x, g)


def _rms_bwd(x, g, dh, res, name, comm=None):
    n, w = x.shape
    t = min(ROW_TILE, n)
    has_res = res is not None

    def body(*refs):
        if has_res:
            x_ref, g_ref, dh_ref, res_ref, dx_ref, dxb_ref, dg_ref = refs
        else:
            x_ref, g_ref, dh_ref, dx_ref, dxb_ref, dg_ref = refs
        _, vjp = jax.vjp(lambda xx, gg: _rmsn(xx, gg, w), x_ref[...], g_ref[...])
        dx, dg = vjp(dh_ref[...])
        if has_res:
            dx = dx + res_ref[...]
        dx_ref[...] = dx
        dxb_ref[...] = dx.astype(BF)
        _acc(dg_ref, dg, pl.program_id(0) == 0)

    in_specs = [_rows(t, w), _full((1, w)), _rows(t, w)] + ([_rows(t, w)] if has_res else [])
    args = [x, g, dh] + ([res] if has_res else [])
    return _pcall(body, grid=(n // t,), in_specs=in_specs, out_specs=[_rows(t, w), _rows(t, w), _full((1, w))],
                  out_shape=[SDS((n, w), F32), SDS((n, w), BF), SDS((1, w), F32)], sem=("arbitrary",), name=name,
                  comm=comm)(*args)


def _loss_call(y, tgt, name):
    n, w = y.shape
    t = min(ROW_TILE, n)

    def body(y_ref, t_ref, dy_ref, dyb_ref, l_ref):
        e = y_ref[...] - t_ref[...]
        dy = e * (1.0 / w)
        dy_ref[...] = dy
        dyb_ref[...] = dy.astype(BF)
        part = jnp.sum(jnp.sum(e * e, axis=-1, keepdims=True), axis=0, keepdims=True) * (0.5 / w)
        _acc(l_ref, jnp.broadcast_to(part, (8, LANES)), pl.program_id(0) == 0)

    return pl.pallas_call(body, grid=(n // t,), in_specs=[_rows(t, w), _rows(t, w)],
                          out_specs=[_rows(t, w), _rows(t, w), _full((8, LANES))],
                          out_shape=[SDS((n, w), F32), SDS((n, w), BF), SDS((8, LANES), F32)],
                          compiler_params=_params(("arbitrary",)), name=name)(y, tgt)


def _merge_core(zg0, zg1, zg2, y0, y1, y2):
    return jax.nn.sigmoid(zg0) * y0 + jax.nn.sigmoid(zg1) * y1 + jax.nn.sigmoid(zg2) * y2


def _merge_fwd(z, y_gm, y_mla, y_mem, name):
    n = z.shape[0]
    t = min(ROW_TILE, n)
    w = D_MODEL

    def body(g0, g1, g2, y0, y1, y2, o_ref):
        o_ref[...] = _merge_core(g0[...].astype(F32), g1[...].astype(F32), g2[...].astype(F32), y0[...].astype(F32), y1[...].astype(F32),
                                 y2[...].astype(F32)).astype(BF)

    return pl.pallas_call(body, grid=(n // t,),
                          in_specs=[_rows(t, w, 0), _rows(t, w, 1), _rows(t, w, 2)] + [_rows(t, w)] * 3,
                          out_specs=_rows(t, w), out_shape=SDS((n, w), BF),
                          compiler_params=_params(("parallel",)), name=name)(z, z, z, y_gm, y_mla, y_mem)


def _merge_bwd(z, y_gm, y_mla, y_mem, dmerged, name):
    n = z.shape[0]
    t = min(ROW_TILE, n)
    w = D_MODEL

    def body(g0, g1, g2, y0, y1, y2, dm, dzg_ref, d0_ref, d1_ref, d2_ref):
        _, vjp = jax.vjp(_merge_core, g0[...].astype(F32), g1[...].astype(F32), g2[...].astype(F32), y0[...].astype(F32), y1[...].astype(F32),
                         y2[...].astype(F32))
        dg0, dg1, dg2, dy0, dy1, dy2 = vjp(dm[...])
        dzg_ref[:, 0:w] = dg0.astype(BF)
        dzg_ref[:, w:2 * w] = dg1.astype(BF)
        dzg_ref[:, 2 * w:3 * w] = dg2.astype(BF)
        d0_ref[...] = dy0.astype(BF)
        d1_ref[...] = dy1.astype(BF)
        d2_ref[...] = dy2.astype(BF)

    return pl.pallas_call(body, grid=(n // t,),
                          in_specs=[_rows(t, w, 0), _rows(t, w, 1), _rows(t, w, 2)] + [_rows(t, w)] * 4,
                          out_specs=[_rows(t, 3 * w, ZG // (3 * w))] + [_rows(t, w)] * 3,
                          out_shape=[SDS((n, Z_COLS), BF)] + [SDS((n, w), BF)] * 3,
                          compiler_params=_params(("parallel",)), name=name)(z, z, z, y_gm, y_mla, y_mem, dmerged)


def _gm_core(zu, zv, g_ln, b_ln, ws, bcols):
    t = zu.shape[0]
    u = jax.nn.gelu(zu)
    v = _layernorm(jax.nn.gelu(zv), g_ln, b_ln)
    row = lax.broadcasted_iota(jnp.int32, (GM_CHUNK, GM_CHUNK), 0)
    col = lax.broadcasted_iota(jnp.int32, (GM_CHUNK, GM_CHUNK), 1)
    wc = [jnp.where(row >= col, ws[g], 0.0) for g in range(GM_GROUPS)]
    chunks = []
    for c in range(t // GM_CHUNK):
        cols = []
        for g in range(GM_GROUPS):
            vc = v[c * GM_CHUNK:(c + 1) * GM_CHUNK, g * LANES:(g + 1) * LANES]
            cols.append(_mm_nn(wc[g], vc) + bcols[g])
        chunks.append(jnp.concatenate(cols, axis=1))
    mixed = chunks[0] if len(chunks) == 1 else jnp.concatenate(chunks, axis=0)
    return u * mixed


def _gm_specs(t):
    return [_rows(t, GM_WIDTH, ZU // GM_WIDTH), _rows(t, GM_WIDTH, ZV // GM_WIDTH), _full((1, GM_WIDTH)),
            _full((1, GM_WIDTH)), _full((GM_GROUPS, GM_CHUNK, GM_CHUNK))] + [_full((GM_CHUNK, 1))] * GM_GROUPS


def _gm_fwd(z, g_ln, b_ln, ws, bcols, name):
    n = z.shape[0]
    t = min(ROW_TILE, n)

    def body(zu, zv, g_ref, b_ref, ws_ref, c0, c1, c2, c3, o_ref):
        out = _gm_core(zu[...].astype(F32), zv[...].astype(F32), g_ref[...], b_ref[...], [ws_ref[g] for g in range(GM_GROUPS)],
                       [c0[...], c1[...], c2[...], c3[...]])
        o_ref[...] = out.astype(BF)

    return pl.pallas_call(body, grid=(n // t,), in_specs=_gm_specs(t), out_specs=_rows(t, GM_WIDTH),
                          out_shape=SDS((n, GM_WIDTH), BF), compiler_params=_params(("parallel",)),
                          name=name)(z, z, g_ln, b_ln, ws, *bcols)


def _gm_bwd(z, g_ln, b_ln, ws, bcols, dgm, dz, name, comm=None):
    n = z.shape[0]
    t = min(ROW_TILE, n)

    def body(zu, zv, g_ref, b_ref, ws_ref, c0, c1, c2, c3, dgm_ref, _, dz_ref, dg_ref, db_ref, dws_ref, e0, e1, e2,
             e3):
        first = pl.program_id(0) == 0
        _, vjp = jax.vjp(_gm_core, zu[...].astype(F32), zv[...].astype(F32), g_ref[...], b_ref[...],
                         [ws_ref[g] for g in range(GM_GROUPS)], [c0[...], c1[...], c2[...], c3[...]])
        dzu, dzv, dg, db, dws, dcols = vjp(dgm_ref[...])
        dz_ref[:, 0:GM_WIDTH] = dzu.astype(BF)
        dz_ref[:, GM_WIDTH:2 * GM_WIDTH] = dzv.astype(BF)
        _acc(dg_ref, dg, first)
        _acc(db_ref, db, first)
        _acc(dws_ref, jnp.stack(dws, axis=0), first)
        for ref, val in zip((e0, e1, e2, e3), dcols):
            _acc(ref, val, first)

    in_specs = _gm_specs(t) + [_rows(t, GM_WIDTH), ANY]
    return _pcall(
        body, grid=(n // t,), in_specs=in_specs,
        out_specs=[_rows(t, 2 * GM_WIDTH, ZU // (2 * GM_WIDTH)), _full((1, GM_WIDTH)), _full((1, GM_WIDTH)),
                   _full((GM_GROUPS, GM_CHUNK, GM_CHUNK))] + [_full((GM_CHUNK, 1))] * GM_GROUPS,
        out_shape=[SDS((n, Z_COLS), BF), SDS((1, GM_WIDTH), F32), SDS((1, GM_WIDTH), F32),
                   SDS((GM_GROUPS, GM_CHUNK, GM_CHUNK), F32)] + [SDS((GM_CHUNK, 1), F32)] * GM_GROUPS,
        sem=("arbitrary",), name=name, comm=comm, aliases={len(in_specs) - 1: 0})(z, z, g_ln, b_ln, ws, *bcols, dgm, dz)


def _rope_tables(pos_f, inv_full, cmask, smask, name, comm=None):
    n = pos_f.shape[0]
    t = min(ROW_TILE, n)

    def body(p_ref, inv_ref, cm_ref, sm_ref, cos_ref, sin_ref):
        ang = p_ref[...] * inv_ref[...]
        cos_ref[...] = jnp.cos(ang) * cm_ref[...]
        sin_ref[...] = jnp.sin(ang) * sm_ref[...]

    return _pcall(body, grid=(n // t,), in_specs=[_rows(t, 1)] + [_full((1, LANES))] * 3,
                  out_specs=[_rows(t, LANES)] * 2, out_shape=[SDS((n, LANES), F32)] * 2, sem=("parallel",),
                  name=name, comm=comm)(pos_f, inv_full, cmask, smask)


def _prep_norms(cq, ckv, g_cq, g_ckv):
    return _rmsn(cq, g_cq, Q_LORA), _rmsn(ckv, g_ckv, KV_LORA)


def _prep_heads(qa, kva, kpe, head_gains, cos_f, sin_s):
    g_qn, g_qp, g_kn, g_kp = head_gains
    qs = _split_lanes(qa)
    kvs = _split_lanes(kva)
    kp = _rope(_rmsn(kpe, g_kp, MLA_ROPE), cos_f, sin_s)
    q_out, k_out = [], []
    for h in range(MLA_HEADS):
        q_out.append(_rmsn(qs[h], g_qn, MLA_NOPE))
        q_out.append(_rope(_rmsn(qs[MLA_HEADS + h], g_qp, MLA_ROPE), cos_f, sin_s))
        k_out.append(_rmsn(kvs[h], g_kn, MLA_NOPE))
        k_out.append(kp)
    return (jnp.concatenate(q_out, axis=1), jnp.concatenate(k_out, axis=1),
            jnp.concatenate(kvs[MLA_HEADS:], axis=1))


def _prep_in_specs(t):
    return ([_rows(t, Q_LORA, CQ // Q_LORA), _rows(t, LANES, KPE // LANES), _rows(t, KV_LORA, CKV // KV_LORA),
             _rows(t, LANES), _rows(t, LANES), _full((1, Q_LORA)), _full((1, KV_LORA))] + [_full((1, LANES))] * 4
            + [_full((Q_LORA, 2048)), _full((KV_LORA, 2048))])


def _prep_fwd(z, cos_f, sin_s, gains, wq, wkv, name):
    n = z.shape[0]
    t = min(ROW_TILE, n)

    def body(cq, kpe, ckv, cos_ref, sin_ref, g_cq, g_ckv, g_qn, g_qp, g_kn, g_kp, wq_ref, wkv_ref, q_ref, k_ref, v_ref):
        cqn, ckvn = _prep_norms(cq[...].astype(F32), ckv[...].astype(F32), g_cq[...], g_ckv[...])
        qa = _dn(cqn, wq_ref[...], 1, 0)
        kva = _dn(ckvn, wkv_ref[...], 1, 0)
        q, k, v = _prep_heads(qa, kva, kpe[...].astype(F32), (g_qn[...], g_qp[...], g_kn[...], g_kp[...]), cos_ref[...],
                              sin_ref[...])
        q_ref[...] = q.astype(BF)
        k_ref[...] = k.astype(BF)
        v_ref[...] = v.astype(BF)

    return pl.pallas_call(body, grid=(n // t,), in_specs=_prep_in_specs(t),
                          out_specs=[_rows(t, 2048), _rows(t, 2048), _rows(t, 1024)],
                          out_shape=[SDS((n, 2048), BF), SDS((n, 2048), BF), SDS((n, 1024), BF)],
                          compiler_params=_params(("parallel",)),
                          name=name)(z, z, z, cos_f, sin_s, *gains, wq, wkv)


def _prep_bwd(z, cos_f, sin_s, gains, wq, wkv, dq, dk, dv, dz, name, comm=None):
    n = z.shape[0]
    t = min(ROW_TILE, n)
    wz = Q_LORA + LANES + KV_LORA

    def body(cq, kpe, ckv, cos_ref, sin_ref, g_cq, g_ckv, g_qn, g_qp, g_kn, g_kp, wq_ref, wkv_ref, dq_ref, dk_ref,
             dv_ref, _, dz_ref, o_cq, o_ckv, o_qn, o_qp, o_kn, o_kp, dwq_ref, dwkv_ref):
        first = pl.program_id(0) == 0
        cos_t, sin_t = cos_ref[...], sin_ref[...]
        (cqn, ckvn), vjp_norms = jax.vjp(_prep_norms, cq[...].astype(F32), ckv[...].astype(F32), g_cq[...], g_ckv[...])
        wq_t, wkv_t = wq_ref[...], wkv_ref[...]
        qa = _dn(cqn, wq_t, 1, 0)
        kva = _dn(ckvn, wkv_t, 1, 0)
        _, vjp_heads = jax.vjp(lambda a, b, c, g: _prep_heads(a, b, c, g, cos_t, sin_t), qa, kva, kpe[...].astype(F32),
                               (g_qn[...], g_qp[...], g_kn[...], g_kp[...]))
        dqa, dkva, dkpe, dhead = vjp_heads((dq_ref[...], dk_ref[...], dv_ref[...]))
        _acc(dwq_ref, _dn(cqn, dqa, 0, 0), first)
        _acc(dwkv_ref, _dn(ckvn, dkva, 0, 0), first)
        dcq, dckv, dg_cq, dg_ckv = vjp_norms((_dn(dqa, wq_t, 1, 1), _dn(dkva, wkv_t, 1, 1)))
        dz_ref[:, 0:Q_LORA] = dcq.astype(BF)
        dz_ref[:, Q_LORA:Q_LORA + LANES] = dkpe.astype(BF)
        dz_ref[:, Q_LORA + LANES:wz] = dckv.astype(BF)
        for ref, val in zip((o_cq, o_ckv, o_qn, o_qp, o_kn, o_kp), (dg_cq, dg_ckv) + tuple(dhead)):
            _acc(ref, val, first)

    gain_specs = [_full((1, Q_LORA)), _full((1, KV_LORA))] + [_full((1, LANES))] * 4
    gain_shapes = [SDS((1, Q_LORA), F32), SDS((1, KV_LORA), F32)] + [SDS((1, LANES), F32)] * 4
    in_specs = _prep_in_specs(t) + [_rows(t, 2048), _rows(t, 2048), _rows(t, 1024), ANY]
    return _pcall(
        body, grid=(n // t,), in_specs=in_specs,
        out_specs=[_rows(t, wz, CQ // wz)] + gain_specs + [_full((Q_LORA, 2048)), _full((KV_LORA, 2048))],
        out_shape=[SDS((n, Z_COLS), BF)] + gain_shapes + [SDS((Q_LORA, 2048), F32), SDS((KV_LORA, 2048), F32)],
        sem=("arbitrary",), name=name, comm=comm,
        aliases={len(in_specs) - 1: 0})(z, z, z, cos_f, sin_s, *gains, wq, wkv, dq, dk, dv, dz)


MLA_QK = 256
MLA_SCALE = 1.0 / math.sqrt(MLA_NOPE + MLA_ROPE)


def _causal_mask(s, q0, k0):
    tq, tk = s.shape
    row = q0 + lax.broadcasted_iota(jnp.int32, (tq, tk), 0)
    col = k0 + lax.broadcasted_iota(jnp.int32, (tq, tk), 1)
    return jnp.where(row >= col, s, -jnp.inf)


def _mla_fwd(q, k, v, batch, seq, name, comm=None):
    n = q.shape[0]
    tq = min(ATT_TILE, seq)
    nq = seq // tq

    nh = ATT_HEADS

    def body(q_ref, k_ref, v_ref, o_ref, lse_ref):
        i = pl.program_id(2)

        def step(j, carry, diagonal=False):
            k0 = pl.multiple_of(j * tq, tq)
            out = []
            for hh in range(nh):
                m, l, acc = carry[hh]
                qb = q_ref[:, hh * MLA_QK:(hh + 1) * MLA_QK]
                kb = k_ref[pl.ds(k0, tq), hh * MLA_QK:(hh + 1) * MLA_QK]
                vb = v_ref[pl.ds(k0, tq), hh * MLA_V:(hh + 1) * MLA_V]
                s = _dn(qb, kb, 1, 1) * MLA_SCALE
                if diagonal:
                    s = _causal_mask(s, i * tq, k0)
                m_new = jnp.maximum(m, jnp.max(s, axis=-1, keepdims=True))
                p = jnp.exp(s - m_new)
                alpha = jnp.exp(m - m_new)
                l = alpha * l + jnp.sum(p, axis=-1, keepdims=True)
                acc = alpha * acc + _dn(p, vb, 1, 0)
                out.append((m_new, l, acc))
            return tuple(out)

        init = tuple((jnp.full((tq, 1), -jnp.inf, F32), jnp.zeros((tq, 1), F32), jnp.zeros((tq, MLA_V), F32))
                     for _ in range(nh))
        final = step(i, lax.fori_loop(0, i, step, init), diagonal=True)
        for hh, (m, l, acc) in enumerate(final):
            o_ref[:, hh * MLA_V:(hh + 1) * MLA_V] = acc / l
            lse_ref[:, hh * LANES:(hh + 1) * LANES] = jnp.broadcast_to(m + jnp.log(l), (tq, LANES))

    return _pcall(
        body, grid=(batch, MLA_HEADS // nh, nq),
        in_specs=[pl.BlockSpec((tq, nh * MLA_QK), lambda b, h, i: (b * nq + i, h)),
                  pl.BlockSpec((seq, nh * MLA_QK), lambda b, h, i: (b, h)),
                  pl.BlockSpec((seq, nh * MLA_V), lambda b, h, i: (b, h))],
        out_specs=[pl.BlockSpec((tq, nh * MLA_V), lambda b, h, i: (b * nq + i, h)),
                   pl.BlockSpec((tq, nh * LANES), lambda b, h, i: (b * nq + i, h))],
        out_shape=[SDS((n, MLA_HEADS * MLA_V), F32), SDS((n, MLA_HEADS * LANES), F32)],
        sem=("parallel", "parallel", "arbitrary"), name=name, comm=comm)(q, k, v)


def _mla_bwd(q, k, v, o, lse, do, batch, seq, name, comm=None):
    n = q.shape[0]
    tk = min(ATT_TILE, seq)
    nk = seq // tk

    nh = ATT_HEADS

    def body(q_ref, k_ref, v_ref, o_ref, lse_ref, do_ref, dq_ref, dk_ref, dv_ref):
        jk = pl.program_id(2)

        @pl.when(jk == 0)
        def _():
            dq_ref[...] = jnp.zeros_like(dq_ref)

        def step(i, carry, diagonal=False):
            q0 = pl.multiple_of(i * tk, tk)
            rows = pl.ds(q0, tk)
            out = []
            for hh in range(nh):
                dk_acc, dv_acc = carry[hh]
                qk_cols = slice(hh * MLA_QK, (hh + 1) * MLA_QK)
                v_cols = slice(hh * MLA_V, (hh + 1) * MLA_V)
                kb = k_ref[:, qk_cols]
                vb = v_ref[:, v_cols]
                qb = q_ref[rows, qk_cols]
                dob = do_ref[rows, v_cols]
                delta = jnp.sum(dob * o_ref[rows, v_cols], axis=-1, keepdims=True)
                s = _dn(qb, kb, 1, 1) * MLA_SCALE
                if diagonal:
                    s = _causal_mask(s, q0, jk * tk)
                p = jnp.exp(s - lse_ref[rows, hh * LANES:hh * LANES + 1])
                dv_acc = dv_acc + _dn(p, dob, 0, 0)
                dp = _dn(dob, vb, 1, 1)
                ds = p * (dp - delta) * MLA_SCALE
                dk_acc = dk_acc + _dn(ds, qb, 0, 0)
                dq_ref[rows, qk_cols] += _dn(ds, kb, 1, 0)
                out.append((dk_acc, dv_acc))
            return tuple(out)

        init = tuple((jnp.zeros((tk, MLA_QK), F32), jnp.zeros((tk, MLA_V), F32)) for _ in range(nh))
        final = lax.fori_loop(jk + 1, nk, step, step(jk, init, diagonal=True))
        for hh, (dk_acc, dv_acc) in enumerate(final):
            dk_ref[:, hh * MLA_QK:(hh + 1) * MLA_QK] = dk_acc
            dv_ref[:, hh * MLA_V:(hh + 1) * MLA_V] = dv_acc

    full_qk = pl.BlockSpec((seq, nh * MLA_QK), lambda b, h, j: (b, h))
    full_v = pl.BlockSpec((seq, nh * MLA_V), lambda b, h, j: (b, h))
    blk_qk = pl.BlockSpec((tk, nh * MLA_QK), lambda b, h, j: (b * nk + j, h))
    blk_v = pl.BlockSpec((tk, nh * MLA_V), lambda b, h, j: (b * nk + j, h))
    return _pcall(
        body, grid=(batch, MLA_HEADS // nh, nk),
        in_specs=[full_qk, blk_qk, blk_v, full_v, full_v, full_v],
        out_specs=[full_qk, blk_qk, blk_v],
        out_shape=[SDS((n, MLA_HEADS * MLA_QK), F32), SDS((n, MLA_HEADS * MLA_QK), F32),
                   SDS((n, MLA_HEADS * MLA_V), F32)],
        sem=("parallel", "parallel", "arbitrary"), name=name, comm=comm)(q, k, v, o, lse, do)


MEM_SCALE = 1.0 / math.sqrt(HEAD_DIM)
MEM_W = MEM_HEADS * HEAD_DIM


def _mem_core(qs, ks, vs, g_mq, g_mk):
    outs = []
    for h in range(MEM_HEADS):
        qh = _rmsn(qs[h], g_mq, HEAD_DIM)
        kh = _rmsn(ks[h], g_mk, HEAD_DIM)
        p = _softmax(_mm_nt(qh, kh) * MEM_SCALE)
        outs.append(_mm_nn(p, vs[h]))
    return jnp.concatenate(outs, axis=1)


def _mem_load(qm, kvm, g_mq, g_mk):
    hs = range(MEM_HEADS)
    qs = [qm[:, h * LANES:(h + 1) * LANES].astype(F32) for h in hs]
    ks = [kvm[:, h * LANES:(h + 1) * LANES] for h in hs]
    vs = [kvm[:, MEM_W + h * LANES:MEM_W + (h + 1) * LANES] for h in hs]
    return qs, ks, vs, g_mq[...], g_mk[...]


def _mem_fwd(z, kvm, g_mq, g_mk, batch, seq, name, comm=None):
    n = z.shape[0]
    t = min(ROW_TILE, seq)
    per = seq // t

    def body(qm, kvm_ref, gq, gk, o_ref):
        o_ref[...] = _mem_core(*_mem_load(qm, kvm_ref, gq, gk)).astype(BF)

    return _pcall(
        body, grid=(n // t,),
        in_specs=[_rows(t, MEM_W, QM // MEM_W), pl.BlockSpec((MEM_LEN, 2 * MEM_W), lambda i: (i // per, 0)),
                  _full((1, LANES)), _full((1, LANES))],
        out_specs=_rows(t, MEM_W), out_shape=SDS((n, MEM_W), BF), sem=("parallel",), name=name,
        comm=comm)(z, kvm, g_mq, g_mk)


def _mem_bwd(z, kvm, g_mq, g_mk, dom, dz, batch, seq, name):
    n = z.shape[0]
    t = min(ROW_TILE, seq)
    per = seq // t

    def body(qm, kvm_ref, gq, gk, dom_ref, _, dz_ref, dkvm_ref, dgq_ref, dgk_ref):
        i = pl.program_id(0)
        _, vjp = jax.vjp(_mem_core, *_mem_load(qm, kvm_ref, gq, gk))
        dqs, dks, dvs, dgq, dgk = vjp(dom_ref[...])
        dz_ref[...] = jnp.concatenate(dqs, axis=1).astype(BF)
        _acc(dkvm_ref, jnp.concatenate(dks + dvs, axis=1), i % per == 0)
        _acc(dgq_ref, dgq, i == 0)
        _acc(dgk_ref, dgk, i == 0)

    kv_spec = pl.BlockSpec((MEM_LEN, 2 * MEM_W), lambda i: (i // per, 0))
    return pl.pallas_call(
        body, grid=(n // t,),
        in_specs=[_rows(t, MEM_W, QM // MEM_W), kv_spec, _full((1, LANES)), _full((1, LANES)), _rows(t, MEM_W), ANY],
        out_specs=[_rows(t, MEM_W, QM // MEM_W), kv_spec, _full((1, LANES)), _full((1, LANES))],
        out_shape=[SDS((n, Z_COLS), BF), SDS((batch * MEM_LEN, 2 * MEM_W), F32), SDS((1, LANES), F32),
                   SDS((1, LANES), F32)],
        input_output_aliases={5: 0},
        compiler_params=_params(("arbitrary",)), name=name)(z, kvm, g_mq, g_mk, dom, dz)


def _me():
    return lax.axis_index("x"), lax.axis_index("y"), lax.axis_index("c")


def _other_chips(x, y):
    return [(1 - x, y), (x, 1 - y), (1 - x, 1 - y)]


def _shard_shape(name):
    r, c = BIG_SHAPE[name]
    return (r, c // N_CHIPS) if name in COL_SHARDED else (r // N_CHIPS, c)


def _n_pieces(half_rows):
    return max(1, half_rows // PIECE_ROWS)


def _piece_plan(shapes):
    plan = []
    for r, _ in shapes:
        h = r // 2
        n = _n_pieces(h)
        plan.append((h, n, h // n))
    return plan


def _remote(send, recv, sem, src, dst, to):
    return pltpu.make_async_remote_copy(src_ref=src, dst_ref=dst, send_sem=send.at[sem], recv_sem=recv.at[sem],
                                        device_id=to, device_id_type=MESH)


def _gather_far(shards):
    plan = _piece_plan([s.shape for s in shards])
    n_far = 3 * sum(n for _, n, _ in plan)
    n_loc = 2 * sum(n for _, n, _ in plan)

    def copies(s_refs, o_refs, send, recv, local):
        x, y, c = _me()
        k = 2 * x + y
        mine, sends, arrivals = [], [], []
        for t, (h, n, pr) in enumerate(plan):
            s_ref, o_ref = s_refs[t], o_refs[t]
            for core in range(2):
                for p in range(n):
                    rows = pl.ds(core * h + p * pr, pr)
                    mine.append(pltpu.make_async_copy(s_ref.at[rows], o_ref.at[k, rows], local.at[len(mine)]))
            for chip in _other_chips(x, y):
                for p in range(n):
                    rows = pl.ds(c * h + p * pr, pr)
                    s = len(sends)
                    sends.append(_remote(send, recv, s, s_ref.at[rows], o_ref.at[k, rows], (*chip, c)))
                    arrivals.append(_remote(send, recv, s, s_ref.at[rows], o_ref.at[2 * chip[0] + chip[1], rows],
                                            (*chip, c)))
        return sends, arrivals, mine

    return _Phase(shards, [SDS((N_CHIPS,) + s.shape, s.dtype) for s in shards], n_far, n_loc, copies)


def _gather_near(bufs):
    plan = _piece_plan([b.shape[1:] for b in bufs])
    n_sem = 3 * sum(n for _, n, _ in plan)

    def copies(i_refs, o_refs, send, recv, local):
        x, y, c = _me()
        sib = (x, y, 1 - c)
        sends, arrivals = [], []
        for t, (h, n, pr) in enumerate(plan):
            for chip in _other_chips(x, y):
                ci = 2 * chip[0] + chip[1]
                for p in range(n):
                    rows = pl.ds(c * h + p * pr, pr)
                    rows_sib = pl.ds((1 - c) * h + p * pr, pr)
                    s = len(sends)
                    sends.append(_remote(send, recv, s, i_refs[t].at[ci, rows], o_refs[t].at[ci, rows], sib))
                    arrivals.append(_remote(send, recv, s, i_refs[t].at[ci, rows_sib], o_refs[t].at[ci, rows_sib], sib))
        return sends, arrivals, []

    return _Phase(bufs, [SDS(b.shape, b.dtype) for b in bufs], n_sem, 0, copies, {t: t for t in range(len(bufs))})


def _pair_exchange(grads):
    plan = _piece_plan([g.shape[1:] for g in grads])
    n_sem = sum(n for _, n, _ in plan)

    def copies(g_refs, o_refs, send, recv, local):
        x, y, c = _me()
        sends = []
        for t, (h, n, pr) in enumerate(plan):
            for p in range(n):
                sends.append(_remote(send, recv, len(sends), g_refs[t].at[:, pl.ds((1 - c) * h + p * pr, pr)],
                                     o_refs[t].at[:, pl.ds(p * pr, pr)], (x, y, 1 - c)))
        return sends, sends, []

    return _Phase(grads, [SDS((N_CHIPS, g.shape[1] // 2, g.shape[2]), F32) for g in grads], n_sem, 0, copies)


def _pair_add(ck, g, theirs, name):
    _, r, c = g.shape
    (h, n, pr), = _piece_plan([(r, c)])

    def body(ck_ref, g_ref, t_ref, p32_ref, pbf_ref):
        s = g_ref[...] + t_ref[...]
        p32_ref[...] = s
        pbf_ref[...] = s.astype(BF)

    half = pl.BlockSpec((None, pr, c), lambda k, p, ck: (k, p, 0))
    spec = pltpu.PrefetchScalarGridSpec(
        num_scalar_prefetch=1, grid=(N_CHIPS, n),
        in_specs=[pl.BlockSpec((None, pr, c), lambda k, p, ck: (k, ck[0] * n + p, 0)), half], out_specs=[half, half])
    return pl.pallas_call(body, grid_spec=spec, out_shape=[SDS((N_CHIPS, h, c), F32), SDS((N_CHIPS, h, c), BF)],
                          compiler_params=_params(("arbitrary", "arbitrary")), name=name)(ck, g, theirs)


def _scatter_partials(pbfs):
    plan = [(h, _n_pieces(h), h // _n_pieces(h)) for h in [p.shape[1] for p in pbfs]]
    n_sem = 3 * sum(n for _, n, _ in plan)

    def copies(p_refs, o_refs, send, recv, local):
        x, y, c = _me()
        sends = []
        for t, (h, n, pr) in enumerate(plan):
            for j, chip in enumerate(_other_chips(x, y)):
                for p in range(n):
                    rows = pl.ds(p * pr, pr)
                    sends.append(_remote(send, recv, len(sends), p_refs[t].at[2 * chip[0] + chip[1], rows],
                                         o_refs[t].at[j, rows], (*chip, c)))
        return sends, sends, []

    return _Phase(pbfs, [SDS((3,) + p.shape[1:], BF) for p in pbfs], n_sem, 0, copies)


def _sum_chips(ck, p32, slots, name):
    _, h, c = p32.shape
    n = _n_pieces(h)
    pr = h // n

    def body(ck_ref, p_ref, s_ref, o_ref):
        o_ref[...] = ((p_ref[...] + s_ref[0].astype(F32)) + s_ref[1].astype(F32)) + s_ref[2].astype(F32)

    spec = pltpu.PrefetchScalarGridSpec(
        num_scalar_prefetch=1, grid=(n,),
        in_specs=[pl.BlockSpec((None, pr, c), lambda p, ck: (ck[1], p, 0)),
                  pl.BlockSpec((3, pr, c), lambda p, ck: (0, p, 0))],
        out_specs=pl.BlockSpec((pr, c), lambda p, ck: (ck[0] * n + p, 0)))
    return pl.pallas_call(body, grid_spec=spec, out_shape=SDS((2 * h, c), F32),
                          compiler_params=_params(("arbitrary",)), name=name)(ck, p32, slots)


def _join_halves(sums):
    plan = _piece_plan([s.shape for s in sums])
    n_sem = sum(n for _, n, _ in plan)

    def copies(r_refs, o_refs, send, recv, local):
        x, y, c = _me()
        sends, arrivals = [], []
        for t, (h, n, pr) in enumerate(plan):
            for p in range(n):
                rows = pl.ds(c * h + p * pr, pr)
                rows_sib = pl.ds((1 - c) * h + p * pr, pr)
                s = len(sends)
                sends.append(_remote(send, recv, s, r_refs[t].at[rows], o_refs[t].at[rows], (x, y, 1 - c)))
                arrivals.append(_remote(send, recv, s, r_refs[t].at[rows_sib], o_refs[t].at[rows_sib], (x, y, 1 - c)))
        return sends, arrivals, []

    return _Phase(sums, [SDS(s.shape, F32) for s in sums], n_sem, 0, copies, {t: t for t in range(len(sums))})


def _gather_small(s, name):
    def body(s_ref, o_ref, send, recv, local):
        x, y, c = _me()
        me = 4 * x + 2 * y + c
        keep = pltpu.make_async_copy(s_ref, o_ref.at[me], local)
        keep.start()
        sends = []
        for r in range(1, 8):
            fx, fy, fc = (r >> 2) & 1, (r >> 1) & 1, r & 1
            to = (x ^ fx, y ^ fy, c ^ fc)
            sends.append(pltpu.make_async_remote_copy(
                src_ref=s_ref, dst_ref=o_ref.at[me], send_sem=send.at[r - 1], recv_sem=recv.at[r - 1],
                device_id=to, device_id_type=MESH))
        for cp in sends:
            cp.start()
        for r in range(1, 8):
            fx, fy, fc = (r >> 2) & 1, (r >> 1) & 1, r & 1
            src = 4 * (x ^ fx) + 2 * (y ^ fy) + (c ^ fc)
            pltpu.make_async_remote_copy(
                src_ref=s_ref, dst_ref=o_ref.at[src], send_sem=send.at[r - 1], recv_sem=recv.at[r - 1],
                device_id=(x ^ fx, y ^ fy, c ^ fc), device_id_type=MESH).wait_recv()
        for cp in sends:
            cp.wait_send()
        keep.wait()

    return pl.pallas_call(
        body, in_specs=[ANY], out_specs=ANY, out_shape=SDS((8, SMALL_ROWS, LANES), F32),
        scratch_shapes=[pltpu.SemaphoreType.DMA((7,)), pltpu.SemaphoreType.DMA((7,)), pltpu.SemaphoreType.DMA],
        name=name)(s)


def _adam_math(w, g, m, v):
    nm = ADAM_B1 * m + (1.0 - ADAM_B1) * g
    nv = ADAM_B2 * v + (1.0 - ADAM_B2) * (g * g)
    m_hat = nm / (1.0 - ADAM_B1 ** ADAM_STEP)
    v_hat = nv / (1.0 - ADAM_B2 ** ADAM_STEP)
    return -ADAM_LR * (m_hat / (jnp.sqrt(v_hat) + ADAM_EPS) + ADAM_WD * w), nm, nv


def _adamw(w, g, m, v, name):
    _, r, c = w.shape
    t = _pick(r, (256, 128, 64))

    def body(w_ref, g_ref, m_ref, v_ref, go_ref, d_ref, nm_ref, nv_ref):
        g_ = g_ref[...]
        d, nm, nv = _adam_math(w_ref[...], g_, m_ref[...], v_ref[...])
        go_ref[...] = g_
        d_ref[...] = d
        nm_ref[...] = nm
        nv_ref[...] = nv

    lead = pl.BlockSpec((None, t, c), lambda i: (0, i, 0))
    return pl.pallas_call(body, grid=(r // t,), in_specs=[lead, _rows(t, c), lead, lead], out_specs=[lead] * 4,
                          out_shape=[SDS((1, r, c), F32)] * 4, compiler_params=_params(("parallel",)),
                          name=name)(w, g, m, v)


def _small_layout():
    out, r0 = {}, 0
    for n in SMALL:
        size = int(np.prod(SMALL_SHAPE[n]))
        nr = -(-size // LANES)
        out[n] = (r0, nr)
        r0 += nr
    assert r0 <= SMALL_ROWS
    return out, r0


def _pack_small(grads, name):
    layout, used = _small_layout()

    def body(*refs):
        o_ref = refs[-1]
        for n, ref in zip(SMALL, refs[:-1]):
            r0, nr = layout[n]
            if n == "w_spatial":
                for g in range(GM_GROUPS):
                    o_ref[r0 + g * GM_CHUNK:r0 + (g + 1) * GM_CHUNK, :] = ref[g]
            elif n == "b_spatial":
                o_ref[r0:r0 + nr, :] = ref[...]
            else:
                for i in range(nr):
                    o_ref[r0 + i:r0 + i + 1, :] = ref[:, i * LANES:(i + 1) * LANES]
        o_ref[used:SMALL_ROWS, :] = jnp.zeros((SMALL_ROWS - used, LANES), F32)

    return pl.pallas_call(body, out_shape=SDS((SMALL_ROWS, LANES), F32), name=name)(*grads)


def _adamw_small(gathered, ws, ms, vs, name):
    layout, _ = _small_layout()
    n_t = len(SMALL)

    def body(*refs):
        g_ref = refs[0]
        w_refs, m_refs, v_refs = refs[1:1 + n_t], refs[1 + n_t:1 + 2 * n_t], refs[1 + 2 * n_t:1 + 3 * n_t]
        outs = refs[1 + 3 * n_t:1 + 7 * n_t]
        acc = refs[-1]
        total = g_ref[0]
        for j in range(1, 8):
            total = total + g_ref[j]
        acc[...] = total
        for t, n in enumerate(SMALL):
            r0, nr = layout[n]
            o_refs = [outs[t], outs[n_t + t], outs[2 * n_t + t], outs[3 * n_t + t]]
            if n == "w_spatial":
                views = [((0, g), slice(r0 + g * GM_CHUNK, r0 + (g + 1) * GM_CHUNK), slice(None))
                         for g in range(GM_GROUPS)]
            elif n == "b_spatial":
                views = [((0,), slice(r0, r0 + nr), slice(None))]
            else:
                width = SMALL_SHAPE[n][1]
                views = [((slice(None), slice(i * LANES, min((i + 1) * LANES, width))), slice(r0 + i, r0 + i + 1),
                          slice(0, min(LANES, width - i * LANES))) for i in range(nr)]
            for idx, rows, lanes in views:
                g = acc[rows, lanes]
                d, nm, nv = _adam_math(w_refs[t][idx], g, m_refs[t][idx], v_refs[t][idx])
                for ref, val in zip(o_refs, (g, d, nm, nv)):
                    ref[idx] = val

    shapes = [SDS(SMALL_SHAPE[n], F32) for n in SMALL]
    return pl.pallas_call(body, out_shape=shapes * 4, scratch_shapes=[pltpu.VMEM((SMALL_ROWS, LANES), F32)],
                          name=name)(gathered, *ws, *ms, *vs)


def _win_layout(w_in):
    pad = jnp.zeros((w_in.shape[0], LANES - MLA_ROPE), w_in.dtype)
    u, v, cq = w_in[:, 0:512], w_in[:, 512:1024], w_in[:, 1024:1408]
    ckv, kpe, qm, zg = w_in[:, 1408:1664], w_in[:, 1664:1728], w_in[:, 1728:2240], w_in[:, 2240:5312]
    return jnp.concatenate([zg, u, v, qm, cq, kpe, pad, ckv], axis=1)


def _win_unlayout(g):
    zg, u, v, qm = g[:, ZG:ZG + 3072], g[:, ZU:ZU + 512], g[:, ZV:ZV + 512], g[:, QM:QM + 512]
    cq, kpe, ckv = g[:, CQ:CQ + 384], g[:, KPE:KPE + MLA_ROPE], g[:, CKV:CKV + 256]
    return jnp.concatenate([u, v, cq, ckv, kpe, qm, zg], axis=1)


def _wq_layout(w_uq):
    w = w_uq.reshape(Q_LORA, MLA_HEADS, MLA_NOPE + MLA_ROPE)
    nope = w[:, :, :MLA_NOPE].reshape(Q_LORA, MLA_HEADS * MLA_NOPE)
    pe = jnp.pad(w[:, :, MLA_NOPE:], ((0, 0), (0, 0), (0, LANES - MLA_ROPE))).reshape(Q_LORA, MLA_HEADS * LANES)
    return jnp.concatenate([nope, pe], axis=1)


def _wq_unlayout(g):
    nope = g[:, :1024].reshape(Q_LORA, MLA_HEADS, MLA_NOPE)
    pe = g[:, 1024:].reshape(Q_LORA, MLA_HEADS, LANES)[:, :, :MLA_ROPE]
    return jnp.concatenate([nope, pe], axis=2).reshape(Q_LORA, MLA_HEADS * (MLA_NOPE + MLA_ROPE))


def _wkv_layout(w_ukv):
    w = w_ukv.reshape(KV_LORA, MLA_HEADS, MLA_NOPE + MLA_V)
    return jnp.concatenate([w[:, :, :MLA_NOPE].reshape(KV_LORA, 1024), w[:, :, MLA_NOPE:].reshape(KV_LORA, 1024)],
                           axis=1)


def _wkv_unlayout(g):
    kn = g[:, :1024].reshape(KV_LORA, MLA_HEADS, MLA_NOPE)
    v = g[:, 1024:].reshape(KV_LORA, MLA_HEADS, MLA_V)
    return jnp.concatenate([kn, v], axis=2).reshape(KV_LORA, MLA_HEADS * (MLA_NOPE + MLA_V))


def _owner_major(g, name):
    r, c = _shard_shape(name)
    return g.reshape(r, N_CHIPS, c).transpose(1, 0, 2) if name in COL_SHARDED else g.reshape(N_CHIPS, r, c)


def _pad_lanes(g):
    return jnp.pad(g, ((0, 0), (0, LANES - g.shape[1])))


def kernel(x, mem, positions, g_mix, w_in, g_cq, w_uq, g_ckv, w_ukv, g_q_nope, g_q_pe, g_k_nope, g_k_pe, g_gm_ln, b_gm_ln, w_spatial, b_spatial, g_mem, w_mem_kv, g_mq, g_mk, w_o_gm, w_o_mla, w_o_mem, w_out, g_ffn, w_ff1, w_ff2, loss_target, m_g_mix, m_w_in, m_g_cq, m_w_uq, m_g_ckv, m_w_ukv, m_g_q_nope, m_g_q_pe, m_g_k_nope, m_g_k_pe, m_g_gm_ln, m_b_gm_ln, m_w_spatial, m_b_spatial, m_g_mem, m_w_mem_kv, m_g_mq, m_g_mk, m_w_o_gm, m_w_o_mla, m_w_o_mem, m_w_out, m_g_ffn, m_w_ff1, m_w_ff2, v_g_mix, v_w_in, v_g_cq, v_w_uq, v_g_ckv, v_w_ukv, v_g_q_nope, v_g_q_pe, v_g_k_nope, v_g_k_pe, v_g_gm_ln, v_b_gm_ln, v_w_spatial, v_b_spatial, v_g_mem, v_w_mem_kv, v_g_mq, v_g_mk, v_w_o_gm, v_w_o_mla, v_w_o_mem, v_w_out, v_g_ffn, v_w_ff1, v_w_ff2):
    given = dict(locals())
    wts = {n: given[n] for n in WEIGHTS}
    mom = {n: given["m_" + n] for n in WEIGHTS}
    var = {n: given["v_" + n] for n in WEIGHTS}
    batch, seq, _ = x.shape
    n_tok = batch * seq

    def natural(n, g):
        r, c = _shard_shape(n)
        return g.transpose(1, 0, 2).reshape(r, N_CHIPS * c) if n in COL_SHARDED else g.reshape(N_CHIPS * r, c)

    def far(names):
        return _gather_far([wts[n][0].astype(BF) for n in names])

    x2 = x.reshape(n_tok, D_MODEL)
    tgt2 = loss_target.reshape(n_tok, D_MODEL)
    mem2 = mem.reshape(batch * MEM_LEN, D_MODEL)
    pos_f = positions.reshape(n_tok, 1).astype(F32)

    inv = ROPE_BASE ** (-jnp.arange(0, MLA_ROPE, 2, dtype=F32) / MLA_ROPE)
    zeros64 = jnp.zeros((LANES - MLA_ROPE,), F32)
    inv_full = jnp.concatenate([inv, inv, zeros64]).reshape(1, LANES)
    half = MLA_ROPE // 2
    cmask = jnp.concatenate([jnp.ones((MLA_ROPE,), F32), zeros64]).reshape(1, LANES)
    smask = jnp.concatenate([-jnp.ones((half,), F32), jnp.ones((half,), F32), zeros64]).reshape(1, LANES)

    prep_gains = [g_cq, g_ckv, g_q_nope, _pad_lanes(g_q_pe), g_k_nope, _pad_lanes(g_k_pe)]
    ws = w_spatial[0]
    bcols = [b_spatial[0, g].reshape(GM_CHUNK, 1) for g in range(GM_GROUPS)]

    h1, early_far = _rms_fwd(x2, g_mix, "rms_mix", comm=far(EARLY))
    (cos_f, sin_s), early = _rope_tables(pos_f, inv_full, cmask, smask, "rope_tables", comm=_gather_near(early_far))
    full = {n: natural(n, g) for n, g in zip(EARLY, early)}
    win = _win_layout(full["w_in"])
    wq = _wq_layout(full["w_uq"])
    wkv = _wkv_layout(full["w_ukv"])
    z, proj_far = _mm(h1, win, out_dtypes=(BF,), name="mm_in", comm=far(LATE_PROJ))
    gm = _gm_fwd(z, g_gm_ln, b_gm_ln, ws, bcols, "gm_fwd")
    qc, kc, vc = _prep_fwd(z, cos_f, sin_s, prep_gains, wq, wkv, "prep_fwd")
    (o_mla, lse), ff_far = _mla_fwd(qc, kc, vc, batch, seq, "mla_fwd", comm=far(LATE_FF))
    memn = _rms_fwd(mem2, g_mem, "rms_mem")
    kvm, proj = _mm(memn, full["w_mem_kv"], name="mm_memkv", comm=_gather_near(proj_far))
    o_mem, ff = _mem_fwd(z, kvm, g_mq, g_mk, batch, seq, "mem_fwd", comm=_gather_near(ff_far))
    full.update({n: natural(n, g) for n, g in zip(LATE_PROJ + LATE_FF, list(proj) + list(ff))})
    y_gm = _mm(gm, full["w_o_gm"], out_dtypes=(BF,), name="mm_o_gm")
    y_mla = _mm(o_mla, full["w_o_mla"], out_dtypes=(BF,), name="mm_o_mla")
    y_mem = _mm(o_mem, full["w_o_mem"], out_dtypes=(BF,), name="mm_o_mem")
    merged = _merge_fwd(z, y_gm, y_mla, y_mem, "merge_fwd")
    x1 = _mm(merged, full["w_out"], ins=(x2,), epilogue=_add_to, name="mm_out")
    h2 = _rms_fwd(x1, g_ffn, "rms_ffn")
    a_ff, r_ff = _mm(h2, full["w_ff1"], epilogue=_relu2, out_dtypes=(BF, BF), name="mm_ff1")
    dy, dyb, loss_tile = _mm(r_ff, full["w_ff2"], ins=(x1, tgt2), epilogue=_loss_tail, out_dtypes=(F32, BF),
                             total=True, name="mm_ff2")

    gw = {}
    da = _mm(dyb, full["w_ff2"], tb=True, ins=(a_ff,), epilogue=_relu2_bwd, out_dtypes=(BF,), name="mm_d_a")
    gw["w_ff2"] = _owner_major(_mm(r_ff, dyb, ta=True, name="mm_dw_ff2"), "w_ff2")
    gw["w_ff1"] = _mm(h2, da, ta=True, owner_cols=D_FF // N_CHIPS, name="mm_dw_ff1")
    dh2 = _mm(da, full["w_ff1"], tb=True, name="mm_d_h2")
    dx1, dx1b, dg_ffn = _rms_bwd(x1, g_ffn, dh2, dy, "rms_ffn_bwd")
    dmerged = _mm(dx1b, full["w_out"], tb=True, name="mm_d_merged")
    gw["w_out"] = _owner_major(_mm(merged, dx1b, ta=True, name="mm_dw_out"), "w_out")
    dz, dy_gm, dy_mla, dy_mem = _merge_bwd(z, y_gm, y_mla, y_mem, dmerged, "merge_bwd")
    dgm = _mm(dy_gm, full["w_o_gm"], tb=True, name="mm_d_gm")
    gw["w_o_gm"] = _mm(gm, dy_gm, ta=True, owner_cols=D_MODEL // N_CHIPS, name="mm_dw_o_gm")
    do_mla = _mm(dy_mla, full["w_o_mla"], tb=True, name="mm_d_omla")
    gw["w_o_mla"] = _owner_major(_mm(o_mla, dy_mla, ta=True, name="mm_dw_o_mla"), "w_o_mla")
    do_mem = _mm(dy_mem, full["w_o_mem"], tb=True, name="mm_d_omem")
    gw["w_o_mem"] = _mm(o_mem, dy_mem, ta=True, owner_cols=D_MODEL // N_CHIPS, name="mm_dw_o_mem")
    ck = jnp.stack([lax.axis_index("c"), 2 * lax.axis_index("x") + lax.axis_index("y")]).astype(jnp.int32)

    def pair_sums(names, theirs):
        return [_pair_add(ck, gw[n], t, "pair_add_" + n) for n, t in zip(names, theirs)]

    def chip_sums(names, pairs, slots):
        return [_sum_chips(ck, p[0], s, "sum_chips_" + n) for n, p, s in zip(names, pairs, slots)]

    (dz, dg_ln, db_ln, dws, *dbcols), theirs = _gm_bwd(z, g_gm_ln, b_gm_ln, ws, bcols, dgm, dz, "gm_bwd",
                                                      comm=_pair_exchange([gw[n] for n in LATE]))
    pairs = pair_sums(LATE, theirs)
    (dq, dk, dv), slots = _mla_bwd(qc, kc, vc, o_mla, lse, do_mla, batch, seq, "mla_bwd",
                                   comm=_scatter_partials([p[1] for p in pairs]))
    sums = chip_sums(LATE, pairs, slots)
    (dz, dg_cq, dg_ckv, dg_qn, dg_qp, dg_kn, dg_kp, dwq, dwkv), reduced_late = _prep_bwd(
        z, cos_f, sin_s, prep_gains, wq, wkv, dq, dk, dv, dz, "prep_bwd", comm=_join_halves(sums))
    dz, dkvm, dg_mq, dg_mk = _mem_bwd(z, kvm, g_mq, g_mk, do_mem, dz, batch, seq, "mem_bwd")
    dmemn = _mm(dkvm, full["w_mem_kv"], tb=True, name="mm_d_memn")
    gw["w_mem_kv"] = _owner_major(_mm(memn, dkvm, ta=True, name="mm_dw_memkv"), "w_mem_kv")
    _, _, dg_mem = _rms_bwd(mem2, g_mem, dmemn, None, "rms_mem_bwd")
    gw["w_in"] = _owner_major(_win_unlayout(_mm(h1, dz, ta=True, name="mm_dw_in")), "w_in")
    gw["w_uq"] = _owner_major(_wq_unlayout(dwq), "w_uq")
    gw["w_ukv"] = _owner_major(_wkv_unlayout(dwkv), "w_ukv")
    dh1, theirs = _mm(dz, win, tb=True, name="mm_d_h1_top", rows=(0, 2), comm=_pair_exchange([gw[n] for n in EARLY]))
    pairs = pair_sums(EARLY, theirs)
    dh1, slots = _mm(dz, win, tb=True, name="mm_d_h1_bottom", rows=(1, 2), into=dh1,
                     comm=_scatter_partials([p[1] for p in pairs]))
    (grad_x, _, dg_mix), reduced_early = _rms_bwd(x2, g_mix, dh1, dx1, "rms_mix_bwd",
                                                  comm=_join_halves(chip_sums(EARLY, pairs, slots)))
    results = {n: _adamw(wts[n], g, mom[n], var[n], "adamw_" + n)
               for n, g in zip(LATE + EARLY, list(reduced_late) + list(reduced_early))}

    small_g = {"g_mix": dg_mix, "g_cq": dg_cq, "g_ckv": dg_ckv, "g_q_nope": dg_qn, "g_q_pe": dg_qp,
               "g_k_nope": dg_kn, "g_k_pe": dg_kp, "g_gm_ln": dg_ln, "b_gm_ln": db_ln, "w_spatial": dws,
               "b_spatial": jnp.concatenate(dbcols, axis=1).T, "g_mem": dg_mem, "g_mq": dg_mq, "g_mk": dg_mk,
               "g_ffn": dg_ffn}
    packed = _pack_small([small_g[n] for n in SMALL], "pack_small")
    small_out = _adamw_small(_gather_small(packed, "gather_small"), [wts[n] for n in SMALL],
                             [mom[n] for n in SMALL], [var[n] for n in SMALL], "adamw_small")
    for t, n in enumerate(SMALL):
        results[n] = [small_out[j * len(SMALL) + t] for j in range(4)]

    loss = lax.psum(loss_tile[0, 0], ("x", "y", "c"))
    grad_x = grad_x.reshape(batch, seq, D_MODEL)
    return (loss, grad_x, *[results[n][0] for n in WEIGHTS], *[results[n][1] for n in WEIGHTS],
            *[results[n][2] for n in WEIGHTS], *[results[n][3] for n in WEIGHTS])
```

```python
import functools
import math

import numpy as np
import jax
import jax.numpy as jnp
from jax import lax
from jax.experimental import pallas as pl
from jax.experimental.pallas import tpu as pltpu

F32 = jnp.float32
BF = jnp.bfloat16
SDS = jax.ShapeDtypeStruct
MESH = pl.DeviceIdType.MESH

D_MODEL = 1024
MEM_LEN = 256
MEM_HEADS = 4
HEAD_DIM = 128
GM_WIDTH = 512
GM_CHUNK = 128
GM_GROUPS = 4
MLA_HEADS = 8
MLA_NOPE = 128
MLA_ROPE = 64
MLA_V = 128
Q_LORA = 384
KV_LORA = 256
ROPE_BASE = 10000.0
D_FF = 4096
EPS = 1e-6
W_IN_COLS = 5312
ADAM_LR, ADAM_B1, ADAM_B2, ADAM_EPS, ADAM_WD, ADAM_STEP = 0.001, 0.9, 0.999, 1e-08, 0.01, 10

ZG, ZU, ZV, QM, CQ, KPE, CKV = 0, 3072, 3584, 4096, 4608, 4992, 5120
Z_COLS = 5376
LANES = 128
ROW_TILE = 256
ATT_TILE = 512
ATT_HEADS = 2
VMEM_LIMIT = 56 * 1024 * 1024

N_CHIPS = 4
PIECE_ROWS = 256
SMALL_ROWS = 560

BIG = ["w_in", "w_uq", "w_ukv", "w_mem_kv", "w_o_gm", "w_o_mla", "w_o_mem", "w_out", "w_ff1", "w_ff2"]
BIG_SHAPE = {"w_in": (1024, 5312), "w_uq": (384, 1536), "w_ukv": (256, 2048), "w_mem_kv": (1024, 1024),
             "w_o_gm": (512, 1024), "w_o_mla": (1024, 1024), "w_o_mem": (512, 1024), "w_out": (1024, 1024),
             "w_ff1": (1024, 4096), "w_ff2": (4096, 1024)}
COL_SHARDED = {"w_in", "w_uq", "w_ukv", "w_o_gm", "w_o_mem", "w_ff1"}
EARLY = ["w_in", "w_uq", "w_ukv", "w_mem_kv"]
LATE_PROJ = ["w_o_gm", "w_o_mla", "w_o_mem", "w_out"]
LATE_FF = ["w_ff1", "w_ff2"]
LATE = LATE_PROJ + LATE_FF
SMALL = ["w_spatial", "b_spatial", "g_mix", "g_cq", "g_ckv", "g_q_nope", "g_q_pe", "g_k_nope", "g_k_pe", "g_gm_ln",
         "b_gm_ln", "g_mem", "g_mq", "g_mk", "g_ffn"]
SMALL_SHAPE = {"g_mix": (1, 1024), "g_cq": (1, 384), "g_ckv": (1, 256), "g_q_nope": (1, 128), "g_q_pe": (1, 64),
               "g_k_nope": (1, 128), "g_k_pe": (1, 64), "g_gm_ln": (1, 512), "b_gm_ln": (1, 512),
               "w_spatial": (1, 4, 128, 128), "b_spatial": (1, 4, 128), "g_mem": (1, 1024), "g_mq": (1, 128),
               "g_mk": (1, 128), "g_ffn": (1, 1024)}
WEIGHTS = ['g_mix', 'w_in', 'g_cq', 'w_uq', 'g_ckv', 'w_ukv', 'g_q_nope', 'g_q_pe', 'g_k_nope', 'g_k_pe',
           'g_gm_ln', 'b_gm_ln', 'w_spatial', 'b_spatial', 'g_mem', 'w_mem_kv', 'g_mq', 'g_mk', 'w_o_gm',
           'w_o_mla', 'w_o_mem', 'w_out', 'g_ffn', 'w_ff1', 'w_ff2']


def _params(sem=None):
    return pltpu.CompilerParams(vmem_limit_bytes=VMEM_LIMIT, dimension_semantics=sem)


def _pick(n, prefs):
    for p in prefs:
        if n % p == 0:
            return p
    return n


def _full(shape):
    nd = len(shape)
    return pl.BlockSpec(shape, lambda *_: (0,) * nd)


def _rows(t, w, blk=0):
    return pl.BlockSpec((t, w), lambda i: (i, blk))


def _acc(ref, val, first):
    @pl.when(first)
    def _():
        ref[...] = val

    @pl.when(jnp.logical_not(first))
    def _():
        ref[...] += val


ANY = pl.BlockSpec(memory_space=pl.ANY)


class _Phase:
    def __init__(self, operands, out_shapes, n_sem, n_local, copies, aliases=None):
        self.operands, self.out_shapes, self.aliases = list(operands), list(out_shapes), dict(aliases or {})
        self.n_sem, self.n_local, self.copies = n_sem, max(n_local, 1), copies

    def sem_shapes(self):
        return [pltpu.SemaphoreType.DMA((self.n_sem,)), pltpu.SemaphoreType.DMA((self.n_sem,)),
                pltpu.SemaphoreType.DMA((self.n_local,))]

    def start(self, ins, outs, send, recv, local):
        sends, _, locals_ = self.copies(ins, outs, send, recv, local)
        for cp in locals_ + sends:
            cp.start()

    def finish(self, ins, outs, send, recv, local):
        sends, arrivals, locals_ = self.copies(ins, outs, send, recv, local)
        for cp in arrivals:
            cp.wait_recv()
        for cp in sends:
            cp.wait_send()
        for cp in locals_:
            cp.wait()


def _run_phase(phase, name):
    n_in = len(phase.operands)

    def body(*refs):
        ins, outs, sems = refs[:n_in], refs[n_in:n_in + len(phase.out_shapes)], refs[n_in + len(phase.out_shapes):]
        phase.start(ins, outs, *sems)
        phase.finish(ins, outs, *sems)

    return pl.pallas_call(body, in_specs=[ANY] * n_in, out_specs=[ANY] * len(phase.out_shapes),
                          out_shape=phase.out_shapes, scratch_shapes=phase.sem_shapes(),
                          input_output_aliases=phase.aliases, name=name)(*phase.operands)


def _pcall(body, *, grid, in_specs, out_specs, out_shape, scratch_shapes=(), sem=None, name, comm=None, aliases=None):
    single = not isinstance(out_shape, (list, tuple))
    o_specs = [out_specs] if single else list(out_specs)
    o_shape = [out_shape] if single else list(out_shape)
    aliases = dict(aliases or {})
    if comm is None:
        call = pl.pallas_call(body, grid=grid, in_specs=list(in_specs), out_specs=o_specs, out_shape=o_shape,
                              scratch_shapes=list(scratch_shapes), input_output_aliases=aliases,
                              compiler_params=_params(sem), name=name)

        def run_plain(*args):
            res = call(*args)
            return res[0] if single else res

        return run_plain

    n_in, n_out, n_scr = len(in_specs), len(o_specs), len(scratch_shapes)
    nc_in, nc_out = len(comm.operands), len(comm.out_shapes)

    def wrapped(*refs):
        ins, cins = refs[:n_in], refs[n_in:n_in + nc_in]
        o0 = n_in + nc_in
        outs, couts = refs[o0:o0 + n_out], refs[o0 + n_out:o0 + n_out + nc_out]
        s0 = o0 + n_out + nc_out
        scr, csem = refs[s0:s0 + n_scr], refs[s0 + n_scr:]
        ids = [pl.program_id(d) for d in range(len(grid))]
        first = functools.reduce(jnp.logical_and, [i == 0 for i in ids])
        last = functools.reduce(jnp.logical_and, [i == g - 1 for i, g in zip(ids, grid)])

        @pl.when(first)
        def _():
            comm.start(cins, couts, *csem)

        body(*ins, *outs, *scr)

        @pl.when(last)
        def _():
            comm.finish(cins, couts, *csem)

    call = pl.pallas_call(
        wrapped, grid=grid, in_specs=list(in_specs) + [ANY] * nc_in, out_specs=o_specs + [ANY] * nc_out,
        out_shape=o_shape + comm.out_shapes, scratch_shapes=list(scratch_shapes) + comm.sem_shapes(),
        input_output_aliases={**aliases, **{n_in + i: n_out + j for i, j in comm.aliases.items()}},
        compiler_params=_params(("arbitrary",) * len(grid)), name=name)

    def run_carrying(*args):
        res = call(*args, *comm.operands)
        return (res[0] if single else res[:n_out]), res[n_out:]

    return run_carrying


def _dn(a, b, ca, cb):
    return lax.dot_general(a.astype(BF), b.astype(BF), (((ca,), (cb,)), ((), ())), preferred_element_type=F32)


@jax.custom_vjp
def _mm_nn(a, b):
    return _dn(a, b, 1, 0)


def _mm_nn_fwd(a, b):
    return _dn(a, b, 1, 0), (a.astype(BF), b.astype(BF))


def _mm_nn_bwd(res, ct):
    a, b = res
    return _dn(ct, b, 1, 1), _dn(a, ct, 0, 0)


_mm_nn.defvjp(_mm_nn_fwd, _mm_nn_bwd)


@jax.custom_vjp
def _mm_nt(a, b):
    return _dn(a, b, 1, 1)


def _mm_nt_fwd(a, b):
    return _dn(a, b, 1, 1), (a.astype(BF), b.astype(BF))


def _mm_nt_bwd(res, ct):
    a, b = res
    return _dn(ct, b, 1, 0), _dn(ct, a, 0, 0)


_mm_nt.defvjp(_mm_nt_fwd, _mm_nt_bwd)


def _rmsn(x, g, n):
    ms = jnp.sum(x * x, axis=-1, keepdims=True) * (1.0 / n)
    return x * lax.rsqrt(ms + EPS) * g


def _layernorm(x, g, b):
    mu = jnp.mean(x, axis=-1, keepdims=True)
    xc = x - mu
    y = xc * lax.rsqrt(jnp.mean(xc * xc, axis=-1, keepdims=True) + EPS)
    return y * g + b


def _swap_lanes(x):
    half = MLA_ROPE // 2
    lane = lax.broadcasted_iota(jnp.int32, x.shape, 1)
    return jnp.where(lane < half, pltpu.roll(x, LANES - half, axis=1),
                     jnp.where(lane < MLA_ROPE, pltpu.roll(x, half, axis=1), 0.0))


@jax.custom_vjp
def _swap_halves(x):
    return _swap_lanes(x)


_swap_halves.defvjp(lambda x: (_swap_lanes(x), None), lambda _, ct: (_swap_lanes(ct),))


def _rope(x, cos_f, sin_s):
    return x * cos_f + _swap_halves(x) * sin_s


def _lane_blocks(x):
    return tuple(x[:, i * LANES:(i + 1) * LANES] for i in range(x.shape[1] // LANES))


@jax.custom_vjp
def _split_lanes(x):
    return _lane_blocks(x)


_split_lanes.defvjp(lambda x: (_lane_blocks(x), None), lambda _, cts: (jnp.concatenate(cts, axis=1),))


def _softmax(s):
    m = lax.stop_gradient(jnp.max(s, axis=-1, keepdims=True))
    p = jnp.exp(s - m)
    return p / jnp.sum(p, axis=-1, keepdims=True)


def _mm(a, b, *, ta=False, tb=False, ins=(), epilogue=None, out_dtypes=(F32,), owner_cols=None, total=False, name,
        comm=None, rows=None, into=None):
    if ta:
        k_dim, m = a.shape
    else:
        m, k_dim = a.shape
    if tb:
        n, kb = b.shape
    else:
        kb, n = b.shape
    assert k_dim == kb, (a.shape, b.shape, ta, tb)
    part, n_parts = rows if rows is not None else (0, 1)
    tm = _pick(m // n_parts, (1024, 512, 256, 128))
    tn = _pick(n if owner_cols is None else owner_cols, (1024, 768, 512, 384, 256, 128))
    tk = _pick(k_dim, (2048, 1024, 768, 512, 256, 128))
    nk = k_dim // tk
    m_steps = m // tm // n_parts
    off = part * m_steps
    ca = 0 if ta else 1
    cb = 1 if tb else 0
    n_in = len(ins)
    n_out = len(out_dtypes)
    n_pass = 0 if into is None else 1

    def finish(r, in_refs, out_refs, first_tile):
        vals = epilogue(r, *[ref[...].astype(F32) for ref in in_refs]) if epilogue is not None else (r,)
        for ref, val, dt in zip(out_refs, vals, out_dtypes):
            ref[...] = val.astype(dt)
        if total:
            _acc(out_refs[n_out], vals[n_out], first_tile)

    def body(*refs):
        a_ref, b_ref = refs[:2]
        in_refs = refs[2:2 + n_in]
        o0 = 2 + n_in + n_pass
        out_refs = refs[o0:o0 + n_out + int(total)]
        first_tile = jnp.logical_and(pl.program_id(0) == 0, pl.program_id(1) == 0)
        part = _dn(a_ref[...], b_ref[...], ca, cb)
        if nk == 1:
            finish(part, in_refs, out_refs, first_tile)
            return
        acc = refs[-1]
        k = pl.program_id(2)
        _acc(acc, part, k == 0)

        @pl.when(k == nk - 1)
        def _():
            finish(acc[...], in_refs, out_refs, first_tile)

    a_spec = (pl.BlockSpec((tk, tm), lambda i, j, k: (k, i + off)) if ta
              else pl.BlockSpec((tm, tk), lambda i, j, k: (i + off, k)))
    b_spec = pl.BlockSpec((tn, tk), lambda i, j, k: (j, k)) if tb else pl.BlockSpec((tk, tn), lambda i, j, k: (k, j))
    t_spec = pl.BlockSpec((tm, tn), lambda i, j, k: (i + off, j))
    if owner_cols is None:
        o_spec, o_shape = t_spec, (m, n)
    else:
        per = owner_cols // tn
        o_spec = pl.BlockSpec((None, tm, tn), lambda i, j, k: (j // per, i + off, j % per))
        o_shape = (n // owner_cols, m, owner_cols)
    o_specs = [o_spec] * n_out + ([pl.BlockSpec((8, LANES), lambda i, j, k: (0, 0))] if total else [])
    o_shapes = [SDS(o_shape, dt) for dt in out_dtypes] + ([SDS((8, LANES), F32)] if total else [])
    in_specs = [a_spec, b_spec] + [t_spec] * n_in + [ANY] * n_pass
    args = [a, b, *ins] + ([into] if n_pass else [])
    run = _pcall(body, grid=(m_steps, n // tn, nk), in_specs=in_specs, out_specs=o_specs, out_shape=o_shapes,
                 scratch_shapes=[pltpu.VMEM((tm, tn), F32)] if nk > 1 else [],
                 sem=("arbitrary",) * 3 if total else ("parallel", "parallel", "arbitrary"), name=name, comm=comm,
                 aliases={len(in_specs) - 1: 0} if n_pass else None)
    if comm is None:
        outs = run(*args)
        return outs[0] if len(outs) == 1 else outs
    outs, exchanged = run(*args)
    return (outs[0] if len(outs) == 1 else outs), exchanged


def _add_to(r, x):
    return (r + x,)


def _relu2(r):
    p = jnp.maximum(r, 0.0)
    return r, p * p


def _relu2_bwd(dr, a):
    return (dr * (2.0 * jnp.maximum(a, 0.0)),)


def _loss_tail(r, x1, tgt):
    e = (r + x1) - tgt
    dy = e * (1.0 / D_MODEL)
    part = jnp.sum(jnp.sum(e * e, axis=-1, keepdims=True), axis=0, keepdims=True) * (0.5 / D_MODEL)
    return dy, dy, jnp.broadcast_to(part, (8, LANES))


def _rms_fwd(x, g, name, comm=None):
    n, w = x.shape
    t = min(ROW_TILE, n)

    def body(x_ref, g_ref, o_ref):
        o_ref[...] = _rmsn(x_ref[...], g_ref[...], w).astype(BF)

    return _pcall(body, grid=(n // t,), in_specs=[_rows(t, w), _full((1, w))], out_specs=_rows(t, w),
                  out_shape=SDS((n, w), BF), sem=("arbitrary",), name=name, comm=comm)(x, g)


def _rms_bwd(x, g, dh, res, name, comm=None):
    n, w = x.shape
    t = min(ROW_TILE, n)
    has_res = res is not None

    def body(*refs):
        if has_res:
            x_ref, g_ref, dh_ref, res_ref, dx_ref, dxb_ref, dg_ref = refs
        else:
            x_ref, g_ref, dh_ref, dx_ref, dxb_ref, dg_ref = refs
        _, vjp = jax.vjp(lambda xx, gg: _rmsn(xx, gg, w), x_ref[...], g_ref[...])
        dx, dg = vjp(dh_ref[...])
        if has_res:
            dx = dx + res_ref[...]
        dx_ref[...] = dx
        dxb_ref[...] = dx.astype(BF)
        _acc(dg_ref, dg, pl.program_id(0) == 0)

    in_specs = [_rows(t, w), _full((1, w)), _rows(t, w)] + ([_rows(t, w)] if has_res else [])
    args = [x, g, dh] + ([res] if has_res else [])
    return _pcall(body, grid=(n // t,), in_specs=in_specs, out_specs=[_rows(t, w), _rows(t, w), _full((1, w))],
                  out_shape=[SDS((n, w), F32), SDS((n, w), BF), SDS((1, w), F32)], sem=("arbitrary",), name=name,
                  comm=comm)(*args)


def _merge_core(zg0, zg1, zg2, y0, y1, y2):
    return jax.nn.sigmoid(zg0) * y0 + jax.nn.sigmoid(zg1) * y1 + jax.nn.sigmoid(zg2) * y2


def _merge_fwd(z, y_gm, y_mla, y_mem, name):
    n = z.shape[0]
    t = min(ROW_TILE, n)
    w = D_MODEL

    def body(g0, g1, g2, y0, y1, y2, o_ref):
        o_ref[...] = _merge_core(g0[...].astype(F32), g1[...].astype(F32), g2[...].astype(F32), y0[...].astype(F32), y1[...].astype(F32),
                                 y2[...].astype(F32)).astype(BF)

    return pl.pallas_call(body, grid=(n // t,),
                          in_specs=[_rows(t, w, 0), _rows(t, w, 1), _rows(t, w, 2)] + [_rows(t, w)] * 3,
                          out_specs=_rows(t, w), out_shape=SDS((n, w), BF),
                          compiler_params=_params(("parallel",)), name=name)(z, z, z, y_gm, y_mla, y_mem)


def _merge_bwd(z, y_gm, y_mla, y_mem, dmerged, name):
    n = z.shape[0]
    t = min(ROW_TILE, n)
    w = D_MODEL

    def body(g0, g1, g2, y0, y1, y2, dm, dzg_ref, d0_ref, d1_ref, d2_ref):
        _, vjp = jax.vjp(_merge_core, g0[...].astype(F32), g1[...].astype(F32), g2[...].astype(F32), y0[...].astype(F32), y1[...].astype(F32),
                         y2[...].astype(F32))
        dg0, dg1, dg2, dy0, dy1, dy2 = vjp(dm[...])
        dzg_ref[:, 0:w] = dg0.astype(BF)
        dzg_ref[:, w:2 * w] = dg1.astype(BF)
        dzg_ref[:, 2 * w:3 * w] = dg2.astype(BF)
        d0_ref[...] = dy0.astype(BF)
        d1_ref[...] = dy1.astype(BF)
        d2_ref[...] = dy2.astype(BF)

    return pl.pallas_call(body, grid=(n // t,),
                          in_specs=[_rows(t, w, 0), _rows(t, w, 1), _rows(t, w, 2)] + [_rows(t, w)] * 4,
                          out_specs=[_rows(t, 3 * w, ZG // (3 * w))] + [_rows(t, w)] * 3,
                          out_shape=[SDS((n, Z_COLS), BF)] + [SDS((n, w), BF)] * 3,
                          compiler_params=_params(("parallel",)), name=name)(z, z, z, y_gm, y_mla, y_mem, dmerged)


def _gm_core(zu, zv, g_ln, b_ln, ws, bcols):
    t = zu.shape[0]
    u = jax.nn.gelu(zu)
    v = _layernorm(jax.nn.gelu(zv), g_ln, b_ln)
    row = lax.broadcasted_iota(jnp.int32, (GM_CHUNK, GM_CHUNK), 0)
    col = lax.broadcasted_iota(jnp.int32, (GM_CHUNK, GM_CHUNK), 1)
    wc = [jnp.where(row >= col, ws[g], 0.0) for g in range(GM_GROUPS)]
    chunks = []
    for c in range(t // GM_CHUNK):
        cols = []
        for g in range(GM_GROUPS):
            vc = v[c * GM_CHUNK:(c + 1) * GM_CHUNK, g * LANES:(g + 1) * LANES]
            cols.append(_mm_nn(wc[g], vc) + bcols[g])
        chunks.append(jnp.concatenate(cols, axis=1))
    mixed = chunks[0] if len(chunks) == 1 else jnp.concatenate(chunks, axis=0)
    return u * mixed


def _gm_specs(t):
    return [_rows(t, GM_WIDTH, ZU // GM_WIDTH), _rows(t, GM_WIDTH, ZV // GM_WIDTH), _full((1, GM_WIDTH)),
            _full((1, GM_WIDTH)), _full((GM_GROUPS, GM_CHUNK, GM_CHUNK))] + [_full((GM_CHUNK, 1))] * GM_GROUPS


def _gm_fwd(z, g_ln, b_ln, ws, bcols, name):
    n = z.shape[0]
    t = min(ROW_TILE, n)

    def body(zu, zv, g_ref, b_ref, ws_ref, c0, c1, c2, c3, o_ref):
        out = _gm_core(zu[...].astype(F32), zv[...].astype(F32), g_ref[...], b_ref[...], [ws_ref[g] for g in range(GM_GROUPS)],
                       [c0[...], c1[...], c2[...], c3[...]])
        o_ref[...] = out.astype(BF)

    return pl.pallas_call(body, grid=(n // t,), in_specs=_gm_specs(t), out_specs=_rows(t, GM_WIDTH),
                          out_shape=SDS((n, GM_WIDTH), BF), compiler_params=_params(("parallel",)),
                          name=name)(z, z, g_ln, b_ln, ws, *bcols)


def _gm_bwd(z, g_ln, b_ln, ws, bcols, dgm, dz, name, comm=None):
    n = z.shape[0]
    t = min(ROW_TILE, n)

    def body(zu, zv, g_ref, b_ref, ws_ref, c0, c1, c2, c3, dgm_ref, _, dz_ref, dg_ref, db_ref, dws_ref, e0, e1, e2,
             e3):
        first = pl.program_id(0) == 0
        _, vjp = jax.vjp(_gm_core, zu[...].astype(F32), zv[...].astype(F32), g_ref[...], b_ref[...],
                         [ws_ref[g] for g in range(GM_GROUPS)], [c0[...], c1[...], c2[...], c3[...]])
        dzu, dzv, dg, db, dws, dcols = vjp(dgm_ref[...])
        dz_ref[:, 0:GM_WIDTH] = dzu.astype(BF)
        dz_ref[:, GM_WIDTH:2 * GM_WIDTH] = dzv.astype(BF)
        _acc(dg_ref, dg, first)
        _acc(db_ref, db, first)
        _acc(dws_ref, jnp.stack(dws, axis=0), first)
        for ref, val in zip((e0, e1, e2, e3), dcols):
            _acc(ref, val, first)

    in_specs = _gm_specs(t) + [_rows(t, GM_WIDTH), ANY]
    return _pcall(
        body, grid=(n // t,), in_specs=in_specs,
        out_specs=[_rows(t, 2 * GM_WIDTH, ZU // (2 * GM_WIDTH)), _full((1, GM_WIDTH)), _full((1, GM_WIDTH)),
                   _full((GM_GROUPS, GM_CHUNK, GM_CHUNK))] + [_full((GM_CHUNK, 1))] * GM_GROUPS,
        out_shape=[SDS((n, Z_COLS), BF), SDS((1, GM_WIDTH), F32), SDS((1, GM_WIDTH), F32),
                   SDS((GM_GROUPS, GM_CHUNK, GM_CHUNK), F32)] + [SDS((GM_CHUNK, 1), F32)] * GM_GROUPS,
        sem=("arbitrary",), name=name, comm=comm, aliases={len(in_specs) - 1: 0})(z, z, g_ln, b_ln, ws, *bcols, dgm, dz)


def _rope_tables(pos_f, inv_full, cmask, smask, name, comm=None):
    n = pos_f.shape[0]
    t = min(ROW_TILE, n)

    def body(p_ref, inv_ref, cm_ref, sm_ref, cos_ref, sin_ref):
        ang = p_ref[...] * inv_ref[...]
        cos_ref[...] = jnp.cos(ang) * cm_ref[...]
        sin_ref[...] = jnp.sin(ang) * sm_ref[...]

    return _pcall(body, grid=(n // t,), in_specs=[_rows(t, 1)] + [_full((1, LANES))] * 3,
                  out_specs=[_rows(t, LANES)] * 2, out_shape=[SDS((n, LANES), F32)] * 2, sem=("parallel",),
                  name=name, comm=comm)(pos_f, inv_full, cmask, smask)


def _prep_norms(cq, ckv, g_cq, g_ckv):
    return _rmsn(cq, g_cq, Q_LORA), _rmsn(ckv, g_ckv, KV_LORA)


def _prep_heads(qa, kva, kpe, head_gains, cos_f, sin_s):
    g_qn, g_qp, g_kn, g_kp = head_gains
    qs = _split_lanes(qa)
    kvs = _split_lanes(kva)
    kp = _rope(_rmsn(kpe, g_kp, MLA_ROPE), cos_f, sin_s)
    q_out, k_out = [], []
    for h in range(MLA_HEADS):
        q_out.append(_rmsn(qs[h], g_qn, MLA_NOPE))
        q_out.append(_rope(_rmsn(qs[MLA_HEADS + h], g_qp, MLA_ROPE), cos_f, sin_s))
        k_out.append(_rmsn(kvs[h], g_kn, MLA_NOPE))
        k_out.append(kp)
    return (jnp.concatenate(q_out, axis=1), jnp.concatenate(k_out, axis=1),
            jnp.concatenate(kvs[MLA_HEADS:], axis=1))


def _prep_in_specs(t):
    return ([_rows(t, Q_LORA, CQ // Q_LORA), _rows(t, LANES, KPE // LANES), _rows(t, KV_LORA, CKV // KV_LORA),
             _rows(t, LANES), _rows(t, LANES), _full((1, Q_LORA)), _full((1, KV_LORA))] + [_full((1, LANES))] * 4
            + [_full((Q_LORA, 2048)), _full((KV_LORA, 2048))])


def _prep_fwd(z, cos_f, sin_s, gains, wq, wkv, name):
    n = z.shape[0]
    t = min(ROW_TILE, n)

    def body(cq, kpe, ckv, cos_ref, sin_ref, g_cq, g_ckv, g_qn, g_qp, g_kn, g_kp, wq_ref, wkv_ref, q_ref, k_ref, v_ref):
        cqn, ckvn = _prep_norms(cq[...].astype(F32), ckv[...].astype(F32), g_cq[...], g_ckv[...])
        qa = _dn(cqn, wq_ref[...], 1, 0)
        kva = _dn(ckvn, wkv_ref[...], 1, 0)
        q, k, v = _prep_heads(qa, kva, kpe[...].astype(F32), (g_qn[...], g_qp[...], g_kn[...], g_kp[...]), cos_ref[...],
                              sin_ref[...])
        q_ref[...] = q.astype(BF)
        k_ref[...] = k.astype(BF)
        v_ref[...] = v.astype(BF)

    return pl.pallas_call(body, grid=(n // t,), in_specs=_prep_in_specs(t),
                          out_specs=[_rows(t, 2048), _rows(t, 2048), _rows(t, 1024)],
                          out_shape=[SDS((n, 2048), BF), SDS((n, 2048), BF), SDS((n, 1024), BF)],
                          compiler_params=_params(("parallel",)),
                          name=name)(z, z, z, cos_f, sin_s, *gains, wq, wkv)


def _prep_bwd(z, cos_f, sin_s, gains, wq, wkv, dq, dk, dv, dz, name, comm=None):
    n = z.shape[0]
    t = min(ROW_TILE, n)
    wz = Q_LORA + LANES + KV_LORA

    def body(cq, kpe, ckv, cos_ref, sin_ref, g_cq, g_ckv, g_qn, g_qp, g_kn, g_kp, wq_ref, wkv_ref, dq_ref, dk_ref,
             dv_ref, _, dz_ref, o_cq, o_ckv, o_qn, o_qp, o_kn, o_kp, dwq_ref, dwkv_ref):
        first = pl.program_id(0) == 0
        cos_t, sin_t = cos_ref[...], sin_ref[...]
        (cqn, ckvn), vjp_norms = jax.vjp(_prep_norms, cq[...].astype(F32), ckv[...].astype(F32), g_cq[...], g_ckv[...])
        wq_t, wkv_t = wq_ref[...], wkv_ref[...]
        qa = _dn(cqn, wq_t, 1, 0)
        kva = _dn(ckvn, wkv_t, 1, 0)
        _, vjp_heads = jax.vjp(lambda a, b, c, g: _prep_heads(a, b, c, g, cos_t, sin_t), qa, kva, kpe[...].astype(F32),
                               (g_qn[...], g_qp[...], g_kn[...], g_kp[...]))
        dqa, dkva, dkpe, dhead = vjp_heads((dq_ref[...], dk_ref[...], dv_ref[...]))
        _acc(dwq_ref, _dn(cqn, dqa, 0, 0), first)
        _acc(dwkv_ref, _dn(ckvn, dkva, 0, 0), first)
        dcq, dckv, dg_cq, dg_ckv = vjp_norms((_dn(dqa, wq_t, 1, 1), _dn(dkva, wkv_t, 1, 1)))
        dz_ref[:, 0:Q_LORA] = dcq.astype(BF)
        dz_ref[:, Q_LORA:Q_LORA + LANES] = dkpe.astype(BF)
        dz_ref[:, Q_LORA + LANES:wz] = dckv.astype(BF)
        for ref, val in zip((o_cq, o_ckv, o_qn, o_qp, o_kn, o_kp), (dg_cq, dg_ckv) + tuple(dhead)):
            _acc(ref, val, first)

    gain_specs = [_full((1, Q_LORA)), _full((1, KV_LORA))] + [_full((1, LANES))] * 4
    gain_shapes = [SDS((1, Q_LORA), F32), SDS((1, KV_LORA), F32)] + [SDS((1, LANES), F32)] * 4
    in_specs = _prep_in_specs(t) + [_rows(t, 2048), _rows(t, 2048), _rows(t, 1024), ANY]
    return _pcall(
        body, grid=(n // t,), in_specs=in_specs,
        out_specs=[_rows(t, wz, CQ // wz)] + gain_specs + [_full((Q_LORA, 2048)), _full((KV_LORA, 2048))],
        out_shape=[SDS((n, Z_COLS), BF)] + gain_shapes + [SDS((Q_LORA, 2048), F32), SDS((KV_LORA, 2048), F32)],
        sem=("arbitrary",), name=name, comm=comm,
        aliases={len(in_specs) - 1: 0})(z, z, z, cos_f, sin_s, *gains, wq, wkv, dq, dk, dv, dz)


MLA_QK = 256
MLA_SCALE = 1.0 / math.sqrt(MLA_NOPE + MLA_ROPE)
LOG2E = 1.0 / math.log(2.0)
MLA_SCALE_LOG2E = MLA_SCALE * LOG2E


def _causal_mask(s, q0, k0):
    tq, tk = s.shape
    row = q0 + lax.broadcasted_iota(jnp.int32, (tq, tk), 0)
    col = k0 + lax.broadcasted_iota(jnp.int32, (tq, tk), 1)
    return jnp.where(row >= col, s, -jnp.inf)


def _mla_fwd(q, k, v, batch, seq, name, comm=None):
    n = q.shape[0]
    tq = min(ATT_TILE, seq)
    nq = seq // tq

    nh = ATT_HEADS

    def body(q_ref, k_ref, v_ref, o_ref, lse_ref):
        i = pl.program_id(2)

        def step(j, carry, diagonal=False):
            k0 = pl.multiple_of(j * tq, tq)
            out = []
            ones = jnp.ones((tq, LANES), BF)
            for hh in range(nh):
                m, acc = carry[hh]
                qb = q_ref[:, hh * MLA_QK:(hh + 1) * MLA_QK]
                kb = k_ref[pl.ds(k0, tq), hh * MLA_QK:(hh + 1) * MLA_QK]
                vb = v_ref[pl.ds(k0, tq), hh * MLA_V:(hh + 1) * MLA_V]
                s = _dn(qb, kb, 1, 1)
                if diagonal:
                    s = _causal_mask(s, i * tq, k0)
                m_new = jnp.maximum(m, jnp.max(s, axis=-1, keepdims=True))
                p = jnp.exp2((s - m_new) * MLA_SCALE_LOG2E)
                alpha = jnp.exp2((m - m_new) * MLA_SCALE_LOG2E)
                acc = alpha * acc + _dn(p, jnp.concatenate([vb, ones], axis=1), 1, 0)
                out.append((m_new, acc))
            return tuple(out)

        init = tuple((jnp.full((tq, 1), -jnp.inf, F32), jnp.zeros((tq, MLA_V + LANES), F32)) for _ in range(nh))
        final = step(i, lax.fori_loop(0, i, step, init), diagonal=True)
        for hh, (m, acc) in enumerate(final):
            l = acc[:, MLA_V:MLA_V + 1]
            o_ref[:, hh * MLA_V:(hh + 1) * MLA_V] = acc[:, :MLA_V] / l
            lse_ref[:, hh * LANES:(hh + 1) * LANES] = jnp.broadcast_to(m * MLA_SCALE + jnp.log(l), (tq, LANES))

    return _pcall(
        body, grid=(batch, MLA_HEADS // nh, nq),
        in_specs=[pl.BlockSpec((tq, nh * MLA_QK), lambda b, h, i: (b * nq + i, h)),
                  pl.BlockSpec((seq, nh * MLA_QK), lambda b, h, i: (b, h)),
                  pl.BlockSpec((seq, nh * MLA_V), lambda b, h, i: (b, h))],
        out_specs=[pl.BlockSpec((tq, nh * MLA_V), lambda b, h, i: (b * nq + i, h)),
                   pl.BlockSpec((tq, nh * LANES), lambda b, h, i: (b * nq + i, h))],
        out_shape=[SDS((n, MLA_HEADS * MLA_V), F32), SDS((n, MLA_HEADS * LANES), F32)],
        sem=("parallel", "parallel", "arbitrary"), name=name, comm=comm)(q, k, v)


def _mla_bwd(q, k, v, o, lse, do, batch, seq, name, comm=None):
    n = q.shape[0]
    tk = min(ATT_TILE, seq)
    nk = seq // tk

    nh = ATT_HEADS

    def body(q_ref, k_ref, v_ref, o_ref, lse_ref, do_ref, dq_ref, dk_ref, dv_ref):
        jk = pl.program_id(2)

        @pl.when(jk == 0)
        def _():
            dq_ref[...] = jnp.zeros_like(dq_ref)

        def step(i, carry, diagonal=False):
            q0 = pl.multiple_of(i * tk, tk)
            rows = pl.ds(q0, tk)
            out = []
            for hh in range(nh):
                dk_acc, dv_acc = carry[hh]
                qk_cols = slice(hh * MLA_QK, (hh + 1) * MLA_QK)
                v_cols = slice(hh * MLA_V, (hh + 1) * MLA_V)
                kb = k_ref[:, qk_cols]
                vb = v_ref[:, v_cols]
                qb = q_ref[rows, qk_cols]
                dob = do_ref[rows, v_cols]
                delta = jnp.sum(dob * o_ref[rows, v_cols], axis=-1, keepdims=True)
                s = _dn(qb, kb, 1, 1)
                if diagonal:
                    s = _causal_mask(s, q0, jk * tk)
                p = jnp.exp2(s * MLA_SCALE_LOG2E - lse_ref[rows, hh * LANES:hh * LANES + 1] * LOG2E)
                dv_acc = dv_acc + _dn(p, dob, 0, 0)
                dp = _dn(dob, vb, 1, 1)
                ds = p * (dp - delta) * MLA_SCALE
                dk_acc = dk_acc + _dn(ds, qb, 0, 0)
                dq_ref[rows, qk_cols] += _dn(ds, kb, 1, 0)
                out.append((dk_acc, dv_acc))
            return tuple(out)

        init = tuple((jnp.zeros((tk, MLA_QK), F32), jnp.zeros((tk, MLA_V), F32)) for _ in range(nh))
        final = lax.fori_loop(jk + 1, nk, step, step(jk, init, diagonal=True))
        for hh, (dk_acc, dv_acc) in enumerate(final):
            dk_ref[:, hh * MLA_QK:(hh + 1) * MLA_QK] = dk_acc
            dv_ref[:, hh * MLA_V:(hh + 1) * MLA_V] = dv_acc

    full_qk = pl.BlockSpec((seq, nh * MLA_QK), lambda b, h, j: (b, h))
    full_v = pl.BlockSpec((seq, nh * MLA_V), lambda b, h, j: (b, h))
    blk_qk = pl.BlockSpec((tk, nh * MLA_QK), lambda b, h, j: (b * nk + j, h))
    blk_v = pl.BlockSpec((tk, nh * MLA_V), lambda b, h, j: (b * nk + j, h))
    return _pcall(
        body, grid=(batch, MLA_HEADS // nh, nk),
        in_specs=[full_qk, blk_qk, blk_v, full_v, full_v, full_v],
        out_specs=[full_qk, blk_qk, blk_v],
        out_shape=[SDS((n, MLA_HEADS * MLA_QK), F32), SDS((n, MLA_HEADS * MLA_QK), F32),
                   SDS((n, MLA_HEADS * MLA_V), F32)],
        sem=("parallel", "parallel", "arbitrary"), name=name, comm=comm)(q, k, v, o, lse, do)


MEM_SCALE = 1.0 / math.sqrt(HEAD_DIM)
MEM_W = MEM_HEADS * HEAD_DIM


def _mem_core(qs, ks, vs, g_mq, g_mk):
    outs = []
    for h in range(MEM_HEADS):
        qh = _rmsn(qs[h], g_mq, HEAD_DIM)
        kh = _rmsn(ks[h], g_mk, HEAD_DIM)
        p = _softmax(_mm_nt(qh, kh) * MEM_SCALE)
        outs.append(_mm_nn(p, vs[h]))
    return jnp.concatenate(outs, axis=1)


def _mem_load(qm, kvm, g_mq, g_mk):
    hs = range(MEM_HEADS)
    qs = [qm[:, h * LANES:(h + 1) * LANES].astype(F32) for h in hs]
    ks = [kvm[:, h * LANES:(h + 1) * LANES] for h in hs]
    vs = [kvm[:, MEM_W + h * LANES:MEM_W + (h + 1) * LANES] for h in hs]
    return qs, ks, vs, g_mq[...], g_mk[...]


def _mem_fwd(z, kvm, g_mq, g_mk, batch, seq, name, comm=None):
    n = z.shape[0]
    t = min(ROW_TILE, seq)
    per = seq // t

    def body(qm, kvm_ref, gq, gk, o_ref):
        o_ref[...] = _mem_core(*_mem_load(qm, kvm_ref, gq, gk)).astype(BF)

    return _pcall(
        body, grid=(n // t,),
        in_specs=[_rows(t, MEM_W, QM // MEM_W), pl.BlockSpec((MEM_LEN, 2 * MEM_W), lambda i: (i // per, 0)),
                  _full((1, LANES)), _full((1, LANES))],
        out_specs=_rows(t, MEM_W), out_shape=SDS((n, MEM_W), BF), sem=("parallel",), name=name,
        comm=comm)(z, kvm, g_mq, g_mk)


def _mem_bwd(z, kvm, g_mq, g_mk, dom, dz, batch, seq, name):
    n = z.shape[0]
    t = min(ROW_TILE, seq)
    per = seq // t

    def body(qm, kvm_ref, gq, gk, dom_ref, _, dz_ref, dkvm_ref, dgq_ref, dgk_ref):
        i = pl.program_id(0)
        _, vjp = jax.vjp(_mem_core, *_mem_load(qm, kvm_ref, gq, gk))
        dqs, dks, dvs, dgq, dgk = vjp(dom_ref[...])
        dz_ref[...] = jnp.concatenate(dqs, axis=1).astype(BF)
        _acc(dkvm_ref, jnp.concatenate(dks + dvs, axis=1), i % per == 0)
        _acc(dgq_ref, dgq, i == 0)
        _acc(dgk_ref, dgk, i == 0)

    kv_spec = pl.BlockSpec((MEM_LEN, 2 * MEM_W), lambda i: (i // per, 0))
    return pl.pallas_call(
        body, grid=(n // t,),
        in_specs=[_rows(t, MEM_W, QM // MEM_W), kv_spec, _full((1, LANES)), _full((1, LANES)), _rows(t, MEM_W), ANY],
        out_specs=[_rows(t, MEM_W, QM // MEM_W), kv_spec, _full((1, LANES)), _full((1, LANES))],
        out_shape=[SDS((n, Z_COLS), BF), SDS((batch * MEM_LEN, 2 * MEM_W), F32), SDS((1, LANES), F32),
                   SDS((1, LANES), F32)],
        input_output_aliases={5: 0},
        compiler_params=_params(("arbitrary",)), name=name)(z, kvm, g_mq, g_mk, dom, dz)


def _me():
    return lax.axis_index("x"), lax.axis_index("y"), lax.axis_index("c")


def _other_chips(x, y):
    return [(1 - x, y), (x, 1 - y), (1 - x, 1 - y)]


def _shard_shape(name):
    r, c = BIG_SHAPE[name]
    return (r, c // N_CHIPS) if name in COL_SHARDED else (r // N_CHIPS, c)


def _n_pieces(half_rows):
    return max(1, half_rows // PIECE_ROWS)


def _piece_plan(shapes):
    plan = []
    for r, _ in shapes:
        h = r // 2
        n = _n_pieces(h)
        plan.append((h, n, h // n))
    return plan


def _remote(send, recv, sem, src, dst, to):
    return pltpu.make_async_remote_copy(src_ref=src, dst_ref=dst, send_sem=send.at[sem], recv_sem=recv.at[sem],
                                        device_id=to, device_id_type=MESH)


def _gather_far(shards):
    plan = _piece_plan([s.shape for s in shards])
    n_far = 3 * sum(n for _, n, _ in plan)
    n_loc = 2 * sum(n for _, n, _ in plan)

    def copies(s_refs, o_refs, send, recv, local):
        x, y, c = _me()
        k = 2 * x + y
        mine, sends, arrivals = [], [], []
        for t, (h, n, pr) in enumerate(plan):
            s_ref, o_ref = s_refs[t], o_refs[t]
            for core in range(2):
                for p in range(n):
                    rows = pl.ds(core * h + p * pr, pr)
                    mine.append(pltpu.make_async_copy(s_ref.at[rows], o_ref.at[k, rows], local.at[len(mine)]))
            for chip in _other_chips(x, y):
                for p in range(n):
                    rows = pl.ds(c * h + p * pr, pr)
                    s = len(sends)
                    sends.append(_remote(send, recv, s, s_ref.at[rows], o_ref.at[k, rows], (*chip, c)))
                    arrivals.append(_remote(send, recv, s, s_ref.at[rows], o_ref.at[2 * chip[0] + chip[1], rows],
                                            (*chip, c)))
        return sends, arrivals, mine

    return _Phase(shards, [SDS((N_CHIPS,) + s.shape, s.dtype) for s in shards], n_far, n_loc, copies)


def _gather_near(bufs):
    plan = _piece_plan([b.shape[1:] for b in bufs])
    n_sem = 3 * sum(n for _, n, _ in plan)

    def copies(i_refs, o_refs, send, recv, local):
        x, y, c = _me()
        sib = (x, y, 1 - c)
        sends, arrivals = [], []
        for t, (h, n, pr) in enumerate(plan):
            for chip in _other_chips(x, y):
                ci = 2 * chip[0] + chip[1]
                for p in range(n):
                    rows = pl.ds(c * h + p * pr, pr)
                    rows_sib = pl.ds((1 - c) * h + p * pr, pr)
                    s = len(sends)
                    sends.append(_remote(send, recv, s, i_refs[t].at[ci, rows], o_refs[t].at[ci, rows], sib))
                    arrivals.append(_remote(send, recv, s, i_refs[t].at[ci, rows_sib], o_refs[t].at[ci, rows_sib], sib))
        return sends, arrivals, []

    return _Phase(bufs, [SDS(b.shape, b.dtype) for b in bufs], n_sem, 0, copies, {t: t for t in range(len(bufs))})


def _pair_exchange(grads):
    plan = _piece_plan([g.shape[1:] for g in grads])
    n_sem = sum(n for _, n, _ in plan)

    def copies(g_refs, o_refs, send, recv, local):
        x, y, c = _me()
        sends = []
        for t, (h, n, pr) in enumerate(plan):
            for p in range(n):
                sends.append(_remote(send, recv, len(sends), g_refs[t].at[:, pl.ds((1 - c) * h + p * pr, pr)],
                                     o_refs[t].at[:, pl.ds(p * pr, pr)], (x, y, 1 - c)))
        return sends, sends, []

    return _Phase(grads, [SDS((N_CHIPS, g.shape[1] // 2, g.shape[2]), F32) for g in grads], n_sem, 0, copies)


def _pair_add(ck, g, theirs, name):
    _, r, c = g.shape
    (h, n, pr), = _piece_plan([(r, c)])

    def body(ck_ref, g_ref, t_ref, p32_ref, pbf_ref):
        s = g_ref[...] + t_ref[...]
        p32_ref[...] = s
        pbf_ref[...] = s.astype(BF)

    half = pl.BlockSpec((None, pr, c), lambda k, p, ck: (k, p, 0))
    spec = pltpu.PrefetchScalarGridSpec(
        num_scalar_prefetch=1, grid=(N_CHIPS, n),
        in_specs=[pl.BlockSpec((None, pr, c), lambda k, p, ck: (k, ck[0] * n + p, 0)), half], out_specs=[half, half])
    return pl.pallas_call(body, grid_spec=spec, out_shape=[SDS((N_CHIPS, h, c), F32), SDS((N_CHIPS, h, c), BF)],
                          compiler_params=_params(("arbitrary", "arbitrary")), name=name)(ck, g, theirs)


def _scatter_partials(pbfs):
    plan = [(h, _n_pieces(h), h // _n_pieces(h)) for h in [p.shape[1] for p in pbfs]]
    n_sem = 3 * sum(n for _, n, _ in plan)

    def copies(p_refs, o_refs, send, recv, local):
        x, y, c = _me()
        sends = []
        for t, (h, n, pr) in enumerate(plan):
            for j, chip in enumerate(_other_chips(x, y)):
                for p in range(n):
                    rows = pl.ds(p * pr, pr)
                    sends.append(_remote(send, recv, len(sends), p_refs[t].at[2 * chip[0] + chip[1], rows],
                                         o_refs[t].at[j, rows], (*chip, c)))
        return sends, sends, []

    return _Phase(pbfs, [SDS((3,) + p.shape[1:], BF) for p in pbfs], n_sem, 0, copies)


def _sum_chips(ck, p32, slots, name):
    _, h, c = p32.shape
    n = _n_pieces(h)
    pr = h // n

    def body(ck_ref, p_ref, s_ref, o_ref):
        o_ref[...] = ((p_ref[...] + s_ref[0].astype(F32)) + s_ref[1].astype(F32)) + s_ref[2].astype(F32)

    spec = pltpu.PrefetchScalarGridSpec(
        num_scalar_prefetch=1, grid=(n,),
        in_specs=[pl.BlockSpec((None, pr, c), lambda p, ck: (ck[1], p, 0)),
                  pl.BlockSpec((3, pr, c), lambda p, ck: (0, p, 0))],
        out_specs=pl.BlockSpec((pr, c), lambda p, ck: (ck[0] * n + p, 0)))
    return pl.pallas_call(body, grid_spec=spec, out_shape=SDS((2 * h, c), F32),
                          compiler_params=_params(("arbitrary",)), name=name)(ck, p32, slots)


def _join_halves(sums):
    plan = _piece_plan([s.shape for s in sums])
    n_sem = sum(n for _, n, _ in plan)

    def copies(r_refs, o_refs, send, recv, local):
        x, y, c = _me()
        sends, arrivals = [], []
        for t, (h, n, pr) in enumerate(plan):
            for p in range(n):
                rows = pl.ds(c * h + p * pr, pr)
                rows_sib = pl.ds((1 - c) * h + p * pr, pr)
                s = len(sends)
                sends.append(_remote(send, recv, s, r_refs[t].at[rows], o_refs[t].at[rows], (x, y, 1 - c)))
                arrivals.append(_remote(send, recv, s, r_refs[t].at[rows_sib], o_refs[t].at[rows_sib], (x, y, 1 - c)))
        return sends, arrivals, []

    return _Phase(sums, [SDS(s.shape, F32) for s in sums], n_sem, 0, copies, {t: t for t in range(len(sums))})


def _gather_small(s, name):
    def body(s_ref, o_ref, send, recv, local):
        x, y, c = _me()
        me = 4 * x + 2 * y + c
        keep = pltpu.make_async_copy(s_ref, o_ref.at[me], local)
        keep.start()
        sends = []
        for r in range(1, 8):
            fx, fy, fc = (r >> 2) & 1, (r >> 1) & 1, r & 1
            to = (x ^ fx, y ^ fy, c ^ fc)
            sends.append(pltpu.make_async_remote_copy(
                src_ref=s_ref, dst_ref=o_ref.at[me], send_sem=send.at[r - 1], recv_sem=recv.at[r - 1],
                device_id=to, device_id_type=MESH))
        for cp in sends:
            cp.start()
        for r in range(1, 8):
            fx, fy, fc = (r >> 2) & 1, (r >> 1) & 1, r & 1
            src = 4 * (x ^ fx) + 2 * (y ^ fy) + (c ^ fc)
            pltpu.make_async_remote_copy(
                src_ref=s_ref, dst_ref=o_ref.at[src], send_sem=send.at[r - 1], recv_sem=recv.at[r - 1],
                device_id=(x ^ fx, y ^ fy, c ^ fc), device_id_type=MESH).wait_recv()
        for cp in sends:
            cp.wait_send()
        keep.wait()

    return pl.pallas_call(
        body, in_specs=[ANY], out_specs=ANY, out_shape=SDS((8, SMALL_ROWS, LANES), F32),
        scratch_shapes=[pltpu.SemaphoreType.DMA((7,)), pltpu.SemaphoreType.DMA((7,)), pltpu.SemaphoreType.DMA],
        name=name)(s)


def _adam_math(w, g, m, v):
    nm = ADAM_B1 * m + (1.0 - ADAM_B1) * g
    nv = ADAM_B2 * v + (1.0 - ADAM_B2) * (g * g)
    m_hat = nm / (1.0 - ADAM_B1 ** ADAM_STEP)
    v_hat = nv / (1.0 - ADAM_B2 ** ADAM_STEP)
    return -ADAM_LR * (m_hat / (jnp.sqrt(v_hat) + ADAM_EPS) + ADAM_WD * w), nm, nv


def _adamw(w, g, m, v, name):
    _, r, c = w.shape
    t = max(d for d in range(8, r + 1, 8) if r % d == 0 and 16 * d * c * 4 <= VMEM_LIMIT - (8 << 20))

    def body(w_ref, g_ref, m_ref, v_ref, go_ref, d_ref, nm_ref, nv_ref):
        g_ = g_ref[...]
        d, nm, nv = _adam_math(w_ref[...], g_, m_ref[...], v_ref[...])
        go_ref[...] = g_
        d_ref[...] = d
        nm_ref[...] = nm
        nv_ref[...] = nv

    lead = pl.BlockSpec((None, t, c), lambda i: (0, i, 0))
    return pl.pallas_call(body, grid=(r // t,), in_specs=[lead, _rows(t, c), lead, lead], out_specs=[lead] * 4,
                          out_shape=[SDS((1, r, c), F32)] * 4, compiler_params=_params(("parallel",)),
                          name=name)(w, g, m, v)


def _small_layout():
    out, r0 = {}, 0
    for n in SMALL:
        size = int(np.prod(SMALL_SHAPE[n]))
        nr = -(-size // LANES)
        out[n] = (r0, nr)
        r0 += nr
    assert r0 <= SMALL_ROWS
    return out, r0


def _pack_small(grads, loss_tile, name):
    layout, used = _small_layout()

    def body(*refs):
        o_ref = refs[-1]
        o_ref[used:used + 1, :] = refs[-2][0:1, :]
        for n, ref in zip(SMALL, refs[:-2]):
            r0, nr = layout[n]
            if n == "w_spatial":
                for g in range(GM_GROUPS):
                    o_ref[r0 + g * GM_CHUNK:r0 + (g + 1) * GM_CHUNK, :] = ref[g]
            elif n == "b_spatial":
                o_ref[r0:r0 + nr, :] = ref[...]
            else:
                for i in range(nr):
                    o_ref[r0 + i:r0 + i + 1, :] = ref[:, i * LANES:(i + 1) * LANES]
        if used + 1 < SMALL_ROWS:
            o_ref[used + 1:SMALL_ROWS, :] = jnp.zeros((SMALL_ROWS - used - 1, LANES), F32)

    return pl.pallas_call(body, out_shape=SDS((SMALL_ROWS, LANES), F32), name=name)(*grads, loss_tile)


def _adamw_small(gathered, ws, ms, vs, name):
    layout, used = _small_layout()
    n_t = len(SMALL)

    def body(*refs):
        g_ref = refs[0]
        w_refs, m_refs, v_refs = refs[1:1 + n_t], refs[1 + n_t:1 + 2 * n_t], refs[1 + 2 * n_t:1 + 3 * n_t]
        outs = refs[1 + 3 * n_t:1 + 7 * n_t]
        acc = refs[-1]
        total = g_ref[0]
        for j in range(1, 8):
            total = total + g_ref[j]
        acc[...] = total
        refs[1 + 7 * n_t][...] = acc[used:used + 1, :]
        for t, n in enumerate(SMALL):
            r0, nr = layout[n]
            o_refs = [outs[t], outs[n_t + t], outs[2 * n_t + t], outs[3 * n_t + t]]
            if n == "w_spatial":
                views = [((0, g), slice(r0 + g * GM_CHUNK, r0 + (g + 1) * GM_CHUNK), slice(None))
                         for g in range(GM_GROUPS)]
            elif n == "b_spatial":
                views = [((0,), slice(r0, r0 + nr), slice(None))]
            else:
                width = SMALL_SHAPE[n][1]
                views = [((slice(None), slice(i * LANES, min((i + 1) * LANES, width))), slice(r0 + i, r0 + i + 1),
                          slice(0, min(LANES, width - i * LANES))) for i in range(nr)]
            for idx, rows, lanes in views:
                g = acc[rows, lanes]
                d, nm, nv = _adam_math(w_refs[t][idx], g, m_refs[t][idx], v_refs[t][idx])
                for ref, val in zip(o_refs, (g, d, nm, nv)):
                    ref[idx] = val

    shapes = [SDS(SMALL_SHAPE[n], F32) for n in SMALL]
    return pl.pallas_call(body, out_shape=shapes * 4 + [SDS((1, LANES), F32)],
                          scratch_shapes=[pltpu.VMEM((SMALL_ROWS, LANES), F32)], name=name)(gathered, *ws, *ms, *vs)


def _win_layout(w_in):
    pad = jnp.zeros((w_in.shape[0], LANES - MLA_ROPE), w_in.dtype)
    u, v, cq = w_in[:, 0:512], w_in[:, 512:1024], w_in[:, 1024:1408]
    ckv, kpe, qm, zg = w_in[:, 1408:1664], w_in[:, 1664:1728], w_in[:, 1728:2240], w_in[:, 2240:5312]
    return jnp.concatenate([zg, u, v, qm, cq, kpe, pad, ckv], axis=1)


def _win_unlayout(g):
    zg, u, v, qm = g[:, ZG:ZG + 3072], g[:, ZU:ZU + 512], g[:, ZV:ZV + 512], g[:, QM:QM + 512]
    cq, kpe, ckv = g[:, CQ:CQ + 384], g[:, KPE:KPE + MLA_ROPE], g[:, CKV:CKV + 256]
    return jnp.concatenate([u, v, cq, ckv, kpe, qm, zg], axis=1)


def _wq_layout(w_uq):
    w = w_uq.reshape(Q_LORA, MLA_HEADS, MLA_NOPE + MLA_ROPE)
    nope = w[:, :, :MLA_NOPE].reshape(Q_LORA, MLA_HEADS * MLA_NOPE)
    pe = jnp.pad(w[:, :, MLA_NOPE:], ((0, 0), (0, 0), (0, LANES - MLA_ROPE))).reshape(Q_LORA, MLA_HEADS * LANES)
    return jnp.concatenate([nope, pe], axis=1)


def _wq_unlayout(g):
    nope = g[:, :1024].reshape(Q_LORA, MLA_HEADS, MLA_NOPE)
    pe = g[:, 1024:].reshape(Q_LORA, MLA_HEADS, LANES)[:, :, :MLA_ROPE]
    return jnp.concatenate([nope, pe], axis=2).reshape(Q_LORA, MLA_HEADS * (MLA_NOPE + MLA_ROPE))


def _wkv_layout(w_ukv):
    w = w_ukv.reshape(KV_LORA, MLA_HEADS, MLA_NOPE + MLA_V)
    return jnp.concatenate([w[:, :, :MLA_NOPE].reshape(KV_LORA, 1024), w[:, :, MLA_NOPE:].reshape(KV_LORA, 1024)],
                           axis=1)


def _wkv_unlayout(g):
    kn = g[:, :1024].reshape(KV_LORA, MLA_HEADS, MLA_NOPE)
    v = g[:, 1024:].reshape(KV_LORA, MLA_HEADS, MLA_V)
    return jnp.concatenate([kn, v], axis=2).reshape(KV_LORA, MLA_HEADS * (MLA_NOPE + MLA_V))


def _owner_major(g, name):
    r, c = _shard_shape(name)
    return g.reshape(r, N_CHIPS, c).transpose(1, 0, 2) if name in COL_SHARDED else g.reshape(N_CHIPS, r, c)


def _pad_lanes(g):
    return jnp.pad(g, ((0, 0), (0, LANES - g.shape[1])))


def kernel(x, mem, positions, g_mix, w_in, g_cq, w_uq, g_ckv, w_ukv, g_q_nope, g_q_pe, g_k_nope, g_k_pe, g_gm_ln, b_gm_ln, w_spatial, b_spatial, g_mem, w_mem_kv, g_mq, g_mk, w_o_gm, w_o_mla, w_o_mem, w_out, g_ffn, w_ff1, w_ff2, loss_target, m_g_mix, m_w_in, m_g_cq, m_w_uq, m_g_ckv, m_w_ukv, m_g_q_nope, m_g_q_pe, m_g_k_nope, m_g_k_pe, m_g_gm_ln, m_b_gm_ln, m_w_spatial, m_b_spatial, m_g_mem, m_w_mem_kv, m_g_mq, m_g_mk, m_w_o_gm, m_w_o_mla, m_w_o_mem, m_w_out, m_g_ffn, m_w_ff1, m_w_ff2, v_g_mix, v_w_in, v_g_cq, v_w_uq, v_g_ckv, v_w_ukv, v_g_q_nope, v_g_q_pe, v_g_k_nope, v_g_k_pe, v_g_gm_ln, v_b_gm_ln, v_w_spatial, v_b_spatial, v_g_mem, v_w_mem_kv, v_g_mq, v_g_mk, v_w_o_gm, v_w_o_mla, v_w_o_mem, v_w_out, v_g_ffn, v_w_ff1, v_w_ff2):
    given = dict(locals())
    wts = {n: given[n] for n in WEIGHTS}
    mom = {n: given["m_" + n] for n in WEIGHTS}
    var = {n: given["v_" + n] for n in WEIGHTS}
    batch, seq, _ = x.shape
    n_tok = batch * seq

    def natural(n, g):
        r, c = _shard_shape(n)
        return g.transpose(1, 0, 2).reshape(r, N_CHIPS * c) if n in COL_SHARDED else g.reshape(N_CHIPS * r, c)

    def far(names):
        return _gather_far([wts[n][0].astype(BF) for n in names])

    x2 = x.reshape(n_tok, D_MODEL)
    tgt2 = loss_target.reshape(n_tok, D_MODEL)
    mem2 = mem.reshape(batch * MEM_LEN, D_MODEL)
    pos_f = positions.reshape(n_tok, 1).astype(F32)

    inv = ROPE_BASE ** (-jnp.arange(0, MLA_ROPE, 2, dtype=F32) / MLA_ROPE)
    zeros64 = jnp.zeros((LANES - MLA_ROPE,), F32)
    inv_full = jnp.concatenate([inv, inv, zeros64]).reshape(1, LANES)
    half = MLA_ROPE // 2
    cmask = jnp.concatenate([jnp.ones((MLA_ROPE,), F32), zeros64]).reshape(1, LANES)
    smask = jnp.concatenate([-jnp.ones((half,), F32), jnp.ones((half,), F32), zeros64]).reshape(1, LANES)

    prep_gains = [g_cq, g_ckv, g_q_nope, _pad_lanes(g_q_pe), g_k_nope, _pad_lanes(g_k_pe)]
    ws = w_spatial[0]
    bcols = [b_spatial[0, g].reshape(GM_CHUNK, 1) for g in range(GM_GROUPS)]

    h1, early_far = _rms_fwd(x2, g_mix, "rms_mix", comm=far(EARLY))
    (cos_f, sin_s), early = _rope_tables(pos_f, inv_full, cmask, smask, "rope_tables", comm=_gather_near(early_far))
    full = {n: natural(n, g) for n, g in zip(EARLY, early)}
    win = _win_layout(full["w_in"])
    wq = _wq_layout(full["w_uq"])
    wkv = _wkv_layout(full["w_ukv"])
    z, proj_far = _mm(h1, win, out_dtypes=(BF,), name="mm_in", comm=far(LATE_PROJ))
    gm = _gm_fwd(z, g_gm_ln, b_gm_ln, ws, bcols, "gm_fwd")
    qc, kc, vc = _prep_fwd(z, cos_f, sin_s, prep_gains, wq, wkv, "prep_fwd")
    (o_mla, lse), ff_far = _mla_fwd(qc, kc, vc, batch, seq, "mla_fwd", comm=far(LATE_FF))
    memn = _rms_fwd(mem2, g_mem, "rms_mem")
    kvm, proj = _mm(memn, full["w_mem_kv"], name="mm_memkv", comm=_gather_near(proj_far))
    o_mem, ff = _mem_fwd(z, kvm, g_mq, g_mk, batch, seq, "mem_fwd", comm=_gather_near(ff_far))
    full.update({n: natural(n, g) for n, g in zip(LATE_PROJ + LATE_FF, list(proj) + list(ff))})
    y_gm = _mm(gm, full["w_o_gm"], out_dtypes=(BF,), name="mm_o_gm")
    y_mla = _mm(o_mla, full["w_o_mla"], out_dtypes=(BF,), name="mm_o_mla")
    y_mem = _mm(o_mem, full["w_o_mem"], out_dtypes=(BF,), name="mm_o_mem")
    merged = _merge_fwd(z, y_gm, y_mla, y_mem, "merge_fwd")
    x1 = _mm(merged, full["w_out"], ins=(x2,), epilogue=_add_to, name="mm_out")
    h2 = _rms_fwd(x1, g_ffn, "rms_ffn")
    a_ff, r_ff = _mm(h2, full["w_ff1"], epilogue=_relu2, out_dtypes=(BF, BF), name="mm_ff1")
    dy, dyb, loss_tile = _mm(r_ff, full["w_ff2"], ins=(x1, tgt2), epilogue=_loss_tail, out_dtypes=(F32, BF),
                             total=True, name="mm_ff2")

    gw = {}
    da = _mm(dyb, full["w_ff2"], tb=True, ins=(a_ff,), epilogue=_relu2_bwd, out_dtypes=(BF,), name="mm_d_a")
    gw["w_ff2"] = _owner_major(_mm(r_ff, dyb, ta=True, name="mm_dw_ff2"), "w_ff2")
    gw["w_ff1"] = _mm(h2, da, ta=True, owner_cols=D_FF // N_CHIPS, name="mm_dw_ff1")
    dh2 = _mm(da, full["w_ff1"], tb=True, name="mm_d_h2")
    dx1, dx1b, dg_ffn = _rms_bwd(x1, g_ffn, dh2, dy, "rms_ffn_bwd")
    dmerged = _mm(dx1b, full["w_out"], tb=True, name="mm_d_merged")
    gw["w_out"] = _owner_major(_mm(merged, dx1b, ta=True, name="mm_dw_out"), "w_out")
    dz, dy_gm, dy_mla, dy_mem = _merge_bwd(z, y_gm, y_mla, y_mem, dmerged, "merge_bwd")
    dgm = _mm(dy_gm, full["w_o_gm"], tb=True, name="mm_d_gm")
    gw["w_o_gm"] = _mm(gm, dy_gm, ta=True, owner_cols=D_MODEL // N_CHIPS, name="mm_dw_o_gm")
    do_mla = _mm(dy_mla, full["w_o_mla"], tb=True, name="mm_d_omla")
    gw["w_o_mla"] = _owner_major(_mm(o_mla, dy_mla, ta=True, name="mm_dw_o_mla"), "w_o_mla")
    do_mem = _mm(dy_mem, full["w_o_mem"], tb=True, name="mm_d_omem")
    gw["w_o_mem"] = _mm(o_mem, dy_mem, ta=True, owner_cols=D_MODEL // N_CHIPS, name="mm_dw_o_mem")
    ck = jnp.stack([lax.axis_index("c"), 2 * lax.axis_index("x") + lax.axis_index("y")]).astype(jnp.int32)

    def pair_sums(names, theirs):
        return [_pair_add(ck, gw[n], t, "pair_add_" + n) for n, t in zip(names, theirs)]

    def chip_sums(names, pairs, slots):
        return [_sum_chips(ck, p[0], s, "sum_chips_" + n) for n, p, s in zip(names, pairs, slots)]

    (dz, dg_ln, db_ln, dws, *dbcols), theirs = _gm_bwd(z, g_gm_ln, b_gm_ln, ws, bcols, dgm, dz, "gm_bwd",
                                                      comm=_pair_exchange([gw[n] for n in LATE]))
    pairs = pair_sums(LATE, theirs)
    (dq, dk, dv), slots = _mla_bwd(qc, kc, vc, o_mla, lse, do_mla, batch, seq, "mla_bwd",
                                   comm=_scatter_partials([p[1] for p in pairs]))
    sums = chip_sums(LATE, pairs, slots)
    (dz, dg_cq, dg_ckv, dg_qn, dg_qp, dg_kn, dg_kp, dwq, dwkv), reduced_late = _prep_bwd(
        z, cos_f, sin_s, prep_gains, wq, wkv, dq, dk, dv, dz, "prep_bwd", comm=_join_halves(sums))
    dz, dkvm, dg_mq, dg_mk = _mem_bwd(z, kvm, g_mq, g_mk, do_mem, dz, batch, seq, "mem_bwd")
    dmemn = _mm(dkvm, full["w_mem_kv"], tb=True, name="mm_d_memn")
    gw["w_mem_kv"] = _owner_major(_mm(memn, dkvm, ta=True, name="mm_dw_memkv"), "w_mem_kv")
    _, _, dg_mem = _rms_bwd(mem2, g_mem, dmemn, None, "rms_mem_bwd")
    gw["w_in"] = _owner_major(_win_unlayout(_mm(h1, dz, ta=True, name="mm_dw_in")), "w_in")
    gw["w_uq"] = _owner_major(_wq_unlayout(dwq), "w_uq")
    gw["w_ukv"] = _owner_major(_wkv_unlayout(dwkv), "w_ukv")
    dh1, theirs = _mm(dz, win, tb=True, name="mm_d_h1_top", rows=(0, 2), comm=_pair_exchange([gw[n] for n in EARLY]))
    pairs = pair_sums(EARLY, theirs)
    dh1, slots = _mm(dz, win, tb=True, name="mm_d_h1_bottom", rows=(1, 2), into=dh1,
                     comm=_scatter_partials([p[1] for p in pairs]))
    grad_x, _, dg_mix = _rms_bwd(x2, g_mix, dh1, dx1, "rms_mix_bwd")
    reduced_early = _run_phase(_join_halves(chip_sums(EARLY, pairs, slots)), "join_early")
    reduced = dict(zip(LATE + EARLY, list(reduced_late) + list(reduced_early)))

    def swapped(a):
        return jnp.swapaxes(a, -1, -2)

    results = {n: _adamw(wts[n], reduced[n], mom[n], var[n], "adamw_" + n) for n in BIG if n != "w_in"}
    results["w_in"] = [swapped(r) for r in _adamw(swapped(w_in), swapped(reduced["w_in"]), swapped(m_w_in),
                                                  swapped(v_w_in), "adamw_w_in")]

    small_g = {"g_mix": dg_mix, "g_cq": dg_cq, "g_ckv": dg_ckv, "g_q_nope": dg_qn, "g_q_pe": dg_qp,
               "g_k_nope": dg_kn, "g_k_pe": dg_kp, "g_gm_ln": dg_ln, "b_gm_ln": db_ln, "w_spatial": dws,
               "b_spatial": jnp.concatenate(dbcols, axis=1).T, "g_mem": dg_mem, "g_mq": dg_mq, "g_mk": dg_mk,
               "g_ffn": dg_ffn}
    packed = _pack_small([small_g[n] for n in SMALL], loss_tile, "pack_small")
    small_out = _adamw_small(_gather_small(packed, "gather_small"), [wts[n] for n in SMALL],
                             [mom[n] for n in SMALL], [var[n] for n in SMALL], "adamw_small")
    for t, n in enumerate(SMALL):
        results[n] = [small_out[j * len(SMALL) + t] for j in range(4)]

    loss = small_out[4 * len(SMALL)][0, 0]
    grad_x = grad_x.reshape(batch, seq, D_MODEL)
    return (loss, grad_x, *[results[n][0] for n in WEIGHTS], *[results[n][1] for n in WEIGHTS],
            *[results[n][2] for n in WEIGHTS], *[results[n][3] for n in WEIGHTS])
```

```python
import functools
import math

import numpy as np
import jax
import jax.numpy as jnp
from jax import lax
from jax.experimental import pallas as pl
from jax.experimental.pallas import tpu as pltpu

F32 = jnp.float32
BF = jnp.bfloat16
SDS = jax.ShapeDtypeStruct
MESH = pl.DeviceIdType.MESH

D_MODEL = 1024
MEM_LEN = 256
MEM_HEADS = 4
HEAD_DIM = 128
GM_WIDTH = 512
GM_CHUNK = 128
GM_GROUPS = 4
MLA_HEADS = 8
MLA_NOPE = 128
MLA_ROPE = 64
MLA_V = 128
Q_LORA = 384
KV_LORA = 256
ROPE_BASE = 10000.0
D_FF = 4096
EPS = 1e-6
W_IN_COLS = 5312
ADAM_LR, ADAM_B1, ADAM_B2, ADAM_EPS, ADAM_WD, ADAM_STEP = 0.001, 0.9, 0.999, 1e-08, 0.01, 10

ZG, ZU, ZV, QM, CQ, KPE, CKV = 0, 3072, 3584, 4096, 4608, 4992, 5120
Z_COLS = 5376
LANES = 128
ROW_TILE = 256
ATT_TILE = 1024
ATT_HEADS = 2
VMEM_LIMIT = 56 * 1024 * 1024

N_CHIPS = 4
PIECE_ROWS = 256
SMALL_ROWS = 560

BIG = ["w_in", "w_uq", "w_ukv", "w_mem_kv", "w_o_gm", "w_o_mla", "w_o_mem", "w_out", "w_ff1", "w_ff2"]
BIG_SHAPE = {"w_in": (1024, 5312), "w_uq": (384, 1536), "w_ukv": (256, 2048), "w_mem_kv": (1024, 1024),
             "w_o_gm": (512, 1024), "w_o_mla": (1024, 1024), "w_o_mem": (512, 1024), "w_out": (1024, 1024),
             "w_ff1": (1024, 4096), "w_ff2": (4096, 1024)}
COL_SHARDED = {"w_in", "w_uq", "w_ukv", "w_o_gm", "w_o_mem", "w_ff1"}
EARLY = ["w_in", "w_uq", "w_ukv", "w_mem_kv"]
LATE_PROJ = ["w_o_gm", "w_o_mla", "w_o_mem", "w_out"]
LATE_FF = ["w_ff1", "w_ff2"]
LATE = LATE_PROJ + LATE_FF
SMALL = ["w_spatial", "b_spatial", "g_mix", "g_cq", "g_ckv", "g_q_nope", "g_q_pe", "g_k_nope", "g_k_pe", "g_gm_ln",
         "b_gm_ln", "g_mem", "g_mq", "g_mk", "g_ffn"]
SMALL_SHAPE = {"g_mix": (1, 1024), "g_cq": (1, 384), "g_ckv": (1, 256), "g_q_nope": (1, 128), "g_q_pe": (1, 64),
               "g_k_nope": (1, 128), "g_k_pe": (1, 64), "g_gm_ln": (1, 512), "b_gm_ln": (1, 512),
               "w_spatial": (1, 4, 128, 128), "b_spatial": (1, 4, 128), "g_mem": (1, 1024), "g_mq": (1, 128),
               "g_mk": (1, 128), "g_ffn": (1, 1024)}
WEIGHTS = ['g_mix', 'w_in', 'g_cq', 'w_uq', 'g_ckv', 'w_ukv', 'g_q_nope', 'g_q_pe', 'g_k_nope', 'g_k_pe',
           'g_gm_ln', 'b_gm_ln', 'w_spatial', 'b_spatial', 'g_mem', 'w_mem_kv', 'g_mq', 'g_mk', 'w_o_gm',
           'w_o_mla', 'w_o_mem', 'w_out', 'g_ffn', 'w_ff1', 'w_ff2']


def _params(sem=None):
    return pltpu.CompilerParams(vmem_limit_bytes=VMEM_LIMIT, dimension_semantics=sem)


def _pick(n, prefs):
    for p in prefs:
        if n % p == 0:
            return p
    return n


def _full(shape):
    nd = len(shape)
    return pl.BlockSpec(shape, lambda *_: (0,) * nd)


def _rows(t, w, blk=0):
    return pl.BlockSpec((t, w), lambda i: (i, blk))


def _acc(ref, val, first):
    @pl.when(first)
    def _():
        ref[...] = val

    @pl.when(jnp.logical_not(first))
    def _():
        ref[...] += val


ANY = pl.BlockSpec(memory_space=pl.ANY)


class _Phase:
    def __init__(self, operands, out_shapes, n_sem, n_local, copies, aliases=None):
        self.operands, self.out_shapes, self.aliases = list(operands), list(out_shapes), dict(aliases or {})
        self.n_sem, self.n_local, self.copies = n_sem, max(n_local, 1), copies

    def sem_shapes(self):
        return [pltpu.SemaphoreType.DMA((self.n_sem,)), pltpu.SemaphoreType.DMA((self.n_sem,)),
                pltpu.SemaphoreType.DMA((self.n_local,))]

    def start(self, ins, outs, send, recv, local):
        sends, _, locals_ = self.copies(ins, outs, send, recv, local)
        for cp in locals_ + sends:
            cp.start()

    def finish(self, ins, outs, send, recv, local):
        sends, arrivals, locals_ = self.copies(ins, outs, send, recv, local)
        for cp in arrivals:
            cp.wait_recv()
        for cp in sends:
            cp.wait_send()
        for cp in locals_:
            cp.wait()


def _run_phase(phase, name):
    n_in = len(phase.operands)

    def body(*refs):
        ins, outs, sems = refs[:n_in], refs[n_in:n_in + len(phase.out_shapes)], refs[n_in + len(phase.out_shapes):]
        phase.start(ins, outs, *sems)
        phase.finish(ins, outs, *sems)

    return pl.pallas_call(body, in_specs=[ANY] * n_in, out_specs=[ANY] * len(phase.out_shapes),
                          out_shape=phase.out_shapes, scratch_shapes=phase.sem_shapes(),
                          input_output_aliases=phase.aliases, name=name)(*phase.operands)


def _pcall(body, *, grid, in_specs, out_specs, out_shape, scratch_shapes=(), sem=None, name, comm=None, aliases=None):
    single = not isinstance(out_shape, (list, tuple))
    o_specs = [out_specs] if single else list(out_specs)
    o_shape = [out_shape] if single else list(out_shape)
    aliases = dict(aliases or {})
    if comm is None:
        call = pl.pallas_call(body, grid=grid, in_specs=list(in_specs), out_specs=o_specs, out_shape=o_shape,
                              scratch_shapes=list(scratch_shapes), input_output_aliases=aliases,
                              compiler_params=_params(sem), name=name)

        def run_plain(*args):
            res = call(*args)
            return res[0] if single else res

        return run_plain

    n_in, n_out, n_scr = len(in_specs), len(o_specs), len(scratch_shapes)
    nc_in, nc_out = len(comm.operands), len(comm.out_shapes)

    def wrapped(*refs):
        ins, cins = refs[:n_in], refs[n_in:n_in + nc_in]
        o0 = n_in + nc_in
        outs, couts = refs[o0:o0 + n_out], refs[o0 + n_out:o0 + n_out + nc_out]
        s0 = o0 + n_out + nc_out
        scr, csem = refs[s0:s0 + n_scr], refs[s0 + n_scr:]
        ids = [pl.program_id(d) for d in range(len(grid))]
        first = functools.reduce(jnp.logical_and, [i == 0 for i in ids])
        last = functools.reduce(jnp.logical_and, [i == g - 1 for i, g in zip(ids, grid)])

        @pl.when(first)
        def _():
            comm.start(cins, couts, *csem)

        body(*ins, *outs, *scr)

        @pl.when(last)
        def _():
            comm.finish(cins, couts, *csem)

    call = pl.pallas_call(
        wrapped, grid=grid, in_specs=list(in_specs) + [ANY] * nc_in, out_specs=o_specs + [ANY] * nc_out,
        out_shape=o_shape + comm.out_shapes, scratch_shapes=list(scratch_shapes) + comm.sem_shapes(),
        input_output_aliases={**aliases, **{n_in + i: n_out + j for i, j in comm.aliases.items()}},
        compiler_params=_params(("arbitrary",) * len(grid)), name=name)

    def run_carrying(*args):
        res = call(*args, *comm.operands)
        return (res[0] if single else res[:n_out]), res[n_out:]

    return run_carrying


def _dn(a, b, ca, cb):
    return lax.dot_general(a.astype(BF), b.astype(BF), (((ca,), (cb,)), ((), ())), preferred_element_type=F32)


@jax.custom_vjp
def _mm_nn(a, b):
    return _dn(a, b, 1, 0)


def _mm_nn_fwd(a, b):
    return _dn(a, b, 1, 0), (a.astype(BF), b.astype(BF))


def _mm_nn_bwd(res, ct):
    a, b = res
    return _dn(ct, b, 1, 1), _dn(a, ct, 0, 0)


_mm_nn.defvjp(_mm_nn_fwd, _mm_nn_bwd)


@jax.custom_vjp
def _mm_nt(a, b):
    return _dn(a, b, 1, 1)


def _mm_nt_fwd(a, b):
    return _dn(a, b, 1, 1), (a.astype(BF), b.astype(BF))


def _mm_nt_bwd(res, ct):
    a, b = res
    return _dn(ct, b, 1, 0), _dn(ct, a, 0, 0)


_mm_nt.defvjp(_mm_nt_fwd, _mm_nt_bwd)


def _rmsn(x, g, n):
    ms = jnp.sum(x * x, axis=-1, keepdims=True) * (1.0 / n)
    return x * lax.rsqrt(ms + EPS) * g


def _layernorm(x, g, b):
    mu = jnp.mean(x, axis=-1, keepdims=True)
    xc = x - mu
    y = xc * lax.rsqrt(jnp.mean(xc * xc, axis=-1, keepdims=True) + EPS)
    return y * g + b


def _swap_lanes(x):
    half = MLA_ROPE // 2
    lane = lax.broadcasted_iota(jnp.int32, x.shape, 1)
    return jnp.where(lane < half, pltpu.roll(x, LANES - half, axis=1),
                     jnp.where(lane < MLA_ROPE, pltpu.roll(x, half, axis=1), 0.0))


@jax.custom_vjp
def _swap_halves(x):
    return _swap_lanes(x)


_swap_halves.defvjp(lambda x: (_swap_lanes(x), None), lambda _, ct: (_swap_lanes(ct),))


def _rope(x, cos_f, sin_s):
    return x * cos_f + _swap_halves(x) * sin_s


def _lane_blocks(x):
    return tuple(x[:, i * LANES:(i + 1) * LANES] for i in range(x.shape[1] // LANES))


@jax.custom_vjp
def _split_lanes(x):
    return _lane_blocks(x)


_split_lanes.defvjp(lambda x: (_lane_blocks(x), None), lambda _, cts: (jnp.concatenate(cts, axis=1),))


def _softmax(s):
    m = lax.stop_gradient(jnp.max(s, axis=-1, keepdims=True))
    p = jnp.exp(s - m)
    return p / jnp.sum(p, axis=-1, keepdims=True)


def _mm(a, b, *, ta=False, tb=False, ins=(), epilogue=None, out_dtypes=(F32,), owner_cols=None, total=False, name,
        comm=None, rows=None, into=None):
    if ta:
        k_dim, m = a.shape
    else:
        m, k_dim = a.shape
    if tb:
        n, kb = b.shape
    else:
        kb, n = b.shape
    assert k_dim == kb, (a.shape, b.shape, ta, tb)
    part, n_parts = rows if rows is not None else (0, 1)
    tm = _pick(m // n_parts, (1024, 512, 256, 128))
    tn = _pick(n if owner_cols is None else owner_cols, (1024, 768, 512, 384, 256, 128))
    tk = _pick(k_dim, (2048, 1024, 768, 512, 256, 128))
    nk = k_dim // tk
    m_steps = m // tm // n_parts
    off = part * m_steps
    ca = 0 if ta else 1
    cb = 1 if tb else 0
    n_in = len(ins)
    n_out = len(out_dtypes)
    n_pass = 0 if into is None else 1

    def finish(r, in_refs, out_refs, first_tile):
        vals = epilogue(r, *[ref[...].astype(F32) for ref in in_refs]) if epilogue is not None else (r,)
        for ref, val, dt in zip(out_refs, vals, out_dtypes):
            ref[...] = val.astype(dt)
        if total:
            _acc(out_refs[n_out], vals[n_out], first_tile)

    def body(*refs):
        a_ref, b_ref = refs[:2]
        in_refs = refs[2:2 + n_in]
        o0 = 2 + n_in + n_pass
        out_refs = refs[o0:o0 + n_out + int(total)]
        first_tile = jnp.logical_and(pl.program_id(0) == 0, pl.program_id(1) == 0)
        part = _dn(a_ref[...], b_ref[...], ca, cb)
        if nk == 1:
            finish(part, in_refs, out_refs, first_tile)
            return
        acc = refs[-1]
        k = pl.program_id(2)
        _acc(acc, part, k == 0)

        @pl.when(k == nk - 1)
        def _():
            finish(acc[...], in_refs, out_refs, first_tile)

    a_spec = (pl.BlockSpec((tk, tm), lambda i, j, k: (k, i + off)) if ta
              else pl.BlockSpec((tm, tk), lambda i, j, k: (i + off, k)))
    b_spec = pl.BlockSpec((tn, tk), lambda i, j, k: (j, k)) if tb else pl.BlockSpec((tk, tn), lambda i, j, k: (k, j))
    t_spec = pl.BlockSpec((tm, tn), lambda i, j, k: (i + off, j))
    if owner_cols is None:
        o_spec, o_shape = t_spec, (m, n)
    else:
        per = owner_cols // tn
        o_spec = pl.BlockSpec((None, tm, tn), lambda i, j, k: (j // per, i + off, j % per))
        o_shape = (n // owner_cols, m, owner_cols)
    o_specs = [o_spec] * n_out + ([pl.BlockSpec((8, LANES), lambda i, j, k: (0, 0))] if total else [])
    o_shapes = [SDS(o_shape, dt) for dt in out_dtypes] + ([SDS((8, LANES), F32)] if total else [])
    in_specs = [a_spec, b_spec] + [t_spec] * n_in + [ANY] * n_pass
    args = [a, b, *ins] + ([into] if n_pass else [])
    run = _pcall(body, grid=(m_steps, n // tn, nk), in_specs=in_specs, out_specs=o_specs, out_shape=o_shapes,
                 scratch_shapes=[pltpu.VMEM((tm, tn), F32)] if nk > 1 else [],
                 sem=("arbitrary",) * 3 if total else ("parallel", "parallel", "arbitrary"), name=name, comm=comm,
                 aliases={len(in_specs) - 1: 0} if n_pass else None)
    if comm is None:
        outs = run(*args)
        return outs[0] if len(outs) == 1 else outs
    outs, exchanged = run(*args)
    return (outs[0] if len(outs) == 1 else outs), exchanged


def _add_to(r, x):
    return (r + x,)


def _relu2(r):
    p = jnp.maximum(r, 0.0)
    return r, p * p


def _relu2_bwd(dr, a):
    return (dr * (2.0 * jnp.maximum(a, 0.0)),)


def _loss_tail(r, x1, tgt):
    e = (r + x1) - tgt
    dy = e * (1.0 / D_MODEL)
    part = jnp.sum(jnp.sum(e * e, axis=-1, keepdims=True), axis=0, keepdims=True) * (0.5 / D_MODEL)
    return dy, dy, jnp.broadcast_to(part, (8, LANES))


def _rms_fwd(x, g, name, comm=None):
    n, w = x.shape
    t = min(ROW_TILE, n)

    def body(x_ref, g_ref, o_ref):
        o_ref[...] = _rmsn(x_ref[...], g_ref[...], w).astype(BF)

    return _pcall(body, grid=(n // t,), in_specs=[_rows(t, w), _full((1, w))], out_specs=_rows(t, w),
                  out_shape=SDS((n, w), BF), sem=("arbitrary",), name=name, comm=comm)(x, g)


def _rms_bwd(x, g, dh, res, name, comm=None):
    n, w = x.shape
    t = min(ROW_TILE, n)
    has_res = res is not None

    def body(*refs):
        if has_res:
            x_ref, g_ref, dh_ref, res_ref, dx_ref, dxb_ref, dg_ref = refs
        else:
            x_ref, g_ref, dh_ref, dx_ref, dxb_ref, dg_ref = refs
        _, vjp = jax.vjp(lambda xx, gg: _rmsn(xx, gg, w), x_ref[...], g_ref[...])
        dx, dg = vjp(dh_ref[...])
        if has_res:
            dx = dx + res_ref[...]
        dx_ref[...] = dx
        dxb_ref[...] = dx.astype(BF)
        _acc(dg_ref, dg, pl.program_id(0) == 0)

    in_specs = [_rows(t, w), _full((1, w)), _rows(t, w)] + ([_rows(t, w)] if has_res else [])
    args = [x, g, dh] + ([res] if has_res else [])
    return _pcall(body, grid=(n // t,), in_specs=in_specs, out_specs=[_rows(t, w), _rows(t, w), _full((1, w))],
                  out_shape=[SDS((n, w), F32), SDS((n, w), BF), SDS((1, w), F32)], sem=("arbitrary",), name=name,
                  comm=comm)(*args)


def _merge_core(zg0, zg1, zg2, y0, y1, y2):
    return jax.nn.sigmoid(zg0) * y0 + jax.nn.sigmoid(zg1) * y1 + jax.nn.sigmoid(zg2) * y2


def _merge_fwd(z, y_gm, y_mla, y_mem, name):
    n = z.shape[0]
    t = min(ROW_TILE, n)
    w = D_MODEL

    def body(g0, g1, g2, y0, y1, y2, o_ref):
        o_ref[...] = _merge_core(g0[...].astype(F32), g1[...].astype(F32), g2[...].astype(F32), y0[...].astype(F32), y1[...].astype(F32),
                                 y2[...].astype(F32)).astype(BF)

    return pl.pallas_call(body, grid=(n // t,),
                          in_specs=[_rows(t, w, 0), _rows(t, w, 1), _rows(t, w, 2)] + [_rows(t, w)] * 3,
                          out_specs=_rows(t, w), out_shape=SDS((n, w), BF),
                          compiler_params=_params(("parallel",)), name=name)(z, z, z, y_gm, y_mla, y_mem)


def _merge_bwd(z, y_gm, y_mla, y_mem, dmerged, name):
    n = z.shape[0]
    t = min(ROW_TILE, n)
    w = D_MODEL

    def body(g0, g1, g2, y0, y1, y2, dm, dzg_ref, d0_ref, d1_ref, d2_ref):
        _, vjp = jax.vjp(_merge_core, g0[...].astype(F32), g1[...].astype(F32), g2[...].astype(F32), y0[...].astype(F32), y1[...].astype(F32),
                         y2[...].astype(F32))
        dg0, dg1, dg2, dy0, dy1, dy2 = vjp(dm[...])
        dzg_ref[:, 0:w] = dg0.astype(BF)
        dzg_ref[:, w:2 * w] = dg1.astype(BF)
        dzg_ref[:, 2 * w:3 * w] = dg2.astype(BF)
        d0_ref[...] = dy0.astype(BF)
        d1_ref[...] = dy1.astype(BF)
        d2_ref[...] = dy2.astype(BF)

    return pl.pallas_call(body, grid=(n // t,),
                          in_specs=[_rows(t, w, 0), _rows(t, w, 1), _rows(t, w, 2)] + [_rows(t, w)] * 4,
                          out_specs=[_rows(t, 3 * w, ZG // (3 * w))] + [_rows(t, w)] * 3,
                          out_shape=[SDS((n, Z_COLS), BF)] + [SDS((n, w), BF)] * 3,
                          compiler_params=_params(("parallel",)), name=name)(z, z, z, y_gm, y_mla, y_mem, dmerged)


def _gm_core(zu, zv, g_ln, b_ln, ws, bcols):
    t = zu.shape[0]
    u = jax.nn.gelu(zu)
    v = _layernorm(jax.nn.gelu(zv), g_ln, b_ln)
    row = lax.broadcasted_iota(jnp.int32, (GM_CHUNK, GM_CHUNK), 0)
    col = lax.broadcasted_iota(jnp.int32, (GM_CHUNK, GM_CHUNK), 1)
    wc = [jnp.where(row >= col, ws[g], 0.0) for g in range(GM_GROUPS)]
    chunks = []
    for c in range(t // GM_CHUNK):
        cols = []
        for g in range(GM_GROUPS):
            vc = v[c * GM_CHUNK:(c + 1) * GM_CHUNK, g * LANES:(g + 1) * LANES]
            cols.append(_mm_nn(wc[g], vc) + bcols[g])
        chunks.append(jnp.concatenate(cols, axis=1))
    mixed = chunks[0] if len(chunks) == 1 else jnp.concatenate(chunks, axis=0)
    return u * mixed


def _gm_specs(t):
    return [_rows(t, GM_WIDTH, ZU // GM_WIDTH), _rows(t, GM_WIDTH, ZV // GM_WIDTH), _full((1, GM_WIDTH)),
            _full((1, GM_WIDTH)), _full((GM_GROUPS, GM_CHUNK, GM_CHUNK))] + [_full((GM_CHUNK, 1))] * GM_GROUPS


def _gm_fwd(z, g_ln, b_ln, ws, bcols, name):
    n = z.shape[0]
    t = min(ROW_TILE, n)

    def body(zu, zv, g_ref, b_ref, ws_ref, c0, c1, c2, c3, o_ref):
        out = _gm_core(zu[...].astype(F32), zv[...].astype(F32), g_ref[...], b_ref[...], [ws_ref[g] for g in range(GM_GROUPS)],
                       [c0[...], c1[...], c2[...], c3[...]])
        o_ref[...] = out.astype(BF)

    return pl.pallas_call(body, grid=(n // t,), in_specs=_gm_specs(t), out_specs=_rows(t, GM_WIDTH),
                          out_shape=SDS((n, GM_WIDTH), BF), compiler_params=_params(("parallel",)),
                          name=name)(z, z, g_ln, b_ln, ws, *bcols)


def _gm_bwd(z, g_ln, b_ln, ws, bcols, dgm, dz, name, comm=None):
    n = z.shape[0]
    t = min(ROW_TILE, n)

    def body(zu, zv, g_ref, b_ref, ws_ref, c0, c1, c2, c3, dgm_ref, _, dz_ref, dg_ref, db_ref, dws_ref, e0, e1, e2,
             e3):
        first = pl.program_id(0) == 0
        _, vjp = jax.vjp(_gm_core, zu[...].astype(F32), zv[...].astype(F32), g_ref[...], b_ref[...],
                         [ws_ref[g] for g in range(GM_GROUPS)], [c0[...], c1[...], c2[...], c3[...]])
        dzu, dzv, dg, db, dws, dcols = vjp(dgm_ref[...])
        dz_ref[:, 0:GM_WIDTH] = dzu.astype(BF)
        dz_ref[:, GM_WIDTH:2 * GM_WIDTH] = dzv.astype(BF)
        _acc(dg_ref, dg, first)
        _acc(db_ref, db, first)
        _acc(dws_ref, jnp.stack(dws, axis=0), first)
        for ref, val in zip((e0, e1, e2, e3), dcols):
            _acc(ref, val, first)

    in_specs = _gm_specs(t) + [_rows(t, GM_WIDTH), ANY]
    return _pcall(
        body, grid=(n // t,), in_specs=in_specs,
        out_specs=[_rows(t, 2 * GM_WIDTH, ZU // (2 * GM_WIDTH)), _full((1, GM_WIDTH)), _full((1, GM_WIDTH)),
                   _full((GM_GROUPS, GM_CHUNK, GM_CHUNK))] + [_full((GM_CHUNK, 1))] * GM_GROUPS,
        out_shape=[SDS((n, Z_COLS), BF), SDS((1, GM_WIDTH), F32), SDS((1, GM_WIDTH), F32),
                   SDS((GM_GROUPS, GM_CHUNK, GM_CHUNK), F32)] + [SDS((GM_CHUNK, 1), F32)] * GM_GROUPS,
        sem=("arbitrary",), name=name, comm=comm, aliases={len(in_specs) - 1: 0})(z, z, g_ln, b_ln, ws, *bcols, dgm, dz)


def _rope_tables(pos_f, inv_full, cmask, smask, name, comm=None):
    n = pos_f.shape[0]
    t = min(ROW_TILE, n)

    def body(p_ref, inv_ref, cm_ref, sm_ref, cos_ref, sin_ref):
        ang = p_ref[...] * inv_ref[...]
        cos_ref[...] = jnp.cos(ang) * cm_ref[...]
        sin_ref[...] = jnp.sin(ang) * sm_ref[...]

    return _pcall(body, grid=(n // t,), in_specs=[_rows(t, 1)] + [_full((1, LANES))] * 3,
                  out_specs=[_rows(t, LANES)] * 2, out_shape=[SDS((n, LANES), F32)] * 2, sem=("parallel",),
                  name=name, comm=comm)(pos_f, inv_full, cmask, smask)


def _prep_norms(cq, ckv, g_cq, g_ckv):
    return _rmsn(cq, g_cq, Q_LORA), _rmsn(ckv, g_ckv, KV_LORA)


def _prep_heads(qa, kva, kpe, head_gains, cos_f, sin_s):
    g_qn, g_qp, g_kn, g_kp = head_gains
    qs = _split_lanes(qa)
    kvs = _split_lanes(kva)
    kp = _rope(_rmsn(kpe, g_kp, MLA_ROPE), cos_f, sin_s)
    q_out, k_out = [], []
    for h in range(MLA_HEADS):
        q_out.append(_rmsn(qs[h], g_qn, MLA_NOPE))
        q_out.append(_rope(_rmsn(qs[MLA_HEADS + h], g_qp, MLA_ROPE), cos_f, sin_s))
        k_out.append(_rmsn(kvs[h], g_kn, MLA_NOPE))
        k_out.append(kp)
    return (jnp.concatenate(q_out, axis=1), jnp.concatenate(k_out, axis=1),
            jnp.concatenate(kvs[MLA_HEADS:], axis=1))


def _prep_in_specs(t):
    return ([_rows(t, Q_LORA, CQ // Q_LORA), _rows(t, LANES, KPE // LANES), _rows(t, KV_LORA, CKV // KV_LORA),
             _rows(t, LANES), _rows(t, LANES), _full((1, Q_LORA)), _full((1, KV_LORA))] + [_full((1, LANES))] * 4
            + [_full((Q_LORA, 2048)), _full((KV_LORA, 2048))])


def _prep_fwd(z, cos_f, sin_s, gains, wq, wkv, name):
    n = z.shape[0]
    t = min(ROW_TILE, n)

    def body(cq, kpe, ckv, cos_ref, sin_ref, g_cq, g_ckv, g_qn, g_qp, g_kn, g_kp, wq_ref, wkv_ref, q_ref, k_ref, v_ref):
        cqn, ckvn = _prep_norms(cq[...].astype(F32), ckv[...].astype(F32), g_cq[...], g_ckv[...])
        qa = _dn(cqn, wq_ref[...], 1, 0)
        kva = _dn(ckvn, wkv_ref[...], 1, 0)
        q, k, v = _prep_heads(qa, kva, kpe[...].astype(F32), (g_qn[...], g_qp[...], g_kn[...], g_kp[...]), cos_ref[...],
                              sin_ref[...])
        q_ref[...] = q.astype(BF)
        k_ref[...] = k.astype(BF)
        v_ref[...] = v.astype(BF)

    return pl.pallas_call(body, grid=(n // t,), in_specs=_prep_in_specs(t),
                          out_specs=[_rows(t, 2048), _rows(t, 2048), _rows(t, 1024)],
                          out_shape=[SDS((n, 2048), BF), SDS((n, 2048), BF), SDS((n, 1024), BF)],
                          compiler_params=_params(("parallel",)),
                          name=name)(z, z, z, cos_f, sin_s, *gains, wq, wkv)


def _prep_bwd(z, cos_f, sin_s, gains, wq, wkv, dq, dk, dv, dz, name, comm=None):
    n = z.shape[0]
    t = min(ROW_TILE, n)
    wz = Q_LORA + LANES + KV_LORA

    def body(cq, kpe, ckv, cos_ref, sin_ref, g_cq, g_ckv, g_qn, g_qp, g_kn, g_kp, wq_ref, wkv_ref, dq_ref, dk_ref,
             dv_ref, _, dz_ref, o_cq, o_ckv, o_qn, o_qp, o_kn, o_kp, dwq_ref, dwkv_ref):
        first = pl.program_id(0) == 0
        cos_t, sin_t = cos_ref[...], sin_ref[...]
        (cqn, ckvn), vjp_norms = jax.vjp(_prep_norms, cq[...].astype(F32), ckv[...].astype(F32), g_cq[...], g_ckv[...])
        wq_t, wkv_t = wq_ref[...], wkv_ref[...]
        qa = _dn(cqn, wq_t, 1, 0)
        kva = _dn(ckvn, wkv_t, 1, 0)
        _, vjp_heads = jax.vjp(lambda a, b, c, g: _prep_heads(a, b, c, g, cos_t, sin_t), qa, kva, kpe[...].astype(F32),
                               (g_qn[...], g_qp[...], g_kn[...], g_kp[...]))
        dqa, dkva, dkpe, dhead = vjp_heads((dq_ref[...], dk_ref[...], dv_ref[...]))
        _acc(dwq_ref, _dn(cqn, dqa, 0, 0), first)
        _acc(dwkv_ref, _dn(ckvn, dkva, 0, 0), first)
        dcq, dckv, dg_cq, dg_ckv = vjp_norms((_dn(dqa, wq_t, 1, 1), _dn(dkva, wkv_t, 1, 1)))
        dz_ref[:, 0:Q_LORA] = dcq.astype(BF)
        dz_ref[:, Q_LORA:Q_LORA + LANES] = dkpe.astype(BF)
        dz_ref[:, Q_LORA + LANES:wz] = dckv.astype(BF)
        for ref, val in zip((o_cq, o_ckv, o_qn, o_qp, o_kn, o_kp), (dg_cq, dg_ckv) + tuple(dhead)):
            _acc(ref, val, first)

    gain_specs = [_full((1, Q_LORA)), _full((1, KV_LORA))] + [_full((1, LANES))] * 4
    gain_shapes = [SDS((1, Q_LORA), F32), SDS((1, KV_LORA), F32)] + [SDS((1, LANES), F32)] * 4
    in_specs = _prep_in_specs(t) + [_rows(t, 2048), _rows(t, 2048), _rows(t, 1024), ANY]
    return _pcall(
        body, grid=(n // t,), in_specs=in_specs,
        out_specs=[_rows(t, wz, CQ // wz)] + gain_specs + [_full((Q_LORA, 2048)), _full((KV_LORA, 2048))],
        out_shape=[SDS((n, Z_COLS), BF)] + gain_shapes + [SDS((Q_LORA, 2048), F32), SDS((KV_LORA, 2048), F32)],
        sem=("arbitrary",), name=name, comm=comm,
        aliases={len(in_specs) - 1: 0})(z, z, z, cos_f, sin_s, *gains, wq, wkv, dq, dk, dv, dz)


MLA_QK = 256
MLA_SCALE = 1.0 / math.sqrt(MLA_NOPE + MLA_ROPE)
LOG2E = 1.0 / math.log(2.0)
MLA_SCALE_LOG2E = MLA_SCALE * LOG2E


def _causal_mask(s, q0, k0):
    tq, tk = s.shape
    row = q0 + lax.broadcasted_iota(jnp.int32, (tq, tk), 0)
    col = k0 + lax.broadcasted_iota(jnp.int32, (tq, tk), 1)
    return jnp.where(row >= col, s, -jnp.inf)


def _mla_fwd(q, k, v, batch, seq, name, comm=None):
    n = q.shape[0]
    tq = min(ATT_TILE, seq)
    nq = seq // tq

    nh = ATT_HEADS

    def body(q_ref, k_ref, v_ref, o_ref, lse_ref):
        i = pl.program_id(2)

        def step(j, carry, diagonal=False):
            k0 = pl.multiple_of(j * tq, tq)
            out = []
            ones = jnp.ones((tq, LANES), BF)
            for hh in range(nh):
                m, acc = carry[hh]
                qb = q_ref[:, hh * MLA_QK:(hh + 1) * MLA_QK]
                kb = k_ref[pl.ds(k0, tq), hh * MLA_QK:(hh + 1) * MLA_QK]
                vb = v_ref[pl.ds(k0, tq), hh * MLA_V:(hh + 1) * MLA_V]
                s = _dn(qb, kb, 1, 1)
                if diagonal:
                    s = _causal_mask(s, i * tq, k0)
                m_new = jnp.maximum(m, jnp.max(s, axis=-1, keepdims=True))
                p = jnp.exp2((s - m_new) * MLA_SCALE_LOG2E)
                alpha = jnp.exp2((m - m_new) * MLA_SCALE_LOG2E)
                acc = alpha * acc + _dn(p, jnp.concatenate([vb, ones], axis=1), 1, 0)
                out.append((m_new, acc))
            return tuple(out)

        init = tuple((jnp.full((tq, 1), -jnp.inf, F32), jnp.zeros((tq, MLA_V + LANES), F32)) for _ in range(nh))
        final = step(i, lax.fori_loop(0, i, step, init), diagonal=True)
        for hh, (m, acc) in enumerate(final):
            l = acc[:, MLA_V:MLA_V + 1]
            o_ref[:, hh * MLA_V:(hh + 1) * MLA_V] = acc[:, :MLA_V] / l
            lse_ref[:, hh * LANES:(hh + 1) * LANES] = jnp.broadcast_to(m * MLA_SCALE + jnp.log(l), (tq, LANES))

    return _pcall(
        body, grid=(batch, MLA_HEADS // nh, nq),
        in_specs=[pl.BlockSpec((tq, nh * MLA_QK), lambda b, h, i: (b * nq + i, h)),
                  pl.BlockSpec((seq, nh * MLA_QK), lambda b, h, i: (b, h)),
                  pl.BlockSpec((seq, nh * MLA_V), lambda b, h, i: (b, h))],
        out_specs=[pl.BlockSpec((tq, nh * MLA_V), lambda b, h, i: (b * nq + i, h)),
                   pl.BlockSpec((tq, nh * LANES), lambda b, h, i: (b * nq + i, h))],
        out_shape=[SDS((n, MLA_HEADS * MLA_V), F32), SDS((n, MLA_HEADS * LANES), F32)],
        sem=("parallel", "parallel", "arbitrary"), name=name, comm=comm)(q, k, v)


def _mla_bwd(q, k, v, o, lse, do, batch, seq, name, comm=None):
    n = q.shape[0]
    tk = min(ATT_TILE, seq)
    nk = seq // tk

    nh = ATT_HEADS

    def body(q_ref, k_ref, v_ref, o_ref, lse_ref, do_ref, dq_ref, dk_ref, dv_ref):
        jk = pl.program_id(2)

        @pl.when(jk == 0)
        def _():
            dq_ref[...] = jnp.zeros_like(dq_ref)

        def step(i, carry, diagonal=False):
            q0 = pl.multiple_of(i * tk, tk)
            rows = pl.ds(q0, tk)
            out = []
            for hh in range(nh):
                dk_acc, dv_acc = carry[hh]
                qk_cols = slice(hh * MLA_QK, (hh + 1) * MLA_QK)
                v_cols = slice(hh * MLA_V, (hh + 1) * MLA_V)
                kb = k_ref[:, qk_cols]
                vb = v_ref[:, v_cols]
                qb = q_ref[rows, qk_cols]
                dob = do_ref[rows, v_cols]
                delta = jnp.sum(dob * o_ref[rows, v_cols], axis=-1, keepdims=True)
                s = _dn(qb, kb, 1, 1)
                if diagonal:
                    s = _causal_mask(s, q0, jk * tk)
                p = jnp.exp2(s * MLA_SCALE_LOG2E - lse_ref[rows, hh * LANES:hh * LANES + 1] * LOG2E)
                dv_acc = dv_acc + _dn(p, dob, 0, 0)
                dp = _dn(dob, vb, 1, 1)
                ds = p * (dp - delta) * MLA_SCALE
                dk_acc = dk_acc + _dn(ds, qb, 0, 0)
                dq_ref[rows, qk_cols] += _dn(ds, kb, 1, 0)
                out.append((dk_acc, dv_acc))
            return tuple(out)

        init = tuple((jnp.zeros((tk, MLA_QK), F32), jnp.zeros((tk, MLA_V), F32)) for _ in range(nh))
        final = lax.fori_loop(jk + 1, nk, step, step(jk, init, diagonal=True))
        for hh, (dk_acc, dv_acc) in enumerate(final):
            dk_ref[:, hh * MLA_QK:(hh + 1) * MLA_QK] = dk_acc
            dv_ref[:, hh * MLA_V:(hh + 1) * MLA_V] = dv_acc

    full_qk = pl.BlockSpec((seq, nh * MLA_QK), lambda b, h, j: (b, h))
    full_v = pl.BlockSpec((seq, nh * MLA_V), lambda b, h, j: (b, h))
    blk_qk = pl.BlockSpec((tk, nh * MLA_QK), lambda b, h, j: (b * nk + j, h))
    blk_v = pl.BlockSpec((tk, nh * MLA_V), lambda b, h, j: (b * nk + j, h))
    return _pcall(
        body, grid=(batch, MLA_HEADS // nh, nk),
        in_specs=[full_qk, blk_qk, blk_v, full_v, full_v, full_v],
        out_specs=[full_qk, blk_qk, blk_v],
        out_shape=[SDS((n, MLA_HEADS * MLA_QK), F32), SDS((n, MLA_HEADS * MLA_QK), F32),
                   SDS((n, MLA_HEADS * MLA_V), F32)],
        sem=("parallel", "parallel", "arbitrary"), name=name, comm=comm)(q, k, v, o, lse, do)


MEM_SCALE = 1.0 / math.sqrt(HEAD_DIM)
MEM_W = MEM_HEADS * HEAD_DIM


def _mem_core(qs, ks, vs, g_mq, g_mk):
    outs = []
    for h in range(MEM_HEADS):
        qh = _rmsn(qs[h], g_mq, HEAD_DIM)
        kh = _rmsn(ks[h], g_mk, HEAD_DIM)
        p = _softmax(_mm_nt(qh, kh) * MEM_SCALE)
        outs.append(_mm_nn(p, vs[h]))
    return jnp.concatenate(outs, axis=1)


def _mem_load(qm, kvm, g_mq, g_mk):
    hs = range(MEM_HEADS)
    qs = [qm[:, h * LANES:(h + 1) * LANES].astype(F32) for h in hs]
    ks = [kvm[:, h * LANES:(h + 1) * LANES] for h in hs]
    vs = [kvm[:, MEM_W + h * LANES:MEM_W + (h + 1) * LANES] for h in hs]
    return qs, ks, vs, g_mq[...], g_mk[...]


def _mem_fwd(z, kvm, g_mq, g_mk, batch, seq, name, comm=None):
    n = z.shape[0]
    t = min(ROW_TILE, seq)
    per = seq // t

    def body(qm, kvm_ref, gq, gk, o_ref):
        o_ref[...] = _mem_core(*_mem_load(qm, kvm_ref, gq, gk)).astype(BF)

    return _pcall(
        body, grid=(n // t,),
        in_specs=[_rows(t, MEM_W, QM // MEM_W), pl.BlockSpec((MEM_LEN, 2 * MEM_W), lambda i: (i // per, 0)),
                  _full((1, LANES)), _full((1, LANES))],
        out_specs=_rows(t, MEM_W), out_shape=SDS((n, MEM_W), BF), sem=("parallel",), name=name,
        comm=comm)(z, kvm, g_mq, g_mk)


def _mem_bwd(z, kvm, g_mq, g_mk, dom, dz, batch, seq, name):
    n = z.shape[0]
    t = min(ROW_TILE, seq)
    per = seq // t

    def body(qm, kvm_ref, gq, gk, dom_ref, _, dz_ref, dkvm_ref, dgq_ref, dgk_ref):
        i = pl.program_id(0)
        _, vjp = jax.vjp(_mem_core, *_mem_load(qm, kvm_ref, gq, gk))
        dqs, dks, dvs, dgq, dgk = vjp(dom_ref[...])
        dz_ref[...] = jnp.concatenate(dqs, axis=1).astype(BF)
        _acc(dkvm_ref, jnp.concatenate(dks + dvs, axis=1), i % per == 0)
        _acc(dgq_ref, dgq, i == 0)
        _acc(dgk_ref, dgk, i == 0)

    kv_spec = pl.BlockSpec((MEM_LEN, 2 * MEM_W), lambda i: (i // per, 0))
    return pl.pallas_call(
        body, grid=(n // t,),
        in_specs=[_rows(t, MEM_W, QM // MEM_W), kv_spec, _full((1, LANES)), _full((1, LANES)), _rows(t, MEM_W), ANY],
        out_specs=[_rows(t, MEM_W, QM // MEM_W), kv_spec, _full((1, LANES)), _full((1, LANES))],
        out_shape=[SDS((n, Z_COLS), BF), SDS((batch * MEM_LEN, 2 * MEM_W), F32), SDS((1, LANES), F32),
                   SDS((1, LANES), F32)],
        input_output_aliases={5: 0},
        compiler_params=_params(("arbitrary",)), name=name)(z, kvm, g_mq, g_mk, dom, dz)


def _me():
    return lax.axis_index("x"), lax.axis_index("y"), lax.axis_index("c")


def _other_chips(x, y):
    return [(1 - x, y), (x, 1 - y), (1 - x, 1 - y)]


def _shard_shape(name):
    r, c = BIG_SHAPE[name]
    return (r, c // N_CHIPS) if name in COL_SHARDED else (r // N_CHIPS, c)


def _n_pieces(half_rows):
    return max(1, half_rows // PIECE_ROWS)


def _piece_plan(shapes):
    plan = []
    for r, _ in shapes:
        h = r // 2
        n = _n_pieces(h)
        plan.append((h, n, h // n))
    return plan


def _remote(send, recv, sem, src, dst, to):
    return pltpu.make_async_remote_copy(src_ref=src, dst_ref=dst, send_sem=send.at[sem], recv_sem=recv.at[sem],
                                        device_id=to, device_id_type=MESH)


def _gather_far(shards):
    plan = _piece_plan([s.shape for s in shards])
    n_far = 3 * sum(n for _, n, _ in plan)
    n_loc = 2 * sum(n for _, n, _ in plan)

    def copies(s_refs, o_refs, send, recv, local):
        x, y, c = _me()
        k = 2 * x + y
        mine, sends, arrivals = [], [], []
        for t, (h, n, pr) in enumerate(plan):
            s_ref, o_ref = s_refs[t], o_refs[t]
            for core in range(2):
                for p in range(n):
                    rows = pl.ds(core * h + p * pr, pr)
                    mine.append(pltpu.make_async_copy(s_ref.at[rows], o_ref.at[k, rows], local.at[len(mine)]))
            for chip in _other_chips(x, y):
                for p in range(n):
                    rows = pl.ds(c * h + p * pr, pr)
                    s = len(sends)
                    sends.append(_remote(send, recv, s, s_ref.at[rows], o_ref.at[k, rows], (*chip, c)))
                    arrivals.append(_remote(send, recv, s, s_ref.at[rows], o_ref.at[2 * chip[0] + chip[1], rows],
                                            (*chip, c)))
        return sends, arrivals, mine

    return _Phase(shards, [SDS((N_CHIPS,) + s.shape, s.dtype) for s in shards], n_far, n_loc, copies)


def _gather_near(bufs):
    plan = _piece_plan([b.shape[1:] for b in bufs])
    n_sem = 3 * sum(n for _, n, _ in plan)

    def copies(i_refs, o_refs, send, recv, local):
        x, y, c = _me()
        sib = (x, y, 1 - c)
        sends, arrivals = [], []
        for t, (h, n, pr) in enumerate(plan):
            for chip in _other_chips(x, y):
                ci = 2 * chip[0] + chip[1]
                for p in range(n):
                    rows = pl.ds(c * h + p * pr, pr)
                    rows_sib = pl.ds((1 - c) * h + p * pr, pr)
                    s = len(sends)
                    sends.append(_remote(send, recv, s, i_refs[t].at[ci, rows], o_refs[t].at[ci, rows], sib))
                    arrivals.append(_remote(send, recv, s, i_refs[t].at[ci, rows_sib], o_refs[t].at[ci, rows_sib], sib))
        return sends, arrivals, []

    return _Phase(bufs, [SDS(b.shape, b.dtype) for b in bufs], n_sem, 0, copies, {t: t for t in range(len(bufs))})


def _pair_exchange(grads):
    plan = _piece_plan([g.shape[1:] for g in grads])
    n_sem = sum(n for _, n, _ in plan)

    def copies(g_refs, o_refs, send, recv, local):
        x, y, c = _me()
        sends = []
        for t, (h, n, pr) in enumerate(plan):
            for p in range(n):
                sends.append(_remote(send, recv, len(sends), g_refs[t].at[:, pl.ds((1 - c) * h + p * pr, pr)],
                                     o_refs[t].at[:, pl.ds(p * pr, pr)], (x, y, 1 - c)))
        return sends, sends, []

    return _Phase(grads, [SDS((N_CHIPS, g.shape[1] // 2, g.shape[2]), F32) for g in grads], n_sem, 0, copies)


def _pair_add(ck, g, theirs, name):
    _, r, c = g.shape
    (h, n, pr), = _piece_plan([(r, c)])

    def body(ck_ref, g_ref, t_ref, p32_ref, pbf_ref):
        s = g_ref[...] + t_ref[...]
        p32_ref[...] = s
        pbf_ref[...] = s.astype(BF)

    half = pl.BlockSpec((None, pr, c), lambda k, p, ck: (k, p, 0))
    spec = pltpu.PrefetchScalarGridSpec(
        num_scalar_prefetch=1, grid=(N_CHIPS, n),
        in_specs=[pl.BlockSpec((None, pr, c), lambda k, p, ck: (k, ck[0] * n + p, 0)), half], out_specs=[half, half])
    return pl.pallas_call(body, grid_spec=spec, out_shape=[SDS((N_CHIPS, h, c), F32), SDS((N_CHIPS, h, c), BF)],
                          compiler_params=_params(("arbitrary", "arbitrary")), name=name)(ck, g, theirs)


def _scatter_partials(pbfs):
    plan = [(h, _n_pieces(h), h // _n_pieces(h)) for h in [p.shape[1] for p in pbfs]]
    n_sem = 3 * sum(n for _, n, _ in plan)

    def copies(p_refs, o_refs, send, recv, local):
        x, y, c = _me()
        sends = []
        for t, (h, n, pr) in enumerate(plan):
            for j, chip in enumerate(_other_chips(x, y)):
                for p in range(n):
                    rows = pl.ds(p * pr, pr)
                    sends.append(_remote(send, recv, len(sends), p_refs[t].at[2 * chip[0] + chip[1], rows],
                                         o_refs[t].at[j, rows], (*chip, c)))
        return sends, sends, []

    return _Phase(pbfs, [SDS((3,) + p.shape[1:], BF) for p in pbfs], n_sem, 0, copies)


def _sum_chips(ck, p32, slots, name):
    _, h, c = p32.shape
    n = _n_pieces(h)
    pr = h // n

    def body(ck_ref, p_ref, s_ref, o_ref):
        o_ref[...] = ((p_ref[...] + s_ref[0].astype(F32)) + s_ref[1].astype(F32)) + s_ref[2].astype(F32)

    spec = pltpu.PrefetchScalarGridSpec(
        num_scalar_prefetch=1, grid=(n,),
        in_specs=[pl.BlockSpec((None, pr, c), lambda p, ck: (ck[1], p, 0)),
                  pl.BlockSpec((3, pr, c), lambda p, ck: (0, p, 0))],
        out_specs=pl.BlockSpec((pr, c), lambda p, ck: (ck[0] * n + p, 0)))
    return pl.pallas_call(body, grid_spec=spec, out_shape=SDS((2 * h, c), F32),
                          compiler_params=_params(("arbitrary",)), name=name)(ck, p32, slots)


def _join_halves(sums):
    plan = _piece_plan([s.shape for s in sums])
    n_sem = sum(n for _, n, _ in plan)

    def copies(r_refs, o_refs, send, recv, local):
        x, y, c = _me()
        sends, arrivals = [], []
        for t, (h, n, pr) in enumerate(plan):
            for p in range(n):
                rows = pl.ds(c * h + p * pr, pr)
                rows_sib = pl.ds((1 - c) * h + p * pr, pr)
                s = len(sends)
                sends.append(_remote(send, recv, s, r_refs[t].at[rows], o_refs[t].at[rows], (x, y, 1 - c)))
                arrivals.append(_remote(send, recv, s, r_refs[t].at[rows_sib], o_refs[t].at[rows_sib], (x, y, 1 - c)))
        return sends, arrivals, []

    return _Phase(sums, [SDS(s.shape, F32) for s in sums], n_sem, 0, copies, {t: t for t in range(len(sums))})


def _gather_small(s, name):
    def body(s_ref, o_ref, send, recv, local):
        x, y, c = _me()
        me = 4 * x + 2 * y + c
        keep = pltpu.make_async_copy(s_ref, o_ref.at[me], local)
        keep.start()
        sends = []
        for r in range(1, 8):
            fx, fy, fc = (r >> 2) & 1, (r >> 1) & 1, r & 1
            to = (x ^ fx, y ^ fy, c ^ fc)
            sends.append(pltpu.make_async_remote_copy(
                src_ref=s_ref, dst_ref=o_ref.at[me], send_sem=send.at[r - 1], recv_sem=recv.at[r - 1],
                device_id=to, device_id_type=MESH))
        for cp in sends:
            cp.start()
        for r in range(1, 8):
            fx, fy, fc = (r >> 2) & 1, (r >> 1) & 1, r & 1
            src = 4 * (x ^ fx) + 2 * (y ^ fy) + (c ^ fc)
            pltpu.make_async_remote_copy(
                src_ref=s_ref, dst_ref=o_ref.at[src], send_sem=send.at[r - 1], recv_sem=recv.at[r - 1],
                device_id=(x ^ fx, y ^ fy, c ^ fc), device_id_type=MESH).wait_recv()
        for cp in sends:
            cp.wait_send()
        keep.wait()

    return pl.pallas_call(
        body, in_specs=[ANY], out_specs=ANY, out_shape=SDS((8, SMALL_ROWS, LANES), F32),
        scratch_shapes=[pltpu.SemaphoreType.DMA((7,)), pltpu.SemaphoreType.DMA((7,)), pltpu.SemaphoreType.DMA],
        name=name)(s)


def _adam_math(w, g, m, v):
    nm = ADAM_B1 * m + (1.0 - ADAM_B1) * g
    nv = ADAM_B2 * v + (1.0 - ADAM_B2) * (g * g)
    m_hat = nm / (1.0 - ADAM_B1 ** ADAM_STEP)
    v_hat = nv / (1.0 - ADAM_B2 ** ADAM_STEP)
    return -ADAM_LR * (m_hat / (jnp.sqrt(v_hat) + ADAM_EPS) + ADAM_WD * w), nm, nv


def _adamw(w, g, m, v, name):
    _, r, c = w.shape
    t = max(d for d in range(8, r + 1, 8) if r % d == 0 and 16 * d * c * 4 <= VMEM_LIMIT - (8 << 20))

    def body(w_ref, g_ref, m_ref, v_ref, go_ref, d_ref, nm_ref, nv_ref):
        g_ = g_ref[...]
        d, nm, nv = _adam_math(w_ref[...], g_, m_ref[...], v_ref[...])
        go_ref[...] = g_
        d_ref[...] = d
        nm_ref[...] = nm
        nv_ref[...] = nv

    lead = pl.BlockSpec((None, t, c), lambda i: (0, i, 0))
    return pl.pallas_call(body, grid=(r // t,), in_specs=[lead, _rows(t, c), lead, lead], out_specs=[lead] * 4,
                          out_shape=[SDS((1, r, c), F32)] * 4, compiler_params=_params(("parallel",)),
                          name=name)(w, g, m, v)


def _small_layout():
    out, r0 = {}, 0
    for n in SMALL:
        size = int(np.prod(SMALL_SHAPE[n]))
        nr = -(-size // LANES)
        out[n] = (r0, nr)
        r0 += nr
    assert r0 <= SMALL_ROWS
    return out, r0


def _pack_small(grads, loss_tile, name):
    layout, used = _small_layout()

    def body(*refs):
        o_ref = refs[-1]
        o_ref[used:used + 1, :] = refs[-2][0:1, :]
        for n, ref in zip(SMALL, refs[:-2]):
            r0, nr = layout[n]
            if n == "w_spatial":
                for g in range(GM_GROUPS):
                    o_ref[r0 + g * GM_CHUNK:r0 + (g + 1) * GM_CHUNK, :] = ref[g]
            elif n == "b_spatial":
                o_ref[r0:r0 + nr, :] = ref[...]
            else:
                for i in range(nr):
                    o_ref[r0 + i:r0 + i + 1, :] = ref[:, i * LANES:(i + 1) * LANES]
        if used + 1 < SMALL_ROWS:
            o_ref[used + 1:SMALL_ROWS, :] = jnp.zeros((SMALL_ROWS - used - 1, LANES), F32)

    return pl.pallas_call(body, out_shape=SDS((SMALL_ROWS, LANES), F32), name=name)(*grads, loss_tile)


def _adamw_small(gathered, ws, ms, vs, name):
    layout, used = _small_layout()
    n_t = len(SMALL)

    def body(*refs):
        g_ref = refs[0]
        w_refs, m_refs, v_refs = refs[1:1 + n_t], refs[1 + n_t:1 + 2 * n_t], refs[1 + 2 * n_t:1 + 3 * n_t]
        outs = refs[1 + 3 * n_t:1 + 7 * n_t]
        acc = refs[-1]
        total = g_ref[0]
        for j in range(1, 8):
            total = total + g_ref[j]
        acc[...] = total
        refs[1 + 7 * n_t][...] = acc[used:used + 1, :]
        for t, n in enumerate(SMALL):
            r0, nr = layout[n]
            o_refs = [outs[t], outs[n_t + t], outs[2 * n_t + t], outs[3 * n_t + t]]
            if n == "w_spatial":
                views = [((0, g), slice(r0 + g * GM_CHUNK, r0 + (g + 1) * GM_CHUNK), slice(None))
                         for g in range(GM_GROUPS)]
            elif n == "b_spatial":
                views = [((0,), slice(r0, r0 + nr), slice(None))]
            else:
                width = SMALL_SHAPE[n][1]
                views = [((slice(None), slice(i * LANES, min((i + 1) * LANES, width))), slice(r0 + i, r0 + i + 1),
                          slice(0, min(LANES, width - i * LANES))) for i in range(nr)]
            for idx, rows, lanes in views:
                g = acc[rows, lanes]
                d, nm, nv = _adam_math(w_refs[t][idx], g, m_refs[t][idx], v_refs[t][idx])
                for ref, val in zip(o_refs, (g, d, nm, nv)):
                    ref[idx] = val

    shapes = [SDS(SMALL_SHAPE[n], F32) for n in SMALL]
    return pl.pallas_call(body, out_shape=shapes * 4 + [SDS((1, LANES), F32)],
                          scratch_shapes=[pltpu.VMEM((SMALL_ROWS, LANES), F32)], name=name)(gathered, *ws, *ms, *vs)


def _win_layout(w_in):
    pad = jnp.zeros((w_in.shape[0], LANES - MLA_ROPE), w_in.dtype)
    u, v, cq = w_in[:, 0:512], w_in[:, 512:1024], w_in[:, 1024:1408]
    ckv, kpe, qm, zg = w_in[:, 1408:1664], w_in[:, 1664:1728], w_in[:, 1728:2240], w_in[:, 2240:5312]
    return jnp.concatenate([zg, u, v, qm, cq, kpe, pad, ckv], axis=1)


def _win_unlayout(g):
    zg, u, v, qm = g[:, ZG:ZG + 3072], g[:, ZU:ZU + 512], g[:, ZV:ZV + 512], g[:, QM:QM + 512]
    cq, kpe, ckv = g[:, CQ:CQ + 384], g[:, KPE:KPE + MLA_ROPE], g[:, CKV:CKV + 256]
    return jnp.concatenate([u, v, cq, ckv, kpe, qm, zg], axis=1)


def _wq_layout(w_uq):
    w = w_uq.reshape(Q_LORA, MLA_HEADS, MLA_NOPE + MLA_ROPE)
    nope = w[:, :, :MLA_NOPE].reshape(Q_LORA, MLA_HEADS * MLA_NOPE)
    pe = jnp.pad(w[:, :, MLA_NOPE:], ((0, 0), (0, 0), (0, LANES - MLA_ROPE))).reshape(Q_LORA, MLA_HEADS * LANES)
    return jnp.concatenate([nope, pe], axis=1)


def _wq_unlayout(g):
    nope = g[:, :1024].reshape(Q_LORA, MLA_HEADS, MLA_NOPE)
    pe = g[:, 1024:].reshape(Q_LORA, MLA_HEADS, LANES)[:, :, :MLA_ROPE]
    return jnp.concatenate([nope, pe], axis=2).reshape(Q_LORA, MLA_HEADS * (MLA_NOPE + MLA_ROPE))


def _wkv_layout(w_ukv):
    w = w_ukv.reshape(KV_LORA, MLA_HEADS, MLA_NOPE + MLA_V)
    return jnp.concatenate([w[:, :, :MLA_NOPE].reshape(KV_LORA, 1024), w[:, :, MLA_NOPE:].reshape(KV_LORA, 1024)],
                           axis=1)


def _wkv_unlayout(g):
    kn = g[:, :1024].reshape(KV_LORA, MLA_HEADS, MLA_NOPE)
    v = g[:, 1024:].reshape(KV_LORA, MLA_HEADS, MLA_V)
    return jnp.concatenate([kn, v], axis=2).reshape(KV_LORA, MLA_HEADS * (MLA_NOPE + MLA_V))


def _owner_major(g, name):
    r, c = _shard_shape(name)
    return g.reshape(r, N_CHIPS, c).transpose(1, 0, 2) if name in COL_SHARDED else g.reshape(N_CHIPS, r, c)


def _pad_lanes(g):
    return jnp.pad(g, ((0, 0), (0, LANES - g.shape[1])))


def kernel(x, mem, positions, g_mix, w_in, g_cq, w_uq, g_ckv, w_ukv, g_q_nope, g_q_pe, g_k_nope, g_k_pe, g_gm_ln, b_gm_ln, w_spatial, b_spatial, g_mem, w_mem_kv, g_mq, g_mk, w_o_gm, w_o_mla, w_o_mem, w_out, g_ffn, w_ff1, w_ff2, loss_target, m_g_mix, m_w_in, m_g_cq, m_w_uq, m_g_ckv, m_w_ukv, m_g_q_nope, m_g_q_pe, m_g_k_nope, m_g_k_pe, m_g_gm_ln, m_b_gm_ln, m_w_spatial, m_b_spatial, m_g_mem, m_w_mem_kv, m_g_mq, m_g_mk, m_w_o_gm, m_w_o_mla, m_w_o_mem, m_w_out, m_g_ffn, m_w_ff1, m_w_ff2, v_g_mix, v_w_in, v_g_cq, v_w_uq, v_g_ckv, v_w_ukv, v_g_q_nope, v_g_q_pe, v_g_k_nope, v_g_k_pe, v_g_gm_ln, v_b_gm_ln, v_w_spatial, v_b_spatial, v_g_mem, v_w_mem_kv, v_g_mq, v_g_mk, v_w_o_gm, v_w_o_mla, v_w_o_mem, v_w_out, v_g_ffn, v_w_ff1, v_w_ff2):
    given = dict(locals())
    wts = {n: given[n] for n in WEIGHTS}
    mom = {n: given["m_" + n] for n in WEIGHTS}
    var = {n: given["v_" + n] for n in WEIGHTS}
    batch, seq, _ = x.shape
    n_tok = batch * seq

    def natural(n, g):
        r, c = _shard_shape(n)
        return g.transpose(1, 0, 2).reshape(r, N_CHIPS * c) if n in COL_SHARDED else g.reshape(N_CHIPS * r, c)

    def far(names):
        return _gather_far([wts[n][0].astype(BF) for n in names])

    x2 = x.reshape(n_tok, D_MODEL)
    tgt2 = loss_target.reshape(n_tok, D_MODEL)
    mem2 = mem.reshape(batch * MEM_LEN, D_MODEL)
    pos_f = positions.reshape(n_tok, 1).astype(F32)

    inv = ROPE_BASE ** (-jnp.arange(0, MLA_ROPE, 2, dtype=F32) / MLA_ROPE)
    zeros64 = jnp.zeros((LANES - MLA_ROPE,), F32)
    inv_full = jnp.concatenate([inv, inv, zeros64]).reshape(1, LANES)
    half = MLA_ROPE // 2
    cmask = jnp.concatenate([jnp.ones((MLA_ROPE,), F32), zeros64]).reshape(1, LANES)
    smask = jnp.concatenate([-jnp.ones((half,), F32), jnp.ones((half,), F32), zeros64]).reshape(1, LANES)

    prep_gains = [g_cq, g_ckv, g_q_nope, _pad_lanes(g_q_pe), g_k_nope, _pad_lanes(g_k_pe)]
    ws = w_spatial[0]
    bcols = [b_spatial[0, g].reshape(GM_CHUNK, 1) for g in range(GM_GROUPS)]

    h1, early_far = _rms_fwd(x2, g_mix, "rms_mix", comm=far(EARLY))
    (cos_f, sin_s), early = _rope_tables(pos_f, inv_full, cmask, smask, "rope_tables", comm=_gather_near(early_far))
    full = {n: natural(n, g) for n, g in zip(EARLY, early)}
    win = _win_layout(full["w_in"])
    wq = _wq_layout(full["w_uq"])
    wkv = _wkv_layout(full["w_ukv"])
    z, proj_far = _mm(h1, win, out_dtypes=(BF,), name="mm_in", comm=far(LATE_PROJ))
    gm = _gm_fwd(z, g_gm_ln, b_gm_ln, ws, bcols, "gm_fwd")
    qc, kc, vc = _prep_fwd(z, cos_f, sin_s, prep_gains, wq, wkv, "prep_fwd")
    (o_mla, lse), ff_far = _mla_fwd(qc, kc, vc, batch, seq, "mla_fwd", comm=far(LATE_FF))
    memn = _rms_fwd(mem2, g_mem, "rms_mem")
    kvm, proj = _mm(memn, full["w_mem_kv"], name="mm_memkv", comm=_gather_near(proj_far))
    o_mem, ff = _mem_fwd(z, kvm, g_mq, g_mk, batch, seq, "mem_fwd", comm=_gather_near(ff_far))
    full.update({n: natural(n, g) for n, g in zip(LATE_PROJ + LATE_FF, list(proj) + list(ff))})
    y_gm = _mm(gm, full["w_o_gm"], out_dtypes=(BF,), name="mm_o_gm")
    y_mla = _mm(o_mla, full["w_o_mla"], out_dtypes=(BF,), name="mm_o_mla")
    y_mem = _mm(o_mem, full["w_o_mem"], out_dtypes=(BF,), name="mm_o_mem")
    merged = _merge_fwd(z, y_gm, y_mla, y_mem, "merge_fwd")
    x1 = _mm(merged, full["w_out"], ins=(x2,), epilogue=_add_to, name="mm_out")
    h2 = _rms_fwd(x1, g_ffn, "rms_ffn")
    a_ff, r_ff = _mm(h2, full["w_ff1"], epilogue=_relu2, out_dtypes=(BF, BF), name="mm_ff1")
    dy, dyb, loss_tile = _mm(r_ff, full["w_ff2"], ins=(x1, tgt2), epilogue=_loss_tail, out_dtypes=(F32, BF),
                             total=True, name="mm_ff2")

    gw = {}
    da = _mm(dyb, full["w_ff2"], tb=True, ins=(a_ff,), epilogue=_relu2_bwd, out_dtypes=(BF,), name="mm_d_a")
    gw["w_ff2"] = _owner_major(_mm(r_ff, dyb, ta=True, name="mm_dw_ff2"), "w_ff2")
    gw["w_ff1"] = _mm(h2, da, ta=True, owner_cols=D_FF // N_CHIPS, name="mm_dw_ff1")
    dh2 = _mm(da, full["w_ff1"], tb=True, name="mm_d_h2")
    dx1, dx1b, dg_ffn = _rms_bwd(x1, g_ffn, dh2, dy, "rms_ffn_bwd")
    dmerged = _mm(dx1b, full["w_out"], tb=True, name="mm_d_merged")
    gw["w_out"] = _owner_major(_mm(merged, dx1b, ta=True, name="mm_dw_out"), "w_out")
    dz, dy_gm, dy_mla, dy_mem = _merge_bwd(z, y_gm, y_mla, y_mem, dmerged, "merge_bwd")
    dgm = _mm(dy_gm, full["w_o_gm"], tb=True, name="mm_d_gm")
    gw["w_o_gm"] = _mm(gm, dy_gm, ta=True, owner_cols=D_MODEL // N_CHIPS, name="mm_dw_o_gm")
    do_mla = _mm(dy_mla, full["w_o_mla"], tb=True, name="mm_d_omla")
    gw["w_o_mla"] = _owner_major(_mm(o_mla, dy_mla, ta=True, name="mm_dw_o_mla"), "w_o_mla")
    do_mem = _mm(dy_mem, full["w_o_mem"], tb=True, name="mm_d_omem")
    gw["w_o_mem"] = _mm(o_mem, dy_mem, ta=True, owner_cols=D_MODEL // N_CHIPS, name="mm_dw_o_mem")
    ck = jnp.stack([lax.axis_index("c"), 2 * lax.axis_index("x") + lax.axis_index("y")]).astype(jnp.int32)

    def pair_sums(names, theirs):
        return [_pair_add(ck, gw[n], t, "pair_add_" + n) for n, t in zip(names, theirs)]

    def chip_sums(names, pairs, slots):
        return [_sum_chips(ck, p[0], s, "sum_chips_" + n) for n, p, s in zip(names, pairs, slots)]

    (dz, dg_ln, db_ln, dws, *dbcols), theirs = _gm_bwd(z, g_gm_ln, b_gm_ln, ws, bcols, dgm, dz, "gm_bwd",
                                                      comm=_pair_exchange([gw[n] for n in LATE]))
    pairs = pair_sums(LATE, theirs)
    (dq, dk, dv), slots = _mla_bwd(qc, kc, vc, o_mla, lse, do_mla, batch, seq, "mla_bwd",
                                   comm=_scatter_partials([p[1] for p in pairs]))
    sums = chip_sums(LATE, pairs, slots)
    (dz, dg_cq, dg_ckv, dg_qn, dg_qp, dg_kn, dg_kp, dwq, dwkv), reduced_late = _prep_bwd(
        z, cos_f, sin_s, prep_gains, wq, wkv, dq, dk, dv, dz, "prep_bwd", comm=_join_halves(sums))
    dz, dkvm, dg_mq, dg_mk = _mem_bwd(z, kvm, g_mq, g_mk, do_mem, dz, batch, seq, "mem_bwd")
    dmemn = _mm(dkvm, full["w_mem_kv"], tb=True, name="mm_d_memn")
    gw["w_mem_kv"] = _owner_major(_mm(memn, dkvm, ta=True, name="mm_dw_memkv"), "w_mem_kv")
    _, _, dg_mem = _rms_bwd(mem2, g_mem, dmemn, None, "rms_mem_bwd")
    gw["w_in"] = _owner_major(_win_unlayout(_mm(h1, dz, ta=True, name="mm_dw_in")), "w_in")
    gw["w_uq"] = _owner_major(_wq_unlayout(dwq), "w_uq")
    gw["w_ukv"] = _owner_major(_wkv_unlayout(dwkv), "w_ukv")
    dh1, theirs = _mm(dz, win, tb=True, name="mm_d_h1_top", rows=(0, 2), comm=_pair_exchange([gw[n] for n in EARLY]))
    pairs = pair_sums(EARLY, theirs)
    dh1, slots = _mm(dz, win, tb=True, name="mm_d_h1_bottom", rows=(1, 2), into=dh1,
                     comm=_scatter_partials([p[1] for p in pairs]))
    grad_x, _, dg_mix = _rms_bwd(x2, g_mix, dh1, dx1, "rms_mix_bwd")
    reduced_early = _run_phase(_join_halves(chip_sums(EARLY, pairs, slots)), "join_early")
    reduced = dict(zip(LATE + EARLY, list(reduced_late) + list(reduced_early)))

    def swapped(a):
        return jnp.swapaxes(a, -1, -2)

    results = {n: _adamw(wts[n], reduced[n], mom[n], var[n], "adamw_" + n) for n in BIG if n != "w_in"}
    results["w_in"] = [swapped(r) for r in _adamw(swapped(w_in), swapped(reduced["w_in"]), swapped(m_w_in),
                                                  swapped(v_w_in), "adamw_w_in")]

    small_g = {"g_mix": dg_mix, "g_cq": dg_cq, "g_ckv": dg_ckv, "g_q_nope": dg_qn, "g_q_pe": dg_qp,
               "g_k_nope": dg_kn, "g_k_pe": dg_kp, "g_gm_ln": dg_ln, "b_gm_ln": db_ln, "w_spatial": dws,
               "b_spatial": jnp.concatenate(dbcols, axis=1).T, "g_mem": dg_mem, "g_mq": dg_mq, "g_mk": dg_mk,
               "g_ffn": dg_ffn}
    packed = _pack_small([small_g[n] for n in SMALL], loss_tile, "pack_small")
    small_out = _adamw_small(_gather_small(packed, "gather_small"), [wts[n] for n in SMALL],
                             [mom[n] for n in SMALL], [var[n] for n in SMALL], "adamw_small")
    for t, n in enumerate(SMALL):
        results[n] = [small_out[j * len(SMALL) + t] for j in range(4)]

    loss = small_out[4 * len(SMALL)][0, 0]
    grad_x = grad_x.reshape(batch, seq, D_MODEL)
    return (loss, grad_x, *[results[n][0] for n in WEIGHTS], *[results[n][1] for n in WEIGHTS],
            *[results[n][2] for n in WEIGHTS], *[results[n][3] for n in WEIGHTS])
```

```python
import functools
import math

import numpy as np
import jax
import jax.numpy as jnp
from jax import lax
from jax.experimental import pallas as pl
from jax.experimental.pallas import tpu as pltpu

F32 = jnp.float32
BF = jnp.bfloat16
SDS = jax.ShapeDtypeStruct
MESH = pl.DeviceIdType.MESH

D_MODEL = 1024
MEM_LEN = 256
MEM_HEADS = 4
HEAD_DIM = 128
GM_WIDTH = 512
GM_CHUNK = 128
GM_GROUPS = 4
MLA_HEADS = 8
MLA_NOPE = 128
MLA_ROPE = 64
MLA_V = 128
Q_LORA = 384
KV_LORA = 256
ROPE_BASE = 10000.0
D_FF = 4096
EPS = 1e-6
W_IN_COLS = 5312
ADAM_LR, ADAM_B1, ADAM_B2, ADAM_EPS, ADAM_WD, ADAM_STEP = 0.001, 0.9, 0.999, 1e-08, 0.01, 10

ZG, ZU, ZV, QM, CQ, KPE, CKV = 0, 3072, 3584, 4096, 4608, 4992, 5120
Z_COLS = 5376
LANES = 128
ROW_TILE = 512
ATT_TILE = 1024
ATT_HEADS = 2
VMEM_LIMIT = 60 * 1024 * 1024

N_CHIPS = 4
PIECE_ROWS = 256
SMALL_ROWS = 560

BIG = ["w_in", "w_uq", "w_ukv", "w_mem_kv", "w_o_gm", "w_o_mla", "w_o_mem", "w_out", "w_ff1", "w_ff2"]
BIG_SHAPE = {"w_in": (1024, 5312), "w_uq": (384, 1536), "w_ukv": (256, 2048), "w_mem_kv": (1024, 1024),
             "w_o_gm": (512, 1024), "w_o_mla": (1024, 1024), "w_o_mem": (512, 1024), "w_out": (1024, 1024),
             "w_ff1": (1024, 4096), "w_ff2": (4096, 1024)}
COL_SHARDED = {"w_in", "w_uq", "w_ukv", "w_o_gm", "w_o_mem", "w_ff1"}
EARLY = ["w_in", "w_uq", "w_ukv", "w_mem_kv"]
LATE_PROJ = ["w_o_gm", "w_o_mla", "w_o_mem", "w_out"]
LATE_FF = ["w_ff1", "w_ff2"]
LATE = LATE_PROJ + LATE_FF
SMALL = ["w_spatial", "b_spatial", "g_mix", "g_cq", "g_ckv", "g_q_nope", "g_q_pe", "g_k_nope", "g_k_pe", "g_gm_ln",
         "b_gm_ln", "g_mem", "g_mq", "g_mk", "g_ffn"]
SMALL_SHAPE = {"g_mix": (1, 1024), "g_cq": (1, 384), "g_ckv": (1, 256), "g_q_nope": (1, 128), "g_q_pe": (1, 64),
               "g_k_nope": (1, 128), "g_k_pe": (1, 64), "g_gm_ln": (1, 512), "b_gm_ln": (1, 512),
               "w_spatial": (1, 4, 128, 128), "b_spatial": (1, 4, 128), "g_mem": (1, 1024), "g_mq": (1, 128),
               "g_mk": (1, 128), "g_ffn": (1, 1024)}
WEIGHTS = ['g_mix', 'w_in', 'g_cq', 'w_uq', 'g_ckv', 'w_ukv', 'g_q_nope', 'g_q_pe', 'g_k_nope', 'g_k_pe',
           'g_gm_ln', 'b_gm_ln', 'w_spatial', 'b_spatial', 'g_mem', 'w_mem_kv', 'g_mq', 'g_mk', 'w_o_gm',
           'w_o_mla', 'w_o_mem', 'w_out', 'g_ffn', 'w_ff1', 'w_ff2']


def _params(sem=None):
    return pltpu.CompilerParams(vmem_limit_bytes=VMEM_LIMIT, dimension_semantics=sem)


def _pick(n, prefs):
    for p in prefs:
        if n % p == 0:
            return p
    return n


def _full(shape):
    nd = len(shape)
    return pl.BlockSpec(shape, lambda *_: (0,) * nd)


def _rows(t, w, blk=0):
    return pl.BlockSpec((t, w), lambda i: (i, blk))


def _acc(ref, val, first):
    @pl.when(first)
    def _():
        ref[...] = val

    @pl.when(jnp.logical_not(first))
    def _():
        ref[...] += val


ANY = pl.BlockSpec(memory_space=pl.ANY)


class _Phase:
    def __init__(self, operands, out_shapes, n_sem, n_local, copies, aliases=None):
        self.operands, self.out_shapes, self.aliases = list(operands), list(out_shapes), dict(aliases or {})
        self.n_sem, self.n_local, self.copies = n_sem, max(n_local, 1), copies

    def sem_shapes(self):
        return [pltpu.SemaphoreType.DMA((self.n_sem,)), pltpu.SemaphoreType.DMA((self.n_sem,)),
                pltpu.SemaphoreType.DMA((self.n_local,))]

    def start(self, ins, outs, send, recv, local):
        sends, _, locals_ = self.copies(ins, outs, send, recv, local)
        for cp in locals_ + sends:
            cp.start()

    def finish(self, ins, outs, send, recv, local):
        sends, arrivals, locals_ = self.copies(ins, outs, send, recv, local)
        for cp in arrivals:
            cp.wait_recv()
        for cp in sends:
            cp.wait_send()
        for cp in locals_:
            cp.wait()


def _run_phase(phase, name):
    n_in = len(phase.operands)

    def body(*refs):
        ins, outs, sems = refs[:n_in], refs[n_in:n_in + len(phase.out_shapes)], refs[n_in + len(phase.out_shapes):]
        phase.start(ins, outs, *sems)
        phase.finish(ins, outs, *sems)

    return pl.pallas_call(body, in_specs=[ANY] * n_in, out_specs=[ANY] * len(phase.out_shapes),
                          out_shape=phase.out_shapes, scratch_shapes=phase.sem_shapes(),
                          input_output_aliases=phase.aliases, name=name)(*phase.operands)


def _pcall(body, *, grid, in_specs, out_specs, out_shape, scratch_shapes=(), sem=None, name, comm=None, aliases=None):
    single = not isinstance(out_shape, (list, tuple))
    o_specs = [out_specs] if single else list(out_specs)
    o_shape = [out_shape] if single else list(out_shape)
    aliases = dict(aliases or {})
    if comm is None:
        call = pl.pallas_call(body, grid=grid, in_specs=list(in_specs), out_specs=o_specs, out_shape=o_shape,
                              scratch_shapes=list(scratch_shapes), input_output_aliases=aliases,
                              compiler_params=_params(sem), name=name)

        def run_plain(*args):
            res = call(*args)
            return res[0] if single else res

        return run_plain

    n_in, n_out, n_scr = len(in_specs), len(o_specs), len(scratch_shapes)
    nc_in, nc_out = len(comm.operands), len(comm.out_shapes)

    def wrapped(*refs):
        ins, cins = refs[:n_in], refs[n_in:n_in + nc_in]
        o0 = n_in + nc_in
        outs, couts = refs[o0:o0 + n_out], refs[o0 + n_out:o0 + n_out + nc_out]
        s0 = o0 + n_out + nc_out
        scr, csem = refs[s0:s0 + n_scr], refs[s0 + n_scr:]
        ids = [pl.program_id(d) for d in range(len(grid))]
        first = functools.reduce(jnp.logical_and, [i == 0 for i in ids])
        last = functools.reduce(jnp.logical_and, [i == g - 1 for i, g in zip(ids, grid)])

        @pl.when(first)
        def _():
            comm.start(cins, couts, *csem)

        body(*ins, *outs, *scr)

        @pl.when(last)
        def _():
            comm.finish(cins, couts, *csem)

    call = pl.pallas_call(
        wrapped, grid=grid, in_specs=list(in_specs) + [ANY] * nc_in, out_specs=o_specs + [ANY] * nc_out,
        out_shape=o_shape + comm.out_shapes, scratch_shapes=list(scratch_shapes) + comm.sem_shapes(),
        input_output_aliases={**aliases, **{n_in + i: n_out + j for i, j in comm.aliases.items()}},
        compiler_params=_params(("arbitrary",) * len(grid)), name=name)

    def run_carrying(*args):
        res = call(*args, *comm.operands)
        return (res[0] if single else res[:n_out]), res[n_out:]

    return run_carrying


def _dn(a, b, ca, cb):
    return lax.dot_general(a.astype(BF), b.astype(BF), (((ca,), (cb,)), ((), ())), preferred_element_type=F32)


@jax.custom_vjp
def _mm_nn(a, b):
    return _dn(a, b, 1, 0)


def _mm_nn_fwd(a, b):
    return _dn(a, b, 1, 0), (a.astype(BF), b.astype(BF))


def _mm_nn_bwd(res, ct):
    a, b = res
    return _dn(ct, b, 1, 1), _dn(a, ct, 0, 0)


_mm_nn.defvjp(_mm_nn_fwd, _mm_nn_bwd)


@jax.custom_vjp
def _mm_nt(a, b):
    return _dn(a, b, 1, 1)


def _mm_nt_fwd(a, b):
    return _dn(a, b, 1, 1), (a.astype(BF), b.astype(BF))


def _mm_nt_bwd(res, ct):
    a, b = res
    return _dn(ct, b, 1, 0), _dn(ct, a, 0, 0)


_mm_nt.defvjp(_mm_nt_fwd, _mm_nt_bwd)


def _rmsn(x, g, n):
    ms = jnp.sum(x * x, axis=-1, keepdims=True) * (1.0 / n)
    return x * lax.rsqrt(ms + EPS) * g


def _layernorm(x, g, b):
    mu = jnp.mean(x, axis=-1, keepdims=True)
    xc = x - mu
    y = xc * lax.rsqrt(jnp.mean(xc * xc, axis=-1, keepdims=True) + EPS)
    return y * g + b


def _swap_lanes(x):
    half = MLA_ROPE // 2
    lane = lax.broadcasted_iota(jnp.int32, x.shape, 1)
    return jnp.where(lane < half, pltpu.roll(x, LANES - half, axis=1),
                     jnp.where(lane < MLA_ROPE, pltpu.roll(x, half, axis=1), 0.0))


@jax.custom_vjp
def _swap_halves(x):
    return _swap_lanes(x)


_swap_halves.defvjp(lambda x: (_swap_lanes(x), None), lambda _, ct: (_swap_lanes(ct),))


def _rope(x, cos_f, sin_s):
    return x * cos_f + _swap_halves(x) * sin_s


def _lane_blocks(x):
    return tuple(x[:, i * LANES:(i + 1) * LANES] for i in range(x.shape[1] // LANES))


@jax.custom_vjp
def _split_lanes(x):
    return _lane_blocks(x)


_split_lanes.defvjp(lambda x: (_lane_blocks(x), None), lambda _, cts: (jnp.concatenate(cts, axis=1),))


def _softmax(s):
    m = lax.stop_gradient(jnp.max(s, axis=-1, keepdims=True))
    p = jnp.exp(s - m)
    return p / jnp.sum(p, axis=-1, keepdims=True)


def _mm(a, b, *, ta=False, tb=False, ins=(), epilogue=None, out_dtypes=(F32,), owner_cols=None, total=False, name,
        comm=None, rows=None, into=None):
    if ta:
        k_dim, m = a.shape
    else:
        m, k_dim = a.shape
    if tb:
        n, kb = b.shape
    else:
        kb, n = b.shape
    assert k_dim == kb, (a.shape, b.shape, ta, tb)
    part, n_parts = rows if rows is not None else (0, 1)
    tm = _pick(m // n_parts, (1024, 512, 256, 128))
    tn = _pick(n if owner_cols is None else owner_cols, (1024, 768, 512, 384, 256, 128))
    tk = _pick(k_dim, (2048, 1024, 768, 512, 256, 128))
    nk = k_dim // tk
    m_steps = m // tm // n_parts
    off = part * m_steps
    ca = 0 if ta else 1
    cb = 1 if tb else 0
    n_in = len(ins)
    n_out = len(out_dtypes)
    n_pass = 0 if into is None else 1

    def finish(r, in_refs, out_refs, first_tile):
        vals = epilogue(r, *[ref[...].astype(F32) for ref in in_refs]) if epilogue is not None else (r,)
        for ref, val, dt in zip(out_refs, vals, out_dtypes):
            ref[...] = val.astype(dt)
        if total:
            _acc(out_refs[n_out], vals[n_out], first_tile)

    def body(*refs):
        a_ref, b_ref = refs[:2]
        in_refs = refs[2:2 + n_in]
        o0 = 2 + n_in + n_pass
        out_refs = refs[o0:o0 + n_out + int(total)]
        first_tile = jnp.logical_and(pl.program_id(0) == 0, pl.program_id(1) == 0)
        part = _dn(a_ref[...], b_ref[...], ca, cb)
        if nk == 1:
            finish(part, in_refs, out_refs, first_tile)
            return
        acc = refs[-1]
        k = pl.program_id(2)
        _acc(acc, part, k == 0)

        @pl.when(k == nk - 1)
        def _():
            finish(acc[...], in_refs, out_refs, first_tile)

    a_spec = (pl.BlockSpec((tk, tm), lambda i, j, k: (k, i + off)) if ta
              else pl.BlockSpec((tm, tk), lambda i, j, k: (i + off, k)))
    b_spec = pl.BlockSpec((tn, tk), lambda i, j, k: (j, k)) if tb else pl.BlockSpec((tk, tn), lambda i, j, k: (k, j))
    t_spec = pl.BlockSpec((tm, tn), lambda i, j, k: (i + off, j))
    if owner_cols is None:
        o_spec, o_shape = t_spec, (m, n)
    else:
        per = owner_cols // tn
        o_spec = pl.BlockSpec((None, tm, tn), lambda i, j, k: (j // per, i + off, j % per))
        o_shape = (n // owner_cols, m, owner_cols)
    o_specs = [o_spec] * n_out + ([pl.BlockSpec((8, LANES), lambda i, j, k: (0, 0))] if total else [])
    o_shapes = [SDS(o_shape, dt) for dt in out_dtypes] + ([SDS((8, LANES), F32)] if total else [])
    in_specs = [a_spec, b_spec] + [t_spec] * n_in + [ANY] * n_pass
    args = [a, b, *ins] + ([into] if n_pass else [])
    run = _pcall(body, grid=(m_steps, n // tn, nk), in_specs=in_specs, out_specs=o_specs, out_shape=o_shapes,
                 scratch_shapes=[pltpu.VMEM((tm, tn), F32)] if nk > 1 else [],
                 sem=("arbitrary",) * 3 if total else ("parallel", "parallel", "arbitrary"), name=name, comm=comm,
                 aliases={len(in_specs) - 1: 0} if n_pass else None)
    if comm is None:
        outs = run(*args)
        return outs[0] if len(outs) == 1 else outs
    outs, exchanged = run(*args)
    return (outs[0] if len(outs) == 1 else outs), exchanged


def _add_to(r, x):
    return (r + x,)


def _relu2(r):
    p = jnp.maximum(r, 0.0)
    return r, p * p


def _relu2_bwd(dr, a):
    return (dr * (2.0 * jnp.maximum(a, 0.0)),)


def _loss_tail(r, x1, tgt):
    e = (r + x1) - tgt
    dy = e * (1.0 / D_MODEL)
    part = jnp.sum(jnp.sum(e * e, axis=-1, keepdims=True), axis=0, keepdims=True) * (0.5 / D_MODEL)
    return dy, dy, jnp.broadcast_to(part, (8, LANES))


def _rms_fwd(x, g, name, comm=None):
    n, w = x.shape
    t = min(ROW_TILE, n)

    def body(x_ref, g_ref, o_ref):
        o_ref[...] = _rmsn(x_ref[...], g_ref[...], w).astype(BF)

    return _pcall(body, grid=(n // t,), in_specs=[_rows(t, w), _full((1, w))], out_specs=_rows(t, w),
                  out_shape=SDS((n, w), BF), sem=("arbitrary",), name=name, comm=comm)(x, g)


def _rms_bwd(x, g, dh, res, name, comm=None):
    n, w = x.shape
    t = min(ROW_TILE, n)
    has_res = res is not None

    def body(*refs):
        if has_res:
            x_ref, g_ref, dh_ref, res_ref, dx_ref, dxb_ref, dg_ref = refs
        else:
            x_ref, g_ref, dh_ref, dx_ref, dxb_ref, dg_ref = refs
        _, vjp = jax.vjp(lambda xx, gg: _rmsn(xx, gg, w), x_ref[...], g_ref[...])
        dx, dg = vjp(dh_ref[...])
        if has_res:
            dx = dx + res_ref[...]
        dx_ref[...] = dx
        dxb_ref[...] = dx.astype(BF)
        _acc(dg_ref, dg, pl.program_id(0) == 0)

    in_specs = [_rows(t, w), _full((1, w)), _rows(t, w)] + ([_rows(t, w)] if has_res else [])
    args = [x, g, dh] + ([res] if has_res else [])
    return _pcall(body, grid=(n // t,), in_specs=in_specs, out_specs=[_rows(t, w), _rows(t, w), _full((1, w))],
                  out_shape=[SDS((n, w), F32), SDS((n, w), BF), SDS((1, w), F32)], sem=("arbitrary",), name=name,
                  comm=comm)(*args)


def _merge_core(zg0, zg1, zg2, y0, y1, y2):
    return jax.nn.sigmoid(zg0) * y0 + jax.nn.sigmoid(zg1) * y1 + jax.nn.sigmoid(zg2) * y2


def _merge_fwd(z, y_gm, y_mla, y_mem, name):
    n = z.shape[0]
    t = min(ROW_TILE, n)
    w = D_MODEL

    def body(g0, g1, g2, y0, y1, y2, o_ref):
        o_ref[...] = _merge_core(g0[...].astype(F32), g1[...].astype(F32), g2[...].astype(F32), y0[...].astype(F32), y1[...].astype(F32),
                                 y2[...].astype(F32)).astype(BF)

    return pl.pallas_call(body, grid=(n // t,),
                          in_specs=[_rows(t, w, 0), _rows(t, w, 1), _rows(t, w, 2)] + [_rows(t, w)] * 3,
                          out_specs=_rows(t, w), out_shape=SDS((n, w), BF),
                          compiler_params=_params(("parallel",)), name=name)(z, z, z, y_gm, y_mla, y_mem)


def _merge_bwd(z, y_gm, y_mla, y_mem, dmerged, name):
    n = z.shape[0]
    t = min(ROW_TILE, n)
    w = D_MODEL

    def body(g0, g1, g2, y0, y1, y2, dm, dzg_ref, d0_ref, d1_ref, d2_ref):
        _, vjp = jax.vjp(_merge_core, g0[...].astype(F32), g1[...].astype(F32), g2[...].astype(F32), y0[...].astype(F32), y1[...].astype(F32),
                         y2[...].astype(F32))
        dg0, dg1, dg2, dy0, dy1, dy2 = vjp(dm[...])
        dzg_ref[:, 0:w] = dg0.astype(BF)
        dzg_ref[:, w:2 * w] = dg1.astype(BF)
        dzg_ref[:, 2 * w:3 * w] = dg2.astype(BF)
        d0_ref[...] = dy0.astype(BF)
        d1_ref[...] = dy1.astype(BF)
        d2_ref[...] = dy2.astype(BF)

    return pl.pallas_call(body, grid=(n // t,),
                          in_specs=[_rows(t, w, 0), _rows(t, w, 1), _rows(t, w, 2)] + [_rows(t, w)] * 4,
                          out_specs=[_rows(t, 3 * w, ZG // (3 * w))] + [_rows(t, w)] * 3,
                          out_shape=[SDS((n, Z_COLS), BF)] + [SDS((n, w), BF)] * 3,
                          compiler_params=_params(("parallel",)), name=name)(z, z, z, y_gm, y_mla, y_mem, dmerged)


def _gm_core(zu, zv, g_ln, b_ln, ws, bcols):
    t = zu.shape[0]
    u = jax.nn.gelu(zu)
    v = _layernorm(jax.nn.gelu(zv), g_ln, b_ln)
    row = lax.broadcasted_iota(jnp.int32, (GM_CHUNK, GM_CHUNK), 0)
    col = lax.broadcasted_iota(jnp.int32, (GM_CHUNK, GM_CHUNK), 1)
    wc = [jnp.where(row >= col, ws[g], 0.0) for g in range(GM_GROUPS)]
    chunks = []
    for c in range(t // GM_CHUNK):
        cols = []
        for g in range(GM_GROUPS):
            vc = v[c * GM_CHUNK:(c + 1) * GM_CHUNK, g * LANES:(g + 1) * LANES]
            cols.append(_mm_nn(wc[g], vc) + bcols[g])
        chunks.append(jnp.concatenate(cols, axis=1))
    mixed = chunks[0] if len(chunks) == 1 else jnp.concatenate(chunks, axis=0)
    return u * mixed


def _gm_specs(t):
    return [_rows(t, GM_WIDTH, ZU // GM_WIDTH), _rows(t, GM_WIDTH, ZV // GM_WIDTH), _full((1, GM_WIDTH)),
            _full((1, GM_WIDTH)), _full((GM_GROUPS, GM_CHUNK, GM_CHUNK))] + [_full((GM_CHUNK, 1))] * GM_GROUPS


def _gm_fwd(z, g_ln, b_ln, ws, bcols, name):
    n = z.shape[0]
    t = min(ROW_TILE, n)

    def body(zu, zv, g_ref, b_ref, ws_ref, c0, c1, c2, c3, o_ref):
        out = _gm_core(zu[...].astype(F32), zv[...].astype(F32), g_ref[...], b_ref[...], [ws_ref[g] for g in range(GM_GROUPS)],
                       [c0[...], c1[...], c2[...], c3[...]])
        o_ref[...] = out.astype(BF)

    return pl.pallas_call(body, grid=(n // t,), in_specs=_gm_specs(t), out_specs=_rows(t, GM_WIDTH),
                          out_shape=SDS((n, GM_WIDTH), BF), compiler_params=_params(("parallel",)),
                          name=name)(z, z, g_ln, b_ln, ws, *bcols)


def _gm_bwd(z, g_ln, b_ln, ws, bcols, dgm, dz, name, comm=None):
    n = z.shape[0]
    t = min(ROW_TILE, n)

    def body(zu, zv, g_ref, b_ref, ws_ref, c0, c1, c2, c3, dgm_ref, _, dz_ref, dg_ref, db_ref, dws_ref, e0, e1, e2,
             e3):
        first = pl.program_id(0) == 0
        _, vjp = jax.vjp(_gm_core, zu[...].astype(F32), zv[...].astype(F32), g_ref[...], b_ref[...],
                         [ws_ref[g] for g in range(GM_GROUPS)], [c0[...], c1[...], c2[...], c3[...]])
        dzu, dzv, dg, db, dws, dcols = vjp(dgm_ref[...])
        dz_ref[:, 0:GM_WIDTH] = dzu.astype(BF)
        dz_ref[:, GM_WIDTH:2 * GM_WIDTH] = dzv.astype(BF)
        _acc(dg_ref, dg, first)
        _acc(db_ref, db, first)
        _acc(dws_ref, jnp.stack(dws, axis=0), first)
        for ref, val in zip((e0, e1, e2, e3), dcols):
            _acc(ref, val, first)

    in_specs = _gm_specs(t) + [_rows(t, GM_WIDTH), ANY]
    return _pcall(
        body, grid=(n // t,), in_specs=in_specs,
        out_specs=[_rows(t, 2 * GM_WIDTH, ZU // (2 * GM_WIDTH)), _full((1, GM_WIDTH)), _full((1, GM_WIDTH)),
                   _full((GM_GROUPS, GM_CHUNK, GM_CHUNK))] + [_full((GM_CHUNK, 1))] * GM_GROUPS,
        out_shape=[SDS((n, Z_COLS), BF), SDS((1, GM_WIDTH), F32), SDS((1, GM_WIDTH), F32),
                   SDS((GM_GROUPS, GM_CHUNK, GM_CHUNK), F32)] + [SDS((GM_CHUNK, 1), F32)] * GM_GROUPS,
        sem=("arbitrary",), name=name, comm=comm, aliases={len(in_specs) - 1: 0})(z, z, g_ln, b_ln, ws, *bcols, dgm, dz)


def _rope_tables(pos_f, inv_full, cmask, smask, name, comm=None):
    n = pos_f.shape[0]
    t = min(ROW_TILE, n)

    def body(p_ref, inv_ref, cm_ref, sm_ref, cos_ref, sin_ref):
        ang = p_ref[...] * inv_ref[...]
        cos_ref[...] = jnp.cos(ang) * cm_ref[...]
        sin_ref[...] = jnp.sin(ang) * sm_ref[...]

    return _pcall(body, grid=(n // t,), in_specs=[_rows(t, 1)] + [_full((1, LANES))] * 3,
                  out_specs=[_rows(t, LANES)] * 2, out_shape=[SDS((n, LANES), F32)] * 2, sem=("parallel",),
                  name=name, comm=comm)(pos_f, inv_full, cmask, smask)


def _prep_norms(cq, ckv, g_cq, g_ckv):
    return _rmsn(cq, g_cq, Q_LORA), _rmsn(ckv, g_ckv, KV_LORA)


def _prep_heads(qa, kva, kpe, head_gains, cos_f, sin_s):
    g_qn, g_qp, g_kn, g_kp = head_gains
    qs = _split_lanes(qa)
    kvs = _split_lanes(kva)
    kp = _rope(_rmsn(kpe, g_kp, MLA_ROPE), cos_f, sin_s)
    q_out, k_out = [], []
    for h in range(MLA_HEADS):
        q_out.append(_rmsn(qs[h], g_qn, MLA_NOPE))
        q_out.append(_rope(_rmsn(qs[MLA_HEADS + h], g_qp, MLA_ROPE), cos_f, sin_s))
        k_out.append(_rmsn(kvs[h], g_kn, MLA_NOPE))
        k_out.append(kp)
    return (jnp.concatenate(q_out, axis=1), jnp.concatenate(k_out, axis=1),
            jnp.concatenate(kvs[MLA_HEADS:], axis=1))


def _prep_in_specs(t):
    return ([_rows(t, Q_LORA, CQ // Q_LORA), _rows(t, LANES, KPE // LANES), _rows(t, KV_LORA, CKV // KV_LORA),
             _rows(t, LANES), _rows(t, LANES), _full((1, Q_LORA)), _full((1, KV_LORA))] + [_full((1, LANES))] * 4
            + [_full((Q_LORA, 2048)), _full((KV_LORA, 2048))])


def _prep_fwd(z, cos_f, sin_s, gains, wq, wkv, name):
    n = z.shape[0]
    t = min(ROW_TILE, n)

    def body(cq, kpe, ckv, cos_ref, sin_ref, g_cq, g_ckv, g_qn, g_qp, g_kn, g_kp, wq_ref, wkv_ref, q_ref, k_ref, v_ref):
        cqn, ckvn = _prep_norms(cq[...].astype(F32), ckv[...].astype(F32), g_cq[...], g_ckv[...])
        qa = _dn(cqn, wq_ref[...], 1, 0)
        kva = _dn(ckvn, wkv_ref[...], 1, 0)
        q, k, v = _prep_heads(qa, kva, kpe[...].astype(F32), (g_qn[...], g_qp[...], g_kn[...], g_kp[...]), cos_ref[...],
                              sin_ref[...])
        q_ref[...] = q.astype(BF)
        k_ref[...] = k.astype(BF)
        v_ref[...] = v.astype(BF)

    return pl.pallas_call(body, grid=(n // t,), in_specs=_prep_in_specs(t),
                          out_specs=[_rows(t, 2048), _rows(t, 2048), _rows(t, 1024)],
                          out_shape=[SDS((n, 2048), BF), SDS((n, 2048), BF), SDS((n, 1024), BF)],
                          compiler_params=_params(("parallel",)),
                          name=name)(z, z, z, cos_f, sin_s, *gains, wq, wkv)


def _prep_bwd(z, cos_f, sin_s, gains, wq, wkv, dq, dk, dv, dz, name, comm=None):
    n = z.shape[0]
    t = min(ROW_TILE, n)
    wz = Q_LORA + LANES + KV_LORA

    def body(cq, kpe, ckv, cos_ref, sin_ref, g_cq, g_ckv, g_qn, g_qp, g_kn, g_kp, wq_ref, wkv_ref, dq_ref, dk_ref,
             dv_ref, _, dz_ref, o_cq, o_ckv, o_qn, o_qp, o_kn, o_kp, dwq_ref, dwkv_ref):
        first = pl.program_id(0) == 0
        cos_t, sin_t = cos_ref[...], sin_ref[...]
        (cqn, ckvn), vjp_norms = jax.vjp(_prep_norms, cq[...].astype(F32), ckv[...].astype(F32), g_cq[...], g_ckv[...])
        wq_t, wkv_t = wq_ref[...], wkv_ref[...]
        qa = _dn(cqn, wq_t, 1, 0)
        kva = _dn(ckvn, wkv_t, 1, 0)
        _, vjp_heads = jax.vjp(lambda a, b, c, g: _prep_heads(a, b, c, g, cos_t, sin_t), qa, kva, kpe[...].astype(F32),
                               (g_qn[...], g_qp[...], g_kn[...], g_kp[...]))
        dqa, dkva, dkpe, dhead = vjp_heads((dq_ref[...], dk_ref[...], dv_ref[...]))
        _acc(dwq_ref, _dn(cqn, dqa, 0, 0), first)
        _acc(dwkv_ref, _dn(ckvn, dkva, 0, 0), first)
        dcq, dckv, dg_cq, dg_ckv = vjp_norms((_dn(dqa, wq_t, 1, 1), _dn(dkva, wkv_t, 1, 1)))
        dz_ref[:, 0:Q_LORA] = dcq.astype(BF)
        dz_ref[:, Q_LORA:Q_LORA + LANES] = dkpe.astype(BF)
        dz_ref[:, Q_LORA + LANES:wz] = dckv.astype(BF)
        for ref, val in zip((o_cq, o_ckv, o_qn, o_qp, o_kn, o_kp), (dg_cq, dg_ckv) + tuple(dhead)):
            _acc(ref, val, first)

    gain_specs = [_full((1, Q_LORA)), _full((1, KV_LORA))] + [_full((1, LANES))] * 4
    gain_shapes = [SDS((1, Q_LORA), F32), SDS((1, KV_LORA), F32)] + [SDS((1, LANES), F32)] * 4
    in_specs = _prep_in_specs(t) + [_rows(t, 2048), _rows(t, 2048), _rows(t, 1024), ANY]
    return _pcall(
        body, grid=(n // t,), in_specs=in_specs,
        out_specs=[_rows(t, wz, CQ // wz)] + gain_specs + [_full((Q_LORA, 2048)), _full((KV_LORA, 2048))],
        out_shape=[SDS((n, Z_COLS), BF)] + gain_shapes + [SDS((Q_LORA, 2048), F32), SDS((KV_LORA, 2048), F32)],
        sem=("arbitrary",), name=name, comm=comm,
        aliases={len(in_specs) - 1: 0})(z, z, z, cos_f, sin_s, *gains, wq, wkv, dq, dk, dv, dz)


MLA_QK = 256
MLA_SCALE = 1.0 / math.sqrt(MLA_NOPE + MLA_ROPE)
LOG2E = 1.0 / math.log(2.0)
MLA_SCALE_LOG2E = MLA_SCALE * LOG2E


def _causal_mask(s, q0, k0):
    tq, tk = s.shape
    row = q0 + lax.broadcasted_iota(jnp.int32, (tq, tk), 0)
    col = k0 + lax.broadcasted_iota(jnp.int32, (tq, tk), 1)
    return jnp.where(row >= col, s, -jnp.inf)


def _mla_fwd(q, k, v, batch, seq, name, comm=None):
    n = q.shape[0]
    tq = min(ATT_TILE, seq)
    nq = seq // tq

    nh = ATT_HEADS

    def body(q_ref, k_ref, v_ref, o_ref, lse_ref):
        i = pl.program_id(2)

        def step(j, carry, diagonal=False):
            k0 = pl.multiple_of(j * tq, tq)
            out = []
            ones = jnp.ones((tq, LANES), BF)
            for hh in range(nh):
                m, acc = carry[hh]
                qb = q_ref[:, hh * MLA_QK:(hh + 1) * MLA_QK]
                kb = k_ref[pl.ds(k0, tq), hh * MLA_QK:(hh + 1) * MLA_QK]
                vb = v_ref[pl.ds(k0, tq), hh * MLA_V:(hh + 1) * MLA_V]
                s = _dn(qb, kb, 1, 1)
                if diagonal:
                    s = _causal_mask(s, i * tq, k0)
                m_new = jnp.maximum(m, jnp.max(s, axis=-1, keepdims=True))
                p = jnp.exp2((s - m_new) * MLA_SCALE_LOG2E)
                alpha = jnp.exp2((m - m_new) * MLA_SCALE_LOG2E)
                acc = alpha * acc + _dn(p, jnp.concatenate([vb, ones], axis=1), 1, 0)
                out.append((m_new, acc))
            return tuple(out)

        init = tuple((jnp.full((tq, 1), -jnp.inf, F32), jnp.zeros((tq, MLA_V + LANES), F32)) for _ in range(nh))
        final = step(i, lax.fori_loop(0, i, step, init), diagonal=True)
        for hh, (m, acc) in enumerate(final):
            l = acc[:, MLA_V:MLA_V + 1]
            o_ref[:, hh * MLA_V:(hh + 1) * MLA_V] = acc[:, :MLA_V] / l
            lse_ref[:, hh * LANES:(hh + 1) * LANES] = jnp.broadcast_to(m * MLA_SCALE + jnp.log(l), (tq, LANES))

    return _pcall(
        body, grid=(batch, MLA_HEADS // nh, nq),
        in_specs=[pl.BlockSpec((tq, nh * MLA_QK), lambda b, h, i: (b * nq + i, h)),
                  pl.BlockSpec((seq, nh * MLA_QK), lambda b, h, i: (b, h)),
                  pl.BlockSpec((seq, nh * MLA_V), lambda b, h, i: (b, h))],
        out_specs=[pl.BlockSpec((tq, nh * MLA_V), lambda b, h, i: (b * nq + i, h)),
                   pl.BlockSpec((tq, nh * LANES), lambda b, h, i: (b * nq + i, h))],
        out_shape=[SDS((n, MLA_HEADS * MLA_V), F32), SDS((n, MLA_HEADS * LANES), F32)],
        sem=("parallel", "parallel", "arbitrary"), name=name, comm=comm)(q, k, v)


def _mla_bwd(q, k, v, o, lse, do, batch, seq, name, comm=None):
    n = q.shape[0]
    tk = min(ATT_TILE, seq)
    nk = seq // tk

    nh = ATT_HEADS

    def body(q_ref, k_ref, v_ref, o_ref, lse_ref, do_ref, dq_ref, dk_ref, dv_ref):
        jk = pl.program_id(2)

        @pl.when(jk == 0)
        def _():
            dq_ref[...] = jnp.zeros_like(dq_ref)

        def step(i, carry, diagonal=False):
            q0 = pl.multiple_of(i * tk, tk)
            rows = pl.ds(q0, tk)
            out = []
            for hh in range(nh):
                dk_acc, dv_acc = carry[hh]
                qk_cols = slice(hh * MLA_QK, (hh + 1) * MLA_QK)
                v_cols = slice(hh * MLA_V, (hh + 1) * MLA_V)
                kb = k_ref[:, qk_cols]
                vb = v_ref[:, v_cols]
                qb = q_ref[rows, qk_cols]
                dob = do_ref[rows, v_cols]
                delta = jnp.sum(dob * o_ref[rows, v_cols], axis=-1, keepdims=True)
                s = _dn(qb, kb, 1, 1)
                if diagonal:
                    s = _causal_mask(s, q0, jk * tk)
                p = jnp.exp2(s * MLA_SCALE_LOG2E - lse_ref[rows, hh * LANES:hh * LANES + 1] * LOG2E)
                dv_acc = dv_acc + _dn(p, dob, 0, 0)
                dp = _dn(dob, vb, 1, 1)
                ds = p * (dp - delta) * MLA_SCALE
                dk_acc = dk_acc + _dn(ds, qb, 0, 0)
                dq_ref[rows, qk_cols] += _dn(ds, kb, 1, 0)
                out.append((dk_acc, dv_acc))
            return tuple(out)

        init = tuple((jnp.zeros((tk, MLA_QK), F32), jnp.zeros((tk, MLA_V), F32)) for _ in range(nh))
        final = lax.fori_loop(jk + 1, nk, step, step(jk, init, diagonal=True))
        for hh, (dk_acc, dv_acc) in enumerate(final):
            dk_ref[:, hh * MLA_QK:(hh + 1) * MLA_QK] = dk_acc
            dv_ref[:, hh * MLA_V:(hh + 1) * MLA_V] = dv_acc

    full_qk = pl.BlockSpec((seq, nh * MLA_QK), lambda b, h, j: (b, h))
    full_v = pl.BlockSpec((seq, nh * MLA_V), lambda b, h, j: (b, h))
    blk_qk = pl.BlockSpec((tk, nh * MLA_QK), lambda b, h, j: (b * nk + j, h))
    blk_v = pl.BlockSpec((tk, nh * MLA_V), lambda b, h, j: (b * nk + j, h))
    return _pcall(
        body, grid=(batch, MLA_HEADS // nh, nk),
        in_specs=[full_qk, blk_qk, blk_v, full_v, full_v, full_v],
        out_specs=[full_qk, blk_qk, blk_v],
        out_shape=[SDS((n, MLA_HEADS * MLA_QK), F32), SDS((n, MLA_HEADS * MLA_QK), F32),
                   SDS((n, MLA_HEADS * MLA_V), F32)],
        sem=("parallel", "parallel", "arbitrary"), name=name, comm=comm)(q, k, v, o, lse, do)


MEM_SCALE = 1.0 / math.sqrt(HEAD_DIM)
MEM_W = MEM_HEADS * HEAD_DIM


def _mem_core(qs, ks, vs, g_mq, g_mk):
    outs = []
    for h in range(MEM_HEADS):
        qh = _rmsn(qs[h], g_mq, HEAD_DIM)
        kh = _rmsn(ks[h], g_mk, HEAD_DIM)
        p = _softmax(_mm_nt(qh, kh) * MEM_SCALE)
        outs.append(_mm_nn(p, vs[h]))
    return jnp.concatenate(outs, axis=1)


def _mem_load(qm, kvm, g_mq, g_mk):
    hs = range(MEM_HEADS)
    qs = [qm[:, h * LANES:(h + 1) * LANES].astype(F32) for h in hs]
    ks = [kvm[:, h * LANES:(h + 1) * LANES] for h in hs]
    vs = [kvm[:, MEM_W + h * LANES:MEM_W + (h + 1) * LANES] for h in hs]
    return qs, ks, vs, g_mq[...], g_mk[...]


def _mem_fwd(z, kvm, g_mq, g_mk, batch, seq, name, comm=None):
    n = z.shape[0]
    t = min(ROW_TILE, seq)
    per = seq // t

    def body(qm, kvm_ref, gq, gk, o_ref):
        o_ref[...] = _mem_core(*_mem_load(qm, kvm_ref, gq, gk)).astype(BF)

    return _pcall(
        body, grid=(n // t,),
        in_specs=[_rows(t, MEM_W, QM // MEM_W), pl.BlockSpec((MEM_LEN, 2 * MEM_W), lambda i: (i // per, 0)),
                  _full((1, LANES)), _full((1, LANES))],
        out_specs=_rows(t, MEM_W), out_shape=SDS((n, MEM_W), BF), sem=("parallel",), name=name,
        comm=comm)(z, kvm, g_mq, g_mk)


def _mem_bwd(z, kvm, g_mq, g_mk, dom, dz, batch, seq, name):
    n = z.shape[0]
    t = min(ROW_TILE, seq)
    per = seq // t

    def body(qm, kvm_ref, gq, gk, dom_ref, _, dz_ref, dkvm_ref, dgq_ref, dgk_ref):
        i = pl.program_id(0)
        _, vjp = jax.vjp(_mem_core, *_mem_load(qm, kvm_ref, gq, gk))
        dqs, dks, dvs, dgq, dgk = vjp(dom_ref[...])
        dz_ref[...] = jnp.concatenate(dqs, axis=1).astype(BF)
        _acc(dkvm_ref, jnp.concatenate(dks + dvs, axis=1), i % per == 0)
        _acc(dgq_ref, dgq, i == 0)
        _acc(dgk_ref, dgk, i == 0)

    kv_spec = pl.BlockSpec((MEM_LEN, 2 * MEM_W), lambda i: (i // per, 0))
    return pl.pallas_call(
        body, grid=(n // t,),
        in_specs=[_rows(t, MEM_W, QM // MEM_W), kv_spec, _full((1, LANES)), _full((1, LANES)), _rows(t, MEM_W), ANY],
        out_specs=[_rows(t, MEM_W, QM // MEM_W), kv_spec, _full((1, LANES)), _full((1, LANES))],
        out_shape=[SDS((n, Z_COLS), BF), SDS((batch * MEM_LEN, 2 * MEM_W), F32), SDS((1, LANES), F32),
                   SDS((1, LANES), F32)],
        input_output_aliases={5: 0},
        compiler_params=_params(("arbitrary",)), name=name)(z, kvm, g_mq, g_mk, dom, dz)


def _me():
    return lax.axis_index("x"), lax.axis_index("y"), lax.axis_index("c")


def _other_chips(x, y):
    return [(1 - x, y), (x, 1 - y), (1 - x, 1 - y)]


def _shard_shape(name):
    r, c = BIG_SHAPE[name]
    return (r, c // N_CHIPS) if name in COL_SHARDED else (r // N_CHIPS, c)


def _n_pieces(half_rows):
    return max(1, half_rows // PIECE_ROWS)


def _piece_plan(shapes):
    plan = []
    for r, _ in shapes:
        h = r // 2
        n = _n_pieces(h)
        plan.append((h, n, h // n))
    return plan


def _remote(send, recv, sem, src, dst, to):
    return pltpu.make_async_remote_copy(src_ref=src, dst_ref=dst, send_sem=send.at[sem], recv_sem=recv.at[sem],
                                        device_id=to, device_id_type=MESH)


def _gather_far(shards):
    plan = _piece_plan([s.shape for s in shards])
    n_far = 3 * sum(n for _, n, _ in plan)
    n_loc = 2 * sum(n for _, n, _ in plan)

    def copies(s_refs, o_refs, send, recv, local):
        x, y, c = _me()
        k = 2 * x + y
        mine, sends, arrivals = [], [], []
        for t, (h, n, pr) in enumerate(plan):
            s_ref, o_ref = s_refs[t], o_refs[t]
            for core in range(2):
                for p in range(n):
                    rows = pl.ds(core * h + p * pr, pr)
                    mine.append(pltpu.make_async_copy(s_ref.at[rows], o_ref.at[k, rows], local.at[len(mine)]))
            for chip in _other_chips(x, y):
                for p in range(n):
                    rows = pl.ds(c * h + p * pr, pr)
                    s = len(sends)
                    sends.append(_remote(send, recv, s, s_ref.at[rows], o_ref.at[k, rows], (*chip, c)))
                    arrivals.append(_remote(send, recv, s, s_ref.at[rows], o_ref.at[2 * chip[0] + chip[1], rows],
                                            (*chip, c)))
        return sends, arrivals, mine

    return _Phase(shards, [SDS((N_CHIPS,) + s.shape, s.dtype) for s in shards], n_far, n_loc, copies)


def _gather_near(bufs):
    plan = _piece_plan([b.shape[1:] for b in bufs])
    n_sem = 3 * sum(n for _, n, _ in plan)

    def copies(i_refs, o_refs, send, recv, local):
        x, y, c = _me()
        sib = (x, y, 1 - c)
        sends, arrivals = [], []
        for t, (h, n, pr) in enumerate(plan):
            for chip in _other_chips(x, y):
                ci = 2 * chip[0] + chip[1]
                for p in range(n):
                    rows = pl.ds(c * h + p * pr, pr)
                    rows_sib = pl.ds((1 - c) * h + p * pr, pr)
                    s = len(sends)
                    sends.append(_remote(send, recv, s, i_refs[t].at[ci, rows], o_refs[t].at[ci, rows], sib))
                    arrivals.append(_remote(send, recv, s, i_refs[t].at[ci, rows_sib], o_refs[t].at[ci, rows_sib], sib))
        return sends, arrivals, []

    return _Phase(bufs, [SDS(b.shape, b.dtype) for b in bufs], n_sem, 0, copies, {t: t for t in range(len(bufs))})


def _pair_exchange(grads):
    plan = _piece_plan([g.shape[1:] for g in grads])
    n_sem = sum(n for _, n, _ in plan)

    def copies(g_refs, o_refs, send, recv, local):
        x, y, c = _me()
        sends = []
        for t, (h, n, pr) in enumerate(plan):
            for p in range(n):
                sends.append(_remote(send, recv, len(sends), g_refs[t].at[:, pl.ds((1 - c) * h + p * pr, pr)],
                                     o_refs[t].at[:, pl.ds(p * pr, pr)], (x, y, 1 - c)))
        return sends, sends, []

    return _Phase(grads, [SDS((N_CHIPS, g.shape[1] // 2, g.shape[2]), F32) for g in grads], n_sem, 0, copies)


def _pair_add(ck, g, theirs, name):
    _, r, c = g.shape
    (h, n, pr), = _piece_plan([(r, c)])

    def body(ck_ref, g_ref, t_ref, p32_ref, pbf_ref):
        s = g_ref[...] + t_ref[...]
        p32_ref[...] = s
        pbf_ref[...] = s.astype(BF)

    half = pl.BlockSpec((None, pr, c), lambda k, p, ck: (k, p, 0))
    spec = pltpu.PrefetchScalarGridSpec(
        num_scalar_prefetch=1, grid=(N_CHIPS, n),
        in_specs=[pl.BlockSpec((None, pr, c), lambda k, p, ck: (k, ck[0] * n + p, 0)), half], out_specs=[half, half])
    return pl.pallas_call(body, grid_spec=spec, out_shape=[SDS((N_CHIPS, h, c), F32), SDS((N_CHIPS, h, c), BF)],
                          compiler_params=_params(("arbitrary", "arbitrary")), name=name)(ck, g, theirs)


def _scatter_partials(pbfs):
    plan = [(h, _n_pieces(h), h // _n_pieces(h)) for h in [p.shape[1] for p in pbfs]]
    n_sem = 3 * sum(n for _, n, _ in plan)

    def copies(p_refs, o_refs, send, recv, local):
        x, y, c = _me()
        sends = []
        for t, (h, n, pr) in enumerate(plan):
            for j, chip in enumerate(_other_chips(x, y)):
                for p in range(n):
                    rows = pl.ds(p * pr, pr)
                    sends.append(_remote(send, recv, len(sends), p_refs[t].at[2 * chip[0] + chip[1], rows],
                                         o_refs[t].at[j, rows], (*chip, c)))
        return sends, sends, []

    return _Phase(pbfs, [SDS((3,) + p.shape[1:], BF) for p in pbfs], n_sem, 0, copies)


def _sum_chips(ck, p32, slots, name):
    _, h, c = p32.shape
    n = _n_pieces(h)
    pr = h // n

    def body(ck_ref, p_ref, s_ref, o_ref):
        o_ref[...] = ((p_ref[...] + s_ref[0].astype(F32)) + s_ref[1].astype(F32)) + s_ref[2].astype(F32)

    spec = pltpu.PrefetchScalarGridSpec(
        num_scalar_prefetch=1, grid=(n,),
        in_specs=[pl.BlockSpec((None, pr, c), lambda p, ck: (ck[1], p, 0)),
                  pl.BlockSpec((3, pr, c), lambda p, ck: (0, p, 0))],
        out_specs=pl.BlockSpec((pr, c), lambda p, ck: (ck[0] * n + p, 0)))
    return pl.pallas_call(body, grid_spec=spec, out_shape=SDS((2 * h, c), F32),
                          compiler_params=_params(("arbitrary",)), name=name)(ck, p32, slots)


def _join_halves(sums):
    plan = _piece_plan([s.shape for s in sums])
    n_sem = sum(n for _, n, _ in plan)

    def copies(r_refs, o_refs, send, recv, local):
        x, y, c = _me()
        sends, arrivals = [], []
        for t, (h, n, pr) in enumerate(plan):
            for p in range(n):
                rows = pl.ds(c * h + p * pr, pr)
                rows_sib = pl.ds((1 - c) * h + p * pr, pr)
                s = len(sends)
                sends.append(_remote(send, recv, s, r_refs[t].at[rows], o_refs[t].at[rows], (x, y, 1 - c)))
                arrivals.append(_remote(send, recv, s, r_refs[t].at[rows_sib], o_refs[t].at[rows_sib], (x, y, 1 - c)))
        return sends, arrivals, []

    return _Phase(sums, [SDS(s.shape, F32) for s in sums], n_sem, 0, copies, {t: t for t in range(len(sums))})


def _gather_small(s, name):
    def body(s_ref, o_ref, send, recv, local):
        x, y, c = _me()
        me = 4 * x + 2 * y + c
        keep = pltpu.make_async_copy(s_ref, o_ref.at[me], local)
        keep.start()
        sends = []
        for r in range(1, 8):
            fx, fy, fc = (r >> 2) & 1, (r >> 1) & 1, r & 1
            to = (x ^ fx, y ^ fy, c ^ fc)
            sends.append(pltpu.make_async_remote_copy(
                src_ref=s_ref, dst_ref=o_ref.at[me], send_sem=send.at[r - 1], recv_sem=recv.at[r - 1],
                device_id=to, device_id_type=MESH))
        for cp in sends:
            cp.start()
        for r in range(1, 8):
            fx, fy, fc = (r >> 2) & 1, (r >> 1) & 1, r & 1
            src = 4 * (x ^ fx) + 2 * (y ^ fy) + (c ^ fc)
            pltpu.make_async_remote_copy(
                src_ref=s_ref, dst_ref=o_ref.at[src], send_sem=send.at[r - 1], recv_sem=recv.at[r - 1],
                device_id=(x ^ fx, y ^ fy, c ^ fc), device_id_type=MESH).wait_recv()
        for cp in sends:
            cp.wait_send()
        keep.wait()

    return pl.pallas_call(
        body, in_specs=[ANY], out_specs=ANY, out_shape=SDS((8, SMALL_ROWS, LANES), F32),
        scratch_shapes=[pltpu.SemaphoreType.DMA((7,)), pltpu.SemaphoreType.DMA((7,)), pltpu.SemaphoreType.DMA],
        name=name)(s)


def _adam_math(w, g, m, v):
    nm = ADAM_B1 * m + (1.0 - ADAM_B1) * g
    nv = ADAM_B2 * v + (1.0 - ADAM_B2) * (g * g)
    m_hat = nm / (1.0 - ADAM_B1 ** ADAM_STEP)
    v_hat = nv / (1.0 - ADAM_B2 ** ADAM_STEP)
    return -ADAM_LR * (m_hat / (jnp.sqrt(v_hat) + ADAM_EPS) + ADAM_WD * w), nm, nv


def _adamw(w, g, m, v, name):
    _, r, c = w.shape
    t = max(d for d in range(8, r + 1, 8) if r % d == 0 and 16 * d * c * 4 <= VMEM_LIMIT - (8 << 20))

    def body(w_ref, g_ref, m_ref, v_ref, go_ref, d_ref, nm_ref, nv_ref):
        g_ = g_ref[...]
        d, nm, nv = _adam_math(w_ref[...], g_, m_ref[...], v_ref[...])
        go_ref[...] = g_
        d_ref[...] = d
        nm_ref[...] = nm
        nv_ref[...] = nv

    lead = pl.BlockSpec((None, t, c), lambda i: (0, i, 0))
    return pl.pallas_call(body, grid=(r // t,), in_specs=[lead, _rows(t, c), lead, lead], out_specs=[lead] * 4,
                          out_shape=[SDS((1, r, c), F32)] * 4, compiler_params=_params(("parallel",)),
                          name=name)(w, g, m, v)


def _small_layout():
    out, r0 = {}, 0
    for n in SMALL:
        size = int(np.prod(SMALL_SHAPE[n]))
        nr = -(-size // LANES)
        out[n] = (r0, nr)
        r0 += nr
    assert r0 <= SMALL_ROWS
    return out, r0


def _pack_small(grads, loss_tile, name):
    layout, used = _small_layout()

    def body(*refs):
        o_ref = refs[-1]
        o_ref[used:used + 1, :] = refs[-2][0:1, :]
        for n, ref in zip(SMALL, refs[:-2]):
            r0, nr = layout[n]
            if n == "w_spatial":
                for g in range(GM_GROUPS):
                    o_ref[r0 + g * GM_CHUNK:r0 + (g + 1) * GM_CHUNK, :] = ref[g]
            elif n == "b_spatial":
                o_ref[r0:r0 + nr, :] = ref[...]
            else:
                for i in range(nr):
                    o_ref[r0 + i:r0 + i + 1, :] = ref[:, i * LANES:(i + 1) * LANES]
        if used + 1 < SMALL_ROWS:
            o_ref[used + 1:SMALL_ROWS, :] = jnp.zeros((SMALL_ROWS - used - 1, LANES), F32)

    return pl.pallas_call(body, out_shape=SDS((SMALL_ROWS, LANES), F32), name=name)(*grads, loss_tile)


def _adamw_small(gathered, ws, ms, vs, name):
    layout, used = _small_layout()
    n_t = len(SMALL)

    def body(*refs):
        g_ref = refs[0]
        w_refs, m_refs, v_refs = refs[1:1 + n_t], refs[1 + n_t:1 + 2 * n_t], refs[1 + 2 * n_t:1 + 3 * n_t]
        outs = refs[1 + 3 * n_t:1 + 7 * n_t]
        acc = refs[-1]
        total = g_ref[0]
        for j in range(1, 8):
            total = total + g_ref[j]
        acc[...] = total
        refs[1 + 7 * n_t][...] = acc[used:used + 1, :]
        for t, n in enumerate(SMALL):
            r0, nr = layout[n]
            o_refs = [outs[t], outs[n_t + t], outs[2 * n_t + t], outs[3 * n_t + t]]
            if n == "w_spatial":
                views = [((0, g), slice(r0 + g * GM_CHUNK, r0 + (g + 1) * GM_CHUNK), slice(None))
                         for g in range(GM_GROUPS)]
            elif n == "b_spatial":
                views = [((0,), slice(r0, r0 + nr), slice(None))]
            else:
                width = SMALL_SHAPE[n][1]
                views = [((slice(None), slice(i * LANES, min((i + 1) * LANES, width))), slice(r0 + i, r0 + i + 1),
                          slice(0, min(LANES, width - i * LANES))) for i in range(nr)]
            for idx, rows, lanes in views:
                g = acc[rows, lanes]
                d, nm, nv = _adam_math(w_refs[t][idx], g, m_refs[t][idx], v_refs[t][idx])
                for ref, val in zip(o_refs, (g, d, nm, nv)):
                    ref[idx] = val

    shapes = [SDS(SMALL_SHAPE[n], F32) for n in SMALL]
    return pl.pallas_call(body, out_shape=shapes * 4 + [SDS((1, LANES), F32)],
                          scratch_shapes=[pltpu.VMEM((SMALL_ROWS, LANES), F32)], name=name)(gathered, *ws, *ms, *vs)


def _win_layout(w_in):
    pad = jnp.zeros((w_in.shape[0], LANES - MLA_ROPE), w_in.dtype)
    u, v, cq = w_in[:, 0:512], w_in[:, 512:1024], w_in[:, 1024:1408]
    ckv, kpe, qm, zg = w_in[:, 1408:1664], w_in[:, 1664:1728], w_in[:, 1728:2240], w_in[:, 2240:5312]
    return jnp.concatenate([zg, u, v, qm, cq, kpe, pad, ckv], axis=1)


def _win_unlayout(g):
    zg, u, v, qm = g[:, ZG:ZG + 3072], g[:, ZU:ZU + 512], g[:, ZV:ZV + 512], g[:, QM:QM + 512]
    cq, kpe, ckv = g[:, CQ:CQ + 384], g[:, KPE:KPE + MLA_ROPE], g[:, CKV:CKV + 256]
    return jnp.concatenate([u, v, cq, ckv, kpe, qm, zg], axis=1)


def _wq_layout(w_uq):
    w = w_uq.reshape(Q_LORA, MLA_HEADS, MLA_NOPE + MLA_ROPE)
    nope = w[:, :, :MLA_NOPE].reshape(Q_LORA, MLA_HEADS * MLA_NOPE)
    pe = jnp.pad(w[:, :, MLA_NOPE:], ((0, 0), (0, 0), (0, LANES - MLA_ROPE))).reshape(Q_LORA, MLA_HEADS * LANES)
    return jnp.concatenate([nope, pe], axis=1)


def _wq_unlayout(g):
    nope = g[:, :1024].reshape(Q_LORA, MLA_HEADS, MLA_NOPE)
    pe = g[:, 1024:].reshape(Q_LORA, MLA_HEADS, LANES)[:, :, :MLA_ROPE]
    return jnp.concatenate([nope, pe], axis=2).reshape(Q_LORA, MLA_HEADS * (MLA_NOPE + MLA_ROPE))


def _wkv_layout(w_ukv):
    w = w_ukv.reshape(KV_LORA, MLA_HEADS, MLA_NOPE + MLA_V)
    return jnp.concatenate([w[:, :, :MLA_NOPE].reshape(KV_LORA, 1024), w[:, :, MLA_NOPE:].reshape(KV_LORA, 1024)],
                           axis=1)


def _wkv_unlayout(g):
    kn = g[:, :1024].reshape(KV_LORA, MLA_HEADS, MLA_NOPE)
    v = g[:, 1024:].reshape(KV_LORA, MLA_HEADS, MLA_V)
    return jnp.concatenate([kn, v], axis=2).reshape(KV_LORA, MLA_HEADS * (MLA_NOPE + MLA_V))


def _owner_major(g, name):
    r, c = _shard_shape(name)
    return g.reshape(r, N_CHIPS, c).transpose(1, 0, 2) if name in COL_SHARDED else g.reshape(N_CHIPS, r, c)


def _pad_lanes(g):
    return jnp.pad(g, ((0, 0), (0, LANES - g.shape[1])))


def kernel(x, mem, positions, g_mix, w_in, g_cq, w_uq, g_ckv, w_ukv, g_q_nope, g_q_pe, g_k_nope, g_k_pe, g_gm_ln, b_gm_ln, w_spatial, b_spatial, g_mem, w_mem_kv, g_mq, g_mk, w_o_gm, w_o_mla, w_o_mem, w_out, g_ffn, w_ff1, w_ff2, loss_target, m_g_mix, m_w_in, m_g_cq, m_w_uq, m_g_ckv, m_w_ukv, m_g_q_nope, m_g_q_pe, m_g_k_nope, m_g_k_pe, m_g_gm_ln, m_b_gm_ln, m_w_spatial, m_b_spatial, m_g_mem, m_w_mem_kv, m_g_mq, m_g_mk, m_w_o_gm, m_w_o_mla, m_w_o_mem, m_w_out, m_g_ffn, m_w_ff1, m_w_ff2, v_g_mix, v_w_in, v_g_cq, v_w_uq, v_g_ckv, v_w_ukv, v_g_q_nope, v_g_q_pe, v_g_k_nope, v_g_k_pe, v_g_gm_ln, v_b_gm_ln, v_w_spatial, v_b_spatial, v_g_mem, v_w_mem_kv, v_g_mq, v_g_mk, v_w_o_gm, v_w_o_mla, v_w_o_mem, v_w_out, v_g_ffn, v_w_ff1, v_w_ff2):
    given = dict(locals())
    wts = {n: given[n] for n in WEIGHTS}
    mom = {n: given["m_" + n] for n in WEIGHTS}
    var = {n: given["v_" + n] for n in WEIGHTS}
    batch, seq, _ = x.shape
    n_tok = batch * seq

    def natural(n, g):
        r, c = _shard_shape(n)
        return g.transpose(1, 0, 2).reshape(r, N_CHIPS * c) if n in COL_SHARDED else g.reshape(N_CHIPS * r, c)

    def far(names):
        return _gather_far([wts[n][0].astype(BF) for n in names])

    x2 = x.reshape(n_tok, D_MODEL)
    tgt2 = loss_target.reshape(n_tok, D_MODEL)
    mem2 = mem.reshape(batch * MEM_LEN, D_MODEL)
    pos_f = positions.reshape(n_tok, 1).astype(F32)

    inv = ROPE_BASE ** (-jnp.arange(0, MLA_ROPE, 2, dtype=F32) / MLA_ROPE)
    zeros64 = jnp.zeros((LANES - MLA_ROPE,), F32)
    inv_full = jnp.concatenate([inv, inv, zeros64]).reshape(1, LANES)
    half = MLA_ROPE // 2
    cmask = jnp.concatenate([jnp.ones((MLA_ROPE,), F32), zeros64]).reshape(1, LANES)
    smask = jnp.concatenate([-jnp.ones((half,), F32), jnp.ones((half,), F32), zeros64]).reshape(1, LANES)

    prep_gains = [g_cq, g_ckv, g_q_nope, _pad_lanes(g_q_pe), g_k_nope, _pad_lanes(g_k_pe)]
    ws = w_spatial[0]
    bcols = [b_spatial[0, g].reshape(GM_CHUNK, 1) for g in range(GM_GROUPS)]

    h1, early_far = _rms_fwd(x2, g_mix, "rms_mix", comm=far(EARLY))
    (cos_f, sin_s), early = _rope_tables(pos_f, inv_full, cmask, smask, "rope_tables", comm=_gather_near(early_far))
    full = {n: natural(n, g) for n, g in zip(EARLY, early)}
    win = _win_layout(full["w_in"])
    wq = _wq_layout(full["w_uq"])
    wkv = _wkv_layout(full["w_ukv"])
    z, proj_far = _mm(h1, win, out_dtypes=(BF,), name="mm_in", comm=far(LATE_PROJ))
    gm = _gm_fwd(z, g_gm_ln, b_gm_ln, ws, bcols, "gm_fwd")
    qc, kc, vc = _prep_fwd(z, cos_f, sin_s, prep_gains, wq, wkv, "prep_fwd")
    (o_mla, lse), ff_far = _mla_fwd(qc, kc, vc, batch, seq, "mla_fwd", comm=far(LATE_FF))
    memn = _rms_fwd(mem2, g_mem, "rms_mem")
    kvm, proj = _mm(memn, full["w_mem_kv"], name="mm_memkv", comm=_gather_near(proj_far))
    o_mem, ff = _mem_fwd(z, kvm, g_mq, g_mk, batch, seq, "mem_fwd", comm=_gather_near(ff_far))
    full.update({n: natural(n, g) for n, g in zip(LATE_PROJ + LATE_FF, list(proj) + list(ff))})
    y_gm = _mm(gm, full["w_o_gm"], out_dtypes=(BF,), name="mm_o_gm")
    y_mla = _mm(o_mla, full["w_o_mla"], out_dtypes=(BF,), name="mm_o_mla")
    y_mem = _mm(o_mem, full["w_o_mem"], out_dtypes=(BF,), name="mm_o_mem")
    merged = _merge_fwd(z, y_gm, y_mla, y_mem, "merge_fwd")
    x1 = _mm(merged, full["w_out"], ins=(x2,), epilogue=_add_to, name="mm_out")
    h2 = _rms_fwd(x1, g_ffn, "rms_ffn")
    a_ff, r_ff = _mm(h2, full["w_ff1"], epilogue=_relu2, out_dtypes=(BF, BF), name="mm_ff1")
    dy, dyb, loss_tile = _mm(r_ff, full["w_ff2"], ins=(x1, tgt2), epilogue=_loss_tail, out_dtypes=(F32, BF),
                             total=True, name="mm_ff2")

    gw = {}
    da = _mm(dyb, full["w_ff2"], tb=True, ins=(a_ff,), epilogue=_relu2_bwd, out_dtypes=(BF,), name="mm_d_a")
    gw["w_ff2"] = _owner_major(_mm(r_ff, dyb, ta=True, name="mm_dw_ff2"), "w_ff2")
    gw["w_ff1"] = _mm(h2, da, ta=True, owner_cols=D_FF // N_CHIPS, name="mm_dw_ff1")
    dh2 = _mm(da, full["w_ff1"], tb=True, name="mm_d_h2")
    dx1, dx1b, dg_ffn = _rms_bwd(x1, g_ffn, dh2, dy, "rms_ffn_bwd")
    dmerged = _mm(dx1b, full["w_out"], tb=True, name="mm_d_merged")
    gw["w_out"] = _owner_major(_mm(merged, dx1b, ta=True, name="mm_dw_out"), "w_out")
    dz, dy_gm, dy_mla, dy_mem = _merge_bwd(z, y_gm, y_mla, y_mem, dmerged, "merge_bwd")
    dgm = _mm(dy_gm, full["w_o_gm"], tb=True, name="mm_d_gm")
    gw["w_o_gm"] = _mm(gm, dy_gm, ta=True, owner_cols=D_MODEL // N_CHIPS, name="mm_dw_o_gm")
    do_mla = _mm(dy_mla, full["w_o_mla"], tb=True, name="mm_d_omla")
    gw["w_o_mla"] = _owner_major(_mm(o_mla, dy_mla, ta=True, name="mm_dw_o_mla"), "w_o_mla")
    do_mem = _mm(dy_mem, full["w_o_mem"], tb=True, name="mm_d_omem")
    gw["w_o_mem"] = _mm(o_mem, dy_mem, ta=True, owner_cols=D_MODEL // N_CHIPS, name="mm_dw_o_mem")
    ck = jnp.stack([lax.axis_index("c"), 2 * lax.axis_index("x") + lax.axis_index("y")]).astype(jnp.int32)

    def pair_sums(names, theirs):
        return [_pair_add(ck, gw[n], t, "pair_add_" + n) for n, t in zip(names, theirs)]

    def chip_sums(names, pairs, slots):
        return [_sum_chips(ck, p[0], s, "sum_chips_" + n) for n, p, s in zip(names, pairs, slots)]

    (dz, dg_ln, db_ln, dws, *dbcols), theirs = _gm_bwd(z, g_gm_ln, b_gm_ln, ws, bcols, dgm, dz, "gm_bwd",
                                                      comm=_pair_exchange([gw[n] for n in LATE]))
    pairs = pair_sums(LATE, theirs)
    (dq, dk, dv), slots = _mla_bwd(qc, kc, vc, o_mla, lse, do_mla, batch, seq, "mla_bwd",
                                   comm=_scatter_partials([p[1] for p in pairs]))
    sums = chip_sums(LATE, pairs, slots)
    (dz, dg_cq, dg_ckv, dg_qn, dg_qp, dg_kn, dg_kp, dwq, dwkv), reduced_late = _prep_bwd(
        z, cos_f, sin_s, prep_gains, wq, wkv, dq, dk, dv, dz, "prep_bwd", comm=_join_halves(sums))
    dz, dkvm, dg_mq, dg_mk = _mem_bwd(z, kvm, g_mq, g_mk, do_mem, dz, batch, seq, "mem_bwd")
    dmemn = _mm(dkvm, full["w_mem_kv"], tb=True, name="mm_d_memn")
    gw["w_mem_kv"] = _owner_major(_mm(memn, dkvm, ta=True, name="mm_dw_memkv"), "w_mem_kv")
    _, _, dg_mem = _rms_bwd(mem2, g_mem, dmemn, None, "rms_mem_bwd")
    gw["w_in"] = _owner_major(_win_unlayout(_mm(h1, dz, ta=True, name="mm_dw_in")), "w_in")
    gw["w_uq"] = _owner_major(_wq_unlayout(dwq), "w_uq")
    gw["w_ukv"] = _owner_major(_wkv_unlayout(dwkv), "w_ukv")
    dh1, theirs = _mm(dz, win, tb=True, name="mm_d_h1_top", rows=(0, 2), comm=_pair_exchange([gw[n] for n in EARLY]))
    pairs = pair_sums(EARLY, theirs)
    dh1, slots = _mm(dz, win, tb=True, name="mm_d_h1_bottom", rows=(1, 2), into=dh1,
                     comm=_scatter_partials([p[1] for p in pairs]))
    grad_x, _, dg_mix = _rms_bwd(x2, g_mix, dh1, dx1, "rms_mix_bwd")
    reduced_early = _run_phase(_join_halves(chip_sums(EARLY, pairs, slots)), "join_early")
    reduced = dict(zip(LATE + EARLY, list(reduced_late) + list(reduced_early)))

    def swapped(a):
        return jnp.swapaxes(a, -1, -2)

    results = {n: _adamw(wts[n], reduced[n], mom[n], var[n], "adamw_" + n) for n in BIG if n != "w_in"}
    results["w_in"] = [swapped(r) for r in _adamw(swapped(w_in), swapped(reduced["w_in"]), swapped(m_w_in),
                                                  swapped(v_w_in), "adamw_w_in")]

    small_g = {"g_mix": dg_mix, "g_cq": dg_cq, "g_ckv": dg_ckv, "g_q_nope": dg_qn, "g_q_pe": dg_qp,
               "g_k_nope": dg_kn, "g_k_pe": dg_kp, "g_gm_ln": dg_ln, "b_gm_ln": db_ln, "w_spatial": dws,
               "b_spatial": jnp.concatenate(dbcols, axis=1).T, "g_mem": dg_mem, "g_mq": dg_mq, "g_mk": dg_mk,
               "g_ffn": dg_ffn}
    packed = _pack_small([small_g[n] for n in SMALL], loss_tile, "pack_small")
    small_out = _adamw_small(_gather_small(packed, "gather_small"), [wts[n] for n in SMALL],
                             [mom[n] for n in SMALL], [var[n] for n in SMALL], "adamw_small")
    for t, n in enumerate(SMALL):
        results[n] = [small_out[j * len(SMALL) + t] for j in range(4)]

    loss = small_out[4 * len(SMALL)][0, 0]
    grad_x = grad_x.reshape(batch, seq, D_MODEL)
    return (loss, grad_x, *[results[n][0] for n in WEIGHTS], *[results[n][1] for n in WEIGHTS],
            *[results[n][2] for n in WEIGHTS], *[results[n][3] for n in WEIGHTS])
```

```python
import functools
import math

import numpy as np
import jax
import jax.numpy as jnp
from jax import lax
from jax.experimental import pallas as pl
from jax.experimental.pallas import tpu as pltpu

F32 = jnp.float32
BF = jnp.bfloat16
SDS = jax.ShapeDtypeStruct
MESH = pl.DeviceIdType.MESH

D_MODEL = 1024
MEM_LEN = 256
MEM_HEADS = 4
HEAD_DIM = 128
GM_WIDTH = 512
GM_CHUNK = 128
GM_GROUPS = 4
MLA_HEADS = 8
MLA_NOPE = 128
MLA_ROPE = 64
MLA_V = 128
Q_LORA = 384
KV_LORA = 256
ROPE_BASE = 10000.0
D_FF = 4096
EPS = 1e-6
W_IN_COLS = 5312
ADAM_LR, ADAM_B1, ADAM_B2, ADAM_EPS, ADAM_WD, ADAM_STEP = 0.001, 0.9, 0.999, 1e-08, 0.01, 10

ZG, ZU, ZV, QM, CQ, KPE, CKV = 0, 3072, 3584, 4096, 4608, 4992, 5120
Z_COLS = 5376
LANES = 128
ROW_TILE = 512
ATT_TILE = 1024
ATT_HEADS = 2
VMEM_LIMIT = 60 * 1024 * 1024

N_CHIPS = 4
PIECE_ROWS = 256
SMALL_ROWS = 560

BIG = ["w_in", "w_uq", "w_ukv", "w_mem_kv", "w_o_gm", "w_o_mla", "w_o_mem", "w_out", "w_ff1", "w_ff2"]
BIG_SHAPE = {"w_in": (1024, 5312), "w_uq": (384, 1536), "w_ukv": (256, 2048), "w_mem_kv": (1024, 1024),
             "w_o_gm": (512, 1024), "w_o_mla": (1024, 1024), "w_o_mem": (512, 1024), "w_out": (1024, 1024),
             "w_ff1": (1024, 4096), "w_ff2": (4096, 1024)}
COL_SHARDED = {"w_in", "w_uq", "w_ukv", "w_o_gm", "w_o_mem", "w_ff1"}
EARLY = ["w_in", "w_uq", "w_ukv", "w_mem_kv"]
LATE_PROJ = ["w_o_gm", "w_o_mla", "w_o_mem", "w_out"]
LATE_FF = ["w_ff1", "w_ff2"]
LATE = LATE_PROJ + LATE_FF
SMALL = ["w_spatial", "b_spatial", "g_mix", "g_cq", "g_ckv", "g_q_nope", "g_q_pe", "g_k_nope", "g_k_pe", "g_gm_ln",
         "b_gm_ln", "g_mem", "g_mq", "g_mk", "g_ffn"]
SMALL_SHAPE = {"g_mix": (1, 1024), "g_cq": (1, 384), "g_ckv": (1, 256), "g_q_nope": (1, 128), "g_q_pe": (1, 64),
               "g_k_nope": (1, 128), "g_k_pe": (1, 64), "g_gm_ln": (1, 512), "b_gm_ln": (1, 512),
               "w_spatial": (1, 4, 128, 128), "b_spatial": (1, 4, 128), "g_mem": (1, 1024), "g_mq": (1, 128),
               "g_mk": (1, 128), "g_ffn": (1, 1024)}
WEIGHTS = ['g_mix', 'w_in', 'g_cq', 'w_uq', 'g_ckv', 'w_ukv', 'g_q_nope', 'g_q_pe', 'g_k_nope', 'g_k_pe',
           'g_gm_ln', 'b_gm_ln', 'w_spatial', 'b_spatial', 'g_mem', 'w_mem_kv', 'g_mq', 'g_mk', 'w_o_gm',
           'w_o_mla', 'w_o_mem', 'w_out', 'g_ffn', 'w_ff1', 'w_ff2']


def _params(sem=None):
    return pltpu.CompilerParams(vmem_limit_bytes=VMEM_LIMIT, dimension_semantics=sem)


def _pick(n, prefs):
    for p in prefs:
        if n % p == 0:
            return p
    return n


def _full(shape):
    nd = len(shape)
    return pl.BlockSpec(shape, lambda *_: (0,) * nd)


def _rows(t, w, blk=0):
    return pl.BlockSpec((t, w), lambda i: (i, blk))


def _acc(ref, val, first):
    @pl.when(first)
    def _():
        ref[...] = val

    @pl.when(jnp.logical_not(first))
    def _():
        ref[...] += val


ANY = pl.BlockSpec(memory_space=pl.ANY)


class _Phase:
    def __init__(self, operands, out_shapes, n_sem, n_local, copies, aliases=None):
        self.operands, self.out_shapes, self.aliases = list(operands), list(out_shapes), dict(aliases or {})
        self.n_sem, self.n_local, self.copies = n_sem, max(n_local, 1), copies

    def sem_shapes(self):
        return [pltpu.SemaphoreType.DMA((self.n_sem,)), pltpu.SemaphoreType.DMA((self.n_sem,)),
                pltpu.SemaphoreType.DMA((self.n_local,))]

    def start(self, ins, outs, send, recv, local):
        sends, _, locals_ = self.copies(ins, outs, send, recv, local)
        for cp in locals_ + sends:
            cp.start()

    def finish(self, ins, outs, send, recv, local):
        sends, arrivals, locals_ = self.copies(ins, outs, send, recv, local)
        for cp in arrivals:
            cp.wait_recv()
        for cp in sends:
            cp.wait_send()
        for cp in locals_:
            cp.wait()


class _Shifted:
    def __init__(self, ref, base):
        self.ref, self.base = ref, base

    @property
    def at(self):
        return self

    def __getitem__(self, i):
        return self.ref.at[i + self.base]


def _together(first, second):
    n_in, n_out = len(first.operands), len(first.out_shapes)

    def copies(ins, outs, send, recv, local):
        a = first.copies(ins[:n_in], outs[:n_out], send, recv, local)
        b = second.copies(ins[n_in:], outs[n_out:], _Shifted(send, first.n_sem), _Shifted(recv, first.n_sem),
                          _Shifted(local, first.n_local))
        return a[0] + b[0], a[1] + b[1], a[2] + b[2]

    aliases = {**first.aliases, **{n_in + i: n_out + j for i, j in second.aliases.items()}}
    return _Phase(first.operands + second.operands, first.out_shapes + second.out_shapes, first.n_sem + second.n_sem,
                  first.n_local + second.n_local, copies, aliases)


def _run_phase(phase, name):
    n_in = len(phase.operands)

    def body(*refs):
        ins, outs, sems = refs[:n_in], refs[n_in:n_in + len(phase.out_shapes)], refs[n_in + len(phase.out_shapes):]
        phase.start(ins, outs, *sems)
        phase.finish(ins, outs, *sems)

    return pl.pallas_call(body, in_specs=[ANY] * n_in, out_specs=[ANY] * len(phase.out_shapes),
                          out_shape=phase.out_shapes, scratch_shapes=phase.sem_shapes(),
                          input_output_aliases=phase.aliases, name=name)(*phase.operands)


def _pcall(body, *, grid, in_specs, out_specs, out_shape, scratch_shapes=(), sem=None, name, comm=None, aliases=None):
    single = not isinstance(out_shape, (list, tuple))
    o_specs = [out_specs] if single else list(out_specs)
    o_shape = [out_shape] if single else list(out_shape)
    aliases = dict(aliases or {})
    if comm is None:
        call = pl.pallas_call(body, grid=grid, in_specs=list(in_specs), out_specs=o_specs, out_shape=o_shape,
                              scratch_shapes=list(scratch_shapes), input_output_aliases=aliases,
                              compiler_params=_params(sem), name=name)

        def run_plain(*args):
            res = call(*args)
            return res[0] if single else res

        return run_plain

    n_in, n_out, n_scr = len(in_specs), len(o_specs), len(scratch_shapes)
    nc_in, nc_out = len(comm.operands), len(comm.out_shapes)

    def wrapped(*refs):
        ins, cins = refs[:n_in], refs[n_in:n_in + nc_in]
        o0 = n_in + nc_in
        outs, couts = refs[o0:o0 + n_out], refs[o0 + n_out:o0 + n_out + nc_out]
        s0 = o0 + n_out + nc_out
        scr, csem = refs[s0:s0 + n_scr], refs[s0 + n_scr:]
        ids = [pl.program_id(d) for d in range(len(grid))]
        first = functools.reduce(jnp.logical_and, [i == 0 for i in ids])
        last = functools.reduce(jnp.logical_and, [i == g - 1 for i, g in zip(ids, grid)])

        @pl.when(first)
        def _():
            comm.start(cins, couts, *csem)

        body(*ins, *outs, *scr)

        @pl.when(last)
        def _():
            comm.finish(cins, couts, *csem)

    call = pl.pallas_call(
        wrapped, grid=grid, in_specs=list(in_specs) + [ANY] * nc_in, out_specs=o_specs + [ANY] * nc_out,
        out_shape=o_shape + comm.out_shapes, scratch_shapes=list(scratch_shapes) + comm.sem_shapes(),
        input_output_aliases={**aliases, **{n_in + i: n_out + j for i, j in comm.aliases.items()}},
        compiler_params=_params(("arbitrary",) * len(grid)), name=name)

    def run_carrying(*args):
        res = call(*args, *comm.operands)
        return (res[0] if single else res[:n_out]), res[n_out:]

    return run_carrying


def _dn(a, b, ca, cb):
    return lax.dot_general(a.astype(BF), b.astype(BF), (((ca,), (cb,)), ((), ())), preferred_element_type=F32)


@jax.custom_vjp
def _mm_nn(a, b):
    return _dn(a, b, 1, 0)


def _mm_nn_fwd(a, b):
    return _dn(a, b, 1, 0), (a.astype(BF), b.astype(BF))


def _mm_nn_bwd(res, ct):
    a, b = res
    return _dn(ct, b, 1, 1), _dn(a, ct, 0, 0)


_mm_nn.defvjp(_mm_nn_fwd, _mm_nn_bwd)


@jax.custom_vjp
def _mm_nt(a, b):
    return _dn(a, b, 1, 1)


def _mm_nt_fwd(a, b):
    return _dn(a, b, 1, 1), (a.astype(BF), b.astype(BF))


def _mm_nt_bwd(res, ct):
    a, b = res
    return _dn(ct, b, 1, 0), _dn(ct, a, 0, 0)


_mm_nt.defvjp(_mm_nt_fwd, _mm_nt_bwd)


def _rmsn(x, g, n):
    ms = jnp.sum(x * x, axis=-1, keepdims=True) * (1.0 / n)
    return x * lax.rsqrt(ms + EPS) * g


def _layernorm(x, g, b):
    mu = jnp.mean(x, axis=-1, keepdims=True)
    xc = x - mu
    y = xc * lax.rsqrt(jnp.mean(xc * xc, axis=-1, keepdims=True) + EPS)
    return y * g + b


def _swap_lanes(x):
    half = MLA_ROPE // 2
    lane = lax.broadcasted_iota(jnp.int32, x.shape, 1)
    return jnp.where(lane < half, pltpu.roll(x, LANES - half, axis=1),
                     jnp.where(lane < MLA_ROPE, pltpu.roll(x, half, axis=1), 0.0))


@jax.custom_vjp
def _swap_halves(x):
    return _swap_lanes(x)


_swap_halves.defvjp(lambda x: (_swap_lanes(x), None), lambda _, ct: (_swap_lanes(ct),))


def _rope(x, cos_f, sin_s):
    return x * cos_f + _swap_halves(x) * sin_s


def _lane_blocks(x):
    return tuple(x[:, i * LANES:(i + 1) * LANES] for i in range(x.shape[1] // LANES))


@jax.custom_vjp
def _split_lanes(x):
    return _lane_blocks(x)


_split_lanes.defvjp(lambda x: (_lane_blocks(x), None), lambda _, cts: (jnp.concatenate(cts, axis=1),))


def _softmax(s):
    m = lax.stop_gradient(jnp.max(s, axis=-1, keepdims=True))
    p = jnp.exp(s - m)
    return p / jnp.sum(p, axis=-1, keepdims=True)


def _mm(a, b, *, ta=False, tb=False, ins=(), epilogue=None, out_dtypes=(F32,), owner_cols=None, total=False, name,
        comm=None, rows=None, into=None):
    if ta:
        k_dim, m = a.shape
    else:
        m, k_dim = a.shape
    if tb:
        n, kb = b.shape
    else:
        kb, n = b.shape
    assert k_dim == kb, (a.shape, b.shape, ta, tb)
    part, n_parts = rows if rows is not None else (0, 1)
    tm = _pick(m // n_parts, (1024, 512, 256, 128))
    tn = _pick(n if owner_cols is None else owner_cols, (1024, 768, 512, 384, 256, 128))
    tk = _pick(k_dim, (2048, 1024, 768, 512, 256, 128))
    nk = k_dim // tk
    m_steps = m // tm // n_parts
    off = part * m_steps
    ca = 0 if ta else 1
    cb = 1 if tb else 0
    n_in = len(ins)
    n_out = len(out_dtypes)
    n_pass = 0 if into is None else 1

    def finish(r, in_refs, out_refs, first_tile):
        vals = epilogue(r, *[ref[...].astype(F32) for ref in in_refs]) if epilogue is not None else (r,)
        for ref, val, dt in zip(out_refs, vals, out_dtypes):
            ref[...] = val.astype(dt)
        if total:
            _acc(out_refs[n_out], vals[n_out], first_tile)

    def body(*refs):
        a_ref, b_ref = refs[:2]
        in_refs = refs[2:2 + n_in]
        o0 = 2 + n_in + n_pass
        out_refs = refs[o0:o0 + n_out + int(total)]
        first_tile = jnp.logical_and(pl.program_id(0) == 0, pl.program_id(1) == 0)
        part = _dn(a_ref[...], b_ref[...], ca, cb)
        if nk == 1:
            finish(part, in_refs, out_refs, first_tile)
            return
        acc = refs[-1]
        k = pl.program_id(2)
        _acc(acc, part, k == 0)

        @pl.when(k == nk - 1)
        def _():
            finish(acc[...], in_refs, out_refs, first_tile)

    a_spec = (pl.BlockSpec((tk, tm), lambda i, j, k: (k, i + off)) if ta
              else pl.BlockSpec((tm, tk), lambda i, j, k: (i + off, k)))
    b_spec = pl.BlockSpec((tn, tk), lambda i, j, k: (j, k)) if tb else pl.BlockSpec((tk, tn), lambda i, j, k: (k, j))
    t_spec = pl.BlockSpec((tm, tn), lambda i, j, k: (i + off, j))
    if owner_cols is None:
        o_spec, o_shape = t_spec, (m, n)
    else:
        per = owner_cols // tn
        o_spec = pl.BlockSpec((None, tm, tn), lambda i, j, k: (j // per, i + off, j % per))
        o_shape = (n // owner_cols, m, owner_cols)
    o_specs = [o_spec] * n_out + ([pl.BlockSpec((8, LANES), lambda i, j, k: (0, 0))] if total else [])
    o_shapes = [SDS(o_shape, dt) for dt in out_dtypes] + ([SDS((8, LANES), F32)] if total else [])
    in_specs = [a_spec, b_spec] + [t_spec] * n_in + [ANY] * n_pass
    args = [a, b, *ins] + ([into] if n_pass else [])
    run = _pcall(body, grid=(m_steps, n // tn, nk), in_specs=in_specs, out_specs=o_specs, out_shape=o_shapes,
                 scratch_shapes=[pltpu.VMEM((tm, tn), F32)] if nk > 1 else [],
                 sem=("arbitrary",) * 3 if total else ("parallel", "parallel", "arbitrary"), name=name, comm=comm,
                 aliases={len(in_specs) - 1: 0} if n_pass else None)
    if comm is None:
        outs = run(*args)
        return outs[0] if len(outs) == 1 else outs
    outs, exchanged = run(*args)
    return (outs[0] if len(outs) == 1 else outs), exchanged


def _add_to(r, x):
    return (r + x,)


def _relu2(r):
    p = jnp.maximum(r, 0.0)
    return r, p * p


def _relu2_bwd(dr, a):
    return (dr * (2.0 * jnp.maximum(a, 0.0)),)


def _loss_tail(r, x1, tgt):
    e = (r + x1) - tgt
    dy = e * (1.0 / D_MODEL)
    part = jnp.sum(jnp.sum(e * e, axis=-1, keepdims=True), axis=0, keepdims=True) * (0.5 / D_MODEL)
    return dy, dy, jnp.broadcast_to(part, (8, LANES))


def _rms_fwd(x, g, name, comm=None):
    n, w = x.shape
    t = min(ROW_TILE, n)

    def body(x_ref, g_ref, o_ref):
        o_ref[...] = _rmsn(x_ref[...], g_ref[...], w).astype(BF)

    return _pcall(body, grid=(n // t,), in_specs=[_rows(t, w), _full((1, w))], out_specs=_rows(t, w),
                  out_shape=SDS((n, w), BF), sem=("arbitrary",), name=name, comm=comm)(x, g)


def _rms_bwd(x, g, dh, res, name, comm=None):
    n, w = x.shape
    t = min(ROW_TILE, n)
    has_res = res is not None

    def body(*refs):
        if has_res:
            x_ref, g_ref, dh_ref, res_ref, dx_ref, dxb_ref, dg_ref = refs
        else:
            x_ref, g_ref, dh_ref, dx_ref, dxb_ref, dg_ref = refs
        _, vjp = jax.vjp(lambda xx, gg: _rmsn(xx, gg, w), x_ref[...], g_ref[...])
        dx, dg = vjp(dh_ref[...])
        if has_res:
            dx = dx + res_ref[...]
        dx_ref[...] = dx
        dxb_ref[...] = dx.astype(BF)
        _acc(dg_ref, dg, pl.program_id(0) == 0)

    in_specs = [_rows(t, w), _full((1, w)), _rows(t, w)] + ([_rows(t, w)] if has_res else [])
    args = [x, g, dh] + ([res] if has_res else [])
    return _pcall(body, grid=(n // t,), in_specs=in_specs, out_specs=[_rows(t, w), _rows(t, w), _full((1, w))],
                  out_shape=[SDS((n, w), F32), SDS((n, w), BF), SDS((1, w), F32)], sem=("arbitrary",), name=name,
                  comm=comm)(*args)


def _merge_core(zg0, zg1, zg2, y0, y1, y2):
    return jax.nn.sigmoid(zg0) * y0 + jax.nn.sigmoid(zg1) * y1 + jax.nn.sigmoid(zg2) * y2


def _merge_fwd(z, y_gm, y_mla, y_mem, name):
    n = z.shape[0]
    t = min(ROW_TILE, n)
    w = D_MODEL

    def body(g0, g1, g2, y0, y1, y2, o_ref):
        o_ref[...] = _merge_core(g0[...].astype(F32), g1[...].astype(F32), g2[...].astype(F32), y0[...].astype(F32), y1[...].astype(F32),
                                 y2[...].astype(F32)).astype(BF)

    return pl.pallas_call(body, grid=(n // t,),
                          in_specs=[_rows(t, w, 0), _rows(t, w, 1), _rows(t, w, 2)] + [_rows(t, w)] * 3,
                          out_specs=_rows(t, w), out_shape=SDS((n, w), BF),
                          compiler_params=_params(("parallel",)), name=name)(z, z, z, y_gm, y_mla, y_mem)


def _merge_bwd(z, y_gm, y_mla, y_mem, dmerged, name):
    n = z.shape[0]
    t = min(ROW_TILE, n)
    w = D_MODEL

    def body(g0, g1, g2, y0, y1, y2, dm, dzg_ref, d0_ref, d1_ref, d2_ref):
        _, vjp = jax.vjp(_merge_core, g0[...].astype(F32), g1[...].astype(F32), g2[...].astype(F32), y0[...].astype(F32), y1[...].astype(F32),
                         y2[...].astype(F32))
        dg0, dg1, dg2, dy0, dy1, dy2 = vjp(dm[...])
        dzg_ref[:, 0:w] = dg0.astype(BF)
        dzg_ref[:, w:2 * w] = dg1.astype(BF)
        dzg_ref[:, 2 * w:3 * w] = dg2.astype(BF)
        d0_ref[...] = dy0.astype(BF)
        d1_ref[...] = dy1.astype(BF)
        d2_ref[...] = dy2.astype(BF)

    return pl.pallas_call(body, grid=(n // t,),
                          in_specs=[_rows(t, w, 0), _rows(t, w, 1), _rows(t, w, 2)] + [_rows(t, w)] * 4,
                          out_specs=[_rows(t, 3 * w, ZG // (3 * w))] + [_rows(t, w)] * 3,
                          out_shape=[SDS((n, Z_COLS), BF)] + [SDS((n, w), BF)] * 3,
                          compiler_params=_params(("parallel",)), name=name)(z, z, z, y_gm, y_mla, y_mem, dmerged)


def _gm_core(zu, zv, g_ln, b_ln, ws, bcols):
    t = zu.shape[0]
    u = jax.nn.gelu(zu)
    v = _layernorm(jax.nn.gelu(zv), g_ln, b_ln)
    row = lax.broadcasted_iota(jnp.int32, (GM_CHUNK, GM_CHUNK), 0)
    col = lax.broadcasted_iota(jnp.int32, (GM_CHUNK, GM_CHUNK), 1)
    wc = [jnp.where(row >= col, ws[g], 0.0) for g in range(GM_GROUPS)]
    chunks = []
    for c in range(t // GM_CHUNK):
        cols = []
        for g in range(GM_GROUPS):
            vc = v[c * GM_CHUNK:(c + 1) * GM_CHUNK, g * LANES:(g + 1) * LANES]
            cols.append(_mm_nn(wc[g], vc) + bcols[g])
        chunks.append(jnp.concatenate(cols, axis=1))
    mixed = chunks[0] if len(chunks) == 1 else jnp.concatenate(chunks, axis=0)
    return u * mixed


def _gm_specs(t):
    return [_rows(t, GM_WIDTH, ZU // GM_WIDTH), _rows(t, GM_WIDTH, ZV // GM_WIDTH), _full((1, GM_WIDTH)),
            _full((1, GM_WIDTH)), _full((GM_GROUPS, GM_CHUNK, GM_CHUNK))] + [_full((GM_CHUNK, 1))] * GM_GROUPS


def _gm_fwd(z, g_ln, b_ln, ws, bcols, name):
    n = z.shape[0]
    t = min(ROW_TILE, n)

    def body(zu, zv, g_ref, b_ref, ws_ref, c0, c1, c2, c3, o_ref):
        out = _gm_core(zu[...].astype(F32), zv[...].astype(F32), g_ref[...], b_ref[...], [ws_ref[g] for g in range(GM_GROUPS)],
                       [c0[...], c1[...], c2[...], c3[...]])
        o_ref[...] = out.astype(BF)

    return pl.pallas_call(body, grid=(n // t,), in_specs=_gm_specs(t), out_specs=_rows(t, GM_WIDTH),
                          out_shape=SDS((n, GM_WIDTH), BF), compiler_params=_params(("parallel",)),
                          name=name)(z, z, g_ln, b_ln, ws, *bcols)


def _gm_bwd(z, g_ln, b_ln, ws, bcols, dgm, dz, name, comm=None):
    n = z.shape[0]
    t = min(ROW_TILE, n)

    def body(zu, zv, g_ref, b_ref, ws_ref, c0, c1, c2, c3, dgm_ref, _, dz_ref, dg_ref, db_ref, dws_ref, e0, e1, e2,
             e3):
        first = pl.program_id(0) == 0
        _, vjp = jax.vjp(_gm_core, zu[...].astype(F32), zv[...].astype(F32), g_ref[...], b_ref[...],
                         [ws_ref[g] for g in range(GM_GROUPS)], [c0[...], c1[...], c2[...], c3[...]])
        dzu, dzv, dg, db, dws, dcols = vjp(dgm_ref[...])
        dz_ref[:, 0:GM_WIDTH] = dzu.astype(BF)
        dz_ref[:, GM_WIDTH:2 * GM_WIDTH] = dzv.astype(BF)
        _acc(dg_ref, dg, first)
        _acc(db_ref, db, first)
        _acc(dws_ref, jnp.stack(dws, axis=0), first)
        for ref, val in zip((e0, e1, e2, e3), dcols):
            _acc(ref, val, first)

    in_specs = _gm_specs(t) + [_rows(t, GM_WIDTH), ANY]
    return _pcall(
        body, grid=(n // t,), in_specs=in_specs,
        out_specs=[_rows(t, 2 * GM_WIDTH, ZU // (2 * GM_WIDTH)), _full((1, GM_WIDTH)), _full((1, GM_WIDTH)),
                   _full((GM_GROUPS, GM_CHUNK, GM_CHUNK))] + [_full((GM_CHUNK, 1))] * GM_GROUPS,
        out_shape=[SDS((n, Z_COLS), BF), SDS((1, GM_WIDTH), F32), SDS((1, GM_WIDTH), F32),
                   SDS((GM_GROUPS, GM_CHUNK, GM_CHUNK), F32)] + [SDS((GM_CHUNK, 1), F32)] * GM_GROUPS,
        sem=("arbitrary",), name=name, comm=comm, aliases={len(in_specs) - 1: 0})(z, z, g_ln, b_ln, ws, *bcols, dgm, dz)


def _rope_tables(pos_f, inv_full, cmask, smask, name, comm=None):
    n = pos_f.shape[0]
    t = min(ROW_TILE, n)

    def body(p_ref, inv_ref, cm_ref, sm_ref, cos_ref, sin_ref):
        ang = p_ref[...] * inv_ref[...]
        cos_ref[...] = jnp.cos(ang) * cm_ref[...]
        sin_ref[...] = jnp.sin(ang) * sm_ref[...]

    return _pcall(body, grid=(n // t,), in_specs=[_rows(t, 1)] + [_full((1, LANES))] * 3,
                  out_specs=[_rows(t, LANES)] * 2, out_shape=[SDS((n, LANES), F32)] * 2, sem=("parallel",),
                  name=name, comm=comm)(pos_f, inv_full, cmask, smask)


def _prep_norms(cq, ckv, g_cq, g_ckv):
    return _rmsn(cq, g_cq, Q_LORA), _rmsn(ckv, g_ckv, KV_LORA)


def _prep_heads(qa, kva, kpe, head_gains, cos_f, sin_s):
    g_qn, g_qp, g_kn, g_kp = head_gains
    qs = _split_lanes(qa)
    kvs = _split_lanes(kva)
    kp = _rope(_rmsn(kpe, g_kp, MLA_ROPE), cos_f, sin_s)
    q_out, k_out = [], []
    for h in range(MLA_HEADS):
        q_out.append(_rmsn(qs[h], g_qn, MLA_NOPE))
        q_out.append(_rope(_rmsn(qs[MLA_HEADS + h], g_qp, MLA_ROPE), cos_f, sin_s))
        k_out.append(_rmsn(kvs[h], g_kn, MLA_NOPE))
        k_out.append(kp)
    return (jnp.concatenate(q_out, axis=1), jnp.concatenate(k_out, axis=1),
            jnp.concatenate(kvs[MLA_HEADS:], axis=1))


def _prep_in_specs(t):
    return ([_rows(t, Q_LORA, CQ // Q_LORA), _rows(t, LANES, KPE // LANES), _rows(t, KV_LORA, CKV // KV_LORA),
             _rows(t, LANES), _rows(t, LANES), _full((1, Q_LORA)), _full((1, KV_LORA))] + [_full((1, LANES))] * 4
            + [_full((Q_LORA, 2048)), _full((KV_LORA, 2048))])


def _prep_fwd(z, cos_f, sin_s, gains, wq, wkv, name):
    n = z.shape[0]
    t = min(ROW_TILE, n)

    def body(cq, kpe, ckv, cos_ref, sin_ref, g_cq, g_ckv, g_qn, g_qp, g_kn, g_kp, wq_ref, wkv_ref, q_ref, k_ref, v_ref):
        cqn, ckvn = _prep_norms(cq[...].astype(F32), ckv[...].astype(F32), g_cq[...], g_ckv[...])
        qa = _dn(cqn, wq_ref[...], 1, 0)
        kva = _dn(ckvn, wkv_ref[...], 1, 0)
        q, k, v = _prep_heads(qa, kva, kpe[...].astype(F32), (g_qn[...], g_qp[...], g_kn[...], g_kp[...]), cos_ref[...],
                              sin_ref[...])
        q_ref[...] = q.astype(BF)
        k_ref[...] = k.astype(BF)
        v_ref[...] = v.astype(BF)

    return pl.pallas_call(body, grid=(n // t,), in_specs=_prep_in_specs(t),
                          out_specs=[_rows(t, 2048), _rows(t, 2048), _rows(t, 1024)],
                          out_shape=[SDS((n, 2048), BF), SDS((n, 2048), BF), SDS((n, 1024), BF)],
                          compiler_params=_params(("parallel",)),
                          name=name)(z, z, z, cos_f, sin_s, *gains, wq, wkv)


def _prep_bwd(z, cos_f, sin_s, gains, wq, wkv, dq, dk, dv, dz, name, comm=None):
    n = z.shape[0]
    t = min(ROW_TILE, n)
    wz = Q_LORA + LANES + KV_LORA

    def body(cq, kpe, ckv, cos_ref, sin_ref, g_cq, g_ckv, g_qn, g_qp, g_kn, g_kp, wq_ref, wkv_ref, dq_ref, dk_ref,
             dv_ref, _, dz_ref, o_cq, o_ckv, o_qn, o_qp, o_kn, o_kp, dwq_ref, dwkv_ref):
        first = pl.program_id(0) == 0
        cos_t, sin_t = cos_ref[...], sin_ref[...]
        (cqn, ckvn), vjp_norms = jax.vjp(_prep_norms, cq[...].astype(F32), ckv[...].astype(F32), g_cq[...], g_ckv[...])
        wq_t, wkv_t = wq_ref[...], wkv_ref[...]
        qa = _dn(cqn, wq_t, 1, 0)
        kva = _dn(ckvn, wkv_t, 1, 0)
        _, vjp_heads = jax.vjp(lambda a, b, c, g: _prep_heads(a, b, c, g, cos_t, sin_t), qa, kva, kpe[...].astype(F32),
                               (g_qn[...], g_qp[...], g_kn[...], g_kp[...]))
        dqa, dkva, dkpe, dhead = vjp_heads((dq_ref[...], dk_ref[...], dv_ref[...]))
        _acc(dwq_ref, _dn(cqn, dqa, 0, 0), first)
        _acc(dwkv_ref, _dn(ckvn, dkva, 0, 0), first)
        dcq, dckv, dg_cq, dg_ckv = vjp_norms((_dn(dqa, wq_t, 1, 1), _dn(dkva, wkv_t, 1, 1)))
        dz_ref[:, 0:Q_LORA] = dcq.astype(BF)
        dz_ref[:, Q_LORA:Q_LORA + LANES] = dkpe.astype(BF)
        dz_ref[:, Q_LORA + LANES:wz] = dckv.astype(BF)
        for ref, val in zip((o_cq, o_ckv, o_qn, o_qp, o_kn, o_kp), (dg_cq, dg_ckv) + tuple(dhead)):
            _acc(ref, val, first)

    gain_specs = [_full((1, Q_LORA)), _full((1, KV_LORA))] + [_full((1, LANES))] * 4
    gain_shapes = [SDS((1, Q_LORA), F32), SDS((1, KV_LORA), F32)] + [SDS((1, LANES), F32)] * 4
    in_specs = _prep_in_specs(t) + [_rows(t, 2048), _rows(t, 2048), _rows(t, 1024), ANY]
    return _pcall(
        body, grid=(n // t,), in_specs=in_specs,
        out_specs=[_rows(t, wz, CQ // wz)] + gain_specs + [_full((Q_LORA, 2048)), _full((KV_LORA, 2048))],
        out_shape=[SDS((n, Z_COLS), BF)] + gain_shapes + [SDS((Q_LORA, 2048), F32), SDS((KV_LORA, 2048), F32)],
        sem=("arbitrary",), name=name, comm=comm,
        aliases={len(in_specs) - 1: 0})(z, z, z, cos_f, sin_s, *gains, wq, wkv, dq, dk, dv, dz)


MLA_QK = 256
MLA_SCALE = 1.0 / math.sqrt(MLA_NOPE + MLA_ROPE)
LOG2E = 1.0 / math.log(2.0)
MLA_SCALE_LOG2E = MLA_SCALE * LOG2E


def _causal_mask(s, q0, k0):
    tq, tk = s.shape
    row = q0 + lax.broadcasted_iota(jnp.int32, (tq, tk), 0)
    col = k0 + lax.broadcasted_iota(jnp.int32, (tq, tk), 1)
    return jnp.where(row >= col, s, -jnp.inf)


def _mla_fwd(q, k, v, batch, seq, name, comm=None):
    n = q.shape[0]
    tq = min(ATT_TILE, seq)
    nq = seq // tq

    nh = ATT_HEADS

    def body(q_ref, k_ref, v_ref, o_ref, lse_ref):
        i = pl.program_id(2)

        def step(j, carry, diagonal=False):
            k0 = pl.multiple_of(j * tq, tq)
            out = []
            ones = jnp.ones((tq, LANES), BF)
            for hh in range(nh):
                m, acc = carry[hh]
                qb = q_ref[:, hh * MLA_QK:(hh + 1) * MLA_QK]
                kb = k_ref[pl.ds(k0, tq), hh * MLA_QK:(hh + 1) * MLA_QK]
                vb = v_ref[pl.ds(k0, tq), hh * MLA_V:(hh + 1) * MLA_V]
                s = _dn(qb, kb, 1, 1)
                if diagonal:
                    s = _causal_mask(s, i * tq, k0)
                m_new = jnp.maximum(m, jnp.max(s, axis=-1, keepdims=True))
                p = jnp.exp2((s - m_new) * MLA_SCALE_LOG2E)
                alpha = jnp.exp2((m - m_new) * MLA_SCALE_LOG2E)
                acc = alpha * acc + _dn(p, jnp.concatenate([vb, ones], axis=1), 1, 0)
                out.append((m_new, acc))
            return tuple(out)

        init = tuple((jnp.full((tq, 1), -jnp.inf, F32), jnp.zeros((tq, MLA_V + LANES), F32)) for _ in range(nh))
        final = step(i, lax.fori_loop(0, i, step, init), diagonal=True)
        for hh, (m, acc) in enumerate(final):
            l = acc[:, MLA_V:MLA_V + 1]
            o_ref[:, hh * MLA_V:(hh + 1) * MLA_V] = acc[:, :MLA_V] / l
            lse_ref[:, hh * LANES:(hh + 1) * LANES] = jnp.broadcast_to(m * MLA_SCALE + jnp.log(l), (tq, LANES))

    return _pcall(
        body, grid=(batch, MLA_HEADS // nh, nq),
        in_specs=[pl.BlockSpec((tq, nh * MLA_QK), lambda b, h, i: (b * nq + i, h)),
                  pl.BlockSpec((seq, nh * MLA_QK), lambda b, h, i: (b, h)),
                  pl.BlockSpec((seq, nh * MLA_V), lambda b, h, i: (b, h))],
        out_specs=[pl.BlockSpec((tq, nh * MLA_V), lambda b, h, i: (b * nq + i, h)),
                   pl.BlockSpec((tq, nh * LANES), lambda b, h, i: (b * nq + i, h))],
        out_shape=[SDS((n, MLA_HEADS * MLA_V), F32), SDS((n, MLA_HEADS * LANES), F32)],
        sem=("parallel", "parallel", "arbitrary"), name=name, comm=comm)(q, k, v)


def _mla_bwd(q, k, v, o, lse, do, batch, seq, name, comm=None):
    n = q.shape[0]
    tk = min(ATT_TILE, seq)
    nk = seq // tk

    nh = ATT_HEADS

    def body(q_ref, k_ref, v_ref, o_ref, lse_ref, do_ref, dq_ref, dk_ref, dv_ref):
        jk = pl.program_id(2)

        @pl.when(jk == 0)
        def _():
            dq_ref[...] = jnp.zeros_like(dq_ref)

        def step(i, carry, diagonal=False):
            q0 = pl.multiple_of(i * tk, tk)
            rows = pl.ds(q0, tk)
            out = []
            for hh in range(nh):
                dk_acc, dv_acc = carry[hh]
                qk_cols = slice(hh * MLA_QK, (hh + 1) * MLA_QK)
                v_cols = slice(hh * MLA_V, (hh + 1) * MLA_V)
                kb = k_ref[:, qk_cols]
                vb = v_ref[:, v_cols]
                qb = q_ref[rows, qk_cols]
                dob = do_ref[rows, v_cols]
                delta = jnp.sum(dob * o_ref[rows, v_cols], axis=-1, keepdims=True)
                s = _dn(qb, kb, 1, 1)
                if diagonal:
                    s = _causal_mask(s, q0, jk * tk)
                p = jnp.exp2(s * MLA_SCALE_LOG2E - lse_ref[rows, hh * LANES:hh * LANES + 1] * LOG2E)
                dv_acc = dv_acc + _dn(p, dob, 0, 0)
                dp = _dn(dob, vb, 1, 1)
                ds = p * (dp - delta) * MLA_SCALE
                dk_acc = dk_acc + _dn(ds, qb, 0, 0)
                dq_ref[rows, qk_cols] += _dn(ds, kb, 1, 0)
                out.append((dk_acc, dv_acc))
            return tuple(out)

        init = tuple((jnp.zeros((tk, MLA_QK), F32), jnp.zeros((tk, MLA_V), F32)) for _ in range(nh))
        final = lax.fori_loop(jk + 1, nk, step, step(jk, init, diagonal=True))
        for hh, (dk_acc, dv_acc) in enumerate(final):
            dk_ref[:, hh * MLA_QK:(hh + 1) * MLA_QK] = dk_acc
            dv_ref[:, hh * MLA_V:(hh + 1) * MLA_V] = dv_acc

    full_qk = pl.BlockSpec((seq, nh * MLA_QK), lambda b, h, j: (b, h))
    full_v = pl.BlockSpec((seq, nh * MLA_V), lambda b, h, j: (b, h))
    blk_qk = pl.BlockSpec((tk, nh * MLA_QK), lambda b, h, j: (b * nk + j, h))
    blk_v = pl.BlockSpec((tk, nh * MLA_V), lambda b, h, j: (b * nk + j, h))
    return _pcall(
        body, grid=(batch, MLA_HEADS // nh, nk),
        in_specs=[full_qk, blk_qk, blk_v, full_v, full_v, full_v],
        out_specs=[full_qk, blk_qk, blk_v],
        out_shape=[SDS((n, MLA_HEADS * MLA_QK), F32), SDS((n, MLA_HEADS * MLA_QK), F32),
                   SDS((n, MLA_HEADS * MLA_V), F32)],
        sem=("parallel", "parallel", "arbitrary"), name=name, comm=comm)(q, k, v, o, lse, do)


MEM_SCALE = 1.0 / math.sqrt(HEAD_DIM)
MEM_W = MEM_HEADS * HEAD_DIM


def _mem_core(qs, ks, vs, g_mq, g_mk):
    outs = []
    for h in range(MEM_HEADS):
        qh = _rmsn(qs[h], g_mq, HEAD_DIM)
        kh = _rmsn(ks[h], g_mk, HEAD_DIM)
        p = _softmax(_mm_nt(qh, kh) * MEM_SCALE)
        outs.append(_mm_nn(p, vs[h]))
    return jnp.concatenate(outs, axis=1)


def _mem_load(qm, kvm, g_mq, g_mk):
    hs = range(MEM_HEADS)
    qs = [qm[:, h * LANES:(h + 1) * LANES].astype(F32) for h in hs]
    ks = [kvm[:, h * LANES:(h + 1) * LANES] for h in hs]
    vs = [kvm[:, MEM_W + h * LANES:MEM_W + (h + 1) * LANES] for h in hs]
    return qs, ks, vs, g_mq[...], g_mk[...]


def _mem_fwd(z, kvm, g_mq, g_mk, batch, seq, name, comm=None):
    n = z.shape[0]
    t = min(ROW_TILE, seq)
    per = seq // t

    def body(qm, kvm_ref, gq, gk, o_ref):
        o_ref[...] = _mem_core(*_mem_load(qm, kvm_ref, gq, gk)).astype(BF)

    return _pcall(
        body, grid=(n // t,),
        in_specs=[_rows(t, MEM_W, QM // MEM_W), pl.BlockSpec((MEM_LEN, 2 * MEM_W), lambda i: (i // per, 0)),
                  _full((1, LANES)), _full((1, LANES))],
        out_specs=_rows(t, MEM_W), out_shape=SDS((n, MEM_W), BF), sem=("parallel",), name=name,
        comm=comm)(z, kvm, g_mq, g_mk)


def _mem_bwd(z, kvm, g_mq, g_mk, dom, dz, batch, seq, name):
    n = z.shape[0]
    t = min(ROW_TILE, seq)
    per = seq // t

    def body(qm, kvm_ref, gq, gk, dom_ref, _, dz_ref, dkvm_ref, dgq_ref, dgk_ref):
        i = pl.program_id(0)
        _, vjp = jax.vjp(_mem_core, *_mem_load(qm, kvm_ref, gq, gk))
        dqs, dks, dvs, dgq, dgk = vjp(dom_ref[...])
        dz_ref[...] = jnp.concatenate(dqs, axis=1).astype(BF)
        _acc(dkvm_ref, jnp.concatenate(dks + dvs, axis=1), i % per == 0)
        _acc(dgq_ref, dgq, i == 0)
        _acc(dgk_ref, dgk, i == 0)

    kv_spec = pl.BlockSpec((MEM_LEN, 2 * MEM_W), lambda i: (i // per, 0))
    return pl.pallas_call(
        body, grid=(n // t,),
        in_specs=[_rows(t, MEM_W, QM // MEM_W), kv_spec, _full((1, LANES)), _full((1, LANES)), _rows(t, MEM_W), ANY],
        out_specs=[_rows(t, MEM_W, QM // MEM_W), kv_spec, _full((1, LANES)), _full((1, LANES))],
        out_shape=[SDS((n, Z_COLS), BF), SDS((batch * MEM_LEN, 2 * MEM_W), F32), SDS((1, LANES), F32),
                   SDS((1, LANES), F32)],
        input_output_aliases={5: 0},
        compiler_params=_params(("arbitrary",)), name=name)(z, kvm, g_mq, g_mk, dom, dz)


def _me():
    return lax.axis_index("x"), lax.axis_index("y"), lax.axis_index("c")


def _other_chips(x, y):
    return [(1 - x, y), (x, 1 - y), (1 - x, 1 - y)]


def _shard_shape(name):
    r, c = BIG_SHAPE[name]
    return (r, c // N_CHIPS) if name in COL_SHARDED else (r // N_CHIPS, c)


def _n_pieces(half_rows):
    return max(1, half_rows // PIECE_ROWS)


def _piece_plan(shapes):
    plan = []
    for r, _ in shapes:
        h = r // 2
        n = _n_pieces(h)
        plan.append((h, n, h // n))
    return plan


def _remote(send, recv, sem, src, dst, to):
    return pltpu.make_async_remote_copy(src_ref=src, dst_ref=dst, send_sem=send.at[sem], recv_sem=recv.at[sem],
                                        device_id=to, device_id_type=MESH)


def _gather_far(shards):
    plan = _piece_plan([s.shape for s in shards])
    n_far = 3 * sum(n for _, n, _ in plan)
    n_loc = 2 * sum(n for _, n, _ in plan)

    def copies(s_refs, o_refs, send, recv, local):
        x, y, c = _me()
        k = 2 * x + y
        mine, sends, arrivals = [], [], []
        for t, (h, n, pr) in enumerate(plan):
            s_ref, o_ref = s_refs[t], o_refs[t]
            for core in range(2):
                for p in range(n):
                    rows = pl.ds(core * h + p * pr, pr)
                    mine.append(pltpu.make_async_copy(s_ref.at[rows], o_ref.at[k, rows], local.at[len(mine)]))
            for chip in _other_chips(x, y):
                for p in range(n):
                    rows = pl.ds(c * h + p * pr, pr)
                    s = len(sends)
                    sends.append(_remote(send, recv, s, s_ref.at[rows], o_ref.at[k, rows], (*chip, c)))
                    arrivals.append(_remote(send, recv, s, s_ref.at[rows], o_ref.at[2 * chip[0] + chip[1], rows],
                                            (*chip, c)))
        return sends, arrivals, mine

    return _Phase(shards, [SDS((N_CHIPS,) + s.shape, s.dtype) for s in shards], n_far, n_loc, copies)


def _gather_near(bufs):
    plan = _piece_plan([b.shape[1:] for b in bufs])
    n_sem = 3 * sum(n for _, n, _ in plan)

    def copies(i_refs, o_refs, send, recv, local):
        x, y, c = _me()
        sib = (x, y, 1 - c)
        sends, arrivals = [], []
        for t, (h, n, pr) in enumerate(plan):
            for chip in _other_chips(x, y):
                ci = 2 * chip[0] + chip[1]
                for p in range(n):
                    rows = pl.ds(c * h + p * pr, pr)
                    rows_sib = pl.ds((1 - c) * h + p * pr, pr)
                    s = len(sends)
                    sends.append(_remote(send, recv, s, i_refs[t].at[ci, rows], o_refs[t].at[ci, rows], sib))
                    arrivals.append(_remote(send, recv, s, i_refs[t].at[ci, rows_sib], o_refs[t].at[ci, rows_sib], sib))
        return sends, arrivals, []

    return _Phase(bufs, [SDS(b.shape, b.dtype) for b in bufs], n_sem, 0, copies, {t: t for t in range(len(bufs))})


def _pair_exchange(grads):
    plan = _piece_plan([g.shape[1:] for g in grads])
    n_sem = sum(n for _, n, _ in plan)

    def copies(g_refs, o_refs, send, recv, local):
        x, y, c = _me()
        sends = []
        for t, (h, n, pr) in enumerate(plan):
            for p in range(n):
                sends.append(_remote(send, recv, len(sends), g_refs[t].at[:, pl.ds((1 - c) * h + p * pr, pr)],
                                     o_refs[t].at[:, pl.ds(p * pr, pr)], (x, y, 1 - c)))
        return sends, sends, []

    return _Phase(grads, [SDS((N_CHIPS, g.shape[1] // 2, g.shape[2]), F32) for g in grads], n_sem, 0, copies)


def _pair_add(ck, g, theirs, name):
    _, r, c = g.shape
    (h, n, pr), = _piece_plan([(r, c)])

    def body(ck_ref, g_ref, t_ref, pbf_ref):
        pbf_ref[...] = (g_ref[...] + t_ref[...]).astype(BF)

    half = pl.BlockSpec((None, pr, c), lambda k, p, ck: (k, p, 0))
    spec = pltpu.PrefetchScalarGridSpec(
        num_scalar_prefetch=1, grid=(N_CHIPS, n),
        in_specs=[pl.BlockSpec((None, pr, c), lambda k, p, ck: (k, ck[0] * n + p, 0)), half], out_specs=half)
    return pl.pallas_call(body, grid_spec=spec, out_shape=SDS((N_CHIPS, h, c), BF),
                          compiler_params=_params(("arbitrary", "arbitrary")), name=name)(ck, g, theirs)


def _scatter_partials(pbfs):
    plan = [(h, _n_pieces(h), h // _n_pieces(h)) for h in [p.shape[1] for p in pbfs]]
    n_sem = 3 * sum(n for _, n, _ in plan)

    def copies(p_refs, o_refs, send, recv, local):
        x, y, c = _me()
        sends = []
        for t, (h, n, pr) in enumerate(plan):
            for j, chip in enumerate(_other_chips(x, y)):
                for p in range(n):
                    rows = pl.ds(p * pr, pr)
                    sends.append(_remote(send, recv, len(sends), p_refs[t].at[2 * chip[0] + chip[1], rows],
                                         o_refs[t].at[j, rows], (*chip, c)))
        return sends, sends, []

    return _Phase(pbfs, [SDS((3,) + p.shape[1:], BF) for p in pbfs], n_sem, 0, copies)


def _sum_chips(ck, pbf, slots, name):
    _, h, c = pbf.shape
    n = _n_pieces(h)
    pr = h // n

    def body(ck_ref, p_ref, s_ref, o_ref):
        o_ref[...] = (((p_ref[...].astype(F32) + s_ref[0].astype(F32)) + s_ref[1].astype(F32))
                      + s_ref[2].astype(F32))

    spec = pltpu.PrefetchScalarGridSpec(
        num_scalar_prefetch=1, grid=(n,),
        in_specs=[pl.BlockSpec((None, pr, c), lambda p, ck: (ck[1], p, 0)),
                  pl.BlockSpec((3, pr, c), lambda p, ck: (0, p, 0))],
        out_specs=pl.BlockSpec((pr, c), lambda p, ck: (ck[0] * n + p, 0)))
    return pl.pallas_call(body, grid_spec=spec, out_shape=SDS((2 * h, c), F32),
                          compiler_params=_params(("arbitrary",)), name=name)(ck, pbf, slots)


def _join_halves(sums):
    plan = _piece_plan([s.shape for s in sums])
    n_sem = sum(n for _, n, _ in plan)

    def copies(r_refs, o_refs, send, recv, local):
        x, y, c = _me()
        sends, arrivals = [], []
        for t, (h, n, pr) in enumerate(plan):
            for p in range(n):
                rows = pl.ds(c * h + p * pr, pr)
                rows_sib = pl.ds((1 - c) * h + p * pr, pr)
                s = len(sends)
                sends.append(_remote(send, recv, s, r_refs[t].at[rows], o_refs[t].at[rows], (x, y, 1 - c)))
                arrivals.append(_remote(send, recv, s, r_refs[t].at[rows_sib], o_refs[t].at[rows_sib], (x, y, 1 - c)))
        return sends, arrivals, []

    return _Phase(sums, [SDS(s.shape, F32) for s in sums], n_sem, 0, copies, {t: t for t in range(len(sums))})


def _gather_small(s, name):
    def body(s_ref, o_ref, send, recv, local):
        x, y, c = _me()
        me = 4 * x + 2 * y + c
        keep = pltpu.make_async_copy(s_ref, o_ref.at[me], local)
        keep.start()
        sends = []
        for r in range(1, 8):
            fx, fy, fc = (r >> 2) & 1, (r >> 1) & 1, r & 1
            to = (x ^ fx, y ^ fy, c ^ fc)
            sends.append(pltpu.make_async_remote_copy(
                src_ref=s_ref, dst_ref=o_ref.at[me], send_sem=send.at[r - 1], recv_sem=recv.at[r - 1],
                device_id=to, device_id_type=MESH))
        for cp in sends:
            cp.start()
        for r in range(1, 8):
            fx, fy, fc = (r >> 2) & 1, (r >> 1) & 1, r & 1
            src = 4 * (x ^ fx) + 2 * (y ^ fy) + (c ^ fc)
            pltpu.make_async_remote_copy(
                src_ref=s_ref, dst_ref=o_ref.at[src], send_sem=send.at[r - 1], recv_sem=recv.at[r - 1],
                device_id=(x ^ fx, y ^ fy, c ^ fc), device_id_type=MESH).wait_recv()
        for cp in sends:
            cp.wait_send()
        keep.wait()

    return pl.pallas_call(
        body, in_specs=[ANY], out_specs=ANY, out_shape=SDS((8, SMALL_ROWS, LANES), F32),
        scratch_shapes=[pltpu.SemaphoreType.DMA((7,)), pltpu.SemaphoreType.DMA((7,)), pltpu.SemaphoreType.DMA],
        name=name)(s)


def _adam_math(w, g, m, v):
    nm = ADAM_B1 * m + (1.0 - ADAM_B1) * g
    nv = ADAM_B2 * v + (1.0 - ADAM_B2) * (g * g)
    m_hat = nm / (1.0 - ADAM_B1 ** ADAM_STEP)
    v_hat = nv / (1.0 - ADAM_B2 ** ADAM_STEP)
    return -ADAM_LR * (m_hat / (jnp.sqrt(v_hat) + ADAM_EPS) + ADAM_WD * w), nm, nv


def _adamw(w, g, m, v, name):
    _, r, c = w.shape
    t = max(d for d in range(8, r + 1, 8) if r % d == 0 and 16 * d * c * 4 <= VMEM_LIMIT - (8 << 20))

    def body(w_ref, g_ref, m_ref, v_ref, go_ref, d_ref, nm_ref, nv_ref):
        g_ = g_ref[...]
        d, nm, nv = _adam_math(w_ref[...], g_, m_ref[...], v_ref[...])
        go_ref[...] = g_
        d_ref[...] = d
        nm_ref[...] = nm
        nv_ref[...] = nv

    lead = pl.BlockSpec((None, t, c), lambda i: (0, i, 0))
    return pl.pallas_call(body, grid=(r // t,), in_specs=[lead, _rows(t, c), lead, lead], out_specs=[lead] * 4,
                          out_shape=[SDS((1, r, c), F32)] * 4, compiler_params=_params(("parallel",)),
                          name=name)(w, g, m, v)


def _small_layout():
    out, r0 = {}, 0
    for n in SMALL:
        size = int(np.prod(SMALL_SHAPE[n]))
        nr = -(-size // LANES)
        out[n] = (r0, nr)
        r0 += nr
    assert r0 <= SMALL_ROWS
    return out, r0


def _pack_small(grads, loss_tile, name):
    layout, used = _small_layout()

    def body(*refs):
        o_ref = refs[-1]
        o_ref[used:used + 1, :] = refs[-2][0:1, :]
        for n, ref in zip(SMALL, refs[:-2]):
            r0, nr = layout[n]
            if n == "w_spatial":
                for g in range(GM_GROUPS):
                    o_ref[r0 + g * GM_CHUNK:r0 + (g + 1) * GM_CHUNK, :] = ref[g]
            elif n == "b_spatial":
                o_ref[r0:r0 + nr, :] = ref[...]
            else:
                for i in range(nr):
                    o_ref[r0 + i:r0 + i + 1, :] = ref[:, i * LANES:(i + 1) * LANES]
        if used + 1 < SMALL_ROWS:
            o_ref[used + 1:SMALL_ROWS, :] = jnp.zeros((SMALL_ROWS - used - 1, LANES), F32)

    return pl.pallas_call(body, out_shape=SDS((SMALL_ROWS, LANES), F32), name=name)(*grads, loss_tile)


def _adamw_small(gathered, ws, ms, vs, name):
    layout, used = _small_layout()
    n_t = len(SMALL)

    def body(*refs):
        g_ref = refs[0]
        w_refs, m_refs, v_refs = refs[1:1 + n_t], refs[1 + n_t:1 + 2 * n_t], refs[1 + 2 * n_t:1 + 3 * n_t]
        outs = refs[1 + 3 * n_t:1 + 7 * n_t]
        acc = refs[-1]
        total = g_ref[0]
        for j in range(1, 8):
            total = total + g_ref[j]
        acc[...] = total
        refs[1 + 7 * n_t][...] = acc[used:used + 1, :]
        for t, n in enumerate(SMALL):
            r0, nr = layout[n]
            o_refs = [outs[t], outs[n_t + t], outs[2 * n_t + t], outs[3 * n_t + t]]
            if n == "w_spatial":
                views = [((0, g), slice(r0 + g * GM_CHUNK, r0 + (g + 1) * GM_CHUNK), slice(None))
                         for g in range(GM_GROUPS)]
            elif n == "b_spatial":
                views = [((0,), slice(r0, r0 + nr), slice(None))]
            else:
                width = SMALL_SHAPE[n][1]
                views = [((slice(None), slice(i * LANES, min((i + 1) * LANES, width))), slice(r0 + i, r0 + i + 1),
                          slice(0, min(LANES, width - i * LANES))) for i in range(nr)]
            for idx, rows, lanes in views:
                g = acc[rows, lanes]
                d, nm, nv = _adam_math(w_refs[t][idx], g, m_refs[t][idx], v_refs[t][idx])
                for ref, val in zip(o_refs, (g, d, nm, nv)):
                    ref[idx] = val

    shapes = [SDS(SMALL_SHAPE[n], F32) for n in SMALL]
    return pl.pallas_call(body, out_shape=shapes * 4 + [SDS((1, LANES), F32)],
                          scratch_shapes=[pltpu.VMEM((SMALL_ROWS, LANES), F32)], name=name)(gathered, *ws, *ms, *vs)


def _win_layout(w_in):
    pad = jnp.zeros((w_in.shape[0], LANES - MLA_ROPE), w_in.dtype)
    u, v, cq = w_in[:, 0:512], w_in[:, 512:1024], w_in[:, 1024:1408]
    ckv, kpe, qm, zg = w_in[:, 1408:1664], w_in[:, 1664:1728], w_in[:, 1728:2240], w_in[:, 2240:5312]
    return jnp.concatenate([zg, u, v, qm, cq, kpe, pad, ckv], axis=1)


def _win_unlayout(g):
    zg, u, v, qm = g[:, ZG:ZG + 3072], g[:, ZU:ZU + 512], g[:, ZV:ZV + 512], g[:, QM:QM + 512]
    cq, kpe, ckv = g[:, CQ:CQ + 384], g[:, KPE:KPE + MLA_ROPE], g[:, CKV:CKV + 256]
    return jnp.concatenate([u, v, cq, ckv, kpe, qm, zg], axis=1)


def _wq_layout(w_uq):
    w = w_uq.reshape(Q_LORA, MLA_HEADS, MLA_NOPE + MLA_ROPE)
    nope = w[:, :, :MLA_NOPE].reshape(Q_LORA, MLA_HEADS * MLA_NOPE)
    pe = jnp.pad(w[:, :, MLA_NOPE:], ((0, 0), (0, 0), (0, LANES - MLA_ROPE))).reshape(Q_LORA, MLA_HEADS * LANES)
    return jnp.concatenate([nope, pe], axis=1)


def _wq_unlayout(g):
    nope = g[:, :1024].reshape(Q_LORA, MLA_HEADS, MLA_NOPE)
    pe = g[:, 1024:].reshape(Q_LORA, MLA_HEADS, LANES)[:, :, :MLA_ROPE]
    return jnp.concatenate([nope, pe], axis=2).reshape(Q_LORA, MLA_HEADS * (MLA_NOPE + MLA_ROPE))


def _wkv_layout(w_ukv):
    w = w_ukv.reshape(KV_LORA, MLA_HEADS, MLA_NOPE + MLA_V)
    return jnp.concatenate([w[:, :, :MLA_NOPE].reshape(KV_LORA, 1024), w[:, :, MLA_NOPE:].reshape(KV_LORA, 1024)],
                           axis=1)


def _wkv_unlayout(g):
    kn = g[:, :1024].reshape(KV_LORA, MLA_HEADS, MLA_NOPE)
    v = g[:, 1024:].reshape(KV_LORA, MLA_HEADS, MLA_V)
    return jnp.concatenate([kn, v], axis=2).reshape(KV_LORA, MLA_HEADS * (MLA_NOPE + MLA_V))


def _owner_major(g, name):
    r, c = _shard_shape(name)
    return g.reshape(r, N_CHIPS, c).transpose(1, 0, 2) if name in COL_SHARDED else g.reshape(N_CHIPS, r, c)


def _pad_lanes(g):
    return jnp.pad(g, ((0, 0), (0, LANES - g.shape[1])))


def kernel(x, mem, positions, g_mix, w_in, g_cq, w_uq, g_ckv, w_ukv, g_q_nope, g_q_pe, g_k_nope, g_k_pe, g_gm_ln, b_gm_ln, w_spatial, b_spatial, g_mem, w_mem_kv, g_mq, g_mk, w_o_gm, w_o_mla, w_o_mem, w_out, g_ffn, w_ff1, w_ff2, loss_target, m_g_mix, m_w_in, m_g_cq, m_w_uq, m_g_ckv, m_w_ukv, m_g_q_nope, m_g_q_pe, m_g_k_nope, m_g_k_pe, m_g_gm_ln, m_b_gm_ln, m_w_spatial, m_b_spatial, m_g_mem, m_w_mem_kv, m_g_mq, m_g_mk, m_w_o_gm, m_w_o_mla, m_w_o_mem, m_w_out, m_g_ffn, m_w_ff1, m_w_ff2, v_g_mix, v_w_in, v_g_cq, v_w_uq, v_g_ckv, v_w_ukv, v_g_q_nope, v_g_q_pe, v_g_k_nope, v_g_k_pe, v_g_gm_ln, v_b_gm_ln, v_w_spatial, v_b_spatial, v_g_mem, v_w_mem_kv, v_g_mq, v_g_mk, v_w_o_gm, v_w_o_mla, v_w_o_mem, v_w_out, v_g_ffn, v_w_ff1, v_w_ff2):
    given = dict(locals())
    wts = {n: given[n] for n in WEIGHTS}
    mom = {n: given["m_" + n] for n in WEIGHTS}
    var = {n: given["v_" + n] for n in WEIGHTS}
    batch, seq, _ = x.shape
    n_tok = batch * seq

    def natural(n, g):
        r, c = _shard_shape(n)
        return g.transpose(1, 0, 2).reshape(r, N_CHIPS * c) if n in COL_SHARDED else g.reshape(N_CHIPS * r, c)

    def far(names):
        return _gather_far([wts[n][0].astype(BF) for n in names])

    x2 = x.reshape(n_tok, D_MODEL)
    tgt2 = loss_target.reshape(n_tok, D_MODEL)
    mem2 = mem.reshape(batch * MEM_LEN, D_MODEL)
    pos_f = positions.reshape(n_tok, 1).astype(F32)

    inv = ROPE_BASE ** (-jnp.arange(0, MLA_ROPE, 2, dtype=F32) / MLA_ROPE)
    zeros64 = jnp.zeros((LANES - MLA_ROPE,), F32)
    inv_full = jnp.concatenate([inv, inv, zeros64]).reshape(1, LANES)
    half = MLA_ROPE // 2
    cmask = jnp.concatenate([jnp.ones((MLA_ROPE,), F32), zeros64]).reshape(1, LANES)
    smask = jnp.concatenate([-jnp.ones((half,), F32), jnp.ones((half,), F32), zeros64]).reshape(1, LANES)

    prep_gains = [g_cq, g_ckv, g_q_nope, _pad_lanes(g_q_pe), g_k_nope, _pad_lanes(g_k_pe)]
    ws = w_spatial[0]
    bcols = [b_spatial[0, g].reshape(GM_CHUNK, 1) for g in range(GM_GROUPS)]

    h1, in_far = _rms_fwd(x2, g_mix, "rms_mix", comm=far(EARLY[:1]))
    (cos_f, sin_s), early = _rope_tables(pos_f, inv_full, cmask, smask, "rope_tables",
                                         comm=_together(_gather_near(in_far), far(EARLY[1:])))
    memn, rest = _rms_fwd(mem2, g_mem, "rms_mem", comm=_gather_near(early[1:]))
    full = {n: natural(n, g) for n, g in zip(EARLY, list(early[:1]) + list(rest))}
    win = _win_layout(full["w_in"])
    wq = _wq_layout(full["w_uq"])
    wkv = _wkv_layout(full["w_ukv"])
    z, proj_far = _mm(h1, win, out_dtypes=(BF,), name="mm_in", comm=far(LATE_PROJ))
    gm = _gm_fwd(z, g_gm_ln, b_gm_ln, ws, bcols, "gm_fwd")
    qc, kc, vc = _prep_fwd(z, cos_f, sin_s, prep_gains, wq, wkv, "prep_fwd")
    (o_mla, lse), ff_far = _mla_fwd(qc, kc, vc, batch, seq, "mla_fwd", comm=far(LATE_FF))
    kvm, proj = _mm(memn, full["w_mem_kv"], name="mm_memkv", comm=_gather_near(proj_far))
    o_mem, ff = _mem_fwd(z, kvm, g_mq, g_mk, batch, seq, "mem_fwd", comm=_gather_near(ff_far))
    full.update({n: natural(n, g) for n, g in zip(LATE_PROJ + LATE_FF, list(proj) + list(ff))})
    y_gm = _mm(gm, full["w_o_gm"], out_dtypes=(BF,), name="mm_o_gm")
    y_mla = _mm(o_mla, full["w_o_mla"], out_dtypes=(BF,), name="mm_o_mla")
    y_mem = _mm(o_mem, full["w_o_mem"], out_dtypes=(BF,), name="mm_o_mem")
    merged = _merge_fwd(z, y_gm, y_mla, y_mem, "merge_fwd")
    x1 = _mm(merged, full["w_out"], ins=(x2,), epilogue=_add_to, name="mm_out")
    h2 = _rms_fwd(x1, g_ffn, "rms_ffn")
    a_ff, r_ff = _mm(h2, full["w_ff1"], epilogue=_relu2, out_dtypes=(BF, BF), name="mm_ff1")
    dy, dyb, loss_tile = _mm(r_ff, full["w_ff2"], ins=(x1, tgt2), epilogue=_loss_tail, out_dtypes=(F32, BF),
                             total=True, name="mm_ff2")

    gw = {}
    da = _mm(dyb, full["w_ff2"], tb=True, ins=(a_ff,), epilogue=_relu2_bwd, out_dtypes=(BF,), name="mm_d_a")
    gw["w_ff2"] = _owner_major(_mm(r_ff, dyb, ta=True, name="mm_dw_ff2"), "w_ff2")
    gw["w_ff1"] = _mm(h2, da, ta=True, owner_cols=D_FF // N_CHIPS, name="mm_dw_ff1")
    dh2 = _mm(da, full["w_ff1"], tb=True, name="mm_d_h2")
    dx1, dx1b, dg_ffn = _rms_bwd(x1, g_ffn, dh2, dy, "rms_ffn_bwd")
    dmerged = _mm(dx1b, full["w_out"], tb=True, name="mm_d_merged")
    gw["w_out"] = _owner_major(_mm(merged, dx1b, ta=True, name="mm_dw_out"), "w_out")
    dz, dy_gm, dy_mla, dy_mem = _merge_bwd(z, y_gm, y_mla, y_mem, dmerged, "merge_bwd")
    dgm = _mm(dy_gm, full["w_o_gm"], tb=True, name="mm_d_gm")
    gw["w_o_gm"] = _mm(gm, dy_gm, ta=True, owner_cols=D_MODEL // N_CHIPS, name="mm_dw_o_gm")
    do_mla = _mm(dy_mla, full["w_o_mla"], tb=True, name="mm_d_omla")
    gw["w_o_mla"] = _owner_major(_mm(o_mla, dy_mla, ta=True, name="mm_dw_o_mla"), "w_o_mla")
    do_mem = _mm(dy_mem, full["w_o_mem"], tb=True, name="mm_d_omem")
    gw["w_o_mem"] = _mm(o_mem, dy_mem, ta=True, owner_cols=D_MODEL // N_CHIPS, name="mm_dw_o_mem")
    ck = jnp.stack([lax.axis_index("c"), 2 * lax.axis_index("x") + lax.axis_index("y")]).astype(jnp.int32)

    def pair_sums(names, theirs):
        return [_pair_add(ck, gw[n], t, "pair_add_" + n) for n, t in zip(names, theirs)]

    def chip_sums(names, pairs, slots):
        return [_sum_chips(ck, p, s, "sum_chips_" + n) for n, p, s in zip(names, pairs, slots)]

    (dz, dg_ln, db_ln, dws, *dbcols), theirs = _gm_bwd(z, g_gm_ln, b_gm_ln, ws, bcols, dgm, dz, "gm_bwd",
                                                      comm=_pair_exchange([gw[n] for n in LATE]))
    pairs = pair_sums(LATE, theirs)
    (dq, dk, dv), slots = _mla_bwd(qc, kc, vc, o_mla, lse, do_mla, batch, seq, "mla_bwd",
                                   comm=_scatter_partials(pairs))
    sums = chip_sums(LATE, pairs, slots)
    (dz, dg_cq, dg_ckv, dg_qn, dg_qp, dg_kn, dg_kp, dwq, dwkv), reduced_late = _prep_bwd(
        z, cos_f, sin_s, prep_gains, wq, wkv, dq, dk, dv, dz, "prep_bwd", comm=_join_halves(sums))
    dz, dkvm, dg_mq, dg_mk = _mem_bwd(z, kvm, g_mq, g_mk, do_mem, dz, batch, seq, "mem_bwd")
    dmemn = _mm(dkvm, full["w_mem_kv"], tb=True, name="mm_d_memn")
    gw["w_mem_kv"] = _owner_major(_mm(memn, dkvm, ta=True, name="mm_dw_memkv"), "w_mem_kv")
    _, _, dg_mem = _rms_bwd(mem2, g_mem, dmemn, None, "rms_mem_bwd")
    gw["w_in"] = _owner_major(_win_unlayout(_mm(h1, dz, ta=True, name="mm_dw_in")), "w_in")
    gw["w_uq"] = _owner_major(_wq_unlayout(dwq), "w_uq")
    gw["w_ukv"] = _owner_major(_wkv_unlayout(dwkv), "w_ukv")
    dh1, theirs = _mm(dz, win, tb=True, name="mm_d_h1_top", rows=(0, 2), comm=_pair_exchange([gw[n] for n in EARLY]))
    pairs = pair_sums(EARLY, theirs)
    dh1, slots = _mm(dz, win, tb=True, name="mm_d_h1_bottom", rows=(1, 2), into=dh1,
                     comm=_scatter_partials(pairs))
    grad_x, _, dg_mix = _rms_bwd(x2, g_mix, dh1, dx1, "rms_mix_bwd")
    reduced_early = _run_phase(_join_halves(chip_sums(EARLY, pairs, slots)), "join_early")
    reduced = dict(zip(LATE + EARLY, list(reduced_late) + list(reduced_early)))

    def swapped(a):
        return jnp.swapaxes(a, -1, -2)

    results = {n: _adamw(wts[n], reduced[n], mom[n], var[n], "adamw_" + n) for n in BIG if n != "w_in"}
    results["w_in"] = [swapped(r) for r in _adamw(swapped(w_in), swapped(reduced["w_in"]), swapped(m_w_in),
                                                  swapped(v_w_in), "adamw_w_in")]

    small_g = {"g_mix": dg_mix, "g_cq": dg_cq, "g_ckv": dg_ckv, "g_q_nope": dg_qn, "g_q_pe": dg_qp,
               "g_k_nope": dg_kn, "g_k_pe": dg_kp, "g_gm_ln": dg_ln, "b_gm_ln": db_ln, "w_spatial": dws,
               "b_spatial": jnp.concatenate(dbcols, axis=1).T, "g_mem": dg_mem, "g_mq": dg_mq, "g_mk": dg_mk,
               "g_ffn": dg_ffn}
    packed = _pack_small([small_g[n] for n in SMALL], loss_tile, "pack_small")
    small_out = _adamw_small(_gather_small(packed, "gather_small"), [wts[n] for n in SMALL],
                             [mom[n] for n in SMALL], [var[n] for n in SMALL], "adamw_small")
    for t, n in enumerate(SMALL):
        results[n] = [small_out[j * len(SMALL) + t] for j in range(4)]

    loss = small_out[4 * len(SMALL)][0, 0]
    grad_x = grad_x.reshape(batch, seq, D_MODEL)
    return (loss, grad_x, *[results[n][0] for n in WEIGHTS], *[results[n][1] for n in WEIGHTS],
            *[results[n][2] for n in WEIGHTS], *[results[n][3] for n in WEIGHTS])
```

```python
import functools
import math

import numpy as np
import jax
import jax.numpy as jnp
from jax import lax
from jax.experimental import pallas as pl
from jax.experimental.pallas import tpu as pltpu

F32 = jnp.float32
BF = jnp.bfloat16
SDS = jax.ShapeDtypeStruct
MESH = pl.DeviceIdType.MESH

D_MODEL = 1024
MEM_LEN = 256
MEM_HEADS = 4
HEAD_DIM = 128
GM_WIDTH = 512
GM_CHUNK = 128
GM_GROUPS = 4
MLA_HEADS = 8
MLA_NOPE = 128
MLA_ROPE = 64
MLA_V = 128
Q_LORA = 384
KV_LORA = 256
ROPE_BASE = 10000.0
D_FF = 4096
EPS = 1e-6
W_IN_COLS = 5312
ADAM_LR, ADAM_B1, ADAM_B2, ADAM_EPS, ADAM_WD, ADAM_STEP = 0.001, 0.9, 0.999, 1e-08, 0.01, 10

ZG, ZU, ZV, QM, CQ, KPE, CKV = 0, 3072, 3584, 4096, 4608, 4992, 5120
Z_COLS = 5376
LANES = 128
ROW_TILE = 512
ATT_TILE = 1024
ATT_HEADS = 2
VMEM_LIMIT = 60 * 1024 * 1024

N_CHIPS = 4
PIECE_ROWS = 256
SMALL_ROWS = 560

BIG = ["w_in", "w_uq", "w_ukv", "w_mem_kv", "w_o_gm", "w_o_mla", "w_o_mem", "w_out", "w_ff1", "w_ff2"]
BIG_SHAPE = {"w_in": (1024, 5312), "w_uq": (384, 1536), "w_ukv": (256, 2048), "w_mem_kv": (1024, 1024),
             "w_o_gm": (512, 1024), "w_o_mla": (1024, 1024), "w_o_mem": (512, 1024), "w_out": (1024, 1024),
             "w_ff1": (1024, 4096), "w_ff2": (4096, 1024)}
COL_SHARDED = {"w_in", "w_uq", "w_ukv", "w_o_gm", "w_o_mem", "w_ff1"}
EARLY = ["w_in", "w_uq", "w_ukv", "w_mem_kv"]
LATE_PROJ = ["w_o_gm", "w_o_mla", "w_o_mem", "w_out"]
LATE_FF = ["w_ff1", "w_ff2"]
LATE = LATE_PROJ + LATE_FF
SMALL = ["w_spatial", "b_spatial", "g_mix", "g_cq", "g_ckv", "g_q_nope", "g_q_pe", "g_k_nope", "g_k_pe", "g_gm_ln",
         "b_gm_ln", "g_mem", "g_mq", "g_mk", "g_ffn"]
SMALL_SHAPE = {"g_mix": (1, 1024), "g_cq": (1, 384), "g_ckv": (1, 256), "g_q_nope": (1, 128), "g_q_pe": (1, 64),
               "g_k_nope": (1, 128), "g_k_pe": (1, 64), "g_gm_ln": (1, 512), "b_gm_ln": (1, 512),
               "w_spatial": (1, 4, 128, 128), "b_spatial": (1, 4, 128), "g_mem": (1, 1024), "g_mq": (1, 128),
               "g_mk": (1, 128), "g_ffn": (1, 1024)}
WEIGHTS = ['g_mix', 'w_in', 'g_cq', 'w_uq', 'g_ckv', 'w_ukv', 'g_q_nope', 'g_q_pe', 'g_k_nope', 'g_k_pe',
           'g_gm_ln', 'b_gm_ln', 'w_spatial', 'b_spatial', 'g_mem', 'w_mem_kv', 'g_mq', 'g_mk', 'w_o_gm',
           'w_o_mla', 'w_o_mem', 'w_out', 'g_ffn', 'w_ff1', 'w_ff2']


def _params(sem=None):
    return pltpu.CompilerParams(vmem_limit_bytes=VMEM_LIMIT, dimension_semantics=sem)


def _pick(n, prefs):
    for p in prefs:
        if n % p == 0:
            return p
    return n


def _full(shape):
    nd = len(shape)
    return pl.BlockSpec(shape, lambda *_: (0,) * nd)


def _rows(t, w, blk=0):
    return pl.BlockSpec((t, w), lambda i: (i, blk))


def _acc(ref, val, first):
    @pl.when(first)
    def _():
        ref[...] = val

    @pl.when(jnp.logical_not(first))
    def _():
        ref[...] += val


ANY = pl.BlockSpec(memory_space=pl.ANY)


class _Phase:
    def __init__(self, operands, out_shapes, n_sem, n_local, copies, aliases=None):
        self.operands, self.out_shapes, self.aliases = list(operands), list(out_shapes), dict(aliases or {})
        self.n_sem, self.n_local, self.copies = n_sem, max(n_local, 1), copies

    def sem_shapes(self):
        return [pltpu.SemaphoreType.DMA((self.n_sem,)), pltpu.SemaphoreType.DMA((self.n_sem,)),
                pltpu.SemaphoreType.DMA((self.n_local,))]

    def start(self, ins, outs, send, recv, local):
        sends, _, locals_ = self.copies(ins, outs, send, recv, local)
        for cp in locals_ + sends:
            cp.start()

    def finish(self, ins, outs, send, recv, local):
        sends, arrivals, locals_ = self.copies(ins, outs, send, recv, local)
        for cp in arrivals:
            cp.wait_recv()
        for cp in sends:
            cp.wait_send()
        for cp in locals_:
            cp.wait()


class _Shifted:
    def __init__(self, ref, base):
        self.ref, self.base = ref, base

    @property
    def at(self):
        return self

    def __getitem__(self, i):
        return self.ref.at[i + self.base]


def _together(first, second):
    n_in, n_out = len(first.operands), len(first.out_shapes)

    def copies(ins, outs, send, recv, local):
        a = first.copies(ins[:n_in], outs[:n_out], send, recv, local)
        b = second.copies(ins[n_in:], outs[n_out:], _Shifted(send, first.n_sem), _Shifted(recv, first.n_sem),
                          _Shifted(local, first.n_local))
        return a[0] + b[0], a[1] + b[1], a[2] + b[2]

    aliases = {**first.aliases, **{n_in + i: n_out + j for i, j in second.aliases.items()}}
    return _Phase(first.operands + second.operands, first.out_shapes + second.out_shapes, first.n_sem + second.n_sem,
                  first.n_local + second.n_local, copies, aliases)


def _run_phase(phase, name):
    n_in = len(phase.operands)

    def body(*refs):
        ins, outs, sems = refs[:n_in], refs[n_in:n_in + len(phase.out_shapes)], refs[n_in + len(phase.out_shapes):]
        phase.start(ins, outs, *sems)
        phase.finish(ins, outs, *sems)

    return pl.pallas_call(body, in_specs=[ANY] * n_in, out_specs=[ANY] * len(phase.out_shapes),
                          out_shape=phase.out_shapes, scratch_shapes=phase.sem_shapes(),
                          input_output_aliases=phase.aliases, name=name)(*phase.operands)


def _pcall(body, *, grid, in_specs, out_specs, out_shape, scratch_shapes=(), sem=None, name, comm=None, aliases=None):
    single = not isinstance(out_shape, (list, tuple))
    o_specs = [out_specs] if single else list(out_specs)
    o_shape = [out_shape] if single else list(out_shape)
    aliases = dict(aliases or {})
    if comm is None:
        call = pl.pallas_call(body, grid=grid, in_specs=list(in_specs), out_specs=o_specs, out_shape=o_shape,
                              scratch_shapes=list(scratch_shapes), input_output_aliases=aliases,
                              compiler_params=_params(sem), name=name)

        def run_plain(*args):
            res = call(*args)
            return res[0] if single else res

        return run_plain

    n_in, n_out, n_scr = len(in_specs), len(o_specs), len(scratch_shapes)
    nc_in, nc_out = len(comm.operands), len(comm.out_shapes)

    def wrapped(*refs):
        ins, cins = refs[:n_in], refs[n_in:n_in + nc_in]
        o0 = n_in + nc_in
        outs, couts = refs[o0:o0 + n_out], refs[o0 + n_out:o0 + n_out + nc_out]
        s0 = o0 + n_out + nc_out
        scr, csem = refs[s0:s0 + n_scr], refs[s0 + n_scr:]
        ids = [pl.program_id(d) for d in range(len(grid))]
        first = functools.reduce(jnp.logical_and, [i == 0 for i in ids])
        last = functools.reduce(jnp.logical_and, [i == g - 1 for i, g in zip(ids, grid)])

        @pl.when(first)
        def _():
            comm.start(cins, couts, *csem)

        body(*ins, *outs, *scr)

        @pl.when(last)
        def _():
            comm.finish(cins, couts, *csem)

    call = pl.pallas_call(
        wrapped, grid=grid, in_specs=list(in_specs) + [ANY] * nc_in, out_specs=o_specs + [ANY] * nc_out,
        out_shape=o_shape + comm.out_shapes, scratch_shapes=list(scratch_shapes) + comm.sem_shapes(),
        input_output_aliases={**aliases, **{n_in + i: n_out + j for i, j in comm.aliases.items()}},
        compiler_params=_params(("arbitrary",) * len(grid)), name=name)

    def run_carrying(*args):
        res = call(*args, *comm.operands)
        return (res[0] if single else res[:n_out]), res[n_out:]

    return run_carrying


def _dn(a, b, ca, cb):
    return lax.dot_general(a.astype(BF), b.astype(BF), (((ca,), (cb,)), ((), ())), preferred_element_type=F32)


@jax.custom_vjp
def _mm_nn(a, b):
    return _dn(a, b, 1, 0)


def _mm_nn_fwd(a, b):
    return _dn(a, b, 1, 0), (a.astype(BF), b.astype(BF))


def _mm_nn_bwd(res, ct):
    a, b = res
    return _dn(ct, b, 1, 1), _dn(a, ct, 0, 0)


_mm_nn.defvjp(_mm_nn_fwd, _mm_nn_bwd)


@jax.custom_vjp
def _mm_nt(a, b):
    return _dn(a, b, 1, 1)


def _mm_nt_fwd(a, b):
    return _dn(a, b, 1, 1), (a.astype(BF), b.astype(BF))


def _mm_nt_bwd(res, ct):
    a, b = res
    return _dn(ct, b, 1, 0), _dn(ct, a, 0, 0)


_mm_nt.defvjp(_mm_nt_fwd, _mm_nt_bwd)


def _rmsn(x, g, n):
    ms = jnp.sum(x * x, axis=-1, keepdims=True) * (1.0 / n)
    return x * lax.rsqrt(ms + EPS) * g


def _layernorm(x, g, b):
    mu = jnp.mean(x, axis=-1, keepdims=True)
    xc = x - mu
    y = xc * lax.rsqrt(jnp.mean(xc * xc, axis=-1, keepdims=True) + EPS)
    return y * g + b


def _swap_lanes(x):
    half = MLA_ROPE // 2
    lane = lax.broadcasted_iota(jnp.int32, x.shape, 1)
    return jnp.where(lane < half, pltpu.roll(x, LANES - half, axis=1),
                     jnp.where(lane < MLA_ROPE, pltpu.roll(x, half, axis=1), 0.0))


@jax.custom_vjp
def _swap_halves(x):
    return _swap_lanes(x)


_swap_halves.defvjp(lambda x: (_swap_lanes(x), None), lambda _, ct: (_swap_lanes(ct),))


def _rope(x, cos_f, sin_s):
    return x * cos_f + _swap_halves(x) * sin_s


def _lane_blocks(x):
    return tuple(x[:, i * LANES:(i + 1) * LANES] for i in range(x.shape[1] // LANES))


@jax.custom_vjp
def _split_lanes(x):
    return _lane_blocks(x)


_split_lanes.defvjp(lambda x: (_lane_blocks(x), None), lambda _, cts: (jnp.concatenate(cts, axis=1),))


def _softmax(s):
    m = lax.stop_gradient(jnp.max(s, axis=-1, keepdims=True))
    p = jnp.exp(s - m)
    return p / jnp.sum(p, axis=-1, keepdims=True)


def _mm(a, b, *, ta=False, tb=False, ins=(), row_ins=(), epilogue=None, out_dtypes=(F32,), owner_cols=None,
        total=False, name, comm=None, rows=None, into=None):
    if ta:
        k_dim, m = a.shape
    else:
        m, k_dim = a.shape
    if tb:
        n, kb = b.shape
    else:
        kb, n = b.shape
    assert k_dim == kb, (a.shape, b.shape, ta, tb)
    part, n_parts = rows if rows is not None else (0, 1)
    tm = _pick(m // n_parts, (1024, 512, 256, 128))
    tn = _pick(n if owner_cols is None else owner_cols, (1024, 768, 512, 384, 256, 128))
    tk = _pick(k_dim, (2048, 1024, 768, 512, 256, 128))
    nk = k_dim // tk
    m_steps = m // tm // n_parts
    off = part * m_steps
    ca = 0 if ta else 1
    cb = 1 if tb else 0
    n_in = len(ins) + len(row_ins)
    n_out = len(out_dtypes)
    n_pass = 0 if into is None else 1
    total_shape = total if isinstance(total, tuple) else (8, LANES)
    assert not (isinstance(total, tuple) and n != tn), "a per-column total needs the whole width in one tile"

    def finish(r, in_refs, out_refs, first_tile):
        vals = epilogue(r, *[ref[...].astype(F32) for ref in in_refs]) if epilogue is not None else (r,)
        for ref, val, dt in zip(out_refs, vals, out_dtypes):
            ref[...] = val.astype(dt)
        if total:
            _acc(out_refs[n_out], vals[n_out], first_tile)

    def body(*refs):
        a_ref, b_ref = refs[:2]
        in_refs = refs[2:2 + n_in]
        o0 = 2 + n_in + n_pass
        out_refs = refs[o0:o0 + n_out + int(bool(total))]
        first_tile = jnp.logical_and(pl.program_id(0) == 0, pl.program_id(1) == 0)
        part = _dn(a_ref[...], b_ref[...], ca, cb)
        if nk == 1:
            finish(part, in_refs, out_refs, first_tile)
            return
        acc = refs[-1]
        k = pl.program_id(2)
        _acc(acc, part, k == 0)

        @pl.when(k == nk - 1)
        def _():
            finish(acc[...], in_refs, out_refs, first_tile)

    a_spec = (pl.BlockSpec((tk, tm), lambda i, j, k: (k, i + off)) if ta
              else pl.BlockSpec((tm, tk), lambda i, j, k: (i + off, k)))
    b_spec = pl.BlockSpec((tn, tk), lambda i, j, k: (j, k)) if tb else pl.BlockSpec((tk, tn), lambda i, j, k: (k, j))
    t_spec = pl.BlockSpec((tm, tn), lambda i, j, k: (i + off, j))
    if owner_cols is None:
        o_spec, o_shape = t_spec, (m, n)
    else:
        per = owner_cols // tn
        o_spec = pl.BlockSpec((None, tm, tn), lambda i, j, k: (j // per, i + off, j % per))
        o_shape = (n // owner_cols, m, owner_cols)
    o_specs = [o_spec] * n_out + ([pl.BlockSpec(total_shape, lambda i, j, k: (0, 0))] if total else [])
    o_shapes = [SDS(o_shape, dt) for dt in out_dtypes] + ([SDS(total_shape, F32)] if total else [])
    row_spec = pl.BlockSpec((1, tn), lambda i, j, k: (0, j))
    in_specs = [a_spec, b_spec] + [t_spec] * len(ins) + [row_spec] * len(row_ins) + [ANY] * n_pass
    args = [a, b, *ins, *row_ins] + ([into] if n_pass else [])
    run = _pcall(body, grid=(m_steps, n // tn, nk), in_specs=in_specs, out_specs=o_specs, out_shape=o_shapes,
                 scratch_shapes=[pltpu.VMEM((tm, tn), F32)] if nk > 1 else [],
                 sem=("arbitrary",) * 3 if total else ("parallel", "parallel", "arbitrary"), name=name, comm=comm,
                 aliases={len(in_specs) - 1: 0} if n_pass else None)
    if comm is None:
        outs = run(*args)
        return outs[0] if len(outs) == 1 else outs
    outs, exchanged = run(*args)
    return (outs[0] if len(outs) == 1 else outs), exchanged


def _add_to(r, x):
    return (r + x,)


def _residual_rms(r, x, g):
    x1 = r + x
    return x1, _rmsn(x1, g, D_MODEL)


def _rms_bwd_tail(dh, x, res, g):
    _, vjp = jax.vjp(lambda xx, gg: _rmsn(xx, gg, D_MODEL), x, g)
    dx, dg = vjp(dh)
    dx = dx + res
    return dx, dx, dg


def _relu2(r):
    p = jnp.maximum(r, 0.0)
    return r, p * p


def _relu2_bwd(dr, a):
    return (dr * (2.0 * jnp.maximum(a, 0.0)),)


def _loss_tail(r, x1, tgt):
    e = (r + x1) - tgt
    dy = e * (1.0 / D_MODEL)
    part = jnp.sum(jnp.sum(e * e, axis=-1, keepdims=True), axis=0, keepdims=True) * (0.5 / D_MODEL)
    return dy, dy, jnp.broadcast_to(part, (8, LANES))


def _rms_fwd(x, g, name, comm=None):
    n, w = x.shape
    t = min(ROW_TILE, n)

    def body(x_ref, g_ref, o_ref):
        o_ref[...] = _rmsn(x_ref[...], g_ref[...], w).astype(BF)

    return _pcall(body, grid=(n // t,), in_specs=[_rows(t, w), _full((1, w))], out_specs=_rows(t, w),
                  out_shape=SDS((n, w), BF), sem=("arbitrary",), name=name, comm=comm)(x, g)


def _rms_bwd(x, g, dh, res, name, comm=None):
    n, w = x.shape
    t = min(ROW_TILE, n)
    has_res = res is not None

    def body(*refs):
        if has_res:
            x_ref, g_ref, dh_ref, res_ref, dx_ref, dxb_ref, dg_ref = refs
        else:
            x_ref, g_ref, dh_ref, dx_ref, dxb_ref, dg_ref = refs
        _, vjp = jax.vjp(lambda xx, gg: _rmsn(xx, gg, w), x_ref[...], g_ref[...])
        dx, dg = vjp(dh_ref[...])
        if has_res:
            dx = dx + res_ref[...]
        dx_ref[...] = dx
        dxb_ref[...] = dx.astype(BF)
        _acc(dg_ref, dg, pl.program_id(0) == 0)

    in_specs = [_rows(t, w), _full((1, w)), _rows(t, w)] + ([_rows(t, w)] if has_res else [])
    args = [x, g, dh] + ([res] if has_res else [])
    return _pcall(body, grid=(n // t,), in_specs=in_specs, out_specs=[_rows(t, w), _rows(t, w), _full((1, w))],
                  out_shape=[SDS((n, w), F32), SDS((n, w), BF), SDS((1, w), F32)], sem=("arbitrary",), name=name,
                  comm=comm)(*args)


def _merge_core(zg0, zg1, zg2, y0, y1, y2):
    return jax.nn.sigmoid(zg0) * y0 + jax.nn.sigmoid(zg1) * y1 + jax.nn.sigmoid(zg2) * y2


def _merge_fwd(z, y_gm, y_mla, y_mem, name):
    n = z.shape[0]
    t = min(ROW_TILE, n)
    w = D_MODEL

    def body(g0, g1, g2, y0, y1, y2, o_ref):
        o_ref[...] = _merge_core(g0[...].astype(F32), g1[...].astype(F32), g2[...].astype(F32), y0[...].astype(F32), y1[...].astype(F32),
                                 y2[...].astype(F32)).astype(BF)

    return pl.pallas_call(body, grid=(n // t,),
                          in_specs=[_rows(t, w, 0), _rows(t, w, 1), _rows(t, w, 2)] + [_rows(t, w)] * 3,
                          out_specs=_rows(t, w), out_shape=SDS((n, w), BF),
                          compiler_params=_params(("parallel",)), name=name)(z, z, z, y_gm, y_mla, y_mem)


def _merge_bwd(z, y_gm, y_mla, y_mem, dmerged, name):
    n = z.shape[0]
    t = min(ROW_TILE, n)
    w = D_MODEL

    def body(g0, g1, g2, y0, y1, y2, dm, dzg_ref, d0_ref, d1_ref, d2_ref):
        _, vjp = jax.vjp(_merge_core, g0[...].astype(F32), g1[...].astype(F32), g2[...].astype(F32), y0[...].astype(F32), y1[...].astype(F32),
                         y2[...].astype(F32))
        dg0, dg1, dg2, dy0, dy1, dy2 = vjp(dm[...])
        dzg_ref[:, 0:w] = dg0.astype(BF)
        dzg_ref[:, w:2 * w] = dg1.astype(BF)
        dzg_ref[:, 2 * w:3 * w] = dg2.astype(BF)
        d0_ref[...] = dy0.astype(BF)
        d1_ref[...] = dy1.astype(BF)
        d2_ref[...] = dy2.astype(BF)

    return pl.pallas_call(body, grid=(n // t,),
                          in_specs=[_rows(t, w, 0), _rows(t, w, 1), _rows(t, w, 2)] + [_rows(t, w)] * 4,
                          out_specs=[_rows(t, 3 * w, ZG // (3 * w))] + [_rows(t, w)] * 3,
                          out_shape=[SDS((n, Z_COLS), BF)] + [SDS((n, w), BF)] * 3,
                          compiler_params=_params(("parallel",)), name=name)(z, z, z, y_gm, y_mla, y_mem, dmerged)


def _gm_core(zu, zv, g_ln, b_ln, ws, bcols):
    t = zu.shape[0]
    u = jax.nn.gelu(zu)
    v = _layernorm(jax.nn.gelu(zv), g_ln, b_ln)
    row = lax.broadcasted_iota(jnp.int32, (GM_CHUNK, GM_CHUNK), 0)
    col = lax.broadcasted_iota(jnp.int32, (GM_CHUNK, GM_CHUNK), 1)
    wc = [jnp.where(row >= col, ws[g], 0.0) for g in range(GM_GROUPS)]
    chunks = []
    for c in range(t // GM_CHUNK):
        cols = []
        for g in range(GM_GROUPS):
            vc = v[c * GM_CHUNK:(c + 1) * GM_CHUNK, g * LANES:(g + 1) * LANES]
            cols.append(_mm_nn(wc[g], vc) + bcols[g])
        chunks.append(jnp.concatenate(cols, axis=1))
    mixed = chunks[0] if len(chunks) == 1 else jnp.concatenate(chunks, axis=0)
    return u * mixed


def _gm_specs(t):
    return [_rows(t, GM_WIDTH, ZU // GM_WIDTH), _rows(t, GM_WIDTH, ZV // GM_WIDTH), _full((1, GM_WIDTH)),
            _full((1, GM_WIDTH)), _full((GM_GROUPS, GM_CHUNK, GM_CHUNK))] + [_full((GM_CHUNK, 1))] * GM_GROUPS


def _gm_fwd(z, g_ln, b_ln, ws, bcols, name):
    n = z.shape[0]
    t = min(ROW_TILE, n)

    def body(zu, zv, g_ref, b_ref, ws_ref, c0, c1, c2, c3, o_ref):
        out = _gm_core(zu[...].astype(F32), zv[...].astype(F32), g_ref[...], b_ref[...], [ws_ref[g] for g in range(GM_GROUPS)],
                       [c0[...], c1[...], c2[...], c3[...]])
        o_ref[...] = out.astype(BF)

    return pl.pallas_call(body, grid=(n // t,), in_specs=_gm_specs(t), out_specs=_rows(t, GM_WIDTH),
                          out_shape=SDS((n, GM_WIDTH), BF), compiler_params=_params(("parallel",)),
                          name=name)(z, z, g_ln, b_ln, ws, *bcols)


def _gm_bwd(z, g_ln, b_ln, ws, bcols, dgm, dz, name, comm=None):
    n = z.shape[0]
    t = min(ROW_TILE, n)

    def body(zu, zv, g_ref, b_ref, ws_ref, c0, c1, c2, c3, dgm_ref, _, dz_ref, dg_ref, db_ref, dws_ref, e0, e1, e2,
             e3):
        first = pl.program_id(0) == 0
        _, vjp = jax.vjp(_gm_core, zu[...].astype(F32), zv[...].astype(F32), g_ref[...], b_ref[...],
                         [ws_ref[g] for g in range(GM_GROUPS)], [c0[...], c1[...], c2[...], c3[...]])
        dzu, dzv, dg, db, dws, dcols = vjp(dgm_ref[...])
        dz_ref[:, 0:GM_WIDTH] = dzu.astype(BF)
        dz_ref[:, GM_WIDTH:2 * GM_WIDTH] = dzv.astype(BF)
        _acc(dg_ref, dg, first)
        _acc(db_ref, db, first)
        _acc(dws_ref, jnp.stack(dws, axis=0), first)
        for ref, val in zip((e0, e1, e2, e3), dcols):
            _acc(ref, val, first)

    in_specs = _gm_specs(t) + [_rows(t, GM_WIDTH), ANY]
    return _pcall(
        body, grid=(n // t,), in_specs=in_specs,
        out_specs=[_rows(t, 2 * GM_WIDTH, ZU // (2 * GM_WIDTH)), _full((1, GM_WIDTH)), _full((1, GM_WIDTH)),
                   _full((GM_GROUPS, GM_CHUNK, GM_CHUNK))] + [_full((GM_CHUNK, 1))] * GM_GROUPS,
        out_shape=[SDS((n, Z_COLS), BF), SDS((1, GM_WIDTH), F32), SDS((1, GM_WIDTH), F32),
                   SDS((GM_GROUPS, GM_CHUNK, GM_CHUNK), F32)] + [SDS((GM_CHUNK, 1), F32)] * GM_GROUPS,
        sem=("arbitrary",), name=name, comm=comm, aliases={len(in_specs) - 1: 0})(z, z, g_ln, b_ln, ws, *bcols, dgm, dz)


def _rope_tables(pos_f, inv_full, cmask, smask, name, comm=None):
    n = pos_f.shape[0]
    t = min(ROW_TILE, n)

    def body(p_ref, inv_ref, cm_ref, sm_ref, cos_ref, sin_ref):
        ang = p_ref[...] * inv_ref[...]
        cos_ref[...] = jnp.cos(ang) * cm_ref[...]
        sin_ref[...] = jnp.sin(ang) * sm_ref[...]

    return _pcall(body, grid=(n // t,), in_specs=[_rows(t, 1)] + [_full((1, LANES))] * 3,
                  out_specs=[_rows(t, LANES)] * 2, out_shape=[SDS((n, LANES), F32)] * 2, sem=("parallel",),
                  name=name, comm=comm)(pos_f, inv_full, cmask, smask)


def _prep_norms(cq, ckv, g_cq, g_ckv):
    return _rmsn(cq, g_cq, Q_LORA), _rmsn(ckv, g_ckv, KV_LORA)


def _prep_heads(qa, kva, kpe, head_gains, cos_f, sin_s):
    g_qn, g_qp, g_kn, g_kp = head_gains
    qs = _split_lanes(qa)
    kvs = _split_lanes(kva)
    kp = _rope(_rmsn(kpe, g_kp, MLA_ROPE), cos_f, sin_s)
    q_out, k_out = [], []
    for h in range(MLA_HEADS):
        q_out.append(_rmsn(qs[h], g_qn, MLA_NOPE))
        q_out.append(_rope(_rmsn(qs[MLA_HEADS + h], g_qp, MLA_ROPE), cos_f, sin_s))
        k_out.append(_rmsn(kvs[h], g_kn, MLA_NOPE))
        k_out.append(kp)
    return (jnp.concatenate(q_out, axis=1), jnp.concatenate(k_out, axis=1),
            jnp.concatenate(kvs[MLA_HEADS:], axis=1))


def _prep_in_specs(t):
    return ([_rows(t, Q_LORA, CQ // Q_LORA), _rows(t, LANES, KPE // LANES), _rows(t, KV_LORA, CKV // KV_LORA),
             _rows(t, LANES), _rows(t, LANES), _full((1, Q_LORA)), _full((1, KV_LORA))] + [_full((1, LANES))] * 4
            + [_full((Q_LORA, 2048)), _full((KV_LORA, 2048))])


def _prep_fwd(z, cos_f, sin_s, gains, wq, wkv, name):
    n = z.shape[0]
    t = min(ROW_TILE, n)

    def body(cq, kpe, ckv, cos_ref, sin_ref, g_cq, g_ckv, g_qn, g_qp, g_kn, g_kp, wq_ref, wkv_ref, q_ref, k_ref, v_ref):
        cqn, ckvn = _prep_norms(cq[...].astype(F32), ckv[...].astype(F32), g_cq[...], g_ckv[...])
        qa = _dn(cqn, wq_ref[...], 1, 0)
        kva = _dn(ckvn, wkv_ref[...], 1, 0)
        q, k, v = _prep_heads(qa, kva, kpe[...].astype(F32), (g_qn[...], g_qp[...], g_kn[...], g_kp[...]), cos_ref[...],
                              sin_ref[...])
        q_ref[...] = q.astype(BF)
        k_ref[...] = k.astype(BF)
        v_ref[...] = v.astype(BF)

    return pl.pallas_call(body, grid=(n // t,), in_specs=_prep_in_specs(t),
                          out_specs=[_rows(t, 2048), _rows(t, 2048), _rows(t, 1024)],
                          out_shape=[SDS((n, 2048), BF), SDS((n, 2048), BF), SDS((n, 1024), BF)],
                          compiler_params=_params(("parallel",)),
                          name=name)(z, z, z, cos_f, sin_s, *gains, wq, wkv)


def _prep_bwd(z, cos_f, sin_s, gains, wq, wkv, dq, dk, dv, dz, name, comm=None):
    n = z.shape[0]
    t = min(ROW_TILE, n)
    wz = Q_LORA + LANES + KV_LORA

    def body(cq, kpe, ckv, cos_ref, sin_ref, g_cq, g_ckv, g_qn, g_qp, g_kn, g_kp, wq_ref, wkv_ref, dq_ref, dk_ref,
             dv_ref, _, dz_ref, o_cq, o_ckv, o_qn, o_qp, o_kn, o_kp, dwq_ref, dwkv_ref):
        first = pl.program_id(0) == 0
        cos_t, sin_t = cos_ref[...], sin_ref[...]
        (cqn, ckvn), vjp_norms = jax.vjp(_prep_norms, cq[...].astype(F32), ckv[...].astype(F32), g_cq[...], g_ckv[...])
        wq_t, wkv_t = wq_ref[...], wkv_ref[...]
        qa = _dn(cqn, wq_t, 1, 0)
        kva = _dn(ckvn, wkv_t, 1, 0)
        _, vjp_heads = jax.vjp(lambda a, b, c, g: _prep_heads(a, b, c, g, cos_t, sin_t), qa, kva, kpe[...].astype(F32),
                               (g_qn[...], g_qp[...], g_kn[...], g_kp[...]))
        dqa, dkva, dkpe, dhead = vjp_heads((dq_ref[...], dk_ref[...], dv_ref[...]))
        _acc(dwq_ref, _dn(cqn, dqa, 0, 0), first)
        _acc(dwkv_ref, _dn(ckvn, dkva, 0, 0), first)
        dcq, dckv, dg_cq, dg_ckv = vjp_norms((_dn(dqa, wq_t, 1, 1), _dn(dkva, wkv_t, 1, 1)))
        dz_ref[:, 0:Q_LORA] = dcq.astype(BF)
        dz_ref[:, Q_LORA:Q_LORA + LANES] = dkpe.astype(BF)
        dz_ref[:, Q_LORA + LANES:wz] = dckv.astype(BF)
        for ref, val in zip((o_cq, o_ckv, o_qn, o_qp, o_kn, o_kp), (dg_cq, dg_ckv) + tuple(dhead)):
            _acc(ref, val, first)

    gain_specs = [_full((1, Q_LORA)), _full((1, KV_LORA))] + [_full((1, LANES))] * 4
    gain_shapes = [SDS((1, Q_LORA), F32), SDS((1, KV_LORA), F32)] + [SDS((1, LANES), F32)] * 4
    in_specs = _prep_in_specs(t) + [_rows(t, 2048), _rows(t, 2048), _rows(t, 1024), ANY]
    return _pcall(
        body, grid=(n // t,), in_specs=in_specs,
        out_specs=[_rows(t, wz, CQ // wz)] + gain_specs + [_full((Q_LORA, 2048)), _full((KV_LORA, 2048))],
        out_shape=[SDS((n, Z_COLS), BF)] + gain_shapes + [SDS((Q_LORA, 2048), F32), SDS((KV_LORA, 2048), F32)],
        sem=("arbitrary",), name=name, comm=comm,
        aliases={len(in_specs) - 1: 0})(z, z, z, cos_f, sin_s, *gains, wq, wkv, dq, dk, dv, dz)


MLA_QK = 256
MLA_SCALE = 1.0 / math.sqrt(MLA_NOPE + MLA_ROPE)
LOG2E = 1.0 / math.log(2.0)
MLA_SCALE_LOG2E = MLA_SCALE * LOG2E


def _causal_mask(s, q0, k0):
    tq, tk = s.shape
    row = q0 + lax.broadcasted_iota(jnp.int32, (tq, tk), 0)
    col = k0 + lax.broadcasted_iota(jnp.int32, (tq, tk), 1)
    return jnp.where(row >= col, s, -jnp.inf)


def _mla_fwd(q, k, v, batch, seq, name, comm=None):
    n = q.shape[0]
    tq = min(ATT_TILE, seq)
    nq = seq // tq

    nh = ATT_HEADS

    def body(q_ref, k_ref, v_ref, o_ref, lse_ref):
        i = pl.program_id(2)

        def step(j, carry, diagonal=False):
            k0 = pl.multiple_of(j * tq, tq)
            out = []
            ones = jnp.ones((tq, LANES), BF)
            for hh in range(nh):
                m, acc = carry[hh]
                qb = q_ref[:, hh * MLA_QK:(hh + 1) * MLA_QK]
                kb = k_ref[pl.ds(k0, tq), hh * MLA_QK:(hh + 1) * MLA_QK]
                vb = v_ref[pl.ds(k0, tq), hh * MLA_V:(hh + 1) * MLA_V]
                s = _dn(qb, kb, 1, 1)
                if diagonal:
                    s = _causal_mask(s, i * tq, k0)
                m_new = jnp.maximum(m, jnp.max(s, axis=-1, keepdims=True))
                p = jnp.exp2((s - m_new) * MLA_SCALE_LOG2E)
                alpha = jnp.exp2((m - m_new) * MLA_SCALE_LOG2E)
                acc = alpha * acc + _dn(p, jnp.concatenate([vb, ones], axis=1), 1, 0)
                out.append((m_new, acc))
            return tuple(out)

        init = tuple((jnp.full((tq, 1), -jnp.inf, F32), jnp.zeros((tq, MLA_V + LANES), F32)) for _ in range(nh))
        final = step(i, lax.fori_loop(0, i, step, init), diagonal=True)
        for hh, (m, acc) in enumerate(final):
            l = acc[:, MLA_V:MLA_V + 1]
            o_ref[:, hh * MLA_V:(hh + 1) * MLA_V] = acc[:, :MLA_V] / l
            lse_ref[:, hh * LANES:(hh + 1) * LANES] = jnp.broadcast_to(m * MLA_SCALE + jnp.log(l), (tq, LANES))

    return _pcall(
        body, grid=(batch, MLA_HEADS // nh, nq),
        in_specs=[pl.BlockSpec((tq, nh * MLA_QK), lambda b, h, i: (b * nq + i, h)),
                  pl.BlockSpec((seq, nh * MLA_QK), lambda b, h, i: (b, h)),
                  pl.BlockSpec((seq, nh * MLA_V), lambda b, h, i: (b, h))],
        out_specs=[pl.BlockSpec((tq, nh * MLA_V), lambda b, h, i: (b * nq + i, h)),
                   pl.BlockSpec((tq, nh * LANES), lambda b, h, i: (b * nq + i, h))],
        out_shape=[SDS((n, MLA_HEADS * MLA_V), F32), SDS((n, MLA_HEADS * LANES), F32)],
        sem=("parallel", "parallel", "arbitrary"), name=name, comm=comm)(q, k, v)


def _mla_bwd(q, k, v, o, lse, do, batch, seq, name, comm=None):
    n = q.shape[0]
    tk = min(ATT_TILE, seq)
    nk = seq // tk

    nh = ATT_HEADS

    def body(q_ref, k_ref, v_ref, o_ref, lse_ref, do_ref, dq_ref, dk_ref, dv_ref):
        jk = pl.program_id(2)

        @pl.when(jk == 0)
        def _():
            dq_ref[...] = jnp.zeros_like(dq_ref)

        def step(i, carry, diagonal=False):
            q0 = pl.multiple_of(i * tk, tk)
            rows = pl.ds(q0, tk)
            out = []
            for hh in range(nh):
                dk_acc, dv_acc = carry[hh]
                qk_cols = slice(hh * MLA_QK, (hh + 1) * MLA_QK)
                v_cols = slice(hh * MLA_V, (hh + 1) * MLA_V)
                kb = k_ref[:, qk_cols]
                vb = v_ref[:, v_cols]
                qb = q_ref[rows, qk_cols]
                dob = do_ref[rows, v_cols]
                delta = jnp.sum(dob * o_ref[rows, v_cols], axis=-1, keepdims=True)
                s = _dn(qb, kb, 1, 1)
                if diagonal:
                    s = _causal_mask(s, q0, jk * tk)
                p = jnp.exp2(s * MLA_SCALE_LOG2E - lse_ref[rows, hh * LANES:hh * LANES + 1] * LOG2E)
                dv_acc = dv_acc + _dn(p, dob, 0, 0)
                dp = _dn(dob, vb, 1, 1)
                ds = p * (dp - delta) * MLA_SCALE
                dk_acc = dk_acc + _dn(ds, qb, 0, 0)
                dq_ref[rows, qk_cols] += _dn(ds, kb, 1, 0)
                out.append((dk_acc, dv_acc))
            return tuple(out)

        init = tuple((jnp.zeros((tk, MLA_QK), F32), jnp.zeros((tk, MLA_V), F32)) for _ in range(nh))
        final = lax.fori_loop(jk + 1, nk, step, step(jk, init, diagonal=True))
        for hh, (dk_acc, dv_acc) in enumerate(final):
            dk_ref[:, hh * MLA_QK:(hh + 1) * MLA_QK] = dk_acc
            dv_ref[:, hh * MLA_V:(hh + 1) * MLA_V] = dv_acc

    full_qk = pl.BlockSpec((seq, nh * MLA_QK), lambda b, h, j: (b, h))
    full_v = pl.BlockSpec((seq, nh * MLA_V), lambda b, h, j: (b, h))
    blk_qk = pl.BlockSpec((tk, nh * MLA_QK), lambda b, h, j: (b * nk + j, h))
    blk_v = pl.BlockSpec((tk, nh * MLA_V), lambda b, h, j: (b * nk + j, h))
    return _pcall(
        body, grid=(batch, MLA_HEADS // nh, nk),
        in_specs=[full_qk, blk_qk, blk_v, full_v, full_v, full_v],
        out_specs=[full_qk, blk_qk, blk_v],
        out_shape=[SDS((n, MLA_HEADS * MLA_QK), F32), SDS((n, MLA_HEADS * MLA_QK), F32),
                   SDS((n, MLA_HEADS * MLA_V), F32)],
        sem=("parallel", "parallel", "arbitrary"), name=name, comm=comm)(q, k, v, o, lse, do)


MEM_SCALE = 1.0 / math.sqrt(HEAD_DIM)
MEM_W = MEM_HEADS * HEAD_DIM


def _mem_core(qs, ks, vs, g_mq, g_mk):
    outs = []
    for h in range(MEM_HEADS):
        qh = _rmsn(qs[h], g_mq, HEAD_DIM)
        kh = _rmsn(ks[h], g_mk, HEAD_DIM)
        p = _softmax(_mm_nt(qh, kh) * MEM_SCALE)
        outs.append(_mm_nn(p, vs[h]))
    return jnp.concatenate(outs, axis=1)


def _mem_load(qm, kvm, g_mq, g_mk):
    hs = range(MEM_HEADS)
    qs = [qm[:, h * LANES:(h + 1) * LANES].astype(F32) for h in hs]
    ks = [kvm[:, h * LANES:(h + 1) * LANES] for h in hs]
    vs = [kvm[:, MEM_W + h * LANES:MEM_W + (h + 1) * LANES] for h in hs]
    return qs, ks, vs, g_mq[...], g_mk[...]


def _mem_fwd(z, kvm, g_mq, g_mk, batch, seq, name, comm=None):
    n = z.shape[0]
    t = min(ROW_TILE, seq)
    per = seq // t

    def body(qm, kvm_ref, gq, gk, o_ref):
        o_ref[...] = _mem_core(*_mem_load(qm, kvm_ref, gq, gk)).astype(BF)

    return _pcall(
        body, grid=(n // t,),
        in_specs=[_rows(t, MEM_W, QM // MEM_W), pl.BlockSpec((MEM_LEN, 2 * MEM_W), lambda i: (i // per, 0)),
                  _full((1, LANES)), _full((1, LANES))],
        out_specs=_rows(t, MEM_W), out_shape=SDS((n, MEM_W), BF), sem=("parallel",), name=name,
        comm=comm)(z, kvm, g_mq, g_mk)


def _mem_bwd(z, kvm, g_mq, g_mk, dom, dz, batch, seq, name):
    n = z.shape[0]
    t = min(ROW_TILE, seq)
    per = seq // t

    def body(qm, kvm_ref, gq, gk, dom_ref, _, dz_ref, dkvm_ref, dgq_ref, dgk_ref):
        i = pl.program_id(0)
        _, vjp = jax.vjp(_mem_core, *_mem_load(qm, kvm_ref, gq, gk))
        dqs, dks, dvs, dgq, dgk = vjp(dom_ref[...])
        dz_ref[...] = jnp.concatenate(dqs, axis=1).astype(BF)
        _acc(dkvm_ref, jnp.concatenate(dks + dvs, axis=1), i % per == 0)
        _acc(dgq_ref, dgq, i == 0)
        _acc(dgk_ref, dgk, i == 0)

    kv_spec = pl.BlockSpec((MEM_LEN, 2 * MEM_W), lambda i: (i // per, 0))
    return pl.pallas_call(
        body, grid=(n // t,),
        in_specs=[_rows(t, MEM_W, QM // MEM_W), kv_spec, _full((1, LANES)), _full((1, LANES)), _rows(t, MEM_W), ANY],
        out_specs=[_rows(t, MEM_W, QM // MEM_W), kv_spec, _full((1, LANES)), _full((1, LANES))],
        out_shape=[SDS((n, Z_COLS), BF), SDS((batch * MEM_LEN, 2 * MEM_W), F32), SDS((1, LANES), F32),
                   SDS((1, LANES), F32)],
        input_output_aliases={5: 0},
        compiler_params=_params(("arbitrary",)), name=name)(z, kvm, g_mq, g_mk, dom, dz)


def _me():
    return lax.axis_index("x"), lax.axis_index("y"), lax.axis_index("c")


def _other_chips(x, y):
    return [(1 - x, y), (x, 1 - y), (1 - x, 1 - y)]


def _shard_shape(name):
    r, c = BIG_SHAPE[name]
    return (r, c // N_CHIPS) if name in COL_SHARDED else (r // N_CHIPS, c)


def _n_pieces(half_rows):
    return max(1, half_rows // PIECE_ROWS)


def _piece_plan(shapes):
    plan = []
    for r, _ in shapes:
        h = r // 2
        n = _n_pieces(h)
        plan.append((h, n, h // n))
    return plan


def _remote(send, recv, sem, src, dst, to):
    return pltpu.make_async_remote_copy(src_ref=src, dst_ref=dst, send_sem=send.at[sem], recv_sem=recv.at[sem],
                                        device_id=to, device_id_type=MESH)


def _gather_far(shards):
    plan = _piece_plan([s.shape for s in shards])
    n_far = 3 * sum(n for _, n, _ in plan)
    n_loc = 2 * sum(n for _, n, _ in plan)

    def copies(s_refs, o_refs, send, recv, local):
        x, y, c = _me()
        k = 2 * x + y
        mine, sends, arrivals = [], [], []
        for t, (h, n, pr) in enumerate(plan):
            s_ref, o_ref = s_refs[t], o_refs[t]
            for core in range(2):
                for p in range(n):
                    rows = pl.ds(core * h + p * pr, pr)
                    mine.append(pltpu.make_async_copy(s_ref.at[rows], o_ref.at[k, rows], local.at[len(mine)]))
            for chip in _other_chips(x, y):
                for p in range(n):
                    rows = pl.ds(c * h + p * pr, pr)
                    s = len(sends)
                    sends.append(_remote(send, recv, s, s_ref.at[rows], o_ref.at[k, rows], (*chip, c)))
                    arrivals.append(_remote(send, recv, s, s_ref.at[rows], o_ref.at[2 * chip[0] + chip[1], rows],
                                            (*chip, c)))
        return sends, arrivals, mine

    return _Phase(shards, [SDS((N_CHIPS,) + s.shape, s.dtype) for s in shards], n_far, n_loc, copies)


def _gather_near(bufs):
    plan = _piece_plan([b.shape[1:] for b in bufs])
    n_sem = 3 * sum(n for _, n, _ in plan)

    def copies(i_refs, o_refs, send, recv, local):
        x, y, c = _me()
        sib = (x, y, 1 - c)
        sends, arrivals = [], []
        for t, (h, n, pr) in enumerate(plan):
            for chip in _other_chips(x, y):
                ci = 2 * chip[0] + chip[1]
                for p in range(n):
                    rows = pl.ds(c * h + p * pr, pr)
                    rows_sib = pl.ds((1 - c) * h + p * pr, pr)
                    s = len(sends)
                    sends.append(_remote(send, recv, s, i_refs[t].at[ci, rows], o_refs[t].at[ci, rows], sib))
                    arrivals.append(_remote(send, recv, s, i_refs[t].at[ci, rows_sib], o_refs[t].at[ci, rows_sib], sib))
        return sends, arrivals, []

    return _Phase(bufs, [SDS(b.shape, b.dtype) for b in bufs], n_sem, 0, copies, {t: t for t in range(len(bufs))})


def _pair_exchange(grads):
    plan = _piece_plan([g.shape[1:] for g in grads])
    n_sem = sum(n for _, n, _ in plan)

    def copies(g_refs, o_refs, send, recv, local):
        x, y, c = _me()
        sends = []
        for t, (h, n, pr) in enumerate(plan):
            for p in range(n):
                sends.append(_remote(send, recv, len(sends), g_refs[t].at[:, pl.ds((1 - c) * h + p * pr, pr)],
                                     o_refs[t].at[:, pl.ds(p * pr, pr)], (x, y, 1 - c)))
        return sends, sends, []

    return _Phase(grads, [SDS((N_CHIPS, g.shape[1] // 2, g.shape[2]), F32) for g in grads], n_sem, 0, copies)


def _pair_add(ck, g, theirs, name):
    _, r, c = g.shape
    (h, n, pr), = _piece_plan([(r, c)])

    def body(ck_ref, g_ref, t_ref, pbf_ref):
        pbf_ref[...] = (g_ref[...] + t_ref[...]).astype(BF)

    half = pl.BlockSpec((None, pr, c), lambda k, p, ck: (k, p, 0))
    spec = pltpu.PrefetchScalarGridSpec(
        num_scalar_prefetch=1, grid=(N_CHIPS, n),
        in_specs=[pl.BlockSpec((None, pr, c), lambda k, p, ck: (k, ck[0] * n + p, 0)), half], out_specs=half)
    return pl.pallas_call(body, grid_spec=spec, out_shape=SDS((N_CHIPS, h, c), BF),
                          compiler_params=_params(("arbitrary", "arbitrary")), name=name)(ck, g, theirs)


def _scatter_partials(pbfs):
    plan = [(h, _n_pieces(h), h // _n_pieces(h)) for h in [p.shape[1] for p in pbfs]]
    n_sem = 3 * sum(n for _, n, _ in plan)

    def copies(p_refs, o_refs, send, recv, local):
        x, y, c = _me()
        sends = []
        for t, (h, n, pr) in enumerate(plan):
            for j, chip in enumerate(_other_chips(x, y)):
                for p in range(n):
                    rows = pl.ds(p * pr, pr)
                    sends.append(_remote(send, recv, len(sends), p_refs[t].at[2 * chip[0] + chip[1], rows],
                                         o_refs[t].at[j, rows], (*chip, c)))
        return sends, sends, []

    return _Phase(pbfs, [SDS((3,) + p.shape[1:], BF) for p in pbfs], n_sem, 0, copies)


def _sum_chips(ck, pbf, slots, name):
    _, h, c = pbf.shape
    n = _n_pieces(h)
    pr = h // n

    def body(ck_ref, p_ref, s_ref, o_ref):
        o_ref[...] = (((p_ref[...].astype(F32) + s_ref[0].astype(F32)) + s_ref[1].astype(F32))
                      + s_ref[2].astype(F32))

    spec = pltpu.PrefetchScalarGridSpec(
        num_scalar_prefetch=1, grid=(n,),
        in_specs=[pl.BlockSpec((None, pr, c), lambda p, ck: (ck[1], p, 0)),
                  pl.BlockSpec((3, pr, c), lambda p, ck: (0, p, 0))],
        out_specs=pl.BlockSpec((pr, c), lambda p, ck: (ck[0] * n + p, 0)))
    return pl.pallas_call(body, grid_spec=spec, out_shape=SDS((2 * h, c), F32),
                          compiler_params=_params(("arbitrary",)), name=name)(ck, pbf, slots)


def _join_halves(sums):
    plan = _piece_plan([s.shape for s in sums])
    n_sem = sum(n for _, n, _ in plan)

    def copies(r_refs, o_refs, send, recv, local):
        x, y, c = _me()
        sends, arrivals = [], []
        for t, (h, n, pr) in enumerate(plan):
            for p in range(n):
                rows = pl.ds(c * h + p * pr, pr)
                rows_sib = pl.ds((1 - c) * h + p * pr, pr)
                s = len(sends)
                sends.append(_remote(send, recv, s, r_refs[t].at[rows], o_refs[t].at[rows], (x, y, 1 - c)))
                arrivals.append(_remote(send, recv, s, r_refs[t].at[rows_sib], o_refs[t].at[rows_sib], (x, y, 1 - c)))
        return sends, arrivals, []

    return _Phase(sums, [SDS(s.shape, F32) for s in sums], n_sem, 0, copies, {t: t for t in range(len(sums))})


def _gather_small(s, name):
    def body(s_ref, o_ref, send, recv, local):
        x, y, c = _me()
        me = 4 * x + 2 * y + c
        keep = pltpu.make_async_copy(s_ref, o_ref.at[me], local)
        keep.start()
        sends = []
        for r in range(1, 8):
            fx, fy, fc = (r >> 2) & 1, (r >> 1) & 1, r & 1
            to = (x ^ fx, y ^ fy, c ^ fc)
            sends.append(pltpu.make_async_remote_copy(
                src_ref=s_ref, dst_ref=o_ref.at[me], send_sem=send.at[r - 1], recv_sem=recv.at[r - 1],
                device_id=to, device_id_type=MESH))
        for cp in sends:
            cp.start()
        for r in range(1, 8):
            fx, fy, fc = (r >> 2) & 1, (r >> 1) & 1, r & 1
            src = 4 * (x ^ fx) + 2 * (y ^ fy) + (c ^ fc)
            pltpu.make_async_remote_copy(
                src_ref=s_ref, dst_ref=o_ref.at[src], send_sem=send.at[r - 1], recv_sem=recv.at[r - 1],
                device_id=(x ^ fx, y ^ fy, c ^ fc), device_id_type=MESH).wait_recv()
        for cp in sends:
            cp.wait_send()
        keep.wait()

    return pl.pallas_call(
        body, in_specs=[ANY], out_specs=ANY, out_shape=SDS((8, SMALL_ROWS, LANES), F32),
        scratch_shapes=[pltpu.SemaphoreType.DMA((7,)), pltpu.SemaphoreType.DMA((7,)), pltpu.SemaphoreType.DMA],
        name=name)(s)


def _adam_math(w, g, m, v):
    nm = ADAM_B1 * m + (1.0 - ADAM_B1) * g
    nv = ADAM_B2 * v + (1.0 - ADAM_B2) * (g * g)
    m_hat = nm / (1.0 - ADAM_B1 ** ADAM_STEP)
    v_hat = nv / (1.0 - ADAM_B2 ** ADAM_STEP)
    return -ADAM_LR * (m_hat / (jnp.sqrt(v_hat) + ADAM_EPS) + ADAM_WD * w), nm, nv


def _adamw(w, g, m, v, name):
    _, r, c = w.shape
    t = max(d for d in range(8, r + 1, 8) if r % d == 0 and 16 * d * c * 4 <= VMEM_LIMIT - (8 << 20))

    def body(w_ref, g_ref, m_ref, v_ref, go_ref, d_ref, nm_ref, nv_ref):
        g_ = g_ref[...]
        d, nm, nv = _adam_math(w_ref[...], g_, m_ref[...], v_ref[...])
        go_ref[...] = g_
        d_ref[...] = d
        nm_ref[...] = nm
        nv_ref[...] = nv

    lead = pl.BlockSpec((None, t, c), lambda i: (0, i, 0))
    return pl.pallas_call(body, grid=(r // t,), in_specs=[lead, _rows(t, c), lead, lead], out_specs=[lead] * 4,
                          out_shape=[SDS((1, r, c), F32)] * 4, compiler_params=_params(("parallel",)),
                          name=name)(w, g, m, v)


def _small_layout():
    out, r0 = {}, 0
    for n in SMALL:
        size = int(np.prod(SMALL_SHAPE[n]))
        nr = -(-size // LANES)
        out[n] = (r0, nr)
        r0 += nr
    assert r0 <= SMALL_ROWS
    return out, r0


def _pack_small(grads, loss_tile, name):
    layout, used = _small_layout()

    def body(*refs):
        o_ref = refs[-1]
        o_ref[used:used + 1, :] = refs[-2][0:1, :]
        for n, ref in zip(SMALL, refs[:-2]):
            r0, nr = layout[n]
            if n == "w_spatial":
                for g in range(GM_GROUPS):
                    o_ref[r0 + g * GM_CHUNK:r0 + (g + 1) * GM_CHUNK, :] = ref[g]
            elif n == "b_spatial":
                o_ref[r0:r0 + nr, :] = ref[...]
            else:
                for i in range(nr):
                    o_ref[r0 + i:r0 + i + 1, :] = ref[:, i * LANES:(i + 1) * LANES]
        if used + 1 < SMALL_ROWS:
            o_ref[used + 1:SMALL_ROWS, :] = jnp.zeros((SMALL_ROWS - used - 1, LANES), F32)

    return pl.pallas_call(body, out_shape=SDS((SMALL_ROWS, LANES), F32), name=name)(*grads, loss_tile)


def _adamw_small(gathered, ws, ms, vs, name):
    layout, used = _small_layout()
    n_t = len(SMALL)

    def body(*refs):
        g_ref = refs[0]
        w_refs, m_refs, v_refs = refs[1:1 + n_t], refs[1 + n_t:1 + 2 * n_t], refs[1 + 2 * n_t:1 + 3 * n_t]
        outs = refs[1 + 3 * n_t:1 + 7 * n_t]
        acc = refs[-1]
        total = g_ref[0]
        for j in range(1, 8):
            total = total + g_ref[j]
        acc[...] = total
        refs[1 + 7 * n_t][...] = acc[used:used + 1, :]
        for t, n in enumerate(SMALL):
            r0, nr = layout[n]
            o_refs = [outs[t], outs[n_t + t], outs[2 * n_t + t], outs[3 * n_t + t]]
            if n == "w_spatial":
                views = [((0, g), slice(r0 + g * GM_CHUNK, r0 + (g + 1) * GM_CHUNK), slice(None))
                         for g in range(GM_GROUPS)]
            elif n == "b_spatial":
                views = [((0,), slice(r0, r0 + nr), slice(None))]
            else:
                width = SMALL_SHAPE[n][1]
                views = [((slice(None), slice(i * LANES, min((i + 1) * LANES, width))), slice(r0 + i, r0 + i + 1),
                          slice(0, min(LANES, width - i * LANES))) for i in range(nr)]
            for idx, rows, lanes in views:
                g = acc[rows, lanes]
                d, nm, nv = _adam_math(w_refs[t][idx], g, m_refs[t][idx], v_refs[t][idx])
                for ref, val in zip(o_refs, (g, d, nm, nv)):
                    ref[idx] = val

    shapes = [SDS(SMALL_SHAPE[n], F32) for n in SMALL]
    return pl.pallas_call(body, out_shape=shapes * 4 + [SDS((1, LANES), F32)],
                          scratch_shapes=[pltpu.VMEM((SMALL_ROWS, LANES), F32)], name=name)(gathered, *ws, *ms, *vs)


def _win_layout(w_in):
    pad = jnp.zeros((w_in.shape[0], LANES - MLA_ROPE), w_in.dtype)
    u, v, cq = w_in[:, 0:512], w_in[:, 512:1024], w_in[:, 1024:1408]
    ckv, kpe, qm, zg = w_in[:, 1408:1664], w_in[:, 1664:1728], w_in[:, 1728:2240], w_in[:, 2240:5312]
    return jnp.concatenate([zg, u, v, qm, cq, kpe, pad, ckv], axis=1)


def _win_unlayout(g):
    zg, u, v, qm = g[:, ZG:ZG + 3072], g[:, ZU:ZU + 512], g[:, ZV:ZV + 512], g[:, QM:QM + 512]
    cq, kpe, ckv = g[:, CQ:CQ + 384], g[:, KPE:KPE + MLA_ROPE], g[:, CKV:CKV + 256]
    return jnp.concatenate([u, v, cq, ckv, kpe, qm, zg], axis=1)


def _wq_layout(w_uq):
    w = w_uq.reshape(Q_LORA, MLA_HEADS, MLA_NOPE + MLA_ROPE)
    nope = w[:, :, :MLA_NOPE].reshape(Q_LORA, MLA_HEADS * MLA_NOPE)
    pe = jnp.pad(w[:, :, MLA_NOPE:], ((0, 0), (0, 0), (0, LANES - MLA_ROPE))).reshape(Q_LORA, MLA_HEADS * LANES)
    return jnp.concatenate([nope, pe], axis=1)


def _wq_unlayout(g):
    nope = g[:, :1024].reshape(Q_LORA, MLA_HEADS, MLA_NOPE)
    pe = g[:, 1024:].reshape(Q_LORA, MLA_HEADS, LANES)[:, :, :MLA_ROPE]
    return jnp.concatenate([nope, pe], axis=2).reshape(Q_LORA, MLA_HEADS * (MLA_NOPE + MLA_ROPE))


def _wkv_layout(w_ukv):
    w = w_ukv.reshape(KV_LORA, MLA_HEADS, MLA_NOPE + MLA_V)
    return jnp.concatenate([w[:, :, :MLA_NOPE].reshape(KV_LORA, 1024), w[:, :, MLA_NOPE:].reshape(KV_LORA, 1024)],
                           axis=1)


def _wkv_unlayout(g):
    kn = g[:, :1024].reshape(KV_LORA, MLA_HEADS, MLA_NOPE)
    v = g[:, 1024:].reshape(KV_LORA, MLA_HEADS, MLA_V)
    return jnp.concatenate([kn, v], axis=2).reshape(KV_LORA, MLA_HEADS * (MLA_NOPE + MLA_V))


def _owner_major(g, name):
    r, c = _shard_shape(name)
    return g.reshape(r, N_CHIPS, c).transpose(1, 0, 2) if name in COL_SHARDED else g.reshape(N_CHIPS, r, c)


def _pad_lanes(g):
    return jnp.pad(g, ((0, 0), (0, LANES - g.shape[1])))


def kernel(x, mem, positions, g_mix, w_in, g_cq, w_uq, g_ckv, w_ukv, g_q_nope, g_q_pe, g_k_nope, g_k_pe, g_gm_ln, b_gm_ln, w_spatial, b_spatial, g_mem, w_mem_kv, g_mq, g_mk, w_o_gm, w_o_mla, w_o_mem, w_out, g_ffn, w_ff1, w_ff2, loss_target, m_g_mix, m_w_in, m_g_cq, m_w_uq, m_g_ckv, m_w_ukv, m_g_q_nope, m_g_q_pe, m_g_k_nope, m_g_k_pe, m_g_gm_ln, m_b_gm_ln, m_w_spatial, m_b_spatial, m_g_mem, m_w_mem_kv, m_g_mq, m_g_mk, m_w_o_gm, m_w_o_mla, m_w_o_mem, m_w_out, m_g_ffn, m_w_ff1, m_w_ff2, v_g_mix, v_w_in, v_g_cq, v_w_uq, v_g_ckv, v_w_ukv, v_g_q_nope, v_g_q_pe, v_g_k_nope, v_g_k_pe, v_g_gm_ln, v_b_gm_ln, v_w_spatial, v_b_spatial, v_g_mem, v_w_mem_kv, v_g_mq, v_g_mk, v_w_o_gm, v_w_o_mla, v_w_o_mem, v_w_out, v_g_ffn, v_w_ff1, v_w_ff2):
    given = dict(locals())
    wts = {n: given[n] for n in WEIGHTS}
    mom = {n: given["m_" + n] for n in WEIGHTS}
    var = {n: given["v_" + n] for n in WEIGHTS}
    batch, seq, _ = x.shape
    n_tok = batch * seq

    def natural(n, g):
        r, c = _shard_shape(n)
        return g.transpose(1, 0, 2).reshape(r, N_CHIPS * c) if n in COL_SHARDED else g.reshape(N_CHIPS * r, c)

    def far(names):
        return _gather_far([wts[n][0].astype(BF) for n in names])

    x2 = x.reshape(n_tok, D_MODEL)
    tgt2 = loss_target.reshape(n_tok, D_MODEL)
    mem2 = mem.reshape(batch * MEM_LEN, D_MODEL)
    pos_f = positions.reshape(n_tok, 1).astype(F32)

    inv = ROPE_BASE ** (-jnp.arange(0, MLA_ROPE, 2, dtype=F32) / MLA_ROPE)
    zeros64 = jnp.zeros((LANES - MLA_ROPE,), F32)
    inv_full = jnp.concatenate([inv, inv, zeros64]).reshape(1, LANES)
    half = MLA_ROPE // 2
    cmask = jnp.concatenate([jnp.ones((MLA_ROPE,), F32), zeros64]).reshape(1, LANES)
    smask = jnp.concatenate([-jnp.ones((half,), F32), jnp.ones((half,), F32), zeros64]).reshape(1, LANES)

    prep_gains = [g_cq, g_ckv, g_q_nope, _pad_lanes(g_q_pe), g_k_nope, _pad_lanes(g_k_pe)]
    ws = w_spatial[0]
    bcols = [b_spatial[0, g].reshape(GM_CHUNK, 1) for g in range(GM_GROUPS)]

    h1, in_far = _rms_fwd(x2, g_mix, "rms_mix", comm=far(EARLY[:1]))
    (cos_f, sin_s), early = _rope_tables(pos_f, inv_full, cmask, smask, "rope_tables",
                                         comm=_together(_gather_near(in_far), far(EARLY[1:])))
    memn, rest = _rms_fwd(mem2, g_mem, "rms_mem", comm=_gather_near(early[1:]))
    full = {n: natural(n, g) for n, g in zip(EARLY, list(early[:1]) + list(rest))}
    win = _win_layout(full["w_in"])
    wq = _wq_layout(full["w_uq"])
    wkv = _wkv_layout(full["w_ukv"])
    z, proj_far = _mm(h1, win, out_dtypes=(BF,), name="mm_in", comm=far(LATE_PROJ))
    gm = _gm_fwd(z, g_gm_ln, b_gm_ln, ws, bcols, "gm_fwd")
    qc, kc, vc = _prep_fwd(z, cos_f, sin_s, prep_gains, wq, wkv, "prep_fwd")
    (o_mla, lse), ff_far = _mla_fwd(qc, kc, vc, batch, seq, "mla_fwd", comm=far(LATE_FF))
    kvm, proj = _mm(memn, full["w_mem_kv"], name="mm_memkv", comm=_gather_near(proj_far))
    o_mem, ff = _mem_fwd(z, kvm, g_mq, g_mk, batch, seq, "mem_fwd", comm=_gather_near(ff_far))
    full.update({n: natural(n, g) for n, g in zip(LATE_PROJ + LATE_FF, list(proj) + list(ff))})
    y_gm = _mm(gm, full["w_o_gm"], out_dtypes=(BF,), name="mm_o_gm")
    y_mla = _mm(o_mla, full["w_o_mla"], out_dtypes=(BF,), name="mm_o_mla")
    y_mem = _mm(o_mem, full["w_o_mem"], out_dtypes=(BF,), name="mm_o_mem")
    merged = _merge_fwd(z, y_gm, y_mla, y_mem, "merge_fwd")
    x1, h2 = _mm(merged, full["w_out"], ins=(x2,), row_ins=(g_ffn,), epilogue=_residual_rms, out_dtypes=(F32, BF),
                 name="mm_out")
    a_ff, r_ff = _mm(h2, full["w_ff1"], epilogue=_relu2, out_dtypes=(BF, BF), name="mm_ff1")
    dy, dyb, loss_tile = _mm(r_ff, full["w_ff2"], ins=(x1, tgt2), epilogue=_loss_tail, out_dtypes=(F32, BF),
                             total=True, name="mm_ff2")

    gw = {}
    da = _mm(dyb, full["w_ff2"], tb=True, ins=(a_ff,), epilogue=_relu2_bwd, out_dtypes=(BF,), name="mm_d_a")
    gw["w_ff2"] = _owner_major(_mm(r_ff, dyb, ta=True, name="mm_dw_ff2"), "w_ff2")
    gw["w_ff1"] = _mm(h2, da, ta=True, owner_cols=D_FF // N_CHIPS, name="mm_dw_ff1")
    dx1, dx1b, dg_ffn = _mm(da, full["w_ff1"], tb=True, ins=(x1, dy), row_ins=(g_ffn,), epilogue=_rms_bwd_tail,
                            out_dtypes=(F32, BF), total=(1, D_MODEL), name="mm_d_h2")
    dmerged = _mm(dx1b, full["w_out"], tb=True, name="mm_d_merged")
    gw["w_out"] = _owner_major(_mm(merged, dx1b, ta=True, name="mm_dw_out"), "w_out")
    dz, dy_gm, dy_mla, dy_mem = _merge_bwd(z, y_gm, y_mla, y_mem, dmerged, "merge_bwd")
    dgm = _mm(dy_gm, full["w_o_gm"], tb=True, name="mm_d_gm")
    gw["w_o_gm"] = _mm(gm, dy_gm, ta=True, owner_cols=D_MODEL // N_CHIPS, name="mm_dw_o_gm")
    do_mla = _mm(dy_mla, full["w_o_mla"], tb=True, name="mm_d_omla")
    gw["w_o_mla"] = _owner_major(_mm(o_mla, dy_mla, ta=True, name="mm_dw_o_mla"), "w_o_mla")
    do_mem = _mm(dy_mem, full["w_o_mem"], tb=True, name="mm_d_omem")
    gw["w_o_mem"] = _mm(o_mem, dy_mem, ta=True, owner_cols=D_MODEL // N_CHIPS, name="mm_dw_o_mem")
    ck = jnp.stack([lax.axis_index("c"), 2 * lax.axis_index("x") + lax.axis_index("y")]).astype(jnp.int32)

    def pair_sums(names, theirs):
        return [_pair_add(ck, gw[n], t, "pair_add_" + n) for n, t in zip(names, theirs)]

    def chip_sums(names, pairs, slots):
        return [_sum_chips(ck, p, s, "sum_chips_" + n) for n, p, s in zip(names, pairs, slots)]

    (dz, dg_ln, db_ln, dws, *dbcols), theirs = _gm_bwd(z, g_gm_ln, b_gm_ln, ws, bcols, dgm, dz, "gm_bwd",
                                                      comm=_pair_exchange([gw[n] for n in LATE]))
    pairs = pair_sums(LATE, theirs)
    (dq, dk, dv), slots = _mla_bwd(qc, kc, vc, o_mla, lse, do_mla, batch, seq, "mla_bwd",
                                   comm=_scatter_partials(pairs))
    sums = chip_sums(LATE, pairs, slots)
    (dz, dg_cq, dg_ckv, dg_qn, dg_qp, dg_kn, dg_kp, dwq, dwkv), reduced_late = _prep_bwd(
        z, cos_f, sin_s, prep_gains, wq, wkv, dq, dk, dv, dz, "prep_bwd", comm=_join_halves(sums))
    dz, dkvm, dg_mq, dg_mk = _mem_bwd(z, kvm, g_mq, g_mk, do_mem, dz, batch, seq, "mem_bwd")
    dmemn = _mm(dkvm, full["w_mem_kv"], tb=True, name="mm_d_memn")
    gw["w_mem_kv"] = _owner_major(_mm(memn, dkvm, ta=True, name="mm_dw_memkv"), "w_mem_kv")
    _, _, dg_mem = _rms_bwd(mem2, g_mem, dmemn, None, "rms_mem_bwd")
    gw["w_in"] = _owner_major(_win_unlayout(_mm(h1, dz, ta=True, name="mm_dw_in")), "w_in")
    gw["w_uq"] = _owner_major(_wq_unlayout(dwq), "w_uq")
    gw["w_ukv"] = _owner_major(_wkv_unlayout(dwkv), "w_ukv")
    dh1, theirs = _mm(dz, win, tb=True, name="mm_d_h1_top", rows=(0, 2), comm=_pair_exchange([gw[n] for n in EARLY]))
    pairs = pair_sums(EARLY, theirs)
    dh1, slots = _mm(dz, win, tb=True, name="mm_d_h1_bottom", rows=(1, 2), into=dh1,
                     comm=_scatter_partials(pairs))
    grad_x, _, dg_mix = _rms_bwd(x2, g_mix, dh1, dx1, "rms_mix_bwd")
    reduced_early = _run_phase(_join_halves(chip_sums(EARLY, pairs, slots)), "join_early")
    reduced = dict(zip(LATE + EARLY, list(reduced_late) + list(reduced_early)))

    def swapped(a):
        return jnp.swapaxes(a, -1, -2)

    results = {n: _adamw(wts[n], reduced[n], mom[n], var[n], "adamw_" + n) for n in BIG if n != "w_in"}
    results["w_in"] = [swapped(r) for r in _adamw(swapped(w_in), swapped(reduced["w_in"]), swapped(m_w_in),
                                                  swapped(v_w_in), "adamw_w_in")]

    small_g = {"g_mix": dg_mix, "g_cq": dg_cq, "g_ckv": dg_ckv, "g_q_nope": dg_qn, "g_q_pe": dg_qp,
               "g_k_nope": dg_kn, "g_k_pe": dg_kp, "g_gm_ln": dg_ln, "b_gm_ln": db_ln, "w_spatial": dws,
               "b_spatial": jnp.concatenate(dbcols, axis=1).T, "g_mem": dg_mem, "g_mq": dg_mq, "g_mk": dg_mk,
               "g_ffn": dg_ffn}
    packed = _pack_small([small_g[n] for n in SMALL], loss_tile, "pack_small")
    small_out = _adamw_small(_gather_small(packed, "gather_small"), [wts[n] for n in SMALL],
                             [mom[n] for n in SMALL], [var[n] for n in SMALL], "adamw_small")
    for t, n in enumerate(SMALL):
        results[n] = [small_out[j * len(SMALL) + t] for j in range(4)]

    loss = small_out[4 * len(SMALL)][0, 0]
    grad_x = grad_x.reshape(batch, seq, D_MODEL)
    return (loss, grad_x, *[results[n][0] for n in WEIGHTS], *[results[n][1] for n in WEIGHTS],
            *[results[n][2] for n in WEIGHTS], *[results[n][3] for n in WEIGHTS])
```

```python
import functools
import math

import numpy as np
import jax
import jax.numpy as jnp
from jax import lax
from jax.experimental import pallas as pl
from jax.experimental.pallas import tpu as pltpu

F32 = jnp.float32
BF = jnp.bfloat16
SDS = jax.ShapeDtypeStruct
MESH = pl.DeviceIdType.MESH

D_MODEL = 1024
MEM_LEN = 256
MEM_HEADS = 4
HEAD_DIM = 128
GM_WIDTH = 512
GM_CHUNK = 128
GM_GROUPS = 4
MLA_HEADS = 8
MLA_NOPE = 128
MLA_ROPE = 64
MLA_V = 128
Q_LORA = 384
KV_LORA = 256
ROPE_BASE = 10000.0
D_FF = 4096
EPS = 1e-6
W_IN_COLS = 5312
ADAM_LR, ADAM_B1, ADAM_B2, ADAM_EPS, ADAM_WD, ADAM_STEP = 0.001, 0.9, 0.999, 1e-08, 0.01, 10

ZG, ZU, ZV, QM, CQ, KPE, CKV = 0, 3072, 3584, 4096, 4608, 4992, 5120
Z_COLS = 5376
LANES = 128
ROW_TILE = 512
ATT_TILE = 1024
ATT_HEADS = 2
ATT_HEADS_FWD = 4
VMEM_LIMIT = 60 * 1024 * 1024

N_CHIPS = 4
PIECE_ROWS = 256
SMALL_ROWS = 560

BIG = ["w_in", "w_uq", "w_ukv", "w_mem_kv", "w_o_gm", "w_o_mla", "w_o_mem", "w_out", "w_ff1", "w_ff2"]
BIG_SHAPE = {"w_in": (1024, 5312), "w_uq": (384, 1536), "w_ukv": (256, 2048), "w_mem_kv": (1024, 1024),
             "w_o_gm": (512, 1024), "w_o_mla": (1024, 1024), "w_o_mem": (512, 1024), "w_out": (1024, 1024),
             "w_ff1": (1024, 4096), "w_ff2": (4096, 1024)}
COL_SHARDED = {"w_in", "w_uq", "w_ukv", "w_o_gm", "w_o_mem", "w_ff1"}
EARLY = ["w_in", "w_uq", "w_ukv", "w_mem_kv"]
LATE_PROJ = ["w_o_gm", "w_o_mla", "w_o_mem", "w_out"]
LATE_FF = ["w_ff1", "w_ff2"]
LATE = LATE_PROJ + LATE_FF
SMALL = ["w_spatial", "b_spatial", "g_mix", "g_cq", "g_ckv", "g_q_nope", "g_q_pe", "g_k_nope", "g_k_pe", "g_gm_ln",
         "b_gm_ln", "g_mem", "g_mq", "g_mk", "g_ffn"]
SMALL_SHAPE = {"g_mix": (1, 1024), "g_cq": (1, 384), "g_ckv": (1, 256), "g_q_nope": (1, 128), "g_q_pe": (1, 64),
               "g_k_nope": (1, 128), "g_k_pe": (1, 64), "g_gm_ln": (1, 512), "b_gm_ln": (1, 512),
               "w_spatial": (1, 4, 128, 128), "b_spatial": (1, 4, 128), "g_mem": (1, 1024), "g_mq": (1, 128),
               "g_mk": (1, 128), "g_ffn": (1, 1024)}
WEIGHTS = ['g_mix', 'w_in', 'g_cq', 'w_uq', 'g_ckv', 'w_ukv', 'g_q_nope', 'g_q_pe', 'g_k_nope', 'g_k_pe',
           'g_gm_ln', 'b_gm_ln', 'w_spatial', 'b_spatial', 'g_mem', 'w_mem_kv', 'g_mq', 'g_mk', 'w_o_gm',
           'w_o_mla', 'w_o_mem', 'w_out', 'g_ffn', 'w_ff1', 'w_ff2']


def _params(sem=None):
    return pltpu.CompilerParams(vmem_limit_bytes=VMEM_LIMIT, dimension_semantics=sem)


def _pick(n, prefs):
    for p in prefs:
        if n % p == 0:
            return p
    return n


def _full(shape):
    nd = len(shape)
    return pl.BlockSpec(shape, lambda *_: (0,) * nd)


def _rows(t, w, blk=0):
    return pl.BlockSpec((t, w), lambda i: (i, blk))


def _acc(ref, val, first):
    @pl.when(first)
    def _():
        ref[...] = val

    @pl.when(jnp.logical_not(first))
    def _():
        ref[...] += val


ANY = pl.BlockSpec(memory_space=pl.ANY)


class _Phase:
    def __init__(self, operands, out_shapes, n_sem, n_local, copies, aliases=None):
        self.operands, self.out_shapes, self.aliases = list(operands), list(out_shapes), dict(aliases or {})
        self.n_sem, self.n_local, self.copies = n_sem, max(n_local, 1), copies

    def sem_shapes(self):
        return [pltpu.SemaphoreType.DMA((self.n_sem,)), pltpu.SemaphoreType.DMA((self.n_sem,)),
                pltpu.SemaphoreType.DMA((self.n_local,))]

    def start(self, ins, outs, send, recv, local):
        sends, _, locals_ = self.copies(ins, outs, send, recv, local)
        for cp in locals_ + sends:
            cp.start()

    def finish(self, ins, outs, send, recv, local):
        sends, arrivals, locals_ = self.copies(ins, outs, send, recv, local)
        for cp in arrivals:
            cp.wait_recv()
        for cp in sends:
            cp.wait_send()
        for cp in locals_:
            cp.wait()


class _Shifted:
    def __init__(self, ref, base):
        self.ref, self.base = ref, base

    @property
    def at(self):
        return self

    def __getitem__(self, i):
        return self.ref.at[i + self.base]


def _together(first, second):
    n_in, n_out = len(first.operands), len(first.out_shapes)

    def copies(ins, outs, send, recv, local):
        a = first.copies(ins[:n_in], outs[:n_out], send, recv, local)
        b = second.copies(ins[n_in:], outs[n_out:], _Shifted(send, first.n_sem), _Shifted(recv, first.n_sem),
                          _Shifted(local, first.n_local))
        return a[0] + b[0], a[1] + b[1], a[2] + b[2]

    aliases = {**first.aliases, **{n_in + i: n_out + j for i, j in second.aliases.items()}}
    return _Phase(first.operands + second.operands, first.out_shapes + second.out_shapes, first.n_sem + second.n_sem,
                  first.n_local + second.n_local, copies, aliases)


def _run_phase(phase, name):
    n_in = len(phase.operands)

    def body(*refs):
        ins, outs, sems = refs[:n_in], refs[n_in:n_in + len(phase.out_shapes)], refs[n_in + len(phase.out_shapes):]
        phase.start(ins, outs, *sems)
        phase.finish(ins, outs, *sems)

    return pl.pallas_call(body, in_specs=[ANY] * n_in, out_specs=[ANY] * len(phase.out_shapes),
                          out_shape=phase.out_shapes, scratch_shapes=phase.sem_shapes(),
                          input_output_aliases=phase.aliases, name=name)(*phase.operands)


def _pcall(body, *, grid, in_specs, out_specs, out_shape, scratch_shapes=(), sem=None, name, comm=None, aliases=None):
    single = not isinstance(out_shape, (list, tuple))
    o_specs = [out_specs] if single else list(out_specs)
    o_shape = [out_shape] if single else list(out_shape)
    aliases = dict(aliases or {})
    if comm is None:
        call = pl.pallas_call(body, grid=grid, in_specs=list(in_specs), out_specs=o_specs, out_shape=o_shape,
                              scratch_shapes=list(scratch_shapes), input_output_aliases=aliases,
                              compiler_params=_params(sem), name=name)

        def run_plain(*args):
            res = call(*args)
            return res[0] if single else res

        return run_plain

    n_in, n_out, n_scr = len(in_specs), len(o_specs), len(scratch_shapes)
    nc_in, nc_out = len(comm.operands), len(comm.out_shapes)

    def wrapped(*refs):
        ins, cins = refs[:n_in], refs[n_in:n_in + nc_in]
        o0 = n_in + nc_in
        outs, couts = refs[o0:o0 + n_out], refs[o0 + n_out:o0 + n_out + nc_out]
        s0 = o0 + n_out + nc_out
        scr, csem = refs[s0:s0 + n_scr], refs[s0 + n_scr:]
        ids = [pl.program_id(d) for d in range(len(grid))]
        first = functools.reduce(jnp.logical_and, [i == 0 for i in ids])
        last = functools.reduce(jnp.logical_and, [i == g - 1 for i, g in zip(ids, grid)])

        @pl.when(first)
        def _():
            comm.start(cins, couts, *csem)

        body(*ins, *outs, *scr)

        @pl.when(last)
        def _():
            comm.finish(cins, couts, *csem)

    call = pl.pallas_call(
        wrapped, grid=grid, in_specs=list(in_specs) + [ANY] * nc_in, out_specs=o_specs + [ANY] * nc_out,
        out_shape=o_shape + comm.out_shapes, scratch_shapes=list(scratch_shapes) + comm.sem_shapes(),
        input_output_aliases={**aliases, **{n_in + i: n_out + j for i, j in comm.aliases.items()}},
        compiler_params=_params(("arbitrary",) * len(grid)), name=name)

    def run_carrying(*args):
        res = call(*args, *comm.operands)
        return (res[0] if single else res[:n_out]), res[n_out:]

    return run_carrying


def _dn(a, b, ca, cb):
    return lax.dot_general(a.astype(BF), b.astype(BF), (((ca,), (cb,)), ((), ())), preferred_element_type=F32)


@jax.custom_vjp
def _mm_nn(a, b):
    return _dn(a, b, 1, 0)


def _mm_nn_fwd(a, b):
    return _dn(a, b, 1, 0), (a.astype(BF), b.astype(BF))


def _mm_nn_bwd(res, ct):
    a, b = res
    return _dn(ct, b, 1, 1), _dn(a, ct, 0, 0)


_mm_nn.defvjp(_mm_nn_fwd, _mm_nn_bwd)


@jax.custom_vjp
def _mm_nt(a, b):
    return _dn(a, b, 1, 1)


def _mm_nt_fwd(a, b):
    return _dn(a, b, 1, 1), (a.astype(BF), b.astype(BF))


def _mm_nt_bwd(res, ct):
    a, b = res
    return _dn(ct, b, 1, 0), _dn(ct, a, 0, 0)


_mm_nt.defvjp(_mm_nt_fwd, _mm_nt_bwd)


def _rmsn(x, g, n):
    ms = jnp.sum(x * x, axis=-1, keepdims=True) * (1.0 / n)
    return x * lax.rsqrt(ms + EPS) * g


def _layernorm(x, g, b):
    mu = jnp.mean(x, axis=-1, keepdims=True)
    xc = x - mu
    y = xc * lax.rsqrt(jnp.mean(xc * xc, axis=-1, keepdims=True) + EPS)
    return y * g + b


def _swap_lanes(x):
    half = MLA_ROPE // 2
    lane = lax.broadcasted_iota(jnp.int32, x.shape, 1)
    return jnp.where(lane < half, pltpu.roll(x, LANES - half, axis=1),
                     jnp.where(lane < MLA_ROPE, pltpu.roll(x, half, axis=1), 0.0))


@jax.custom_vjp
def _swap_halves(x):
    return _swap_lanes(x)


_swap_halves.defvjp(lambda x: (_swap_lanes(x), None), lambda _, ct: (_swap_lanes(ct),))


def _rope(x, cos_f, sin_s):
    return x * cos_f + _swap_halves(x) * sin_s


def _lane_blocks(x):
    return tuple(x[:, i * LANES:(i + 1) * LANES] for i in range(x.shape[1] // LANES))


@jax.custom_vjp
def _split_lanes(x):
    return _lane_blocks(x)


_split_lanes.defvjp(lambda x: (_lane_blocks(x), None), lambda _, cts: (jnp.concatenate(cts, axis=1),))


def _softmax(s):
    m = lax.stop_gradient(jnp.max(s, axis=-1, keepdims=True))
    p = jnp.exp(s - m)
    return p / jnp.sum(p, axis=-1, keepdims=True)


def _mm(a, b, *, ta=False, tb=False, ins=(), row_ins=(), epilogue=None, out_dtypes=(F32,), owner_cols=None,
        total=False, name, comm=None, rows=None, into=None):
    if ta:
        k_dim, m = a.shape
    else:
        m, k_dim = a.shape
    if tb:
        n, kb = b.shape
    else:
        kb, n = b.shape
    assert k_dim == kb, (a.shape, b.shape, ta, tb)
    part, n_parts = rows if rows is not None else (0, 1)
    tm = _pick(m // n_parts, (1024, 512, 256, 128))
    tn = _pick(n if owner_cols is None else owner_cols, (1024, 768, 512, 384, 256, 128))
    tk = _pick(k_dim, (2048, 1024, 768, 512, 256, 128))
    nk = k_dim // tk
    m_steps = m // tm // n_parts
    off = part * m_steps
    ca = 0 if ta else 1
    cb = 1 if tb else 0
    n_in = len(ins) + len(row_ins)
    n_out = len(out_dtypes)
    n_pass = 0 if into is None else 1
    total_shape = total if isinstance(total, tuple) else (8, LANES)
    assert not (isinstance(total, tuple) and n != tn), "a per-column total needs the whole width in one tile"

    def finish(r, in_refs, out_refs, first_tile):
        vals = epilogue(r, *[ref[...].astype(F32) for ref in in_refs]) if epilogue is not None else (r,)
        for ref, val, dt in zip(out_refs, vals, out_dtypes):
            ref[...] = val.astype(dt)
        if total:
            _acc(out_refs[n_out], vals[n_out], first_tile)

    def body(*refs):
        a_ref, b_ref = refs[:2]
        in_refs = refs[2:2 + n_in]
        o0 = 2 + n_in + n_pass
        out_refs = refs[o0:o0 + n_out + int(bool(total))]
        first_tile = jnp.logical_and(pl.program_id(0) == 0, pl.program_id(1) == 0)
        part = _dn(a_ref[...], b_ref[...], ca, cb)
        if nk == 1:
            finish(part, in_refs, out_refs, first_tile)
            return
        acc = refs[-1]
        k = pl.program_id(2)
        _acc(acc, part, k == 0)

        @pl.when(k == nk - 1)
        def _():
            finish(acc[...], in_refs, out_refs, first_tile)

    a_spec = (pl.BlockSpec((tk, tm), lambda i, j, k: (k, i + off)) if ta
              else pl.BlockSpec((tm, tk), lambda i, j, k: (i + off, k)))
    b_spec = pl.BlockSpec((tn, tk), lambda i, j, k: (j, k)) if tb else pl.BlockSpec((tk, tn), lambda i, j, k: (k, j))
    t_spec = pl.BlockSpec((tm, tn), lambda i, j, k: (i + off, j))
    if owner_cols is None:
        o_spec, o_shape = t_spec, (m, n)
    else:
        per = owner_cols // tn
        o_spec = pl.BlockSpec((None, tm, tn), lambda i, j, k: (j // per, i + off, j % per))
        o_shape = (n // owner_cols, m, owner_cols)
    o_specs = [o_spec] * n_out + ([pl.BlockSpec(total_shape, lambda i, j, k: (0, 0))] if total else [])
    o_shapes = [SDS(o_shape, dt) for dt in out_dtypes] + ([SDS(total_shape, F32)] if total else [])
    row_spec = pl.BlockSpec((1, tn), lambda i, j, k: (0, j))
    in_specs = [a_spec, b_spec] + [t_spec] * len(ins) + [row_spec] * len(row_ins) + [ANY] * n_pass
    args = [a, b, *ins, *row_ins] + ([into] if n_pass else [])
    run = _pcall(body, grid=(m_steps, n // tn, nk), in_specs=in_specs, out_specs=o_specs, out_shape=o_shapes,
                 scratch_shapes=[pltpu.VMEM((tm, tn), F32)] if nk > 1 else [],
                 sem=("arbitrary",) * 3 if total else ("parallel", "parallel", "arbitrary"), name=name, comm=comm,
                 aliases={len(in_specs) - 1: 0} if n_pass else None)
    if comm is None:
        outs = run(*args)
        return outs[0] if len(outs) == 1 else outs
    outs, exchanged = run(*args)
    return (outs[0] if len(outs) == 1 else outs), exchanged


def _add_to(r, x):
    return (r + x,)


def _residual_rms(r, x, g):
    x1 = r + x
    return x1, _rmsn(x1, g, D_MODEL)


def _rms_bwd_tail(dh, x, res, g):
    _, vjp = jax.vjp(lambda xx, gg: _rmsn(xx, gg, D_MODEL), x, g)
    dx, dg = vjp(dh)
    dx = dx + res
    return dx, dx, dg


def _relu2(r):
    p = jnp.maximum(r, 0.0)
    return r, p * p


def _relu2_bwd(dr, a):
    return (dr * (2.0 * jnp.maximum(a, 0.0)),)


def _loss_tail(r, x1, tgt):
    e = (r + x1) - tgt
    dy = e * (1.0 / D_MODEL)
    part = jnp.sum(jnp.sum(e * e, axis=-1, keepdims=True), axis=0, keepdims=True) * (0.5 / D_MODEL)
    return dy, dy, jnp.broadcast_to(part, (8, LANES))


def _rms_fwd(x, g, name, comm=None):
    n, w = x.shape
    t = min(ROW_TILE, n)

    def body(x_ref, g_ref, o_ref):
        o_ref[...] = _rmsn(x_ref[...], g_ref[...], w).astype(BF)

    return _pcall(body, grid=(n // t,), in_specs=[_rows(t, w), _full((1, w))], out_specs=_rows(t, w),
                  out_shape=SDS((n, w), BF), sem=("arbitrary",), name=name, comm=comm)(x, g)


def _rms_bwd(x, g, dh, res, name, comm=None):
    n, w = x.shape
    t = min(ROW_TILE, n)
    has_res = res is not None

    def body(*refs):
        if has_res:
            x_ref, g_ref, dh_ref, res_ref, dx_ref, dxb_ref, dg_ref = refs
        else:
            x_ref, g_ref, dh_ref, dx_ref, dxb_ref, dg_ref = refs
        _, vjp = jax.vjp(lambda xx, gg: _rmsn(xx, gg, w), x_ref[...], g_ref[...])
        dx, dg = vjp(dh_ref[...])
        if has_res:
            dx = dx + res_ref[...]
        dx_ref[...] = dx
        dxb_ref[...] = dx.astype(BF)
        _acc(dg_ref, dg, pl.program_id(0) == 0)

    in_specs = [_rows(t, w), _full((1, w)), _rows(t, w)] + ([_rows(t, w)] if has_res else [])
    args = [x, g, dh] + ([res] if has_res else [])
    return _pcall(body, grid=(n // t,), in_specs=in_specs, out_specs=[_rows(t, w), _rows(t, w), _full((1, w))],
                  out_shape=[SDS((n, w), F32), SDS((n, w), BF), SDS((1, w), F32)], sem=("arbitrary",), name=name,
                  comm=comm)(*args)


def _merge_core(zg0, zg1, zg2, y0, y1, y2):
    return jax.nn.sigmoid(zg0) * y0 + jax.nn.sigmoid(zg1) * y1 + jax.nn.sigmoid(zg2) * y2


def _merge_fwd(z, y_gm, y_mla, y_mem, name):
    n = z.shape[0]
    t = min(ROW_TILE, n)
    w = D_MODEL

    def body(g0, g1, g2, y0, y1, y2, o_ref):
        o_ref[...] = _merge_core(g0[...].astype(F32), g1[...].astype(F32), g2[...].astype(F32), y0[...].astype(F32), y1[...].astype(F32),
                                 y2[...].astype(F32)).astype(BF)

    return pl.pallas_call(body, grid=(n // t,),
                          in_specs=[_rows(t, w, 0), _rows(t, w, 1), _rows(t, w, 2)] + [_rows(t, w)] * 3,
                          out_specs=_rows(t, w), out_shape=SDS((n, w), BF),
                          compiler_params=_params(("parallel",)), name=name)(z, z, z, y_gm, y_mla, y_mem)


def _merge_bwd(z, y_gm, y_mla, y_mem, dmerged, name):
    n = z.shape[0]
    t = min(ROW_TILE, n)
    w = D_MODEL

    def body(g0, g1, g2, y0, y1, y2, dm, dzg_ref, d0_ref, d1_ref, d2_ref):
        _, vjp = jax.vjp(_merge_core, g0[...].astype(F32), g1[...].astype(F32), g2[...].astype(F32), y0[...].astype(F32), y1[...].astype(F32),
                         y2[...].astype(F32))
        dg0, dg1, dg2, dy0, dy1, dy2 = vjp(dm[...])
        dzg_ref[:, 0:w] = dg0.astype(BF)
        dzg_ref[:, w:2 * w] = dg1.astype(BF)
        dzg_ref[:, 2 * w:3 * w] = dg2.astype(BF)
        d0_ref[...] = dy0.astype(BF)
        d1_ref[...] = dy1.astype(BF)
        d2_ref[...] = dy2.astype(BF)

    return pl.pallas_call(body, grid=(n // t,),
                          in_specs=[_rows(t, w, 0), _rows(t, w, 1), _rows(t, w, 2)] + [_rows(t, w)] * 4,
                          out_specs=[_rows(t, 3 * w, ZG // (3 * w))] + [_rows(t, w)] * 3,
                          out_shape=[SDS((n, Z_COLS), BF)] + [SDS((n, w), BF)] * 3,
                          compiler_params=_params(("parallel",)), name=name)(z, z, z, y_gm, y_mla, y_mem, dmerged)


def _gm_core(zu, zv, g_ln, b_ln, ws, bcols):
    t = zu.shape[0]
    u = jax.nn.gelu(zu)
    v = _layernorm(jax.nn.gelu(zv), g_ln, b_ln)
    row = lax.broadcasted_iota(jnp.int32, (GM_CHUNK, GM_CHUNK), 0)
    col = lax.broadcasted_iota(jnp.int32, (GM_CHUNK, GM_CHUNK), 1)
    wc = [jnp.where(row >= col, ws[g], 0.0) for g in range(GM_GROUPS)]
    chunks = []
    for c in range(t // GM_CHUNK):
        cols = []
        for g in range(GM_GROUPS):
            vc = v[c * GM_CHUNK:(c + 1) * GM_CHUNK, g * LANES:(g + 1) * LANES]
            cols.append(_mm_nn(wc[g], vc) + bcols[g])
        chunks.append(jnp.concatenate(cols, axis=1))
    mixed = chunks[0] if len(chunks) == 1 else jnp.concatenate(chunks, axis=0)
    return u * mixed


def _gm_specs(t):
    return [_rows(t, GM_WIDTH, ZU // GM_WIDTH), _rows(t, GM_WIDTH, ZV // GM_WIDTH), _full((1, GM_WIDTH)),
            _full((1, GM_WIDTH)), _full((GM_GROUPS, GM_CHUNK, GM_CHUNK))] + [_full((GM_CHUNK, 1))] * GM_GROUPS


def _gm_fwd(z, g_ln, b_ln, ws, bcols, name):
    n = z.shape[0]
    t = min(ROW_TILE, n)

    def body(zu, zv, g_ref, b_ref, ws_ref, c0, c1, c2, c3, o_ref):
        out = _gm_core(zu[...].astype(F32), zv[...].astype(F32), g_ref[...], b_ref[...], [ws_ref[g] for g in range(GM_GROUPS)],
                       [c0[...], c1[...], c2[...], c3[...]])
        o_ref[...] = out.astype(BF)

    return pl.pallas_call(body, grid=(n // t,), in_specs=_gm_specs(t), out_specs=_rows(t, GM_WIDTH),
                          out_shape=SDS((n, GM_WIDTH), BF), compiler_params=_params(("parallel",)),
                          name=name)(z, z, g_ln, b_ln, ws, *bcols)


def _gm_bwd(z, g_ln, b_ln, ws, bcols, dgm, dz, name, comm=None):
    n = z.shape[0]
    t = min(ROW_TILE, n)

    def body(zu, zv, g_ref, b_ref, ws_ref, c0, c1, c2, c3, dgm_ref, _, dz_ref, dg_ref, db_ref, dws_ref, e0, e1, e2,
             e3):
        first = pl.program_id(0) == 0
        _, vjp = jax.vjp(_gm_core, zu[...].astype(F32), zv[...].astype(F32), g_ref[...], b_ref[...],
                         [ws_ref[g] for g in range(GM_GROUPS)], [c0[...], c1[...], c2[...], c3[...]])
        dzu, dzv, dg, db, dws, dcols = vjp(dgm_ref[...])
        dz_ref[:, 0:GM_WIDTH] = dzu.astype(BF)
        dz_ref[:, GM_WIDTH:2 * GM_WIDTH] = dzv.astype(BF)
        _acc(dg_ref, dg, first)
        _acc(db_ref, db, first)
        _acc(dws_ref, jnp.stack(dws, axis=0), first)
        for ref, val in zip((e0, e1, e2, e3), dcols):
            _acc(ref, val, first)

    in_specs = _gm_specs(t) + [_rows(t, GM_WIDTH), ANY]
    return _pcall(
        body, grid=(n // t,), in_specs=in_specs,
        out_specs=[_rows(t, 2 * GM_WIDTH, ZU // (2 * GM_WIDTH)), _full((1, GM_WIDTH)), _full((1, GM_WIDTH)),
                   _full((GM_GROUPS, GM_CHUNK, GM_CHUNK))] + [_full((GM_CHUNK, 1))] * GM_GROUPS,
        out_shape=[SDS((n, Z_COLS), BF), SDS((1, GM_WIDTH), F32), SDS((1, GM_WIDTH), F32),
                   SDS((GM_GROUPS, GM_CHUNK, GM_CHUNK), F32)] + [SDS((GM_CHUNK, 1), F32)] * GM_GROUPS,
        sem=("arbitrary",), name=name, comm=comm, aliases={len(in_specs) - 1: 0})(z, z, g_ln, b_ln, ws, *bcols, dgm, dz)


def _rope_tables(pos_f, inv_full, cmask, smask, name, comm=None):
    n = pos_f.shape[0]
    t = min(ROW_TILE, n)

    def body(p_ref, inv_ref, cm_ref, sm_ref, cos_ref, sin_ref):
        ang = p_ref[...] * inv_ref[...]
        cos_ref[...] = jnp.cos(ang) * cm_ref[...]
        sin_ref[...] = jnp.sin(ang) * sm_ref[...]

    return _pcall(body, grid=(n // t,), in_specs=[_rows(t, 1)] + [_full((1, LANES))] * 3,
                  out_specs=[_rows(t, LANES)] * 2, out_shape=[SDS((n, LANES), F32)] * 2, sem=("parallel",),
                  name=name, comm=comm)(pos_f, inv_full, cmask, smask)


def _prep_norms(cq, ckv, g_cq, g_ckv):
    return _rmsn(cq, g_cq, Q_LORA), _rmsn(ckv, g_ckv, KV_LORA)


def _prep_heads(qa, kva, kpe, head_gains, cos_f, sin_s):
    g_qn, g_qp, g_kn, g_kp = head_gains
    qs = _split_lanes(qa)
    kvs = _split_lanes(kva)
    kp = _rope(_rmsn(kpe, g_kp, MLA_ROPE), cos_f, sin_s)
    q_out, k_out = [], []
    for h in range(MLA_HEADS):
        q_out.append(_rmsn(qs[h], g_qn, MLA_NOPE))
        q_out.append(_rope(_rmsn(qs[MLA_HEADS + h], g_qp, MLA_ROPE), cos_f, sin_s))
        k_out.append(_rmsn(kvs[h], g_kn, MLA_NOPE))
        k_out.append(kp)
    return (jnp.concatenate(q_out, axis=1), jnp.concatenate(k_out, axis=1),
            jnp.concatenate(kvs[MLA_HEADS:], axis=1))


def _prep_in_specs(t):
    return ([_rows(t, Q_LORA, CQ // Q_LORA), _rows(t, LANES, KPE // LANES), _rows(t, KV_LORA, CKV // KV_LORA),
             _rows(t, LANES), _rows(t, LANES), _full((1, Q_LORA)), _full((1, KV_LORA))] + [_full((1, LANES))] * 4
            + [_full((Q_LORA, 2048)), _full((KV_LORA, 2048))])


def _prep_fwd(z, cos_f, sin_s, gains, wq, wkv, name):
    n = z.shape[0]
    t = min(ROW_TILE, n)

    def body(cq, kpe, ckv, cos_ref, sin_ref, g_cq, g_ckv, g_qn, g_qp, g_kn, g_kp, wq_ref, wkv_ref, q_ref, k_ref, v_ref):
        cqn, ckvn = _prep_norms(cq[...].astype(F32), ckv[...].astype(F32), g_cq[...], g_ckv[...])
        qa = _dn(cqn, wq_ref[...], 1, 0)
        kva = _dn(ckvn, wkv_ref[...], 1, 0)
        q, k, v = _prep_heads(qa, kva, kpe[...].astype(F32), (g_qn[...], g_qp[...], g_kn[...], g_kp[...]), cos_ref[...],
                              sin_ref[...])
        q_ref[...] = q.astype(BF)
        k_ref[...] = k.astype(BF)
        v_ref[...] = v.astype(BF)

    return pl.pallas_call(body, grid=(n // t,), in_specs=_prep_in_specs(t),
                          out_specs=[_rows(t, 2048), _rows(t, 2048), _rows(t, 1024)],
                          out_shape=[SDS((n, 2048), BF), SDS((n, 2048), BF), SDS((n, 1024), BF)],
                          compiler_params=_params(("parallel",)),
                          name=name)(z, z, z, cos_f, sin_s, *gains, wq, wkv)


def _prep_bwd(z, cos_f, sin_s, gains, wq, wkv, dq, dk, dv, dz, name, comm=None):
    n = z.shape[0]
    t = min(ROW_TILE, n)
    wz = Q_LORA + LANES + KV_LORA

    def body(cq, kpe, ckv, cos_ref, sin_ref, g_cq, g_ckv, g_qn, g_qp, g_kn, g_kp, wq_ref, wkv_ref, dq_ref, dk_ref,
             dv_ref, _, dz_ref, o_cq, o_ckv, o_qn, o_qp, o_kn, o_kp, dwq_ref, dwkv_ref):
        first = pl.program_id(0) == 0
        cos_t, sin_t = cos_ref[...], sin_ref[...]
        (cqn, ckvn), vjp_norms = jax.vjp(_prep_norms, cq[...].astype(F32), ckv[...].astype(F32), g_cq[...], g_ckv[...])
        wq_t, wkv_t = wq_ref[...], wkv_ref[...]
        qa = _dn(cqn, wq_t, 1, 0)
        kva = _dn(ckvn, wkv_t, 1, 0)
        _, vjp_heads = jax.vjp(lambda a, b, c, g: _prep_heads(a, b, c, g, cos_t, sin_t), qa, kva, kpe[...].astype(F32),
                               (g_qn[...], g_qp[...], g_kn[...], g_kp[...]))
        dqa, dkva, dkpe, dhead = vjp_heads((dq_ref[...], dk_ref[...], dv_ref[...]))
        _acc(dwq_ref, _dn(cqn, dqa, 0, 0), first)
        _acc(dwkv_ref, _dn(ckvn, dkva, 0, 0), first)
        dcq, dckv, dg_cq, dg_ckv = vjp_norms((_dn(dqa, wq_t, 1, 1), _dn(dkva, wkv_t, 1, 1)))
        dz_ref[:, 0:Q_LORA] = dcq.astype(BF)
        dz_ref[:, Q_LORA:Q_LORA + LANES] = dkpe.astype(BF)
        dz_ref[:, Q_LORA + LANES:wz] = dckv.astype(BF)
        for ref, val in zip((o_cq, o_ckv, o_qn, o_qp, o_kn, o_kp), (dg_cq, dg_ckv) + tuple(dhead)):
            _acc(ref, val, first)

    gain_specs = [_full((1, Q_LORA)), _full((1, KV_LORA))] + [_full((1, LANES))] * 4
    gain_shapes = [SDS((1, Q_LORA), F32), SDS((1, KV_LORA), F32)] + [SDS((1, LANES), F32)] * 4
    in_specs = _prep_in_specs(t) + [_rows(t, 2048), _rows(t, 2048), _rows(t, 1024), ANY]
    return _pcall(
        body, grid=(n // t,), in_specs=in_specs,
        out_specs=[_rows(t, wz, CQ // wz)] + gain_specs + [_full((Q_LORA, 2048)), _full((KV_LORA, 2048))],
        out_shape=[SDS((n, Z_COLS), BF)] + gain_shapes + [SDS((Q_LORA, 2048), F32), SDS((KV_LORA, 2048), F32)],
        sem=("arbitrary",), name=name, comm=comm,
        aliases={len(in_specs) - 1: 0})(z, z, z, cos_f, sin_s, *gains, wq, wkv, dq, dk, dv, dz)


MLA_QK = 256
MLA_SCALE = 1.0 / math.sqrt(MLA_NOPE + MLA_ROPE)
LOG2E = 1.0 / math.log(2.0)
MLA_SCALE_LOG2E = MLA_SCALE * LOG2E


def _causal_mask(s, q0, k0):
    tq, tk = s.shape
    row = q0 + lax.broadcasted_iota(jnp.int32, (tq, tk), 0)
    col = k0 + lax.broadcasted_iota(jnp.int32, (tq, tk), 1)
    return jnp.where(row >= col, s, -jnp.inf)


def _mla_fwd(q, k, v, batch, seq, name, comm=None):
    n = q.shape[0]
    tq = min(ATT_TILE, seq)
    nq = seq // tq

    nh = ATT_HEADS_FWD

    def body(q_ref, k_ref, v_ref, o_ref, lse_ref):
        i = pl.program_id(2)

        def step(j, carry, diagonal=False):
            k0 = pl.multiple_of(j * tq, tq)
            out = []
            ones = jnp.ones((tq, LANES), BF)
            for hh in range(nh):
                m, acc = carry[hh]
                qb = q_ref[:, hh * MLA_QK:(hh + 1) * MLA_QK]
                kb = k_ref[pl.ds(k0, tq), hh * MLA_QK:(hh + 1) * MLA_QK]
                vb = v_ref[pl.ds(k0, tq), hh * MLA_V:(hh + 1) * MLA_V]
                s = _dn(qb, kb, 1, 1)
                if diagonal:
                    s = _causal_mask(s, i * tq, k0)
                m_new = jnp.maximum(m, jnp.max(s, axis=-1, keepdims=True))
                p = jnp.exp2((s - m_new) * MLA_SCALE_LOG2E)
                alpha = jnp.exp2((m - m_new) * MLA_SCALE_LOG2E)
                acc = alpha * acc + _dn(p, jnp.concatenate([vb, ones], axis=1), 1, 0)
                out.append((m_new, acc))
            return tuple(out)

        init = tuple((jnp.full((tq, 1), -jnp.inf, F32), jnp.zeros((tq, MLA_V + LANES), F32)) for _ in range(nh))
        final = step(i, lax.fori_loop(0, i, step, init), diagonal=True)
        for hh, (m, acc) in enumerate(final):
            l = acc[:, MLA_V:MLA_V + 1]
            o_ref[:, hh * MLA_V:(hh + 1) * MLA_V] = acc[:, :MLA_V] / l
            lse_ref[:, hh * LANES:(hh + 1) * LANES] = jnp.broadcast_to(m * MLA_SCALE + jnp.log(l), (tq, LANES))

    return _pcall(
        body, grid=(batch, MLA_HEADS // nh, nq),
        in_specs=[pl.BlockSpec((tq, nh * MLA_QK), lambda b, h, i: (b * nq + i, h)),
                  pl.BlockSpec((seq, nh * MLA_QK), lambda b, h, i: (b, h)),
                  pl.BlockSpec((seq, nh * MLA_V), lambda b, h, i: (b, h))],
        out_specs=[pl.BlockSpec((tq, nh * MLA_V), lambda b, h, i: (b * nq + i, h)),
                   pl.BlockSpec((tq, nh * LANES), lambda b, h, i: (b * nq + i, h))],
        out_shape=[SDS((n, MLA_HEADS * MLA_V), F32), SDS((n, MLA_HEADS * LANES), F32)],
        sem=("parallel", "parallel", "arbitrary"), name=name, comm=comm)(q, k, v)


def _mla_bwd(q, k, v, o, lse, do, batch, seq, name, comm=None):
    n = q.shape[0]
    tk = min(ATT_TILE, seq)
    nk = seq // tk

    nh = ATT_HEADS

    def body(q_ref, k_ref, v_ref, o_ref, lse_ref, do_ref, dq_ref, dk_ref, dv_ref):
        jk = pl.program_id(2)

        @pl.when(jk == 0)
        def _():
            dq_ref[...] = jnp.zeros_like(dq_ref)

        def step(i, carry, diagonal=False):
            q0 = pl.multiple_of(i * tk, tk)
            rows = pl.ds(q0, tk)
            out = []
            for hh in range(nh):
                dk_acc, dv_acc = carry[hh]
                qk_cols = slice(hh * MLA_QK, (hh + 1) * MLA_QK)
                v_cols = slice(hh * MLA_V, (hh + 1) * MLA_V)
                kb = k_ref[:, qk_cols]
                vb = v_ref[:, v_cols]
                qb = q_ref[rows, qk_cols]
                dob = do_ref[rows, v_cols]
                delta = jnp.sum(dob * o_ref[rows, v_cols], axis=-1, keepdims=True)
                s = _dn(qb, kb, 1, 1)
                if diagonal:
                    s = _causal_mask(s, q0, jk * tk)
                p = jnp.exp2(s * MLA_SCALE_LOG2E - lse_ref[rows, hh * LANES:hh * LANES + 1] * LOG2E)
                dv_acc = dv_acc + _dn(p, dob, 0, 0)
                dp = _dn(dob, vb, 1, 1)
                ds = p * (dp - delta) * MLA_SCALE
                dk_acc = dk_acc + _dn(ds, qb, 0, 0)
                dq_ref[rows, qk_cols] += _dn(ds, kb, 1, 0)
                out.append((dk_acc, dv_acc))
            return tuple(out)

        init = tuple((jnp.zeros((tk, MLA_QK), F32), jnp.zeros((tk, MLA_V), F32)) for _ in range(nh))
        final = lax.fori_loop(jk + 1, nk, step, step(jk, init, diagonal=True))
        for hh, (dk_acc, dv_acc) in enumerate(final):
            dk_ref[:, hh * MLA_QK:(hh + 1) * MLA_QK] = dk_acc
            dv_ref[:, hh * MLA_V:(hh + 1) * MLA_V] = dv_acc

    full_qk = pl.BlockSpec((seq, nh * MLA_QK), lambda b, h, j: (b, h))
    full_v = pl.BlockSpec((seq, nh * MLA_V), lambda b, h, j: (b, h))
    blk_qk = pl.BlockSpec((tk, nh * MLA_QK), lambda b, h, j: (b * nk + j, h))
    blk_v = pl.BlockSpec((tk, nh * MLA_V), lambda b, h, j: (b * nk + j, h))
    return _pcall(
        body, grid=(batch, MLA_HEADS // nh, nk),
        in_specs=[full_qk, blk_qk, blk_v, full_v, full_v, full_v],
        out_specs=[full_qk, blk_qk, blk_v],
        out_shape=[SDS((n, MLA_HEADS * MLA_QK), F32), SDS((n, MLA_HEADS * MLA_QK), F32),
                   SDS((n, MLA_HEADS * MLA_V), F32)],
        sem=("parallel", "parallel", "arbitrary"), name=name, comm=comm)(q, k, v, o, lse, do)


MEM_SCALE = 1.0 / math.sqrt(HEAD_DIM)
MEM_W = MEM_HEADS * HEAD_DIM


def _mem_core(qs, ks, vs, g_mq, g_mk):
    outs = []
    for h in range(MEM_HEADS):
        qh = _rmsn(qs[h], g_mq, HEAD_DIM)
        kh = _rmsn(ks[h], g_mk, HEAD_DIM)
        p = _softmax(_mm_nt(qh, kh) * MEM_SCALE)
        outs.append(_mm_nn(p, vs[h]))
    return jnp.concatenate(outs, axis=1)


def _mem_load(qm, kvm, g_mq, g_mk):
    hs = range(MEM_HEADS)
    qs = [qm[:, h * LANES:(h + 1) * LANES].astype(F32) for h in hs]
    ks = [kvm[:, h * LANES:(h + 1) * LANES] for h in hs]
    vs = [kvm[:, MEM_W + h * LANES:MEM_W + (h + 1) * LANES] for h in hs]
    return qs, ks, vs, g_mq[...], g_mk[...]


def _mem_fwd(z, kvm, g_mq, g_mk, batch, seq, name, comm=None):
    n = z.shape[0]
    t = min(ROW_TILE, seq)
    per = seq // t

    def body(qm, kvm_ref, gq, gk, o_ref):
        o_ref[...] = _mem_core(*_mem_load(qm, kvm_ref, gq, gk)).astype(BF)

    return _pcall(
        body, grid=(n // t,),
        in_specs=[_rows(t, MEM_W, QM // MEM_W), pl.BlockSpec((MEM_LEN, 2 * MEM_W), lambda i: (i // per, 0)),
                  _full((1, LANES)), _full((1, LANES))],
        out_specs=_rows(t, MEM_W), out_shape=SDS((n, MEM_W), BF), sem=("parallel",), name=name,
        comm=comm)(z, kvm, g_mq, g_mk)


def _mem_bwd(z, kvm, g_mq, g_mk, dom, dz, batch, seq, name):
    n = z.shape[0]
    t = min(ROW_TILE, seq)
    per = seq // t

    def body(qm, kvm_ref, gq, gk, dom_ref, _, dz_ref, dkvm_ref, dgq_ref, dgk_ref):
        i = pl.program_id(0)
        _, vjp = jax.vjp(_mem_core, *_mem_load(qm, kvm_ref, gq, gk))
        dqs, dks, dvs, dgq, dgk = vjp(dom_ref[...])
        dz_ref[...] = jnp.concatenate(dqs, axis=1).astype(BF)
        _acc(dkvm_ref, jnp.concatenate(dks + dvs, axis=1), i % per == 0)
        _acc(dgq_ref, dgq, i == 0)
        _acc(dgk_ref, dgk, i == 0)

    kv_spec = pl.BlockSpec((MEM_LEN, 2 * MEM_W), lambda i: (i // per, 0))
    return pl.pallas_call(
        body, grid=(n // t,),
        in_specs=[_rows(t, MEM_W, QM // MEM_W), kv_spec, _full((1, LANES)), _full((1, LANES)), _rows(t, MEM_W), ANY],
        out_specs=[_rows(t, MEM_W, QM // MEM_W), kv_spec, _full((1, LANES)), _full((1, LANES))],
        out_shape=[SDS((n, Z_COLS), BF), SDS((batch * MEM_LEN, 2 * MEM_W), F32), SDS((1, LANES), F32),
                   SDS((1, LANES), F32)],
        input_output_aliases={5: 0},
        compiler_params=_params(("arbitrary",)), name=name)(z, kvm, g_mq, g_mk, dom, dz)


def _me():
    return lax.axis_index("x"), lax.axis_index("y"), lax.axis_index("c")


def _other_chips(x, y):
    return [(1 - x, y), (x, 1 - y), (1 - x, 1 - y)]


def _shard_shape(name):
    r, c = BIG_SHAPE[name]
    return (r, c // N_CHIPS) if name in COL_SHARDED else (r // N_CHIPS, c)


def _n_pieces(half_rows):
    return max(1, half_rows // PIECE_ROWS)


def _piece_plan(shapes):
    plan = []
    for r, _ in shapes:
        h = r // 2
        n = _n_pieces(h)
        plan.append((h, n, h // n))
    return plan


def _remote(send, recv, sem, src, dst, to):
    return pltpu.make_async_remote_copy(src_ref=src, dst_ref=dst, send_sem=send.at[sem], recv_sem=recv.at[sem],
                                        device_id=to, device_id_type=MESH)


def _gather_far(shards):
    plan = _piece_plan([s.shape for s in shards])
    n_far = 3 * sum(n for _, n, _ in plan)
    n_loc = 2 * sum(n for _, n, _ in plan)

    def copies(s_refs, o_refs, send, recv, local):
        x, y, c = _me()
        k = 2 * x + y
        mine, sends, arrivals = [], [], []
        for t, (h, n, pr) in enumerate(plan):
            s_ref, o_ref = s_refs[t], o_refs[t]
            for core in range(2):
                for p in range(n):
                    rows = pl.ds(core * h + p * pr, pr)
                    mine.append(pltpu.make_async_copy(s_ref.at[rows], o_ref.at[k, rows], local.at[len(mine)]))
            for chip in _other_chips(x, y):
                for p in range(n):
                    rows = pl.ds(c * h + p * pr, pr)
                    s = len(sends)
                    sends.append(_remote(send, recv, s, s_ref.at[rows], o_ref.at[k, rows], (*chip, c)))
                    arrivals.append(_remote(send, recv, s, s_ref.at[rows], o_ref.at[2 * chip[0] + chip[1], rows],
                                            (*chip, c)))
        return sends, arrivals, mine

    return _Phase(shards, [SDS((N_CHIPS,) + s.shape, s.dtype) for s in shards], n_far, n_loc, copies)


def _gather_near(bufs):
    plan = _piece_plan([b.shape[1:] for b in bufs])
    n_sem = 3 * sum(n for _, n, _ in plan)

    def copies(i_refs, o_refs, send, recv, local):
        x, y, c = _me()
        sib = (x, y, 1 - c)
        sends, arrivals = [], []
        for t, (h, n, pr) in enumerate(plan):
            for chip in _other_chips(x, y):
                ci = 2 * chip[0] + chip[1]
                for p in range(n):
                    rows = pl.ds(c * h + p * pr, pr)
                    rows_sib = pl.ds((1 - c) * h + p * pr, pr)
                    s = len(sends)
                    sends.append(_remote(send, recv, s, i_refs[t].at[ci, rows], o_refs[t].at[ci, rows], sib))
                    arrivals.append(_remote(send, recv, s, i_refs[t].at[ci, rows_sib], o_refs[t].at[ci, rows_sib], sib))
        return sends, arrivals, []

    return _Phase(bufs, [SDS(b.shape, b.dtype) for b in bufs], n_sem, 0, copies, {t: t for t in range(len(bufs))})


def _pair_exchange(grads):
    plan = _piece_plan([g.shape[1:] for g in grads])
    n_sem = sum(n for _, n, _ in plan)

    def copies(g_refs, o_refs, send, recv, local):
        x, y, c = _me()
        sends = []
        for t, (h, n, pr) in enumerate(plan):
            for p in range(n):
                sends.append(_remote(send, recv, len(sends), g_refs[t].at[:, pl.ds((1 - c) * h + p * pr, pr)],
                                     o_refs[t].at[:, pl.ds(p * pr, pr)], (x, y, 1 - c)))
        return sends, sends, []

    return _Phase(grads, [SDS((N_CHIPS, g.shape[1] // 2, g.shape[2]), F32) for g in grads], n_sem, 0, copies)


def _pair_add(ck, g, theirs, name):
    _, r, c = g.shape
    (h, n, pr), = _piece_plan([(r, c)])

    def body(ck_ref, g_ref, t_ref, pbf_ref):
        pbf_ref[...] = (g_ref[...] + t_ref[...]).astype(BF)

    half = pl.BlockSpec((None, pr, c), lambda k, p, ck: (k, p, 0))
    spec = pltpu.PrefetchScalarGridSpec(
        num_scalar_prefetch=1, grid=(N_CHIPS, n),
        in_specs=[pl.BlockSpec((None, pr, c), lambda k, p, ck: (k, ck[0] * n + p, 0)), half], out_specs=half)
    return pl.pallas_call(body, grid_spec=spec, out_shape=SDS((N_CHIPS, h, c), BF),
                          compiler_params=_params(("arbitrary", "arbitrary")), name=name)(ck, g, theirs)


def _scatter_partials(pbfs):
    plan = [(h, _n_pieces(h), h // _n_pieces(h)) for h in [p.shape[1] for p in pbfs]]
    n_sem = 3 * sum(n for _, n, _ in plan)

    def copies(p_refs, o_refs, send, recv, local):
        x, y, c = _me()
        sends = []
        for t, (h, n, pr) in enumerate(plan):
            for j, chip in enumerate(_other_chips(x, y)):
                for p in range(n):
                    rows = pl.ds(p * pr, pr)
                    sends.append(_remote(send, recv, len(sends), p_refs[t].at[2 * chip[0] + chip[1], rows],
                                         o_refs[t].at[j, rows], (*chip, c)))
        return sends, sends, []

    return _Phase(pbfs, [SDS((3,) + p.shape[1:], BF) for p in pbfs], n_sem, 0, copies)


def _sum_chips(ck, pbf, slots, name):
    _, h, c = pbf.shape
    n = _n_pieces(h)
    pr = h // n

    def body(ck_ref, p_ref, s_ref, o_ref):
        o_ref[...] = (((p_ref[...].astype(F32) + s_ref[0].astype(F32)) + s_ref[1].astype(F32))
                      + s_ref[2].astype(F32))

    spec = pltpu.PrefetchScalarGridSpec(
        num_scalar_prefetch=1, grid=(n,),
        in_specs=[pl.BlockSpec((None, pr, c), lambda p, ck: (ck[1], p, 0)),
                  pl.BlockSpec((3, pr, c), lambda p, ck: (0, p, 0))],
        out_specs=pl.BlockSpec((pr, c), lambda p, ck: (ck[0] * n + p, 0)))
    return pl.pallas_call(body, grid_spec=spec, out_shape=SDS((2 * h, c), F32),
                          compiler_params=_params(("arbitrary",)), name=name)(ck, pbf, slots)


def _join_halves(sums):
    plan = _piece_plan([s.shape for s in sums])
    n_sem = sum(n for _, n, _ in plan)

    def copies(r_refs, o_refs, send, recv, local):
        x, y, c = _me()
        sends, arrivals = [], []
        for t, (h, n, pr) in enumerate(plan):
            for p in range(n):
                rows = pl.ds(c * h + p * pr, pr)
                rows_sib = pl.ds((1 - c) * h + p * pr, pr)
                s = len(sends)
                sends.append(_remote(send, recv, s, r_refs[t].at[rows], o_refs[t].at[rows], (x, y, 1 - c)))
                arrivals.append(_remote(send, recv, s, r_refs[t].at[rows_sib], o_refs[t].at[rows_sib], (x, y, 1 - c)))
        return sends, arrivals, []

    return _Phase(sums, [SDS(s.shape, F32) for s in sums], n_sem, 0, copies, {t: t for t in range(len(sums))})


def _gather_small(s, name):
    def body(s_ref, o_ref, send, recv, local):
        x, y, c = _me()
        me = 4 * x + 2 * y + c
        keep = pltpu.make_async_copy(s_ref, o_ref.at[me], local)
        keep.start()
        sends = []
        for r in range(1, 8):
            fx, fy, fc = (r >> 2) & 1, (r >> 1) & 1, r & 1
            to = (x ^ fx, y ^ fy, c ^ fc)
            sends.append(pltpu.make_async_remote_copy(
                src_ref=s_ref, dst_ref=o_ref.at[me], send_sem=send.at[r - 1], recv_sem=recv.at[r - 1],
                device_id=to, device_id_type=MESH))
        for cp in sends:
            cp.start()
        for r in range(1, 8):
            fx, fy, fc = (r >> 2) & 1, (r >> 1) & 1, r & 1
            src = 4 * (x ^ fx) + 2 * (y ^ fy) + (c ^ fc)
            pltpu.make_async_remote_copy(
                src_ref=s_ref, dst_ref=o_ref.at[src], send_sem=send.at[r - 1], recv_sem=recv.at[r - 1],
                device_id=(x ^ fx, y ^ fy, c ^ fc), device_id_type=MESH).wait_recv()
        for cp in sends:
            cp.wait_send()
        keep.wait()

    return pl.pallas_call(
        body, in_specs=[ANY], out_specs=ANY, out_shape=SDS((8, SMALL_ROWS, LANES), F32),
        scratch_shapes=[pltpu.SemaphoreType.DMA((7,)), pltpu.SemaphoreType.DMA((7,)), pltpu.SemaphoreType.DMA],
        name=name)(s)


def _adam_math(w, g, m, v):
    nm = ADAM_B1 * m + (1.0 - ADAM_B1) * g
    nv = ADAM_B2 * v + (1.0 - ADAM_B2) * (g * g)
    m_hat = nm / (1.0 - ADAM_B1 ** ADAM_STEP)
    v_hat = nv / (1.0 - ADAM_B2 ** ADAM_STEP)
    return -ADAM_LR * (m_hat / (jnp.sqrt(v_hat) + ADAM_EPS) + ADAM_WD * w), nm, nv


def _adamw(w, g, m, v, name):
    _, r, c = w.shape
    t = max(d for d in range(8, r + 1, 8) if r % d == 0 and 16 * d * c * 4 <= VMEM_LIMIT - (8 << 20))

    def body(w_ref, g_ref, m_ref, v_ref, go_ref, d_ref, nm_ref, nv_ref):
        g_ = g_ref[...]
        d, nm, nv = _adam_math(w_ref[...], g_, m_ref[...], v_ref[...])
        go_ref[...] = g_
        d_ref[...] = d
        nm_ref[...] = nm
        nv_ref[...] = nv

    lead = pl.BlockSpec((None, t, c), lambda i: (0, i, 0))
    return pl.pallas_call(body, grid=(r // t,), in_specs=[lead, _rows(t, c), lead, lead], out_specs=[lead] * 4,
                          out_shape=[SDS((1, r, c), F32)] * 4, compiler_params=_params(("parallel",)),
                          name=name)(w, g, m, v)


def _small_layout():
    out, r0 = {}, 0
    for n in SMALL:
        size = int(np.prod(SMALL_SHAPE[n]))
        nr = -(-size // LANES)
        out[n] = (r0, nr)
        r0 += nr
    assert r0 <= SMALL_ROWS
    return out, r0


def _pack_small(grads, loss_tile, name):
    layout, used = _small_layout()

    def body(*refs):
        o_ref = refs[-1]
        o_ref[used:used + 1, :] = refs[-2][0:1, :]
        for n, ref in zip(SMALL, refs[:-2]):
            r0, nr = layout[n]
            if n == "w_spatial":
                for g in range(GM_GROUPS):
                    o_ref[r0 + g * GM_CHUNK:r0 + (g + 1) * GM_CHUNK, :] = ref[g]
            elif n == "b_spatial":
                o_ref[r0:r0 + nr, :] = ref[...]
            else:
                for i in range(nr):
                    o_ref[r0 + i:r0 + i + 1, :] = ref[:, i * LANES:(i + 1) * LANES]
        if used + 1 < SMALL_ROWS:
            o_ref[used + 1:SMALL_ROWS, :] = jnp.zeros((SMALL_ROWS - used - 1, LANES), F32)

    return pl.pallas_call(body, out_shape=SDS((SMALL_ROWS, LANES), F32), name=name)(*grads, loss_tile)


def _adamw_small(gathered, ws, ms, vs, name):
    layout, used = _small_layout()
    n_t = len(SMALL)

    def body(*refs):
        g_ref = refs[0]
        w_refs, m_refs, v_refs = refs[1:1 + n_t], refs[1 + n_t:1 + 2 * n_t], refs[1 + 2 * n_t:1 + 3 * n_t]
        outs = refs[1 + 3 * n_t:1 + 7 * n_t]
        acc = refs[-1]
        total = g_ref[0]
        for j in range(1, 8):
            total = total + g_ref[j]
        acc[...] = total
        refs[1 + 7 * n_t][...] = acc[used:used + 1, :]
        for t, n in enumerate(SMALL):
            r0, nr = layout[n]
            o_refs = [outs[t], outs[n_t + t], outs[2 * n_t + t], outs[3 * n_t + t]]
            if n == "w_spatial":
                views = [((0, g), slice(r0 + g * GM_CHUNK, r0 + (g + 1) * GM_CHUNK), slice(None))
                         for g in range(GM_GROUPS)]
            elif n == "b_spatial":
                views = [((0,), slice(r0, r0 + nr), slice(None))]
            else:
                width = SMALL_SHAPE[n][1]
                views = [((slice(None), slice(i * LANES, min((i + 1) * LANES, width))), slice(r0 + i, r0 + i + 1),
                          slice(0, min(LANES, width - i * LANES))) for i in range(nr)]
            for idx, rows, lanes in views:
                g = acc[rows, lanes]
                d, nm, nv = _adam_math(w_refs[t][idx], g, m_refs[t][idx], v_refs[t][idx])
                for ref, val in zip(o_refs, (g, d, nm, nv)):
                    ref[idx] = val

    shapes = [SDS(SMALL_SHAPE[n], F32) for n in SMALL]
    return pl.pallas_call(body, out_shape=shapes * 4 + [SDS((1, LANES), F32)],
                          scratch_shapes=[pltpu.VMEM((SMALL_ROWS, LANES), F32)], name=name)(gathered, *ws, *ms, *vs)


def _win_layout(w_in):
    pad = jnp.zeros((w_in.shape[0], LANES - MLA_ROPE), w_in.dtype)
    u, v, cq = w_in[:, 0:512], w_in[:, 512:1024], w_in[:, 1024:1408]
    ckv, kpe, qm, zg = w_in[:, 1408:1664], w_in[:, 1664:1728], w_in[:, 1728:2240], w_in[:, 2240:5312]
    return jnp.concatenate([zg, u, v, qm, cq, kpe, pad, ckv], axis=1)


def _win_unlayout(g):
    zg, u, v, qm = g[:, ZG:ZG + 3072], g[:, ZU:ZU + 512], g[:, ZV:ZV + 512], g[:, QM:QM + 512]
    cq, kpe, ckv = g[:, CQ:CQ + 384], g[:, KPE:KPE + MLA_ROPE], g[:, CKV:CKV + 256]
    return jnp.concatenate([u, v, cq, ckv, kpe, qm, zg], axis=1)


def _wq_layout(w_uq):
    w = w_uq.reshape(Q_LORA, MLA_HEADS, MLA_NOPE + MLA_ROPE)
    nope = w[:, :, :MLA_NOPE].reshape(Q_LORA, MLA_HEADS * MLA_NOPE)
    pe = jnp.pad(w[:, :, MLA_NOPE:], ((0, 0), (0, 0), (0, LANES - MLA_ROPE))).reshape(Q_LORA, MLA_HEADS * LANES)
    return jnp.concatenate([nope, pe], axis=1)


def _wq_unlayout(g):
    nope = g[:, :1024].reshape(Q_LORA, MLA_HEADS, MLA_NOPE)
    pe = g[:, 1024:].reshape(Q_LORA, MLA_HEADS, LANES)[:, :, :MLA_ROPE]
    return jnp.concatenate([nope, pe], axis=2).reshape(Q_LORA, MLA_HEADS * (MLA_NOPE + MLA_ROPE))


def _wkv_layout(w_ukv):
    w = w_ukv.reshape(KV_LORA, MLA_HEADS, MLA_NOPE + MLA_V)
    return jnp.concatenate([w[:, :, :MLA_NOPE].reshape(KV_LORA, 1024), w[:, :, MLA_NOPE:].reshape(KV_LORA, 1024)],
                           axis=1)


def _wkv_unlayout(g):
    kn = g[:, :1024].reshape(KV_LORA, MLA_HEADS, MLA_NOPE)
    v = g[:, 1024:].reshape(KV_LORA, MLA_HEADS, MLA_V)
    return jnp.concatenate([kn, v], axis=2).reshape(KV_LORA, MLA_HEADS * (MLA_NOPE + MLA_V))


def _owner_major(g, name):
    r, c = _shard_shape(name)
    return g.reshape(r, N_CHIPS, c).transpose(1, 0, 2) if name in COL_SHARDED else g.reshape(N_CHIPS, r, c)


def _pad_lanes(g):
    return jnp.pad(g, ((0, 0), (0, LANES - g.shape[1])))


def kernel(x, mem, positions, g_mix, w_in, g_cq, w_uq, g_ckv, w_ukv, g_q_nope, g_q_pe, g_k_nope, g_k_pe, g_gm_ln, b_gm_ln, w_spatial, b_spatial, g_mem, w_mem_kv, g_mq, g_mk, w_o_gm, w_o_mla, w_o_mem, w_out, g_ffn, w_ff1, w_ff2, loss_target, m_g_mix, m_w_in, m_g_cq, m_w_uq, m_g_ckv, m_w_ukv, m_g_q_nope, m_g_q_pe, m_g_k_nope, m_g_k_pe, m_g_gm_ln, m_b_gm_ln, m_w_spatial, m_b_spatial, m_g_mem, m_w_mem_kv, m_g_mq, m_g_mk, m_w_o_gm, m_w_o_mla, m_w_o_mem, m_w_out, m_g_ffn, m_w_ff1, m_w_ff2, v_g_mix, v_w_in, v_g_cq, v_w_uq, v_g_ckv, v_w_ukv, v_g_q_nope, v_g_q_pe, v_g_k_nope, v_g_k_pe, v_g_gm_ln, v_b_gm_ln, v_w_spatial, v_b_spatial, v_g_mem, v_w_mem_kv, v_g_mq, v_g_mk, v_w_o_gm, v_w_o_mla, v_w_o_mem, v_w_out, v_g_ffn, v_w_ff1, v_w_ff2):
    given = dict(locals())
    wts = {n: given[n] for n in WEIGHTS}
    mom = {n: given["m_" + n] for n in WEIGHTS}
    var = {n: given["v_" + n] for n in WEIGHTS}
    batch, seq, _ = x.shape
    n_tok = batch * seq

    def natural(n, g):
        r, c = _shard_shape(n)
        return g.transpose(1, 0, 2).reshape(r, N_CHIPS * c) if n in COL_SHARDED else g.reshape(N_CHIPS * r, c)

    def far(names):
        return _gather_far([wts[n][0].astype(BF) for n in names])

    x2 = x.reshape(n_tok, D_MODEL)
    tgt2 = loss_target.reshape(n_tok, D_MODEL)
    mem2 = mem.reshape(batch * MEM_LEN, D_MODEL)
    pos_f = positions.reshape(n_tok, 1).astype(F32)

    inv = ROPE_BASE ** (-jnp.arange(0, MLA_ROPE, 2, dtype=F32) / MLA_ROPE)
    zeros64 = jnp.zeros((LANES - MLA_ROPE,), F32)
    inv_full = jnp.concatenate([inv, inv, zeros64]).reshape(1, LANES)
    half = MLA_ROPE // 2
    cmask = jnp.concatenate([jnp.ones((MLA_ROPE,), F32), zeros64]).reshape(1, LANES)
    smask = jnp.concatenate([-jnp.ones((half,), F32), jnp.ones((half,), F32), zeros64]).reshape(1, LANES)

    prep_gains = [g_cq, g_ckv, g_q_nope, _pad_lanes(g_q_pe), g_k_nope, _pad_lanes(g_k_pe)]
    ws = w_spatial[0]
    bcols = [b_spatial[0, g].reshape(GM_CHUNK, 1) for g in range(GM_GROUPS)]

    h1, in_far = _rms_fwd(x2, g_mix, "rms_mix", comm=far(EARLY[:1]))
    (cos_f, sin_s), early = _rope_tables(pos_f, inv_full, cmask, smask, "rope_tables",
                                         comm=_together(_gather_near(in_far), far(EARLY[1:])))
    memn, rest = _rms_fwd(mem2, g_mem, "rms_mem", comm=_gather_near(early[1:]))
    full = {n: natural(n, g) for n, g in zip(EARLY, list(early[:1]) + list(rest))}
    win = _win_layout(full["w_in"])
    wq = _wq_layout(full["w_uq"])
    wkv = _wkv_layout(full["w_ukv"])
    z, proj_far = _mm(h1, win, out_dtypes=(BF,), name="mm_in", comm=far(LATE_PROJ))
    gm = _gm_fwd(z, g_gm_ln, b_gm_ln, ws, bcols, "gm_fwd")
    qc, kc, vc = _prep_fwd(z, cos_f, sin_s, prep_gains, wq, wkv, "prep_fwd")
    (o_mla, lse), ff_far = _mla_fwd(qc, kc, vc, batch, seq, "mla_fwd", comm=far(LATE_FF))
    kvm, proj = _mm(memn, full["w_mem_kv"], name="mm_memkv", comm=_gather_near(proj_far))
    o_mem, ff = _mem_fwd(z, kvm, g_mq, g_mk, batch, seq, "mem_fwd", comm=_gather_near(ff_far))
    full.update({n: natural(n, g) for n, g in zip(LATE_PROJ + LATE_FF, list(proj) + list(ff))})
    y_gm = _mm(gm, full["w_o_gm"], out_dtypes=(BF,), name="mm_o_gm")
    y_mla = _mm(o_mla, full["w_o_mla"], out_dtypes=(BF,), name="mm_o_mla")
    y_mem = _mm(o_mem, full["w_o_mem"], out_dtypes=(BF,), name="mm_o_mem")
    merged = _merge_fwd(z, y_gm, y_mla, y_mem, "merge_fwd")
    x1, h2 = _mm(merged, full["w_out"], ins=(x2,), row_ins=(g_ffn,), epilogue=_residual_rms, out_dtypes=(F32, BF),
                 name="mm_out")
    a_ff, r_ff = _mm(h2, full["w_ff1"], epilogue=_relu2, out_dtypes=(BF, BF), name="mm_ff1")
    dy, dyb, loss_tile = _mm(r_ff, full["w_ff2"], ins=(x1, tgt2), epilogue=_loss_tail, out_dtypes=(F32, BF),
                             total=True, name="mm_ff2")

    gw = {}
    da = _mm(dyb, full["w_ff2"], tb=True, ins=(a_ff,), epilogue=_relu2_bwd, out_dtypes=(BF,), name="mm_d_a")
    gw["w_ff2"] = _owner_major(_mm(r_ff, dyb, ta=True, name="mm_dw_ff2"), "w_ff2")
    gw["w_ff1"] = _mm(h2, da, ta=True, owner_cols=D_FF // N_CHIPS, name="mm_dw_ff1")
    dx1, dx1b, dg_ffn = _mm(da, full["w_ff1"], tb=True, ins=(x1, dy), row_ins=(g_ffn,), epilogue=_rms_bwd_tail,
                            out_dtypes=(F32, BF), total=(1, D_MODEL), name="mm_d_h2")
    dmerged = _mm(dx1b, full["w_out"], tb=True, name="mm_d_merged")
    gw["w_out"] = _owner_major(_mm(merged, dx1b, ta=True, name="mm_dw_out"), "w_out")
    dz, dy_gm, dy_mla, dy_mem = _merge_bwd(z, y_gm, y_mla, y_mem, dmerged, "merge_bwd")
    dgm = _mm(dy_gm, full["w_o_gm"], tb=True, name="mm_d_gm")
    gw["w_o_gm"] = _owner_major(_mm(gm, dy_gm, ta=True, name="mm_dw_o_gm"), "w_o_gm")
    do_mla = _mm(dy_mla, full["w_o_mla"], tb=True, name="mm_d_omla")
    gw["w_o_mla"] = _owner_major(_mm(o_mla, dy_mla, ta=True, name="mm_dw_o_mla"), "w_o_mla")
    do_mem = _mm(dy_mem, full["w_o_mem"], tb=True, name="mm_d_omem")
    gw["w_o_mem"] = _owner_major(_mm(o_mem, dy_mem, ta=True, name="mm_dw_o_mem"), "w_o_mem")
    ck = jnp.stack([lax.axis_index("c"), 2 * lax.axis_index("x") + lax.axis_index("y")]).astype(jnp.int32)

    def pair_sums(names, theirs):
        return [_pair_add(ck, gw[n], t, "pair_add_" + n) for n, t in zip(names, theirs)]

    def chip_sums(names, pairs, slots):
        return [_sum_chips(ck, p, s, "sum_chips_" + n) for n, p, s in zip(names, pairs, slots)]

    (dz, dg_ln, db_ln, dws, *dbcols), theirs = _gm_bwd(z, g_gm_ln, b_gm_ln, ws, bcols, dgm, dz, "gm_bwd",
                                                      comm=_pair_exchange([gw[n] for n in LATE]))
    pairs = pair_sums(LATE, theirs)
    (dq, dk, dv), slots = _mla_bwd(qc, kc, vc, o_mla, lse, do_mla, batch, seq, "mla_bwd",
                                   comm=_scatter_partials(pairs))
    sums = chip_sums(LATE, pairs, slots)
    (dz, dg_cq, dg_ckv, dg_qn, dg_qp, dg_kn, dg_kp, dwq, dwkv), reduced_late = _prep_bwd(
        z, cos_f, sin_s, prep_gains, wq, wkv, dq, dk, dv, dz, "prep_bwd", comm=_join_halves(sums))
    dz, dkvm, dg_mq, dg_mk = _mem_bwd(z, kvm, g_mq, g_mk, do_mem, dz, batch, seq, "mem_bwd")
    dmemn = _mm(dkvm, full["w_mem_kv"], tb=True, name="mm_d_memn")
    gw["w_mem_kv"] = _owner_major(_mm(memn, dkvm, ta=True, name="mm_dw_memkv"), "w_mem_kv")
    _, _, dg_mem = _rms_bwd(mem2, g_mem, dmemn, None, "rms_mem_bwd")
    gw["w_in"] = _owner_major(_win_unlayout(_mm(h1, dz, ta=True, name="mm_dw_in")), "w_in")
    gw["w_uq"] = _owner_major(_wq_unlayout(dwq), "w_uq")
    gw["w_ukv"] = _owner_major(_wkv_unlayout(dwkv), "w_ukv")
    dh1, theirs = _mm(dz, win, tb=True, name="mm_d_h1_top", rows=(0, 2), comm=_pair_exchange([gw[n] for n in EARLY]))
    pairs = pair_sums(EARLY, theirs)
    dh1, slots = _mm(dz, win, tb=True, name="mm_d_h1_bottom", rows=(1, 2), into=dh1,
                     comm=_scatter_partials(pairs))
    grad_x, _, dg_mix = _rms_bwd(x2, g_mix, dh1, dx1, "rms_mix_bwd")
    reduced_early = _run_phase(_join_halves(chip_sums(EARLY, pairs, slots)), "join_early")
    reduced = dict(zip(LATE + EARLY, list(reduced_late) + list(reduced_early)))

    def swapped(a):
        return jnp.swapaxes(a, -1, -2)

    results = {n: _adamw(wts[n], reduced[n], mom[n], var[n], "adamw_" + n) for n in BIG if n != "w_in"}
    results["w_in"] = [swapped(r) for r in _adamw(swapped(w_in), swapped(reduced["w_in"]), swapped(m_w_in),
                                                  swapped(v_w_in), "adamw_w_in")]

    small_g = {"g_mix": dg_mix, "g_cq": dg_cq, "g_ckv": dg_ckv, "g_q_nope": dg_qn, "g_q_pe": dg_qp,
               "g_k_nope": dg_kn, "g_k_pe": dg_kp, "g_gm_ln": dg_ln, "b_gm_ln": db_ln, "w_spatial": dws,
               "b_spatial": jnp.concatenate(dbcols, axis=1).T, "g_mem": dg_mem, "g_mq": dg_mq, "g_mk": dg_mk,
               "g_ffn": dg_ffn}
    packed = _pack_small([small_g[n] for n in SMALL], loss_tile, "pack_small")
    small_out = _adamw_small(_gather_small(packed, "gather_small"), [wts[n] for n in SMALL],
                             [mom[n] for n in SMALL], [var[n] for n in SMALL], "adamw_small")
    for t, n in enumerate(SMALL):
        results[n] = [small_out[j * len(SMALL) + t] for j in range(4)]

    loss = small_out[4 * len(SMALL)][0, 0]
    grad_x = grad_x.reshape(batch, seq, D_MODEL)
    return (loss, grad_x, *[results[n][0] for n in WEIGHTS], *[results[n][1] for n in WEIGHTS],
            *[results[n][2] for n in WEIGHTS], *[results[n][3] for n in WEIGHTS])
```

```python
import functools
import math

import numpy as np
import jax
import jax.numpy as jnp
from jax import lax
from jax.experimental import pallas as pl
from jax.experimental.pallas import tpu as pltpu

F32 = jnp.float32
BF = jnp.bfloat16
SDS = jax.ShapeDtypeStruct
MESH = pl.DeviceIdType.MESH

D_MODEL = 1024
MEM_LEN = 256
MEM_HEADS = 4
HEAD_DIM = 128
GM_WIDTH = 512
GM_CHUNK = 128
GM_GROUPS = 4
MLA_HEADS = 8
MLA_NOPE = 128
MLA_ROPE = 64
MLA_V = 128
Q_LORA = 384
KV_LORA = 256
ROPE_BASE = 10000.0
D_FF = 4096
EPS = 1e-6
W_IN_COLS = 5312
ADAM_LR, ADAM_B1, ADAM_B2, ADAM_EPS, ADAM_WD, ADAM_STEP = 0.001, 0.9, 0.999, 1e-08, 0.01, 10

ZG, ZU, ZV, QM, CQ, KPE, CKV = 0, 3072, 3584, 4096, 4608, 4992, 5120
Z_COLS = 5376
LANES = 128
ROW_TILE = 512
ATT_TILE = 1024
ATT_HEADS = 2
ATT_HEADS_FWD = 4
VMEM_LIMIT = 60 * 1024 * 1024

N_CHIPS = 4
GATHER_PIECE_ROWS = 128
PIECE_ROWS = 256
SMALL_ROWS = 560

BIG = ["w_in", "w_uq", "w_ukv", "w_mem_kv", "w_o_gm", "w_o_mla", "w_o_mem", "w_out", "w_ff1", "w_ff2"]
BIG_SHAPE = {"w_in": (1024, 5312), "w_uq": (384, 1536), "w_ukv": (256, 2048), "w_mem_kv": (1024, 1024),
             "w_o_gm": (512, 1024), "w_o_mla": (1024, 1024), "w_o_mem": (512, 1024), "w_out": (1024, 1024),
             "w_ff1": (1024, 4096), "w_ff2": (4096, 1024)}
COL_SHARDED = {"w_in", "w_uq", "w_ukv", "w_o_gm", "w_o_mem", "w_ff1"}
EARLY = ["w_in", "w_uq", "w_ukv", "w_mem_kv"]
LATE_PROJ = ["w_o_gm", "w_o_mla", "w_o_mem", "w_out"]
LATE_FF = ["w_ff1", "w_ff2"]
LATE = LATE_PROJ + LATE_FF
SMALL = ["w_spatial", "b_spatial", "g_mix", "g_cq", "g_ckv", "g_q_nope", "g_q_pe", "g_k_nope", "g_k_pe", "g_gm_ln",
         "b_gm_ln", "g_mem", "g_mq", "g_mk", "g_ffn"]
SMALL_SHAPE = {"g_mix": (1, 1024), "g_cq": (1, 384), "g_ckv": (1, 256), "g_q_nope": (1, 128), "g_q_pe": (1, 64),
               "g_k_nope": (1, 128), "g_k_pe": (1, 64), "g_gm_ln": (1, 512), "b_gm_ln": (1, 512),
               "w_spatial": (1, 4, 128, 128), "b_spatial": (1, 4, 128), "g_mem": (1, 1024), "g_mq": (1, 128),
               "g_mk": (1, 128), "g_ffn": (1, 1024)}
WEIGHTS = ['g_mix', 'w_in', 'g_cq', 'w_uq', 'g_ckv', 'w_ukv', 'g_q_nope', 'g_q_pe', 'g_k_nope', 'g_k_pe',
           'g_gm_ln', 'b_gm_ln', 'w_spatial', 'b_spatial', 'g_mem', 'w_mem_kv', 'g_mq', 'g_mk', 'w_o_gm',
           'w_o_mla', 'w_o_mem', 'w_out', 'g_ffn', 'w_ff1', 'w_ff2']


def _params(sem=None):
    return pltpu.CompilerParams(vmem_limit_bytes=VMEM_LIMIT, dimension_semantics=sem)


def _pick(n, prefs):
    for p in prefs:
        if n % p == 0:
            return p
    return n


def _full(shape):
    nd = len(shape)
    return pl.BlockSpec(shape, lambda *_: (0,) * nd)


def _rows(t, w, blk=0):
    return pl.BlockSpec((t, w), lambda i: (i, blk))


def _acc(ref, val, first):
    @pl.when(first)
    def _():
        ref[...] = val

    @pl.when(jnp.logical_not(first))
    def _():
        ref[...] += val


ANY = pl.BlockSpec(memory_space=pl.ANY)


class _Phase:
    def __init__(self, operands, out_shapes, n_sem, n_local, copies, aliases=None):
        self.operands, self.out_shapes, self.aliases = list(operands), list(out_shapes), dict(aliases or {})
        self.n_sem, self.n_local, self.copies = n_sem, max(n_local, 1), copies

    def sem_shapes(self):
        return [pltpu.SemaphoreType.DMA((self.n_sem,)), pltpu.SemaphoreType.DMA((self.n_sem,)),
                pltpu.SemaphoreType.DMA((self.n_local,))]

    def start(self, ins, outs, send, recv, local):
        sends, _, locals_ = self.copies(ins, outs, send, recv, local)
        for cp in locals_ + sends:
            cp.start()

    def finish(self, ins, outs, send, recv, local):
        sends, arrivals, locals_ = self.copies(ins, outs, send, recv, local)
        for cp in arrivals:
            cp.wait_recv()
        for cp in sends:
            cp.wait_send()
        for cp in locals_:
            cp.wait()


class _Shifted:
    def __init__(self, ref, base):
        self.ref, self.base = ref, base

    @property
    def at(self):
        return self

    def __getitem__(self, i):
        return self.ref.at[i + self.base]


def _together(first, second):
    n_in, n_out = len(first.operands), len(first.out_shapes)

    def copies(ins, outs, send, recv, local):
        a = first.copies(ins[:n_in], outs[:n_out], send, recv, local)
        b = second.copies(ins[n_in:], outs[n_out:], _Shifted(send, first.n_sem), _Shifted(recv, first.n_sem),
                          _Shifted(local, first.n_local))
        return a[0] + b[0], a[1] + b[1], a[2] + b[2]

    aliases = {**first.aliases, **{n_in + i: n_out + j for i, j in second.aliases.items()}}
    return _Phase(first.operands + second.operands, first.out_shapes + second.out_shapes, first.n_sem + second.n_sem,
                  first.n_local + second.n_local, copies, aliases)


def _run_phase(phase, name):
    n_in = len(phase.operands)

    def body(*refs):
        ins, outs, sems = refs[:n_in], refs[n_in:n_in + len(phase.out_shapes)], refs[n_in + len(phase.out_shapes):]
        phase.start(ins, outs, *sems)
        phase.finish(ins, outs, *sems)

    return pl.pallas_call(body, in_specs=[ANY] * n_in, out_specs=[ANY] * len(phase.out_shapes),
                          out_shape=phase.out_shapes, scratch_shapes=phase.sem_shapes(),
                          input_output_aliases=phase.aliases, name=name)(*phase.operands)


def _pcall(body, *, grid, in_specs, out_specs, out_shape, scratch_shapes=(), sem=None, name, comm=None, aliases=None):
    single = not isinstance(out_shape, (list, tuple))
    o_specs = [out_specs] if single else list(out_specs)
    o_shape = [out_shape] if single else list(out_shape)
    aliases = dict(aliases or {})
    if comm is None:
        call = pl.pallas_call(body, grid=grid, in_specs=list(in_specs), out_specs=o_specs, out_shape=o_shape,
                              scratch_shapes=list(scratch_shapes), input_output_aliases=aliases,
                              compiler_params=_params(sem), name=name)

        def run_plain(*args):
            res = call(*args)
            return res[0] if single else res

        return run_plain

    n_in, n_out, n_scr = len(in_specs), len(o_specs), len(scratch_shapes)
    nc_in, nc_out = len(comm.operands), len(comm.out_shapes)

    def wrapped(*refs):
        ins, cins = refs[:n_in], refs[n_in:n_in + nc_in]
        o0 = n_in + nc_in
        outs, couts = refs[o0:o0 + n_out], refs[o0 + n_out:o0 + n_out + nc_out]
        s0 = o0 + n_out + nc_out
        scr, csem = refs[s0:s0 + n_scr], refs[s0 + n_scr:]
        ids = [pl.program_id(d) for d in range(len(grid))]
        first = functools.reduce(jnp.logical_and, [i == 0 for i in ids])
        last = functools.reduce(jnp.logical_and, [i == g - 1 for i, g in zip(ids, grid)])

        @pl.when(first)
        def _():
            comm.start(cins, couts, *csem)

        body(*ins, *outs, *scr)

        @pl.when(last)
        def _():
            comm.finish(cins, couts, *csem)

    call = pl.pallas_call(
        wrapped, grid=grid, in_specs=list(in_specs) + [ANY] * nc_in, out_specs=o_specs + [ANY] * nc_out,
        out_shape=o_shape + comm.out_shapes, scratch_shapes=list(scratch_shapes) + comm.sem_shapes(),
        input_output_aliases={**aliases, **{n_in + i: n_out + j for i, j in comm.aliases.items()}},
        compiler_params=_params(("arbitrary",) * len(grid)), name=name)

    def run_carrying(*args):
        res = call(*args, *comm.operands)
        return (res[0] if single else res[:n_out]), res[n_out:]

    return run_carrying


def _dn(a, b, ca, cb):
    return lax.dot_general(a.astype(BF), b.astype(BF), (((ca,), (cb,)), ((), ())), preferred_element_type=F32)


@jax.custom_vjp
def _mm_nn(a, b):
    return _dn(a, b, 1, 0)


def _mm_nn_fwd(a, b):
    return _dn(a, b, 1, 0), (a.astype(BF), b.astype(BF))


def _mm_nn_bwd(res, ct):
    a, b = res
    return _dn(ct, b, 1, 1), _dn(a, ct, 0, 0)


_mm_nn.defvjp(_mm_nn_fwd, _mm_nn_bwd)


@jax.custom_vjp
def _mm_nt(a, b):
    return _dn(a, b, 1, 1)


def _mm_nt_fwd(a, b):
    return _dn(a, b, 1, 1), (a.astype(BF), b.astype(BF))


def _mm_nt_bwd(res, ct):
    a, b = res
    return _dn(ct, b, 1, 0), _dn(ct, a, 0, 0)


_mm_nt.defvjp(_mm_nt_fwd, _mm_nt_bwd)


def _rmsn(x, g, n):
    ms = jnp.sum(x * x, axis=-1, keepdims=True) * (1.0 / n)
    return x * lax.rsqrt(ms + EPS) * g


def _layernorm(x, g, b):
    mu = jnp.mean(x, axis=-1, keepdims=True)
    xc = x - mu
    y = xc * lax.rsqrt(jnp.mean(xc * xc, axis=-1, keepdims=True) + EPS)
    return y * g + b


def _swap_lanes(x):
    half = MLA_ROPE // 2
    lane = lax.broadcasted_iota(jnp.int32, x.shape, 1)
    return jnp.where(lane < half, pltpu.roll(x, LANES - half, axis=1),
                     jnp.where(lane < MLA_ROPE, pltpu.roll(x, half, axis=1), 0.0))


@jax.custom_vjp
def _swap_halves(x):
    return _swap_lanes(x)


_swap_halves.defvjp(lambda x: (_swap_lanes(x), None), lambda _, ct: (_swap_lanes(ct),))


def _rope(x, cos_f, sin_s):
    return x * cos_f + _swap_halves(x) * sin_s


def _lane_blocks(x):
    return tuple(x[:, i * LANES:(i + 1) * LANES] for i in range(x.shape[1] // LANES))


@jax.custom_vjp
def _split_lanes(x):
    return _lane_blocks(x)


_split_lanes.defvjp(lambda x: (_lane_blocks(x), None), lambda _, cts: (jnp.concatenate(cts, axis=1),))


def _softmax(s):
    m = lax.stop_gradient(jnp.max(s, axis=-1, keepdims=True))
    p = jnp.exp(s - m)
    return p / jnp.sum(p, axis=-1, keepdims=True)


def _mm(a, b, *, ta=False, tb=False, ins=(), row_ins=(), epilogue=None, out_dtypes=(F32,), owner_cols=None,
        total=False, name, comm=None, rows=None, into=None):
    if ta:
        k_dim, m = a.shape
    else:
        m, k_dim = a.shape
    if tb:
        n, kb = b.shape
    else:
        kb, n = b.shape
    assert k_dim == kb, (a.shape, b.shape, ta, tb)
    part, n_parts = rows if rows is not None else (0, 1)
    tm = _pick(m // n_parts, (1024, 512, 256, 128))
    tn = _pick(n if owner_cols is None else owner_cols, (1024, 768, 512, 384, 256, 128))
    tk = _pick(k_dim, (2048, 1024, 768, 512, 256, 128))
    nk = k_dim // tk
    if m // tm // n_parts * (n // tn) * nk == 1 and tm >= 512 and tn >= 512 and not total:
        tm, tn = tm // 2, tn // 2
    m_steps = m // tm // n_parts
    off = part * m_steps
    ca = 0 if ta else 1
    cb = 1 if tb else 0
    n_in = len(ins) + len(row_ins)
    n_out = len(out_dtypes)
    n_pass = 0 if into is None else 1
    total_shape = total if isinstance(total, tuple) else (8, LANES)
    assert not (isinstance(total, tuple) and n != tn), "a per-column total needs the whole width in one tile"

    def finish(r, in_refs, out_refs, first_tile):
        vals = epilogue(r, *[ref[...].astype(F32) for ref in in_refs]) if epilogue is not None else (r,)
        for ref, val, dt in zip(out_refs, vals, out_dtypes):
            ref[...] = val.astype(dt)
        if total:
            _acc(out_refs[n_out], vals[n_out], first_tile)

    def body(*refs):
        a_ref, b_ref = refs[:2]
        in_refs = refs[2:2 + n_in]
        o0 = 2 + n_in + n_pass
        out_refs = refs[o0:o0 + n_out + int(bool(total))]
        first_tile = jnp.logical_and(pl.program_id(0) == 0, pl.program_id(1) == 0)
        part = _dn(a_ref[...], b_ref[...], ca, cb)
        if nk == 1:
            finish(part, in_refs, out_refs, first_tile)
            return
        acc = refs[-1]
        k = pl.program_id(2)
        _acc(acc, part, k == 0)

        @pl.when(k == nk - 1)
        def _():
            finish(acc[...], in_refs, out_refs, first_tile)

    a_spec = (pl.BlockSpec((tk, tm), lambda i, j, k: (k, i + off)) if ta
              else pl.BlockSpec((tm, tk), lambda i, j, k: (i + off, k)))
    b_spec = pl.BlockSpec((tn, tk), lambda i, j, k: (j, k)) if tb else pl.BlockSpec((tk, tn), lambda i, j, k: (k, j))
    t_spec = pl.BlockSpec((tm, tn), lambda i, j, k: (i + off, j))
    if owner_cols is None:
        o_spec, o_shape = t_spec, (m, n)
    else:
        per = owner_cols // tn
        o_spec = pl.BlockSpec((None, tm, tn), lambda i, j, k: (j // per, i + off, j % per))
        o_shape = (n // owner_cols, m, owner_cols)
    o_specs = [o_spec] * n_out + ([pl.BlockSpec(total_shape, lambda i, j, k: (0, 0))] if total else [])
    o_shapes = [SDS(o_shape, dt) for dt in out_dtypes] + ([SDS(total_shape, F32)] if total else [])
    row_spec = pl.BlockSpec((1, tn), lambda i, j, k: (0, j))
    in_specs = [a_spec, b_spec] + [t_spec] * len(ins) + [row_spec] * len(row_ins) + [ANY] * n_pass
    args = [a, b, *ins, *row_ins] + ([into] if n_pass else [])
    run = _pcall(body, grid=(m_steps, n // tn, nk), in_specs=in_specs, out_specs=o_specs, out_shape=o_shapes,
                 scratch_shapes=[pltpu.VMEM((tm, tn), F32)] if nk > 1 else [],
                 sem=("arbitrary",) * 3 if total else ("parallel", "parallel", "arbitrary"), name=name, comm=comm,
                 aliases={len(in_specs) - 1: 0} if n_pass else None)
    if comm is None:
        outs = run(*args)
        return outs[0] if len(outs) == 1 else outs
    outs, exchanged = run(*args)
    return (outs[0] if len(outs) == 1 else outs), exchanged


def _add_to(r, x):
    return (r + x,)


def _residual_rms(r, x, g):
    x1 = r + x
    return x1, _rmsn(x1, g, D_MODEL)


def _rms_bwd_tail(dh, x, res, g):
    _, vjp = jax.vjp(lambda xx, gg: _rmsn(xx, gg, D_MODEL), x, g)
    dx, dg = vjp(dh)
    dx = dx + res
    return dx, dx, dg


def _relu2(r):
    p = jnp.maximum(r, 0.0)
    return r, p * p


def _relu2_bwd(dr, a):
    return (dr * (2.0 * jnp.maximum(a, 0.0)),)


def _loss_tail(r, x1, tgt):
    e = (r + x1) - tgt
    dy = e * (1.0 / D_MODEL)
    part = jnp.sum(jnp.sum(e * e, axis=-1, keepdims=True), axis=0, keepdims=True) * (0.5 / D_MODEL)
    return dy, dy, jnp.broadcast_to(part, (8, LANES))


def _rms_fwd(x, g, name, comm=None):
    n, w = x.shape
    t = min(ROW_TILE, n)

    def body(x_ref, g_ref, o_ref):
        o_ref[...] = _rmsn(x_ref[...], g_ref[...], w).astype(BF)

    return _pcall(body, grid=(n // t,), in_specs=[_rows(t, w), _full((1, w))], out_specs=_rows(t, w),
                  out_shape=SDS((n, w), BF), sem=("arbitrary",), name=name, comm=comm)(x, g)


def _rms_bwd(x, g, dh, res, name, comm=None):
    n, w = x.shape
    t = min(ROW_TILE, n)
    has_res = res is not None

    def body(*refs):
        if has_res:
            x_ref, g_ref, dh_ref, res_ref, dx_ref, dxb_ref, dg_ref = refs
        else:
            x_ref, g_ref, dh_ref, dx_ref, dxb_ref, dg_ref = refs
        _, vjp = jax.vjp(lambda xx, gg: _rmsn(xx, gg, w), x_ref[...], g_ref[...])
        dx, dg = vjp(dh_ref[...])
        if has_res:
            dx = dx + res_ref[...]
        dx_ref[...] = dx
        dxb_ref[...] = dx.astype(BF)
        _acc(dg_ref, dg, pl.program_id(0) == 0)

    in_specs = [_rows(t, w), _full((1, w)), _rows(t, w)] + ([_rows(t, w)] if has_res else [])
    args = [x, g, dh] + ([res] if has_res else [])
    return _pcall(body, grid=(n // t,), in_specs=in_specs, out_specs=[_rows(t, w), _rows(t, w), _full((1, w))],
                  out_shape=[SDS((n, w), F32), SDS((n, w), BF), SDS((1, w), F32)], sem=("arbitrary",), name=name,
                  comm=comm)(*args)


def _merge_core(zg0, zg1, zg2, y0, y1, y2):
    return jax.nn.sigmoid(zg0) * y0 + jax.nn.sigmoid(zg1) * y1 + jax.nn.sigmoid(zg2) * y2


def _merge_fwd(z, y_gm, y_mla, y_mem, name):
    n = z.shape[0]
    t = min(ROW_TILE, n)
    w = D_MODEL

    def body(g0, g1, g2, y0, y1, y2, o_ref):
        o_ref[...] = _merge_core(g0[...].astype(F32), g1[...].astype(F32), g2[...].astype(F32), y0[...].astype(F32), y1[...].astype(F32),
                                 y2[...].astype(F32)).astype(BF)

    return pl.pallas_call(body, grid=(n // t,),
                          in_specs=[_rows(t, w, 0), _rows(t, w, 1), _rows(t, w, 2)] + [_rows(t, w)] * 3,
                          out_specs=_rows(t, w), out_shape=SDS((n, w), BF),
                          compiler_params=_params(("parallel",)), name=name)(z, z, z, y_gm, y_mla, y_mem)


def _merge_bwd(z, y_gm, y_mla, y_mem, dmerged, name):
    n = z.shape[0]
    t = min(ROW_TILE, n)
    w = D_MODEL

    def body(g0, g1, g2, y0, y1, y2, dm, dzg_ref, d0_ref, d1_ref, d2_ref):
        _, vjp = jax.vjp(_merge_core, g0[...].astype(F32), g1[...].astype(F32), g2[...].astype(F32), y0[...].astype(F32), y1[...].astype(F32),
                         y2[...].astype(F32))
        dg0, dg1, dg2, dy0, dy1, dy2 = vjp(dm[...])
        dzg_ref[:, 0:w] = dg0.astype(BF)
        dzg_ref[:, w:2 * w] = dg1.astype(BF)
        dzg_ref[:, 2 * w:3 * w] = dg2.astype(BF)
        d0_ref[...] = dy0.astype(BF)
        d1_ref[...] = dy1.astype(BF)
        d2_ref[...] = dy2.astype(BF)

    return pl.pallas_call(body, grid=(n // t,),
                          in_specs=[_rows(t, w, 0), _rows(t, w, 1), _rows(t, w, 2)] + [_rows(t, w)] * 4,
                          out_specs=[_rows(t, 3 * w, ZG // (3 * w))] + [_rows(t, w)] * 3,
                          out_shape=[SDS((n, Z_COLS), BF)] + [SDS((n, w), BF)] * 3,
                          compiler_params=_params(("parallel",)), name=name)(z, z, z, y_gm, y_mla, y_mem, dmerged)


def _gm_core(zu, zv, g_ln, b_ln, ws, bcols):
    t = zu.shape[0]
    u = jax.nn.gelu(zu)
    v = _layernorm(jax.nn.gelu(zv), g_ln, b_ln)
    row = lax.broadcasted_iota(jnp.int32, (GM_CHUNK, GM_CHUNK), 0)
    col = lax.broadcasted_iota(jnp.int32, (GM_CHUNK, GM_CHUNK), 1)
    wc = [jnp.where(row >= col, ws[g], 0.0) for g in range(GM_GROUPS)]
    chunks = []
    for c in range(t // GM_CHUNK):
        cols = []
        for g in range(GM_GROUPS):
            vc = v[c * GM_CHUNK:(c + 1) * GM_CHUNK, g * LANES:(g + 1) * LANES]
            cols.append(_mm_nn(wc[g], vc) + bcols[g])
        chunks.append(jnp.concatenate(cols, axis=1))
    mixed = chunks[0] if len(chunks) == 1 else jnp.concatenate(chunks, axis=0)
    return u * mixed


def _gm_specs(t):
    return [_rows(t, GM_WIDTH, ZU // GM_WIDTH), _rows(t, GM_WIDTH, ZV // GM_WIDTH), _full((1, GM_WIDTH)),
            _full((1, GM_WIDTH)), _full((GM_GROUPS, GM_CHUNK, GM_CHUNK))] + [_full((GM_CHUNK, 1))] * GM_GROUPS


def _gm_fwd(z, g_ln, b_ln, ws, bcols, name):
    n = z.shape[0]
    t = min(ROW_TILE, n)

    def body(zu, zv, g_ref, b_ref, ws_ref, c0, c1, c2, c3, o_ref):
        out = _gm_core(zu[...].astype(F32), zv[...].astype(F32), g_ref[...], b_ref[...], [ws_ref[g] for g in range(GM_GROUPS)],
                       [c0[...], c1[...], c2[...], c3[...]])
        o_ref[...] = out.astype(BF)

    return pl.pallas_call(body, grid=(n // t,), in_specs=_gm_specs(t), out_specs=_rows(t, GM_WIDTH),
                          out_shape=SDS((n, GM_WIDTH), BF), compiler_params=_params(("parallel",)),
                          name=name)(z, z, g_ln, b_ln, ws, *bcols)


def _gm_bwd(z, g_ln, b_ln, ws, bcols, dgm, dz, name, comm=None):
    n = z.shape[0]
    t = min(ROW_TILE, n)

    def body(zu, zv, g_ref, b_ref, ws_ref, c0, c1, c2, c3, dgm_ref, _, dz_ref, dg_ref, db_ref, dws_ref, e0, e1, e2,
             e3):
        first = pl.program_id(0) == 0
        _, vjp = jax.vjp(_gm_core, zu[...].astype(F32), zv[...].astype(F32), g_ref[...], b_ref[...],
                         [ws_ref[g] for g in range(GM_GROUPS)], [c0[...], c1[...], c2[...], c3[...]])
        dzu, dzv, dg, db, dws, dcols = vjp(dgm_ref[...])
        dz_ref[:, 0:GM_WIDTH] = dzu.astype(BF)
        dz_ref[:, GM_WIDTH:2 * GM_WIDTH] = dzv.astype(BF)
        _acc(dg_ref, dg, first)
        _acc(db_ref, db, first)
        _acc(dws_ref, jnp.stack(dws, axis=0), first)
        for ref, val in zip((e0, e1, e2, e3), dcols):
            _acc(ref, val, first)

    in_specs = _gm_specs(t) + [_rows(t, GM_WIDTH), ANY]
    return _pcall(
        body, grid=(n // t,), in_specs=in_specs,
        out_specs=[_rows(t, 2 * GM_WIDTH, ZU // (2 * GM_WIDTH)), _full((1, GM_WIDTH)), _full((1, GM_WIDTH)),
                   _full((GM_GROUPS, GM_CHUNK, GM_CHUNK))] + [_full((GM_CHUNK, 1))] * GM_GROUPS,
        out_shape=[SDS((n, Z_COLS), BF), SDS((1, GM_WIDTH), F32), SDS((1, GM_WIDTH), F32),
                   SDS((GM_GROUPS, GM_CHUNK, GM_CHUNK), F32)] + [SDS((GM_CHUNK, 1), F32)] * GM_GROUPS,
        sem=("arbitrary",), name=name, comm=comm, aliases={len(in_specs) - 1: 0})(z, z, g_ln, b_ln, ws, *bcols, dgm, dz)


def _rope_tables(pos_f, inv_full, cmask, smask, name, comm=None):
    n = pos_f.shape[0]
    t = min(ROW_TILE, n)

    def body(p_ref, inv_ref, cm_ref, sm_ref, cos_ref, sin_ref):
        ang = p_ref[...] * inv_ref[...]
        cos_ref[...] = jnp.cos(ang) * cm_ref[...]
        sin_ref[...] = jnp.sin(ang) * sm_ref[...]

    return _pcall(body, grid=(n // t,), in_specs=[_rows(t, 1)] + [_full((1, LANES))] * 3,
                  out_specs=[_rows(t, LANES)] * 2, out_shape=[SDS((n, LANES), F32)] * 2, sem=("parallel",),
                  name=name, comm=comm)(pos_f, inv_full, cmask, smask)


def _prep_norms(cq, ckv, g_cq, g_ckv):
    return _rmsn(cq, g_cq, Q_LORA), _rmsn(ckv, g_ckv, KV_LORA)


def _prep_heads(qa, kva, kpe, head_gains, cos_f, sin_s):
    g_qn, g_qp, g_kn, g_kp = head_gains
    qs = _split_lanes(qa)
    kvs = _split_lanes(kva)
    kp = _rope(_rmsn(kpe, g_kp, MLA_ROPE), cos_f, sin_s)
    q_out, k_out = [], []
    for h in range(MLA_HEADS):
        q_out.append(_rmsn(qs[h], g_qn, MLA_NOPE))
        q_out.append(_rope(_rmsn(qs[MLA_HEADS + h], g_qp, MLA_ROPE), cos_f, sin_s))
        k_out.append(_rmsn(kvs[h], g_kn, MLA_NOPE))
        k_out.append(kp)
    return (jnp.concatenate(q_out, axis=1), jnp.concatenate(k_out, axis=1),
            jnp.concatenate(kvs[MLA_HEADS:], axis=1))


def _prep_in_specs(t):
    return ([_rows(t, Q_LORA, CQ // Q_LORA), _rows(t, LANES, KPE // LANES), _rows(t, KV_LORA, CKV // KV_LORA),
             _rows(t, LANES), _rows(t, LANES), _full((1, Q_LORA)), _full((1, KV_LORA))] + [_full((1, LANES))] * 4
            + [_full((Q_LORA, 2048)), _full((KV_LORA, 2048))])


def _prep_fwd(z, cos_f, sin_s, gains, wq, wkv, name):
    n = z.shape[0]
    t = min(ROW_TILE, n)

    def body(cq, kpe, ckv, cos_ref, sin_ref, g_cq, g_ckv, g_qn, g_qp, g_kn, g_kp, wq_ref, wkv_ref, q_ref, k_ref, v_ref):
        cqn, ckvn = _prep_norms(cq[...].astype(F32), ckv[...].astype(F32), g_cq[...], g_ckv[...])
        qa = _dn(cqn, wq_ref[...], 1, 0)
        kva = _dn(ckvn, wkv_ref[...], 1, 0)
        q, k, v = _prep_heads(qa, kva, kpe[...].astype(F32), (g_qn[...], g_qp[...], g_kn[...], g_kp[...]), cos_ref[...],
                              sin_ref[...])
        q_ref[...] = q.astype(BF)
        k_ref[...] = k.astype(BF)
        v_ref[...] = v.astype(BF)

    return pl.pallas_call(body, grid=(n // t,), in_specs=_prep_in_specs(t),
                          out_specs=[_rows(t, 2048), _rows(t, 2048), _rows(t, 1024)],
                          out_shape=[SDS((n, 2048), BF), SDS((n, 2048), BF), SDS((n, 1024), BF)],
                          compiler_params=_params(("parallel",)),
                          name=name)(z, z, z, cos_f, sin_s, *gains, wq, wkv)


def _prep_bwd(z, cos_f, sin_s, gains, wq, wkv, dq, dk, dv, dz, name, comm=None):
    n = z.shape[0]
    t = min(ROW_TILE, n)
    wz = Q_LORA + LANES + KV_LORA

    def body(cq, kpe, ckv, cos_ref, sin_ref, g_cq, g_ckv, g_qn, g_qp, g_kn, g_kp, wq_ref, wkv_ref, dq_ref, dk_ref,
             dv_ref, _, dz_ref, o_cq, o_ckv, o_qn, o_qp, o_kn, o_kp, dwq_ref, dwkv_ref):
        first = pl.program_id(0) == 0
        cos_t, sin_t = cos_ref[...], sin_ref[...]
        (cqn, ckvn), vjp_norms = jax.vjp(_prep_norms, cq[...].astype(F32), ckv[...].astype(F32), g_cq[...], g_ckv[...])
        wq_t, wkv_t = wq_ref[...], wkv_ref[...]
        qa = _dn(cqn, wq_t, 1, 0)
        kva = _dn(ckvn, wkv_t, 1, 0)
        _, vjp_heads = jax.vjp(lambda a, b, c, g: _prep_heads(a, b, c, g, cos_t, sin_t), qa, kva, kpe[...].astype(F32),
                               (g_qn[...], g_qp[...], g_kn[...], g_kp[...]))
        dqa, dkva, dkpe, dhead = vjp_heads((dq_ref[...], dk_ref[...], dv_ref[...]))
        _acc(dwq_ref, _dn(cqn, dqa, 0, 0), first)
        _acc(dwkv_ref, _dn(ckvn, dkva, 0, 0), first)
        dcq, dckv, dg_cq, dg_ckv = vjp_norms((_dn(dqa, wq_t, 1, 1), _dn(dkva, wkv_t, 1, 1)))
        dz_ref[:, 0:Q_LORA] = dcq.astype(BF)
        dz_ref[:, Q_LORA:Q_LORA + LANES] = dkpe.astype(BF)
        dz_ref[:, Q_LORA + LANES:wz] = dckv.astype(BF)
        for ref, val in zip((o_cq, o_ckv, o_qn, o_qp, o_kn, o_kp), (dg_cq, dg_ckv) + tuple(dhead)):
            _acc(ref, val, first)

    gain_specs = [_full((1, Q_LORA)), _full((1, KV_LORA))] + [_full((1, LANES))] * 4
    gain_shapes = [SDS((1, Q_LORA), F32), SDS((1, KV_LORA), F32)] + [SDS((1, LANES), F32)] * 4
    in_specs = _prep_in_specs(t) + [_rows(t, 2048), _rows(t, 2048), _rows(t, 1024), ANY]
    return _pcall(
        body, grid=(n // t,), in_specs=in_specs,
        out_specs=[_rows(t, wz, CQ // wz)] + gain_specs + [_full((Q_LORA, 2048)), _full((KV_LORA, 2048))],
        out_shape=[SDS((n, Z_COLS), BF)] + gain_shapes + [SDS((Q_LORA, 2048), F32), SDS((KV_LORA, 2048), F32)],
        sem=("arbitrary",), name=name, comm=comm,
        aliases={len(in_specs) - 1: 0})(z, z, z, cos_f, sin_s, *gains, wq, wkv, dq, dk, dv, dz)


MLA_QK = 256
MLA_SCALE = 1.0 / math.sqrt(MLA_NOPE + MLA_ROPE)
LOG2E = 1.0 / math.log(2.0)
MLA_SCALE_LOG2E = MLA_SCALE * LOG2E


def _causal_mask(s, q0, k0):
    tq, tk = s.shape
    row = q0 + lax.broadcasted_iota(jnp.int32, (tq, tk), 0)
    col = k0 + lax.broadcasted_iota(jnp.int32, (tq, tk), 1)
    return jnp.where(row >= col, s, -jnp.inf)


def _mla_fwd(q, k, v, batch, seq, name, comm=None):
    n = q.shape[0]
    tq = min(ATT_TILE, seq)
    nq = seq // tq

    nh = ATT_HEADS_FWD

    def body(q_ref, k_ref, v_ref, o_ref, lse_ref):
        i = pl.program_id(2)

        def step(j, carry, diagonal=False):
            k0 = pl.multiple_of(j * tq, tq)
            out = []
            ones = jnp.ones((tq, LANES), BF)
            for hh in range(nh):
                m, acc = carry[hh]
                qb = q_ref[:, hh * MLA_QK:(hh + 1) * MLA_QK]
                kb = k_ref[pl.ds(k0, tq), hh * MLA_QK:(hh + 1) * MLA_QK]
                vb = v_ref[pl.ds(k0, tq), hh * MLA_V:(hh + 1) * MLA_V]
                s = _dn(qb, kb, 1, 1)
                if diagonal:
                    s = _causal_mask(s, i * tq, k0)
                m_new = jnp.maximum(m, jnp.max(s, axis=-1, keepdims=True))
                p = jnp.exp2((s - m_new) * MLA_SCALE_LOG2E)
                alpha = jnp.exp2((m - m_new) * MLA_SCALE_LOG2E)
                acc = alpha * acc + _dn(p, jnp.concatenate([vb, ones], axis=1), 1, 0)
                out.append((m_new, acc))
            return tuple(out)

        init = tuple((jnp.full((tq, 1), -jnp.inf, F32), jnp.zeros((tq, MLA_V + LANES), F32)) for _ in range(nh))
        final = step(i, lax.fori_loop(0, i, step, init), diagonal=True)
        for hh, (m, acc) in enumerate(final):
            l = acc[:, MLA_V:MLA_V + 1]
            o_ref[:, hh * MLA_V:(hh + 1) * MLA_V] = acc[:, :MLA_V] / l
            lse_ref[:, hh * LANES:(hh + 1) * LANES] = jnp.broadcast_to(m * MLA_SCALE + jnp.log(l), (tq, LANES))

    return _pcall(
        body, grid=(batch, MLA_HEADS // nh, nq),
        in_specs=[pl.BlockSpec((tq, nh * MLA_QK), lambda b, h, i: (b * nq + i, h)),
                  pl.BlockSpec((seq, nh * MLA_QK), lambda b, h, i: (b, h)),
                  pl.BlockSpec((seq, nh * MLA_V), lambda b, h, i: (b, h))],
        out_specs=[pl.BlockSpec((tq, nh * MLA_V), lambda b, h, i: (b * nq + i, h)),
                   pl.BlockSpec((tq, nh * LANES), lambda b, h, i: (b * nq + i, h))],
        out_shape=[SDS((n, MLA_HEADS * MLA_V), F32), SDS((n, MLA_HEADS * LANES), F32)],
        sem=("parallel", "parallel", "arbitrary"), name=name, comm=comm)(q, k, v)


def _mla_bwd(q, k, v, o, lse, do, batch, seq, name, comm=None):
    n = q.shape[0]
    tk = min(ATT_TILE, seq)
    nk = seq // tk

    nh = ATT_HEADS

    def body(q_ref, k_ref, v_ref, o_ref, lse_ref, do_ref, dq_ref, dk_ref, dv_ref):
        jk = pl.program_id(2)

        @pl.when(jk == 0)
        def _():
            dq_ref[...] = jnp.zeros_like(dq_ref)

        def step(i, carry, diagonal=False):
            q0 = pl.multiple_of(i * tk, tk)
            rows = pl.ds(q0, tk)
            out = []
            for hh in range(nh):
                dk_acc, dv_acc = carry[hh]
                qk_cols = slice(hh * MLA_QK, (hh + 1) * MLA_QK)
                v_cols = slice(hh * MLA_V, (hh + 1) * MLA_V)
                kb = k_ref[:, qk_cols]
                vb = v_ref[:, v_cols]
                qb = q_ref[rows, qk_cols]
                dob = do_ref[rows, v_cols]
                delta = jnp.sum(dob * o_ref[rows, v_cols], axis=-1, keepdims=True)
                s = _dn(qb, kb, 1, 1)
                if diagonal:
                    s = _causal_mask(s, q0, jk * tk)
                p = jnp.exp2(s * MLA_SCALE_LOG2E - lse_ref[rows, hh * LANES:hh * LANES + 1] * LOG2E)
                dv_acc = dv_acc + _dn(p, dob, 0, 0)
                dp = _dn(dob, vb, 1, 1)
                ds = p * (dp - delta) * MLA_SCALE
                dk_acc = dk_acc + _dn(ds, qb, 0, 0)
                dq_ref[rows, qk_cols] += _dn(ds, kb, 1, 0)
                out.append((dk_acc, dv_acc))
            return tuple(out)

        init = tuple((jnp.zeros((tk, MLA_QK), F32), jnp.zeros((tk, MLA_V), F32)) for _ in range(nh))
        final = lax.fori_loop(jk + 1, nk, step, step(jk, init, diagonal=True))
        for hh, (dk_acc, dv_acc) in enumerate(final):
            dk_ref[:, hh * MLA_QK:(hh + 1) * MLA_QK] = dk_acc
            dv_ref[:, hh * MLA_V:(hh + 1) * MLA_V] = dv_acc

    full_qk = pl.BlockSpec((seq, nh * MLA_QK), lambda b, h, j: (b, h))
    full_v = pl.BlockSpec((seq, nh * MLA_V), lambda b, h, j: (b, h))
    blk_qk = pl.BlockSpec((tk, nh * MLA_QK), lambda b, h, j: (b * nk + j, h))
    blk_v = pl.BlockSpec((tk, nh * MLA_V), lambda b, h, j: (b * nk + j, h))
    return _pcall(
        body, grid=(batch, MLA_HEADS // nh, nk),
        in_specs=[full_qk, blk_qk, blk_v, full_v, full_v, full_v],
        out_specs=[full_qk, blk_qk, blk_v],
        out_shape=[SDS((n, MLA_HEADS * MLA_QK), F32), SDS((n, MLA_HEADS * MLA_QK), F32),
                   SDS((n, MLA_HEADS * MLA_V), F32)],
        sem=("parallel", "parallel", "arbitrary"), name=name, comm=comm)(q, k, v, o, lse, do)


MEM_SCALE = 1.0 / math.sqrt(HEAD_DIM)
MEM_W = MEM_HEADS * HEAD_DIM


def _mem_core(qs, ks, vs, g_mq, g_mk):
    outs = []
    for h in range(MEM_HEADS):
        qh = _rmsn(qs[h], g_mq, HEAD_DIM)
        kh = _rmsn(ks[h], g_mk, HEAD_DIM)
        p = _softmax(_mm_nt(qh, kh) * MEM_SCALE)
        outs.append(_mm_nn(p, vs[h]))
    return jnp.concatenate(outs, axis=1)


def _mem_load(qm, kvm, g_mq, g_mk):
    hs = range(MEM_HEADS)
    qs = [qm[:, h * LANES:(h + 1) * LANES].astype(F32) for h in hs]
    ks = [kvm[:, h * LANES:(h + 1) * LANES] for h in hs]
    vs = [kvm[:, MEM_W + h * LANES:MEM_W + (h + 1) * LANES] for h in hs]
    return qs, ks, vs, g_mq[...], g_mk[...]


def _mem_fwd(z, kvm, g_mq, g_mk, batch, seq, name, comm=None):
    n = z.shape[0]
    t = min(ROW_TILE, seq)
    per = seq // t

    def body(qm, kvm_ref, gq, gk, o_ref):
        o_ref[...] = _mem_core(*_mem_load(qm, kvm_ref, gq, gk)).astype(BF)

    return _pcall(
        body, grid=(n // t,),
        in_specs=[_rows(t, MEM_W, QM // MEM_W), pl.BlockSpec((MEM_LEN, 2 * MEM_W), lambda i: (i // per, 0)),
                  _full((1, LANES)), _full((1, LANES))],
        out_specs=_rows(t, MEM_W), out_shape=SDS((n, MEM_W), BF), sem=("parallel",), name=name,
        comm=comm)(z, kvm, g_mq, g_mk)


def _mem_bwd(z, kvm, g_mq, g_mk, dom, dz, batch, seq, name):
    n = z.shape[0]
    t = min(ROW_TILE, seq)
    per = seq // t

    def body(qm, kvm_ref, gq, gk, dom_ref, _, dz_ref, dkvm_ref, dgq_ref, dgk_ref):
        i = pl.program_id(0)
        _, vjp = jax.vjp(_mem_core, *_mem_load(qm, kvm_ref, gq, gk))
        dqs, dks, dvs, dgq, dgk = vjp(dom_ref[...])
        dz_ref[...] = jnp.concatenate(dqs, axis=1).astype(BF)
        _acc(dkvm_ref, jnp.concatenate(dks + dvs, axis=1), i % per == 0)
        _acc(dgq_ref, dgq, i == 0)
        _acc(dgk_ref, dgk, i == 0)

    kv_spec = pl.BlockSpec((MEM_LEN, 2 * MEM_W), lambda i: (i // per, 0))
    return pl.pallas_call(
        body, grid=(n // t,),
        in_specs=[_rows(t, MEM_W, QM // MEM_W), kv_spec, _full((1, LANES)), _full((1, LANES)), _rows(t, MEM_W), ANY],
        out_specs=[_rows(t, MEM_W, QM // MEM_W), kv_spec, _full((1, LANES)), _full((1, LANES))],
        out_shape=[SDS((n, Z_COLS), BF), SDS((batch * MEM_LEN, 2 * MEM_W), F32), SDS((1, LANES), F32),
                   SDS((1, LANES), F32)],
        input_output_aliases={5: 0},
        compiler_params=_params(("arbitrary",)), name=name)(z, kvm, g_mq, g_mk, dom, dz)


def _me():
    return lax.axis_index("x"), lax.axis_index("y"), lax.axis_index("c")


def _other_chips(x, y):
    return [(1 - x, y), (x, 1 - y), (1 - x, 1 - y)]


def _shard_shape(name):
    r, c = BIG_SHAPE[name]
    return (r, c // N_CHIPS) if name in COL_SHARDED else (r // N_CHIPS, c)


def _n_pieces(half_rows, piece_rows=PIECE_ROWS):
    return max(1, half_rows // piece_rows)


def _piece_plan(shapes, piece_rows=PIECE_ROWS):
    plan = []
    for r, _ in shapes:
        h = r // 2
        n = _n_pieces(h, piece_rows)
        plan.append((h, n, h // n))
    return plan


def _remote(send, recv, sem, src, dst, to):
    return pltpu.make_async_remote_copy(src_ref=src, dst_ref=dst, send_sem=send.at[sem], recv_sem=recv.at[sem],
                                        device_id=to, device_id_type=MESH)


def _gather_far(shards):
    plan = _piece_plan([s.shape for s in shards], GATHER_PIECE_ROWS)
    n_far = 3 * sum(n for _, n, _ in plan)
    n_loc = 2 * sum(n for _, n, _ in plan)

    def copies(s_refs, o_refs, send, recv, local):
        x, y, c = _me()
        k = 2 * x + y
        mine, sends, arrivals = [], [], []
        for t, (h, n, pr) in enumerate(plan):
            s_ref, o_ref = s_refs[t], o_refs[t]
            for core in range(2):
                for p in range(n):
                    rows = pl.ds(core * h + p * pr, pr)
                    mine.append(pltpu.make_async_copy(s_ref.at[rows], o_ref.at[k, rows], local.at[len(mine)]))
            for chip in _other_chips(x, y):
                for p in range(n):
                    rows = pl.ds(c * h + p * pr, pr)
                    s = len(sends)
                    sends.append(_remote(send, recv, s, s_ref.at[rows], o_ref.at[k, rows], (*chip, c)))
                    arrivals.append(_remote(send, recv, s, s_ref.at[rows], o_ref.at[2 * chip[0] + chip[1], rows],
                                            (*chip, c)))
        return sends, arrivals, mine

    return _Phase(shards, [SDS((N_CHIPS,) + s.shape, s.dtype) for s in shards], n_far, n_loc, copies)


def _gather_near(bufs):
    plan = _piece_plan([b.shape[1:] for b in bufs], GATHER_PIECE_ROWS)
    n_sem = 3 * sum(n for _, n, _ in plan)

    def copies(i_refs, o_refs, send, recv, local):
        x, y, c = _me()
        sib = (x, y, 1 - c)
        sends, arrivals = [], []
        for t, (h, n, pr) in enumerate(plan):
            for chip in _other_chips(x, y):
                ci = 2 * chip[0] + chip[1]
                for p in range(n):
                    rows = pl.ds(c * h + p * pr, pr)
                    rows_sib = pl.ds((1 - c) * h + p * pr, pr)
                    s = len(sends)
                    sends.append(_remote(send, recv, s, i_refs[t].at[ci, rows], o_refs[t].at[ci, rows], sib))
                    arrivals.append(_remote(send, recv, s, i_refs[t].at[ci, rows_sib], o_refs[t].at[ci, rows_sib], sib))
        return sends, arrivals, []

    return _Phase(bufs, [SDS(b.shape, b.dtype) for b in bufs], n_sem, 0, copies, {t: t for t in range(len(bufs))})


def _pair_exchange(grads):
    plan = _piece_plan([g.shape[1:] for g in grads])
    n_sem = sum(n for _, n, _ in plan)

    def copies(g_refs, o_refs, send, recv, local):
        x, y, c = _me()
        sends = []
        for t, (h, n, pr) in enumerate(plan):
            for p in range(n):
                sends.append(_remote(send, recv, len(sends), g_refs[t].at[:, pl.ds((1 - c) * h + p * pr, pr)],
                                     o_refs[t].at[:, pl.ds(p * pr, pr)], (x, y, 1 - c)))
        return sends, sends, []

    return _Phase(grads, [SDS((N_CHIPS, g.shape[1] // 2, g.shape[2]), F32) for g in grads], n_sem, 0, copies)


def _pair_add(ck, g, theirs, name):
    _, r, c = g.shape
    (h, n, pr), = _piece_plan([(r, c)])

    def body(ck_ref, g_ref, t_ref, pbf_ref):
        pbf_ref[...] = (g_ref[...] + t_ref[...]).astype(BF)

    half = pl.BlockSpec((None, pr, c), lambda k, p, ck: (k, p, 0))
    spec = pltpu.PrefetchScalarGridSpec(
        num_scalar_prefetch=1, grid=(N_CHIPS, n),
        in_specs=[pl.BlockSpec((None, pr, c), lambda k, p, ck: (k, ck[0] * n + p, 0)), half], out_specs=half)
    return pl.pallas_call(body, grid_spec=spec, out_shape=SDS((N_CHIPS, h, c), BF),
                          compiler_params=_params(("arbitrary", "arbitrary")), name=name)(ck, g, theirs)


def _scatter_partials(pbfs):
    plan = [(h, _n_pieces(h), h // _n_pieces(h)) for h in [p.shape[1] for p in pbfs]]
    n_sem = 3 * sum(n for _, n, _ in plan)

    def copies(p_refs, o_refs, send, recv, local):
        x, y, c = _me()
        sends = []
        for t, (h, n, pr) in enumerate(plan):
            for j, chip in enumerate(_other_chips(x, y)):
                for p in range(n):
                    rows = pl.ds(p * pr, pr)
                    sends.append(_remote(send, recv, len(sends), p_refs[t].at[2 * chip[0] + chip[1], rows],
                                         o_refs[t].at[j, rows], (*chip, c)))
        return sends, sends, []

    return _Phase(pbfs, [SDS((3,) + p.shape[1:], BF) for p in pbfs], n_sem, 0, copies)


def _sum_chips(ck, pbf, slots, name):
    _, h, c = pbf.shape
    n = _n_pieces(h)
    pr = h // n

    def body(ck_ref, p_ref, s_ref, o_ref):
        o_ref[...] = (((p_ref[...].astype(F32) + s_ref[0].astype(F32)) + s_ref[1].astype(F32))
                      + s_ref[2].astype(F32))

    spec = pltpu.PrefetchScalarGridSpec(
        num_scalar_prefetch=1, grid=(n,),
        in_specs=[pl.BlockSpec((None, pr, c), lambda p, ck: (ck[1], p, 0)),
                  pl.BlockSpec((3, pr, c), lambda p, ck: (0, p, 0))],
        out_specs=pl.BlockSpec((pr, c), lambda p, ck: (ck[0] * n + p, 0)))
    return pl.pallas_call(body, grid_spec=spec, out_shape=SDS((2 * h, c), F32),
                          compiler_params=_params(("arbitrary",)), name=name)(ck, pbf, slots)


def _join_halves(sums):
    plan = _piece_plan([s.shape for s in sums])
    n_sem = sum(n for _, n, _ in plan)

    def copies(r_refs, o_refs, send, recv, local):
        x, y, c = _me()
        sends, arrivals = [], []
        for t, (h, n, pr) in enumerate(plan):
            for p in range(n):
                rows = pl.ds(c * h + p * pr, pr)
                rows_sib = pl.ds((1 - c) * h + p * pr, pr)
                s = len(sends)
                sends.append(_remote(send, recv, s, r_refs[t].at[rows], o_refs[t].at[rows], (x, y, 1 - c)))
                arrivals.append(_remote(send, recv, s, r_refs[t].at[rows_sib], o_refs[t].at[rows_sib], (x, y, 1 - c)))
        return sends, arrivals, []

    return _Phase(sums, [SDS(s.shape, F32) for s in sums], n_sem, 0, copies, {t: t for t in range(len(sums))})


def _gather_small(s, name):
    def body(s_ref, o_ref, send, recv, local):
        x, y, c = _me()
        me = 4 * x + 2 * y + c
        keep = pltpu.make_async_copy(s_ref, o_ref.at[me], local)
        keep.start()
        sends = []
        for r in range(1, 8):
            fx, fy, fc = (r >> 2) & 1, (r >> 1) & 1, r & 1
            to = (x ^ fx, y ^ fy, c ^ fc)
            sends.append(pltpu.make_async_remote_copy(
                src_ref=s_ref, dst_ref=o_ref.at[me], send_sem=send.at[r - 1], recv_sem=recv.at[r - 1],
                device_id=to, device_id_type=MESH))
        for cp in sends:
            cp.start()
        for r in range(1, 8):
            fx, fy, fc = (r >> 2) & 1, (r >> 1) & 1, r & 1
            src = 4 * (x ^ fx) + 2 * (y ^ fy) + (c ^ fc)
            pltpu.make_async_remote_copy(
                src_ref=s_ref, dst_ref=o_ref.at[src], send_sem=send.at[r - 1], recv_sem=recv.at[r - 1],
                device_id=(x ^ fx, y ^ fy, c ^ fc), device_id_type=MESH).wait_recv()
        for cp in sends:
            cp.wait_send()
        keep.wait()

    return pl.pallas_call(
        body, in_specs=[ANY], out_specs=ANY, out_shape=SDS((8, SMALL_ROWS, LANES), F32),
        scratch_shapes=[pltpu.SemaphoreType.DMA((7,)), pltpu.SemaphoreType.DMA((7,)), pltpu.SemaphoreType.DMA],
        name=name)(s)


def _adam_math(w, g, m, v):
    nm = ADAM_B1 * m + (1.0 - ADAM_B1) * g
    nv = ADAM_B2 * v + (1.0 - ADAM_B2) * (g * g)
    m_hat = nm / (1.0 - ADAM_B1 ** ADAM_STEP)
    v_hat = nv / (1.0 - ADAM_B2 ** ADAM_STEP)
    return -ADAM_LR * (m_hat / (jnp.sqrt(v_hat) + ADAM_EPS) + ADAM_WD * w), nm, nv


def _adamw(w, g, m, v, name):
    _, r, c = w.shape
    t = max(d for d in range(8, r + 1, 8) if r % d == 0 and 16 * d * c * 4 <= VMEM_LIMIT - (8 << 20))

    def body(w_ref, g_ref, m_ref, v_ref, go_ref, d_ref, nm_ref, nv_ref):
        g_ = g_ref[...]
        d, nm, nv = _adam_math(w_ref[...], g_, m_ref[...], v_ref[...])
        go_ref[...] = g_
        d_ref[...] = d
        nm_ref[...] = nm
        nv_ref[...] = nv

    lead = pl.BlockSpec((None, t, c), lambda i: (0, i, 0))
    return pl.pallas_call(body, grid=(r // t,), in_specs=[lead, _rows(t, c), lead, lead], out_specs=[lead] * 4,
                          out_shape=[SDS((1, r, c), F32)] * 4, compiler_params=_params(("parallel",)),
                          name=name)(w, g, m, v)


def _small_layout():
    out, r0 = {}, 0
    for n in SMALL:
        size = int(np.prod(SMALL_SHAPE[n]))
        nr = -(-size // LANES)
        out[n] = (r0, nr)
        r0 += nr
    assert r0 <= SMALL_ROWS
    return out, r0


def _pack_small(grads, loss_tile, name):
    layout, used = _small_layout()

    def body(*refs):
        o_ref = refs[-1]
        o_ref[used:used + 1, :] = refs[-2][0:1, :]
        for n, ref in zip(SMALL, refs[:-2]):
            r0, nr = layout[n]
            if n == "w_spatial":
                for g in range(GM_GROUPS):
                    o_ref[r0 + g * GM_CHUNK:r0 + (g + 1) * GM_CHUNK, :] = ref[g]
            elif n == "b_spatial":
                o_ref[r0:r0 + nr, :] = ref[...]
            else:
                for i in range(nr):
                    o_ref[r0 + i:r0 + i + 1, :] = ref[:, i * LANES:(i + 1) * LANES]
        if used + 1 < SMALL_ROWS:
            o_ref[used + 1:SMALL_ROWS, :] = jnp.zeros((SMALL_ROWS - used - 1, LANES), F32)

    return pl.pallas_call(body, out_shape=SDS((SMALL_ROWS, LANES), F32), name=name)(*grads, loss_tile)


def _adamw_small(gathered, ws, ms, vs, name):
    layout, used = _small_layout()
    n_t = len(SMALL)

    def body(*refs):
        g_ref = refs[0]
        w_refs, m_refs, v_refs = refs[1:1 + n_t], refs[1 + n_t:1 + 2 * n_t], refs[1 + 2 * n_t:1 + 3 * n_t]
        outs = refs[1 + 3 * n_t:1 + 7 * n_t]
        acc = refs[-1]
        total = g_ref[0]
        for j in range(1, 8):
            total = total + g_ref[j]
        acc[...] = total
        refs[1 + 7 * n_t][...] = acc[used:used + 1, :]
        for t, n in enumerate(SMALL):
            r0, nr = layout[n]
            o_refs = [outs[t], outs[n_t + t], outs[2 * n_t + t], outs[3 * n_t + t]]
            if n == "w_spatial":
                views = [((0, g), slice(r0 + g * GM_CHUNK, r0 + (g + 1) * GM_CHUNK), slice(None))
                         for g in range(GM_GROUPS)]
            elif n == "b_spatial":
                views = [((0,), slice(r0, r0 + nr), slice(None))]
            else:
                width = SMALL_SHAPE[n][1]
                views = [((slice(None), slice(i * LANES, min((i + 1) * LANES, width))), slice(r0 + i, r0 + i + 1),
                          slice(0, min(LANES, width - i * LANES))) for i in range(nr)]
            for idx, rows, lanes in views:
                g = acc[rows, lanes]
                d, nm, nv = _adam_math(w_refs[t][idx], g, m_refs[t][idx], v_refs[t][idx])
                for ref, val in zip(o_refs, (g, d, nm, nv)):
                    ref[idx] = val

    shapes = [SDS(SMALL_SHAPE[n], F32) for n in SMALL]
    return pl.pallas_call(body, out_shape=shapes * 4 + [SDS((1, LANES), F32)],
                          scratch_shapes=[pltpu.VMEM((SMALL_ROWS, LANES), F32)], name=name)(gathered, *ws, *ms, *vs)


def _win_layout(w_in):
    pad = jnp.zeros((w_in.shape[0], LANES - MLA_ROPE), w_in.dtype)
    u, v, cq = w_in[:, 0:512], w_in[:, 512:1024], w_in[:, 1024:1408]
    ckv, kpe, qm, zg = w_in[:, 1408:1664], w_in[:, 1664:1728], w_in[:, 1728:2240], w_in[:, 2240:5312]
    return jnp.concatenate([zg, u, v, qm, cq, kpe, pad, ckv], axis=1)


def _win_unlayout(g):
    zg, u, v, qm = g[:, ZG:ZG + 3072], g[:, ZU:ZU + 512], g[:, ZV:ZV + 512], g[:, QM:QM + 512]
    cq, kpe, ckv = g[:, CQ:CQ + 384], g[:, KPE:KPE + MLA_ROPE], g[:, CKV:CKV + 256]
    return jnp.concatenate([u, v, cq, ckv, kpe, qm, zg], axis=1)


def _wq_layout(w_uq):
    w = w_uq.reshape(Q_LORA, MLA_HEADS, MLA_NOPE + MLA_ROPE)
    nope = w[:, :, :MLA_NOPE].reshape(Q_LORA, MLA_HEADS * MLA_NOPE)
    pe = jnp.pad(w[:, :, MLA_NOPE:], ((0, 0), (0, 0), (0, LANES - MLA_ROPE))).reshape(Q_LORA, MLA_HEADS * LANES)
    return jnp.concatenate([nope, pe], axis=1)


def _wq_unlayout(g):
    nope = g[:, :1024].reshape(Q_LORA, MLA_HEADS, MLA_NOPE)
    pe = g[:, 1024:].reshape(Q_LORA, MLA_HEADS, LANES)[:, :, :MLA_ROPE]
    return jnp.concatenate([nope, pe], axis=2).reshape(Q_LORA, MLA_HEADS * (MLA_NOPE + MLA_ROPE))


def _wkv_layout(w_ukv):
    w = w_ukv.reshape(KV_LORA, MLA_HEADS, MLA_NOPE + MLA_V)
    return jnp.concatenate([w[:, :, :MLA_NOPE].reshape(KV_LORA, 1024), w[:, :, MLA_NOPE:].reshape(KV_LORA, 1024)],
                           axis=1)


def _wkv_unlayout(g):
    kn = g[:, :1024].reshape(KV_LORA, MLA_HEADS, MLA_NOPE)
    v = g[:, 1024:].reshape(KV_LORA, MLA_HEADS, MLA_V)
    return jnp.concatenate([kn, v], axis=2).reshape(KV_LORA, MLA_HEADS * (MLA_NOPE + MLA_V))


def _owner_major(g, name):
    r, c = _shard_shape(name)
    return g.reshape(r, N_CHIPS, c).transpose(1, 0, 2) if name in COL_SHARDED else g.reshape(N_CHIPS, r, c)


def _pad_lanes(g):
    return jnp.pad(g, ((0, 0), (0, LANES - g.shape[1])))


def kernel(x, mem, positions, g_mix, w_in, g_cq, w_uq, g_ckv, w_ukv, g_q_nope, g_q_pe, g_k_nope, g_k_pe, g_gm_ln, b_gm_ln, w_spatial, b_spatial, g_mem, w_mem_kv, g_mq, g_mk, w_o_gm, w_o_mla, w_o_mem, w_out, g_ffn, w_ff1, w_ff2, loss_target, m_g_mix, m_w_in, m_g_cq, m_w_uq, m_g_ckv, m_w_ukv, m_g_q_nope, m_g_q_pe, m_g_k_nope, m_g_k_pe, m_g_gm_ln, m_b_gm_ln, m_w_spatial, m_b_spatial, m_g_mem, m_w_mem_kv, m_g_mq, m_g_mk, m_w_o_gm, m_w_o_mla, m_w_o_mem, m_w_out, m_g_ffn, m_w_ff1, m_w_ff2, v_g_mix, v_w_in, v_g_cq, v_w_uq, v_g_ckv, v_w_ukv, v_g_q_nope, v_g_q_pe, v_g_k_nope, v_g_k_pe, v_g_gm_ln, v_b_gm_ln, v_w_spatial, v_b_spatial, v_g_mem, v_w_mem_kv, v_g_mq, v_g_mk, v_w_o_gm, v_w_o_mla, v_w_o_mem, v_w_out, v_g_ffn, v_w_ff1, v_w_ff2):
    given = dict(locals())
    wts = {n: given[n] for n in WEIGHTS}
    mom = {n: given["m_" + n] for n in WEIGHTS}
    var = {n: given["v_" + n] for n in WEIGHTS}
    batch, seq, _ = x.shape
    n_tok = batch * seq

    def natural(n, g):
        r, c = _shard_shape(n)
        return g.transpose(1, 0, 2).reshape(r, N_CHIPS * c) if n in COL_SHARDED else g.reshape(N_CHIPS * r, c)

    def far(names):
        return _gather_far([wts[n][0].astype(BF) for n in names])

    x2 = x.reshape(n_tok, D_MODEL)
    tgt2 = loss_target.reshape(n_tok, D_MODEL)
    mem2 = mem.reshape(batch * MEM_LEN, D_MODEL)
    pos_f = positions.reshape(n_tok, 1).astype(F32)

    inv = ROPE_BASE ** (-jnp.arange(0, MLA_ROPE, 2, dtype=F32) / MLA_ROPE)
    zeros64 = jnp.zeros((LANES - MLA_ROPE,), F32)
    inv_full = jnp.concatenate([inv, inv, zeros64]).reshape(1, LANES)
    half = MLA_ROPE // 2
    cmask = jnp.concatenate([jnp.ones((MLA_ROPE,), F32), zeros64]).reshape(1, LANES)
    smask = jnp.concatenate([-jnp.ones((half,), F32), jnp.ones((half,), F32), zeros64]).reshape(1, LANES)

    prep_gains = [g_cq, g_ckv, g_q_nope, _pad_lanes(g_q_pe), g_k_nope, _pad_lanes(g_k_pe)]
    ws = w_spatial[0]
    bcols = [b_spatial[0, g].reshape(GM_CHUNK, 1) for g in range(GM_GROUPS)]

    h1, in_far = _rms_fwd(x2, g_mix, "rms_mix", comm=far(EARLY[:1]))
    (cos_f, sin_s), early = _rope_tables(pos_f, inv_full, cmask, smask, "rope_tables",
                                         comm=_together(_gather_near(in_far), far(EARLY[1:])))
    memn, rest = _rms_fwd(mem2, g_mem, "rms_mem", comm=_gather_near(early[1:]))
    full = {n: natural(n, g) for n, g in zip(EARLY, list(early[:1]) + list(rest))}
    win = _win_layout(full["w_in"])
    wq = _wq_layout(full["w_uq"])
    wkv = _wkv_layout(full["w_ukv"])
    z, proj_far = _mm(h1, win, out_dtypes=(BF,), name="mm_in", comm=far(LATE_PROJ))
    gm = _gm_fwd(z, g_gm_ln, b_gm_ln, ws, bcols, "gm_fwd")
    qc, kc, vc = _prep_fwd(z, cos_f, sin_s, prep_gains, wq, wkv, "prep_fwd")
    (o_mla, lse), ff_far = _mla_fwd(qc, kc, vc, batch, seq, "mla_fwd", comm=far(LATE_FF))
    kvm, proj = _mm(memn, full["w_mem_kv"], name="mm_memkv", comm=_gather_near(proj_far))
    o_mem, ff = _mem_fwd(z, kvm, g_mq, g_mk, batch, seq, "mem_fwd", comm=_gather_near(ff_far))
    full.update({n: natural(n, g) for n, g in zip(LATE_PROJ + LATE_FF, list(proj) + list(ff))})
    y_gm = _mm(gm, full["w_o_gm"], out_dtypes=(BF,), name="mm_o_gm")
    y_mla = _mm(o_mla, full["w_o_mla"], out_dtypes=(BF,), name="mm_o_mla")
    y_mem = _mm(o_mem, full["w_o_mem"], out_dtypes=(BF,), name="mm_o_mem")
    merged = _merge_fwd(z, y_gm, y_mla, y_mem, "merge_fwd")
    x1, h2 = _mm(merged, full["w_out"], ins=(x2,), row_ins=(g_ffn,), epilogue=_residual_rms, out_dtypes=(F32, BF),
                 name="mm_out")
    a_ff, r_ff = _mm(h2, full["w_ff1"], epilogue=_relu2, out_dtypes=(BF, BF), name="mm_ff1")
    dy, dyb, loss_tile = _mm(r_ff, full["w_ff2"], ins=(x1, tgt2), epilogue=_loss_tail, out_dtypes=(F32, BF),
                             total=True, name="mm_ff2")

    gw = {}
    da = _mm(dyb, full["w_ff2"], tb=True, ins=(a_ff,), epilogue=_relu2_bwd, out_dtypes=(BF,), name="mm_d_a")
    gw["w_ff2"] = _owner_major(_mm(r_ff, dyb, ta=True, name="mm_dw_ff2"), "w_ff2")
    gw["w_ff1"] = _mm(h2, da, ta=True, owner_cols=D_FF // N_CHIPS, name="mm_dw_ff1")
    dx1, dx1b, dg_ffn = _mm(da, full["w_ff1"], tb=True, ins=(x1, dy), row_ins=(g_ffn,), epilogue=_rms_bwd_tail,
                            out_dtypes=(F32, BF), total=(1, D_MODEL), name="mm_d_h2")
    dmerged = _mm(dx1b, full["w_out"], tb=True, name="mm_d_merged")
    gw["w_out"] = _owner_major(_mm(merged, dx1b, ta=True, name="mm_dw_out"), "w_out")
    dz, dy_gm, dy_mla, dy_mem = _merge_bwd(z, y_gm, y_mla, y_mem, dmerged, "merge_bwd")
    dgm = _mm(dy_gm, full["w_o_gm"], tb=True, name="mm_d_gm")
    gw["w_o_gm"] = _owner_major(_mm(gm, dy_gm, ta=True, name="mm_dw_o_gm"), "w_o_gm")
    do_mla = _mm(dy_mla, full["w_o_mla"], tb=True, name="mm_d_omla")
    gw["w_o_mla"] = _owner_major(_mm(o_mla, dy_mla, ta=True, name="mm_dw_o_mla"), "w_o_mla")
    do_mem = _mm(dy_mem, full["w_o_mem"], tb=True, name="mm_d_omem")
    gw["w_o_mem"] = _owner_major(_mm(o_mem, dy_mem, ta=True, name="mm_dw_o_mem"), "w_o_mem")
    ck = jnp.stack([lax.axis_index("c"), 2 * lax.axis_index("x") + lax.axis_index("y")]).astype(jnp.int32)

    def pair_sums(names, theirs):
        return [_pair_add(ck, gw[n], t, "pair_add_" + n) for n, t in zip(names, theirs)]

    def chip_sums(names, pairs, slots):
        return [_sum_chips(ck, p, s, "sum_chips_" + n) for n, p, s in zip(names, pairs, slots)]

    (dz, dg_ln, db_ln, dws, *dbcols), theirs = _gm_bwd(z, g_gm_ln, b_gm_ln, ws, bcols, dgm, dz, "gm_bwd",
                                                      comm=_pair_exchange([gw[n] for n in LATE]))
    pairs = pair_sums(LATE, theirs)
    (dq, dk, dv), slots = _mla_bwd(qc, kc, vc, o_mla, lse, do_mla, batch, seq, "mla_bwd",
                                   comm=_scatter_partials(pairs))
    sums = chip_sums(LATE, pairs, slots)
    (dz, dg_cq, dg_ckv, dg_qn, dg_qp, dg_kn, dg_kp, dwq, dwkv), reduced_late = _prep_bwd(
        z, cos_f, sin_s, prep_gains, wq, wkv, dq, dk, dv, dz, "prep_bwd", comm=_join_halves(sums))
    dz, dkvm, dg_mq, dg_mk = _mem_bwd(z, kvm, g_mq, g_mk, do_mem, dz, batch, seq, "mem_bwd")
    dmemn = _mm(dkvm, full["w_mem_kv"], tb=True, name="mm_d_memn")
    gw["w_mem_kv"] = _owner_major(_mm(memn, dkvm, ta=True, name="mm_dw_memkv"), "w_mem_kv")
    _, _, dg_mem = _rms_bwd(mem2, g_mem, dmemn, None, "rms_mem_bwd")
    gw["w_in"] = _owner_major(_win_unlayout(_mm(h1, dz, ta=True, name="mm_dw_in")), "w_in")
    gw["w_uq"] = _owner_major(_wq_unlayout(dwq), "w_uq")
    gw["w_ukv"] = _owner_major(_wkv_unlayout(dwkv), "w_ukv")
    dh1, theirs = _mm(dz, win, tb=True, name="mm_d_h1_top", rows=(0, 2), comm=_pair_exchange([gw[n] for n in EARLY]))
    pairs = pair_sums(EARLY, theirs)
    dh1, slots = _mm(dz, win, tb=True, name="mm_d_h1_bottom", rows=(1, 2), into=dh1,
                     comm=_scatter_partials(pairs))
    grad_x, _, dg_mix = _rms_bwd(x2, g_mix, dh1, dx1, "rms_mix_bwd")
    reduced_early = _run_phase(_join_halves(chip_sums(EARLY, pairs, slots)), "join_early")
    reduced = dict(zip(LATE + EARLY, list(reduced_late) + list(reduced_early)))

    def swapped(a):
        return jnp.swapaxes(a, -1, -2)

    results = {n: _adamw(wts[n], reduced[n], mom[n], var[n], "adamw_" + n) for n in BIG if n != "w_in"}
    results["w_in"] = [swapped(r) for r in _adamw(swapped(w_in), swapped(reduced["w_in"]), swapped(m_w_in),
                                                  swapped(v_w_in), "adamw_w_in")]

    small_g = {"g_mix": dg_mix, "g_cq": dg_cq, "g_ckv": dg_ckv, "g_q_nope": dg_qn, "g_q_pe": dg_qp,
               "g_k_nope": dg_kn, "g_k_pe": dg_kp, "g_gm_ln": dg_ln, "b_gm_ln": db_ln, "w_spatial": dws,
               "b_spatial": jnp.concatenate(dbcols, axis=1).T, "g_mem": dg_mem, "g_mq": dg_mq, "g_mk": dg_mk,
               "g_ffn": dg_ffn}
    packed = _pack_small([small_g[n] for n in SMALL], loss_tile, "pack_small")
    small_out = _adamw_small(_gather_small(packed, "gather_small"), [wts[n] for n in SMALL],
                             [mom[n] for n in SMALL], [var[n] for n in SMALL], "adamw_small")
    for t, n in enumerate(SMALL):
        results[n] = [small_out[j * len(SMALL) + t] for j in range(4)]

    loss = small_out[4 * len(SMALL)][0, 0]
    grad_x = grad_x.reshape(batch, seq, D_MODEL)
    return (loss, grad_x, *[results[n][0] for n in WEIGHTS], *[results[n][1] for n in WEIGHTS],
            *[results[n][2] for n in WEIGHTS], *[results[n][3] for n in WEIGHTS])
```

```python
import functools
import math

import numpy as np
import jax
import jax.numpy as jnp
from jax import lax
from jax.experimental import pallas as pl
from jax.experimental.pallas import tpu as pltpu

F32 = jnp.float32
BF = jnp.bfloat16
SDS = jax.ShapeDtypeStruct
MESH = pl.DeviceIdType.MESH

D_MODEL = 1024
MEM_LEN = 256
MEM_HEADS = 4
HEAD_DIM = 128
GM_WIDTH = 512
GM_CHUNK = 128
GM_GROUPS = 4
MLA_HEADS = 8
MLA_NOPE = 128
MLA_ROPE = 64
MLA_V = 128
Q_LORA = 384
KV_LORA = 256
ROPE_BASE = 10000.0
D_FF = 4096
EPS = 1e-6
W_IN_COLS = 5312
ADAM_LR, ADAM_B1, ADAM_B2, ADAM_EPS, ADAM_WD, ADAM_STEP = 0.001, 0.9, 0.999, 1e-08, 0.01, 10

ZG, ZU, ZV, QM, CQ, KPE, CKV = 0, 3072, 3584, 4096, 4608, 4992, 5120
Z_COLS = 5376
LANES = 128
ROW_TILE = 512
ATT_TILE = 1024
ATT_HEADS = 2
ATT_HEADS_FWD = 4
VMEM_LIMIT = 60 * 1024 * 1024

N_CHIPS = 4
PIECE_ROWS = 256
SMALL_ROWS = 560

BIG = ["w_in", "w_uq", "w_ukv", "w_mem_kv", "w_o_gm", "w_o_mla", "w_o_mem", "w_out", "w_ff1", "w_ff2"]
BIG_SHAPE = {"w_in": (1024, 5312), "w_uq": (384, 1536), "w_ukv": (256, 2048), "w_mem_kv": (1024, 1024),
             "w_o_gm": (512, 1024), "w_o_mla": (1024, 1024), "w_o_mem": (512, 1024), "w_out": (1024, 1024),
             "w_ff1": (1024, 4096), "w_ff2": (4096, 1024)}
COL_SHARDED = {"w_in", "w_uq", "w_ukv", "w_o_gm", "w_o_mem", "w_ff1"}
EARLY = ["w_in", "w_uq", "w_ukv", "w_mem_kv"]
LATE_PROJ = ["w_o_gm", "w_o_mla", "w_o_mem", "w_out"]
LATE_FF = ["w_ff1", "w_ff2"]
LATE = LATE_PROJ + LATE_FF
SMALL = ["w_spatial", "b_spatial", "g_mix", "g_cq", "g_ckv", "g_q_nope", "g_q_pe", "g_k_nope", "g_k_pe", "g_gm_ln",
         "b_gm_ln", "g_mem", "g_mq", "g_mk", "g_ffn"]
SMALL_SHAPE = {"g_mix": (1, 1024), "g_cq": (1, 384), "g_ckv": (1, 256), "g_q_nope": (1, 128), "g_q_pe": (1, 64),
               "g_k_nope": (1, 128), "g_k_pe": (1, 64), "g_gm_ln": (1, 512), "b_gm_ln": (1, 512),
               "w_spatial": (1, 4, 128, 128), "b_spatial": (1, 4, 128), "g_mem": (1, 1024), "g_mq": (1, 128),
               "g_mk": (1, 128), "g_ffn": (1, 1024)}
WEIGHTS = ['g_mix', 'w_in', 'g_cq', 'w_uq', 'g_ckv', 'w_ukv', 'g_q_nope', 'g_q_pe', 'g_k_nope', 'g_k_pe',
           'g_gm_ln', 'b_gm_ln', 'w_spatial', 'b_spatial', 'g_mem', 'w_mem_kv', 'g_mq', 'g_mk', 'w_o_gm',
           'w_o_mla', 'w_o_mem', 'w_out', 'g_ffn', 'w_ff1', 'w_ff2']


def _params(sem=None):
    return pltpu.CompilerParams(vmem_limit_bytes=VMEM_LIMIT, dimension_semantics=sem)


def _pick(n, prefs):
    for p in prefs:
        if n % p == 0:
            return p
    return n


def _full(shape):
    nd = len(shape)
    return pl.BlockSpec(shape, lambda *_: (0,) * nd)


def _rows(t, w, blk=0):
    return pl.BlockSpec((t, w), lambda i: (i, blk))


def _acc(ref, val, first):
    @pl.when(first)
    def _():
        ref[...] = val

    @pl.when(jnp.logical_not(first))
    def _():
        ref[...] += val


ANY = pl.BlockSpec(memory_space=pl.ANY)


class _Phase:
    def __init__(self, operands, out_shapes, n_sem, n_local, copies, aliases=None):
        self.operands, self.out_shapes, self.aliases = list(operands), list(out_shapes), dict(aliases or {})
        self.n_sem, self.n_local, self.copies = n_sem, max(n_local, 1), copies

    def sem_shapes(self):
        return [pltpu.SemaphoreType.DMA((self.n_sem,)), pltpu.SemaphoreType.DMA((self.n_sem,)),
                pltpu.SemaphoreType.DMA((self.n_local,))]

    def start(self, ins, outs, send, recv, local):
        sends, _, locals_ = self.copies(ins, outs, send, recv, local)
        for cp in locals_ + sends:
            cp.start()

    def finish(self, ins, outs, send, recv, local):
        sends, arrivals, locals_ = self.copies(ins, outs, send, recv, local)
        for cp in arrivals:
            cp.wait_recv()
        for cp in sends:
            cp.wait_send()
        for cp in locals_:
            cp.wait()


class _Shifted:
    def __init__(self, ref, base):
        self.ref, self.base = ref, base

    @property
    def at(self):
        return self

    def __getitem__(self, i):
        return self.ref.at[i + self.base]


def _together(first, second):
    n_in, n_out = len(first.operands), len(first.out_shapes)

    def copies(ins, outs, send, recv, local):
        a = first.copies(ins[:n_in], outs[:n_out], send, recv, local)
        b = second.copies(ins[n_in:], outs[n_out:], _Shifted(send, first.n_sem), _Shifted(recv, first.n_sem),
                          _Shifted(local, first.n_local))
        return a[0] + b[0], a[1] + b[1], a[2] + b[2]

    aliases = {**first.aliases, **{n_in + i: n_out + j for i, j in second.aliases.items()}}
    return _Phase(first.operands + second.operands, first.out_shapes + second.out_shapes, first.n_sem + second.n_sem,
                  first.n_local + second.n_local, copies, aliases)


def _run_phase(phase, name):
    n_in = len(phase.operands)

    def body(*refs):
        ins, outs, sems = refs[:n_in], refs[n_in:n_in + len(phase.out_shapes)], refs[n_in + len(phase.out_shapes):]
        phase.start(ins, outs, *sems)
        phase.finish(ins, outs, *sems)

    return pl.pallas_call(body, in_specs=[ANY] * n_in, out_specs=[ANY] * len(phase.out_shapes),
                          out_shape=phase.out_shapes, scratch_shapes=phase.sem_shapes(),
                          input_output_aliases=phase.aliases, name=name)(*phase.operands)


def _pcall(body, *, grid, in_specs, out_specs, out_shape, scratch_shapes=(), sem=None, name, comm=None, aliases=None):
    single = not isinstance(out_shape, (list, tuple))
    o_specs = [out_specs] if single else list(out_specs)
    o_shape = [out_shape] if single else list(out_shape)
    aliases = dict(aliases or {})
    if comm is None:
        call = pl.pallas_call(body, grid=grid, in_specs=list(in_specs), out_specs=o_specs, out_shape=o_shape,
                              scratch_shapes=list(scratch_shapes), input_output_aliases=aliases,
                              compiler_params=_params(sem), name=name)

        def run_plain(*args):
            res = call(*args)
            return res[0] if single else res

        return run_plain

    n_in, n_out, n_scr = len(in_specs), len(o_specs), len(scratch_shapes)
    nc_in, nc_out = len(comm.operands), len(comm.out_shapes)

    def wrapped(*refs):
        ins, cins = refs[:n_in], refs[n_in:n_in + nc_in]
        o0 = n_in + nc_in
        outs, couts = refs[o0:o0 + n_out], refs[o0 + n_out:o0 + n_out + nc_out]
        s0 = o0 + n_out + nc_out
        scr, csem = refs[s0:s0 + n_scr], refs[s0 + n_scr:]
        ids = [pl.program_id(d) for d in range(len(grid))]
        first = functools.reduce(jnp.logical_and, [i == 0 for i in ids])
        last = functools.reduce(jnp.logical_and, [i == g - 1 for i, g in zip(ids, grid)])

        @pl.when(first)
        def _():
            comm.start(cins, couts, *csem)

        body(*ins, *outs, *scr)

        @pl.when(last)
        def _():
            comm.finish(cins, couts, *csem)

    call = pl.pallas_call(
        wrapped, grid=grid, in_specs=list(in_specs) + [ANY] * nc_in, out_specs=o_specs + [ANY] * nc_out,
        out_shape=o_shape + comm.out_shapes, scratch_shapes=list(scratch_shapes) + comm.sem_shapes(),
        input_output_aliases={**aliases, **{n_in + i: n_out + j for i, j in comm.aliases.items()}},
        compiler_params=_params(("arbitrary",) * len(grid)), name=name)

    def run_carrying(*args):
        res = call(*args, *comm.operands)
        return (res[0] if single else res[:n_out]), res[n_out:]

    return run_carrying


def _dn(a, b, ca, cb):
    return lax.dot_general(a.astype(BF), b.astype(BF), (((ca,), (cb,)), ((), ())), preferred_element_type=F32)


@jax.custom_vjp
def _mm_nn(a, b):
    return _dn(a, b, 1, 0)


def _mm_nn_fwd(a, b):
    return _dn(a, b, 1, 0), (a.astype(BF), b.astype(BF))


def _mm_nn_bwd(res, ct):
    a, b = res
    return _dn(ct, b, 1, 1), _dn(a, ct, 0, 0)


_mm_nn.defvjp(_mm_nn_fwd, _mm_nn_bwd)


@jax.custom_vjp
def _mm_nt(a, b):
    return _dn(a, b, 1, 1)


def _mm_nt_fwd(a, b):
    return _dn(a, b, 1, 1), (a.astype(BF), b.astype(BF))


def _mm_nt_bwd(res, ct):
    a, b = res
    return _dn(ct, b, 1, 0), _dn(ct, a, 0, 0)


_mm_nt.defvjp(_mm_nt_fwd, _mm_nt_bwd)


def _rmsn(x, g, n):
    ms = jnp.sum(x * x, axis=-1, keepdims=True) * (1.0 / n)
    return x * lax.rsqrt(ms + EPS) * g


def _layernorm(x, g, b):
    mu = jnp.mean(x, axis=-1, keepdims=True)
    xc = x - mu
    y = xc * lax.rsqrt(jnp.mean(xc * xc, axis=-1, keepdims=True) + EPS)
    return y * g + b


def _swap_lanes(x):
    half = MLA_ROPE // 2
    lane = lax.broadcasted_iota(jnp.int32, x.shape, 1)
    return jnp.where(lane < half, pltpu.roll(x, LANES - half, axis=1),
                     jnp.where(lane < MLA_ROPE, pltpu.roll(x, half, axis=1), 0.0))


@jax.custom_vjp
def _swap_halves(x):
    return _swap_lanes(x)


_swap_halves.defvjp(lambda x: (_swap_lanes(x), None), lambda _, ct: (_swap_lanes(ct),))


def _rope(x, cos_f, sin_s):
    return x * cos_f + _swap_halves(x) * sin_s


def _lane_blocks(x):
    return tuple(x[:, i * LANES:(i + 1) * LANES] for i in range(x.shape[1] // LANES))


@jax.custom_vjp
def _split_lanes(x):
    return _lane_blocks(x)


_split_lanes.defvjp(lambda x: (_lane_blocks(x), None), lambda _, cts: (jnp.concatenate(cts, axis=1),))


def _softmax(s):
    m = lax.stop_gradient(jnp.max(s, axis=-1, keepdims=True))
    p = jnp.exp(s - m)
    return p / jnp.sum(p, axis=-1, keepdims=True)


def _mm(a, b, *, ta=False, tb=False, ins=(), row_ins=(), epilogue=None, out_dtypes=(F32,), owner_cols=None,
        total=False, name, comm=None, rows=None, into=None):
    if ta:
        k_dim, m = a.shape
    else:
        m, k_dim = a.shape
    if tb:
        n, kb = b.shape
    else:
        kb, n = b.shape
    assert k_dim == kb, (a.shape, b.shape, ta, tb)
    part, n_parts = rows if rows is not None else (0, 1)
    tm = _pick(m // n_parts, (1024, 512, 256, 128))
    tn = _pick(n if owner_cols is None else owner_cols, (1024, 768, 512, 384, 256, 128))
    tk = _pick(k_dim, (2048, 1024, 768, 512, 256, 128))
    nk = k_dim // tk
    m_steps = m // tm // n_parts
    off = part * m_steps
    ca = 0 if ta else 1
    cb = 1 if tb else 0
    n_in = len(ins) + len(row_ins)
    n_out = len(out_dtypes)
    n_pass = 0 if into is None else 1
    total_shape = total if isinstance(total, tuple) else (8, LANES)
    assert not (isinstance(total, tuple) and n != tn), "a per-column total needs the whole width in one tile"

    def finish(r, in_refs, out_refs, first_tile):
        vals = epilogue(r, *[ref[...].astype(F32) for ref in in_refs]) if epilogue is not None else (r,)
        for ref, val, dt in zip(out_refs, vals, out_dtypes):
            ref[...] = val.astype(dt)
        if total:
            _acc(out_refs[n_out], vals[n_out], first_tile)

    def body(*refs):
        a_ref, b_ref = refs[:2]
        in_refs = refs[2:2 + n_in]
        o0 = 2 + n_in + n_pass
        out_refs = refs[o0:o0 + n_out + int(bool(total))]
        first_tile = jnp.logical_and(pl.program_id(0) == 0, pl.program_id(1) == 0)
        part = _dn(a_ref[...], b_ref[...], ca, cb)
        if nk == 1:
            finish(part, in_refs, out_refs, first_tile)
            return
        acc = refs[-1]
        k = pl.program_id(2)
        _acc(acc, part, k == 0)

        @pl.when(k == nk - 1)
        def _():
            finish(acc[...], in_refs, out_refs, first_tile)

    a_spec = (pl.BlockSpec((tk, tm), lambda i, j, k: (k, i + off)) if ta
              else pl.BlockSpec((tm, tk), lambda i, j, k: (i + off, k)))
    b_spec = pl.BlockSpec((tn, tk), lambda i, j, k: (j, k)) if tb else pl.BlockSpec((tk, tn), lambda i, j, k: (k, j))
    t_spec = pl.BlockSpec((tm, tn), lambda i, j, k: (i + off, j))
    if owner_cols is None:
        o_spec, o_shape = t_spec, (m, n)
    else:
        per = owner_cols // tn
        o_spec = pl.BlockSpec((None, tm, tn), lambda i, j, k: (j // per, i + off, j % per))
        o_shape = (n // owner_cols, m, owner_cols)
    o_specs = [o_spec] * n_out + ([pl.BlockSpec(total_shape, lambda i, j, k: (0, 0))] if total else [])
    o_shapes = [SDS(o_shape, dt) for dt in out_dtypes] + ([SDS(total_shape, F32)] if total else [])
    row_spec = pl.BlockSpec((1, tn), lambda i, j, k: (0, j))
    in_specs = [a_spec, b_spec] + [t_spec] * len(ins) + [row_spec] * len(row_ins) + [ANY] * n_pass
    args = [a, b, *ins, *row_ins] + ([into] if n_pass else [])
    run = _pcall(body, grid=(m_steps, n // tn, nk), in_specs=in_specs, out_specs=o_specs, out_shape=o_shapes,
                 scratch_shapes=[pltpu.VMEM((tm, tn), F32)] if nk > 1 else [],
                 sem=("arbitrary",) * 3 if total else ("parallel", "parallel", "arbitrary"), name=name, comm=comm,
                 aliases={len(in_specs) - 1: 0} if n_pass else None)
    if comm is None:
        outs = run(*args)
        return outs[0] if len(outs) == 1 else outs
    outs, exchanged = run(*args)
    return (outs[0] if len(outs) == 1 else outs), exchanged


def _add_to(r, x):
    return (r + x,)


def _residual_rms(r, x, g):
    x1 = r + x
    return x1, _rmsn(x1, g, D_MODEL)


def _rms_bwd_tail(dh, x, res, g):
    _, vjp = jax.vjp(lambda xx, gg: _rmsn(xx, gg, D_MODEL), x, g)
    dx, dg = vjp(dh)
    dx = dx + res
    return dx, dx, dg


def _relu2(r):
    p = jnp.maximum(r, 0.0)
    return r, p * p


def _relu2_bwd(dr, a):
    return (dr * (2.0 * jnp.maximum(a, 0.0)),)


def _loss_tail(r, x1, tgt):
    e = (r + x1) - tgt
    dy = e * (1.0 / D_MODEL)
    part = jnp.sum(jnp.sum(e * e, axis=-1, keepdims=True), axis=0, keepdims=True) * (0.5 / D_MODEL)
    return dy, dy, jnp.broadcast_to(part, (8, LANES))


def _rms_fwd(x, g, name, comm=None):
    n, w = x.shape
    t = min(ROW_TILE, n)

    def body(x_ref, g_ref, o_ref):
        o_ref[...] = _rmsn(x_ref[...], g_ref[...], w).astype(BF)

    return _pcall(body, grid=(n // t,), in_specs=[_rows(t, w), _full((1, w))], out_specs=_rows(t, w),
                  out_shape=SDS((n, w), BF), sem=("arbitrary",), name=name, comm=comm)(x, g)


def _rms_bwd(x, g, dh, res, name, comm=None):
    n, w = x.shape
    t = min(ROW_TILE, n)
    has_res = res is not None

    def body(*refs):
        if has_res:
            x_ref, g_ref, dh_ref, res_ref, dx_ref, dxb_ref, dg_ref = refs
        else:
            x_ref, g_ref, dh_ref, dx_ref, dxb_ref, dg_ref = refs
        _, vjp = jax.vjp(lambda xx, gg: _rmsn(xx, gg, w), x_ref[...], g_ref[...])
        dx, dg = vjp(dh_ref[...])
        if has_res:
            dx = dx + res_ref[...]
        dx_ref[...] = dx
        dxb_ref[...] = dx.astype(BF)
        _acc(dg_ref, dg, pl.program_id(0) == 0)

    in_specs = [_rows(t, w), _full((1, w)), _rows(t, w)] + ([_rows(t, w)] if has_res else [])
    args = [x, g, dh] + ([res] if has_res else [])
    return _pcall(body, grid=(n // t,), in_specs=in_specs, out_specs=[_rows(t, w), _rows(t, w), _full((1, w))],
                  out_shape=[SDS((n, w), F32), SDS((n, w), BF), SDS((1, w), F32)], sem=("arbitrary",), name=name,
                  comm=comm)(*args)


def _merge_core(zg0, zg1, zg2, y0, y1, y2):
    return jax.nn.sigmoid(zg0) * y0 + jax.nn.sigmoid(zg1) * y1 + jax.nn.sigmoid(zg2) * y2


def _merge_fwd(z, y_gm, y_mla, y_mem, name):
    n = z.shape[0]
    t = min(ROW_TILE, n)
    w = D_MODEL

    def body(g0, g1, g2, y0, y1, y2, o_ref):
        o_ref[...] = _merge_core(g0[...].astype(F32), g1[...].astype(F32), g2[...].astype(F32), y0[...].astype(F32), y1[...].astype(F32),
                                 y2[...].astype(F32)).astype(BF)

    return pl.pallas_call(body, grid=(n // t,),
                          in_specs=[_rows(t, w, 0), _rows(t, w, 1), _rows(t, w, 2)] + [_rows(t, w)] * 3,
                          out_specs=_rows(t, w), out_shape=SDS((n, w), BF),
                          compiler_params=_params(("parallel",)), name=name)(z, z, z, y_gm, y_mla, y_mem)


def _merge_bwd(z, y_gm, y_mla, y_mem, dmerged, name):
    n = z.shape[0]
    t = min(ROW_TILE, n)
    w = D_MODEL

    def body(g0, g1, g2, y0, y1, y2, dm, dzg_ref, d0_ref, d1_ref, d2_ref):
        _, vjp = jax.vjp(_merge_core, g0[...].astype(F32), g1[...].astype(F32), g2[...].astype(F32), y0[...].astype(F32), y1[...].astype(F32),
                         y2[...].astype(F32))
        dg0, dg1, dg2, dy0, dy1, dy2 = vjp(dm[...])
        dzg_ref[:, 0:w] = dg0.astype(BF)
        dzg_ref[:, w:2 * w] = dg1.astype(BF)
        dzg_ref[:, 2 * w:3 * w] = dg2.astype(BF)
        d0_ref[...] = dy0.astype(BF)
        d1_ref[...] = dy1.astype(BF)
        d2_ref[...] = dy2.astype(BF)

    return pl.pallas_call(body, grid=(n // t,),
                          in_specs=[_rows(t, w, 0), _rows(t, w, 1), _rows(t, w, 2)] + [_rows(t, w)] * 4,
                          out_specs=[_rows(t, 3 * w, ZG // (3 * w))] + [_rows(t, w)] * 3,
                          out_shape=[SDS((n, Z_COLS), BF)] + [SDS((n, w), BF)] * 3,
                          compiler_params=_params(("parallel",)), name=name)(z, z, z, y_gm, y_mla, y_mem, dmerged)


def _gm_core(zu, zv, g_ln, b_ln, ws, bcols):
    t = zu.shape[0]
    u = jax.nn.gelu(zu)
    v = _layernorm(jax.nn.gelu(zv), g_ln, b_ln)
    row = lax.broadcasted_iota(jnp.int32, (GM_CHUNK, GM_CHUNK), 0)
    col = lax.broadcasted_iota(jnp.int32, (GM_CHUNK, GM_CHUNK), 1)
    wc = [jnp.where(row >= col, ws[g], 0.0) for g in range(GM_GROUPS)]
    chunks = []
    for c in range(t // GM_CHUNK):
        cols = []
        for g in range(GM_GROUPS):
            vc = v[c * GM_CHUNK:(c + 1) * GM_CHUNK, g * LANES:(g + 1) * LANES]
            cols.append(_mm_nn(wc[g], vc) + bcols[g])
        chunks.append(jnp.concatenate(cols, axis=1))
    mixed = chunks[0] if len(chunks) == 1 else jnp.concatenate(chunks, axis=0)
    return u * mixed


def _gm_specs(t):
    return [_rows(t, GM_WIDTH, ZU // GM_WIDTH), _rows(t, GM_WIDTH, ZV // GM_WIDTH), _full((1, GM_WIDTH)),
            _full((1, GM_WIDTH)), _full((GM_GROUPS, GM_CHUNK, GM_CHUNK))] + [_full((GM_CHUNK, 1))] * GM_GROUPS


def _gm_fwd(z, g_ln, b_ln, ws, bcols, name):
    n = z.shape[0]
    t = min(ROW_TILE, n)

    def body(zu, zv, g_ref, b_ref, ws_ref, c0, c1, c2, c3, o_ref):
        out = _gm_core(zu[...].astype(F32), zv[...].astype(F32), g_ref[...], b_ref[...], [ws_ref[g] for g in range(GM_GROUPS)],
                       [c0[...], c1[...], c2[...], c3[...]])
        o_ref[...] = out.astype(BF)

    return pl.pallas_call(body, grid=(n // t,), in_specs=_gm_specs(t), out_specs=_rows(t, GM_WIDTH),
                          out_shape=SDS((n, GM_WIDTH), BF), compiler_params=_params(("parallel",)),
                          name=name)(z, z, g_ln, b_ln, ws, *bcols)


def _gm_bwd(z, g_ln, b_ln, ws, bcols, dgm, dz, name, comm=None):
    n = z.shape[0]
    t = min(ROW_TILE, n)

    def body(zu, zv, g_ref, b_ref, ws_ref, c0, c1, c2, c3, dgm_ref, _, dz_ref, dg_ref, db_ref, dws_ref, e0, e1, e2,
             e3):
        first = pl.program_id(0) == 0
        _, vjp = jax.vjp(_gm_core, zu[...].astype(F32), zv[...].astype(F32), g_ref[...], b_ref[...],
                         [ws_ref[g] for g in range(GM_GROUPS)], [c0[...], c1[...], c2[...], c3[...]])
        dzu, dzv, dg, db, dws, dcols = vjp(dgm_ref[...])
        dz_ref[:, 0:GM_WIDTH] = dzu.astype(BF)
        dz_ref[:, GM_WIDTH:2 * GM_WIDTH] = dzv.astype(BF)
        _acc(dg_ref, dg, first)
        _acc(db_ref, db, first)
        _acc(dws_ref, jnp.stack(dws, axis=0), first)
        for ref, val in zip((e0, e1, e2, e3), dcols):
            _acc(ref, val, first)

    in_specs = _gm_specs(t) + [_rows(t, GM_WIDTH), ANY]
    return _pcall(
        body, grid=(n // t,), in_specs=in_specs,
        out_specs=[_rows(t, 2 * GM_WIDTH, ZU // (2 * GM_WIDTH)), _full((1, GM_WIDTH)), _full((1, GM_WIDTH)),
                   _full((GM_GROUPS, GM_CHUNK, GM_CHUNK))] + [_full((GM_CHUNK, 1))] * GM_GROUPS,
        out_shape=[SDS((n, Z_COLS), BF), SDS((1, GM_WIDTH), F32), SDS((1, GM_WIDTH), F32),
                   SDS((GM_GROUPS, GM_CHUNK, GM_CHUNK), F32)] + [SDS((GM_CHUNK, 1), F32)] * GM_GROUPS,
        sem=("arbitrary",), name=name, comm=comm, aliases={len(in_specs) - 1: 0})(z, z, g_ln, b_ln, ws, *bcols, dgm, dz)


def _rope_tables(pos_f, inv_full, cmask, smask, name, comm=None):
    n = pos_f.shape[0]
    t = min(ROW_TILE, n)

    def body(p_ref, inv_ref, cm_ref, sm_ref, cos_ref, sin_ref):
        ang = p_ref[...] * inv_ref[...]
        cos_ref[...] = jnp.cos(ang) * cm_ref[...]
        sin_ref[...] = jnp.sin(ang) * sm_ref[...]

    return _pcall(body, grid=(n // t,), in_specs=[_rows(t, 1)] + [_full((1, LANES))] * 3,
                  out_specs=[_rows(t, LANES)] * 2, out_shape=[SDS((n, LANES), F32)] * 2, sem=("parallel",),
                  name=name, comm=comm)(pos_f, inv_full, cmask, smask)


def _prep_norms(cq, ckv, g_cq, g_ckv):
    return _rmsn(cq, g_cq, Q_LORA), _rmsn(ckv, g_ckv, KV_LORA)


def _prep_heads(qa, kva, kpe, head_gains, cos_f, sin_s):
    g_qn, g_qp, g_kn, g_kp = head_gains
    qs = _split_lanes(qa)
    kvs = _split_lanes(kva)
    kp = _rope(_rmsn(kpe, g_kp, MLA_ROPE), cos_f, sin_s)
    q_out, k_out = [], []
    for h in range(MLA_HEADS):
        q_out.append(_rmsn(qs[h], g_qn, MLA_NOPE))
        q_out.append(_rope(_rmsn(qs[MLA_HEADS + h], g_qp, MLA_ROPE), cos_f, sin_s))
        k_out.append(_rmsn(kvs[h], g_kn, MLA_NOPE))
        k_out.append(kp)
    return (jnp.concatenate(q_out, axis=1), jnp.concatenate(k_out, axis=1),
            jnp.concatenate(kvs[MLA_HEADS:], axis=1))


def _prep_in_specs(t):
    return ([_rows(t, Q_LORA, CQ // Q_LORA), _rows(t, LANES, KPE // LANES), _rows(t, KV_LORA, CKV // KV_LORA),
             _rows(t, LANES), _rows(t, LANES), _full((1, Q_LORA)), _full((1, KV_LORA))] + [_full((1, LANES))] * 4
            + [_full((Q_LORA, 2048)), _full((KV_LORA, 2048))])


def _prep_fwd(z, cos_f, sin_s, gains, wq, wkv, name):
    n = z.shape[0]
    t = min(ROW_TILE, n)

    def body(cq, kpe, ckv, cos_ref, sin_ref, g_cq, g_ckv, g_qn, g_qp, g_kn, g_kp, wq_ref, wkv_ref, q_ref, k_ref, v_ref):
        cqn, ckvn = _prep_norms(cq[...].astype(F32), ckv[...].astype(F32), g_cq[...], g_ckv[...])
        qa = _dn(cqn, wq_ref[...], 1, 0)
        kva = _dn(ckvn, wkv_ref[...], 1, 0)
        q, k, v = _prep_heads(qa, kva, kpe[...].astype(F32), (g_qn[...], g_qp[...], g_kn[...], g_kp[...]), cos_ref[...],
                              sin_ref[...])
        q_ref[...] = q.astype(BF)
        k_ref[...] = k.astype(BF)
        v_ref[...] = v.astype(BF)

    return pl.pallas_call(body, grid=(n // t,), in_specs=_prep_in_specs(t),
                          out_specs=[_rows(t, 2048), _rows(t, 2048), _rows(t, 1024)],
                          out_shape=[SDS((n, 2048), BF), SDS((n, 2048), BF), SDS((n, 1024), BF)],
                          compiler_params=_params(("parallel",)),
                          name=name)(z, z, z, cos_f, sin_s, *gains, wq, wkv)


def _prep_bwd(z, cos_f, sin_s, gains, wq, wkv, dq, dk, dv, dz, name, comm=None):
    n = z.shape[0]
    t = min(ROW_TILE, n)
    wz = Q_LORA + LANES + KV_LORA

    def body(cq, kpe, ckv, cos_ref, sin_ref, g_cq, g_ckv, g_qn, g_qp, g_kn, g_kp, wq_ref, wkv_ref, dq_ref, dk_ref,
             dv_ref, _, dz_ref, o_cq, o_ckv, o_qn, o_qp, o_kn, o_kp, dwq_ref, dwkv_ref):
        first = pl.program_id(0) == 0
        cos_t, sin_t = cos_ref[...], sin_ref[...]
        (cqn, ckvn), vjp_norms = jax.vjp(_prep_norms, cq[...].astype(F32), ckv[...].astype(F32), g_cq[...], g_ckv[...])
        wq_t, wkv_t = wq_ref[...], wkv_ref[...]
        qa = _dn(cqn, wq_t, 1, 0)
        kva = _dn(ckvn, wkv_t, 1, 0)
        _, vjp_heads = jax.vjp(lambda a, b, c, g: _prep_heads(a, b, c, g, cos_t, sin_t), qa, kva, kpe[...].astype(F32),
                               (g_qn[...], g_qp[...], g_kn[...], g_kp[...]))
        dqa, dkva, dkpe, dhead = vjp_heads((dq_ref[...], dk_ref[...], dv_ref[...]))
        _acc(dwq_ref, _dn(cqn, dqa, 0, 0), first)
        _acc(dwkv_ref, _dn(ckvn, dkva, 0, 0), first)
        dcq, dckv, dg_cq, dg_ckv = vjp_norms((_dn(dqa, wq_t, 1, 1), _dn(dkva, wkv_t, 1, 1)))
        dz_ref[:, 0:Q_LORA] = dcq.astype(BF)
        dz_ref[:, Q_LORA:Q_LORA + LANES] = dkpe.astype(BF)
        dz_ref[:, Q_LORA + LANES:wz] = dckv.astype(BF)
        for ref, val in zip((o_cq, o_ckv, o_qn, o_qp, o_kn, o_kp), (dg_cq, dg_ckv) + tuple(dhead)):
            _acc(ref, val, first)

    gain_specs = [_full((1, Q_LORA)), _full((1, KV_LORA))] + [_full((1, LANES))] * 4
    gain_shapes = [SDS((1, Q_LORA), F32), SDS((1, KV_LORA), F32)] + [SDS((1, LANES), F32)] * 4
    in_specs = _prep_in_specs(t) + [_rows(t, 2048), _rows(t, 2048), _rows(t, 1024), ANY]
    return _pcall(
        body, grid=(n // t,), in_specs=in_specs,
        out_specs=[_rows(t, wz, CQ // wz)] + gain_specs + [_full((Q_LORA, 2048)), _full((KV_LORA, 2048))],
        out_shape=[SDS((n, Z_COLS), BF)] + gain_shapes + [SDS((Q_LORA, 2048), F32), SDS((KV_LORA, 2048), F32)],
        sem=("arbitrary",), name=name, comm=comm,
        aliases={len(in_specs) - 1: 0})(z, z, z, cos_f, sin_s, *gains, wq, wkv, dq, dk, dv, dz)


MLA_QK = 256
MLA_SCALE = 1.0 / math.sqrt(MLA_NOPE + MLA_ROPE)
LOG2E = 1.0 / math.log(2.0)
MLA_SCALE_LOG2E = MLA_SCALE * LOG2E


def _causal_mask(s, q0, k0):
    tq, tk = s.shape
    row = q0 + lax.broadcasted_iota(jnp.int32, (tq, tk), 0)
    col = k0 + lax.broadcasted_iota(jnp.int32, (tq, tk), 1)
    return jnp.where(row >= col, s, -jnp.inf)


def _mla_fwd(q, k, v, batch, seq, name, comm=None):
    n = q.shape[0]
    tq = min(ATT_TILE, seq)
    nq = seq // tq

    nh = ATT_HEADS_FWD

    def body(q_ref, k_ref, v_ref, o_ref, lse_ref):
        i = pl.program_id(2)

        def step(j, carry, diagonal=False):
            k0 = pl.multiple_of(j * tq, tq)
            out = []
            ones = jnp.ones((tq, LANES), BF)
            for hh in range(nh):
                m, acc = carry[hh]
                qb = q_ref[:, hh * MLA_QK:(hh + 1) * MLA_QK]
                kb = k_ref[pl.ds(k0, tq), hh * MLA_QK:(hh + 1) * MLA_QK]
                vb = v_ref[pl.ds(k0, tq), hh * MLA_V:(hh + 1) * MLA_V]
                s = _dn(qb, kb, 1, 1)
                if diagonal:
                    s = _causal_mask(s, i * tq, k0)
                m_new = jnp.maximum(m, jnp.max(s, axis=-1, keepdims=True))
                p = jnp.exp2((s - m_new) * MLA_SCALE_LOG2E)
                alpha = jnp.exp2((m - m_new) * MLA_SCALE_LOG2E)
                acc = alpha * acc + _dn(p, jnp.concatenate([vb, ones], axis=1), 1, 0)
                out.append((m_new, acc))
            return tuple(out)

        init = tuple((jnp.full((tq, 1), -jnp.inf, F32), jnp.zeros((tq, MLA_V + LANES), F32)) for _ in range(nh))
        final = step(i, lax.fori_loop(0, i, step, init), diagonal=True)
        for hh, (m, acc) in enumerate(final):
            l = acc[:, MLA_V:MLA_V + 1]
            o_ref[:, hh * MLA_V:(hh + 1) * MLA_V] = acc[:, :MLA_V] / l
            lse_ref[:, hh * LANES:(hh + 1) * LANES] = jnp.broadcast_to(m * MLA_SCALE + jnp.log(l), (tq, LANES))

    return _pcall(
        body, grid=(batch, MLA_HEADS // nh, nq),
        in_specs=[pl.BlockSpec((tq, nh * MLA_QK), lambda b, h, i: (b * nq + i, h)),
                  pl.BlockSpec((seq, nh * MLA_QK), lambda b, h, i: (b, h)),
                  pl.BlockSpec((seq, nh * MLA_V), lambda b, h, i: (b, h))],
        out_specs=[pl.BlockSpec((tq, nh * MLA_V), lambda b, h, i: (b * nq + i, h)),
                   pl.BlockSpec((tq, nh * LANES), lambda b, h, i: (b * nq + i, h))],
        out_shape=[SDS((n, MLA_HEADS * MLA_V), F32), SDS((n, MLA_HEADS * LANES), F32)],
        sem=("parallel", "parallel", "arbitrary"), name=name, comm=comm)(q, k, v)


def _mla_bwd(q, k, v, o, lse, do, batch, seq, name, comm=None):
    n = q.shape[0]
    tk = min(ATT_TILE, seq)
    nk = seq // tk

    nh = ATT_HEADS

    def body(q_ref, k_ref, v_ref, o_ref, lse_ref, do_ref, dq_ref, dk_ref, dv_ref):
        jk = pl.program_id(2)

        @pl.when(jk == 0)
        def _():
            dq_ref[...] = jnp.zeros_like(dq_ref)

        def step(i, carry, diagonal=False):
            q0 = pl.multiple_of(i * tk, tk)
            rows = pl.ds(q0, tk)
            out = []
            for hh in range(nh):
                dk_acc, dv_acc = carry[hh]
                qk_cols = slice(hh * MLA_QK, (hh + 1) * MLA_QK)
                v_cols = slice(hh * MLA_V, (hh + 1) * MLA_V)
                kb = k_ref[:, qk_cols]
                vb = v_ref[:, v_cols]
                qb = q_ref[rows, qk_cols]
                dob = do_ref[rows, v_cols]
                delta = jnp.sum(dob * o_ref[rows, v_cols], axis=-1, keepdims=True)
                s = _dn(qb, kb, 1, 1)
                if diagonal:
                    s = _causal_mask(s, q0, jk * tk)
                p = jnp.exp2(s * MLA_SCALE_LOG2E - lse_ref[rows, hh * LANES:hh * LANES + 1] * LOG2E)
                dv_acc = dv_acc + _dn(p, dob, 0, 0)
                dp = _dn(dob, vb, 1, 1)
                ds = p * (dp - delta) * MLA_SCALE
                dk_acc = dk_acc + _dn(ds, qb, 0, 0)
                dq_ref[rows, qk_cols] += _dn(ds, kb, 1, 0)
                out.append((dk_acc, dv_acc))
            return tuple(out)

        init = tuple((jnp.zeros((tk, MLA_QK), F32), jnp.zeros((tk, MLA_V), F32)) for _ in range(nh))
        final = lax.fori_loop(jk + 1, nk, step, step(jk, init, diagonal=True))
        for hh, (dk_acc, dv_acc) in enumerate(final):
            dk_ref[:, hh * MLA_QK:(hh + 1) * MLA_QK] = dk_acc
            dv_ref[:, hh * MLA_V:(hh + 1) * MLA_V] = dv_acc

    full_qk = pl.BlockSpec((seq, nh * MLA_QK), lambda b, h, j: (b, h))
    full_v = pl.BlockSpec((seq, nh * MLA_V), lambda b, h, j: (b, h))
    blk_qk = pl.BlockSpec((tk, nh * MLA_QK), lambda b, h, j: (b * nk + j, h))
    blk_v = pl.BlockSpec((tk, nh * MLA_V), lambda b, h, j: (b * nk + j, h))
    return _pcall(
        body, grid=(batch, MLA_HEADS // nh, nk),
        in_specs=[full_qk, blk_qk, blk_v, full_v, full_v, full_v],
        out_specs=[full_qk, blk_qk, blk_v],
        out_shape=[SDS((n, MLA_HEADS * MLA_QK), F32), SDS((n, MLA_HEADS * MLA_QK), F32),
                   SDS((n, MLA_HEADS * MLA_V), F32)],
        sem=("parallel", "parallel", "arbitrary"), name=name, comm=comm)(q, k, v, o, lse, do)


MEM_SCALE = 1.0 / math.sqrt(HEAD_DIM)
MEM_W = MEM_HEADS * HEAD_DIM


def _mem_core(qs, ks, vs, g_mq, g_mk):
    outs = []
    for h in range(MEM_HEADS):
        qh = _rmsn(qs[h], g_mq, HEAD_DIM)
        kh = _rmsn(ks[h], g_mk, HEAD_DIM)
        p = _softmax(_mm_nt(qh, kh) * MEM_SCALE)
        outs.append(_mm_nn(p, vs[h]))
    return jnp.concatenate(outs, axis=1)


def _mem_load(qm, kvm, g_mq, g_mk):
    hs = range(MEM_HEADS)
    qs = [qm[:, h * LANES:(h + 1) * LANES].astype(F32) for h in hs]
    ks = [kvm[:, h * LANES:(h + 1) * LANES] for h in hs]
    vs = [kvm[:, MEM_W + h * LANES:MEM_W + (h + 1) * LANES] for h in hs]
    return qs, ks, vs, g_mq[...], g_mk[...]


def _mem_fwd(z, kvm, g_mq, g_mk, batch, seq, name, comm=None):
    n = z.shape[0]
    t = min(ROW_TILE, seq)
    per = seq // t

    def body(qm, kvm_ref, gq, gk, o_ref):
        o_ref[...] = _mem_core(*_mem_load(qm, kvm_ref, gq, gk)).astype(BF)

    return _pcall(
        body, grid=(n // t,),
        in_specs=[_rows(t, MEM_W, QM // MEM_W), pl.BlockSpec((MEM_LEN, 2 * MEM_W), lambda i: (i // per, 0)),
                  _full((1, LANES)), _full((1, LANES))],
        out_specs=_rows(t, MEM_W), out_shape=SDS((n, MEM_W), BF), sem=("parallel",), name=name,
        comm=comm)(z, kvm, g_mq, g_mk)


def _mem_bwd(z, kvm, g_mq, g_mk, dom, dz, batch, seq, name):
    n = z.shape[0]
    t = min(ROW_TILE, seq)
    per = seq // t

    def body(qm, kvm_ref, gq, gk, dom_ref, _, dz_ref, dkvm_ref, dgq_ref, dgk_ref):
        i = pl.program_id(0)
        _, vjp = jax.vjp(_mem_core, *_mem_load(qm, kvm_ref, gq, gk))
        dqs, dks, dvs, dgq, dgk = vjp(dom_ref[...])
        dz_ref[...] = jnp.concatenate(dqs, axis=1).astype(BF)
        _acc(dkvm_ref, jnp.concatenate(dks + dvs, axis=1), i % per == 0)
        _acc(dgq_ref, dgq, i == 0)
        _acc(dgk_ref, dgk, i == 0)

    kv_spec = pl.BlockSpec((MEM_LEN, 2 * MEM_W), lambda i: (i // per, 0))
    return pl.pallas_call(
        body, grid=(n // t,),
        in_specs=[_rows(t, MEM_W, QM // MEM_W), kv_spec, _full((1, LANES)), _full((1, LANES)), _rows(t, MEM_W), ANY],
        out_specs=[_rows(t, MEM_W, QM // MEM_W), kv_spec, _full((1, LANES)), _full((1, LANES))],
        out_shape=[SDS((n, Z_COLS), BF), SDS((batch * MEM_LEN, 2 * MEM_W), F32), SDS((1, LANES), F32),
                   SDS((1, LANES), F32)],
        input_output_aliases={5: 0},
        compiler_params=_params(("arbitrary",)), name=name)(z, kvm, g_mq, g_mk, dom, dz)


def _me():
    return lax.axis_index("x"), lax.axis_index("y"), lax.axis_index("c")


def _other_chips(x, y):
    return [(1 - x, y), (x, 1 - y), (1 - x, 1 - y)]


def _shard_shape(name):
    r, c = BIG_SHAPE[name]
    return (r, c // N_CHIPS) if name in COL_SHARDED else (r // N_CHIPS, c)


def _n_pieces(half_rows):
    for n in range(max(1, half_rows // PIECE_ROWS), 0, -1):
        if half_rows % n == 0 and (half_rows // n) % 16 == 0:
            return n
    return 1


def _piece_plan(shapes):
    plan = []
    for r, _ in shapes:
        h = r // 2
        n = _n_pieces(h)
        plan.append((h, n, h // n))
    return plan


def _remote(send, recv, sem, src, dst, to):
    return pltpu.make_async_remote_copy(src_ref=src, dst_ref=dst, send_sem=send.at[sem], recv_sem=recv.at[sem],
                                        device_id=to, device_id_type=MESH)


def _gather_far(shards):
    plan = _piece_plan([s.shape for s in shards])
    n_far = 3 * sum(n for _, n, _ in plan)
    n_loc = 2 * sum(n for _, n, _ in plan)

    def copies(s_refs, o_refs, send, recv, local):
        x, y, c = _me()
        k = 2 * x + y
        mine, sends, arrivals = [], [], []
        for t, (h, n, pr) in enumerate(plan):
            s_ref, o_ref = s_refs[t], o_refs[t]
            for core in range(2):
                for p in range(n):
                    rows = pl.ds(core * h + p * pr, pr)
                    mine.append(pltpu.make_async_copy(s_ref.at[rows], o_ref.at[k, rows], local.at[len(mine)]))
            for chip in _other_chips(x, y):
                for p in range(n):
                    rows = pl.ds(c * h + p * pr, pr)
                    s = len(sends)
                    sends.append(_remote(send, recv, s, s_ref.at[rows], o_ref.at[k, rows], (*chip, c)))
                    arrivals.append(_remote(send, recv, s, s_ref.at[rows], o_ref.at[2 * chip[0] + chip[1], rows],
                                            (*chip, c)))
        return sends, arrivals, mine

    return _Phase(shards, [SDS((N_CHIPS,) + s.shape, s.dtype) for s in shards], n_far, n_loc, copies)


def _gather_near(bufs):
    plan = _piece_plan([b.shape[1:] for b in bufs])
    n_sem = 3 * sum(n for _, n, _ in plan)

    def copies(i_refs, o_refs, send, recv, local):
        x, y, c = _me()
        sib = (x, y, 1 - c)
        sends, arrivals = [], []
        for t, (h, n, pr) in enumerate(plan):
            for chip in _other_chips(x, y):
                ci = 2 * chip[0] + chip[1]
                for p in range(n):
                    rows = pl.ds(c * h + p * pr, pr)
                    rows_sib = pl.ds((1 - c) * h + p * pr, pr)
                    s = len(sends)
                    sends.append(_remote(send, recv, s, i_refs[t].at[ci, rows], o_refs[t].at[ci, rows], sib))
                    arrivals.append(_remote(send, recv, s, i_refs[t].at[ci, rows_sib], o_refs[t].at[ci, rows_sib], sib))
        return sends, arrivals, []

    return _Phase(bufs, [SDS(b.shape, b.dtype) for b in bufs], n_sem, 0, copies, {t: t for t in range(len(bufs))})


def _pair_exchange(grads):
    plan = _piece_plan([g.shape[1:] for g in grads])
    n_sem = sum(n for _, n, _ in plan)

    def copies(g_refs, o_refs, send, recv, local):
        x, y, c = _me()
        sends = []
        for t, (h, n, pr) in enumerate(plan):
            for p in range(n):
                sends.append(_remote(send, recv, len(sends), g_refs[t].at[:, pl.ds((1 - c) * h + p * pr, pr)],
                                     o_refs[t].at[:, pl.ds(p * pr, pr)], (x, y, 1 - c)))
        return sends, sends, []

    return _Phase(grads, [SDS((N_CHIPS, g.shape[1] // 2, g.shape[2]), F32) for g in grads], n_sem, 0, copies)


def _pair_add(ck, g, theirs, name):
    _, r, c = g.shape
    (h, n, pr), = _piece_plan([(r, c)])

    def body(ck_ref, g_ref, t_ref, pbf_ref):
        pbf_ref[...] = (g_ref[...] + t_ref[...]).astype(BF)

    half = pl.BlockSpec((None, pr, c), lambda k, p, ck: (k, p, 0))
    spec = pltpu.PrefetchScalarGridSpec(
        num_scalar_prefetch=1, grid=(N_CHIPS, n),
        in_specs=[pl.BlockSpec((None, pr, c), lambda k, p, ck: (k, ck[0] * n + p, 0)), half], out_specs=half)
    return pl.pallas_call(body, grid_spec=spec, out_shape=SDS((N_CHIPS, h, c), BF),
                          compiler_params=_params(("arbitrary", "arbitrary")), name=name)(ck, g, theirs)


def _scatter_partials(pbfs):
    plan = [(h, _n_pieces(h), h // _n_pieces(h)) for h in [p.shape[1] for p in pbfs]]
    n_sem = 3 * sum(n for _, n, _ in plan)

    def copies(p_refs, o_refs, send, recv, local):
        x, y, c = _me()
        sends = []
        for t, (h, n, pr) in enumerate(plan):
            for j, chip in enumerate(_other_chips(x, y)):
                for p in range(n):
                    rows = pl.ds(p * pr, pr)
                    sends.append(_remote(send, recv, len(sends), p_refs[t].at[2 * chip[0] + chip[1], rows],
                                         o_refs[t].at[j, rows], (*chip, c)))
        return sends, sends, []

    return _Phase(pbfs, [SDS((3,) + p.shape[1:], BF) for p in pbfs], n_sem, 0, copies)


def _sum_chips(ck, pbf, slots, name):
    _, h, c = pbf.shape
    n = _n_pieces(h)
    pr = h // n

    def body(ck_ref, p_ref, s_ref, o_ref):
        o_ref[...] = (((p_ref[...].astype(F32) + s_ref[0].astype(F32)) + s_ref[1].astype(F32))
                      + s_ref[2].astype(F32))

    spec = pltpu.PrefetchScalarGridSpec(
        num_scalar_prefetch=1, grid=(n,),
        in_specs=[pl.BlockSpec((None, pr, c), lambda p, ck: (ck[1], p, 0)),
                  pl.BlockSpec((3, pr, c), lambda p, ck: (0, p, 0))],
        out_specs=pl.BlockSpec((pr, c), lambda p, ck: (ck[0] * n + p, 0)))
    return pl.pallas_call(body, grid_spec=spec, out_shape=SDS((2 * h, c), F32),
                          compiler_params=_params(("arbitrary",)), name=name)(ck, pbf, slots)


def _join_halves(sums):
    plan = _piece_plan([s.shape for s in sums])
    n_sem = sum(n for _, n, _ in plan)

    def copies(r_refs, o_refs, send, recv, local):
        x, y, c = _me()
        sends, arrivals = [], []
        for t, (h, n, pr) in enumerate(plan):
            for p in range(n):
                rows = pl.ds(c * h + p * pr, pr)
                rows_sib = pl.ds((1 - c) * h + p * pr, pr)
                s = len(sends)
                sends.append(_remote(send, recv, s, r_refs[t].at[rows], o_refs[t].at[rows], (x, y, 1 - c)))
                arrivals.append(_remote(send, recv, s, r_refs[t].at[rows_sib], o_refs[t].at[rows_sib], (x, y, 1 - c)))
        return sends, arrivals, []

    return _Phase(sums, [SDS(s.shape, F32) for s in sums], n_sem, 0, copies, {t: t for t in range(len(sums))})


def _gather_small(s, name):
    def body(s_ref, o_ref, send, recv, local):
        x, y, c = _me()
        me = 4 * x + 2 * y + c
        keep = pltpu.make_async_copy(s_ref, o_ref.at[me], local)
        keep.start()
        sends = []
        for r in range(1, 8):
            fx, fy, fc = (r >> 2) & 1, (r >> 1) & 1, r & 1
            to = (x ^ fx, y ^ fy, c ^ fc)
            sends.append(pltpu.make_async_remote_copy(
                src_ref=s_ref, dst_ref=o_ref.at[me], send_sem=send.at[r - 1], recv_sem=recv.at[r - 1],
                device_id=to, device_id_type=MESH))
        for cp in sends:
            cp.start()
        for r in range(1, 8):
            fx, fy, fc = (r >> 2) & 1, (r >> 1) & 1, r & 1
            src = 4 * (x ^ fx) + 2 * (y ^ fy) + (c ^ fc)
            pltpu.make_async_remote_copy(
                src_ref=s_ref, dst_ref=o_ref.at[src], send_sem=send.at[r - 1], recv_sem=recv.at[r - 1],
                device_id=(x ^ fx, y ^ fy, c ^ fc), device_id_type=MESH).wait_recv()
        for cp in sends:
            cp.wait_send()
        keep.wait()

    return pl.pallas_call(
        body, in_specs=[ANY], out_specs=ANY, out_shape=SDS((8, SMALL_ROWS, LANES), F32),
        scratch_shapes=[pltpu.SemaphoreType.DMA((7,)), pltpu.SemaphoreType.DMA((7,)), pltpu.SemaphoreType.DMA],
        name=name)(s)


def _adam_math(w, g, m, v):
    nm = ADAM_B1 * m + (1.0 - ADAM_B1) * g
    nv = ADAM_B2 * v + (1.0 - ADAM_B2) * (g * g)
    m_hat = nm / (1.0 - ADAM_B1 ** ADAM_STEP)
    v_hat = nv / (1.0 - ADAM_B2 ** ADAM_STEP)
    return -ADAM_LR * (m_hat / (jnp.sqrt(v_hat) + ADAM_EPS) + ADAM_WD * w), nm, nv


def _adamw(w, g, m, v, name):
    _, r, c = w.shape
    t = max(d for d in range(8, r + 1, 8) if r % d == 0 and 16 * d * c * 4 <= VMEM_LIMIT - (8 << 20))

    def body(w_ref, g_ref, m_ref, v_ref, go_ref, d_ref, nm_ref, nv_ref):
        g_ = g_ref[...]
        d, nm, nv = _adam_math(w_ref[...], g_, m_ref[...], v_ref[...])
        go_ref[...] = g_
        d_ref[...] = d
        nm_ref[...] = nm
        nv_ref[...] = nv

    lead = pl.BlockSpec((None, t, c), lambda i: (0, i, 0))
    return pl.pallas_call(body, grid=(r // t,), in_specs=[lead, _rows(t, c), lead, lead], out_specs=[lead] * 4,
                          out_shape=[SDS((1, r, c), F32)] * 4, compiler_params=_params(("parallel",)),
                          name=name)(w, g, m, v)


def _small_layout():
    out, r0 = {}, 0
    for n in SMALL:
        size = int(np.prod(SMALL_SHAPE[n]))
        nr = -(-size // LANES)
        out[n] = (r0, nr)
        r0 += nr
    assert r0 <= SMALL_ROWS
    return out, r0


def _pack_small(grads, loss_tile, name):
    layout, used = _small_layout()

    def body(*refs):
        o_ref = refs[-1]
        o_ref[used:used + 1, :] = refs[-2][0:1, :]
        for n, ref in zip(SMALL, refs[:-2]):
            r0, nr = layout[n]
            if n == "w_spatial":
                for g in range(GM_GROUPS):
                    o_ref[r0 + g * GM_CHUNK:r0 + (g + 1) * GM_CHUNK, :] = ref[g]
            elif n == "b_spatial":
                o_ref[r0:r0 + nr, :] = ref[...]
            else:
                for i in range(nr):
                    o_ref[r0 + i:r0 + i + 1, :] = ref[:, i * LANES:(i + 1) * LANES]
        if used + 1 < SMALL_ROWS:
            o_ref[used + 1:SMALL_ROWS, :] = jnp.zeros((SMALL_ROWS - used - 1, LANES), F32)

    return pl.pallas_call(body, out_shape=SDS((SMALL_ROWS, LANES), F32), name=name)(*grads, loss_tile)


def _adamw_small(gathered, ws, ms, vs, name):
    layout, used = _small_layout()
    n_t = len(SMALL)

    def body(*refs):
        g_ref = refs[0]
        w_refs, m_refs, v_refs = refs[1:1 + n_t], refs[1 + n_t:1 + 2 * n_t], refs[1 + 2 * n_t:1 + 3 * n_t]
        outs = refs[1 + 3 * n_t:1 + 7 * n_t]
        acc = refs[-1]
        total = g_ref[0]
        for j in range(1, 8):
            total = total + g_ref[j]
        acc[...] = total
        refs[1 + 7 * n_t][...] = acc[used:used + 1, :]
        for t, n in enumerate(SMALL):
            r0, nr = layout[n]
            o_refs = [outs[t], outs[n_t + t], outs[2 * n_t + t], outs[3 * n_t + t]]
            if n == "w_spatial":
                views = [((0, g), slice(r0 + g * GM_CHUNK, r0 + (g + 1) * GM_CHUNK), slice(None))
                         for g in range(GM_GROUPS)]
            elif n == "b_spatial":
                views = [((0,), slice(r0, r0 + nr), slice(None))]
            else:
                width = SMALL_SHAPE[n][1]
                views = [((slice(None), slice(i * LANES, min((i + 1) * LANES, width))), slice(r0 + i, r0 + i + 1),
                          slice(0, min(LANES, width - i * LANES))) for i in range(nr)]
            for idx, rows, lanes in views:
                g = acc[rows, lanes]
                d, nm, nv = _adam_math(w_refs[t][idx], g, m_refs[t][idx], v_refs[t][idx])
                for ref, val in zip(o_refs, (g, d, nm, nv)):
                    ref[idx] = val

    shapes = [SDS(SMALL_SHAPE[n], F32) for n in SMALL]
    return pl.pallas_call(body, out_shape=shapes * 4 + [SDS((1, LANES), F32)],
                          scratch_shapes=[pltpu.VMEM((SMALL_ROWS, LANES), F32)], name=name)(gathered, *ws, *ms, *vs)


def _win_layout(w_in):
    pad = jnp.zeros((w_in.shape[0], LANES - MLA_ROPE), w_in.dtype)
    u, v, cq = w_in[:, 0:512], w_in[:, 512:1024], w_in[:, 1024:1408]
    ckv, kpe, qm, zg = w_in[:, 1408:1664], w_in[:, 1664:1728], w_in[:, 1728:2240], w_in[:, 2240:5312]
    return jnp.concatenate([zg, u, v, qm, cq, kpe, pad, ckv], axis=1)


def _win_unlayout_rows(gt):
    zg, u, v, qm = gt[ZG:ZG + 3072], gt[ZU:ZU + 512], gt[ZV:ZV + 512], gt[QM:QM + 512]
    cq, kpe, ckv = gt[CQ:CQ + 384], gt[KPE:KPE + MLA_ROPE], gt[CKV:CKV + 256]
    return jnp.concatenate([u, v, cq, ckv, kpe, qm, zg], axis=0)


def _wq_layout(w_uq):
    w = w_uq.reshape(Q_LORA, MLA_HEADS, MLA_NOPE + MLA_ROPE)
    nope = w[:, :, :MLA_NOPE].reshape(Q_LORA, MLA_HEADS * MLA_NOPE)
    pe = jnp.pad(w[:, :, MLA_NOPE:], ((0, 0), (0, 0), (0, LANES - MLA_ROPE))).reshape(Q_LORA, MLA_HEADS * LANES)
    return jnp.concatenate([nope, pe], axis=1)


def _wq_unlayout(g):
    nope = g[:, :1024].reshape(Q_LORA, MLA_HEADS, MLA_NOPE)
    pe = g[:, 1024:].reshape(Q_LORA, MLA_HEADS, LANES)[:, :, :MLA_ROPE]
    return jnp.concatenate([nope, pe], axis=2).reshape(Q_LORA, MLA_HEADS * (MLA_NOPE + MLA_ROPE))


def _wkv_layout(w_ukv):
    w = w_ukv.reshape(KV_LORA, MLA_HEADS, MLA_NOPE + MLA_V)
    return jnp.concatenate([w[:, :, :MLA_NOPE].reshape(KV_LORA, 1024), w[:, :, MLA_NOPE:].reshape(KV_LORA, 1024)],
                           axis=1)


def _wkv_unlayout(g):
    kn = g[:, :1024].reshape(KV_LORA, MLA_HEADS, MLA_NOPE)
    v = g[:, 1024:].reshape(KV_LORA, MLA_HEADS, MLA_V)
    return jnp.concatenate([kn, v], axis=2).reshape(KV_LORA, MLA_HEADS * (MLA_NOPE + MLA_V))


def _owner_major(g, name):
    r, c = _shard_shape(name)
    return g.reshape(r, N_CHIPS, c).transpose(1, 0, 2) if name in COL_SHARDED else g.reshape(N_CHIPS, r, c)


def _pad_lanes(g):
    return jnp.pad(g, ((0, 0), (0, LANES - g.shape[1])))


def kernel(x, mem, positions, g_mix, w_in, g_cq, w_uq, g_ckv, w_ukv, g_q_nope, g_q_pe, g_k_nope, g_k_pe, g_gm_ln, b_gm_ln, w_spatial, b_spatial, g_mem, w_mem_kv, g_mq, g_mk, w_o_gm, w_o_mla, w_o_mem, w_out, g_ffn, w_ff1, w_ff2, loss_target, m_g_mix, m_w_in, m_g_cq, m_w_uq, m_g_ckv, m_w_ukv, m_g_q_nope, m_g_q_pe, m_g_k_nope, m_g_k_pe, m_g_gm_ln, m_b_gm_ln, m_w_spatial, m_b_spatial, m_g_mem, m_w_mem_kv, m_g_mq, m_g_mk, m_w_o_gm, m_w_o_mla, m_w_o_mem, m_w_out, m_g_ffn, m_w_ff1, m_w_ff2, v_g_mix, v_w_in, v_g_cq, v_w_uq, v_g_ckv, v_w_ukv, v_g_q_nope, v_g_q_pe, v_g_k_nope, v_g_k_pe, v_g_gm_ln, v_b_gm_ln, v_w_spatial, v_b_spatial, v_g_mem, v_w_mem_kv, v_g_mq, v_g_mk, v_w_o_gm, v_w_o_mla, v_w_o_mem, v_w_out, v_g_ffn, v_w_ff1, v_w_ff2):
    given = dict(locals())
    wts = {n: given[n] for n in WEIGHTS}
    mom = {n: given["m_" + n] for n in WEIGHTS}
    var = {n: given["v_" + n] for n in WEIGHTS}
    batch, seq, _ = x.shape
    n_tok = batch * seq

    def natural(n, g):
        r, c = _shard_shape(n)
        return g.transpose(1, 0, 2).reshape(r, N_CHIPS * c) if n in COL_SHARDED else g.reshape(N_CHIPS * r, c)

    def far(names):
        return _gather_far([wts[n][0].astype(BF) for n in names])

    x2 = x.reshape(n_tok, D_MODEL)
    tgt2 = loss_target.reshape(n_tok, D_MODEL)
    mem2 = mem.reshape(batch * MEM_LEN, D_MODEL)
    pos_f = positions.reshape(n_tok, 1).astype(F32)

    inv = ROPE_BASE ** (-jnp.arange(0, MLA_ROPE, 2, dtype=F32) / MLA_ROPE)
    zeros64 = jnp.zeros((LANES - MLA_ROPE,), F32)
    inv_full = jnp.concatenate([inv, inv, zeros64]).reshape(1, LANES)
    half = MLA_ROPE // 2
    cmask = jnp.concatenate([jnp.ones((MLA_ROPE,), F32), zeros64]).reshape(1, LANES)
    smask = jnp.concatenate([-jnp.ones((half,), F32), jnp.ones((half,), F32), zeros64]).reshape(1, LANES)

    prep_gains = [g_cq, g_ckv, g_q_nope, _pad_lanes(g_q_pe), g_k_nope, _pad_lanes(g_k_pe)]
    ws = w_spatial[0]
    bcols = [b_spatial[0, g].reshape(GM_CHUNK, 1) for g in range(GM_GROUPS)]

    h1, in_far = _rms_fwd(x2, g_mix, "rms_mix", comm=far(EARLY[:1]))
    (cos_f, sin_s), early = _rope_tables(pos_f, inv_full, cmask, smask, "rope_tables",
                                         comm=_together(_gather_near(in_far), far(EARLY[1:])))
    memn, rest = _rms_fwd(mem2, g_mem, "rms_mem", comm=_gather_near(early[1:]))
    full = {n: natural(n, g) for n, g in zip(EARLY, list(early[:1]) + list(rest))}
    win = _win_layout(full["w_in"])
    wq = _wq_layout(full["w_uq"])
    wkv = _wkv_layout(full["w_ukv"])
    z, proj_far = _mm(h1, win, out_dtypes=(BF,), name="mm_in", comm=far(LATE_PROJ))
    gm = _gm_fwd(z, g_gm_ln, b_gm_ln, ws, bcols, "gm_fwd")
    qc, kc, vc = _prep_fwd(z, cos_f, sin_s, prep_gains, wq, wkv, "prep_fwd")
    (o_mla, lse), ff_far = _mla_fwd(qc, kc, vc, batch, seq, "mla_fwd", comm=far(LATE_FF))
    kvm, proj = _mm(memn, full["w_mem_kv"], name="mm_memkv", comm=_gather_near(proj_far))
    o_mem, ff = _mem_fwd(z, kvm, g_mq, g_mk, batch, seq, "mem_fwd", comm=_gather_near(ff_far))
    full.update({n: natural(n, g) for n, g in zip(LATE_PROJ + LATE_FF, list(proj) + list(ff))})
    y_gm = _mm(gm, full["w_o_gm"], out_dtypes=(BF,), name="mm_o_gm")
    y_mla = _mm(o_mla, full["w_o_mla"], out_dtypes=(BF,), name="mm_o_mla")
    y_mem = _mm(o_mem, full["w_o_mem"], out_dtypes=(BF,), name="mm_o_mem")
    merged = _merge_fwd(z, y_gm, y_mla, y_mem, "merge_fwd")
    x1, h2 = _mm(merged, full["w_out"], ins=(x2,), row_ins=(g_ffn,), epilogue=_residual_rms, out_dtypes=(F32, BF),
                 name="mm_out")
    a_ff, r_ff = _mm(h2, full["w_ff1"], epilogue=_relu2, out_dtypes=(BF, BF), name="mm_ff1")
    dy, dyb, loss_tile = _mm(r_ff, full["w_ff2"], ins=(x1, tgt2), epilogue=_loss_tail, out_dtypes=(F32, BF),
                             total=True, name="mm_ff2")

    gw = {}
    da = _mm(dyb, full["w_ff2"], tb=True, ins=(a_ff,), epilogue=_relu2_bwd, out_dtypes=(BF,), name="mm_d_a")
    gw["w_ff2"] = _owner_major(_mm(r_ff, dyb, ta=True, name="mm_dw_ff2"), "w_ff2")
    gw["w_ff1"] = _mm(h2, da, ta=True, owner_cols=D_FF // N_CHIPS, name="mm_dw_ff1")
    dx1, dx1b, dg_ffn = _mm(da, full["w_ff1"], tb=True, ins=(x1, dy), row_ins=(g_ffn,), epilogue=_rms_bwd_tail,
                            out_dtypes=(F32, BF), total=(1, D_MODEL), name="mm_d_h2")
    dmerged = _mm(dx1b, full["w_out"], tb=True, name="mm_d_merged")
    gw["w_out"] = _owner_major(_mm(merged, dx1b, ta=True, name="mm_dw_out"), "w_out")
    dz, dy_gm, dy_mla, dy_mem = _merge_bwd(z, y_gm, y_mla, y_mem, dmerged, "merge_bwd")
    dgm = _mm(dy_gm, full["w_o_gm"], tb=True, name="mm_d_gm")
    gw["w_o_gm"] = _owner_major(_mm(gm, dy_gm, ta=True, name="mm_dw_o_gm"), "w_o_gm")
    do_mla = _mm(dy_mla, full["w_o_mla"], tb=True, name="mm_d_omla")
    gw["w_o_mla"] = _owner_major(_mm(o_mla, dy_mla, ta=True, name="mm_dw_o_mla"), "w_o_mla")
    do_mem = _mm(dy_mem, full["w_o_mem"], tb=True, name="mm_d_omem")
    gw["w_o_mem"] = _owner_major(_mm(o_mem, dy_mem, ta=True, name="mm_dw_o_mem"), "w_o_mem")
    ck = jnp.stack([lax.axis_index("c"), 2 * lax.axis_index("x") + lax.axis_index("y")]).astype(jnp.int32)

    def pair_sums(names, theirs):
        return [_pair_add(ck, gw[n], t, "pair_add_" + n) for n, t in zip(names, theirs)]

    def chip_sums(names, pairs, slots):
        return [_sum_chips(ck, p, s, "sum_chips_" + n) for n, p, s in zip(names, pairs, slots)]

    (dz, dg_ln, db_ln, dws, *dbcols), theirs = _gm_bwd(z, g_gm_ln, b_gm_ln, ws, bcols, dgm, dz, "gm_bwd",
                                                      comm=_pair_exchange([gw[n] for n in LATE]))
    pairs = pair_sums(LATE, theirs)
    (dq, dk, dv), slots = _mla_bwd(qc, kc, vc, o_mla, lse, do_mla, batch, seq, "mla_bwd",
                                   comm=_scatter_partials(pairs))
    sums = chip_sums(LATE, pairs, slots)
    (dz, dg_cq, dg_ckv, dg_qn, dg_qp, dg_kn, dg_kp, dwq, dwkv), reduced_late = _prep_bwd(
        z, cos_f, sin_s, prep_gains, wq, wkv, dq, dk, dv, dz, "prep_bwd", comm=_join_halves(sums))
    dz, dkvm, dg_mq, dg_mk = _mem_bwd(z, kvm, g_mq, g_mk, do_mem, dz, batch, seq, "mem_bwd")
    dmemn = _mm(dkvm, full["w_mem_kv"], tb=True, name="mm_d_memn")
    gw["w_mem_kv"] = _owner_major(_mm(memn, dkvm, ta=True, name="mm_dw_memkv"), "w_mem_kv")
    _, _, dg_mem = _rms_bwd(mem2, g_mem, dmemn, None, "rms_mem_bwd")
    gw["w_in"] = _win_unlayout_rows(_mm(dz, h1, ta=True, name="mm_dw_in")).reshape(N_CHIPS, W_IN_COLS // N_CHIPS,
                                                                                  D_MODEL)
    gw["w_uq"] = _owner_major(_wq_unlayout(dwq), "w_uq")
    gw["w_ukv"] = _owner_major(_wkv_unlayout(dwkv), "w_ukv")
    dh1, theirs = _mm(dz, win, tb=True, name="mm_d_h1_top", rows=(0, 2), comm=_pair_exchange([gw[n] for n in EARLY]))
    pairs = pair_sums(EARLY, theirs)
    dh1, slots = _mm(dz, win, tb=True, name="mm_d_h1_bottom", rows=(1, 2), into=dh1,
                     comm=_scatter_partials(pairs))
    grad_x, _, dg_mix = _rms_bwd(x2, g_mix, dh1, dx1, "rms_mix_bwd")
    reduced_early = _run_phase(_join_halves(chip_sums(EARLY, pairs, slots)), "join_early")
    reduced = dict(zip(LATE + EARLY, list(reduced_late) + list(reduced_early)))

    def swapped(a):
        return jnp.swapaxes(a, -1, -2)

    results = {n: _adamw(wts[n], reduced[n], mom[n], var[n], "adamw_" + n) for n in BIG if n != "w_in"}
    results["w_in"] = [swapped(r) for r in _adamw(swapped(w_in), reduced["w_in"], swapped(m_w_in), swapped(v_w_in),
                                                  "adamw_w_in")]

    small_g = {"g_mix": dg_mix, "g_cq": dg_cq, "g_ckv": dg_ckv, "g_q_nope": dg_qn, "g_q_pe": dg_qp,
               "g_k_nope": dg_kn, "g_k_pe": dg_kp, "g_gm_ln": dg_ln, "b_gm_ln": db_ln, "w_spatial": dws,
               "b_spatial": jnp.concatenate(dbcols, axis=1).T, "g_mem": dg_mem, "g_mq": dg_mq, "g_mk": dg_mk,
               "g_ffn": dg_ffn}
    packed = _pack_small([small_g[n] for n in SMALL], loss_tile, "pack_small")
    small_out = _adamw_small(_gather_small(packed, "gather_small"), [wts[n] for n in SMALL],
                             [mom[n] for n in SMALL], [var[n] for n in SMALL], "adamw_small")
    for t, n in enumerate(SMALL):
        results[n] = [small_out[j * len(SMALL) + t] for j in range(4)]

    loss = small_out[4 * len(SMALL)][0, 0]
    grad_x = grad_x.reshape(batch, seq, D_MODEL)
    return (loss, grad_x, *[results[n][0] for n in WEIGHTS], *[results[n][1] for n in WEIGHTS],
            *[results[n][2] for n in WEIGHTS], *[results[n][3] for n in WEIGHTS])
```

```python
import functools
import math

import numpy as np
import jax
import jax.numpy as jnp
from jax import lax
from jax.experimental import pallas as pl
from jax.experimental.pallas import tpu as pltpu

F32 = jnp.float32
BF = jnp.bfloat16
SDS = jax.ShapeDtypeStruct
MESH = pl.DeviceIdType.MESH

D_MODEL = 1024
MEM_LEN = 256
MEM_HEADS = 4
HEAD_DIM = 128
GM_WIDTH = 512
GM_CHUNK = 128
GM_GROUPS = 4
MLA_HEADS = 8
MLA_NOPE = 128
MLA_ROPE = 64
MLA_V = 128
Q_LORA = 384
KV_LORA = 256
ROPE_BASE = 10000.0
D_FF = 4096
EPS = 1e-6
W_IN_COLS = 5312
ADAM_LR, ADAM_B1, ADAM_B2, ADAM_EPS, ADAM_WD, ADAM_STEP = 0.001, 0.9, 0.999, 1e-08, 0.01, 10

ZG, ZU, ZV, QM, CQ, KPE, CKV = 0, 3072, 3584, 4096, 4608, 4992, 5120
Z_COLS = 5376
LANES = 128
ROW_TILE = 512
ATT_TILE = 1024
ATT_HEADS = 2
ATT_HEADS_FWD = 4
VMEM_LIMIT = 60 * 1024 * 1024

N_CHIPS = 4
PIECE_ROWS = 256
SMALL_ROWS = 560

BIG = ["w_in", "w_uq", "w_ukv", "w_mem_kv", "w_o_gm", "w_o_mla", "w_o_mem", "w_out", "w_ff1", "w_ff2"]
BIG_SHAPE = {"w_in": (1024, 5312), "w_uq": (384, 1536), "w_ukv": (256, 2048), "w_mem_kv": (1024, 1024),
             "w_o_gm": (512, 1024), "w_o_mla": (1024, 1024), "w_o_mem": (512, 1024), "w_out": (1024, 1024),
             "w_ff1": (1024, 4096), "w_ff2": (4096, 1024)}
COL_SHARDED = {"w_in", "w_uq", "w_ukv", "w_o_gm", "w_o_mem", "w_ff1"}
EARLY = ["w_in", "w_uq", "w_ukv", "w_mem_kv"]
LATE_PROJ = ["w_o_gm", "w_o_mla", "w_o_mem", "w_out"]
LATE_FF = ["w_ff1", "w_ff2"]
LATE = LATE_PROJ + LATE_FF
SMALL = ["w_spatial", "b_spatial", "g_mix", "g_cq", "g_ckv", "g_q_nope", "g_q_pe", "g_k_nope", "g_k_pe", "g_gm_ln",
         "b_gm_ln", "g_mem", "g_mq", "g_mk", "g_ffn"]
SMALL_SHAPE = {"g_mix": (1, 1024), "g_cq": (1, 384), "g_ckv": (1, 256), "g_q_nope": (1, 128), "g_q_pe": (1, 64),
               "g_k_nope": (1, 128), "g_k_pe": (1, 64), "g_gm_ln": (1, 512), "b_gm_ln": (1, 512),
               "w_spatial": (1, 4, 128, 128), "b_spatial": (1, 4, 128), "g_mem": (1, 1024), "g_mq": (1, 128),
               "g_mk": (1, 128), "g_ffn": (1, 1024)}
WEIGHTS = ['g_mix', 'w_in', 'g_cq', 'w_uq', 'g_ckv', 'w_ukv', 'g_q_nope', 'g_q_pe', 'g_k_nope', 'g_k_pe',
           'g_gm_ln', 'b_gm_ln', 'w_spatial', 'b_spatial', 'g_mem', 'w_mem_kv', 'g_mq', 'g_mk', 'w_o_gm',
           'w_o_mla', 'w_o_mem', 'w_out', 'g_ffn', 'w_ff1', 'w_ff2']


def _params(sem=None):
    return pltpu.CompilerParams(vmem_limit_bytes=VMEM_LIMIT, dimension_semantics=sem)


def _pick(n, prefs):
    for p in prefs:
        if n % p == 0:
            return p
    return n


def _full(shape):
    nd = len(shape)
    return pl.BlockSpec(shape, lambda *_: (0,) * nd)


def _rows(t, w, blk=0):
    return pl.BlockSpec((t, w), lambda i: (i, blk))


def _acc(ref, val, first):
    @pl.when(first)
    def _():
        ref[...] = val

    @pl.when(jnp.logical_not(first))
    def _():
        ref[...] += val


ANY = pl.BlockSpec(memory_space=pl.ANY)


class _Phase:
    def __init__(self, operands, out_shapes, n_sem, n_local, copies, aliases=None):
        self.operands, self.out_shapes, self.aliases = list(operands), list(out_shapes), dict(aliases or {})
        self.n_sem, self.n_local, self.copies = n_sem, max(n_local, 1), copies

    def sem_shapes(self):
        return [pltpu.SemaphoreType.DMA((self.n_sem,)), pltpu.SemaphoreType.DMA((self.n_sem,)),
                pltpu.SemaphoreType.DMA((self.n_local,))]

    def start(self, ins, outs, send, recv, local):
        sends, _, locals_ = self.copies(ins, outs, send, recv, local)
        for cp in locals_ + sends:
            cp.start()

    def finish(self, ins, outs, send, recv, local):
        sends, arrivals, locals_ = self.copies(ins, outs, send, recv, local)
        for cp in arrivals:
            cp.wait_recv()
        for cp in sends:
            cp.wait_send()
        for cp in locals_:
            cp.wait()


class _Shifted:
    def __init__(self, ref, base):
        self.ref, self.base = ref, base

    @property
    def at(self):
        return self

    def __getitem__(self, i):
        return self.ref.at[i + self.base]


def _together(first, second):
    n_in, n_out = len(first.operands), len(first.out_shapes)

    def copies(ins, outs, send, recv, local):
        a = first.copies(ins[:n_in], outs[:n_out], send, recv, local)
        b = second.copies(ins[n_in:], outs[n_out:], _Shifted(send, first.n_sem), _Shifted(recv, first.n_sem),
                          _Shifted(local, first.n_local))
        return a[0] + b[0], a[1] + b[1], a[2] + b[2]

    aliases = {**first.aliases, **{n_in + i: n_out + j for i, j in second.aliases.items()}}
    return _Phase(first.operands + second.operands, first.out_shapes + second.out_shapes, first.n_sem + second.n_sem,
                  first.n_local + second.n_local, copies, aliases)


def _run_phase(phase, name):
    n_in = len(phase.operands)

    def body(*refs):
        ins, outs, sems = refs[:n_in], refs[n_in:n_in + len(phase.out_shapes)], refs[n_in + len(phase.out_shapes):]
        phase.start(ins, outs, *sems)
        phase.finish(ins, outs, *sems)

    return pl.pallas_call(body, in_specs=[ANY] * n_in, out_specs=[ANY] * len(phase.out_shapes),
                          out_shape=phase.out_shapes, scratch_shapes=phase.sem_shapes(),
                          input_output_aliases=phase.aliases, name=name)(*phase.operands)


def _pcall(body, *, grid, in_specs, out_specs, out_shape, scratch_shapes=(), sem=None, name, comm=None, aliases=None):
    single = not isinstance(out_shape, (list, tuple))
    o_specs = [out_specs] if single else list(out_specs)
    o_shape = [out_shape] if single else list(out_shape)
    aliases = dict(aliases or {})
    if comm is None:
        call = pl.pallas_call(body, grid=grid, in_specs=list(in_specs), out_specs=o_specs, out_shape=o_shape,
                              scratch_shapes=list(scratch_shapes), input_output_aliases=aliases,
                              compiler_params=_params(sem), name=name)

        def run_plain(*args):
            res = call(*args)
            return res[0] if single else res

        return run_plain

    n_in, n_out, n_scr = len(in_specs), len(o_specs), len(scratch_shapes)
    nc_in, nc_out = len(comm.operands), len(comm.out_shapes)

    def wrapped(*refs):
        ins, cins = refs[:n_in], refs[n_in:n_in + nc_in]
        o0 = n_in + nc_in
        outs, couts = refs[o0:o0 + n_out], refs[o0 + n_out:o0 + n_out + nc_out]
        s0 = o0 + n_out + nc_out
        scr, csem = refs[s0:s0 + n_scr], refs[s0 + n_scr:]
        ids = [pl.program_id(d) for d in range(len(grid))]
        first = functools.reduce(jnp.logical_and, [i == 0 for i in ids])
        last = functools.reduce(jnp.logical_and, [i == g - 1 for i, g in zip(ids, grid)])

        @pl.when(first)
        def _():
            comm.start(cins, couts, *csem)

        body(*ins, *outs, *scr)

        @pl.when(last)
        def _():
            comm.finish(cins, couts, *csem)

    call = pl.pallas_call(
        wrapped, grid=grid, in_specs=list(in_specs) + [ANY] * nc_in, out_specs=o_specs + [ANY] * nc_out,
        out_shape=o_shape + comm.out_shapes, scratch_shapes=list(scratch_shapes) + comm.sem_shapes(),
        input_output_aliases={**aliases, **{n_in + i: n_out + j for i, j in comm.aliases.items()}},
        compiler_params=_params(("arbitrary",) * len(grid)), name=name)

    def run_carrying(*args):
        res = call(*args, *comm.operands)
        return (res[0] if single else res[:n_out]), res[n_out:]

    return run_carrying


def _dn(a, b, ca, cb):
    return lax.dot_general(a.astype(BF), b.astype(BF), (((ca,), (cb,)), ((), ())), preferred_element_type=F32)


@jax.custom_vjp
def _mm_nn(a, b):
    return _dn(a, b, 1, 0)


def _mm_nn_fwd(a, b):
    return _dn(a, b, 1, 0), (a.astype(BF), b.astype(BF))


def _mm_nn_bwd(res, ct):
    a, b = res
    return _dn(ct, b, 1, 1), _dn(a, ct, 0, 0)


_mm_nn.defvjp(_mm_nn_fwd, _mm_nn_bwd)


@jax.custom_vjp
def _mm_nt(a, b):
    return _dn(a, b, 1, 1)


def _mm_nt_fwd(a, b):
    return _dn(a, b, 1, 1), (a.astype(BF), b.astype(BF))


def _mm_nt_bwd(res, ct):
    a, b = res
    return _dn(ct, b, 1, 0), _dn(ct, a, 0, 0)


_mm_nt.defvjp(_mm_nt_fwd, _mm_nt_bwd)


def _rmsn(x, g, n):
    ms = jnp.sum(x * x, axis=-1, keepdims=True) * (1.0 / n)
    return x * lax.rsqrt(ms + EPS) * g


def _layernorm(x, g, b):
    mu = jnp.mean(x, axis=-1, keepdims=True)
    xc = x - mu
    y = xc * lax.rsqrt(jnp.mean(xc * xc, axis=-1, keepdims=True) + EPS)
    return y * g + b


def _swap_lanes(x):
    half = MLA_ROPE // 2
    lane = lax.broadcasted_iota(jnp.int32, x.shape, 1)
    return jnp.where(lane < half, pltpu.roll(x, LANES - half, axis=1),
                     jnp.where(lane < MLA_ROPE, pltpu.roll(x, half, axis=1), 0.0))


@jax.custom_vjp
def _swap_halves(x):
    return _swap_lanes(x)


_swap_halves.defvjp(lambda x: (_swap_lanes(x), None), lambda _, ct: (_swap_lanes(ct),))


def _rope(x, cos_f, sin_s):
    return x * cos_f + _swap_halves(x) * sin_s


def _lane_blocks(x):
    return tuple(x[:, i * LANES:(i + 1) * LANES] for i in range(x.shape[1] // LANES))


@jax.custom_vjp
def _split_lanes(x):
    return _lane_blocks(x)


_split_lanes.defvjp(lambda x: (_lane_blocks(x), None), lambda _, cts: (jnp.concatenate(cts, axis=1),))


def _softmax(s):
    m = lax.stop_gradient(jnp.max(s, axis=-1, keepdims=True))
    p = jnp.exp(s - m)
    return p / jnp.sum(p, axis=-1, keepdims=True)


def _mm(a, b, *, ta=False, tb=False, ins=(), row_ins=(), epilogue=None, out_dtypes=(F32,), owner_cols=None,
        total=False, name, comm=None, rows=None, into=None):
    if ta:
        k_dim, m = a.shape
    else:
        m, k_dim = a.shape
    if tb:
        n, kb = b.shape
    else:
        kb, n = b.shape
    assert k_dim == kb, (a.shape, b.shape, ta, tb)
    part, n_parts = rows if rows is not None else (0, 1)
    tm = _pick(m // n_parts, (1024, 512, 256, 128))
    tn = _pick(n if owner_cols is None else owner_cols, (1024, 768, 512, 384, 256, 128))
    tk = _pick(k_dim, (2048, 1024, 768, 512, 256, 128))
    nk = k_dim // tk
    m_steps = m // tm // n_parts
    off = part * m_steps
    ca = 0 if ta else 1
    cb = 1 if tb else 0
    n_in = len(ins) + len(row_ins)
    n_out = len(out_dtypes)
    n_pass = 0 if into is None else 1
    total_shape = total if isinstance(total, tuple) else (8, LANES)
    assert not (isinstance(total, tuple) and n != tn), "a per-column total needs the whole width in one tile"

    def finish(r, in_refs, out_refs, first_tile):
        vals = epilogue(r, *[ref[...].astype(F32) for ref in in_refs]) if epilogue is not None else (r,)
        for ref, val, dt in zip(out_refs, vals, out_dtypes):
            ref[...] = val.astype(dt)
        if total:
            _acc(out_refs[n_out], vals[n_out], first_tile)

    def body(*refs):
        a_ref, b_ref = refs[:2]
        in_refs = refs[2:2 + n_in]
        o0 = 2 + n_in + n_pass
        out_refs = refs[o0:o0 + n_out + int(bool(total))]
        first_tile = jnp.logical_and(pl.program_id(0) == 0, pl.program_id(1) == 0)
        part = _dn(a_ref[...], b_ref[...], ca, cb)
        if nk == 1:
            finish(part, in_refs, out_refs, first_tile)
            return
        acc = refs[-1]
        k = pl.program_id(2)
        _acc(acc, part, k == 0)

        @pl.when(k == nk - 1)
        def _():
            finish(acc[...], in_refs, out_refs, first_tile)

    a_spec = (pl.BlockSpec((tk, tm), lambda i, j, k: (k, i + off)) if ta
              else pl.BlockSpec((tm, tk), lambda i, j, k: (i + off, k)))
    b_spec = pl.BlockSpec((tn, tk), lambda i, j, k: (j, k)) if tb else pl.BlockSpec((tk, tn), lambda i, j, k: (k, j))
    t_spec = pl.BlockSpec((tm, tn), lambda i, j, k: (i + off, j))
    if owner_cols is None:
        o_spec, o_shape = t_spec, (m, n)
    else:
        per = owner_cols // tn
        o_spec = pl.BlockSpec((None, tm, tn), lambda i, j, k: (j // per, i + off, j % per))
        o_shape = (n // owner_cols, m, owner_cols)
    o_specs = [o_spec] * n_out + ([pl.BlockSpec(total_shape, lambda i, j, k: (0, 0))] if total else [])
    o_shapes = [SDS(o_shape, dt) for dt in out_dtypes] + ([SDS(total_shape, F32)] if total else [])
    row_spec = pl.BlockSpec((1, tn), lambda i, j, k: (0, j))
    in_specs = [a_spec, b_spec] + [t_spec] * len(ins) + [row_spec] * len(row_ins) + [ANY] * n_pass
    args = [a, b, *ins, *row_ins] + ([into] if n_pass else [])
    run = _pcall(body, grid=(m_steps, n // tn, nk), in_specs=in_specs, out_specs=o_specs, out_shape=o_shapes,
                 scratch_shapes=[pltpu.VMEM((tm, tn), F32)] if nk > 1 else [],
                 sem=("arbitrary",) * 3 if total else ("parallel", "parallel", "arbitrary"), name=name, comm=comm,
                 aliases={len(in_specs) - 1: 0} if n_pass else None)
    if comm is None:
        outs = run(*args)
        return outs[0] if len(outs) == 1 else outs
    outs, exchanged = run(*args)
    return (outs[0] if len(outs) == 1 else outs), exchanged


def _add_to(r, x):
    return (r + x,)


def _residual_rms(r, x, g):
    x1 = r + x
    return x1, _rmsn(x1, g, D_MODEL)


def _rms_bwd_tail(dh, x, res, g):
    _, vjp = jax.vjp(lambda xx, gg: _rmsn(xx, gg, D_MODEL), x, g)
    dx, dg = vjp(dh)
    dx = dx + res
    return dx, dx, dg


def _relu2(r):
    p = jnp.maximum(r, 0.0)
    return r, p * p


def _relu2_bwd(dr, a):
    return (dr * (2.0 * jnp.maximum(a, 0.0)),)


def _loss_tail(r, x1, tgt):
    e = (r + x1) - tgt
    dy = e * (1.0 / D_MODEL)
    part = jnp.sum(jnp.sum(e * e, axis=-1, keepdims=True), axis=0, keepdims=True) * (0.5 / D_MODEL)
    return dy, dy, jnp.broadcast_to(part, (8, LANES))


def _rms_fwd(x, g, name, comm=None):
    n, w = x.shape
    t = min(ROW_TILE, n)

    def body(x_ref, g_ref, o_ref):
        o_ref[...] = _rmsn(x_ref[...], g_ref[...], w).astype(BF)

    return _pcall(body, grid=(n // t,), in_specs=[_rows(t, w), _full((1, w))], out_specs=_rows(t, w),
                  out_shape=SDS((n, w), BF), sem=("arbitrary",), name=name, comm=comm)(x, g)


def _rms_bwd(x, g, dh, res, name, comm=None):
    n, w = x.shape
    t = min(ROW_TILE, n)
    has_res = res is not None

    def body(*refs):
        if has_res:
            x_ref, g_ref, dh_ref, res_ref, dx_ref, dxb_ref, dg_ref = refs
        else:
            x_ref, g_ref, dh_ref, dx_ref, dxb_ref, dg_ref = refs
        _, vjp = jax.vjp(lambda xx, gg: _rmsn(xx, gg, w), x_ref[...], g_ref[...])
        dx, dg = vjp(dh_ref[...])
        if has_res:
            dx = dx + res_ref[...]
        dx_ref[...] = dx
        dxb_ref[...] = dx.astype(BF)
        _acc(dg_ref, dg, pl.program_id(0) == 0)

    in_specs = [_rows(t, w), _full((1, w)), _rows(t, w)] + ([_rows(t, w)] if has_res else [])
    args = [x, g, dh] + ([res] if has_res else [])
    return _pcall(body, grid=(n // t,), in_specs=in_specs, out_specs=[_rows(t, w), _rows(t, w), _full((1, w))],
                  out_shape=[SDS((n, w), F32), SDS((n, w), BF), SDS((1, w), F32)], sem=("arbitrary",), name=name,
                  comm=comm)(*args)


def _merge_core(zg0, zg1, zg2, y0, y1, y2):
    return jax.nn.sigmoid(zg0) * y0 + jax.nn.sigmoid(zg1) * y1 + jax.nn.sigmoid(zg2) * y2


def _merge_fwd(z, y_gm, y_mla, y_mem, name):
    n = z.shape[0]
    t = min(ROW_TILE, n)
    w = D_MODEL

    def body(g0, g1, g2, y0, y1, y2, o_ref):
        o_ref[...] = _merge_core(g0[...].astype(F32), g1[...].astype(F32), g2[...].astype(F32), y0[...].astype(F32), y1[...].astype(F32),
                                 y2[...].astype(F32)).astype(BF)

    return pl.pallas_call(body, grid=(n // t,),
                          in_specs=[_rows(t, w, 0), _rows(t, w, 1), _rows(t, w, 2)] + [_rows(t, w)] * 3,
                          out_specs=_rows(t, w), out_shape=SDS((n, w), BF),
                          compiler_params=_params(("parallel",)), name=name)(z, z, z, y_gm, y_mla, y_mem)


def _merge_bwd(z, y_gm, y_mla, y_mem, dmerged, name):
    n = z.shape[0]
    t = min(ROW_TILE, n)
    w = D_MODEL

    def body(g0, g1, g2, y0, y1, y2, dm, dzg_ref, d0_ref, d1_ref, d2_ref):
        _, vjp = jax.vjp(_merge_core, g0[...].astype(F32), g1[...].astype(F32), g2[...].astype(F32), y0[...].astype(F32), y1[...].astype(F32),
                         y2[...].astype(F32))
        dg0, dg1, dg2, dy0, dy1, dy2 = vjp(dm[...])
        dzg_ref[:, 0:w] = dg0.astype(BF)
        dzg_ref[:, w:2 * w] = dg1.astype(BF)
        dzg_ref[:, 2 * w:3 * w] = dg2.astype(BF)
        d0_ref[...] = dy0.astype(BF)
        d1_ref[...] = dy1.astype(BF)
        d2_ref[...] = dy2.astype(BF)

    return pl.pallas_call(body, grid=(n // t,),
                          in_specs=[_rows(t, w, 0), _rows(t, w, 1), _rows(t, w, 2)] + [_rows(t, w)] * 4,
                          out_specs=[_rows(t, 3 * w, ZG // (3 * w))] + [_rows(t, w)] * 3,
                          out_shape=[SDS((n, Z_COLS), BF)] + [SDS((n, w), BF)] * 3,
                          compiler_params=_params(("parallel",)), name=name)(z, z, z, y_gm, y_mla, y_mem, dmerged)


def _gm_core(zu, zv, g_ln, b_ln, ws, bcols):
    t = zu.shape[0]
    u = jax.nn.gelu(zu)
    v = _layernorm(jax.nn.gelu(zv), g_ln, b_ln)
    row = lax.broadcasted_iota(jnp.int32, (GM_CHUNK, GM_CHUNK), 0)
    col = lax.broadcasted_iota(jnp.int32, (GM_CHUNK, GM_CHUNK), 1)
    wc = [jnp.where(row >= col, ws[g], 0.0) for g in range(GM_GROUPS)]
    chunks = []
    for c in range(t // GM_CHUNK):
        cols = []
        for g in range(GM_GROUPS):
            vc = v[c * GM_CHUNK:(c + 1) * GM_CHUNK, g * LANES:(g + 1) * LANES]
            cols.append(_mm_nn(wc[g], vc) + bcols[g])
        chunks.append(jnp.concatenate(cols, axis=1))
    mixed = chunks[0] if len(chunks) == 1 else jnp.concatenate(chunks, axis=0)
    return u * mixed


def _gm_specs(t):
    return [_rows(t, GM_WIDTH, ZU // GM_WIDTH), _rows(t, GM_WIDTH, ZV // GM_WIDTH), _full((1, GM_WIDTH)),
            _full((1, GM_WIDTH)), _full((GM_GROUPS, GM_CHUNK, GM_CHUNK))] + [_full((GM_CHUNK, 1))] * GM_GROUPS


def _gm_fwd(z, g_ln, b_ln, ws, bcols, name):
    n = z.shape[0]
    t = min(ROW_TILE, n)

    def body(zu, zv, g_ref, b_ref, ws_ref, c0, c1, c2, c3, o_ref):
        out = _gm_core(zu[...].astype(F32), zv[...].astype(F32), g_ref[...], b_ref[...], [ws_ref[g] for g in range(GM_GROUPS)],
                       [c0[...], c1[...], c2[...], c3[...]])
        o_ref[...] = out.astype(BF)

    return pl.pallas_call(body, grid=(n // t,), in_specs=_gm_specs(t), out_specs=_rows(t, GM_WIDTH),
                          out_shape=SDS((n, GM_WIDTH), BF), compiler_params=_params(("parallel",)),
                          name=name)(z, z, g_ln, b_ln, ws, *bcols)


def _gm_bwd(z, g_ln, b_ln, ws, bcols, dgm, dz, name, comm=None):
    n = z.shape[0]
    t = min(ROW_TILE, n)

    def body(zu, zv, g_ref, b_ref, ws_ref, c0, c1, c2, c3, dgm_ref, _, dz_ref, dg_ref, db_ref, dws_ref, e0, e1, e2,
             e3):
        first = pl.program_id(0) == 0
        _, vjp = jax.vjp(_gm_core, zu[...].astype(F32), zv[...].astype(F32), g_ref[...], b_ref[...],
                         [ws_ref[g] for g in range(GM_GROUPS)], [c0[...], c1[...], c2[...], c3[...]])
        dzu, dzv, dg, db, dws, dcols = vjp(dgm_ref[...])
        dz_ref[:, 0:GM_WIDTH] = dzu.astype(BF)
        dz_ref[:, GM_WIDTH:2 * GM_WIDTH] = dzv.astype(BF)
        _acc(dg_ref, dg, first)
        _acc(db_ref, db, first)
        _acc(dws_ref, jnp.stack(dws, axis=0), first)
        for ref, val in zip((e0, e1, e2, e3), dcols):
            _acc(ref, val, first)

    in_specs = _gm_specs(t) + [_rows(t, GM_WIDTH), ANY]
    return _pcall(
        body, grid=(n // t,), in_specs=in_specs,
        out_specs=[_rows(t, 2 * GM_WIDTH, ZU // (2 * GM_WIDTH)), _full((1, GM_WIDTH)), _full((1, GM_WIDTH)),
                   _full((GM_GROUPS, GM_CHUNK, GM_CHUNK))] + [_full((GM_CHUNK, 1))] * GM_GROUPS,
        out_shape=[SDS((n, Z_COLS), BF), SDS((1, GM_WIDTH), F32), SDS((1, GM_WIDTH), F32),
                   SDS((GM_GROUPS, GM_CHUNK, GM_CHUNK), F32)] + [SDS((GM_CHUNK, 1), F32)] * GM_GROUPS,
        sem=("arbitrary",), name=name, comm=comm, aliases={len(in_specs) - 1: 0})(z, z, g_ln, b_ln, ws, *bcols, dgm, dz)


def _rope_tables(pos_f, inv_full, cmask, smask, name, comm=None):
    n = pos_f.shape[0]
    t = min(ROW_TILE, n)

    def body(p_ref, inv_ref, cm_ref, sm_ref, cos_ref, sin_ref):
        ang = p_ref[...] * inv_ref[...]
        cos_ref[...] = jnp.cos(ang) * cm_ref[...]
        sin_ref[...] = jnp.sin(ang) * sm_ref[...]

    return _pcall(body, grid=(n // t,), in_specs=[_rows(t, 1)] + [_full((1, LANES))] * 3,
                  out_specs=[_rows(t, LANES)] * 2, out_shape=[SDS((n, LANES), F32)] * 2, sem=("parallel",),
                  name=name, comm=comm)(pos_f, inv_full, cmask, smask)


def _prep_norms(cq, ckv, g_cq, g_ckv):
    return _rmsn(cq, g_cq, Q_LORA), _rmsn(ckv, g_ckv, KV_LORA)


def _prep_heads(qa, kva, kpe, head_gains, cos_f, sin_s):
    g_qn, g_qp, g_kn, g_kp = head_gains
    qs = _split_lanes(qa)
    kvs = _split_lanes(kva)
    kp = _rope(_rmsn(kpe, g_kp, MLA_ROPE), cos_f, sin_s)
    q_out, k_out = [], []
    for h in range(MLA_HEADS):
        q_out.append(_rmsn(qs[h], g_qn, MLA_NOPE))
        q_out.append(_rope(_rmsn(qs[MLA_HEADS + h], g_qp, MLA_ROPE), cos_f, sin_s))
        k_out.append(_rmsn(kvs[h], g_kn, MLA_NOPE))
        k_out.append(kp)
    return (jnp.concatenate(q_out, axis=1), jnp.concatenate(k_out, axis=1),
            jnp.concatenate(kvs[MLA_HEADS:], axis=1))


def _prep_in_specs(t):
    return ([_rows(t, Q_LORA, CQ // Q_LORA), _rows(t, LANES, KPE // LANES), _rows(t, KV_LORA, CKV // KV_LORA),
             _rows(t, LANES), _rows(t, LANES), _full((1, Q_LORA)), _full((1, KV_LORA))] + [_full((1, LANES))] * 4
            + [_full((Q_LORA, 2048)), _full((KV_LORA, 2048))])


def _prep_fwd(z, cos_f, sin_s, gains, wq, wkv, name):
    n = z.shape[0]
    t = min(ROW_TILE, n)

    def body(cq, kpe, ckv, cos_ref, sin_ref, g_cq, g_ckv, g_qn, g_qp, g_kn, g_kp, wq_ref, wkv_ref, q_ref, k_ref, v_ref):
        cqn, ckvn = _prep_norms(cq[...].astype(F32), ckv[...].astype(F32), g_cq[...], g_ckv[...])
        qa = _dn(cqn, wq_ref[...], 1, 0)
        kva = _dn(ckvn, wkv_ref[...], 1, 0)
        q, k, v = _prep_heads(qa, kva, kpe[...].astype(F32), (g_qn[...], g_qp[...], g_kn[...], g_kp[...]), cos_ref[...],
                              sin_ref[...])
        q_ref[...] = q.astype(BF)
        k_ref[...] = k.astype(BF)
        v_ref[...] = v.astype(BF)

    return pl.pallas_call(body, grid=(n // t,), in_specs=_prep_in_specs(t),
                          out_specs=[_rows(t, 2048), _rows(t, 2048), _rows(t, 1024)],
                          out_shape=[SDS((n, 2048), BF), SDS((n, 2048), BF), SDS((n, 1024), BF)],
                          compiler_params=_params(("parallel",)),
                          name=name)(z, z, z, cos_f, sin_s, *gains, wq, wkv)


def _prep_bwd(z, cos_f, sin_s, gains, wq, wkv, dq, dk, dv, dz, name, comm=None):
    n = z.shape[0]
    t = min(ROW_TILE, n)
    wz = Q_LORA + LANES + KV_LORA

    def body(cq, kpe, ckv, cos_ref, sin_ref, g_cq, g_ckv, g_qn, g_qp, g_kn, g_kp, wq_ref, wkv_ref, dq_ref, dk_ref,
             dv_ref, _, dz_ref, o_cq, o_ckv, o_qn, o_qp, o_kn, o_kp, dwq_ref, dwkv_ref):
        first = pl.program_id(0) == 0
        cos_t, sin_t = cos_ref[...], sin_ref[...]
        (cqn, ckvn), vjp_norms = jax.vjp(_prep_norms, cq[...].astype(F32), ckv[...].astype(F32), g_cq[...], g_ckv[...])
        wq_t, wkv_t = wq_ref[...], wkv_ref[...]
        qa = _dn(cqn, wq_t, 1, 0)
        kva = _dn(ckvn, wkv_t, 1, 0)
        _, vjp_heads = jax.vjp(lambda a, b, c, g: _prep_heads(a, b, c, g, cos_t, sin_t), qa, kva, kpe[...].astype(F32),
                               (g_qn[...], g_qp[...], g_kn[...], g_kp[...]))
        dqa, dkva, dkpe, dhead = vjp_heads((dq_ref[...], dk_ref[...], dv_ref[...]))
        _acc(dwq_ref, _dn(cqn, dqa, 0, 0), first)
        _acc(dwkv_ref, _dn(ckvn, dkva, 0, 0), first)
        dcq, dckv, dg_cq, dg_ckv = vjp_norms((_dn(dqa, wq_t, 1, 1), _dn(dkva, wkv_t, 1, 1)))
        dz_ref[:, 0:Q_LORA] = dcq.astype(BF)
        dz_ref[:, Q_LORA:Q_LORA + LANES] = dkpe.astype(BF)
        dz_ref[:, Q_LORA + LANES:wz] = dckv.astype(BF)
        for ref, val in zip((o_cq, o_ckv, o_qn, o_qp, o_kn, o_kp), (dg_cq, dg_ckv) + tuple(dhead)):
            _acc(ref, val, first)

    gain_specs = [_full((1, Q_LORA)), _full((1, KV_LORA))] + [_full((1, LANES))] * 4
    gain_shapes = [SDS((1, Q_LORA), F32), SDS((1, KV_LORA), F32)] + [SDS((1, LANES), F32)] * 4
    in_specs = _prep_in_specs(t) + [_rows(t, 2048), _rows(t, 2048), _rows(t, 1024), ANY]
    return _pcall(
        body, grid=(n // t,), in_specs=in_specs,
        out_specs=[_rows(t, wz, CQ // wz)] + gain_specs + [_full((Q_LORA, 2048)), _full((KV_LORA, 2048))],
        out_shape=[SDS((n, Z_COLS), BF)] + gain_shapes + [SDS((Q_LORA, 2048), F32), SDS((KV_LORA, 2048), F32)],
        sem=("arbitrary",), name=name, comm=comm,
        aliases={len(in_specs) - 1: 0})(z, z, z, cos_f, sin_s, *gains, wq, wkv, dq, dk, dv, dz)


MLA_QK = 256
MLA_SCALE = 1.0 / math.sqrt(MLA_NOPE + MLA_ROPE)
LOG2E = 1.0 / math.log(2.0)
MLA_SCALE_LOG2E = MLA_SCALE * LOG2E


def _causal_mask(s, q0, k0):
    tq, tk = s.shape
    row = q0 + lax.broadcasted_iota(jnp.int32, (tq, tk), 0)
    col = k0 + lax.broadcasted_iota(jnp.int32, (tq, tk), 1)
    return jnp.where(row >= col, s, -jnp.inf)


def _mla_fwd(q, k, v, batch, seq, name, comm=None):
    n = q.shape[0]
    tq = min(ATT_TILE, seq)
    nq = seq // tq

    nh = ATT_HEADS_FWD

    def body(q_ref, k_ref, v_ref, o_ref, lse_ref):
        i = pl.program_id(2)

        def step(j, carry, diagonal=False):
            k0 = pl.multiple_of(j * tq, tq)
            out = []
            ones = jnp.ones((tq, LANES), BF)
            for hh in range(nh):
                m, acc = carry[hh]
                qb = q_ref[:, hh * MLA_QK:(hh + 1) * MLA_QK]
                kb = k_ref[pl.ds(k0, tq), hh * MLA_QK:(hh + 1) * MLA_QK]
                vb = v_ref[pl.ds(k0, tq), hh * MLA_V:(hh + 1) * MLA_V]
                s = _dn(qb, kb, 1, 1)
                if diagonal:
                    s = _causal_mask(s, i * tq, k0)
                m_new = jnp.maximum(m, jnp.max(s, axis=-1, keepdims=True))
                p = jnp.exp2((s - m_new) * MLA_SCALE_LOG2E)
                alpha = jnp.exp2((m - m_new) * MLA_SCALE_LOG2E)
                acc = alpha * acc + _dn(p, jnp.concatenate([vb, ones], axis=1), 1, 0)
                out.append((m_new, acc))
            return tuple(out)

        init = tuple((jnp.full((tq, 1), -jnp.inf, F32), jnp.zeros((tq, MLA_V + LANES), F32)) for _ in range(nh))
        final = step(i, lax.fori_loop(0, i, step, init), diagonal=True)
        for hh, (m, acc) in enumerate(final):
            l = acc[:, MLA_V:MLA_V + 1]
            o_ref[:, hh * MLA_V:(hh + 1) * MLA_V] = acc[:, :MLA_V] / l
            lse_ref[:, hh * LANES:(hh + 1) * LANES] = jnp.broadcast_to(m * MLA_SCALE + jnp.log(l), (tq, LANES))

    return _pcall(
        body, grid=(batch, MLA_HEADS // nh, nq),
        in_specs=[pl.BlockSpec((tq, nh * MLA_QK), lambda b, h, i: (b * nq + i, h)),
                  pl.BlockSpec((seq, nh * MLA_QK), lambda b, h, i: (b, h)),
                  pl.BlockSpec((seq, nh * MLA_V), lambda b, h, i: (b, h))],
        out_specs=[pl.BlockSpec((tq, nh * MLA_V), lambda b, h, i: (b * nq + i, h)),
                   pl.BlockSpec((tq, nh * LANES), lambda b, h, i: (b * nq + i, h))],
        out_shape=[SDS((n, MLA_HEADS * MLA_V), F32), SDS((n, MLA_HEADS * LANES), F32)],
        sem=("parallel", "parallel", "arbitrary"), name=name, comm=comm)(q, k, v)


def _mla_bwd(q, k, v, o, lse, do, batch, seq, name, comm=None):
    n = q.shape[0]
    tk = min(ATT_TILE, seq)
    nk = seq // tk

    nh = ATT_HEADS

    def body(q_ref, k_ref, v_ref, o_ref, lse_ref, do_ref, dq_ref, dk_ref, dv_ref):
        jk = pl.program_id(2)

        @pl.when(jk == 0)
        def _():
            dq_ref[...] = jnp.zeros_like(dq_ref)

        def step(i, carry, diagonal=False):
            q0 = pl.multiple_of(i * tk, tk)
            rows = pl.ds(q0, tk)
            out = []
            for hh in range(nh):
                dk_acc, dv_acc = carry[hh]
                qk_cols = slice(hh * MLA_QK, (hh + 1) * MLA_QK)
                v_cols = slice(hh * MLA_V, (hh + 1) * MLA_V)
                kb = k_ref[:, qk_cols]
                vb = v_ref[:, v_cols]
                qb = q_ref[rows, qk_cols]
                dob = do_ref[rows, v_cols]
                delta = jnp.sum(dob * o_ref[rows, v_cols], axis=-1, keepdims=True)
                s = _dn(qb, kb, 1, 1)
                if diagonal:
                    s = _causal_mask(s, q0, jk * tk)
                p = jnp.exp2(s * MLA_SCALE_LOG2E - lse_ref[rows, hh * LANES:hh * LANES + 1] * LOG2E)
                dv_acc = dv_acc + _dn(p, dob, 0, 0)
                dp = _dn(dob, vb, 1, 1)
                ds = p * (dp - delta) * MLA_SCALE
                dk_acc = dk_acc + _dn(ds, qb, 0, 0)
                dq_ref[rows, qk_cols] += _dn(ds, kb, 1, 0)
                out.append((dk_acc, dv_acc))
            return tuple(out)

        init = tuple((jnp.zeros((tk, MLA_QK), F32), jnp.zeros((tk, MLA_V), F32)) for _ in range(nh))
        final = lax.fori_loop(jk + 1, nk, step, step(jk, init, diagonal=True))
        for hh, (dk_acc, dv_acc) in enumerate(final):
            dk_ref[:, hh * MLA_QK:(hh + 1) * MLA_QK] = dk_acc
            dv_ref[:, hh * MLA_V:(hh + 1) * MLA_V] = dv_acc

    full_qk = pl.BlockSpec((seq, nh * MLA_QK), lambda b, h, j: (b, h))
    full_v = pl.BlockSpec((seq, nh * MLA_V), lambda b, h, j: (b, h))
    blk_qk = pl.BlockSpec((tk, nh * MLA_QK), lambda b, h, j: (b * nk + j, h))
    blk_v = pl.BlockSpec((tk, nh * MLA_V), lambda b, h, j: (b * nk + j, h))
    return _pcall(
        body, grid=(batch, MLA_HEADS // nh, nk),
        in_specs=[full_qk, blk_qk, blk_v, full_v, full_v, full_v],
        out_specs=[full_qk, blk_qk, blk_v],
        out_shape=[SDS((n, MLA_HEADS * MLA_QK), F32), SDS((n, MLA_HEADS * MLA_QK), F32),
                   SDS((n, MLA_HEADS * MLA_V), F32)],
        sem=("parallel", "parallel", "arbitrary"), name=name, comm=comm)(q, k, v, o, lse, do)


MEM_SCALE = 1.0 / math.sqrt(HEAD_DIM)
MEM_W = MEM_HEADS * HEAD_DIM


def _mem_core(qs, ks, vs, g_mq, g_mk):
    outs = []
    for h in range(MEM_HEADS):
        qh = _rmsn(qs[h], g_mq, HEAD_DIM)
        kh = _rmsn(ks[h], g_mk, HEAD_DIM)
        p = _softmax(_mm_nt(qh, kh) * MEM_SCALE)
        outs.append(_mm_nn(p, vs[h]))
    return jnp.concatenate(outs, axis=1)


def _mem_load(qm, kvm, g_mq, g_mk):
    hs = range(MEM_HEADS)
    qs = [qm[:, h * LANES:(h + 1) * LANES].astype(F32) for h in hs]
    ks = [kvm[:, h * LANES:(h + 1) * LANES] for h in hs]
    vs = [kvm[:, MEM_W + h * LANES:MEM_W + (h + 1) * LANES] for h in hs]
    return qs, ks, vs, g_mq[...], g_mk[...]


def _mem_fwd(z, kvm, g_mq, g_mk, batch, seq, name, comm=None):
    n = z.shape[0]
    t = min(ROW_TILE, seq)
    per = seq // t

    def body(qm, kvm_ref, gq, gk, o_ref):
        o_ref[...] = _mem_core(*_mem_load(qm, kvm_ref, gq, gk)).astype(BF)

    return _pcall(
        body, grid=(n // t,),
        in_specs=[_rows(t, MEM_W, QM // MEM_W), pl.BlockSpec((MEM_LEN, 2 * MEM_W), lambda i: (i // per, 0)),
                  _full((1, LANES)), _full((1, LANES))],
        out_specs=_rows(t, MEM_W), out_shape=SDS((n, MEM_W), BF), sem=("parallel",), name=name,
        comm=comm)(z, kvm, g_mq, g_mk)


def _mem_bwd(z, kvm, g_mq, g_mk, dom, dz, batch, seq, name):
    n = z.shape[0]
    t = min(ROW_TILE, seq)
    per = seq // t

    def body(qm, kvm_ref, gq, gk, dom_ref, _, dz_ref, dkvm_ref, dgq_ref, dgk_ref):
        i = pl.program_id(0)
        _, vjp = jax.vjp(_mem_core, *_mem_load(qm, kvm_ref, gq, gk))
        dqs, dks, dvs, dgq, dgk = vjp(dom_ref[...])
        dz_ref[...] = jnp.concatenate(dqs, axis=1).astype(BF)
        _acc(dkvm_ref, jnp.concatenate(dks + dvs, axis=1), i % per == 0)
        _acc(dgq_ref, dgq, i == 0)
        _acc(dgk_ref, dgk, i == 0)

    kv_spec = pl.BlockSpec((MEM_LEN, 2 * MEM_W), lambda i: (i // per, 0))
    return pl.pallas_call(
        body, grid=(n // t,),
        in_specs=[_rows(t, MEM_W, QM // MEM_W), kv_spec, _full((1, LANES)), _full((1, LANES)), _rows(t, MEM_W), ANY],
        out_specs=[_rows(t, MEM_W, QM // MEM_W), kv_spec, _full((1, LANES)), _full((1, LANES))],
        out_shape=[SDS((n, Z_COLS), BF), SDS((batch * MEM_LEN, 2 * MEM_W), F32), SDS((1, LANES), F32),
                   SDS((1, LANES), F32)],
        input_output_aliases={5: 0},
        compiler_params=_params(("arbitrary",)), name=name)(z, kvm, g_mq, g_mk, dom, dz)


def _me():
    return lax.axis_index("x"), lax.axis_index("y"), lax.axis_index("c")


def _other_chips(x, y):
    return [(1 - x, y), (x, 1 - y), (1 - x, 1 - y)]


def _shard_shape(name):
    r, c = BIG_SHAPE[name]
    return (r, c // N_CHIPS) if name in COL_SHARDED else (r // N_CHIPS, c)


def _n_pieces(half_rows):
    for n in range(max(1, half_rows // PIECE_ROWS), 0, -1):
        if half_rows % n == 0 and (half_rows // n) % 16 == 0:
            return n
    return 1


def _piece_plan(shapes):
    plan = []
    for r, _ in shapes:
        h = r // 2
        n = _n_pieces(h)
        plan.append((h, n, h // n))
    return plan


def _remote(send, recv, sem, src, dst, to):
    return pltpu.make_async_remote_copy(src_ref=src, dst_ref=dst, send_sem=send.at[sem], recv_sem=recv.at[sem],
                                        device_id=to, device_id_type=MESH)


def _gather_far(shards):
    plan = _piece_plan([s.shape for s in shards])
    n_far = 3 * sum(n for _, n, _ in plan)
    n_loc = 2 * sum(n for _, n, _ in plan)

    def copies(s_refs, o_refs, send, recv, local):
        x, y, c = _me()
        k = 2 * x + y
        mine, sends, arrivals = [], [], []
        for t, (h, n, pr) in enumerate(plan):
            s_ref, o_ref = s_refs[t], o_refs[t]
            for core in range(2):
                for p in range(n):
                    rows = pl.ds(core * h + p * pr, pr)
                    mine.append(pltpu.make_async_copy(s_ref.at[rows], o_ref.at[k, rows], local.at[len(mine)]))
            for chip in _other_chips(x, y):
                for p in range(n):
                    rows = pl.ds(c * h + p * pr, pr)
                    s = len(sends)
                    sends.append(_remote(send, recv, s, s_ref.at[rows], o_ref.at[k, rows], (*chip, c)))
                    arrivals.append(_remote(send, recv, s, s_ref.at[rows], o_ref.at[2 * chip[0] + chip[1], rows],
                                            (*chip, c)))
        return sends, arrivals, mine

    return _Phase(shards, [SDS((N_CHIPS,) + s.shape, s.dtype) for s in shards], n_far, n_loc, copies)


def _gather_near(bufs):
    plan = _piece_plan([b.shape[1:] for b in bufs])
    n_sem = 3 * sum(n for _, n, _ in plan)

    def copies(i_refs, o_refs, send, recv, local):
        x, y, c = _me()
        sib = (x, y, 1 - c)
        sends, arrivals = [], []
        for t, (h, n, pr) in enumerate(plan):
            for chip in _other_chips(x, y):
                ci = 2 * chip[0] + chip[1]
                for p in range(n):
                    rows = pl.ds(c * h + p * pr, pr)
                    rows_sib = pl.ds((1 - c) * h + p * pr, pr)
                    s = len(sends)
                    sends.append(_remote(send, recv, s, i_refs[t].at[ci, rows], o_refs[t].at[ci, rows], sib))
                    arrivals.append(_remote(send, recv, s, i_refs[t].at[ci, rows_sib], o_refs[t].at[ci, rows_sib], sib))
        return sends, arrivals, []

    return _Phase(bufs, [SDS(b.shape, b.dtype) for b in bufs], n_sem, 0, copies, {t: t for t in range(len(bufs))})


def _pair_exchange(grads):
    plan = _piece_plan([g.shape[1:] for g in grads])
    n_sem = sum(n for _, n, _ in plan)

    def copies(g_refs, o_refs, send, recv, local):
        x, y, c = _me()
        sends = []
        for t, (h, n, pr) in enumerate(plan):
            for p in range(n):
                sends.append(_remote(send, recv, len(sends), g_refs[t].at[:, pl.ds((1 - c) * h + p * pr, pr)],
                                     o_refs[t].at[:, pl.ds(p * pr, pr)], (x, y, 1 - c)))
        return sends, sends, []

    return _Phase(grads, [SDS((N_CHIPS, g.shape[1] // 2, g.shape[2]), F32) for g in grads], n_sem, 0, copies)


def _pair_add(ck, g, theirs, name):
    _, r, c = g.shape
    (h, n, pr), = _piece_plan([(r, c)])

    def body(ck_ref, g_ref, t_ref, pbf_ref):
        pbf_ref[...] = (g_ref[...] + t_ref[...]).astype(BF)

    half = pl.BlockSpec((None, pr, c), lambda k, p, ck: (k, p, 0))
    spec = pltpu.PrefetchScalarGridSpec(
        num_scalar_prefetch=1, grid=(N_CHIPS, n),
        in_specs=[pl.BlockSpec((None, pr, c), lambda k, p, ck: (k, ck[0] * n + p, 0)), half], out_specs=half)
    return pl.pallas_call(body, grid_spec=spec, out_shape=SDS((N_CHIPS, h, c), BF),
                          compiler_params=_params(("arbitrary", "arbitrary")), name=name)(ck, g, theirs)


def _scatter_partials(pbfs):
    plan = [(h, _n_pieces(h), h // _n_pieces(h)) for h in [p.shape[1] for p in pbfs]]
    n_sem = 3 * sum(n for _, n, _ in plan)

    def copies(p_refs, o_refs, send, recv, local):
        x, y, c = _me()
        sends = []
        for t, (h, n, pr) in enumerate(plan):
            for j, chip in enumerate(_other_chips(x, y)):
                for p in range(n):
                    rows = pl.ds(p * pr, pr)
                    sends.append(_remote(send, recv, len(sends), p_refs[t].at[2 * chip[0] + chip[1], rows],
                                         o_refs[t].at[j, rows], (*chip, c)))
        return sends, sends, []

    return _Phase(pbfs, [SDS((3,) + p.shape[1:], BF) for p in pbfs], n_sem, 0, copies)


def _sum_chips(ck, pbf, slots, name):
    _, h, c = pbf.shape
    n = _n_pieces(h)
    pr = h // n

    def body(ck_ref, p_ref, s_ref, o_ref):
        o_ref[...] = (((p_ref[...].astype(F32) + s_ref[0].astype(F32)) + s_ref[1].astype(F32))
                      + s_ref[2].astype(F32))

    spec = pltpu.PrefetchScalarGridSpec(
        num_scalar_prefetch=1, grid=(n,),
        in_specs=[pl.BlockSpec((None, pr, c), lambda p, ck: (ck[1], p, 0)),
                  pl.BlockSpec((3, pr, c), lambda p, ck: (0, p, 0))],
        out_specs=pl.BlockSpec((pr, c), lambda p, ck: (ck[0] * n + p, 0)))
    return pl.pallas_call(body, grid_spec=spec, out_shape=SDS((2 * h, c), F32),
                          compiler_params=_params(("arbitrary",)), name=name)(ck, pbf, slots)


def _join_halves(sums):
    plan = _piece_plan([s.shape for s in sums])
    n_sem = sum(n for _, n, _ in plan)

    def copies(r_refs, o_refs, send, recv, local):
        x, y, c = _me()
        sends, arrivals = [], []
        for t, (h, n, pr) in enumerate(plan):
            for p in range(n):
                rows = pl.ds(c * h + p * pr, pr)
                rows_sib = pl.ds((1 - c) * h + p * pr, pr)
                s = len(sends)
                sends.append(_remote(send, recv, s, r_refs[t].at[rows], o_refs[t].at[rows], (x, y, 1 - c)))
                arrivals.append(_remote(send, recv, s, r_refs[t].at[rows_sib], o_refs[t].at[rows_sib], (x, y, 1 - c)))
        return sends, arrivals, []

    return _Phase(sums, [SDS(s.shape, F32) for s in sums], n_sem, 0, copies, {t: t for t in range(len(sums))})


def _gather_small(s, name):
    def body(s_ref, o_ref, send, recv, local):
        x, y, c = _me()
        me = 4 * x + 2 * y + c
        keep = pltpu.make_async_copy(s_ref, o_ref.at[me], local)
        keep.start()
        sends = []
        for r in range(1, 8):
            fx, fy, fc = (r >> 2) & 1, (r >> 1) & 1, r & 1
            to = (x ^ fx, y ^ fy, c ^ fc)
            sends.append(pltpu.make_async_remote_copy(
                src_ref=s_ref, dst_ref=o_ref.at[me], send_sem=send.at[r - 1], recv_sem=recv.at[r - 1],
                device_id=to, device_id_type=MESH))
        for cp in sends:
            cp.start()
        for r in range(1, 8):
            fx, fy, fc = (r >> 2) & 1, (r >> 1) & 1, r & 1
            src = 4 * (x ^ fx) + 2 * (y ^ fy) + (c ^ fc)
            pltpu.make_async_remote_copy(
                src_ref=s_ref, dst_ref=o_ref.at[src], send_sem=send.at[r - 1], recv_sem=recv.at[r - 1],
                device_id=(x ^ fx, y ^ fy, c ^ fc), device_id_type=MESH).wait_recv()
        for cp in sends:
            cp.wait_send()
        keep.wait()

    return pl.pallas_call(
        body, in_specs=[ANY], out_specs=ANY, out_shape=SDS((8, SMALL_ROWS, LANES), F32),
        scratch_shapes=[pltpu.SemaphoreType.DMA((7,)), pltpu.SemaphoreType.DMA((7,)), pltpu.SemaphoreType.DMA],
        name=name)(s)


def _adam_math(w, g, m, v):
    nm = ADAM_B1 * m + (1.0 - ADAM_B1) * g
    nv = ADAM_B2 * v + (1.0 - ADAM_B2) * (g * g)
    m_hat = nm / (1.0 - ADAM_B1 ** ADAM_STEP)
    v_hat = nv / (1.0 - ADAM_B2 ** ADAM_STEP)
    return -ADAM_LR * (m_hat / (jnp.sqrt(v_hat) + ADAM_EPS) + ADAM_WD * w), nm, nv


def _adamw(w, g, m, v, name):
    _, r, c = w.shape
    t = max(d for d in range(8, r + 1, 8) if r % d == 0 and 16 * d * c * 4 <= VMEM_LIMIT - (8 << 20))

    def body(w_ref, g_ref, m_ref, v_ref, go_ref, d_ref, nm_ref, nv_ref):
        g_ = g_ref[...]
        d, nm, nv = _adam_math(w_ref[...], g_, m_ref[...], v_ref[...])
        go_ref[...] = g_
        d_ref[...] = d
        nm_ref[...] = nm
        nv_ref[...] = nv

    lead = pl.BlockSpec((None, t, c), lambda i: (0, i, 0))
    return pl.pallas_call(body, grid=(r // t,), in_specs=[lead, _rows(t, c), lead, lead], out_specs=[lead] * 4,
                          out_shape=[SDS((1, r, c), F32)] * 4, compiler_params=_params(("parallel",)),
                          name=name)(w, g, m, v)


def _small_layout():
    out, r0 = {}, 0
    for n in SMALL:
        size = int(np.prod(SMALL_SHAPE[n]))
        nr = -(-size // LANES)
        out[n] = (r0, nr)
        r0 += nr
    assert r0 <= SMALL_ROWS
    return out, r0


def _pack_small(grads, loss_tile, name):
    layout, used = _small_layout()

    def body(*refs):
        o_ref = refs[-1]
        o_ref[used:used + 1, :] = refs[-2][0:1, :]
        for n, ref in zip(SMALL, refs[:-2]):
            r0, nr = layout[n]
            if n == "w_spatial":
                for g in range(GM_GROUPS):
                    o_ref[r0 + g * GM_CHUNK:r0 + (g + 1) * GM_CHUNK, :] = ref[g]
            elif n == "b_spatial":
                o_ref[r0:r0 + nr, :] = ref[...]
            else:
                for i in range(nr):
                    o_ref[r0 + i:r0 + i + 1, :] = ref[:, i * LANES:(i + 1) * LANES]
        if used + 1 < SMALL_ROWS:
            o_ref[used + 1:SMALL_ROWS, :] = jnp.zeros((SMALL_ROWS - used - 1, LANES), F32)

    return pl.pallas_call(body, out_shape=SDS((SMALL_ROWS, LANES), F32), name=name)(*grads, loss_tile)


def _adamw_small(gathered, ws, ms, vs, name):
    layout, used = _small_layout()
    n_t = len(SMALL)

    def body(*refs):
        g_ref = refs[0]
        w_refs, m_refs, v_refs = refs[1:1 + n_t], refs[1 + n_t:1 + 2 * n_t], refs[1 + 2 * n_t:1 + 3 * n_t]
        outs = refs[1 + 3 * n_t:1 + 7 * n_t]
        acc = refs[-1]
        total = g_ref[0]
        for j in range(1, 8):
            total = total + g_ref[j]
        acc[...] = total
        refs[1 + 7 * n_t][...] = acc[used:used + 1, :]
        for t, n in enumerate(SMALL):
            r0, nr = layout[n]
            o_refs = [outs[t], outs[n_t + t], outs[2 * n_t + t], outs[3 * n_t + t]]
            if n == "w_spatial":
                views = [((0, g), slice(r0 + g * GM_CHUNK, r0 + (g + 1) * GM_CHUNK), slice(None))
                         for g in range(GM_GROUPS)]
            elif n == "b_spatial":
                views = [((0,), slice(r0, r0 + nr), slice(None))]
            else:
                width = SMALL_SHAPE[n][1]
                views = [((slice(None), slice(i * LANES, min((i + 1) * LANES, width))), slice(r0 + i, r0 + i + 1),
                          slice(0, min(LANES, width - i * LANES))) for i in range(nr)]
            for idx, rows, lanes in views:
                g = acc[rows, lanes]
                d, nm, nv = _adam_math(w_refs[t][idx], g, m_refs[t][idx], v_refs[t][idx])
                for ref, val in zip(o_refs, (g, d, nm, nv)):
                    ref[idx] = val

    shapes = [SDS(SMALL_SHAPE[n], F32) for n in SMALL]
    return pl.pallas_call(body, out_shape=shapes * 4 + [SDS((1, LANES), F32)],
                          scratch_shapes=[pltpu.VMEM((SMALL_ROWS, LANES), F32)], name=name)(gathered, *ws, *ms, *vs)


def _win_layout(w_in):
    pad = jnp.zeros((w_in.shape[0], LANES - MLA_ROPE), w_in.dtype)
    u, v, cq = w_in[:, 0:512], w_in[:, 512:1024], w_in[:, 1024:1408]
    ckv, kpe, qm, zg = w_in[:, 1408:1664], w_in[:, 1664:1728], w_in[:, 1728:2240], w_in[:, 2240:5312]
    return jnp.concatenate([zg, u, v, qm, cq, kpe, pad, ckv], axis=1)


def _win_unlayout_rows(gt):
    zg, u, v, qm = gt[ZG:ZG + 3072], gt[ZU:ZU + 512], gt[ZV:ZV + 512], gt[QM:QM + 512]
    cq, kpe, ckv = gt[CQ:CQ + 384], gt[KPE:KPE + MLA_ROPE], gt[CKV:CKV + 256]
    return jnp.concatenate([u, v, cq, ckv, kpe, qm, zg], axis=0)


def _wq_layout(w_uq):
    w = w_uq.reshape(Q_LORA, MLA_HEADS, MLA_NOPE + MLA_ROPE)
    nope = w[:, :, :MLA_NOPE].reshape(Q_LORA, MLA_HEADS * MLA_NOPE)
    pe = jnp.pad(w[:, :, MLA_NOPE:], ((0, 0), (0, 0), (0, LANES - MLA_ROPE))).reshape(Q_LORA, MLA_HEADS * LANES)
    return jnp.concatenate([nope, pe], axis=1)


def _wq_unlayout(g):
    nope = g[:, :1024].reshape(Q_LORA, MLA_HEADS, MLA_NOPE)
    pe = g[:, 1024:].reshape(Q_LORA, MLA_HEADS, LANES)[:, :, :MLA_ROPE]
    return jnp.concatenate([nope, pe], axis=2).reshape(Q_LORA, MLA_HEADS * (MLA_NOPE + MLA_ROPE))


def _wkv_layout(w_ukv):
    w = w_ukv.reshape(KV_LORA, MLA_HEADS, MLA_NOPE + MLA_V)
    return jnp.concatenate([w[:, :, :MLA_NOPE].reshape(KV_LORA, 1024), w[:, :, MLA_NOPE:].reshape(KV_LORA, 1024)],
                           axis=1)


def _wkv_unlayout(g):
    kn = g[:, :1024].reshape(KV_LORA, MLA_HEADS, MLA_NOPE)
    v = g[:, 1024:].reshape(KV_LORA, MLA_HEADS, MLA_V)
    return jnp.concatenate([kn, v], axis=2).reshape(KV_LORA, MLA_HEADS * (MLA_NOPE + MLA_V))


def _owner_major(g, name):
    r, c = _shard_shape(name)
    return g.reshape(r, N_CHIPS, c).transpose(1, 0, 2) if name in COL_SHARDED else g.reshape(N_CHIPS, r, c)


def _pad_lanes(g):
    return jnp.pad(g, ((0, 0), (0, LANES - g.shape[1])))


def kernel(x, mem, positions, g_mix, w_in, g_cq, w_uq, g_ckv, w_ukv, g_q_nope, g_q_pe, g_k_nope, g_k_pe, g_gm_ln, b_gm_ln, w_spatial, b_spatial, g_mem, w_mem_kv, g_mq, g_mk, w_o_gm, w_o_mla, w_o_mem, w_out, g_ffn, w_ff1, w_ff2, loss_target, m_g_mix, m_w_in, m_g_cq, m_w_uq, m_g_ckv, m_w_ukv, m_g_q_nope, m_g_q_pe, m_g_k_nope, m_g_k_pe, m_g_gm_ln, m_b_gm_ln, m_w_spatial, m_b_spatial, m_g_mem, m_w_mem_kv, m_g_mq, m_g_mk, m_w_o_gm, m_w_o_mla, m_w_o_mem, m_w_out, m_g_ffn, m_w_ff1, m_w_ff2, v_g_mix, v_w_in, v_g_cq, v_w_uq, v_g_ckv, v_w_ukv, v_g_q_nope, v_g_q_pe, v_g_k_nope, v_g_k_pe, v_g_gm_ln, v_b_gm_ln, v_w_spatial, v_b_spatial, v_g_mem, v_w_mem_kv, v_g_mq, v_g_mk, v_w_o_gm, v_w_o_mla, v_w_o_mem, v_w_out, v_g_ffn, v_w_ff1, v_w_ff2):
    given = dict(locals())
    wts = {n: given[n] for n in WEIGHTS}
    mom = {n: given["m_" + n] for n in WEIGHTS}
    var = {n: given["v_" + n] for n in WEIGHTS}
    batch, seq, _ = x.shape
    n_tok = batch * seq

    def natural(n, g):
        r, c = _shard_shape(n)
        return g.transpose(1, 0, 2).reshape(r, N_CHIPS * c) if n in COL_SHARDED else g.reshape(N_CHIPS * r, c)

    def far(names):
        return _gather_far([wts[n][0].astype(BF) for n in names])

    x2 = x.reshape(n_tok, D_MODEL)
    tgt2 = loss_target.reshape(n_tok, D_MODEL)
    mem2 = mem.reshape(batch * MEM_LEN, D_MODEL)
    pos_f = positions.reshape(n_tok, 1).astype(F32)

    inv = ROPE_BASE ** (-jnp.arange(0, MLA_ROPE, 2, dtype=F32) / MLA_ROPE)
    zeros64 = jnp.zeros((LANES - MLA_ROPE,), F32)
    inv_full = jnp.concatenate([inv, inv, zeros64]).reshape(1, LANES)
    half = MLA_ROPE // 2
    cmask = jnp.concatenate([jnp.ones((MLA_ROPE,), F32), zeros64]).reshape(1, LANES)
    smask = jnp.concatenate([-jnp.ones((half,), F32), jnp.ones((half,), F32), zeros64]).reshape(1, LANES)

    prep_gains = [g_cq, g_ckv, g_q_nope, _pad_lanes(g_q_pe), g_k_nope, _pad_lanes(g_k_pe)]
    ws = w_spatial[0]
    bcols = [b_spatial[0, g].reshape(GM_CHUNK, 1) for g in range(GM_GROUPS)]

    h1, in_far = _rms_fwd(x2, g_mix, "rms_mix", comm=far(EARLY[:1]))
    (cos_f, sin_s), early = _rope_tables(pos_f, inv_full, cmask, smask, "rope_tables",
                                         comm=_together(_gather_near(in_far), far(EARLY[1:])))
    memn, rest = _rms_fwd(mem2, g_mem, "rms_mem", comm=_gather_near(early[1:]))
    full = {n: natural(n, g) for n, g in zip(EARLY, list(early[:1]) + list(rest))}
    win = _win_layout(full["w_in"])
    wq = _wq_layout(full["w_uq"])
    wkv = _wkv_layout(full["w_ukv"])
    z, proj_far = _mm(h1, win, out_dtypes=(BF,), name="mm_in", comm=far(LATE_PROJ))
    gm = _gm_fwd(z, g_gm_ln, b_gm_ln, ws, bcols, "gm_fwd")
    qc, kc, vc = _prep_fwd(z, cos_f, sin_s, prep_gains, wq, wkv, "prep_fwd")
    (o_mla, lse), ff_far = _mla_fwd(qc, kc, vc, batch, seq, "mla_fwd", comm=far(LATE_FF))
    kvm, proj = _mm(memn, full["w_mem_kv"], name="mm_memkv", comm=_gather_near(proj_far))
    o_mem, ff = _mem_fwd(z, kvm, g_mq, g_mk, batch, seq, "mem_fwd", comm=_gather_near(ff_far))
    full.update({n: natural(n, g) for n, g in zip(LATE_PROJ + LATE_FF, list(proj) + list(ff))})
    y_gm = _mm(gm, full["w_o_gm"], out_dtypes=(BF,), name="mm_o_gm")
    y_mla = _mm(o_mla, full["w_o_mla"], out_dtypes=(BF,), name="mm_o_mla")
    y_mem = _mm(o_mem, full["w_o_mem"], out_dtypes=(BF,), name="mm_o_mem")
    merged = _merge_fwd(z, y_gm, y_mla, y_mem, "merge_fwd")
    x1, h2 = _mm(merged, full["w_out"], ins=(x2,), row_ins=(g_ffn,), epilogue=_residual_rms, out_dtypes=(F32, BF),
                 name="mm_out")
    a_ff, r_ff = _mm(h2, full["w_ff1"], epilogue=_relu2, out_dtypes=(BF, BF), name="mm_ff1")
    dy, dyb, loss_tile = _mm(r_ff, full["w_ff2"], ins=(x1, tgt2), epilogue=_loss_tail, out_dtypes=(F32, BF),
                             total=True, name="mm_ff2")

    gw = {}
    da = _mm(dyb, full["w_ff2"], tb=True, ins=(a_ff,), epilogue=_relu2_bwd, out_dtypes=(BF,), name="mm_d_a")
    gw["w_ff2"] = _owner_major(_mm(r_ff, dyb, ta=True, name="mm_dw_ff2"), "w_ff2")
    gw["w_ff1"] = _mm(h2, da, ta=True, owner_cols=D_FF // N_CHIPS, name="mm_dw_ff1")
    dx1, dx1b, dg_ffn = _mm(da, full["w_ff1"], tb=True, ins=(x1, dy), row_ins=(g_ffn,), epilogue=_rms_bwd_tail,
                            out_dtypes=(F32, BF), total=(1, D_MODEL), name="mm_d_h2")
    dmerged = _mm(dx1b, full["w_out"], tb=True, name="mm_d_merged")
    gw["w_out"] = _owner_major(_mm(merged, dx1b, ta=True, name="mm_dw_out"), "w_out")
    dz, dy_gm, dy_mla, dy_mem = _merge_bwd(z, y_gm, y_mla, y_mem, dmerged, "merge_bwd")
    dgm = _mm(dy_gm, full["w_o_gm"], tb=True, name="mm_d_gm")
    gw["w_o_gm"] = _owner_major(_mm(gm, dy_gm, ta=True, name="mm_dw_o_gm"), "w_o_gm")
    do_mla = _mm(dy_mla, full["w_o_mla"], tb=True, name="mm_d_omla")
    gw["w_o_mla"] = _owner_major(_mm(o_mla, dy_mla, ta=True, name="mm_dw_o_mla"), "w_o_mla")
    do_mem = _mm(dy_mem, full["w_o_mem"], tb=True, name="mm_d_omem")
    gw["w_o_mem"] = _owner_major(_mm(o_mem, dy_mem, ta=True, name="mm_dw_o_mem"), "w_o_mem")
    ck = jnp.stack([lax.axis_index("c"), 2 * lax.axis_index("x") + lax.axis_index("y")]).astype(jnp.int32)

    def pair_sums(names, theirs):
        return [_pair_add(ck, gw[n], t, "pair_add_" + n) for n, t in zip(names, theirs)]

    def chip_sums(names, pairs, slots):
        return [_sum_chips(ck, p, s, "sum_chips_" + n) for n, p, s in zip(names, pairs, slots)]

    (dz, dg_ln, db_ln, dws, *dbcols), theirs = _gm_bwd(z, g_gm_ln, b_gm_ln, ws, bcols, dgm, dz, "gm_bwd",
                                                      comm=_pair_exchange([gw[n] for n in LATE]))
    pairs = pair_sums(LATE, theirs)
    (dq, dk, dv), slots = _mla_bwd(qc, kc, vc, o_mla, lse, do_mla, batch, seq, "mla_bwd",
                                   comm=_scatter_partials(pairs))
    sums = chip_sums(LATE, pairs, slots)
    (dz, dg_cq, dg_ckv, dg_qn, dg_qp, dg_kn, dg_kp, dwq, dwkv), reduced_late = _prep_bwd(
        z, cos_f, sin_s, prep_gains, wq, wkv, dq, dk, dv, dz, "prep_bwd", comm=_join_halves(sums))
    dz, dkvm, dg_mq, dg_mk = _mem_bwd(z, kvm, g_mq, g_mk, do_mem, dz, batch, seq, "mem_bwd")
    dmemn = _mm(dkvm, full["w_mem_kv"], tb=True, name="mm_d_memn")
    gw["w_mem_kv"] = _owner_major(_mm(memn, dkvm, ta=True, name="mm_dw_memkv"), "w_mem_kv")
    _, _, dg_mem = _rms_bwd(mem2, g_mem, dmemn, None, "rms_mem_bwd")
    gw["w_in"] = _win_unlayout_rows(_mm(h1, dz, ta=True, name="mm_dw_in").T).reshape(N_CHIPS, W_IN_COLS // N_CHIPS,
                                                                                    D_MODEL)
    gw["w_uq"] = _owner_major(_wq_unlayout(dwq), "w_uq")
    gw["w_ukv"] = _owner_major(_wkv_unlayout(dwkv), "w_ukv")
    dh1, theirs = _mm(dz, win, tb=True, name="mm_d_h1_top", rows=(0, 2), comm=_pair_exchange([gw[n] for n in EARLY]))
    pairs = pair_sums(EARLY, theirs)
    dh1, slots = _mm(dz, win, tb=True, name="mm_d_h1_bottom", rows=(1, 2), into=dh1,
                     comm=_scatter_partials(pairs))
    grad_x, _, dg_mix = _rms_bwd(x2, g_mix, dh1, dx1, "rms_mix_bwd")
    reduced_early = _run_phase(_join_halves(chip_sums(EARLY, pairs, slots)), "join_early")
    reduced = dict(zip(LATE + EARLY, list(reduced_late) + list(reduced_early)))

    def swapped(a):
        return jnp.swapaxes(a, -1, -2)

    results = {n: _adamw(wts[n], reduced[n], mom[n], var[n], "adamw_" + n) for n in BIG if n != "w_in"}
    results["w_in"] = [swapped(r) for r in _adamw(swapped(w_in), reduced["w_in"], swapped(m_w_in), swapped(v_w_in),
                                                  "adamw_w_in")]

    small_g = {"g_mix": dg_mix, "g_cq": dg_cq, "g_ckv": dg_ckv, "g_q_nope": dg_qn, "g_q_pe": dg_qp,
               "g_k_nope": dg_kn, "g_k_pe": dg_kp, "g_gm_ln": dg_ln, "b_gm_ln": db_ln, "w_spatial": dws,
               "b_spatial": jnp.concatenate(dbcols, axis=1).T, "g_mem": dg_mem, "g_mq": dg_mq, "g_mk": dg_mk,
               "g_ffn": dg_ffn}
    packed = _pack_small([small_g[n] for n in SMALL], loss_tile, "pack_small")
    small_out = _adamw_small(_gather_small(packed, "gather_small"), [wts[n] for n in SMALL],
                             [mom[n] for n in SMALL], [var[n] for n in SMALL], "adamw_small")
    for t, n in enumerate(SMALL):
        results[n] = [small_out[j * len(SMALL) + t] for j in range(4)]

    loss = small_out[4 * len(SMALL)][0, 0]
    grad_x = grad_x.reshape(batch, seq, D_MODEL)
    return (loss, grad_x, *[results[n][0] for n in WEIGHTS], *[results[n][1] for n in WEIGHTS],
            *[results[n][2] for n in WEIGHTS], *[results[n][3] for n in WEIGHTS])
```

```python
import functools
import math

import numpy as np
import jax
import jax.numpy as jnp
from jax import lax
from jax.experimental import pallas as pl
from jax.experimental.pallas import tpu as pltpu

F32 = jnp.float32
BF = jnp.bfloat16
SDS = jax.ShapeDtypeStruct
MESH = pl.DeviceIdType.MESH

D_MODEL = 1024
MEM_LEN = 256
MEM_HEADS = 4
HEAD_DIM = 128
GM_WIDTH = 512
GM_CHUNK = 128
GM_GROUPS = 4
MLA_HEADS = 8
MLA_NOPE = 128
MLA_ROPE = 64
MLA_V = 128
Q_LORA = 384
KV_LORA = 256
ROPE_BASE = 10000.0
D_FF = 4096
EPS = 1e-6
W_IN_COLS = 5312
ADAM_LR, ADAM_B1, ADAM_B2, ADAM_EPS, ADAM_WD, ADAM_STEP = 0.001, 0.9, 0.999, 1e-08, 0.01, 10

ZG, ZU, ZV, QM, CQ, KPE, CKV = 0, 3072, 3584, 4096, 4608, 4992, 5120
Z_COLS = 5376
LANES = 128
ROW_TILE = 512
ATT_TILE = 1024
ATT_HEADS = 2
ATT_HEADS_FWD = 4
VMEM_LIMIT = 60 * 1024 * 1024

N_CHIPS = 4
PIECE_ROWS = 256
SMALL_ROWS = 560

BIG = ["w_in", "w_uq", "w_ukv", "w_mem_kv", "w_o_gm", "w_o_mla", "w_o_mem", "w_out", "w_ff1", "w_ff2"]
BIG_SHAPE = {"w_in": (1024, 5312), "w_uq": (384, 1536), "w_ukv": (256, 2048), "w_mem_kv": (1024, 1024),
             "w_o_gm": (512, 1024), "w_o_mla": (1024, 1024), "w_o_mem": (512, 1024), "w_out": (1024, 1024),
             "w_ff1": (1024, 4096), "w_ff2": (4096, 1024)}
COL_SHARDED = {"w_in", "w_uq", "w_ukv", "w_o_gm", "w_o_mem", "w_ff1"}
EARLY = ["w_in", "w_uq", "w_ukv", "w_mem_kv"]
LATE_PROJ = ["w_o_gm", "w_o_mla", "w_o_mem", "w_out"]
LATE_FF = ["w_ff1", "w_ff2"]
LATE = LATE_PROJ + LATE_FF
SMALL = ["w_spatial", "b_spatial", "g_mix", "g_cq", "g_ckv", "g_q_nope", "g_q_pe", "g_k_nope", "g_k_pe", "g_gm_ln",
         "b_gm_ln", "g_mem", "g_mq", "g_mk", "g_ffn"]
SMALL_SHAPE = {"g_mix": (1, 1024), "g_cq": (1, 384), "g_ckv": (1, 256), "g_q_nope": (1, 128), "g_q_pe": (1, 64),
               "g_k_nope": (1, 128), "g_k_pe": (1, 64), "g_gm_ln": (1, 512), "b_gm_ln": (1, 512),
               "w_spatial": (1, 4, 128, 128), "b_spatial": (1, 4, 128), "g_mem": (1, 1024), "g_mq": (1, 128),
               "g_mk": (1, 128), "g_ffn": (1, 1024)}
WEIGHTS = ['g_mix', 'w_in', 'g_cq', 'w_uq', 'g_ckv', 'w_ukv', 'g_q_nope', 'g_q_pe', 'g_k_nope', 'g_k_pe',
           'g_gm_ln', 'b_gm_ln', 'w_spatial', 'b_spatial', 'g_mem', 'w_mem_kv', 'g_mq', 'g_mk', 'w_o_gm',
           'w_o_mla', 'w_o_mem', 'w_out', 'g_ffn', 'w_ff1', 'w_ff2']


def _params(sem=None):
    return pltpu.CompilerParams(vmem_limit_bytes=VMEM_LIMIT, dimension_semantics=sem)


def _pick(n, prefs):
    for p in prefs:
        if n % p == 0:
            return p
    return n


def _full(shape):
    nd = len(shape)
    return pl.BlockSpec(shape, lambda *_: (0,) * nd)


def _rows(t, w, blk=0):
    return pl.BlockSpec((t, w), lambda i: (i, blk))


def _acc(ref, val, first):
    @pl.when(first)
    def _():
        ref[...] = val

    @pl.when(jnp.logical_not(first))
    def _():
        ref[...] += val


ANY = pl.BlockSpec(memory_space=pl.ANY)


class _Phase:
    def __init__(self, operands, out_shapes, n_sem, n_local, copies, aliases=None):
        self.operands, self.out_shapes, self.aliases = list(operands), list(out_shapes), dict(aliases or {})
        self.n_sem, self.n_local, self.copies = n_sem, max(n_local, 1), copies

    def sem_shapes(self):
        return [pltpu.SemaphoreType.DMA((self.n_sem,)), pltpu.SemaphoreType.DMA((self.n_sem,)),
                pltpu.SemaphoreType.DMA((self.n_local,))]

    def start(self, ins, outs, send, recv, local):
        sends, _, locals_ = self.copies(ins, outs, send, recv, local)
        for cp in locals_ + sends:
            cp.start()

    def finish(self, ins, outs, send, recv, local):
        sends, arrivals, locals_ = self.copies(ins, outs, send, recv, local)
        for cp in arrivals:
            cp.wait_recv()
        for cp in sends:
            cp.wait_send()
        for cp in locals_:
            cp.wait()


class _Shifted:
    def __init__(self, ref, base):
        self.ref, self.base = ref, base

    @property
    def at(self):
        return self

    def __getitem__(self, i):
        return self.ref.at[i + self.base]


def _together(first, second):
    n_in, n_out = len(first.operands), len(first.out_shapes)

    def copies(ins, outs, send, recv, local):
        a = first.copies(ins[:n_in], outs[:n_out], send, recv, local)
        b = second.copies(ins[n_in:], outs[n_out:], _Shifted(send, first.n_sem), _Shifted(recv, first.n_sem),
                          _Shifted(local, first.n_local))
        return a[0] + b[0], a[1] + b[1], a[2] + b[2]

    aliases = {**first.aliases, **{n_in + i: n_out + j for i, j in second.aliases.items()}}
    return _Phase(first.operands + second.operands, first.out_shapes + second.out_shapes, first.n_sem + second.n_sem,
                  first.n_local + second.n_local, copies, aliases)


def _run_phase(phase, name):
    n_in = len(phase.operands)

    def body(*refs):
        ins, outs, sems = refs[:n_in], refs[n_in:n_in + len(phase.out_shapes)], refs[n_in + len(phase.out_shapes):]
        phase.start(ins, outs, *sems)
        phase.finish(ins, outs, *sems)

    return pl.pallas_call(body, in_specs=[ANY] * n_in, out_specs=[ANY] * len(phase.out_shapes),
                          out_shape=phase.out_shapes, scratch_shapes=phase.sem_shapes(),
                          input_output_aliases=phase.aliases, name=name)(*phase.operands)


def _pcall(body, *, grid, in_specs, out_specs, out_shape, scratch_shapes=(), sem=None, name, comm=None, aliases=None):
    single = not isinstance(out_shape, (list, tuple))
    o_specs = [out_specs] if single else list(out_specs)
    o_shape = [out_shape] if single else list(out_shape)
    aliases = dict(aliases or {})
    if comm is None:
        call = pl.pallas_call(body, grid=grid, in_specs=list(in_specs), out_specs=o_specs, out_shape=o_shape,
                              scratch_shapes=list(scratch_shapes), input_output_aliases=aliases,
                              compiler_params=_params(sem), name=name)

        def run_plain(*args):
            res = call(*args)
            return res[0] if single else res

        return run_plain

    n_in, n_out, n_scr = len(in_specs), len(o_specs), len(scratch_shapes)
    nc_in, nc_out = len(comm.operands), len(comm.out_shapes)

    def wrapped(*refs):
        ins, cins = refs[:n_in], refs[n_in:n_in + nc_in]
        o0 = n_in + nc_in
        outs, couts = refs[o0:o0 + n_out], refs[o0 + n_out:o0 + n_out + nc_out]
        s0 = o0 + n_out + nc_out
        scr, csem = refs[s0:s0 + n_scr], refs[s0 + n_scr:]
        ids = [pl.program_id(d) for d in range(len(grid))]
        first = functools.reduce(jnp.logical_and, [i == 0 for i in ids])
        last = functools.reduce(jnp.logical_and, [i == g - 1 for i, g in zip(ids, grid)])

        @pl.when(first)
        def _():
            comm.start(cins, couts, *csem)

        body(*ins, *outs, *scr)

        @pl.when(last)
        def _():
            comm.finish(cins, couts, *csem)

    call = pl.pallas_call(
        wrapped, grid=grid, in_specs=list(in_specs) + [ANY] * nc_in, out_specs=o_specs + [ANY] * nc_out,
        out_shape=o_shape + comm.out_shapes, scratch_shapes=list(scratch_shapes) + comm.sem_shapes(),
        input_output_aliases={**aliases, **{n_in + i: n_out + j for i, j in comm.aliases.items()}},
        compiler_params=_params(("arbitrary",) * len(grid)), name=name)

    def run_carrying(*args):
        res = call(*args, *comm.operands)
        return (res[0] if single else res[:n_out]), res[n_out:]

    return run_carrying


def _dn(a, b, ca, cb):
    return lax.dot_general(a.astype(BF), b.astype(BF), (((ca,), (cb,)), ((), ())), preferred_element_type=F32)


@jax.custom_vjp
def _mm_nn(a, b):
    return _dn(a, b, 1, 0)


def _mm_nn_fwd(a, b):
    return _dn(a, b, 1, 0), (a.astype(BF), b.astype(BF))


def _mm_nn_bwd(res, ct):
    a, b = res
    return _dn(ct, b, 1, 1), _dn(a, ct, 0, 0)


_mm_nn.defvjp(_mm_nn_fwd, _mm_nn_bwd)


@jax.custom_vjp
def _mm_nt(a, b):
    return _dn(a, b, 1, 1)


def _mm_nt_fwd(a, b):
    return _dn(a, b, 1, 1), (a.astype(BF), b.astype(BF))


def _mm_nt_bwd(res, ct):
    a, b = res
    return _dn(ct, b, 1, 0), _dn(ct, a, 0, 0)


_mm_nt.defvjp(_mm_nt_fwd, _mm_nt_bwd)


def _rmsn(x, g, n):
    ms = jnp.sum(x * x, axis=-1, keepdims=True) * (1.0 / n)
    return x * lax.rsqrt(ms + EPS) * g


def _layernorm(x, g, b):
    mu = jnp.mean(x, axis=-1, keepdims=True)
    xc = x - mu
    y = xc * lax.rsqrt(jnp.mean(xc * xc, axis=-1, keepdims=True) + EPS)
    return y * g + b


def _swap_lanes(x):
    half = MLA_ROPE // 2
    lane = lax.broadcasted_iota(jnp.int32, x.shape, 1)
    return jnp.where(lane < half, pltpu.roll(x, LANES - half, axis=1),
                     jnp.where(lane < MLA_ROPE, pltpu.roll(x, half, axis=1), 0.0))


@jax.custom_vjp
def _swap_halves(x):
    return _swap_lanes(x)


_swap_halves.defvjp(lambda x: (_swap_lanes(x), None), lambda _, ct: (_swap_lanes(ct),))


def _rope(x, cos_f, sin_s):
    return x * cos_f + _swap_halves(x) * sin_s


def _lane_blocks(x):
    return tuple(x[:, i * LANES:(i + 1) * LANES] for i in range(x.shape[1] // LANES))


@jax.custom_vjp
def _split_lanes(x):
    return _lane_blocks(x)


_split_lanes.defvjp(lambda x: (_lane_blocks(x), None), lambda _, cts: (jnp.concatenate(cts, axis=1),))


def _softmax(s):
    m = lax.stop_gradient(jnp.max(s, axis=-1, keepdims=True))
    p = jnp.exp(s - m)
    return p / jnp.sum(p, axis=-1, keepdims=True)


def _mm(a, b, *, ta=False, tb=False, ins=(), row_ins=(), epilogue=None, out_dtypes=(F32,), owner_cols=None,
        total=False, name, comm=None, rows=None, into=None):
    if ta:
        k_dim, m = a.shape
    else:
        m, k_dim = a.shape
    if tb:
        n, kb = b.shape
    else:
        kb, n = b.shape
    assert k_dim == kb, (a.shape, b.shape, ta, tb)
    part, n_parts = rows if rows is not None else (0, 1)
    tm = _pick(m // n_parts, (1024, 512, 256, 128))
    tn = _pick(n if owner_cols is None else owner_cols, (1024, 768, 512, 384, 256, 128))
    tk = _pick(k_dim, (2048, 1024, 768, 512, 256, 128))
    nk = k_dim // tk
    m_steps = m // tm // n_parts
    off = part * m_steps
    ca = 0 if ta else 1
    cb = 1 if tb else 0
    n_in = len(ins) + len(row_ins)
    n_out = len(out_dtypes)
    n_pass = 0 if into is None else 1
    total_shape = total if isinstance(total, tuple) else (8, LANES)
    assert not (isinstance(total, tuple) and n != tn), "a per-column total needs the whole width in one tile"

    def finish(r, in_refs, out_refs, first_tile):
        vals = epilogue(r, *[ref[...].astype(F32) for ref in in_refs]) if epilogue is not None else (r,)
        for ref, val, dt in zip(out_refs, vals, out_dtypes):
            ref[...] = val.astype(dt)
        if total:
            _acc(out_refs[n_out], vals[n_out], first_tile)

    def body(*refs):
        a_ref, b_ref = refs[:2]
        in_refs = refs[2:2 + n_in]
        o0 = 2 + n_in + n_pass
        out_refs = refs[o0:o0 + n_out + int(bool(total))]
        first_tile = jnp.logical_and(pl.program_id(0) == 0, pl.program_id(1) == 0)
        part = _dn(a_ref[...], b_ref[...], ca, cb)
        if nk == 1:
            finish(part, in_refs, out_refs, first_tile)
            return
        acc = refs[-1]
        k = pl.program_id(2)
        _acc(acc, part, k == 0)

        @pl.when(k == nk - 1)
        def _():
            finish(acc[...], in_refs, out_refs, first_tile)

    a_spec = (pl.BlockSpec((tk, tm), lambda i, j, k: (k, i + off)) if ta
              else pl.BlockSpec((tm, tk), lambda i, j, k: (i + off, k)))
    b_spec = pl.BlockSpec((tn, tk), lambda i, j, k: (j, k)) if tb else pl.BlockSpec((tk, tn), lambda i, j, k: (k, j))
    t_spec = pl.BlockSpec((tm, tn), lambda i, j, k: (i + off, j))
    if owner_cols is None:
        o_spec, o_shape = t_spec, (m, n)
    else:
        per = owner_cols // tn
        o_spec = pl.BlockSpec((None, tm, tn), lambda i, j, k: (j // per, i + off, j % per))
        o_shape = (n // owner_cols, m, owner_cols)
    o_specs = [o_spec] * n_out + ([pl.BlockSpec(total_shape, lambda i, j, k: (0, 0))] if total else [])
    o_shapes = [SDS(o_shape, dt) for dt in out_dtypes] + ([SDS(total_shape, F32)] if total else [])
    row_spec = pl.BlockSpec((1, tn), lambda i, j, k: (0, j))
    in_specs = [a_spec, b_spec] + [t_spec] * len(ins) + [row_spec] * len(row_ins) + [ANY] * n_pass
    args = [a, b, *ins, *row_ins] + ([into] if n_pass else [])
    run = _pcall(body, grid=(m_steps, n // tn, nk), in_specs=in_specs, out_specs=o_specs, out_shape=o_shapes,
                 scratch_shapes=[pltpu.VMEM((tm, tn), F32)] if nk > 1 else [],
                 sem=("arbitrary",) * 3 if total else ("parallel", "parallel", "arbitrary"), name=name, comm=comm,
                 aliases={len(in_specs) - 1: 0} if n_pass else None)
    if comm is None:
        outs = run(*args)
        return outs[0] if len(outs) == 1 else outs
    outs, exchanged = run(*args)
    return (outs[0] if len(outs) == 1 else outs), exchanged


def _add_to(r, x):
    return (r + x,)


def _residual_rms(r, x, g):
    x1 = r + x
    return x1, _rmsn(x1, g, D_MODEL)


def _rms_bwd_tail(dh, x, res, g):
    _, vjp = jax.vjp(lambda xx, gg: _rmsn(xx, gg, D_MODEL), x, g)
    dx, dg = vjp(dh)
    dx = dx + res
    return dx, dx, dg


def _relu2(r):
    p = jnp.maximum(r, 0.0)
    return r, p * p


def _relu2_bwd(dr, a):
    return (dr * (2.0 * jnp.maximum(a, 0.0)),)


def _loss_tail(r, x1, tgt):
    e = (r + x1) - tgt
    dy = e * (1.0 / D_MODEL)
    part = jnp.sum(jnp.sum(e * e, axis=-1, keepdims=True), axis=0, keepdims=True) * (0.5 / D_MODEL)
    return dy, dy, jnp.broadcast_to(part, (8, LANES))


def _rms_fwd(x, g, name, comm=None):
    n, w = x.shape
    t = min(ROW_TILE, n)

    def body(x_ref, g_ref, o_ref):
        o_ref[...] = _rmsn(x_ref[...], g_ref[...], w).astype(BF)

    return _pcall(body, grid=(n // t,), in_specs=[_rows(t, w), _full((1, w))], out_specs=_rows(t, w),
                  out_shape=SDS((n, w), BF), sem=("arbitrary",), name=name, comm=comm)(x, g)


def _rms_bwd(x, g, dh, res, name, comm=None):
    n, w = x.shape
    t = min(ROW_TILE, n)
    has_res = res is not None

    def body(*refs):
        if has_res:
            x_ref, g_ref, dh_ref, res_ref, dx_ref, dxb_ref, dg_ref = refs
        else:
            x_ref, g_ref, dh_ref, dx_ref, dxb_ref, dg_ref = refs
        _, vjp = jax.vjp(lambda xx, gg: _rmsn(xx, gg, w), x_ref[...], g_ref[...])
        dx, dg = vjp(dh_ref[...])
        if has_res:
            dx = dx + res_ref[...]
        dx_ref[...] = dx
        dxb_ref[...] = dx.astype(BF)
        _acc(dg_ref, dg, pl.program_id(0) == 0)

    in_specs = [_rows(t, w), _full((1, w)), _rows(t, w)] + ([_rows(t, w)] if has_res else [])
    args = [x, g, dh] + ([res] if has_res else [])
    return _pcall(body, grid=(n // t,), in_specs=in_specs, out_specs=[_rows(t, w), _rows(t, w), _full((1, w))],
                  out_shape=[SDS((n, w), F32), SDS((n, w), BF), SDS((1, w), F32)], sem=("arbitrary",), name=name,
                  comm=comm)(*args)


def _merge_core(zg0, zg1, zg2, y0, y1, y2):
    return jax.nn.sigmoid(zg0) * y0 + jax.nn.sigmoid(zg1) * y1 + jax.nn.sigmoid(zg2) * y2


def _merge_fwd(z, y_gm, y_mla, y_mem, name):
    n = z.shape[0]
    t = min(ROW_TILE, n)
    w = D_MODEL

    def body(g0, g1, g2, y0, y1, y2, o_ref):
        o_ref[...] = _merge_core(g0[...].astype(F32), g1[...].astype(F32), g2[...].astype(F32), y0[...].astype(F32), y1[...].astype(F32),
                                 y2[...].astype(F32)).astype(BF)

    return pl.pallas_call(body, grid=(n // t,),
                          in_specs=[_rows(t, w, 0), _rows(t, w, 1), _rows(t, w, 2)] + [_rows(t, w)] * 3,
                          out_specs=_rows(t, w), out_shape=SDS((n, w), BF),
                          compiler_params=_params(("parallel",)), name=name)(z, z, z, y_gm, y_mla, y_mem)


def _merge_bwd(z, y_gm, y_mla, y_mem, dmerged, name):
    n = z.shape[0]
    t = min(ROW_TILE, n)
    w = D_MODEL

    def body(g0, g1, g2, y0, y1, y2, dm, dzg_ref, d0_ref, d1_ref, d2_ref):
        _, vjp = jax.vjp(_merge_core, g0[...].astype(F32), g1[...].astype(F32), g2[...].astype(F32), y0[...].astype(F32), y1[...].astype(F32),
                         y2[...].astype(F32))
        dg0, dg1, dg2, dy0, dy1, dy2 = vjp(dm[...])
        dzg_ref[:, 0:w] = dg0.astype(BF)
        dzg_ref[:, w:2 * w] = dg1.astype(BF)
        dzg_ref[:, 2 * w:3 * w] = dg2.astype(BF)
        d0_ref[...] = dy0.astype(BF)
        d1_ref[...] = dy1.astype(BF)
        d2_ref[...] = dy2.astype(BF)

    return pl.pallas_call(body, grid=(n // t,),
                          in_specs=[_rows(t, w, 0), _rows(t, w, 1), _rows(t, w, 2)] + [_rows(t, w)] * 4,
                          out_specs=[_rows(t, 3 * w, ZG // (3 * w))] + [_rows(t, w)] * 3,
                          out_shape=[SDS((n, Z_COLS), BF)] + [SDS((n, w), BF)] * 3,
                          compiler_params=_params(("parallel",)), name=name)(z, z, z, y_gm, y_mla, y_mem, dmerged)


def _gm_core(zu, zv, g_ln, b_ln, ws, bcols):
    t = zu.shape[0]
    u = jax.nn.gelu(zu)
    v = _layernorm(jax.nn.gelu(zv), g_ln, b_ln)
    row = lax.broadcasted_iota(jnp.int32, (GM_CHUNK, GM_CHUNK), 0)
    col = lax.broadcasted_iota(jnp.int32, (GM_CHUNK, GM_CHUNK), 1)
    wc = [jnp.where(row >= col, ws[g], 0.0) for g in range(GM_GROUPS)]
    chunks = []
    for c in range(t // GM_CHUNK):
        cols = []
        for g in range(GM_GROUPS):
            vc = v[c * GM_CHUNK:(c + 1) * GM_CHUNK, g * LANES:(g + 1) * LANES]
            cols.append(_mm_nn(wc[g], vc) + bcols[g])
        chunks.append(jnp.concatenate(cols, axis=1))
    mixed = chunks[0] if len(chunks) == 1 else jnp.concatenate(chunks, axis=0)
    return u * mixed


def _gm_specs(t):
    return [_rows(t, GM_WIDTH, ZU // GM_WIDTH), _rows(t, GM_WIDTH, ZV // GM_WIDTH), _full((1, GM_WIDTH)),
            _full((1, GM_WIDTH)), _full((GM_GROUPS, GM_CHUNK, GM_CHUNK))] + [_full((GM_CHUNK, 1))] * GM_GROUPS


def _gm_fwd(z, g_ln, b_ln, ws, bcols, name):
    n = z.shape[0]
    t = min(ROW_TILE, n)

    def body(zu, zv, g_ref, b_ref, ws_ref, c0, c1, c2, c3, o_ref):
        out = _gm_core(zu[...].astype(F32), zv[...].astype(F32), g_ref[...], b_ref[...], [ws_ref[g] for g in range(GM_GROUPS)],
                       [c0[...], c1[...], c2[...], c3[...]])
        o_ref[...] = out.astype(BF)

    return pl.pallas_call(body, grid=(n // t,), in_specs=_gm_specs(t), out_specs=_rows(t, GM_WIDTH),
                          out_shape=SDS((n, GM_WIDTH), BF), compiler_params=_params(("parallel",)),
                          name=name)(z, z, g_ln, b_ln, ws, *bcols)


def _gm_bwd(z, g_ln, b_ln, ws, bcols, dgm, dz, name, comm=None):
    n = z.shape[0]
    t = min(ROW_TILE, n)

    def body(zu, zv, g_ref, b_ref, ws_ref, c0, c1, c2, c3, dgm_ref, _, dz_ref, dg_ref, db_ref, dws_ref, e0, e1, e2,
             e3):
        first = pl.program_id(0) == 0
        _, vjp = jax.vjp(_gm_core, zu[...].astype(F32), zv[...].astype(F32), g_ref[...], b_ref[...],
                         [ws_ref[g] for g in range(GM_GROUPS)], [c0[...], c1[...], c2[...], c3[...]])
        dzu, dzv, dg, db, dws, dcols = vjp(dgm_ref[...])
        dz_ref[:, 0:GM_WIDTH] = dzu.astype(BF)
        dz_ref[:, GM_WIDTH:2 * GM_WIDTH] = dzv.astype(BF)
        _acc(dg_ref, dg, first)
        _acc(db_ref, db, first)
        _acc(dws_ref, jnp.stack(dws, axis=0), first)
        for ref, val in zip((e0, e1, e2, e3), dcols):
            _acc(ref, val, first)

    in_specs = _gm_specs(t) + [_rows(t, GM_WIDTH), ANY]
    return _pcall(
        body, grid=(n // t,), in_specs=in_specs,
        out_specs=[_rows(t, 2 * GM_WIDTH, ZU // (2 * GM_WIDTH)), _full((1, GM_WIDTH)), _full((1, GM_WIDTH)),
                   _full((GM_GROUPS, GM_CHUNK, GM_CHUNK))] + [_full((GM_CHUNK, 1))] * GM_GROUPS,
        out_shape=[SDS((n, Z_COLS), BF), SDS((1, GM_WIDTH), F32), SDS((1, GM_WIDTH), F32),
                   SDS((GM_GROUPS, GM_CHUNK, GM_CHUNK), F32)] + [SDS((GM_CHUNK, 1), F32)] * GM_GROUPS,
        sem=("arbitrary",), name=name, comm=comm, aliases={len(in_specs) - 1: 0})(z, z, g_ln, b_ln, ws, *bcols, dgm, dz)


def _rope_tables(pos_f, inv_full, cmask, smask, name, comm=None):
    n = pos_f.shape[0]
    t = min(ROW_TILE, n)

    def body(p_ref, inv_ref, cm_ref, sm_ref, cos_ref, sin_ref):
        ang = p_ref[...] * inv_ref[...]
        cos_ref[...] = jnp.cos(ang) * cm_ref[...]
        sin_ref[...] = jnp.sin(ang) * sm_ref[...]

    return _pcall(body, grid=(n // t,), in_specs=[_rows(t, 1)] + [_full((1, LANES))] * 3,
                  out_specs=[_rows(t, LANES)] * 2, out_shape=[SDS((n, LANES), F32)] * 2, sem=("parallel",),
                  name=name, comm=comm)(pos_f, inv_full, cmask, smask)


def _prep_norms(cq, ckv, g_cq, g_ckv):
    return _rmsn(cq, g_cq, Q_LORA), _rmsn(ckv, g_ckv, KV_LORA)


def _prep_heads(qa, kva, kpe, head_gains, cos_f, sin_s):
    g_qn, g_qp, g_kn, g_kp = head_gains
    qs = _split_lanes(qa)
    kvs = _split_lanes(kva)
    kp = _rope(_rmsn(kpe, g_kp, MLA_ROPE), cos_f, sin_s)
    q_out, k_out = [], []
    for h in range(MLA_HEADS):
        q_out.append(_rmsn(qs[h], g_qn, MLA_NOPE))
        q_out.append(_rope(_rmsn(qs[MLA_HEADS + h], g_qp, MLA_ROPE), cos_f, sin_s))
        k_out.append(_rmsn(kvs[h], g_kn, MLA_NOPE))
        k_out.append(kp)
    return (jnp.concatenate(q_out, axis=1), jnp.concatenate(k_out, axis=1),
            jnp.concatenate(kvs[MLA_HEADS:], axis=1))


def _prep_in_specs(t):
    return ([_rows(t, Q_LORA, CQ // Q_LORA), _rows(t, LANES, KPE // LANES), _rows(t, KV_LORA, CKV // KV_LORA),
             _rows(t, LANES), _rows(t, LANES), _full((1, Q_LORA)), _full((1, KV_LORA))] + [_full((1, LANES))] * 4
            + [_full((Q_LORA, 2048)), _full((KV_LORA, 2048))])


def _prep_fwd(z, cos_f, sin_s, gains, wq, wkv, name):
    n = z.shape[0]
    t = min(ROW_TILE, n)

    def body(cq, kpe, ckv, cos_ref, sin_ref, g_cq, g_ckv, g_qn, g_qp, g_kn, g_kp, wq_ref, wkv_ref, q_ref, k_ref, v_ref):
        cqn, ckvn = _prep_norms(cq[...].astype(F32), ckv[...].astype(F32), g_cq[...], g_ckv[...])
        qa = _dn(cqn, wq_ref[...], 1, 0)
        kva = _dn(ckvn, wkv_ref[...], 1, 0)
        q, k, v = _prep_heads(qa, kva, kpe[...].astype(F32), (g_qn[...], g_qp[...], g_kn[...], g_kp[...]), cos_ref[...],
                              sin_ref[...])
        q_ref[...] = q.astype(BF)
        k_ref[...] = k.astype(BF)
        v_ref[...] = v.astype(BF)

    return pl.pallas_call(body, grid=(n // t,), in_specs=_prep_in_specs(t),
                          out_specs=[_rows(t, 2048), _rows(t, 2048), _rows(t, 1024)],
                          out_shape=[SDS((n, 2048), BF), SDS((n, 2048), BF), SDS((n, 1024), BF)],
                          compiler_params=_params(("parallel",)),
                          name=name)(z, z, z, cos_f, sin_s, *gains, wq, wkv)


def _prep_bwd(z, cos_f, sin_s, gains, wq, wkv, dq, dk, dv, dz, name, comm=None):
    n = z.shape[0]
    t = min(ROW_TILE, n)
    wz = Q_LORA + LANES + KV_LORA

    def body(cq, kpe, ckv, cos_ref, sin_ref, g_cq, g_ckv, g_qn, g_qp, g_kn, g_kp, wq_ref, wkv_ref, dq_ref, dk_ref,
             dv_ref, _, dz_ref, o_cq, o_ckv, o_qn, o_qp, o_kn, o_kp, dwq_ref, dwkv_ref):
        first = pl.program_id(0) == 0
        cos_t, sin_t = cos_ref[...], sin_ref[...]
        (cqn, ckvn), vjp_norms = jax.vjp(_prep_norms, cq[...].astype(F32), ckv[...].astype(F32), g_cq[...], g_ckv[...])
        wq_t, wkv_t = wq_ref[...], wkv_ref[...]
        qa = _dn(cqn, wq_t, 1, 0)
        kva = _dn(ckvn, wkv_t, 1, 0)
        _, vjp_heads = jax.vjp(lambda a, b, c, g: _prep_heads(a, b, c, g, cos_t, sin_t), qa, kva, kpe[...].astype(F32),
                               (g_qn[...], g_qp[...], g_kn[...], g_kp[...]))
        dqa, dkva, dkpe, dhead = vjp_heads((dq_ref[...], dk_ref[...], dv_ref[...]))
        _acc(dwq_ref, _dn(cqn, dqa, 0, 0), first)
        _acc(dwkv_ref, _dn(ckvn, dkva, 0, 0), first)
        dcq, dckv, dg_cq, dg_ckv = vjp_norms((_dn(dqa, wq_t, 1, 1), _dn(dkva, wkv_t, 1, 1)))
        dz_ref[:, 0:Q_LORA] = dcq.astype(BF)
        dz_ref[:, Q_LORA:Q_LORA + LANES] = dkpe.astype(BF)
        dz_ref[:, Q_LORA + LANES:wz] = dckv.astype(BF)
        for ref, val in zip((o_cq, o_ckv, o_qn, o_qp, o_kn, o_kp), (dg_cq, dg_ckv) + tuple(dhead)):
            _acc(ref, val, first)

    gain_specs = [_full((1, Q_LORA)), _full((1, KV_LORA))] + [_full((1, LANES))] * 4
    gain_shapes = [SDS((1, Q_LORA), F32), SDS((1, KV_LORA), F32)] + [SDS((1, LANES), F32)] * 4
    in_specs = _prep_in_specs(t) + [_rows(t, 2048), _rows(t, 2048), _rows(t, 1024), ANY]
    return _pcall(
        body, grid=(n // t,), in_specs=in_specs,
        out_specs=[_rows(t, wz, CQ // wz)] + gain_specs + [_full((Q_LORA, 2048)), _full((KV_LORA, 2048))],
        out_shape=[SDS((n, Z_COLS), BF)] + gain_shapes + [SDS((Q_LORA, 2048), F32), SDS((KV_LORA, 2048), F32)],
        sem=("arbitrary",), name=name, comm=comm,
        aliases={len(in_specs) - 1: 0})(z, z, z, cos_f, sin_s, *gains, wq, wkv, dq, dk, dv, dz)


MLA_QK = 256
MLA_SCALE = 1.0 / math.sqrt(MLA_NOPE + MLA_ROPE)
LOG2E = 1.0 / math.log(2.0)
MLA_SCALE_LOG2E = MLA_SCALE * LOG2E


def _causal_mask(s, q0, k0):
    tq, tk = s.shape
    row = q0 + lax.broadcasted_iota(jnp.int32, (tq, tk), 0)
    col = k0 + lax.broadcasted_iota(jnp.int32, (tq, tk), 1)
    return jnp.where(row >= col, s, -jnp.inf)


def _mla_fwd(q, k, v, batch, seq, name, comm=None):
    n = q.shape[0]
    tq = min(ATT_TILE, seq)
    nq = seq // tq

    nh = ATT_HEADS_FWD

    def body(q_ref, k_ref, v_ref, o_ref, lse_ref):
        i = pl.program_id(2)

        def step(j, carry, diagonal=False):
            k0 = pl.multiple_of(j * tq, tq)
            out = []
            ones = jnp.ones((tq, LANES), BF)
            for hh in range(nh):
                m, acc = carry[hh]
                qb = q_ref[:, hh * MLA_QK:(hh + 1) * MLA_QK]
                kb = k_ref[pl.ds(k0, tq), hh * MLA_QK:(hh + 1) * MLA_QK]
                vb = v_ref[pl.ds(k0, tq), hh * MLA_V:(hh + 1) * MLA_V]
                s = _dn(qb, kb, 1, 1)
                if diagonal:
                    s = _causal_mask(s, i * tq, k0)
                m_new = jnp.maximum(m, jnp.max(s, axis=-1, keepdims=True))
                p = jnp.exp2((s - m_new) * MLA_SCALE_LOG2E)
                alpha = jnp.exp2((m - m_new) * MLA_SCALE_LOG2E)
                acc = alpha * acc + _dn(p, jnp.concatenate([vb, ones], axis=1), 1, 0)
                out.append((m_new, acc))
            return tuple(out)

        init = tuple((jnp.full((tq, 1), -jnp.inf, F32), jnp.zeros((tq, MLA_V + LANES), F32)) for _ in range(nh))
        final = step(i, lax.fori_loop(0, i, step, init), diagonal=True)
        for hh, (m, acc) in enumerate(final):
            l = acc[:, MLA_V:MLA_V + 1]
            o_ref[:, hh * MLA_V:(hh + 1) * MLA_V] = acc[:, :MLA_V] / l
            lse_ref[:, hh * LANES:(hh + 1) * LANES] = jnp.broadcast_to(m * MLA_SCALE + jnp.log(l), (tq, LANES))

    return _pcall(
        body, grid=(batch, MLA_HEADS // nh, nq),
        in_specs=[pl.BlockSpec((tq, nh * MLA_QK), lambda b, h, i: (b * nq + i, h)),
                  pl.BlockSpec((seq, nh * MLA_QK), lambda b, h, i: (b, h)),
                  pl.BlockSpec((seq, nh * MLA_V), lambda b, h, i: (b, h))],
        out_specs=[pl.BlockSpec((tq, nh * MLA_V), lambda b, h, i: (b * nq + i, h)),
                   pl.BlockSpec((tq, nh * LANES), lambda b, h, i: (b * nq + i, h))],
        out_shape=[SDS((n, MLA_HEADS * MLA_V), F32), SDS((n, MLA_HEADS * LANES), F32)],
        sem=("parallel", "parallel", "arbitrary"), name=name, comm=comm)(q, k, v)


def _mla_bwd(q, k, v, o, lse, do, batch, seq, name, comm=None):
    n = q.shape[0]
    tk = min(ATT_TILE, seq)
    nk = seq // tk

    nh = ATT_HEADS

    def body(q_ref, k_ref, v_ref, o_ref, lse_ref, do_ref, dq_ref, dk_ref, dv_ref):
        jk = pl.program_id(2)

        @pl.when(jk == 0)
        def _():
            dq_ref[...] = jnp.zeros_like(dq_ref)

        def step(i, carry, diagonal=False):
            q0 = pl.multiple_of(i * tk, tk)
            rows = pl.ds(q0, tk)
            out = []
            for hh in range(nh):
                dk_acc, dv_acc = carry[hh]
                qk_cols = slice(hh * MLA_QK, (hh + 1) * MLA_QK)
                v_cols = slice(hh * MLA_V, (hh + 1) * MLA_V)
                kb = k_ref[:, qk_cols]
                vb = v_ref[:, v_cols]
                qb = q_ref[rows, qk_cols]
                dob = do_ref[rows, v_cols]
                delta = jnp.sum(dob * o_ref[rows, v_cols], axis=-1, keepdims=True)
                s = _dn(qb, kb, 1, 1)
                if diagonal:
                    s = _causal_mask(s, q0, jk * tk)
                p = jnp.exp2(s * MLA_SCALE_LOG2E - lse_ref[rows, hh * LANES:hh * LANES + 1] * LOG2E)
                dv_acc = dv_acc + _dn(p, dob, 0, 0)
                dp = _dn(dob, vb, 1, 1)
                ds = p * (dp - delta) * MLA_SCALE
                dk_acc = dk_acc + _dn(ds, qb, 0, 0)
                dq_ref[rows, qk_cols] += _dn(ds, kb, 1, 0)
                out.append((dk_acc, dv_acc))
            return tuple(out)

        init = tuple((jnp.zeros((tk, MLA_QK), F32), jnp.zeros((tk, MLA_V), F32)) for _ in range(nh))
        final = lax.fori_loop(jk + 1, nk, step, step(jk, init, diagonal=True))
        for hh, (dk_acc, dv_acc) in enumerate(final):
            dk_ref[:, hh * MLA_QK:(hh + 1) * MLA_QK] = dk_acc
            dv_ref[:, hh * MLA_V:(hh + 1) * MLA_V] = dv_acc

    full_qk = pl.BlockSpec((seq, nh * MLA_QK), lambda b, h, j: (b, h))
    full_v = pl.BlockSpec((seq, nh * MLA_V), lambda b, h, j: (b, h))
    blk_qk = pl.BlockSpec((tk, nh * MLA_QK), lambda b, h, j: (b * nk + j, h))
    blk_v = pl.BlockSpec((tk, nh * MLA_V), lambda b, h, j: (b * nk + j, h))
    return _pcall(
        body, grid=(batch, MLA_HEADS // nh, nk),
        in_specs=[full_qk, blk_qk, blk_v, full_v, full_v, full_v],
        out_specs=[full_qk, blk_qk, blk_v],
        out_shape=[SDS((n, MLA_HEADS * MLA_QK), F32), SDS((n, MLA_HEADS * MLA_QK), F32),
                   SDS((n, MLA_HEADS * MLA_V), F32)],
        sem=("parallel", "parallel", "arbitrary"), name=name, comm=comm)(q, k, v, o, lse, do)


MEM_SCALE = 1.0 / math.sqrt(HEAD_DIM)
MEM_W = MEM_HEADS * HEAD_DIM


def _mem_core(qs, ks, vs, g_mq, g_mk):
    outs = []
    for h in range(MEM_HEADS):
        qh = _rmsn(qs[h], g_mq, HEAD_DIM)
        kh = _rmsn(ks[h], g_mk, HEAD_DIM)
        p = _softmax(_mm_nt(qh, kh) * MEM_SCALE)
        outs.append(_mm_nn(p, vs[h]))
    return jnp.concatenate(outs, axis=1)


def _mem_load(qm, kvm, g_mq, g_mk):
    hs = range(MEM_HEADS)
    qs = [qm[:, h * LANES:(h + 1) * LANES].astype(F32) for h in hs]
    ks = [kvm[:, h * LANES:(h + 1) * LANES] for h in hs]
    vs = [kvm[:, MEM_W + h * LANES:MEM_W + (h + 1) * LANES] for h in hs]
    return qs, ks, vs, g_mq[...], g_mk[...]


def _mem_fwd(z, kvm, g_mq, g_mk, batch, seq, name, comm=None):
    n = z.shape[0]
    t = min(ROW_TILE, seq)
    per = seq // t

    def body(qm, kvm_ref, gq, gk, o_ref):
        o_ref[...] = _mem_core(*_mem_load(qm, kvm_ref, gq, gk)).astype(BF)

    return _pcall(
        body, grid=(n // t,),
        in_specs=[_rows(t, MEM_W, QM // MEM_W), pl.BlockSpec((MEM_LEN, 2 * MEM_W), lambda i: (i // per, 0)),
                  _full((1, LANES)), _full((1, LANES))],
        out_specs=_rows(t, MEM_W), out_shape=SDS((n, MEM_W), BF), sem=("parallel",), name=name,
        comm=comm)(z, kvm, g_mq, g_mk)


def _mem_bwd(z, kvm, g_mq, g_mk, dom, dz, batch, seq, name):
    n = z.shape[0]
    t = min(ROW_TILE, seq)
    per = seq // t

    def body(qm, kvm_ref, gq, gk, dom_ref, _, dz_ref, dkvm_ref, dgq_ref, dgk_ref):
        i = pl.program_id(0)
        _, vjp = jax.vjp(_mem_core, *_mem_load(qm, kvm_ref, gq, gk))
        dqs, dks, dvs, dgq, dgk = vjp(dom_ref[...])
        dz_ref[...] = jnp.concatenate(dqs, axis=1).astype(BF)
        _acc(dkvm_ref, jnp.concatenate(dks + dvs, axis=1), i % per == 0)
        _acc(dgq_ref, dgq, i == 0)
        _acc(dgk_ref, dgk, i == 0)

    kv_spec = pl.BlockSpec((MEM_LEN, 2 * MEM_W), lambda i: (i // per, 0))
    return pl.pallas_call(
        body, grid=(n // t,),
        in_specs=[_rows(t, MEM_W, QM // MEM_W), kv_spec, _full((1, LANES)), _full((1, LANES)), _rows(t, MEM_W), ANY],
        out_specs=[_rows(t, MEM_W, QM // MEM_W), kv_spec, _full((1, LANES)), _full((1, LANES))],
        out_shape=[SDS((n, Z_COLS), BF), SDS((batch * MEM_LEN, 2 * MEM_W), F32), SDS((1, LANES), F32),
                   SDS((1, LANES), F32)],
        input_output_aliases={5: 0},
        compiler_params=_params(("arbitrary",)), name=name)(z, kvm, g_mq, g_mk, dom, dz)


def _me():
    return lax.axis_index("x"), lax.axis_index("y"), lax.axis_index("c")


def _other_chips(x, y):
    return [(1 - x, y), (x, 1 - y), (1 - x, 1 - y)]


def _shard_shape(name):
    r, c = BIG_SHAPE[name]
    return (r, c // N_CHIPS) if name in COL_SHARDED else (r // N_CHIPS, c)


def _n_pieces(half_rows):
    for n in range(max(1, half_rows // PIECE_ROWS), 0, -1):
        if half_rows % n == 0 and (half_rows // n) % 16 == 0:
            return n
    return 1


def _piece_plan(shapes):
    plan = []
    for r, _ in shapes:
        h = r // 2
        n = _n_pieces(h)
        plan.append((h, n, h // n))
    return plan


def _remote(send, recv, sem, src, dst, to):
    return pltpu.make_async_remote_copy(src_ref=src, dst_ref=dst, send_sem=send.at[sem], recv_sem=recv.at[sem],
                                        device_id=to, device_id_type=MESH)


def _gather_far(shards):
    plan = _piece_plan([s.shape for s in shards])
    n_far = 3 * sum(n for _, n, _ in plan)
    n_loc = 2 * sum(n for _, n, _ in plan)

    def copies(s_refs, o_refs, send, recv, local):
        x, y, c = _me()
        k = 2 * x + y
        mine, sends, arrivals = [], [], []
        for t, (h, n, pr) in enumerate(plan):
            s_ref, o_ref = s_refs[t], o_refs[t]
            for core in range(2):
                for p in range(n):
                    rows = pl.ds(core * h + p * pr, pr)
                    mine.append(pltpu.make_async_copy(s_ref.at[rows], o_ref.at[k, rows], local.at[len(mine)]))
            for chip in _other_chips(x, y):
                for p in range(n):
                    rows = pl.ds(c * h + p * pr, pr)
                    s = len(sends)
                    sends.append(_remote(send, recv, s, s_ref.at[rows], o_ref.at[k, rows], (*chip, c)))
                    arrivals.append(_remote(send, recv, s, s_ref.at[rows], o_ref.at[2 * chip[0] + chip[1], rows],
                                            (*chip, c)))
        return sends, arrivals, mine

    return _Phase(shards, [SDS((N_CHIPS,) + s.shape, s.dtype) for s in shards], n_far, n_loc, copies)


def _gather_near(bufs):
    plan = _piece_plan([b.shape[1:] for b in bufs])
    n_sem = 3 * sum(n for _, n, _ in plan)

    def copies(i_refs, o_refs, send, recv, local):
        x, y, c = _me()
        sib = (x, y, 1 - c)
        sends, arrivals = [], []
        for t, (h, n, pr) in enumerate(plan):
            for chip in _other_chips(x, y):
                ci = 2 * chip[0] + chip[1]
                for p in range(n):
                    rows = pl.ds(c * h + p * pr, pr)
                    rows_sib = pl.ds((1 - c) * h + p * pr, pr)
                    s = len(sends)
                    sends.append(_remote(send, recv, s, i_refs[t].at[ci, rows], o_refs[t].at[ci, rows], sib))
                    arrivals.append(_remote(send, recv, s, i_refs[t].at[ci, rows_sib], o_refs[t].at[ci, rows_sib], sib))
        return sends, arrivals, []

    return _Phase(bufs, [SDS(b.shape, b.dtype) for b in bufs], n_sem, 0, copies, {t: t for t in range(len(bufs))})


def _pair_exchange(grads):
    plan = _piece_plan([g.shape[1:] for g in grads])
    n_sem = sum(n for _, n, _ in plan)

    def copies(g_refs, o_refs, send, recv, local):
        x, y, c = _me()
        sends = []
        for t, (h, n, pr) in enumerate(plan):
            for p in range(n):
                sends.append(_remote(send, recv, len(sends), g_refs[t].at[:, pl.ds((1 - c) * h + p * pr, pr)],
                                     o_refs[t].at[:, pl.ds(p * pr, pr)], (x, y, 1 - c)))
        return sends, sends, []

    return _Phase(grads, [SDS((N_CHIPS, g.shape[1] // 2, g.shape[2]), F32) for g in grads], n_sem, 0, copies)


def _pair_add(ck, g, theirs, name):
    _, r, c = g.shape
    (h, n, pr), = _piece_plan([(r, c)])

    def body(ck_ref, g_ref, t_ref, pbf_ref):
        pbf_ref[...] = (g_ref[...] + t_ref[...]).astype(BF)

    half = pl.BlockSpec((None, pr, c), lambda k, p, ck: (k, p, 0))
    spec = pltpu.PrefetchScalarGridSpec(
        num_scalar_prefetch=1, grid=(N_CHIPS, n),
        in_specs=[pl.BlockSpec((None, pr, c), lambda k, p, ck: (k, ck[0] * n + p, 0)), half], out_specs=half)
    return pl.pallas_call(body, grid_spec=spec, out_shape=SDS((N_CHIPS, h, c), BF),
                          compiler_params=_params(("arbitrary", "arbitrary")), name=name)(ck, g, theirs)


def _scatter_partials(pbfs):
    plan = [(h, _n_pieces(h), h // _n_pieces(h)) for h in [p.shape[1] for p in pbfs]]
    n_sem = 3 * sum(n for _, n, _ in plan)

    def copies(p_refs, o_refs, send, recv, local):
        x, y, c = _me()
        sends = []
        for t, (h, n, pr) in enumerate(plan):
            for j, chip in enumerate(_other_chips(x, y)):
                for p in range(n):
                    rows = pl.ds(p * pr, pr)
                    sends.append(_remote(send, recv, len(sends), p_refs[t].at[2 * chip[0] + chip[1], rows],
                                         o_refs[t].at[j, rows], (*chip, c)))
        return sends, sends, []

    return _Phase(pbfs, [SDS((3,) + p.shape[1:], BF) for p in pbfs], n_sem, 0, copies)


def _sum_chips(ck, pbf, slots, name):
    _, h, c = pbf.shape
    n = _n_pieces(h)
    pr = h // n

    def body(ck_ref, p_ref, s_ref, o_ref):
        o_ref[...] = (((p_ref[...].astype(F32) + s_ref[0].astype(F32)) + s_ref[1].astype(F32))
                      + s_ref[2].astype(F32))

    spec = pltpu.PrefetchScalarGridSpec(
        num_scalar_prefetch=1, grid=(n,),
        in_specs=[pl.BlockSpec((None, pr, c), lambda p, ck: (ck[1], p, 0)),
                  pl.BlockSpec((3, pr, c), lambda p, ck: (0, p, 0))],
        out_specs=pl.BlockSpec((pr, c), lambda p, ck: (ck[0] * n + p, 0)))
    return pl.pallas_call(body, grid_spec=spec, out_shape=SDS((2 * h, c), F32),
                          compiler_params=_params(("arbitrary",)), name=name)(ck, pbf, slots)


def _join_halves(sums):
    plan = _piece_plan([s.shape for s in sums])
    n_sem = sum(n for _, n, _ in plan)

    def copies(r_refs, o_refs, send, recv, local):
        x, y, c = _me()
        sends, arrivals = [], []
        for t, (h, n, pr) in enumerate(plan):
            for p in range(n):
                rows = pl.ds(c * h + p * pr, pr)
                rows_sib = pl.ds((1 - c) * h + p * pr, pr)
                s = len(sends)
                sends.append(_remote(send, recv, s, r_refs[t].at[rows], o_refs[t].at[rows], (x, y, 1 - c)))
                arrivals.append(_remote(send, recv, s, r_refs[t].at[rows_sib], o_refs[t].at[rows_sib], (x, y, 1 - c)))
        return sends, arrivals, []

    return _Phase(sums, [SDS(s.shape, F32) for s in sums], n_sem, 0, copies, {t: t for t in range(len(sums))})


def _gather_small(s):
    def copies(s_refs, o_refs, send, recv, local):
        s_ref, o_ref = s_refs[0], o_refs[0]
        x, y, c = _me()
        me = 4 * x + 2 * y + c
        sends, arrivals = [], []
        for r in range(1, 8):
            peer = (x ^ ((r >> 2) & 1), y ^ ((r >> 1) & 1), c ^ (r & 1))
            sends.append(_remote(send, recv, r - 1, s_ref, o_ref.at[me], peer))
            arrivals.append(_remote(send, recv, r - 1, s_ref, o_ref.at[4 * peer[0] + 2 * peer[1] + peer[2]], peer))
        return sends, arrivals, [pltpu.make_async_copy(s_ref, o_ref.at[me], local.at[0])]

    return _Phase([s], [SDS((8, SMALL_ROWS, LANES), F32)], 7, 1, copies)


def _adam_math(w, g, m, v):
    nm = ADAM_B1 * m + (1.0 - ADAM_B1) * g
    nv = ADAM_B2 * v + (1.0 - ADAM_B2) * (g * g)
    m_hat = nm / (1.0 - ADAM_B1 ** ADAM_STEP)
    v_hat = nv / (1.0 - ADAM_B2 ** ADAM_STEP)
    return -ADAM_LR * (m_hat / (jnp.sqrt(v_hat) + ADAM_EPS) + ADAM_WD * w), nm, nv


def _adamw(w, g, m, v, name):
    _, r, c = w.shape
    t = max(d for d in range(8, r + 1, 8) if r % d == 0 and 16 * d * c * 4 <= VMEM_LIMIT - (8 << 20))

    def body(w_ref, g_ref, m_ref, v_ref, go_ref, d_ref, nm_ref, nv_ref):
        g_ = g_ref[...]
        d, nm, nv = _adam_math(w_ref[...], g_, m_ref[...], v_ref[...])
        go_ref[...] = g_
        d_ref[...] = d
        nm_ref[...] = nm
        nv_ref[...] = nv

    lead = pl.BlockSpec((None, t, c), lambda i: (0, i, 0))
    return pl.pallas_call(body, grid=(r // t,), in_specs=[lead, _rows(t, c), lead, lead], out_specs=[lead] * 4,
                          out_shape=[SDS((1, r, c), F32)] * 4, compiler_params=_params(("parallel",)),
                          name=name)(w, g, m, v)


def _small_layout():
    out, r0 = {}, 0
    for n in SMALL:
        size = int(np.prod(SMALL_SHAPE[n]))
        nr = -(-size // LANES)
        out[n] = (r0, nr)
        r0 += nr
    assert r0 <= SMALL_ROWS
    return out, r0


def _pack_small(grads, loss_tile, name):
    layout, used = _small_layout()

    def body(*refs):
        o_ref = refs[-1]
        o_ref[used:used + 1, :] = refs[-2][0:1, :]
        for n, ref in zip(SMALL, refs[:-2]):
            r0, nr = layout[n]
            if n == "w_spatial":
                for g in range(GM_GROUPS):
                    o_ref[r0 + g * GM_CHUNK:r0 + (g + 1) * GM_CHUNK, :] = ref[g]
            elif n == "b_spatial":
                o_ref[r0:r0 + nr, :] = ref[...]
            else:
                for i in range(nr):
                    o_ref[r0 + i:r0 + i + 1, :] = ref[:, i * LANES:(i + 1) * LANES]
        if used + 1 < SMALL_ROWS:
            o_ref[used + 1:SMALL_ROWS, :] = jnp.zeros((SMALL_ROWS - used - 1, LANES), F32)

    return pl.pallas_call(body, out_shape=SDS((SMALL_ROWS, LANES), F32), name=name)(*grads, loss_tile)


def _adamw_small(gathered, ws, ms, vs, name):
    layout, used = _small_layout()
    n_t = len(SMALL)

    def body(*refs):
        g_ref = refs[0]
        w_refs, m_refs, v_refs = refs[1:1 + n_t], refs[1 + n_t:1 + 2 * n_t], refs[1 + 2 * n_t:1 + 3 * n_t]
        outs = refs[1 + 3 * n_t:1 + 7 * n_t]
        acc = refs[-1]
        total = g_ref[0]
        for j in range(1, 8):
            total = total + g_ref[j]
        acc[...] = total
        refs[1 + 7 * n_t][...] = acc[used:used + 1, :]
        for t, n in enumerate(SMALL):
            r0, nr = layout[n]
            o_refs = [outs[t], outs[n_t + t], outs[2 * n_t + t], outs[3 * n_t + t]]
            if n == "w_spatial":
                views = [((0, g), slice(r0 + g * GM_CHUNK, r0 + (g + 1) * GM_CHUNK), slice(None))
                         for g in range(GM_GROUPS)]
            elif n == "b_spatial":
                views = [((0,), slice(r0, r0 + nr), slice(None))]
            else:
                width = SMALL_SHAPE[n][1]
                views = [((slice(None), slice(i * LANES, min((i + 1) * LANES, width))), slice(r0 + i, r0 + i + 1),
                          slice(0, min(LANES, width - i * LANES))) for i in range(nr)]
            for idx, rows, lanes in views:
                g = acc[rows, lanes]
                d, nm, nv = _adam_math(w_refs[t][idx], g, m_refs[t][idx], v_refs[t][idx])
                for ref, val in zip(o_refs, (g, d, nm, nv)):
                    ref[idx] = val

    shapes = [SDS(SMALL_SHAPE[n], F32) for n in SMALL]
    return pl.pallas_call(body, out_shape=shapes * 4 + [SDS((1, LANES), F32)],
                          scratch_shapes=[pltpu.VMEM((SMALL_ROWS, LANES), F32)], name=name)(gathered, *ws, *ms, *vs)


def _win_layout(w_in):
    pad = jnp.zeros((w_in.shape[0], LANES - MLA_ROPE), w_in.dtype)
    u, v, cq = w_in[:, 0:512], w_in[:, 512:1024], w_in[:, 1024:1408]
    ckv, kpe, qm, zg = w_in[:, 1408:1664], w_in[:, 1664:1728], w_in[:, 1728:2240], w_in[:, 2240:5312]
    return jnp.concatenate([zg, u, v, qm, cq, kpe, pad, ckv], axis=1)


def _win_unlayout_rows(gt):
    zg, u, v, qm = gt[ZG:ZG + 3072], gt[ZU:ZU + 512], gt[ZV:ZV + 512], gt[QM:QM + 512]
    cq, kpe, ckv = gt[CQ:CQ + 384], gt[KPE:KPE + MLA_ROPE], gt[CKV:CKV + 256]
    return jnp.concatenate([u, v, cq, ckv, kpe, qm, zg], axis=0)


def _wq_layout(w_uq):
    w = w_uq.reshape(Q_LORA, MLA_HEADS, MLA_NOPE + MLA_ROPE)
    nope = w[:, :, :MLA_NOPE].reshape(Q_LORA, MLA_HEADS * MLA_NOPE)
    pe = jnp.pad(w[:, :, MLA_NOPE:], ((0, 0), (0, 0), (0, LANES - MLA_ROPE))).reshape(Q_LORA, MLA_HEADS * LANES)
    return jnp.concatenate([nope, pe], axis=1)


def _wq_unlayout(g):
    nope = g[:, :1024].reshape(Q_LORA, MLA_HEADS, MLA_NOPE)
    pe = g[:, 1024:].reshape(Q_LORA, MLA_HEADS, LANES)[:, :, :MLA_ROPE]
    return jnp.concatenate([nope, pe], axis=2).reshape(Q_LORA, MLA_HEADS * (MLA_NOPE + MLA_ROPE))


def _wkv_layout(w_ukv):
    w = w_ukv.reshape(KV_LORA, MLA_HEADS, MLA_NOPE + MLA_V)
    return jnp.concatenate([w[:, :, :MLA_NOPE].reshape(KV_LORA, 1024), w[:, :, MLA_NOPE:].reshape(KV_LORA, 1024)],
                           axis=1)


def _wkv_unlayout(g):
    kn = g[:, :1024].reshape(KV_LORA, MLA_HEADS, MLA_NOPE)
    v = g[:, 1024:].reshape(KV_LORA, MLA_HEADS, MLA_V)
    return jnp.concatenate([kn, v], axis=2).reshape(KV_LORA, MLA_HEADS * (MLA_NOPE + MLA_V))


def _owner_major(g, name):
    r, c = _shard_shape(name)
    return g.reshape(r, N_CHIPS, c).transpose(1, 0, 2) if name in COL_SHARDED else g.reshape(N_CHIPS, r, c)


def _pad_lanes(g):
    return jnp.pad(g, ((0, 0), (0, LANES - g.shape[1])))


def kernel(x, mem, positions, g_mix, w_in, g_cq, w_uq, g_ckv, w_ukv, g_q_nope, g_q_pe, g_k_nope, g_k_pe, g_gm_ln, b_gm_ln, w_spatial, b_spatial, g_mem, w_mem_kv, g_mq, g_mk, w_o_gm, w_o_mla, w_o_mem, w_out, g_ffn, w_ff1, w_ff2, loss_target, m_g_mix, m_w_in, m_g_cq, m_w_uq, m_g_ckv, m_w_ukv, m_g_q_nope, m_g_q_pe, m_g_k_nope, m_g_k_pe, m_g_gm_ln, m_b_gm_ln, m_w_spatial, m_b_spatial, m_g_mem, m_w_mem_kv, m_g_mq, m_g_mk, m_w_o_gm, m_w_o_mla, m_w_o_mem, m_w_out, m_g_ffn, m_w_ff1, m_w_ff2, v_g_mix, v_w_in, v_g_cq, v_w_uq, v_g_ckv, v_w_ukv, v_g_q_nope, v_g_q_pe, v_g_k_nope, v_g_k_pe, v_g_gm_ln, v_b_gm_ln, v_w_spatial, v_b_spatial, v_g_mem, v_w_mem_kv, v_g_mq, v_g_mk, v_w_o_gm, v_w_o_mla, v_w_o_mem, v_w_out, v_g_ffn, v_w_ff1, v_w_ff2):
    given = dict(locals())
    wts = {n: given[n] for n in WEIGHTS}
    mom = {n: given["m_" + n] for n in WEIGHTS}
    var = {n: given["v_" + n] for n in WEIGHTS}
    batch, seq, _ = x.shape
    n_tok = batch * seq

    def natural(n, g):
        r, c = _shard_shape(n)
        return g.transpose(1, 0, 2).reshape(r, N_CHIPS * c) if n in COL_SHARDED else g.reshape(N_CHIPS * r, c)

    def far(names):
        return _gather_far([wts[n][0].astype(BF) for n in names])

    x2 = x.reshape(n_tok, D_MODEL)
    tgt2 = loss_target.reshape(n_tok, D_MODEL)
    mem2 = mem.reshape(batch * MEM_LEN, D_MODEL)
    pos_f = positions.reshape(n_tok, 1).astype(F32)

    inv = ROPE_BASE ** (-jnp.arange(0, MLA_ROPE, 2, dtype=F32) / MLA_ROPE)
    zeros64 = jnp.zeros((LANES - MLA_ROPE,), F32)
    inv_full = jnp.concatenate([inv, inv, zeros64]).reshape(1, LANES)
    half = MLA_ROPE // 2
    cmask = jnp.concatenate([jnp.ones((MLA_ROPE,), F32), zeros64]).reshape(1, LANES)
    smask = jnp.concatenate([-jnp.ones((half,), F32), jnp.ones((half,), F32), zeros64]).reshape(1, LANES)

    prep_gains = [g_cq, g_ckv, g_q_nope, _pad_lanes(g_q_pe), g_k_nope, _pad_lanes(g_k_pe)]
    ws = w_spatial[0]
    bcols = [b_spatial[0, g].reshape(GM_CHUNK, 1) for g in range(GM_GROUPS)]

    h1, in_far = _rms_fwd(x2, g_mix, "rms_mix", comm=far(EARLY[:1]))
    (cos_f, sin_s), early = _rope_tables(pos_f, inv_full, cmask, smask, "rope_tables",
                                         comm=_together(_gather_near(in_far), far(EARLY[1:])))
    memn, rest = _rms_fwd(mem2, g_mem, "rms_mem", comm=_gather_near(early[1:]))
    full = {n: natural(n, g) for n, g in zip(EARLY, list(early[:1]) + list(rest))}
    win = _win_layout(full["w_in"])
    wq = _wq_layout(full["w_uq"])
    wkv = _wkv_layout(full["w_ukv"])
    z, proj_far = _mm(h1, win, out_dtypes=(BF,), name="mm_in", comm=far(LATE_PROJ))
    gm = _gm_fwd(z, g_gm_ln, b_gm_ln, ws, bcols, "gm_fwd")
    qc, kc, vc = _prep_fwd(z, cos_f, sin_s, prep_gains, wq, wkv, "prep_fwd")
    (o_mla, lse), ff_far = _mla_fwd(qc, kc, vc, batch, seq, "mla_fwd", comm=far(LATE_FF))
    kvm, proj = _mm(memn, full["w_mem_kv"], name="mm_memkv", comm=_gather_near(proj_far))
    o_mem, ff = _mem_fwd(z, kvm, g_mq, g_mk, batch, seq, "mem_fwd", comm=_gather_near(ff_far))
    full.update({n: natural(n, g) for n, g in zip(LATE_PROJ + LATE_FF, list(proj) + list(ff))})
    y_gm = _mm(gm, full["w_o_gm"], out_dtypes=(BF,), name="mm_o_gm")
    y_mla = _mm(o_mla, full["w_o_mla"], out_dtypes=(BF,), name="mm_o_mla")
    y_mem = _mm(o_mem, full["w_o_mem"], out_dtypes=(BF,), name="mm_o_mem")
    merged = _merge_fwd(z, y_gm, y_mla, y_mem, "merge_fwd")
    x1, h2 = _mm(merged, full["w_out"], ins=(x2,), row_ins=(g_ffn,), epilogue=_residual_rms, out_dtypes=(F32, BF),
                 name="mm_out")
    a_ff, r_ff = _mm(h2, full["w_ff1"], epilogue=_relu2, out_dtypes=(BF, BF), name="mm_ff1")
    dy, dyb, loss_tile = _mm(r_ff, full["w_ff2"], ins=(x1, tgt2), epilogue=_loss_tail, out_dtypes=(F32, BF),
                             total=True, name="mm_ff2")

    gw = {}
    da = _mm(dyb, full["w_ff2"], tb=True, ins=(a_ff,), epilogue=_relu2_bwd, out_dtypes=(BF,), name="mm_d_a")
    gw["w_ff2"] = _owner_major(_mm(r_ff, dyb, ta=True, name="mm_dw_ff2"), "w_ff2")
    gw["w_ff1"] = _mm(h2, da, ta=True, owner_cols=D_FF // N_CHIPS, name="mm_dw_ff1")
    dx1, dx1b, dg_ffn = _mm(da, full["w_ff1"], tb=True, ins=(x1, dy), row_ins=(g_ffn,), epilogue=_rms_bwd_tail,
                            out_dtypes=(F32, BF), total=(1, D_MODEL), name="mm_d_h2")
    dmerged = _mm(dx1b, full["w_out"], tb=True, name="mm_d_merged")
    gw["w_out"] = _owner_major(_mm(merged, dx1b, ta=True, name="mm_dw_out"), "w_out")
    dz, dy_gm, dy_mla, dy_mem = _merge_bwd(z, y_gm, y_mla, y_mem, dmerged, "merge_bwd")
    dgm = _mm(dy_gm, full["w_o_gm"], tb=True, name="mm_d_gm")
    gw["w_o_gm"] = _owner_major(_mm(gm, dy_gm, ta=True, name="mm_dw_o_gm"), "w_o_gm")
    do_mla = _mm(dy_mla, full["w_o_mla"], tb=True, name="mm_d_omla")
    gw["w_o_mla"] = _owner_major(_mm(o_mla, dy_mla, ta=True, name="mm_dw_o_mla"), "w_o_mla")
    do_mem = _mm(dy_mem, full["w_o_mem"], tb=True, name="mm_d_omem")
    gw["w_o_mem"] = _owner_major(_mm(o_mem, dy_mem, ta=True, name="mm_dw_o_mem"), "w_o_mem")
    ck = jnp.stack([lax.axis_index("c"), 2 * lax.axis_index("x") + lax.axis_index("y")]).astype(jnp.int32)

    def pair_sums(names, theirs):
        return [_pair_add(ck, gw[n], t, "pair_add_" + n) for n, t in zip(names, theirs)]

    def chip_sums(names, pairs, slots):
        return [_sum_chips(ck, p, s, "sum_chips_" + n) for n, p, s in zip(names, pairs, slots)]

    (dz, dg_ln, db_ln, dws, *dbcols), theirs = _gm_bwd(z, g_gm_ln, b_gm_ln, ws, bcols, dgm, dz, "gm_bwd",
                                                      comm=_pair_exchange([gw[n] for n in LATE]))
    pairs = pair_sums(LATE, theirs)
    (dq, dk, dv), slots = _mla_bwd(qc, kc, vc, o_mla, lse, do_mla, batch, seq, "mla_bwd",
                                   comm=_scatter_partials(pairs))
    sums = chip_sums(LATE, pairs, slots)
    (dz, dg_cq, dg_ckv, dg_qn, dg_qp, dg_kn, dg_kp, dwq, dwkv), reduced_late = _prep_bwd(
        z, cos_f, sin_s, prep_gains, wq, wkv, dq, dk, dv, dz, "prep_bwd", comm=_join_halves(sums))
    dz, dkvm, dg_mq, dg_mk = _mem_bwd(z, kvm, g_mq, g_mk, do_mem, dz, batch, seq, "mem_bwd")
    dmemn = _mm(dkvm, full["w_mem_kv"], tb=True, name="mm_d_memn")
    gw["w_mem_kv"] = _owner_major(_mm(memn, dkvm, ta=True, name="mm_dw_memkv"), "w_mem_kv")
    _, _, dg_mem = _rms_bwd(mem2, g_mem, dmemn, None, "rms_mem_bwd")
    gw["w_in"] = _win_unlayout_rows(_mm(h1, dz, ta=True, name="mm_dw_in").T).reshape(N_CHIPS, W_IN_COLS // N_CHIPS,
                                                                                    D_MODEL)
    gw["w_uq"] = _owner_major(_wq_unlayout(dwq), "w_uq")
    gw["w_ukv"] = _owner_major(_wkv_unlayout(dwkv), "w_ukv")
    dh1, theirs = _mm(dz, win, tb=True, name="mm_d_h1_top", rows=(0, 2), comm=_pair_exchange([gw[n] for n in EARLY]))
    pairs = pair_sums(EARLY, theirs)
    dh1, slots = _mm(dz, win, tb=True, name="mm_d_h1_bottom", rows=(1, 2), into=dh1,
                     comm=_scatter_partials(pairs))
    grad_x, _, dg_mix = _rms_bwd(x2, g_mix, dh1, dx1, "rms_mix_bwd")
    small_g = {"g_mix": dg_mix, "g_cq": dg_cq, "g_ckv": dg_ckv, "g_q_nope": dg_qn, "g_q_pe": dg_qp,
               "g_k_nope": dg_kn, "g_k_pe": dg_kp, "g_gm_ln": dg_ln, "b_gm_ln": db_ln, "w_spatial": dws,
               "b_spatial": jnp.concatenate(dbcols, axis=1).T, "g_mem": dg_mem, "g_mq": dg_mq, "g_mk": dg_mk,
               "g_ffn": dg_ffn}
    packed = _pack_small([small_g[n] for n in SMALL], loss_tile, "pack_small")
    *reduced_early, gathered_small = _run_phase(
        _together(_join_halves(chip_sums(EARLY, pairs, slots)), _gather_small(packed)), "join_early_gather_small")
    reduced = dict(zip(LATE + EARLY, list(reduced_late) + list(reduced_early)))

    def swapped(a):
        return jnp.swapaxes(a, -1, -2)

    results = {n: _adamw(wts[n], reduced[n], mom[n], var[n], "adamw_" + n) for n in BIG if n != "w_in"}
    results["w_in"] = [swapped(r) for r in _adamw(swapped(w_in), reduced["w_in"], swapped(m_w_in), swapped(v_w_in),
                                                  "adamw_w_in")]

    small_out = _adamw_small(gathered_small, [wts[n] for n in SMALL], [mom[n] for n in SMALL],
                             [var[n] for n in SMALL], "adamw_small")
    for t, n in enumerate(SMALL):
        results[n] = [small_out[j * len(SMALL) + t] for j in range(4)]

    loss = small_out[4 * len(SMALL)][0, 0]
    grad_x = grad_x.reshape(batch, seq, D_MODEL)
    return (loss, grad_x, *[results[n][0] for n in WEIGHTS], *[results[n][1] for n in WEIGHTS],
            *[results[n][2] for n in WEIGHTS], *[results[n][3] for n in WEIGHTS])
```

```python
import functools
import math

import numpy as np
import jax
import jax.numpy as jnp
from jax import lax
from jax.experimental import pallas as pl
from jax.experimental.pallas import tpu as pltpu

F32 = jnp.float32
BF = jnp.bfloat16
SDS = jax.ShapeDtypeStruct
MESH = pl.DeviceIdType.MESH

D_MODEL = 1024
MEM_LEN = 256
MEM_HEADS = 4
HEAD_DIM = 128
GM_WIDTH = 512
GM_CHUNK = 128
GM_GROUPS = 4
MLA_HEADS = 8
MLA_NOPE = 128
MLA_ROPE = 64
MLA_V = 128
Q_LORA = 384
KV_LORA = 256
ROPE_BASE = 10000.0
D_FF = 4096
EPS = 1e-6
W_IN_COLS = 5312
ADAM_LR, ADAM_B1, ADAM_B2, ADAM_EPS, ADAM_WD, ADAM_STEP = 0.001, 0.9, 0.999, 1e-08, 0.01, 10

ZG, ZU, ZV, QM, CQ, KPE, CKV = 0, 3072, 3584, 4096, 4608, 4992, 5120
Z_COLS = 5376
LANES = 128
ROW_TILE = 512
ATT_TILE = 1024
ATT_HEADS = 2
ATT_HEADS_FWD = 4
VMEM_LIMIT = 60 * 1024 * 1024

N_CHIPS = 4
PIECE_ROWS = 256
SMALL_ROWS = 560

BIG = ["w_in", "w_uq", "w_ukv", "w_mem_kv", "w_o_gm", "w_o_mla", "w_o_mem", "w_out", "w_ff1", "w_ff2"]
BIG_SHAPE = {"w_in": (1024, 5312), "w_uq": (384, 1536), "w_ukv": (256, 2048), "w_mem_kv": (1024, 1024),
             "w_o_gm": (512, 1024), "w_o_mla": (1024, 1024), "w_o_mem": (512, 1024), "w_out": (1024, 1024),
             "w_ff1": (1024, 4096), "w_ff2": (4096, 1024)}
COL_SHARDED = {"w_in", "w_uq", "w_ukv", "w_o_gm", "w_o_mem", "w_ff1"}
EARLY = ["w_in", "w_uq", "w_ukv", "w_mem_kv"]
LATE_PROJ = ["w_o_gm", "w_o_mla", "w_o_mem", "w_out"]
LATE_FF = ["w_ff1", "w_ff2"]
LATE = LATE_PROJ + LATE_FF
SMALL = ["w_spatial", "b_spatial", "g_mix", "g_cq", "g_ckv", "g_q_nope", "g_q_pe", "g_k_nope", "g_k_pe", "g_gm_ln",
         "b_gm_ln", "g_mem", "g_mq", "g_mk", "g_ffn"]
SMALL_SHAPE = {"g_mix": (1, 1024), "g_cq": (1, 384), "g_ckv": (1, 256), "g_q_nope": (1, 128), "g_q_pe": (1, 64),
               "g_k_nope": (1, 128), "g_k_pe": (1, 64), "g_gm_ln": (1, 512), "b_gm_ln": (1, 512),
               "w_spatial": (1, 4, 128, 128), "b_spatial": (1, 4, 128), "g_mem": (1, 1024), "g_mq": (1, 128),
               "g_mk": (1, 128), "g_ffn": (1, 1024)}
WEIGHTS = ['g_mix', 'w_in', 'g_cq', 'w_uq', 'g_ckv', 'w_ukv', 'g_q_nope', 'g_q_pe', 'g_k_nope', 'g_k_pe',
           'g_gm_ln', 'b_gm_ln', 'w_spatial', 'b_spatial', 'g_mem', 'w_mem_kv', 'g_mq', 'g_mk', 'w_o_gm',
           'w_o_mla', 'w_o_mem', 'w_out', 'g_ffn', 'w_ff1', 'w_ff2']


def _params(sem=None):
    return pltpu.CompilerParams(vmem_limit_bytes=VMEM_LIMIT, dimension_semantics=sem)


def _pick(n, prefs):
    for p in prefs:
        if n % p == 0:
            return p
    return n


def _full(shape):
    nd = len(shape)
    return pl.BlockSpec(shape, lambda *_: (0,) * nd)


def _rows(t, w, blk=0):
    return pl.BlockSpec((t, w), lambda i: (i, blk))


def _acc(ref, val, first):
    @pl.when(first)
    def _():
        ref[...] = val

    @pl.when(jnp.logical_not(first))
    def _():
        ref[...] += val


ANY = pl.BlockSpec(memory_space=pl.ANY)


class _Phase:
    def __init__(self, operands, out_shapes, n_sem, n_local, copies, aliases=None):
        self.operands, self.out_shapes, self.aliases = list(operands), list(out_shapes), dict(aliases or {})
        self.n_sem, self.n_local, self.copies = n_sem, max(n_local, 1), copies

    def sem_shapes(self):
        return [pltpu.SemaphoreType.DMA((self.n_sem,)), pltpu.SemaphoreType.DMA((self.n_sem,)),
                pltpu.SemaphoreType.DMA((self.n_local,))]

    def start(self, ins, outs, send, recv, local):
        sends, _, locals_ = self.copies(ins, outs, send, recv, local)
        for cp in locals_ + sends:
            cp.start()

    def finish(self, ins, outs, send, recv, local):
        sends, arrivals, locals_ = self.copies(ins, outs, send, recv, local)
        for cp in arrivals:
            cp.wait_recv()
        for cp in sends:
            cp.wait_send()
        for cp in locals_:
            cp.wait()


class _Shifted:
    def __init__(self, ref, base):
        self.ref, self.base = ref, base

    @property
    def at(self):
        return self

    def __getitem__(self, i):
        return self.ref.at[i + self.base]


def _together(first, second):
    n_in, n_out = len(first.operands), len(first.out_shapes)

    def copies(ins, outs, send, recv, local):
        a = first.copies(ins[:n_in], outs[:n_out], send, recv, local)
        b = second.copies(ins[n_in:], outs[n_out:], _Shifted(send, first.n_sem), _Shifted(recv, first.n_sem),
                          _Shifted(local, first.n_local))
        return a[0] + b[0], a[1] + b[1], a[2] + b[2]

    aliases = {**first.aliases, **{n_in + i: n_out + j for i, j in second.aliases.items()}}
    return _Phase(first.operands + second.operands, first.out_shapes + second.out_shapes, first.n_sem + second.n_sem,
                  first.n_local + second.n_local, copies, aliases)


def _run_phase(phase, name):
    n_in = len(phase.operands)

    def body(*refs):
        ins, outs, sems = refs[:n_in], refs[n_in:n_in + len(phase.out_shapes)], refs[n_in + len(phase.out_shapes):]
        phase.start(ins, outs, *sems)
        phase.finish(ins, outs, *sems)

    return pl.pallas_call(body, in_specs=[ANY] * n_in, out_specs=[ANY] * len(phase.out_shapes),
                          out_shape=phase.out_shapes, scratch_shapes=phase.sem_shapes(),
                          input_output_aliases=phase.aliases, name=name)(*phase.operands)


def _pcall(body, *, grid, in_specs, out_specs, out_shape, scratch_shapes=(), sem=None, name, comm=None, aliases=None):
    single = not isinstance(out_shape, (list, tuple))
    o_specs = [out_specs] if single else list(out_specs)
    o_shape = [out_shape] if single else list(out_shape)
    aliases = dict(aliases or {})
    if comm is None:
        call = pl.pallas_call(body, grid=grid, in_specs=list(in_specs), out_specs=o_specs, out_shape=o_shape,
                              scratch_shapes=list(scratch_shapes), input_output_aliases=aliases,
                              compiler_params=_params(sem), name=name)

        def run_plain(*args):
            res = call(*args)
            return res[0] if single else res

        return run_plain

    n_in, n_out, n_scr = len(in_specs), len(o_specs), len(scratch_shapes)
    nc_in, nc_out = len(comm.operands), len(comm.out_shapes)

    def wrapped(*refs):
        ins, cins = refs[:n_in], refs[n_in:n_in + nc_in]
        o0 = n_in + nc_in
        outs, couts = refs[o0:o0 + n_out], refs[o0 + n_out:o0 + n_out + nc_out]
        s0 = o0 + n_out + nc_out
        scr, csem = refs[s0:s0 + n_scr], refs[s0 + n_scr:]
        ids = [pl.program_id(d) for d in range(len(grid))]
        first = functools.reduce(jnp.logical_and, [i == 0 for i in ids])
        last = functools.reduce(jnp.logical_and, [i == g - 1 for i, g in zip(ids, grid)])

        @pl.when(first)
        def _():
            comm.start(cins, couts, *csem)

        body(*ins, *outs, *scr)

        @pl.when(last)
        def _():
            comm.finish(cins, couts, *csem)

    call = pl.pallas_call(
        wrapped, grid=grid, in_specs=list(in_specs) + [ANY] * nc_in, out_specs=o_specs + [ANY] * nc_out,
        out_shape=o_shape + comm.out_shapes, scratch_shapes=list(scratch_shapes) + comm.sem_shapes(),
        input_output_aliases={**aliases, **{n_in + i: n_out + j for i, j in comm.aliases.items()}},
        compiler_params=_params(("arbitrary",) * len(grid)), name=name)

    def run_carrying(*args):
        res = call(*args, *comm.operands)
        return (res[0] if single else res[:n_out]), res[n_out:]

    return run_carrying


def _dn(a, b, ca, cb):
    return lax.dot_general(a.astype(BF), b.astype(BF), (((ca,), (cb,)), ((), ())), preferred_element_type=F32)


@jax.custom_vjp
def _mm_nn(a, b):
    return _dn(a, b, 1, 0)


def _mm_nn_fwd(a, b):
    return _dn(a, b, 1, 0), (a.astype(BF), b.astype(BF))


def _mm_nn_bwd(res, ct):
    a, b = res
    return _dn(ct, b, 1, 1), _dn(a, ct, 0, 0)


_mm_nn.defvjp(_mm_nn_fwd, _mm_nn_bwd)


@jax.custom_vjp
def _mm_nt(a, b):
    return _dn(a, b, 1, 1)


def _mm_nt_fwd(a, b):
    return _dn(a, b, 1, 1), (a.astype(BF), b.astype(BF))


def _mm_nt_bwd(res, ct):
    a, b = res
    return _dn(ct, b, 1, 0), _dn(ct, a, 0, 0)


_mm_nt.defvjp(_mm_nt_fwd, _mm_nt_bwd)


def _rmsn(x, g, n):
    ms = jnp.sum(x * x, axis=-1, keepdims=True) * (1.0 / n)
    return x * lax.rsqrt(ms + EPS) * g


def _layernorm(x, g, b):
    mu = jnp.mean(x, axis=-1, keepdims=True)
    xc = x - mu
    y = xc * lax.rsqrt(jnp.mean(xc * xc, axis=-1, keepdims=True) + EPS)
    return y * g + b


def _swap_lanes(x):
    half = MLA_ROPE // 2
    lane = lax.broadcasted_iota(jnp.int32, x.shape, 1)
    return jnp.where(lane < half, pltpu.roll(x, LANES - half, axis=1),
                     jnp.where(lane < MLA_ROPE, pltpu.roll(x, half, axis=1), 0.0))


@jax.custom_vjp
def _swap_halves(x):
    return _swap_lanes(x)


_swap_halves.defvjp(lambda x: (_swap_lanes(x), None), lambda _, ct: (_swap_lanes(ct),))


def _rope(x, cos_f, sin_s):
    return x * cos_f + _swap_halves(x) * sin_s


def _lane_blocks(x):
    return tuple(x[:, i * LANES:(i + 1) * LANES] for i in range(x.shape[1] // LANES))


@jax.custom_vjp
def _split_lanes(x):
    return _lane_blocks(x)


_split_lanes.defvjp(lambda x: (_lane_blocks(x), None), lambda _, cts: (jnp.concatenate(cts, axis=1),))


def _softmax(s):
    m = lax.stop_gradient(jnp.max(s, axis=-1, keepdims=True))
    p = jnp.exp(s - m)
    return p / jnp.sum(p, axis=-1, keepdims=True)


def _mm(a, b, *, ta=False, tb=False, ins=(), row_ins=(), epilogue=None, out_dtypes=(F32,), owner_cols=None,
        total=False, name, comm=None, rows=None, into=None):
    if ta:
        k_dim, m = a.shape
    else:
        m, k_dim = a.shape
    if tb:
        n, kb = b.shape
    else:
        kb, n = b.shape
    assert k_dim == kb, (a.shape, b.shape, ta, tb)
    part, n_parts = rows if rows is not None else (0, 1)
    tm = _pick(m // n_parts, (1024, 512, 256, 128))
    tn = _pick(n if owner_cols is None else owner_cols, (1024, 768, 512, 384, 256, 128))
    tk = _pick(k_dim, (2048, 1024, 768, 512, 256, 128))
    nk = k_dim // tk
    m_steps = m // tm // n_parts
    off = part * m_steps
    ca = 0 if ta else 1
    cb = 1 if tb else 0
    n_in = len(ins) + len(row_ins)
    n_out = len(out_dtypes)
    n_pass = 0 if into is None else 1
    total_shape = total if isinstance(total, tuple) else (8, LANES)
    assert not (isinstance(total, tuple) and n != tn), "a per-column total needs the whole width in one tile"

    def finish(r, in_refs, out_refs, first_tile):
        vals = epilogue(r, *[ref[...].astype(F32) for ref in in_refs]) if epilogue is not None else (r,)
        for ref, val, dt in zip(out_refs, vals, out_dtypes):
            ref[...] = val.astype(dt)
        if total:
            _acc(out_refs[n_out], vals[n_out], first_tile)

    def body(*refs):
        a_ref, b_ref = refs[:2]
        in_refs = refs[2:2 + n_in]
        o0 = 2 + n_in + n_pass
        out_refs = refs[o0:o0 + n_out + int(bool(total))]
        first_tile = jnp.logical_and(pl.program_id(0) == 0, pl.program_id(1) == 0)
        part = _dn(a_ref[...], b_ref[...], ca, cb)
        if nk == 1:
            finish(part, in_refs, out_refs, first_tile)
            return
        acc = refs[-1]
        k = pl.program_id(2)
        _acc(acc, part, k == 0)

        @pl.when(k == nk - 1)
        def _():
            finish(acc[...], in_refs, out_refs, first_tile)

    a_spec = (pl.BlockSpec((tk, tm), lambda i, j, k: (k, i + off)) if ta
              else pl.BlockSpec((tm, tk), lambda i, j, k: (i + off, k)))
    b_spec = pl.BlockSpec((tn, tk), lambda i, j, k: (j, k)) if tb else pl.BlockSpec((tk, tn), lambda i, j, k: (k, j))
    t_spec = pl.BlockSpec((tm, tn), lambda i, j, k: (i + off, j))
    if owner_cols is None:
        o_spec, o_shape = t_spec, (m, n)
    else:
        per = owner_cols // tn
        o_spec = pl.BlockSpec((None, tm, tn), lambda i, j, k: (j // per, i + off, j % per))
        o_shape = (n // owner_cols, m, owner_cols)
    o_specs = [o_spec] * n_out + ([pl.BlockSpec(total_shape, lambda i, j, k: (0, 0))] if total else [])
    o_shapes = [SDS(o_shape, dt) for dt in out_dtypes] + ([SDS(total_shape, F32)] if total else [])
    row_spec = pl.BlockSpec((1, tn), lambda i, j, k: (0, j))
    in_specs = [a_spec, b_spec] + [t_spec] * len(ins) + [row_spec] * len(row_ins) + [ANY] * n_pass
    args = [a, b, *ins, *row_ins] + ([into] if n_pass else [])
    run = _pcall(body, grid=(m_steps, n // tn, nk), in_specs=in_specs, out_specs=o_specs, out_shape=o_shapes,
                 scratch_shapes=[pltpu.VMEM((tm, tn), F32)] if nk > 1 else [],
                 sem=("arbitrary",) * 3 if total else ("parallel", "parallel", "arbitrary"), name=name, comm=comm,
                 aliases={len(in_specs) - 1: 0} if n_pass else None)
    if comm is None:
        outs = run(*args)
        return outs[0] if len(outs) == 1 else outs
    outs, exchanged = run(*args)
    return (outs[0] if len(outs) == 1 else outs), exchanged


def _add_to(r, x):
    return (r + x,)


def _residual_rms(r, x, g):
    x1 = r + x
    return x1, _rmsn(x1, g, D_MODEL)


def _rms_bwd_tail(dh, x, res, g):
    _, vjp = jax.vjp(lambda xx, gg: _rmsn(xx, gg, D_MODEL), x, g)
    dx, dg = vjp(dh)
    dx = dx + res
    return dx, dx, dg


def _relu2(r):
    p = jnp.maximum(r, 0.0)
    return r, p * p


def _relu2_bwd(dr, a):
    return (dr * (2.0 * jnp.maximum(a, 0.0)),)


def _loss_tail(r, x1, tgt):
    e = (r + x1) - tgt
    dy = e * (1.0 / D_MODEL)
    part = jnp.sum(jnp.sum(e * e, axis=-1, keepdims=True), axis=0, keepdims=True) * (0.5 / D_MODEL)
    return dy, dy, jnp.broadcast_to(part, (8, LANES))


def _rms_fwd(x, g, name, comm=None):
    n, w = x.shape
    t = min(ROW_TILE, n)

    def body(x_ref, g_ref, o_ref):
        o_ref[...] = _rmsn(x_ref[...], g_ref[...], w).astype(BF)

    return _pcall(body, grid=(n // t,), in_specs=[_rows(t, w), _full((1, w))], out_specs=_rows(t, w),
                  out_shape=SDS((n, w), BF), sem=("arbitrary",), name=name, comm=comm)(x, g)


def _rms_bwd(x, g, dh, res, name, want_dx=True):
    n, w = x.shape
    t = min(ROW_TILE, n)
    has_res = res is not None

    def body(*refs):
        x_ref, g_ref, dh_ref = refs[:3]
        _, vjp = jax.vjp(lambda xx, gg: _rmsn(xx, gg, w), x_ref[...], g_ref[...])
        dx, dg = vjp(dh_ref[...])
        if want_dx:
            refs[-2][...] = dx + refs[3][...] if has_res else dx
        _acc(refs[-1], dg, pl.program_id(0) == 0)

    in_specs = [_rows(t, w), _full((1, w)), _rows(t, w)] + ([_rows(t, w)] if has_res else [])
    args = [x, g, dh] + ([res] if has_res else [])
    return _pcall(body, grid=(n // t,), in_specs=in_specs,
                  out_specs=([_rows(t, w)] if want_dx else []) + [_full((1, w))],
                  out_shape=([SDS((n, w), F32)] if want_dx else []) + [SDS((1, w), F32)], sem=("arbitrary",),
                  name=name)(*args)


def _merge_core(zg0, zg1, zg2, y0, y1, y2):
    return jax.nn.sigmoid(zg0) * y0 + jax.nn.sigmoid(zg1) * y1 + jax.nn.sigmoid(zg2) * y2


def _merge_fwd(z, y_gm, y_mla, y_mem, name):
    n = z.shape[0]
    t = min(ROW_TILE, n)
    w = D_MODEL

    def body(g0, g1, g2, y0, y1, y2, o_ref):
        o_ref[...] = _merge_core(g0[...].astype(F32), g1[...].astype(F32), g2[...].astype(F32), y0[...].astype(F32), y1[...].astype(F32),
                                 y2[...].astype(F32)).astype(BF)

    return pl.pallas_call(body, grid=(n // t,),
                          in_specs=[_rows(t, w, 0), _rows(t, w, 1), _rows(t, w, 2)] + [_rows(t, w)] * 3,
                          out_specs=_rows(t, w), out_shape=SDS((n, w), BF),
                          compiler_params=_params(("parallel",)), name=name)(z, z, z, y_gm, y_mla, y_mem)


def _merge_bwd(z, y_gm, y_mla, y_mem, dmerged, name):
    n = z.shape[0]
    t = min(ROW_TILE, n)
    w = D_MODEL

    def body(g0, g1, g2, y0, y1, y2, dm, dzg_ref, d0_ref, d1_ref, d2_ref):
        _, vjp = jax.vjp(_merge_core, g0[...].astype(F32), g1[...].astype(F32), g2[...].astype(F32), y0[...].astype(F32), y1[...].astype(F32),
                         y2[...].astype(F32))
        dg0, dg1, dg2, dy0, dy1, dy2 = vjp(dm[...])
        dzg_ref[:, 0:w] = dg0.astype(BF)
        dzg_ref[:, w:2 * w] = dg1.astype(BF)
        dzg_ref[:, 2 * w:3 * w] = dg2.astype(BF)
        d0_ref[...] = dy0.astype(BF)
        d1_ref[...] = dy1.astype(BF)
        d2_ref[...] = dy2.astype(BF)

    return pl.pallas_call(body, grid=(n // t,),
                          in_specs=[_rows(t, w, 0), _rows(t, w, 1), _rows(t, w, 2)] + [_rows(t, w)] * 4,
                          out_specs=[_rows(t, 3 * w, ZG // (3 * w))] + [_rows(t, w)] * 3,
                          out_shape=[SDS((n, Z_COLS), BF)] + [SDS((n, w), BF)] * 3,
                          compiler_params=_params(("parallel",)), name=name)(z, z, z, y_gm, y_mla, y_mem, dmerged)


def _gm_core(zu, zv, g_ln, b_ln, ws, bcols):
    t = zu.shape[0]
    u = jax.nn.gelu(zu)
    v = _layernorm(jax.nn.gelu(zv), g_ln, b_ln)
    row = lax.broadcasted_iota(jnp.int32, (GM_CHUNK, GM_CHUNK), 0)
    col = lax.broadcasted_iota(jnp.int32, (GM_CHUNK, GM_CHUNK), 1)
    wc = [jnp.where(row >= col, ws[g], 0.0) for g in range(GM_GROUPS)]
    chunks = []
    for c in range(t // GM_CHUNK):
        cols = []
        for g in range(GM_GROUPS):
            vc = v[c * GM_CHUNK:(c + 1) * GM_CHUNK, g * LANES:(g + 1) * LANES]
            cols.append(_mm_nn(wc[g], vc) + bcols[g])
        chunks.append(jnp.concatenate(cols, axis=1))
    mixed = chunks[0] if len(chunks) == 1 else jnp.concatenate(chunks, axis=0)
    return u * mixed


def _gm_specs(t):
    return [_rows(t, GM_WIDTH, ZU // GM_WIDTH), _rows(t, GM_WIDTH, ZV // GM_WIDTH), _full((1, GM_WIDTH)),
            _full((1, GM_WIDTH)), _full((GM_GROUPS, GM_CHUNK, GM_CHUNK))] + [_full((GM_CHUNK, 1))] * GM_GROUPS


def _gm_fwd(z, g_ln, b_ln, ws, bcols, name):
    n = z.shape[0]
    t = min(ROW_TILE, n)

    def body(zu, zv, g_ref, b_ref, ws_ref, c0, c1, c2, c3, o_ref):
        out = _gm_core(zu[...].astype(F32), zv[...].astype(F32), g_ref[...], b_ref[...], [ws_ref[g] for g in range(GM_GROUPS)],
                       [c0[...], c1[...], c2[...], c3[...]])
        o_ref[...] = out.astype(BF)

    return pl.pallas_call(body, grid=(n // t,), in_specs=_gm_specs(t), out_specs=_rows(t, GM_WIDTH),
                          out_shape=SDS((n, GM_WIDTH), BF), compiler_params=_params(("parallel",)),
                          name=name)(z, z, g_ln, b_ln, ws, *bcols)


def _gm_bwd(z, g_ln, b_ln, ws, bcols, dgm, dz, name, comm=None):
    n = z.shape[0]
    t = min(ROW_TILE, n)

    def body(zu, zv, g_ref, b_ref, ws_ref, c0, c1, c2, c3, dgm_ref, _, dz_ref, dg_ref, db_ref, dws_ref, e0, e1, e2,
             e3):
        first = pl.program_id(0) == 0
        _, vjp = jax.vjp(_gm_core, zu[...].astype(F32), zv[...].astype(F32), g_ref[...], b_ref[...],
                         [ws_ref[g] for g in range(GM_GROUPS)], [c0[...], c1[...], c2[...], c3[...]])
        dzu, dzv, dg, db, dws, dcols = vjp(dgm_ref[...])
        dz_ref[:, 0:GM_WIDTH] = dzu.astype(BF)
        dz_ref[:, GM_WIDTH:2 * GM_WIDTH] = dzv.astype(BF)
        _acc(dg_ref, dg, first)
        _acc(db_ref, db, first)
        _acc(dws_ref, jnp.stack(dws, axis=0), first)
        for ref, val in zip((e0, e1, e2, e3), dcols):
            _acc(ref, val, first)

    in_specs = _gm_specs(t) + [_rows(t, GM_WIDTH), ANY]
    return _pcall(
        body, grid=(n // t,), in_specs=in_specs,
        out_specs=[_rows(t, 2 * GM_WIDTH, ZU // (2 * GM_WIDTH)), _full((1, GM_WIDTH)), _full((1, GM_WIDTH)),
                   _full((GM_GROUPS, GM_CHUNK, GM_CHUNK))] + [_full((GM_CHUNK, 1))] * GM_GROUPS,
        out_shape=[SDS((n, Z_COLS), BF), SDS((1, GM_WIDTH), F32), SDS((1, GM_WIDTH), F32),
                   SDS((GM_GROUPS, GM_CHUNK, GM_CHUNK), F32)] + [SDS((GM_CHUNK, 1), F32)] * GM_GROUPS,
        sem=("arbitrary",), name=name, comm=comm, aliases={len(in_specs) - 1: 0})(z, z, g_ln, b_ln, ws, *bcols, dgm, dz)


def _rope_tables(pos_f, inv_full, cmask, smask, name, comm=None):
    n = pos_f.shape[0]
    t = min(ROW_TILE, n)

    def body(p_ref, inv_ref, cm_ref, sm_ref, cos_ref, sin_ref):
        ang = p_ref[...] * inv_ref[...]
        cos_ref[...] = jnp.cos(ang) * cm_ref[...]
        sin_ref[...] = jnp.sin(ang) * sm_ref[...]

    return _pcall(body, grid=(n // t,), in_specs=[_rows(t, 1)] + [_full((1, LANES))] * 3,
                  out_specs=[_rows(t, LANES)] * 2, out_shape=[SDS((n, LANES), F32)] * 2, sem=("parallel",),
                  name=name, comm=comm)(pos_f, inv_full, cmask, smask)


def _prep_norms(cq, ckv, g_cq, g_ckv):
    return _rmsn(cq, g_cq, Q_LORA), _rmsn(ckv, g_ckv, KV_LORA)


def _prep_heads(qa, kva, kpe, head_gains, cos_f, sin_s):
    g_qn, g_qp, g_kn, g_kp = head_gains
    qs = _split_lanes(qa)
    kvs = _split_lanes(kva)
    kp = _rope(_rmsn(kpe, g_kp, MLA_ROPE), cos_f, sin_s)
    q_out, k_out = [], []
    for h in range(MLA_HEADS):
        q_out.append(_rmsn(qs[h], g_qn, MLA_NOPE))
        q_out.append(_rope(_rmsn(qs[MLA_HEADS + h], g_qp, MLA_ROPE), cos_f, sin_s))
        k_out.append(_rmsn(kvs[h], g_kn, MLA_NOPE))
        k_out.append(kp)
    return (jnp.concatenate(q_out, axis=1), jnp.concatenate(k_out, axis=1),
            jnp.concatenate(kvs[MLA_HEADS:], axis=1))


def _prep_in_specs(t):
    return ([_rows(t, Q_LORA, CQ // Q_LORA), _rows(t, LANES, KPE // LANES), _rows(t, KV_LORA, CKV // KV_LORA),
             _rows(t, LANES), _rows(t, LANES), _full((1, Q_LORA)), _full((1, KV_LORA))] + [_full((1, LANES))] * 4
            + [_full((Q_LORA, 2048)), _full((KV_LORA, 2048))])


def _prep_fwd(z, cos_f, sin_s, gains, wq, wkv, name):
    n = z.shape[0]
    t = min(ROW_TILE, n)

    def body(cq, kpe, ckv, cos_ref, sin_ref, g_cq, g_ckv, g_qn, g_qp, g_kn, g_kp, wq_ref, wkv_ref, q_ref, k_ref, v_ref):
        cqn, ckvn = _prep_norms(cq[...].astype(F32), ckv[...].astype(F32), g_cq[...], g_ckv[...])
        qa = _dn(cqn, wq_ref[...], 1, 0)
        kva = _dn(ckvn, wkv_ref[...], 1, 0)
        q, k, v = _prep_heads(qa, kva, kpe[...].astype(F32), (g_qn[...], g_qp[...], g_kn[...], g_kp[...]), cos_ref[...],
                              sin_ref[...])
        q_ref[...] = q.astype(BF)
        k_ref[...] = k.astype(BF)
        v_ref[...] = v.astype(BF)

    return pl.pallas_call(body, grid=(n // t,), in_specs=_prep_in_specs(t),
                          out_specs=[_rows(t, 2048), _rows(t, 2048), _rows(t, 1024)],
                          out_shape=[SDS((n, 2048), BF), SDS((n, 2048), BF), SDS((n, 1024), BF)],
                          compiler_params=_params(("parallel",)),
                          name=name)(z, z, z, cos_f, sin_s, *gains, wq, wkv)


def _prep_bwd(z, cos_f, sin_s, gains, wq, wkv, dq, dk, dv, dz, name, comm=None):
    n = z.shape[0]
    t = min(ROW_TILE, n)
    wz = Q_LORA + LANES + KV_LORA

    def body(cq, kpe, ckv, cos_ref, sin_ref, g_cq, g_ckv, g_qn, g_qp, g_kn, g_kp, wq_ref, wkv_ref, dq_ref, dk_ref,
             dv_ref, _, dz_ref, o_cq, o_ckv, o_qn, o_qp, o_kn, o_kp, dwq_ref, dwkv_ref):
        first = pl.program_id(0) == 0
        cos_t, sin_t = cos_ref[...], sin_ref[...]
        (cqn, ckvn), vjp_norms = jax.vjp(_prep_norms, cq[...].astype(F32), ckv[...].astype(F32), g_cq[...], g_ckv[...])
        wq_t, wkv_t = wq_ref[...], wkv_ref[...]
        qa = _dn(cqn, wq_t, 1, 0)
        kva = _dn(ckvn, wkv_t, 1, 0)
        _, vjp_heads = jax.vjp(lambda a, b, c, g: _prep_heads(a, b, c, g, cos_t, sin_t), qa, kva, kpe[...].astype(F32),
                               (g_qn[...], g_qp[...], g_kn[...], g_kp[...]))
        dqa, dkva, dkpe, dhead = vjp_heads((dq_ref[...], dk_ref[...], dv_ref[...]))
        _acc(dwq_ref, _dn(cqn, dqa, 0, 0), first)
        _acc(dwkv_ref, _dn(ckvn, dkva, 0, 0), first)
        dcq, dckv, dg_cq, dg_ckv = vjp_norms((_dn(dqa, wq_t, 1, 1), _dn(dkva, wkv_t, 1, 1)))
        dz_ref[:, 0:Q_LORA] = dcq.astype(BF)
        dz_ref[:, Q_LORA:Q_LORA + LANES] = dkpe.astype(BF)
        dz_ref[:, Q_LORA + LANES:wz] = dckv.astype(BF)
        for ref, val in zip((o_cq, o_ckv, o_qn, o_qp, o_kn, o_kp), (dg_cq, dg_ckv) + tuple(dhead)):
            _acc(ref, val, first)

    gain_specs = [_full((1, Q_LORA)), _full((1, KV_LORA))] + [_full((1, LANES))] * 4
    gain_shapes = [SDS((1, Q_LORA), F32), SDS((1, KV_LORA), F32)] + [SDS((1, LANES), F32)] * 4
    in_specs = _prep_in_specs(t) + [_rows(t, 2048), _rows(t, 2048), _rows(t, 1024), ANY]
    return _pcall(
        body, grid=(n // t,), in_specs=in_specs,
        out_specs=[_rows(t, wz, CQ // wz)] + gain_specs + [_full((Q_LORA, 2048)), _full((KV_LORA, 2048))],
        out_shape=[SDS((n, Z_COLS), BF)] + gain_shapes + [SDS((Q_LORA, 2048), F32), SDS((KV_LORA, 2048), F32)],
        sem=("arbitrary",), name=name, comm=comm,
        aliases={len(in_specs) - 1: 0})(z, z, z, cos_f, sin_s, *gains, wq, wkv, dq, dk, dv, dz)


MLA_QK = 256
MLA_SCALE = 1.0 / math.sqrt(MLA_NOPE + MLA_ROPE)
LOG2E = 1.0 / math.log(2.0)
MLA_SCALE_LOG2E = MLA_SCALE * LOG2E


def _causal_mask(s, q0, k0):
    tq, tk = s.shape
    row = q0 + lax.broadcasted_iota(jnp.int32, (tq, tk), 0)
    col = k0 + lax.broadcasted_iota(jnp.int32, (tq, tk), 1)
    return jnp.where(row >= col, s, -jnp.inf)


def _mla_fwd(q, k, v, batch, seq, name, comm=None):
    n = q.shape[0]
    tq = min(ATT_TILE, seq)
    nq = seq // tq

    nh = ATT_HEADS_FWD

    def body(q_ref, k_ref, v_ref, o_ref, lse_ref):
        i = pl.program_id(2)

        def step(j, carry, diagonal=False):
            k0 = pl.multiple_of(j * tq, tq)
            out = []
            ones = jnp.ones((tq, LANES), BF)
            for hh in range(nh):
                m, acc = carry[hh]
                qb = q_ref[:, hh * MLA_QK:(hh + 1) * MLA_QK]
                kb = k_ref[pl.ds(k0, tq), hh * MLA_QK:(hh + 1) * MLA_QK]
                vb = v_ref[pl.ds(k0, tq), hh * MLA_V:(hh + 1) * MLA_V]
                s = _dn(qb, kb, 1, 1)
                if diagonal:
                    s = _causal_mask(s, i * tq, k0)
                m_new = jnp.maximum(m, jnp.max(s, axis=-1, keepdims=True))
                p = jnp.exp2((s - m_new) * MLA_SCALE_LOG2E)
                alpha = jnp.exp2((m - m_new) * MLA_SCALE_LOG2E)
                acc = alpha * acc + _dn(p, jnp.concatenate([vb, ones], axis=1), 1, 0)
                out.append((m_new, acc))
            return tuple(out)

        init = tuple((jnp.full((tq, 1), -jnp.inf, F32), jnp.zeros((tq, MLA_V + LANES), F32)) for _ in range(nh))
        final = step(i, lax.fori_loop(0, i, step, init), diagonal=True)
        for hh, (m, acc) in enumerate(final):
            l = acc[:, MLA_V:MLA_V + 1]
            o_ref[:, hh * MLA_V:(hh + 1) * MLA_V] = acc[:, :MLA_V] / l
            lse_ref[:, hh * LANES:(hh + 1) * LANES] = jnp.broadcast_to(m * MLA_SCALE + jnp.log(l), (tq, LANES))

    return _pcall(
        body, grid=(batch, MLA_HEADS // nh, nq),
        in_specs=[pl.BlockSpec((tq, nh * MLA_QK), lambda b, h, i: (b * nq + i, h)),
                  pl.BlockSpec((seq, nh * MLA_QK), lambda b, h, i: (b, h)),
                  pl.BlockSpec((seq, nh * MLA_V), lambda b, h, i: (b, h))],
        out_specs=[pl.BlockSpec((tq, nh * MLA_V), lambda b, h, i: (b * nq + i, h)),
                   pl.BlockSpec((tq, nh * LANES), lambda b, h, i: (b * nq + i, h))],
        out_shape=[SDS((n, MLA_HEADS * MLA_V), F32), SDS((n, MLA_HEADS * LANES), F32)],
        sem=("parallel", "parallel", "arbitrary"), name=name, comm=comm)(q, k, v)


def _mla_bwd(q, k, v, o, lse, do, batch, seq, name, comm=None):
    n = q.shape[0]
    tk = min(ATT_TILE, seq)
    nk = seq // tk

    nh = ATT_HEADS

    def body(q_ref, k_ref, v_ref, o_ref, lse_ref, do_ref, dq_ref, dk_ref, dv_ref):
        jk = pl.program_id(2)

        @pl.when(jk == 0)
        def _():
            dq_ref[...] = jnp.zeros_like(dq_ref)

        def step(i, carry, diagonal=False):
            q0 = pl.multiple_of(i * tk, tk)
            rows = pl.ds(q0, tk)
            out = []
            for hh in range(nh):
                dk_acc, dv_acc = carry[hh]
                qk_cols = slice(hh * MLA_QK, (hh + 1) * MLA_QK)
                v_cols = slice(hh * MLA_V, (hh + 1) * MLA_V)
                kb = k_ref[:, qk_cols]
                vb = v_ref[:, v_cols]
                qb = q_ref[rows, qk_cols]
                dob = do_ref[rows, v_cols]
                delta = jnp.sum(dob * o_ref[rows, v_cols], axis=-1, keepdims=True)
                s = _dn(qb, kb, 1, 1)
                if diagonal:
                    s = _causal_mask(s, q0, jk * tk)
                p = jnp.exp2(s * MLA_SCALE_LOG2E - lse_ref[rows, hh * LANES:hh * LANES + 1] * LOG2E)
                dv_acc = dv_acc + _dn(p, dob, 0, 0)
                dp = _dn(dob, vb, 1, 1)
                ds = p * (dp - delta) * MLA_SCALE
                dk_acc = dk_acc + _dn(ds, qb, 0, 0)
                dq_ref[rows, qk_cols] += _dn(ds, kb, 1, 0)
                out.append((dk_acc, dv_acc))
            return tuple(out)

        init = tuple((jnp.zeros((tk, MLA_QK), F32), jnp.zeros((tk, MLA_V), F32)) for _ in range(nh))
        final = lax.fori_loop(jk + 1, nk, step, step(jk, init, diagonal=True))
        for hh, (dk_acc, dv_acc) in enumerate(final):
            dk_ref[:, hh * MLA_QK:(hh + 1) * MLA_QK] = dk_acc
            dv_ref[:, hh * MLA_V:(hh + 1) * MLA_V] = dv_acc

    full_qk = pl.BlockSpec((seq, nh * MLA_QK), lambda b, h, j: (b, h))
    full_v = pl.BlockSpec((seq, nh * MLA_V), lambda b, h, j: (b, h))
    blk_qk = pl.BlockSpec((tk, nh * MLA_QK), lambda b, h, j: (b * nk + j, h))
    blk_v = pl.BlockSpec((tk, nh * MLA_V), lambda b, h, j: (b * nk + j, h))
    return _pcall(
        body, grid=(batch, MLA_HEADS // nh, nk),
        in_specs=[full_qk, blk_qk, blk_v, full_v, full_v, full_v],
        out_specs=[full_qk, blk_qk, blk_v],
        out_shape=[SDS((n, MLA_HEADS * MLA_QK), F32), SDS((n, MLA_HEADS * MLA_QK), F32),
                   SDS((n, MLA_HEADS * MLA_V), F32)],
        sem=("parallel", "parallel", "arbitrary"), name=name, comm=comm)(q, k, v, o, lse, do)


MEM_SCALE = 1.0 / math.sqrt(HEAD_DIM)
MEM_W = MEM_HEADS * HEAD_DIM


def _mem_core(qs, ks, vs, g_mq, g_mk):
    outs = []
    for h in range(MEM_HEADS):
        qh = _rmsn(qs[h], g_mq, HEAD_DIM)
        kh = _rmsn(ks[h], g_mk, HEAD_DIM)
        p = _softmax(_mm_nt(qh, kh) * MEM_SCALE)
        outs.append(_mm_nn(p, vs[h]))
    return jnp.concatenate(outs, axis=1)


def _mem_load(qm, kvm, g_mq, g_mk):
    hs = range(MEM_HEADS)
    qs = [qm[:, h * LANES:(h + 1) * LANES].astype(F32) for h in hs]
    ks = [kvm[:, h * LANES:(h + 1) * LANES] for h in hs]
    vs = [kvm[:, MEM_W + h * LANES:MEM_W + (h + 1) * LANES] for h in hs]
    return qs, ks, vs, g_mq[...], g_mk[...]


def _mem_fwd(z, kvm, g_mq, g_mk, batch, seq, name, comm=None):
    n = z.shape[0]
    t = min(ROW_TILE, seq)
    per = seq // t

    def body(qm, kvm_ref, gq, gk, o_ref):
        o_ref[...] = _mem_core(*_mem_load(qm, kvm_ref, gq, gk)).astype(BF)

    return _pcall(
        body, grid=(n // t,),
        in_specs=[_rows(t, MEM_W, QM // MEM_W), pl.BlockSpec((MEM_LEN, 2 * MEM_W), lambda i: (i // per, 0)),
                  _full((1, LANES)), _full((1, LANES))],
        out_specs=_rows(t, MEM_W), out_shape=SDS((n, MEM_W), BF), sem=("parallel",), name=name,
        comm=comm)(z, kvm, g_mq, g_mk)


def _mem_bwd(z, kvm, g_mq, g_mk, dom, dz, batch, seq, name):
    n = z.shape[0]
    t = min(ROW_TILE, seq)
    per = seq // t

    def body(qm, kvm_ref, gq, gk, dom_ref, _, dz_ref, dkvm_ref, dgq_ref, dgk_ref):
        i = pl.program_id(0)
        _, vjp = jax.vjp(_mem_core, *_mem_load(qm, kvm_ref, gq, gk))
        dqs, dks, dvs, dgq, dgk = vjp(dom_ref[...])
        dz_ref[...] = jnp.concatenate(dqs, axis=1).astype(BF)
        _acc(dkvm_ref, jnp.concatenate(dks + dvs, axis=1), i % per == 0)
        _acc(dgq_ref, dgq, i == 0)
        _acc(dgk_ref, dgk, i == 0)

    kv_spec = pl.BlockSpec((MEM_LEN, 2 * MEM_W), lambda i: (i // per, 0))
    return pl.pallas_call(
        body, grid=(n // t,),
        in_specs=[_rows(t, MEM_W, QM // MEM_W), kv_spec, _full((1, LANES)), _full((1, LANES)), _rows(t, MEM_W), ANY],
        out_specs=[_rows(t, MEM_W, QM // MEM_W), kv_spec, _full((1, LANES)), _full((1, LANES))],
        out_shape=[SDS((n, Z_COLS), BF), SDS((batch * MEM_LEN, 2 * MEM_W), F32), SDS((1, LANES), F32),
                   SDS((1, LANES), F32)],
        input_output_aliases={5: 0},
        compiler_params=_params(("arbitrary",)), name=name)(z, kvm, g_mq, g_mk, dom, dz)


def _me():
    return lax.axis_index("x"), lax.axis_index("y"), lax.axis_index("c")


def _other_chips(x, y):
    return [(1 - x, y), (x, 1 - y), (1 - x, 1 - y)]


def _shard_shape(name):
    r, c = BIG_SHAPE[name]
    return (r, c // N_CHIPS) if name in COL_SHARDED else (r // N_CHIPS, c)


def _n_pieces(half_rows):
    for n in range(max(1, half_rows // PIECE_ROWS), 0, -1):
        if half_rows % n == 0 and (half_rows // n) % 16 == 0:
            return n
    return 1


def _piece_plan(shapes):
    plan = []
    for r, _ in shapes:
        h = r // 2
        n = _n_pieces(h)
        plan.append((h, n, h // n))
    return plan


def _remote(send, recv, sem, src, dst, to):
    return pltpu.make_async_remote_copy(src_ref=src, dst_ref=dst, send_sem=send.at[sem], recv_sem=recv.at[sem],
                                        device_id=to, device_id_type=MESH)


def _gather_far(shards):
    plan = _piece_plan([s.shape for s in shards])
    n_far = 3 * sum(n for _, n, _ in plan)
    n_loc = 2 * sum(n for _, n, _ in plan)

    def copies(s_refs, o_refs, send, recv, local):
        x, y, c = _me()
        k = 2 * x + y
        mine, sends, arrivals = [], [], []
        for t, (h, n, pr) in enumerate(plan):
            s_ref, o_ref = s_refs[t], o_refs[t]
            for core in range(2):
                for p in range(n):
                    rows = pl.ds(core * h + p * pr, pr)
                    mine.append(pltpu.make_async_copy(s_ref.at[rows], o_ref.at[k, rows], local.at[len(mine)]))
            for chip in _other_chips(x, y):
                for p in range(n):
                    rows = pl.ds(c * h + p * pr, pr)
                    s = len(sends)
                    sends.append(_remote(send, recv, s, s_ref.at[rows], o_ref.at[k, rows], (*chip, c)))
                    arrivals.append(_remote(send, recv, s, s_ref.at[rows], o_ref.at[2 * chip[0] + chip[1], rows],
                                            (*chip, c)))
        return sends, arrivals, mine

    return _Phase(shards, [SDS((N_CHIPS,) + s.shape, s.dtype) for s in shards], n_far, n_loc, copies)


def _gather_near(bufs):
    plan = _piece_plan([b.shape[1:] for b in bufs])
    n_sem = 3 * sum(n for _, n, _ in plan)

    def copies(i_refs, o_refs, send, recv, local):
        x, y, c = _me()
        sib = (x, y, 1 - c)
        sends, arrivals = [], []
        for t, (h, n, pr) in enumerate(plan):
            for chip in _other_chips(x, y):
                ci = 2 * chip[0] + chip[1]
                for p in range(n):
                    rows = pl.ds(c * h + p * pr, pr)
                    rows_sib = pl.ds((1 - c) * h + p * pr, pr)
                    s = len(sends)
                    sends.append(_remote(send, recv, s, i_refs[t].at[ci, rows], o_refs[t].at[ci, rows], sib))
                    arrivals.append(_remote(send, recv, s, i_refs[t].at[ci, rows_sib], o_refs[t].at[ci, rows_sib], sib))
        return sends, arrivals, []

    return _Phase(bufs, [SDS(b.shape, b.dtype) for b in bufs], n_sem, 0, copies, {t: t for t in range(len(bufs))})


def _pair_exchange(grads):
    plan = _piece_plan([g.shape[1:] for g in grads])
    n_sem = sum(n for _, n, _ in plan)

    def copies(g_refs, o_refs, send, recv, local):
        x, y, c = _me()
        sends = []
        for t, (h, n, pr) in enumerate(plan):
            for p in range(n):
                sends.append(_remote(send, recv, len(sends), g_refs[t].at[:, pl.ds((1 - c) * h + p * pr, pr)],
                                     o_refs[t].at[:, pl.ds(p * pr, pr)], (x, y, 1 - c)))
        return sends, sends, []

    return _Phase(grads, [SDS((N_CHIPS, g.shape[1] // 2, g.shape[2]), F32) for g in grads], n_sem, 0, copies)


def _pair_add(ck, g, theirs, name):
    _, r, c = g.shape
    (h, n, pr), = _piece_plan([(r, c)])

    def body(ck_ref, g_ref, t_ref, pbf_ref):
        pbf_ref[...] = (g_ref[...] + t_ref[...]).astype(BF)

    half = pl.BlockSpec((None, pr, c), lambda k, p, ck: (k, p, 0))
    spec = pltpu.PrefetchScalarGridSpec(
        num_scalar_prefetch=1, grid=(N_CHIPS, n),
        in_specs=[pl.BlockSpec((None, pr, c), lambda k, p, ck: (k, ck[0] * n + p, 0)), half], out_specs=half)
    return pl.pallas_call(body, grid_spec=spec, out_shape=SDS((N_CHIPS, h, c), BF),
                          compiler_params=_params(("arbitrary", "arbitrary")), name=name)(ck, g, theirs)


def _scatter_partials(pbfs):
    plan = [(h, _n_pieces(h), h // _n_pieces(h)) for h in [p.shape[1] for p in pbfs]]
    n_sem = 3 * sum(n for _, n, _ in plan)

    def copies(p_refs, o_refs, send, recv, local):
        x, y, c = _me()
        sends = []
        for t, (h, n, pr) in enumerate(plan):
            for j, chip in enumerate(_other_chips(x, y)):
                for p in range(n):
                    rows = pl.ds(p * pr, pr)
                    sends.append(_remote(send, recv, len(sends), p_refs[t].at[2 * chip[0] + chip[1], rows],
                                         o_refs[t].at[j, rows], (*chip, c)))
        return sends, sends, []

    return _Phase(pbfs, [SDS((3,) + p.shape[1:], BF) for p in pbfs], n_sem, 0, copies)


def _sum_chips(ck, pbf, slots, name):
    _, h, c = pbf.shape
    n = _n_pieces(h)
    pr = h // n

    def body(ck_ref, p_ref, s_ref, o_ref):
        o_ref[...] = (((p_ref[...].astype(F32) + s_ref[0].astype(F32)) + s_ref[1].astype(F32))
                      + s_ref[2].astype(F32))

    spec = pltpu.PrefetchScalarGridSpec(
        num_scalar_prefetch=1, grid=(n,),
        in_specs=[pl.BlockSpec((None, pr, c), lambda p, ck: (ck[1], p, 0)),
                  pl.BlockSpec((3, pr, c), lambda p, ck: (0, p, 0))],
        out_specs=pl.BlockSpec((pr, c), lambda p, ck: (ck[0] * n + p, 0)))
    return pl.pallas_call(body, grid_spec=spec, out_shape=SDS((2 * h, c), F32),
                          compiler_params=_params(("arbitrary",)), name=name)(ck, pbf, slots)


def _join_halves(sums):
    plan = _piece_plan([s.shape for s in sums])
    n_sem = sum(n for _, n, _ in plan)

    def copies(r_refs, o_refs, send, recv, local):
        x, y, c = _me()
        sends, arrivals = [], []
        for t, (h, n, pr) in enumerate(plan):
            for p in range(n):
                rows = pl.ds(c * h + p * pr, pr)
                rows_sib = pl.ds((1 - c) * h + p * pr, pr)
                s = len(sends)
                sends.append(_remote(send, recv, s, r_refs[t].at[rows], o_refs[t].at[rows], (x, y, 1 - c)))
                arrivals.append(_remote(send, recv, s, r_refs[t].at[rows_sib], o_refs[t].at[rows_sib], (x, y, 1 - c)))
        return sends, arrivals, []

    return _Phase(sums, [SDS(s.shape, F32) for s in sums], n_sem, 0, copies, {t: t for t in range(len(sums))})


def _gather_small(s, name):
    def body(s_ref, o_ref, send, recv, local):
        x, y, c = _me()
        me = 4 * x + 2 * y + c
        keep = pltpu.make_async_copy(s_ref, o_ref.at[me], local)
        keep.start()
        sends = []
        for r in range(1, 8):
            fx, fy, fc = (r >> 2) & 1, (r >> 1) & 1, r & 1
            to = (x ^ fx, y ^ fy, c ^ fc)
            sends.append(pltpu.make_async_remote_copy(
                src_ref=s_ref, dst_ref=o_ref.at[me], send_sem=send.at[r - 1], recv_sem=recv.at[r - 1],
                device_id=to, device_id_type=MESH))
        for cp in sends:
            cp.start()
        for r in range(1, 8):
            fx, fy, fc = (r >> 2) & 1, (r >> 1) & 1, r & 1
            src = 4 * (x ^ fx) + 2 * (y ^ fy) + (c ^ fc)
            pltpu.make_async_remote_copy(
                src_ref=s_ref, dst_ref=o_ref.at[src], send_sem=send.at[r - 1], recv_sem=recv.at[r - 1],
                device_id=(x ^ fx, y ^ fy, c ^ fc), device_id_type=MESH).wait_recv()
        for cp in sends:
            cp.wait_send()
        keep.wait()

    return pl.pallas_call(
        body, in_specs=[ANY], out_specs=ANY, out_shape=SDS((8, SMALL_ROWS, LANES), F32),
        scratch_shapes=[pltpu.SemaphoreType.DMA((7,)), pltpu.SemaphoreType.DMA((7,)), pltpu.SemaphoreType.DMA],
        name=name)(s)


def _adam_math(w, g, m, v):
    nm = ADAM_B1 * m + (1.0 - ADAM_B1) * g
    nv = ADAM_B2 * v + (1.0 - ADAM_B2) * (g * g)
    m_hat = nm / (1.0 - ADAM_B1 ** ADAM_STEP)
    v_hat = nv / (1.0 - ADAM_B2 ** ADAM_STEP)
    return -ADAM_LR * (m_hat / (jnp.sqrt(v_hat) + ADAM_EPS) + ADAM_WD * w), nm, nv


def _adamw(w, g, m, v, name):
    _, r, c = w.shape
    t = max(d for d in range(8, r + 1, 8) if r % d == 0 and 16 * d * c * 4 <= VMEM_LIMIT - (8 << 20))

    def body(w_ref, g_ref, m_ref, v_ref, go_ref, d_ref, nm_ref, nv_ref):
        g_ = g_ref[...]
        d, nm, nv = _adam_math(w_ref[...], g_, m_ref[...], v_ref[...])
        go_ref[...] = g_
        d_ref[...] = d
        nm_ref[...] = nm
        nv_ref[...] = nv

    lead = pl.BlockSpec((None, t, c), lambda i: (0, i, 0))
    return pl.pallas_call(body, grid=(r // t,), in_specs=[lead, _rows(t, c), lead, lead], out_specs=[lead] * 4,
                          out_shape=[SDS((1, r, c), F32)] * 4, compiler_params=_params(("parallel",)),
                          name=name)(w, g, m, v)


def _small_layout():
    out, r0 = {}, 0
    for n in SMALL:
        size = int(np.prod(SMALL_SHAPE[n]))
        nr = -(-size // LANES)
        out[n] = (r0, nr)
        r0 += nr
    assert r0 <= SMALL_ROWS
    return out, r0


def _pack_small(grads, loss_tile, name):
    layout, used = _small_layout()

    def body(*refs):
        o_ref = refs[-1]
        o_ref[used:used + 1, :] = refs[-2][0:1, :]
        for n, ref in zip(SMALL, refs[:-2]):
            r0, nr = layout[n]
            if n == "w_spatial":
                for g in range(GM_GROUPS):
                    o_ref[r0 + g * GM_CHUNK:r0 + (g + 1) * GM_CHUNK, :] = ref[g]
            elif n == "b_spatial":
                o_ref[r0:r0 + nr, :] = ref[...]
            else:
                for i in range(nr):
                    o_ref[r0 + i:r0 + i + 1, :] = ref[:, i * LANES:(i + 1) * LANES]
        if used + 1 < SMALL_ROWS:
            o_ref[used + 1:SMALL_ROWS, :] = jnp.zeros((SMALL_ROWS - used - 1, LANES), F32)

    return pl.pallas_call(body, out_shape=SDS((SMALL_ROWS, LANES), F32), name=name)(*grads, loss_tile)


def _adamw_small(gathered, ws, ms, vs, name):
    layout, used = _small_layout()
    n_t = len(SMALL)

    def body(*refs):
        g_ref = refs[0]
        w_refs, m_refs, v_refs = refs[1:1 + n_t], refs[1 + n_t:1 + 2 * n_t], refs[1 + 2 * n_t:1 + 3 * n_t]
        outs = refs[1 + 3 * n_t:1 + 7 * n_t]
        acc = refs[-1]
        total = g_ref[0]
        for j in range(1, 8):
            total = total + g_ref[j]
        acc[...] = total
        refs[1 + 7 * n_t][...] = acc[used:used + 1, :]
        for t, n in enumerate(SMALL):
            r0, nr = layout[n]
            o_refs = [outs[t], outs[n_t + t], outs[2 * n_t + t], outs[3 * n_t + t]]
            if n == "w_spatial":
                views = [((0, g), slice(r0 + g * GM_CHUNK, r0 + (g + 1) * GM_CHUNK), slice(None))
                         for g in range(GM_GROUPS)]
            elif n == "b_spatial":
                views = [((0,), slice(r0, r0 + nr), slice(None))]
            else:
                width = SMALL_SHAPE[n][1]
                views = [((slice(None), slice(i * LANES, min((i + 1) * LANES, width))), slice(r0 + i, r0 + i + 1),
                          slice(0, min(LANES, width - i * LANES))) for i in range(nr)]
            for idx, rows, lanes in views:
                g = acc[rows, lanes]
                d, nm, nv = _adam_math(w_refs[t][idx], g, m_refs[t][idx], v_refs[t][idx])
                for ref, val in zip(o_refs, (g, d, nm, nv)):
                    ref[idx] = val

    shapes = [SDS(SMALL_SHAPE[n], F32) for n in SMALL]
    return pl.pallas_call(body, out_shape=shapes * 4 + [SDS((1, LANES), F32)],
                          scratch_shapes=[pltpu.VMEM((SMALL_ROWS, LANES), F32)], name=name)(gathered, *ws, *ms, *vs)


def _win_layout(w_in):
    pad = jnp.zeros((w_in.shape[0], LANES - MLA_ROPE), w_in.dtype)
    u, v, cq = w_in[:, 0:512], w_in[:, 512:1024], w_in[:, 1024:1408]
    ckv, kpe, qm, zg = w_in[:, 1408:1664], w_in[:, 1664:1728], w_in[:, 1728:2240], w_in[:, 2240:5312]
    return jnp.concatenate([zg, u, v, qm, cq, kpe, pad, ckv], axis=1)


def _win_unlayout_rows(gt):
    zg, u, v, qm = gt[ZG:ZG + 3072], gt[ZU:ZU + 512], gt[ZV:ZV + 512], gt[QM:QM + 512]
    cq, kpe, ckv = gt[CQ:CQ + 384], gt[KPE:KPE + MLA_ROPE], gt[CKV:CKV + 256]
    return jnp.concatenate([u, v, cq, ckv, kpe, qm, zg], axis=0)


def _wq_layout(w_uq):
    w = w_uq.reshape(Q_LORA, MLA_HEADS, MLA_NOPE + MLA_ROPE)
    nope = w[:, :, :MLA_NOPE].reshape(Q_LORA, MLA_HEADS * MLA_NOPE)
    pe = jnp.pad(w[:, :, MLA_NOPE:], ((0, 0), (0, 0), (0, LANES - MLA_ROPE))).reshape(Q_LORA, MLA_HEADS * LANES)
    return jnp.concatenate([nope, pe], axis=1)


def _wq_unlayout(g):
    nope = g[:, :1024].reshape(Q_LORA, MLA_HEADS, MLA_NOPE)
    pe = g[:, 1024:].reshape(Q_LORA, MLA_HEADS, LANES)[:, :, :MLA_ROPE]
    return jnp.concatenate([nope, pe], axis=2).reshape(Q_LORA, MLA_HEADS * (MLA_NOPE + MLA_ROPE))


def _wkv_layout(w_ukv):
    w = w_ukv.reshape(KV_LORA, MLA_HEADS, MLA_NOPE + MLA_V)
    return jnp.concatenate([w[:, :, :MLA_NOPE].reshape(KV_LORA, 1024), w[:, :, MLA_NOPE:].reshape(KV_LORA, 1024)],
                           axis=1)


def _wkv_unlayout(g):
    kn = g[:, :1024].reshape(KV_LORA, MLA_HEADS, MLA_NOPE)
    v = g[:, 1024:].reshape(KV_LORA, MLA_HEADS, MLA_V)
    return jnp.concatenate([kn, v], axis=2).reshape(KV_LORA, MLA_HEADS * (MLA_NOPE + MLA_V))


def _owner_major(g, name):
    r, c = _shard_shape(name)
    return g.reshape(r, N_CHIPS, c).transpose(1, 0, 2) if name in COL_SHARDED else g.reshape(N_CHIPS, r, c)


def _pad_lanes(g):
    return jnp.pad(g, ((0, 0), (0, LANES - g.shape[1])))


def kernel(x, mem, positions, g_mix, w_in, g_cq, w_uq, g_ckv, w_ukv, g_q_nope, g_q_pe, g_k_nope, g_k_pe, g_gm_ln, b_gm_ln, w_spatial, b_spatial, g_mem, w_mem_kv, g_mq, g_mk, w_o_gm, w_o_mla, w_o_mem, w_out, g_ffn, w_ff1, w_ff2, loss_target, m_g_mix, m_w_in, m_g_cq, m_w_uq, m_g_ckv, m_w_ukv, m_g_q_nope, m_g_q_pe, m_g_k_nope, m_g_k_pe, m_g_gm_ln, m_b_gm_ln, m_w_spatial, m_b_spatial, m_g_mem, m_w_mem_kv, m_g_mq, m_g_mk, m_w_o_gm, m_w_o_mla, m_w_o_mem, m_w_out, m_g_ffn, m_w_ff1, m_w_ff2, v_g_mix, v_w_in, v_g_cq, v_w_uq, v_g_ckv, v_w_ukv, v_g_q_nope, v_g_q_pe, v_g_k_nope, v_g_k_pe, v_g_gm_ln, v_b_gm_ln, v_w_spatial, v_b_spatial, v_g_mem, v_w_mem_kv, v_g_mq, v_g_mk, v_w_o_gm, v_w_o_mla, v_w_o_mem, v_w_out, v_g_ffn, v_w_ff1, v_w_ff2):
    given = dict(locals())
    wts = {n: given[n] for n in WEIGHTS}
    mom = {n: given["m_" + n] for n in WEIGHTS}
    var = {n: given["v_" + n] for n in WEIGHTS}
    batch, seq, _ = x.shape
    n_tok = batch * seq

    def natural(n, g):
        r, c = _shard_shape(n)
        return g.transpose(1, 0, 2).reshape(r, N_CHIPS * c) if n in COL_SHARDED else g.reshape(N_CHIPS * r, c)

    def far(names):
        return _gather_far([wts[n][0].astype(BF) for n in names])

    x2 = x.reshape(n_tok, D_MODEL)
    tgt2 = loss_target.reshape(n_tok, D_MODEL)
    mem2 = mem.reshape(batch * MEM_LEN, D_MODEL)
    pos_f = positions.reshape(n_tok, 1).astype(F32)

    inv = ROPE_BASE ** (-jnp.arange(0, MLA_ROPE, 2, dtype=F32) / MLA_ROPE)
    zeros64 = jnp.zeros((LANES - MLA_ROPE,), F32)
    inv_full = jnp.concatenate([inv, inv, zeros64]).reshape(1, LANES)
    half = MLA_ROPE // 2
    cmask = jnp.concatenate([jnp.ones((MLA_ROPE,), F32), zeros64]).reshape(1, LANES)
    smask = jnp.concatenate([-jnp.ones((half,), F32), jnp.ones((half,), F32), zeros64]).reshape(1, LANES)

    prep_gains = [g_cq, g_ckv, g_q_nope, _pad_lanes(g_q_pe), g_k_nope, _pad_lanes(g_k_pe)]
    ws = w_spatial[0]
    bcols = [b_spatial[0, g].reshape(GM_CHUNK, 1) for g in range(GM_GROUPS)]

    h1, in_far = _rms_fwd(x2, g_mix, "rms_mix", comm=far(EARLY[:1]))
    (cos_f, sin_s), early = _rope_tables(pos_f, inv_full, cmask, smask, "rope_tables",
                                         comm=_together(_gather_near(in_far), far(EARLY[1:])))
    memn, rest = _rms_fwd(mem2, g_mem, "rms_mem", comm=_gather_near(early[1:]))
    full = {n: natural(n, g) for n, g in zip(EARLY, list(early[:1]) + list(rest))}
    win = _win_layout(full["w_in"])
    wq = _wq_layout(full["w_uq"])
    wkv = _wkv_layout(full["w_ukv"])
    z, proj_far = _mm(h1, win, out_dtypes=(BF,), name="mm_in", comm=far(LATE_PROJ))
    gm = _gm_fwd(z, g_gm_ln, b_gm_ln, ws, bcols, "gm_fwd")
    qc, kc, vc = _prep_fwd(z, cos_f, sin_s, prep_gains, wq, wkv, "prep_fwd")
    (o_mla, lse), ff_far = _mla_fwd(qc, kc, vc, batch, seq, "mla_fwd", comm=far(LATE_FF))
    kvm, proj = _mm(memn, full["w_mem_kv"], name="mm_memkv", comm=_gather_near(proj_far))
    o_mem, ff = _mem_fwd(z, kvm, g_mq, g_mk, batch, seq, "mem_fwd", comm=_gather_near(ff_far))
    full.update({n: natural(n, g) for n, g in zip(LATE_PROJ + LATE_FF, list(proj) + list(ff))})
    y_gm = _mm(gm, full["w_o_gm"], out_dtypes=(BF,), name="mm_o_gm")
    y_mla = _mm(o_mla, full["w_o_mla"], out_dtypes=(BF,), name="mm_o_mla")
    y_mem = _mm(o_mem, full["w_o_mem"], out_dtypes=(BF,), name="mm_o_mem")
    merged = _merge_fwd(z, y_gm, y_mla, y_mem, "merge_fwd")
    x1, h2 = _mm(merged, full["w_out"], ins=(x2,), row_ins=(g_ffn,), epilogue=_residual_rms, out_dtypes=(F32, BF),
                 name="mm_out")
    a_ff, r_ff = _mm(h2, full["w_ff1"], epilogue=_relu2, out_dtypes=(BF, BF), name="mm_ff1")
    dy, dyb, loss_tile = _mm(r_ff, full["w_ff2"], ins=(x1, tgt2), epilogue=_loss_tail, out_dtypes=(F32, BF),
                             total=True, name="mm_ff2")

    gw = {}
    da = _mm(dyb, full["w_ff2"], tb=True, ins=(a_ff,), epilogue=_relu2_bwd, out_dtypes=(BF,), name="mm_d_a")
    gw["w_ff2"] = _owner_major(_mm(r_ff, dyb, ta=True, name="mm_dw_ff2"), "w_ff2")
    gw["w_ff1"] = _mm(h2, da, ta=True, owner_cols=D_FF // N_CHIPS, name="mm_dw_ff1")
    dx1, dx1b, dg_ffn = _mm(da, full["w_ff1"], tb=True, ins=(x1, dy), row_ins=(g_ffn,), epilogue=_rms_bwd_tail,
                            out_dtypes=(F32, BF), total=(1, D_MODEL), name="mm_d_h2")
    dmerged = _mm(dx1b, full["w_out"], tb=True, name="mm_d_merged")
    gw["w_out"] = _owner_major(_mm(merged, dx1b, ta=True, name="mm_dw_out"), "w_out")
    dz, dy_gm, dy_mla, dy_mem = _merge_bwd(z, y_gm, y_mla, y_mem, dmerged, "merge_bwd")
    dgm = _mm(dy_gm, full["w_o_gm"], tb=True, name="mm_d_gm")
    gw["w_o_gm"] = _owner_major(_mm(gm, dy_gm, ta=True, name="mm_dw_o_gm"), "w_o_gm")
    do_mla = _mm(dy_mla, full["w_o_mla"], tb=True, name="mm_d_omla")
    gw["w_o_mla"] = _owner_major(_mm(o_mla, dy_mla, ta=True, name="mm_dw_o_mla"), "w_o_mla")
    do_mem = _mm(dy_mem, full["w_o_mem"], tb=True, name="mm_d_omem")
    gw["w_o_mem"] = _owner_major(_mm(o_mem, dy_mem, ta=True, name="mm_dw_o_mem"), "w_o_mem")
    ck = jnp.stack([lax.axis_index("c"), 2 * lax.axis_index("x") + lax.axis_index("y")]).astype(jnp.int32)

    def pair_sums(names, theirs):
        return [_pair_add(ck, gw[n], t, "pair_add_" + n) for n, t in zip(names, theirs)]

    def chip_sums(names, pairs, slots):
        return [_sum_chips(ck, p, s, "sum_chips_" + n) for n, p, s in zip(names, pairs, slots)]

    (dz, dg_ln, db_ln, dws, *dbcols), theirs = _gm_bwd(z, g_gm_ln, b_gm_ln, ws, bcols, dgm, dz, "gm_bwd",
                                                      comm=_pair_exchange([gw[n] for n in LATE]))
    pairs = pair_sums(LATE, theirs)
    (dq, dk, dv), slots = _mla_bwd(qc, kc, vc, o_mla, lse, do_mla, batch, seq, "mla_bwd",
                                   comm=_scatter_partials(pairs))
    sums = chip_sums(LATE, pairs, slots)
    (dz, dg_cq, dg_ckv, dg_qn, dg_qp, dg_kn, dg_kp, dwq, dwkv), reduced_late = _prep_bwd(
        z, cos_f, sin_s, prep_gains, wq, wkv, dq, dk, dv, dz, "prep_bwd", comm=_join_halves(sums))
    dz, dkvm, dg_mq, dg_mk = _mem_bwd(z, kvm, g_mq, g_mk, do_mem, dz, batch, seq, "mem_bwd")
    dmemn = _mm(dkvm, full["w_mem_kv"], tb=True, name="mm_d_memn")
    gw["w_mem_kv"] = _owner_major(_mm(memn, dkvm, ta=True, name="mm_dw_memkv"), "w_mem_kv")
    (dg_mem,) = _rms_bwd(mem2, g_mem, dmemn, None, "rms_mem_bwd", want_dx=False)
    gw["w_in"] = _win_unlayout_rows(_mm(h1, dz, ta=True, name="mm_dw_in").T).reshape(N_CHIPS, W_IN_COLS // N_CHIPS,
                                                                                    D_MODEL)
    gw["w_uq"] = _owner_major(_wq_unlayout(dwq), "w_uq")
    gw["w_ukv"] = _owner_major(_wkv_unlayout(dwkv), "w_ukv")
    dh1, theirs = _mm(dz, win, tb=True, name="mm_d_h1_top", rows=(0, 2), comm=_pair_exchange([gw[n] for n in EARLY]))
    pairs = pair_sums(EARLY, theirs)
    dh1, slots = _mm(dz, win, tb=True, name="mm_d_h1_bottom", rows=(1, 2), into=dh1,
                     comm=_scatter_partials(pairs))
    grad_x, dg_mix = _rms_bwd(x2, g_mix, dh1, dx1, "rms_mix_bwd")
    reduced_early = _run_phase(_join_halves(chip_sums(EARLY, pairs, slots)), "join_early")
    reduced = dict(zip(LATE + EARLY, list(reduced_late) + list(reduced_early)))

    def swapped(a):
        return jnp.swapaxes(a, -1, -2)

    results = {n: _adamw(wts[n], reduced[n], mom[n], var[n], "adamw_" + n) for n in BIG if n != "w_in"}
    results["w_in"] = [swapped(r) for r in _adamw(swapped(w_in), reduced["w_in"], swapped(m_w_in), swapped(v_w_in),
                                                  "adamw_w_in")]

    small_g = {"g_mix": dg_mix, "g_cq": dg_cq, "g_ckv": dg_ckv, "g_q_nope": dg_qn, "g_q_pe": dg_qp,
               "g_k_nope": dg_kn, "g_k_pe": dg_kp, "g_gm_ln": dg_ln, "b_gm_ln": db_ln, "w_spatial": dws,
               "b_spatial": jnp.concatenate(dbcols, axis=1).T, "g_mem": dg_mem, "g_mq": dg_mq, "g_mk": dg_mk,
               "g_ffn": dg_ffn}
    packed = _pack_small([small_g[n] for n in SMALL], loss_tile, "pack_small")
    small_out = _adamw_small(_gather_small(packed, "gather_small"), [wts[n] for n in SMALL],
                             [mom[n] for n in SMALL], [var[n] for n in SMALL], "adamw_small")
    for t, n in enumerate(SMALL):
        results[n] = [small_out[j * len(SMALL) + t] for j in range(4)]

    loss = small_out[4 * len(SMALL)][0, 0]
    grad_x = grad_x.reshape(batch, seq, D_MODEL)
    return (loss, grad_x, *[results[n][0] for n in WEIGHTS], *[results[n][1] for n in WEIGHTS],
            *[results[n][2] for n in WEIGHTS], *[results[n][3] for n in WEIGHTS])
```
